```python
import math
import jax, jax.numpy as jnp
from jax import lax
import numpy as np


D_MODEL = 1024
BATCH = 8
SEQ = 8192
DEPTH = 2

HEAD_DIM = 128
BLOCK = 128
GRID_W = 64
EPS = 1e-6
NEG_INF = -1e30
RET_HEADS = D_MODEL // 256
RET_DK = 128
RET_DV = 256
RET_CHUNK = 128
RET_THETA = 10000.0
SWA_HEADS = D_MODEL // HEAD_DIM
SWA_KV_HEADS = 2
WINDOW = 128
T5_BUCKETS = 32
T5_MAX_DIST = 128
AX_HEADS = D_MODEL // HEAD_DIM
AX_KV_HEADS = 2
AX_THETA = 10000.0
D_FF = 4 * D_MODEL

N_EVEN = (DEPTH + 1) // 2
N_ODD = DEPTH // 2
RET_Q = RET_HEADS * RET_DK
RET_V = RET_HEADS * RET_DV
SWA_Q = SWA_HEADS * HEAD_DIM
SWA_KV = SWA_KV_HEADS * HEAD_DIM
EVEN_IN = 2 * RET_Q + 2 * RET_V + SWA_Q + 2 * SWA_KV
EVEN_OUT = RET_V + SWA_Q
AX_Q = AX_HEADS * HEAD_DIM
AX_KV = AX_KV_HEADS * HEAD_DIM
ODD_IN = AX_Q + 2 * AX_KV

kernel_name = "hybrid_retention_swa_axialrope_encoder"


def rms_norm(x, g):
    xf = x.astype(jnp.float32)
    y = xf * lax.rsqrt(jnp.mean(xf * xf, axis=-1, keepdims=True) + EPS)
    return (y * g.astype(jnp.float32)).astype(x.dtype)


def split_cols(a, sizes):
    offs, o = [], 0
    for s in sizes[:-1]:
        o += s
        offs.append(o)
    return jnp.split(a, offs, axis=-1)


def rope_angles(pos, dim, theta):
    inv = theta ** (-jnp.arange(0, dim, 2, dtype=jnp.float32) / dim)
    return pos.astype(jnp.float32)[:, None] * inv[None, :]


def apply_rope(x, ang):
    d2 = x.shape[-1] // 2
    xf = x.astype(jnp.float32)
    x1, x2 = xf[..., :d2], xf[..., d2:]
    c = jnp.cos(ang)[None, :, None, :]
    s = jnp.sin(ang)[None, :, None, :]
    return jnp.concatenate([x1 * c - x2 * s, x2 * c + x1 * s], axis=-1).astype(x.dtype)


def retention_direction(q, k, v, log_gamma, strict):
    Bn, S, H, dk = q.shape
    dv = v.shape[-1]
    C = RET_CHUNK
    nc = S // C
    qc = q.reshape(Bn, nc, C, H, dk)
    kc = k.reshape(Bn, nc, C, H, dk)
    vc = v.reshape(Bn, nc, C, H, dv)
    idx = jnp.arange(C, dtype=jnp.float32)
    diff = idx[:, None] - idx[None, :]
    mask = (diff > 0) if strict else (diff >= 0)
    decay = jnp.where(mask[None], jnp.exp(jnp.maximum(diff, 0.0)[None] * log_gamma[:, None, None]), 0.0)
    scores = jnp.einsum('bnrhd,bnjhd->bnhrj', qc, kc) * decay[None, None]
    inner = jnp.einsum('bnhrj,bnjhe->bnrhe', scores, vc)
    k_w = kc * jnp.exp((C - 1 - idx)[:, None] * log_gamma[None, :])[None, None, :, :, None]
    U = jnp.einsum('bnjhd,bnjhe->nbhde', k_w, vc)
    chunk_decay = jnp.exp(C * log_gamma)[None, :, None, None]

    def step(state, u):
        return chunk_decay * state + u, state

    _, prev = lax.scan(step, jnp.zeros((Bn, H, dk, dv), jnp.float32), U)
    q_w = qc * jnp.exp((idx + 1)[:, None] * log_gamma[None, :])[None, None, :, :, None]
    cross = jnp.einsum('bnrhd,nbhde->bnrhe', q_w, prev)
    return (inner + cross).reshape(Bn, S, H, dv)


def retention_mixer(q, k, v, g, decay_logit, gn_gain):
    Bn, S, _ = q.shape
    dt = v.dtype
    q = q.reshape(Bn, S, RET_HEADS, RET_DK)
    k = k.reshape(Bn, S, RET_HEADS, RET_DK)
    v = v.reshape(Bn, S, RET_HEADS, RET_DV).astype(jnp.float32)
    ang = rope_angles(jnp.arange(S), RET_DK, RET_THETA)
    q = apply_rope(q, ang).astype(jnp.float32)
    k = apply_rope(k, ang).astype(jnp.float32) * (RET_DK ** -0.5)
    log_gamma = jax.nn.log_sigmoid(decay_logit.astype(jnp.float32))
    fwd = retention_direction(q, k, v, log_gamma[0], False)
    bwd = jnp.flip(retention_direction(jnp.flip(q, 1), jnp.flip(k, 1), jnp.flip(v, 1),
                                       log_gamma[1], True), 1)
    y = rms_norm(fwd + bwd, gn_gain.reshape(RET_HEADS, RET_DV))
    y = y.reshape(Bn, S, RET_V).astype(dt)
    return jax.nn.silu(g) * y


def t5_bucket(rel):
    nb = T5_BUCKETS // 2
    max_exact = nb // 2
    ret = jnp.where(rel > 0, nb, 0)
    n = jnp.abs(rel)
    nf = jnp.maximum(n, 1).astype(jnp.float32)
    large = max_exact + (jnp.log(nf / max_exact) / math.log(T5_MAX_DIST / max_exact)
                         * (nb - max_exact)).astype(jnp.int32)
    large = jnp.minimum(large, nb - 1)
    return ret + jnp.where(n < max_exact, n, large)


def window_attention(q, k, v, sink, t5_table):
    Bn, S, Hq, D = q.shape
    Hkv = k.shape[2]
    G = Hq // Hkv
    nb = S // BLOCK
    pad = ((0, 0), (BLOCK, BLOCK), (0, 0), (0, 0))
    kp = jnp.pad(k, pad).reshape(Bn, nb + 2, BLOCK, Hkv, D)
    vp = jnp.pad(v, pad).reshape(Bn, nb + 2, BLOCK, Hkv, D)
    kw = jnp.concatenate([kp[:, :nb], kp[:, 1:nb + 1], kp[:, 2:]], axis=2)
    vw = jnp.concatenate([vp[:, :nb], vp[:, 1:nb + 1], vp[:, 2:]], axis=2)
    qb = q.reshape(Bn, nb, BLOCK, Hkv, G, D)
    s = jnp.einsum('bnqkgd,bnjkd->bnkgqj', qb, kw,
                   preferred_element_type=jnp.float32) * (D ** -0.5)
    r = jnp.arange(BLOCK)
    j = jnp.arange(3 * BLOCK)
    rel = j[None, :] - BLOCK - r[:, None]
    bias = t5_table.astype(jnp.float32)[t5_bucket(rel)]
    bias = bias.transpose(2, 0, 1).reshape(Hkv, G, BLOCK, 3 * BLOCK)
    kpos = (jnp.arange(nb)[:, None] - 1) * BLOCK + j[None, :]
    valid = (jnp.abs(rel) <= WINDOW)[None] & ((kpos >= 0) & (kpos < S))[:, None, :]
    s = jnp.where(valid[None, :, None, None], s + bias[None, None], NEG_INF)
    sink_l = sink.astype(jnp.float32).reshape(Hkv, G)[None, None, :, :, None, None]
    m = jnp.maximum(jnp.max(s, axis=-1, keepdims=True), sink_l)
    p = jnp.exp(s - m)
    p = p / (jnp.sum(p, axis=-1, keepdims=True) + jnp.exp(sink_l - m))
    o = jnp.einsum('bnkgqj,bnjkd->bnqkgd', p.astype(v.dtype), vw)
    return o.reshape(Bn, S, Hq * D)


def axial_attention(q, k, v):
    Bn, S, Hq, D = q.shape
    Hkv = k.shape[2]
    G = Hq // Hkv
    rows = S // GRID_W
    row = jnp.repeat(jnp.arange(rows), GRID_W)
    col = jnp.tile(jnp.arange(GRID_W), rows)
    half = D // 2
    ang_r = rope_angles(row, half, AX_THETA)
    ang_c = rope_angles(col, half, AX_THETA)
    q = jnp.concatenate([apply_rope(q[..., :half], ang_r), apply_rope(q[..., half:], ang_c)], axis=-1)
    k = jnp.concatenate([apply_rope(k[..., :half], ang_r), apply_rope(k[..., half:], ang_c)], axis=-1)
    nb = S // BLOCK
    qb = q.reshape(Bn, nb, BLOCK, Hkv, G, D).transpose(1, 0, 2, 3, 4, 5)
    scale = D ** -0.5

    def block(qi):
        s = jnp.einsum('bqkgd,bjkd->bkgqj', qi, k, preferred_element_type=jnp.float32) * scale
        p = jax.nn.softmax(s, axis=-1)
        return jnp.einsum('bkgqj,bjkd->bqkgd', p.astype(v.dtype), v)

    o = lax.map(block, qb)
    return o.transpose(1, 0, 2, 3, 4, 5).reshape(Bn, S, Hq * D)


def _fwd_setup_inputs(seed: int = 0) -> dict:
    key = jax.random.key(seed)
    ks = jax.random.split(key, 20)
    f32 = jnp.float32

    def w(k, shape, fan_in):
        return jax.random.normal(k, shape, f32) * (fan_in ** -0.5)

    def gain(k, shape):
        return 1.0 + 0.05 * jax.random.normal(k, shape, f32)

    base_logit = jnp.log(2.0 ** (5.0 + jnp.arange(RET_HEADS, dtype=f32)) - 1.0)
    return {
        'x': jax.random.normal(ks[0], (BATCH, SEQ, D_MODEL), f32),
        'norm_mix': gain(ks[1], (DEPTH, D_MODEL)),
        'norm_mlp': gain(ks[2], (DEPTH, D_MODEL)),
        'w_in_even': w(ks[3], (N_EVEN, D_MODEL, EVEN_IN), D_MODEL),
        'w_out_even': w(ks[4], (N_EVEN, EVEN_OUT, D_MODEL), EVEN_OUT),
        'ret_decay_logit': base_logit[None, None, :] + 0.1 * jax.random.normal(ks[5], (N_EVEN, 2, RET_HEADS), f32),
        'ret_norm': gain(ks[6], (N_EVEN, RET_V)),
        'swa_q_norm': gain(ks[7], (N_EVEN, HEAD_DIM)),
        'swa_k_norm': gain(ks[8], (N_EVEN, HEAD_DIM)),
        'swa_sink': 0.5 * jax.random.normal(ks[9], (N_EVEN, SWA_HEADS), f32),
        't5_table': 0.5 * jax.random.normal(ks[10], (T5_BUCKETS, SWA_HEADS), f32),
        'w_in_odd': w(ks[11], (N_ODD, D_MODEL, ODD_IN), D_MODEL),
        'w_out_odd': w(ks[12], (N_ODD, AX_Q, D_MODEL), AX_Q),
        'ax_q_norm': gain(ks[13], (N_ODD, HEAD_DIM)),
        'ax_k_norm': gain(ks[14], (N_ODD, HEAD_DIM)),
        'w_mlp_up': w(ks[15], (DEPTH, D_MODEL, D_FF), D_MODEL),
        'w_mlp_down': w(ks[16], (DEPTH, D_FF, D_MODEL), D_FF),
    }


def _fwd_reference(x, norm_mix, norm_mlp, w_in_even, w_out_even, ret_decay_logit, ret_norm,
              swa_q_norm, swa_k_norm, swa_sink, t5_table, w_in_odd, w_out_odd,
              ax_q_norm, ax_k_norm, w_mlp_up, w_mlp_down):
    Bn, S, _ = x.shape
    for layer in range(DEPTH):
        h = rms_norm(x, norm_mix[layer])
        if layer % 2 == 0:
            i = layer // 2
            proj = h @ w_in_even[i]
            qa, ka, va, ga, qb, kb, vb = split_cols(
                proj, [RET_Q, RET_Q, RET_V, RET_V, SWA_Q, SWA_KV, SWA_KV])
            ya = retention_mixer(qa, ka, va, ga, ret_decay_logit[i], ret_norm[i])
            qb = rms_norm(qb.reshape(Bn, S, SWA_HEADS, HEAD_DIM), swa_q_norm[i])
            kb = rms_norm(kb.reshape(Bn, S, SWA_KV_HEADS, HEAD_DIM), swa_k_norm[i])
            vb = vb.reshape(Bn, S, SWA_KV_HEADS, HEAD_DIM)
            yb = window_attention(qb, kb, vb, swa_sink[i], t5_table)
            y = jnp.concatenate([ya, yb], axis=-1) @ w_out_even[i]
        else:
            i = layer // 2
            proj = h @ w_in_odd[i]
            qc, kc, vc = split_cols(proj, [AX_Q, AX_KV, AX_KV])
            qc = rms_norm(qc.reshape(Bn, S, AX_HEADS, HEAD_DIM), ax_q_norm[i])
            kc = rms_norm(kc.reshape(Bn, S, AX_KV_HEADS, HEAD_DIM), ax_k_norm[i])
            vc = vc.reshape(Bn, S, AX_KV_HEADS, HEAD_DIM)
            y = axial_attention(qc, kc, vc) @ w_out_odd[i]
        x = x + y
        h = rms_norm(x, norm_mlp[layer])
        x = x + jnp.square(jax.nn.relu(h @ w_mlp_up[layer])) @ w_mlp_down[layer]
    return x


import jax as _jax
import jax.numpy as _jnp

TWIN_FORMAT = 'train_step'
FWD_PARAMS = ['x', 'norm_mix', 'norm_mlp', 'w_in_even', 'w_out_even', 'ret_decay_logit', 'ret_norm', 'swa_q_norm', 'swa_k_norm', 'swa_sink', 't5_table', 'w_in_odd', 'w_out_odd', 'ax_q_norm', 'ax_k_norm', 'w_mlp_up', 'w_mlp_down']
TWIN_WEIGHTS = ['norm_mix', 'norm_mlp', 'w_in_even', 'w_out_even', 'ret_decay_logit', 'ret_norm', 'swa_q_norm', 'swa_k_norm', 'swa_sink', 't5_table', 'w_in_odd', 'w_out_odd', 'ax_q_norm', 'ax_k_norm', 'w_mlp_up', 'w_mlp_down']
TWIN_DIFF_INPUT = 'x'
TWIN_INPUTS = ['x', 'norm_mix', 'norm_mlp', 'w_in_even', 'w_out_even', 'ret_decay_logit', 'ret_norm', 'swa_q_norm', 'swa_k_norm', 'swa_sink', 't5_table', 'w_in_odd', 'w_out_odd', 'ax_q_norm', 'ax_k_norm', 'w_mlp_up', 'w_mlp_down', 'loss_target', 'm_norm_mix', 'm_norm_mlp', 'm_w_in_even', 'm_w_out_even', 'm_ret_decay_logit', 'm_ret_norm', 'm_swa_q_norm', 'm_swa_k_norm', 'm_swa_sink', 'm_t5_table', 'm_w_in_odd', 'm_w_out_odd', 'm_ax_q_norm', 'm_ax_k_norm', 'm_w_mlp_up', 'm_w_mlp_down', 'v_norm_mix', 'v_norm_mlp', 'v_w_in_even', 'v_w_out_even', 'v_ret_decay_logit', 'v_ret_norm', 'v_swa_q_norm', 'v_swa_k_norm', 'v_swa_sink', 'v_t5_table', 'v_w_in_odd', 'v_w_out_odd', 'v_ax_q_norm', 'v_ax_k_norm', 'v_w_mlp_up', 'v_w_mlp_down']
TWIN_OUTPUTS = ['loss', 'grad_x', 'grad_norm_mix', 'grad_norm_mlp', 'grad_w_in_even', 'grad_w_out_even', 'grad_ret_decay_logit', 'grad_ret_norm', 'grad_swa_q_norm', 'grad_swa_k_norm', 'grad_swa_sink', 'grad_t5_table', 'grad_w_in_odd', 'grad_w_out_odd', 'grad_ax_q_norm', 'grad_ax_k_norm', 'grad_w_mlp_up', 'grad_w_mlp_down', 'delta_norm_mix', 'delta_norm_mlp', 'delta_w_in_even', 'delta_w_out_even', 'delta_ret_decay_logit', 'delta_ret_norm', 'delta_swa_q_norm', 'delta_swa_k_norm', 'delta_swa_sink', 'delta_t5_table', 'delta_w_in_odd', 'delta_w_out_odd', 'delta_ax_q_norm', 'delta_ax_k_norm', 'delta_w_mlp_up', 'delta_w_mlp_down', 'new_m_norm_mix', 'new_m_norm_mlp', 'new_m_w_in_even', 'new_m_w_out_even', 'new_m_ret_decay_logit', 'new_m_ret_norm', 'new_m_swa_q_norm', 'new_m_swa_k_norm', 'new_m_swa_sink', 'new_m_t5_table', 'new_m_w_in_odd', 'new_m_w_out_odd', 'new_m_ax_q_norm', 'new_m_ax_k_norm', 'new_m_w_mlp_up', 'new_m_w_mlp_down', 'new_v_norm_mix', 'new_v_norm_mlp', 'new_v_w_in_even', 'new_v_w_out_even', 'new_v_ret_decay_logit', 'new_v_ret_norm', 'new_v_swa_q_norm', 'new_v_swa_k_norm', 'new_v_swa_sink', 'new_v_t5_table', 'new_v_w_in_odd', 'new_v_w_out_odd', 'new_v_ax_q_norm', 'new_v_ax_k_norm', 'new_v_w_mlp_up', 'new_v_w_mlp_down']
TWIN_LEAF_KINDS = {'loss': 'loss', 'grad_x': 'grad_x', 'grad_norm_mix': 'grad_w', 'grad_norm_mlp': 'grad_w', 'grad_w_in_even': 'grad_w', 'grad_w_out_even': 'grad_w', 'grad_ret_decay_logit': 'grad_w', 'grad_ret_norm': 'grad_w', 'grad_swa_q_norm': 'grad_w', 'grad_swa_k_norm': 'grad_w', 'grad_swa_sink': 'grad_w', 'grad_t5_table': 'grad_w', 'grad_w_in_odd': 'grad_w', 'grad_w_out_odd': 'grad_w', 'grad_ax_q_norm': 'grad_w', 'grad_ax_k_norm': 'grad_w', 'grad_w_mlp_up': 'grad_w', 'grad_w_mlp_down': 'grad_w', 'delta_norm_mix': 'delta_w', 'delta_norm_mlp': 'delta_w', 'delta_w_in_even': 'delta_w', 'delta_w_out_even': 'delta_w', 'delta_ret_decay_logit': 'delta_w', 'delta_ret_norm': 'delta_w', 'delta_swa_q_norm': 'delta_w', 'delta_swa_k_norm': 'delta_w', 'delta_swa_sink': 'delta_w', 'delta_t5_table': 'delta_w', 'delta_w_in_odd': 'delta_w', 'delta_w_out_odd': 'delta_w', 'delta_ax_q_norm': 'delta_w', 'delta_ax_k_norm': 'delta_w', 'delta_w_mlp_up': 'delta_w', 'delta_w_mlp_down': 'delta_w', 'new_m_norm_mix': 'new_m', 'new_m_norm_mlp': 'new_m', 'new_m_w_in_even': 'new_m', 'new_m_w_out_even': 'new_m', 'new_m_ret_decay_logit': 'new_m', 'new_m_ret_norm': 'new_m', 'new_m_swa_q_norm': 'new_m', 'new_m_swa_k_norm': 'new_m', 'new_m_swa_sink': 'new_m', 'new_m_t5_table': 'new_m', 'new_m_w_in_odd': 'new_m', 'new_m_w_out_odd': 'new_m', 'new_m_ax_q_norm': 'new_m', 'new_m_ax_k_norm': 'new_m', 'new_m_w_mlp_up': 'new_m', 'new_m_w_mlp_down': 'new_m', 'new_v_norm_mix': 'new_v', 'new_v_norm_mlp': 'new_v', 'new_v_w_in_even': 'new_v', 'new_v_w_out_even': 'new_v', 'new_v_ret_decay_logit': 'new_v', 'new_v_ret_norm': 'new_v', 'new_v_swa_q_norm': 'new_v', 'new_v_swa_k_norm': 'new_v', 'new_v_swa_sink': 'new_v', 'new_v_t5_table': 'new_v', 'new_v_w_in_odd': 'new_v', 'new_v_w_out_odd': 'new_v', 'new_v_ax_q_norm': 'new_v', 'new_v_ax_k_norm': 'new_v', 'new_v_w_mlp_up': 'new_v', 'new_v_w_mlp_down': 'new_v'}


def _forward(args):
    return _fwd_reference(*[args[k] for k in FWD_PARAMS])


def _output_shape():
    def fwd():
        inp = _fwd_setup_inputs(0)
        return _fwd_reference(*[inp[k] for k in FWD_PARAMS])
    out = _jax.eval_shape(fwd)
    return out.shape, out.dtype

N_MICROBATCH = 1
ADAM_LR = 0.001
ADAM_B1 = 0.9
ADAM_B2 = 0.999
ADAM_EPS = 1e-08
ADAM_WD = 0.01
ADAM_STEP = 10
PER_EXAMPLE_BATCH_AXIS = {'x': 0, 'loss_target': 0}
SHARED_INPUTS = []
_WEIGHT_DTYPES = {'norm_mix': _jnp.float32, 'norm_mlp': _jnp.float32, 'w_in_even': _jnp.float32, 'w_out_even': _jnp.float32, 'ret_decay_logit': _jnp.float32, 'ret_norm': _jnp.float32, 'swa_q_norm': _jnp.float32, 'swa_k_norm': _jnp.float32, 'swa_sink': _jnp.float32, 't5_table': _jnp.float32, 'w_in_odd': _jnp.float32, 'w_out_odd': _jnp.float32, 'ax_q_norm': _jnp.float32, 'ax_k_norm': _jnp.float32, 'w_mlp_up': _jnp.float32, 'w_mlp_down': _jnp.float32}
MOMENT_SCALE = {'norm_mix': 2.241708e+01, 'norm_mlp': 1.984682e+02, 'w_in_even': 7.550066e-01, 'w_out_even': 1.044265e+00, 'ret_decay_logit': 3.577585e+00, 'ret_norm': 1.067382e+01, 'swa_q_norm': 1.772031e+00, 'swa_k_norm': 1.778192e+00, 'swa_sink': 1.304561e-01, 't5_table': 5.090729e-01, 'w_in_odd': 2.166080e+01, 'w_out_odd': 2.317115e+01, 'ax_q_norm': 9.236727e-01, 'ax_k_norm': 9.242546e-01, 'w_mlp_up': 9.962425e+00, 'w_mlp_down': 4.041268e+01}


def _to_microbatches(a, axis):
    t = _jnp.moveaxis(a, axis, 0)
    t = t.reshape((N_MICROBATCH, t.shape[0] // N_MICROBATCH) + t.shape[1:])
    return _jnp.moveaxis(t, 1, axis + 1)


def setup_inputs(seed: int = 0) -> dict:
    inp = _fwd_setup_inputs(seed)
    key = _jax.random.fold_in(_jax.random.key(seed), 7919)
    shape, _ = _output_shape()
    out = dict(inp)
    out["loss_target"] = _jax.random.normal(_jax.random.fold_in(key, 0), shape, _jnp.float32)
    for i, name in enumerate(TWIN_WEIGHTS):
        w = inp[name].astype(_jnp.float32)
        if MOMENT_SCALE is None:
            s = _jnp.sqrt(_jnp.mean(_jnp.square(w)) + 1e-30)
        else:
            s = MOMENT_SCALE[name]
        km, kv = _jax.random.split(_jax.random.fold_in(key, i + 1))
        out[name] = w
        out["m_" + name] = s * _jax.random.normal(km, w.shape, _jnp.float32)
        out["v_" + name] = (s * s) * _jax.random.uniform(kv, w.shape, _jnp.float32, 0.5, 1.5)
    if N_MICROBATCH > 1:
        for name, axis in PER_EXAMPLE_BATCH_AXIS.items():
            out[name] = _to_microbatches(out[name], axis)
    return {'x': out['x'], 'norm_mix': out['norm_mix'], 'norm_mlp': out['norm_mlp'], 'w_in_even': out['w_in_even'], 'w_out_even': out['w_out_even'], 'ret_decay_logit': out['ret_decay_logit'], 'ret_norm': out['ret_norm'], 'swa_q_norm': out['swa_q_norm'], 'swa_k_norm': out['swa_k_norm'], 'swa_sink': out['swa_sink'], 't5_table': out['t5_table'], 'w_in_odd': out['w_in_odd'], 'w_out_odd': out['w_out_odd'], 'ax_q_norm': out['ax_q_norm'], 'ax_k_norm': out['ax_k_norm'], 'w_mlp_up': out['w_mlp_up'], 'w_mlp_down': out['w_mlp_down'], 'loss_target': out['loss_target'], 'm_norm_mix': out['m_norm_mix'], 'm_norm_mlp': out['m_norm_mlp'], 'm_w_in_even': out['m_w_in_even'], 'm_w_out_even': out['m_w_out_even'], 'm_ret_decay_logit': out['m_ret_decay_logit'], 'm_ret_norm': out['m_ret_norm'], 'm_swa_q_norm': out['m_swa_q_norm'], 'm_swa_k_norm': out['m_swa_k_norm'], 'm_swa_sink': out['m_swa_sink'], 'm_t5_table': out['m_t5_table'], 'm_w_in_odd': out['m_w_in_odd'], 'm_w_out_odd': out['m_w_out_odd'], 'm_ax_q_norm': out['m_ax_q_norm'], 'm_ax_k_norm': out['m_ax_k_norm'], 'm_w_mlp_up': out['m_w_mlp_up'], 'm_w_mlp_down': out['m_w_mlp_down'], 'v_norm_mix': out['v_norm_mix'], 'v_norm_mlp': out['v_norm_mlp'], 'v_w_in_even': out['v_w_in_even'], 'v_w_out_even': out['v_w_out_even'], 'v_ret_decay_logit': out['v_ret_decay_logit'], 'v_ret_norm': out['v_ret_norm'], 'v_swa_q_norm': out['v_swa_q_norm'], 'v_swa_k_norm': out['v_swa_k_norm'], 'v_swa_sink': out['v_swa_sink'], 'v_t5_table': out['v_t5_table'], 'v_w_in_odd': out['v_w_in_odd'], 'v_w_out_odd': out['v_w_out_odd'], 'v_ax_q_norm': out['v_ax_q_norm'], 'v_ax_k_norm': out['v_ax_k_norm'], 'v_w_mlp_up': out['v_w_mlp_up'], 'v_w_mlp_down': out['v_w_mlp_down']}


def _loss(weights, diff, rest, loss_target):
    with _jax.named_scope("forward"):
        args = {**rest, TWIN_DIFF_INPUT: diff, **{k: w.astype(_WEIGHT_DTYPES[k]) for k, w in weights.items()}}
        y = _forward(args)
    with _jax.named_scope("loss_head"):
        err = _jnp.square(y.astype(_jnp.float32) - loss_target)
        return 0.5 * _jnp.sum(_jnp.mean(err, axis=-1)) if err.ndim else 0.5 * err


def _adamw(w, g, m, v):
    m = ADAM_B1 * m + (1.0 - ADAM_B1) * g
    v = ADAM_B2 * v + (1.0 - ADAM_B2) * _jnp.square(g)
    m_hat = m / (1.0 - ADAM_B1 ** ADAM_STEP)
    v_hat = v / (1.0 - ADAM_B2 ** ADAM_STEP)
    delta = -ADAM_LR * (m_hat / (_jnp.sqrt(v_hat) + ADAM_EPS) + ADAM_WD * w)
    return delta, m, v


def reference(x, norm_mix, norm_mlp, w_in_even, w_out_even, ret_decay_logit, ret_norm, swa_q_norm, swa_k_norm, swa_sink, t5_table, w_in_odd, w_out_odd, ax_q_norm, ax_k_norm, w_mlp_up, w_mlp_down, loss_target, m_norm_mix, m_norm_mlp, m_w_in_even, m_w_out_even, m_ret_decay_logit, m_ret_norm, m_swa_q_norm, m_swa_k_norm, m_swa_sink, m_t5_table, m_w_in_odd, m_w_out_odd, m_ax_q_norm, m_ax_k_norm, m_w_mlp_up, m_w_mlp_down, v_norm_mix, v_norm_mlp, v_w_in_even, v_w_out_even, v_ret_decay_logit, v_ret_norm, v_swa_q_norm, v_swa_k_norm, v_swa_sink, v_t5_table, v_w_in_odd, v_w_out_odd, v_ax_q_norm, v_ax_k_norm, v_w_mlp_up, v_w_mlp_down):
    given = dict(x=x, norm_mix=norm_mix, norm_mlp=norm_mlp, w_in_even=w_in_even, w_out_even=w_out_even, ret_decay_logit=ret_decay_logit, ret_norm=ret_norm, swa_q_norm=swa_q_norm, swa_k_norm=swa_k_norm, swa_sink=swa_sink, t5_table=t5_table, w_in_odd=w_in_odd, w_out_odd=w_out_odd, ax_q_norm=ax_q_norm, ax_k_norm=ax_k_norm, w_mlp_up=w_mlp_up, w_mlp_down=w_mlp_down, loss_target=loss_target, m_norm_mix=m_norm_mix, m_norm_mlp=m_norm_mlp, m_w_in_even=m_w_in_even, m_w_out_even=m_w_out_even, m_ret_decay_logit=m_ret_decay_logit, m_ret_norm=m_ret_norm, m_swa_q_norm=m_swa_q_norm, m_swa_k_norm=m_swa_k_norm, m_swa_sink=m_swa_sink, m_t5_table=m_t5_table, m_w_in_odd=m_w_in_odd, m_w_out_odd=m_w_out_odd, m_ax_q_norm=m_ax_q_norm, m_ax_k_norm=m_ax_k_norm, m_w_mlp_up=m_w_mlp_up, m_w_mlp_down=m_w_mlp_down, v_norm_mix=v_norm_mix, v_norm_mlp=v_norm_mlp, v_w_in_even=v_w_in_even, v_w_out_even=v_w_out_even, v_ret_decay_logit=v_ret_decay_logit, v_ret_norm=v_ret_norm, v_swa_q_norm=v_swa_q_norm, v_swa_k_norm=v_swa_k_norm, v_swa_sink=v_swa_sink, v_t5_table=v_t5_table, v_w_in_odd=v_w_in_odd, v_w_out_odd=v_w_out_odd, v_ax_q_norm=v_ax_q_norm, v_ax_k_norm=v_ax_k_norm, v_w_mlp_up=v_w_mlp_up, v_w_mlp_down=v_w_mlp_down)
    weights = {n: given[n] for n in TWIN_WEIGHTS}
    shared = {n: given[n] for n in SHARED_INPUTS}
    per_example = {n: given[n] for n in ['x']}
    grad_fn = _jax.value_and_grad(_loss, argnums=(0, 1))

    def one_microbatch(ex, loss_target):
        ex = dict(ex)
        diff = ex.pop(TWIN_DIFF_INPUT)
        return grad_fn(weights, diff, {**shared, **ex}, loss_target)

    if N_MICROBATCH == 1:
        loss, (grad_w, grad_x) = one_microbatch(per_example, given["loss_target"])
    else:
        def body(carry, xs):
            loss_sum, grad_sum = carry
            l_k, (gw_k, gx_k) = one_microbatch(xs[0], xs[1])
            with _jax.named_scope("update"):
                return (loss_sum + l_k, _jax.tree.map(_jnp.add, grad_sum, gw_k)), gx_k

        init = (_jnp.zeros((), _jnp.float32), _jax.tree.map(_jnp.zeros_like, weights))
        (loss, grad_w), grad_x = _jax.lax.scan(body, init, (per_example, given["loss_target"]))
    with _jax.named_scope("update"):
        delta_w, new_m, new_v = {}, {}, {}
        for n in TWIN_WEIGHTS:
            delta_w[n], new_m[n], new_v[n] = _adamw(weights[n], grad_w[n], given["m_" + n], given["v_" + n])
    return (loss, grad_x, *[grad_w[n] for n in TWIN_WEIGHTS], *[delta_w[n] for n in TWIN_WEIGHTS],
            *[new_m[n] for n in TWIN_WEIGHTS], *[new_v[n] for n in TWIN_WEIGHTS])
```

```python
import functools
import math

import jax
import jax.numpy as jnp
from jax import lax
from jax.experimental import pallas as pl
from jax.experimental.pallas import tpu as pltpu

F32 = jnp.float32
BF16 = jnp.bfloat16
_MXU = BF16

D_MODEL = 1024
HEAD_DIM = 128
EPS = 1e-6
NEG_INF = -1e30
CHUNK = 128
GRID_W = 64
RET_HEADS, RET_DK, RET_DV = 4, 128, 256
RET_Q, RET_V = RET_HEADS * RET_DK, RET_HEADS * RET_DV
RET_THETA = 10000.0
SWA_HEADS, SWA_KV_HEADS = 8, 2
T5_BUCKETS, T5_MAX_DIST = 32, 128
AX_HEADS, AX_KV_HEADS = 8, 2
AX_THETA = 10000.0
D_FF = 4 * D_MODEL
EVEN_IN = 2 * RET_Q + 2 * RET_V + D_MODEL + 2 * SWA_KV_HEADS * HEAD_DIM
ODD_IN = D_MODEL + 2 * AX_KV_HEADS * HEAD_DIM
ATT_SCALE = HEAD_DIM ** -0.5

ADAM_LR, ADAM_B1, ADAM_B2, ADAM_EPS, ADAM_WD, ADAM_STEP = 0.001, 0.9, 0.999, 1e-08, 0.01, 10

N_DEV = 8
VMEM_LIMIT_BYTES = 56 << 20
MESH = pl.DeviceIdType.MESH

_NN = (((1,), (0,)), ((), ()))
_NT = (((1,), (1,)), ((), ()))
_TN = (((0,), (0,)), ((), ()))


def _dot(a, b, dn=_NN):
    return lax.dot_general(a.astype(_MXU), b.astype(_MXU), dn, preferred_element_type=F32)


def _params(*sem):
    return pltpu.CompilerParams(dimension_semantics=sem, vmem_limit_bytes=VMEM_LIMIT_BYTES)


def _sds(shape, dtype):
    return jax.ShapeDtypeStruct(tuple(shape), dtype)


def _rowsum8(x):
    return jnp.sum(x.reshape(x.shape[0] // 8, 8, x.shape[1]), axis=0)


def _swap_halves(x, half):
    width = x.shape[1]
    lane = lax.broadcasted_iota(jnp.int32, x.shape, 1)
    up = pltpu.roll(x, width - half, axis=1)
    down = pltpu.roll(x, half, axis=1)
    return jnp.where((lane & (2 * half - 1)) < half, up, down)


def _sigmoid(x):
    return 1.0 / (1.0 + jnp.exp(-x))


def _norm_matmul(name, x, gain, w, *, tm, tn, out_dtype):
    T, K = x.shape
    N = w.shape[1]
    tm, tn = min(tm, T), min(tn, N)

    def body(x_ref, g_ref, w_ref, y_ref, h_ref, h_sc):
        @pl.when(pl.program_id(1) == 0)
        def _():
            xv = x_ref[...]
            r = lax.rsqrt(jnp.mean(xv * xv, axis=-1, keepdims=True) + EPS)
            h = (xv * r * g_ref[...]).astype(_MXU)
            h_sc[...] = h
            h_ref[...] = h
        y_ref[...] = jnp.dot(h_sc[...], w_ref[...], preferred_element_type=F32).astype(y_ref.dtype)

    return pl.pallas_call(
        body, name=name, grid=(T // tm, N // tn),
        in_specs=[pl.BlockSpec((tm, K), lambda i, j: (i, 0)),
                  pl.BlockSpec((1, K), lambda i, j: (0, 0)),
                  pl.BlockSpec((K, tn), lambda i, j: (0, j))],
        out_specs=[pl.BlockSpec((tm, tn), lambda i, j: (i, j)),
                   pl.BlockSpec((tm, K), lambda i, j: (i, 0))],
        out_shape=[_sds((T, N), out_dtype), _sds((T, K), _MXU)],
        scratch_shapes=[pltpu.VMEM((tm, K), _MXU)],
        compiler_params=_params("parallel", "arbitrary"),
    )(x, gain, w)


def _matmul_res(name, a_list, w, res, *, tm, relu2=False, target=None):
    T = res.shape[0]
    N = w.shape[1]
    K = a_list[0].shape[1]
    n_a = len(a_list)
    tm = min(tm, T)
    with_loss = target is not None

    def body(*refs):
        a_refs = refs[:n_a]
        w_refs = refs[n_a:2 * n_a]
        res_ref = refs[2 * n_a]
        acc = res_ref[...]
        for a_ref, w_ref in zip(a_refs, w_refs):
            a = a_ref[...]
            if relu2:
                a = jnp.square(jnp.maximum(a.astype(F32), 0.0))
            acc = acc + _dot(a, w_ref[...])
        if with_loss:
            tgt_ref, g_ref, loss_ref = refs[2 * n_a + 1:]
            diff = acc - tgt_ref[...]
            g_ref[...] = diff * (1.0 / N)

            @pl.when(pl.program_id(0) == 0)
            def _():
                loss_ref[...] = jnp.zeros_like(loss_ref)
            loss_ref[...] += _rowsum8(diff * diff)
        else:
            refs[2 * n_a + 1][...] = acc

    row = lambda i: (i, 0)
    in_specs = [pl.BlockSpec((tm, K), row) for _ in a_list]
    in_specs += [pl.BlockSpec((K, N), functools.partial(lambda i, b: (b, 0), b=b)) for b in range(n_a)]
    in_specs += [pl.BlockSpec((tm, N), row)]
    args = list(a_list) + [w] * n_a + [res]
    if with_loss:
        in_specs.append(pl.BlockSpec((tm, N), row))
        args.append(target)
        out_specs = [pl.BlockSpec((tm, N), row), pl.BlockSpec((8, N), lambda i: (0, 0))]
        out_shape = [_sds((T, N), F32), _sds((8, N), F32)]
        sem = "arbitrary"
    else:
        out_specs = pl.BlockSpec((tm, N), row)
        out_shape = _sds((T, N), F32)
        sem = "parallel"
    return pl.pallas_call(body, name=name, grid=(T // tm,), in_specs=in_specs, out_specs=out_specs,
                          out_shape=out_shape, compiler_params=_params(sem))(*args)


def _matmul_nt(name, a, w, *, tm, tn, out_dtype, relu_of=None):
    T, K = a.shape
    N = w.shape[0]
    tm, tn = min(tm, T), min(tn, N)

    def body(*refs):
        if relu_of is None:
            a_ref, w_ref, o_ref = refs
            o_ref[...] = _dot(a_ref[...], w_ref[...], _NT).astype(o_ref.dtype)
        else:
            a_ref, w_ref, u_ref, o_ref = refs
            da = _dot(a_ref[...], w_ref[...], _NT)
            o_ref[...] = (da * (2.0 * jnp.maximum(u_ref[...].astype(F32), 0.0))).astype(o_ref.dtype)

    in_specs = [pl.BlockSpec((tm, K), lambda i, j: (i, 0)), pl.BlockSpec((tn, K), lambda i, j: (j, 0))]
    args = [a, w]
    if relu_of is not None:
        in_specs.append(pl.BlockSpec((tm, tn), lambda i, j: (i, j)))
        args.append(relu_of)
    return pl.pallas_call(body, name=name, grid=(T // tm, N // tn), in_specs=in_specs,
                          out_specs=pl.BlockSpec((tm, tn), lambda i, j: (i, j)),
                          out_shape=_sds((T, N), out_dtype),
                          compiler_params=_params("parallel", "parallel"))(*args)


def _matmul_nt_normbwd(name, dy, w, x, gain, dres, *, tm):
    T, K = dy.shape
    N = w.shape[0]
    tm = min(tm, T)

    def body(dy_ref, w_ref, x_ref, g_ref, dres_ref, dx_ref, dg_ref):
        dh = _dot(dy_ref[...], w_ref[...], _NT)
        xv = x_ref[...]
        r = lax.rsqrt(jnp.mean(xv * xv, axis=-1, keepdims=True) + EPS)
        xhat = xv * r
        dxhat = dh * g_ref[...]
        dx_ref[...] = dres_ref[...] + r * (dxhat - xhat * jnp.mean(dxhat * xhat, axis=-1, keepdims=True))

        @pl.when(pl.program_id(0) == 0)
        def _():
            dg_ref[...] = jnp.zeros_like(dg_ref)
        dg_ref[...] += _rowsum8(dh * xhat)

    row = lambda i: (i, 0)
    return pl.pallas_call(
        body, name=name, grid=(T // tm,),
        in_specs=[pl.BlockSpec((tm, K), row), pl.BlockSpec((N, K), lambda i: (0, 0)),
                  pl.BlockSpec((tm, N), row), pl.BlockSpec((1, N), lambda i: (0, 0)), pl.BlockSpec((tm, N), row)],
        out_specs=[pl.BlockSpec((tm, N), row), pl.BlockSpec((8, N), lambda i: (0, 0))],
        out_shape=[_sds((T, N), F32), _sds((8, N), F32)],
        compiler_params=_params("arbitrary"),
    )(dy, w, x, gain, dres)


def _matmul_tn(name, a, b, *, tk, tn, tt, relu2=False):
    T, Ka = a.shape
    Nb = b.shape[1]
    tk, tn, tt = min(tk, Ka), min(tn, Nb), min(tt, T)

    def body(a_ref, b_ref, o_ref):
        @pl.when(pl.program_id(2) == 0)
        def _():
            o_ref[...] = jnp.zeros_like(o_ref)
        av = a_ref[...]
        if relu2:
            av = jnp.square(jnp.maximum(av.astype(F32), 0.0))
        o_ref[...] += _dot(av, b_ref[...], _TN)

    return pl.pallas_call(
        body, name=name, grid=(Ka // tk, Nb // tn, T // tt),
        in_specs=[pl.BlockSpec((tt, tk), lambda i, j, t: (t, i)), pl.BlockSpec((tt, tn), lambda i, j, t: (t, j))],
        out_specs=pl.BlockSpec((tk, tn), lambda i, j, t: (i, j)),
        out_shape=_sds((Ka, Nb), F32),
        compiler_params=_params("parallel", "parallel", "arbitrary"),
    )(a, b)


def _rope_angles(pos, dim, theta):
    inv = theta ** (-jnp.arange(0, dim, 2, dtype=F32) / dim)
    return pos.astype(F32)[:, None] * inv[None, :]


def _ret_rope_tables(T):
    ang = _rope_angles(jnp.arange(T), RET_DK, RET_THETA)
    c, s = jnp.cos(ang), jnp.sin(ang)
    return jnp.concatenate([c, c], axis=1), jnp.concatenate([-s, s], axis=1)


def _axial_rope_tables(T):
    rows = T // GRID_W
    row = jnp.repeat(jnp.arange(rows), GRID_W)
    col = jnp.tile(jnp.arange(GRID_W), rows)
    ar = _rope_angles(row, HEAD_DIM // 2, AX_THETA)
    ac = _rope_angles(col, HEAD_DIM // 2, AX_THETA)
    cos = jnp.concatenate([jnp.cos(ar), jnp.cos(ar), jnp.cos(ac), jnp.cos(ac)], axis=1)
    sin = jnp.concatenate([-jnp.sin(ar), jnp.sin(ar), -jnp.sin(ac), jnp.sin(ac)], axis=1)
    return cos, sin


(TAB_D, TAB_DT, TAB_EF, TAB_EB, TAB_A, TAB_B, TAB_CF, TAB_CB,
 TAB_RA, TAB_RB, TAB_RCF, TAB_RCB, TAB_KF, TAB_KB) = range(14)


def _retention_tables(decay_logit):
    lg = jax.nn.log_sigmoid(decay_logit.astype(F32))
    lam, mu = lg[0][:, None, None], lg[1][:, None, None]
    idx = jnp.arange(CHUNK, dtype=F32)
    diff = (idx[:, None] - idx[None, :])[None]
    df = jnp.where(diff >= 0, jnp.exp(jnp.maximum(diff, 0.0) * lam), 0.0)
    db = jnp.where(diff < 0, jnp.exp(jnp.maximum(-diff, 0.0) * mu), 0.0)
    d = df + db
    r = idx[None, :, None]
    ones = jnp.ones((1, 1, CHUNK), F32)
    a = jnp.exp((r + 1.0) * lam) * ones
    b = jnp.exp((CHUNK - r) * mu) * ones
    cf = jnp.exp((CHUNK - 1.0 - r) * lam) * ones
    cb = jnp.exp(r * mu) * ones
    full = jnp.ones((1, CHUNK, CHUNK), F32)
    kf = CHUNK * jnp.exp(CHUNK * lam) * full
    kb = CHUNK * jnp.exp(CHUNK * mu) * full
    tabs = jnp.stack([d, jnp.swapaxes(d, 1, 2), diff * df, -diff * db, a, b, cf, cb,
                      (r + 1.0) * a, (CHUNK - r) * b, (CHUNK - 1.0 - r) * cf, r * cb, kf, kb], axis=1)

    def lanes(tab):
        return jnp.transpose(tab, (1, 0, 2)).reshape(CHUNK, RET_HEADS * CHUNK)

    def dec(l):
        return jnp.exp(CHUNK * l)[:, 0, :] * jnp.ones((1, RET_DV), F32)

    weights = dict(a=lanes(a), b=lanes(b), cf=lanes(cf), cb=lanes(cb), dec_f=dec(lam), dec_b=dec(mu))
    return tabs, weights, lg


def _t5_bucket(rel):
    nb = T5_BUCKETS // 2
    max_exact = nb // 2
    ret = jnp.where(rel > 0, nb, 0)
    n = jnp.abs(rel)
    nf = jnp.maximum(n, 1).astype(F32)
    large = max_exact + (jnp.log(nf / max_exact) / math.log(T5_MAX_DIST / max_exact)
                         * (nb - max_exact)).astype(jnp.int32)
    large = jnp.minimum(large, nb - 1)
    return ret + jnp.where(n < max_exact, n, large)


def _swa_rel():
    r = jnp.arange(CHUNK)
    j = jnp.arange(3 * CHUNK)
    return j[None, :] - CHUNK - r[:, None]


def _swa_bias(t5_table):
    rel = _swa_rel()
    bias = t5_table.astype(F32)[_t5_bucket(rel)]
    bias = jnp.where((jnp.abs(rel) <= CHUNK)[..., None], bias, NEG_INF)
    return jnp.transpose(bias, (2, 0, 1))


def _prep_even(proj, cos, sin, q_gain, k_gain, *, tm):
    T = proj.shape[0]
    tm = min(tm, T)

    def body(qa_ref, ka_ref, qb_ref, kb_ref, c_ref, s_ref, qg_ref, kg_ref, qr_ref, kr_ref, qn_ref, kn_ref):
        c = jnp.concatenate([c_ref[...]] * RET_HEADS, axis=1)
        s = jnp.concatenate([s_ref[...]] * RET_HEADS, axis=1)
        qa = qa_ref[...]
        qr_ref[...] = (qa * c + _swap_halves(qa, RET_DK // 2) * s).astype(qr_ref.dtype)
        ka = ka_ref[...]
        kr_ref[...] = ((ka * c + _swap_halves(ka, RET_DK // 2) * s) * (RET_DK ** -0.5)).astype(kr_ref.dtype)
        for src, gain, dst, heads in ((qb_ref, qg_ref, qn_ref, SWA_HEADS), (kb_ref, kg_ref, kn_ref, SWA_KV_HEADS)):
            for h in range(heads):
                sl = slice(h * HEAD_DIM, (h + 1) * HEAD_DIM)
                xh = src[:, sl]
                r = lax.rsqrt(jnp.mean(xh * xh, axis=-1, keepdims=True) + EPS)
                dst[:, sl] = (xh * r * gain[...]).astype(dst.dtype)

    row = lambda i: (i, 0)
    const = lambda i: (0, 0)
    return pl.pallas_call(
        body, name="prep_even", grid=(T // tm,),
        in_specs=[pl.BlockSpec((tm, RET_Q), lambda i: (i, 0)), pl.BlockSpec((tm, RET_Q), lambda i: (i, 1)),
                  pl.BlockSpec((tm, D_MODEL), lambda i: (i, 3)), pl.BlockSpec((tm, 256), lambda i: (i, 16)),
                  pl.BlockSpec((tm, RET_DK), row), pl.BlockSpec((tm, RET_DK), row),
                  pl.BlockSpec((1, HEAD_DIM), const), pl.BlockSpec((1, HEAD_DIM), const)],
        out_specs=[pl.BlockSpec((tm, RET_Q), row), pl.BlockSpec((tm, RET_Q), row),
                   pl.BlockSpec((tm, D_MODEL), row), pl.BlockSpec((tm, 256), row)],
        out_shape=[_sds((T, RET_Q), _MXU), _sds((T, RET_Q), _MXU), _sds((T, D_MODEL), _MXU), _sds((T, 256), _MXU)],
        compiler_params=_params("parallel"),
    )(proj, proj, proj, proj, cos, sin, q_gain, k_gain)


def _ret_scan(name, x, y, y_col, w_asc, dec_asc, w_desc, dec_desc):
    T = x.shape[0]
    nc = T // CHUNK

    def body(xa_ref, ya_ref, xd_ref, yd_ref, wa_ref, da_ref, wd_ref, dd_ref, sa_out, sd_out, sa, sd):
        @pl.when(pl.program_id(0) == 0)
        def _():
            sa[...] = jnp.zeros_like(sa)
            sd[...] = jnp.zeros_like(sd)
        sa_out[0] = sa[...].astype(sa_out.dtype)
        sd_out[0] = sd[...].astype(sd_out.dtype)
        for x_ref, y_ref, w_ref, d_ref, st in ((xa_ref, ya_ref, wa_ref, da_ref, sa), (xd_ref, yd_ref, wd_ref, dd_ref, sd)):
            for h in range(RET_HEADS):
                ks = slice(h * RET_DK, (h + 1) * RET_DK)
                vs = slice(h * RET_DV, (h + 1) * RET_DV)
                u = _dot(x_ref[:, ks].astype(F32) * w_ref[:, ks], y_ref[:, vs], _TN)
                st[ks, :] = st[ks, :] * d_ref[h:h + 1, :] + u

    asc = lambda i: (i, 0)
    desc = lambda i: (nc - 1 - i, 0)
    const = lambda i: (0, 0)
    return pl.pallas_call(
        body, name=name, grid=(nc,),
        in_specs=[pl.BlockSpec((CHUNK, RET_Q), asc), pl.BlockSpec((CHUNK, RET_V), lambda i: (i, y_col)),
                  pl.BlockSpec((CHUNK, RET_Q), desc), pl.BlockSpec((CHUNK, RET_V), lambda i: (nc - 1 - i, y_col)),
                  pl.BlockSpec((CHUNK, RET_Q), const), pl.BlockSpec((RET_HEADS, RET_DV), const),
                  pl.BlockSpec((CHUNK, RET_Q), const), pl.BlockSpec((RET_HEADS, RET_DV), const)],
        out_specs=[pl.BlockSpec((1, RET_Q, RET_DV), lambda i: (i, 0, 0)),
                   pl.BlockSpec((1, RET_Q, RET_DV), lambda i: (nc - 1 - i, 0, 0))],
        out_shape=[_sds((nc, RET_Q, RET_DV), _MXU), _sds((nc, RET_Q, RET_DV), _MXU)],
        scratch_shapes=[pltpu.VMEM((RET_Q, RET_DV), F32), pltpu.VMEM((RET_Q, RET_DV), F32)],
        compiler_params=_params("arbitrary"),
    )(x, y, x, y, w_asc, dec_asc, w_desc, dec_desc)


def _ret_out(qr, kr, proj, sf, sb, tabs, gain):
    T = qr.shape[0]
    nc = T // CHUNK

    def body(q_ref, k_ref, v_ref, g_ref, sf_ref, sb_ref, tab_ref, gain_ref, o_ref, y_ref):
        for h in range(RET_HEADS):
            ks = slice(h * RET_DK, (h + 1) * RET_DK)
            vs = slice(h * RET_DV, (h + 1) * RET_DV)
            q, k, v = q_ref[:, ks], k_ref[:, ks], v_ref[:, vs]
            qf = q.astype(F32)
            a_mat = _dot(q, k, _NT) * tab_ref[h, 0]
            o = (_dot(a_mat, v) + _dot(qf * tab_ref[h, 1], sf_ref[0, ks, :]) + _dot(qf * tab_ref[h, 2], sb_ref[0, ks, :]))
            o_ref[:, vs] = o
            r = lax.rsqrt(jnp.mean(o * o, axis=-1, keepdims=True) + EPS)
            g = g_ref[:, vs]
            y_ref[:, vs] = (g * _sigmoid(g) * (o * r * gain_ref[:, vs])).astype(y_ref.dtype)

    row = lambda i: (i, 0)
    return pl.pallas_call(
        body, name="ret_out", grid=(nc,),
        in_specs=[pl.BlockSpec((CHUNK, RET_Q), row), pl.BlockSpec((CHUNK, RET_Q), row),
                  pl.BlockSpec((CHUNK, RET_V), lambda i: (i, 1)), pl.BlockSpec((CHUNK, RET_V), lambda i: (i, 2)),
                  pl.BlockSpec((1, RET_Q, RET_DV), lambda i: (i, 0, 0)), pl.BlockSpec((1, RET_Q, RET_DV), lambda i: (i, 0, 0)),
                  pl.BlockSpec((RET_HEADS, 3, CHUNK, CHUNK), lambda i: (0, 0, 0, 0)),
                  pl.BlockSpec((1, RET_V), lambda i: (0, 0))],
        out_specs=[pl.BlockSpec((CHUNK, RET_V), row), pl.BlockSpec((CHUNK, RET_V), row)],
        out_shape=[_sds((T, RET_V), F32), _sds((T, RET_V), _MXU)],
        compiler_params=_params("parallel"),
    )(qr, kr, proj, proj, sf, sb, tabs, gain)


def _ret_gate_bwd(dycat, proj, ret_o, gain, *, tm):
    T = ret_o.shape[0]
    tm = min(tm, T)

    def body(dy_ref, g_ref, o_ref, gain_ref, do_ref, dg_ref, dgain_ref):
        @pl.when(pl.program_id(0) == 0)
        def _():
            dgain_ref[...] = jnp.zeros_like(dgain_ref)
        for h in range(RET_HEADS):
            vs = slice(h * RET_DV, (h + 1) * RET_DV)
            o, g, dya, gn = o_ref[:, vs], g_ref[:, vs], dy_ref[:, vs], gain_ref[:, vs]
            r = lax.rsqrt(jnp.mean(o * o, axis=-1, keepdims=True) + EPS)
            ohat = o * r
            sg = _sigmoid(g)
            dy = dya * (g * sg)
            dg_ref[:, vs] = (dya * (ohat * gn) * (sg * (1.0 + g * (1.0 - sg)))).astype(dg_ref.dtype)
            dyg = dy * gn
            do_ref[:, vs] = (r * (dyg - ohat * jnp.mean(dyg * ohat, axis=-1, keepdims=True))).astype(do_ref.dtype)
            dgain_ref[:, vs] += _rowsum8(dy * ohat)

    row = lambda i: (i, 0)
    return pl.pallas_call(
        body, name="ret_gate_bwd", grid=(T // tm,),
        in_specs=[pl.BlockSpec((tm, RET_V), row), pl.BlockSpec((tm, RET_V), lambda i: (i, 2)),
                  pl.BlockSpec((tm, RET_V), row), pl.BlockSpec((1, RET_V), lambda i: (0, 0))],
        out_specs=[pl.BlockSpec((tm, RET_V), row), pl.BlockSpec((tm, RET_V), row), pl.BlockSpec((8, RET_V), lambda i: (0, 0))],
        out_shape=[_sds((T, RET_V), _MXU), _sds((T, RET_V), _MXU), _sds((8, RET_V), F32)],
        compiler_params=_params("arbitrary"),
    )(dycat, proj, ret_o, gain)


def _ret_bwd(qr, kr, proj, g_out, sf, sb, rf, rb, tabs):
    T = qr.shape[0]
    nc = T // CHUNK

    def body(q_ref, k_ref, v_ref, g_ref, sf_ref, sb_ref, rf_ref, rb_ref, tab_ref, dq_ref, dk_ref, dv_ref, dl_ref):
        @pl.when(pl.program_id(0) == 0)
        def _():
            dl_ref[...] = jnp.zeros_like(dl_ref)
        for h in range(RET_HEADS):
            ks = slice(h * RET_DK, (h + 1) * RET_DK)
            vs = slice(h * RET_DV, (h + 1) * RET_DV)
            q, k, v, g = q_ref[:, ks], k_ref[:, ks], v_ref[:, vs], g_ref[:, vs]
            s_f, s_b, r_f, r_b = sf_ref[0, ks, :], sb_ref[0, ks, :], rf_ref[0, ks, :], rb_ref[0, ks, :]
            tab = lambda t: tab_ref[h, t]
            qf, kf = q.astype(F32), k.astype(F32)
            qk = _dot(q, k, _NT)
            da_raw = _dot(g, v, _NT)
            x_f, x_b = _dot(g, s_f, _NT), _dot(g, s_b, _NT)
            dq_ref[:, ks] = _dot(da_raw * tab(TAB_D), k) + tab(TAB_A) * x_f + tab(TAB_B) * x_b
            at = _dot(k, q, _NT) * tab(TAB_DT)
            dat = _dot(v, g, _NT) * tab(TAB_DT)
            y_f, y_b = _dot(v, r_f, _NT), _dot(v, r_b, _NT)
            dk_ref[:, ks] = _dot(dat, q) + tab(TAB_CF) * y_f + tab(TAB_CB) * y_b
            dv_ref[:, vs] = (_dot(at, g) + _dot(kf * tab(TAB_CF), r_f) + _dot(kf * tab(TAB_CB), r_b)).astype(dv_ref.dtype)
            inner = da_raw * qk
            rs_f = r_f.astype(F32) * s_f.astype(F32)
            rs_b = r_b.astype(F32) * s_b.astype(F32)
            dl_f = (inner * tab(TAB_EF) + tab(TAB_RA) * qf * x_f + tab(TAB_RCF) * kf * y_f
                    + tab(TAB_KF) * (rs_f[:, :CHUNK] + rs_f[:, CHUNK:]))
            dl_b = (inner * tab(TAB_EB) + tab(TAB_RB) * qf * x_b + tab(TAB_RCB) * kf * y_b
                    + tab(TAB_KB) * (rs_b[:, :CHUNK] + rs_b[:, CHUNK:]))
            dl_ref[2 * h:2 * h + 1, :] += jnp.sum(dl_f, axis=0, keepdims=True)
            dl_ref[2 * h + 1:2 * h + 2, :] += jnp.sum(dl_b, axis=0, keepdims=True)

    row = lambda i: (i, 0)
    st = lambda i: (i, 0, 0)
    return pl.pallas_call(
        body, name="ret_bwd", grid=(nc,),
        in_specs=[pl.BlockSpec((CHUNK, RET_Q), row), pl.BlockSpec((CHUNK, RET_Q), row),
                  pl.BlockSpec((CHUNK, RET_V), lambda i: (i, 1)), pl.BlockSpec((CHUNK, RET_V), row),
                  pl.BlockSpec((1, RET_Q, RET_DV), st), pl.BlockSpec((1, RET_Q, RET_DV), st),
                  pl.BlockSpec((1, RET_Q, RET_DV), st), pl.BlockSpec((1, RET_Q, RET_DV), st),
                  pl.BlockSpec((RET_HEADS, 14, CHUNK, CHUNK), lambda i: (0, 0, 0, 0))],
        out_specs=[pl.BlockSpec((CHUNK, RET_Q), row), pl.BlockSpec((CHUNK, RET_Q), row),
                   pl.BlockSpec((CHUNK, RET_V), row), pl.BlockSpec((8, CHUNK), lambda i: (0, 0))],
        out_shape=[_sds((T, RET_Q), F32), _sds((T, RET_Q), F32), _sds((T, RET_V), _MXU), _sds((8, CHUNK), F32)],
        compiler_params=_params("arbitrary"),
    )(qr, kr, proj, g_out, sf, sb, rf, rb, tabs)


def _swa_probs(q, k_win, bias, sink, valid):
    s = _dot(q, k_win, _NT) * ATT_SCALE + bias
    s = jnp.where(valid, s, NEG_INF)
    m = jnp.maximum(jnp.max(s, axis=-1, keepdims=True), sink)
    p = jnp.exp(s - m)
    e_sink = jnp.exp(sink - m)
    inv = 1.0 / (jnp.sum(p, axis=-1, keepdims=True) + e_sink)
    return p * inv, e_sink * inv


def _swa_valid(i, nb):
    col = lax.broadcasted_iota(jnp.int32, (1, 3 * CHUNK), 1)
    return jnp.logical_and(jnp.logical_or(col >= CHUNK, i > 0), jnp.logical_or(col < 2 * CHUNK, i < nb - 1))


def _swa_window_specs(nb, width, col_block, clamp):
    prev = lambda i: (jnp.maximum(clamp(i) - 1, 0), col_block)
    cur = lambda i: (clamp(i), col_block)
    nxt = lambda i: (jnp.minimum(clamp(i) + 1, nb - 1), col_block)
    return [pl.BlockSpec((CHUNK, width), f) for f in (prev, cur, nxt)]


def _swa_fwd(qn, kn, proj, bias, sink):
    T = qn.shape[0]
    nb = T // CHUNK
    kvw = SWA_KV_HEADS * HEAD_DIM
    group = SWA_HEADS // SWA_KV_HEADS

    def body(q_ref, k0, k1, k2, v0, v1, v2, bias_ref, sink_ref, y_ref):
        i = pl.program_id(0)
        valid = _swa_valid(i, nb)
        for g in range(SWA_KV_HEADS):
            gs = slice(g * HEAD_DIM, (g + 1) * HEAD_DIM)
            k_win = jnp.concatenate([k0[:, gs], k1[:, gs], k2[:, gs]], axis=0)
            v_win = jnp.concatenate([v0[:, gs], v1[:, gs], v2[:, gs]], axis=0).astype(_MXU)
            for hh in range(group):
                h = g * group + hh
                hs = slice(h * HEAD_DIM, (h + 1) * HEAD_DIM)
                p, _ = _swa_probs(q_ref[:, hs], k_win, bias_ref[h], sink_ref[h:h + 1, 0:1], valid)
                y_ref[:, hs] = _dot(p, v_win).astype(y_ref.dtype)

    ident = lambda i: i
    return pl.pallas_call(
        body, name="swa_fwd", grid=(nb,),
        in_specs=[pl.BlockSpec((CHUNK, D_MODEL), lambda i: (i, 0))]
        + _swa_window_specs(nb, kvw, 0, ident) + _swa_window_specs(nb, kvw, 17, ident)
        + [pl.BlockSpec((SWA_HEADS, CHUNK, 3 * CHUNK), lambda i: (0, 0, 0)), pl.BlockSpec((SWA_HEADS, HEAD_DIM), lambda i: (0, 0))],
        out_specs=pl.BlockSpec((CHUNK, D_MODEL), lambda i: (i, 0)),
        out_shape=_sds((T, D_MODEL), _MXU),
        compiler_params=_params("parallel"),
    )(qn, kn, kn, kn, proj, proj, proj, bias, sink)


def _swa_bwd(qn, kn, proj, dycat, bias, sink):
    T = qn.shape[0]
    nb = T // CHUNK
    kvw = SWA_KV_HEADS * HEAD_DIM
    group = SWA_HEADS // SWA_KV_HEADS

    def body(q_ref, k0, k1, k2, v0, v1, v2, dy_ref, bias_ref, sink_ref,
             dq_ref, dk_ref, dv_ref, dbias_ref, dsink_ref, acc_a, acc_b):
        i = pl.program_id(0)

        @pl.when(i == 0)
        def _():
            dbias_ref[...] = jnp.zeros_like(dbias_ref)
            dsink_ref[...] = jnp.zeros_like(dsink_ref)
            acc_a[...] = jnp.zeros_like(acc_a)
            acc_b[...] = jnp.zeros_like(acc_b)

        @pl.when(i < nb)
        def _():
            valid = _swa_valid(i, nb)
            for g in range(SWA_KV_HEADS):
                gs = slice(g * HEAD_DIM, (g + 1) * HEAD_DIM)
                k_win = jnp.concatenate([k0[:, gs], k1[:, gs], k2[:, gs]], axis=0)
                v_win = jnp.concatenate([v0[:, gs], v1[:, gs], v2[:, gs]], axis=0).astype(_MXU)
                dk_win = jnp.zeros((3 * CHUNK, HEAD_DIM), F32)
                dv_win = jnp.zeros((3 * CHUNK, HEAD_DIM), F32)
                for hh in range(group):
                    h = g * group + hh
                    hs = slice(h * HEAD_DIM, (h + 1) * HEAD_DIM)
                    q, dy = q_ref[:, hs], dy_ref[:, hs]
                    p, p_sink = _swa_probs(q, k_win, bias_ref[h], sink_ref[h:h + 1, 0:1], valid)
                    dp = _dot(dy, v_win, _NT)
                    delta = jnp.sum(p * dp, axis=-1, keepdims=True)
                    ds = p * (dp - delta)
                    dbias_ref[h] += ds
                    dsink_ref[h:h + 1, :] += jnp.sum(-p_sink * delta, axis=0, keepdims=True) * jnp.ones((1, HEAD_DIM), F32)
                    dq_ref[:, hs] = _dot(ds, k_win) * ATT_SCALE
                    dk_win = dk_win + _dot(ds, q, _TN) * ATT_SCALE
                    dv_win = dv_win + _dot(p, dy, _TN)
                for win, out_ref, col0 in ((dk_win, dk_ref, 0), (dv_win, dv_ref, kvw)):
                    cs = slice(col0 + g * HEAD_DIM, col0 + (g + 1) * HEAD_DIM)
                    out_ref[:, gs] = acc_a[:, cs] + win[:CHUNK]
                    acc_a[:, cs] = acc_b[:, cs] + win[CHUNK:2 * CHUNK]
                    acc_b[:, cs] = win[2 * CHUNK:]

        @pl.when(i == nb)
        def _():
            dk_ref[...] = acc_a[:, :kvw]
            dv_ref[...] = acc_a[:, kvw:]

    clamp = lambda i: jnp.minimum(i, nb - 1)
    late = lambda i: (jnp.maximum(i - 1, 0), 0)
    return pl.pallas_call(
        body, name="swa_bwd", grid=(nb + 1,),
        in_specs=[pl.BlockSpec((CHUNK, D_MODEL), lambda i: (clamp(i), 0))]
        + _swa_window_specs(nb, kvw, 0, clamp) + _swa_window_specs(nb, kvw, 17, clamp)
        + [pl.BlockSpec((CHUNK, D_MODEL), lambda i: (clamp(i), 1)),
           pl.BlockSpec((SWA_HEADS, CHUNK, 3 * CHUNK), lambda i: (0, 0, 0)), pl.BlockSpec((SWA_HEADS, HEAD_DIM), lambda i: (0, 0))],
        out_specs=[pl.BlockSpec((CHUNK, D_MODEL), lambda i: (clamp(i), 0)),
                   pl.BlockSpec((CHUNK, kvw), late), pl.BlockSpec((CHUNK, kvw), late),
                   pl.BlockSpec((SWA_HEADS, CHUNK, 3 * CHUNK), lambda i: (0, 0, 0)), pl.BlockSpec((SWA_HEADS, HEAD_DIM), lambda i: (0, 0))],
        out_shape=[_sds((T, D_MODEL), F32), _sds((T, kvw), F32), _sds((T, kvw), F32),
                   _sds((SWA_HEADS, CHUNK, 3 * CHUNK), F32), _sds((SWA_HEADS, HEAD_DIM), F32)],
        scratch_shapes=[pltpu.VMEM((CHUNK, 2 * kvw), F32), pltpu.VMEM((CHUNK, 2 * kvw), F32)],
        compiler_params=_params("arbitrary"),
    )(qn, kn, kn, kn, proj, proj, proj, dycat, bias, sink)


def _t5_bucket_reduce(dbias, bucket):
    def body(db_ref, bk_ref, o_ref):
        bk = bk_ref[...]
        row = lax.broadcasted_iota(jnp.int32, (SWA_HEADS, HEAD_DIM), 0)
        lane = lax.broadcasted_iota(jnp.int32, (SWA_HEADS, HEAD_DIM), 1)

        def per_bucket(b, acc):
            mask = bk == b
            for h in range(SWA_HEADS):
                tot = jnp.sum(jnp.sum(jnp.where(mask, db_ref[h], 0.0), axis=0, keepdims=True), axis=1, keepdims=True)
                acc = acc + jnp.where(jnp.logical_and(row == h, lane == b), tot, 0.0)
            return acc

        o_ref[...] = lax.fori_loop(0, T5_BUCKETS, per_bucket, jnp.zeros((SWA_HEADS, HEAD_DIM), F32))

    return pl.pallas_call(body, name="t5_bucket_reduce", out_shape=_sds((SWA_HEADS, HEAD_DIM), F32),
                          compiler_params=pltpu.CompilerParams(vmem_limit_bytes=VMEM_LIMIT_BYTES))(dbias, bucket)


def _headnorm_bwd(x, dy, gain):
    r = lax.rsqrt(jnp.mean(x * x, axis=-1, keepdims=True) + EPS)
    xhat = x * r
    dyg = dy * gain
    return r * (dyg - xhat * jnp.mean(dyg * xhat, axis=-1, keepdims=True)), dy * xhat


def _post_even(proj, dqr, dkr, dva, dga, dqn, dkn, dvb, cos, sin, q_gain, k_gain, *, tm):
    T = proj.shape[0]
    tm = min(tm, T)
    kvw = SWA_KV_HEADS * HEAD_DIM

    def body(qb_ref, kb_ref, dqr_ref, dkr_ref, dva_ref, dga_ref, dqn_ref, dkn_ref, dvb_ref, c_ref, s_ref, qg_ref, kg_ref,
             dp_ref, dqg_ref, dkg_ref):
        @pl.when(pl.program_id(0) == 0)
        def _():
            dqg_ref[...] = jnp.zeros_like(dqg_ref)
            dkg_ref[...] = jnp.zeros_like(dkg_ref)
        c = jnp.concatenate([c_ref[...]] * RET_HEADS, axis=1)
        s = jnp.concatenate([s_ref[...]] * RET_HEADS, axis=1)
        dq = dqr_ref[...]
        dp_ref[:, 0:RET_Q] = (dq * c + _swap_halves(dq * s, RET_DK // 2)).astype(dp_ref.dtype)
        dk = dkr_ref[...] * (RET_DK ** -0.5)
        dp_ref[:, RET_Q:2 * RET_Q] = (dk * c + _swap_halves(dk * s, RET_DK // 2)).astype(dp_ref.dtype)
        off = 2 * RET_Q
        dp_ref[:, off:off + RET_V] = dva_ref[...].astype(dp_ref.dtype)
        dp_ref[:, off + RET_V:off + 2 * RET_V] = dga_ref[...].astype(dp_ref.dtype)
        off += 2 * RET_V
        for src, dsrc, gain, dgain, heads, base in ((qb_ref, dqn_ref, qg_ref, dqg_ref, SWA_HEADS, off),
                                                    (kb_ref, dkn_ref, kg_ref, dkg_ref, SWA_KV_HEADS, off + D_MODEL)):
            for h in range(heads):
                sl = slice(h * HEAD_DIM, (h + 1) * HEAD_DIM)
                dx, dgx = _headnorm_bwd(src[:, sl], dsrc[:, sl], gain[...])
                dp_ref[:, base + h * HEAD_DIM:base + (h + 1) * HEAD_DIM] = dx.astype(dp_ref.dtype)
                dgain[...] += _rowsum8(dgx)
        dp_ref[:, off + D_MODEL + kvw:] = dvb_ref[...].astype(dp_ref.dtype)

    row = lambda i: (i, 0)
    const = lambda i: (0, 0)
    return pl.pallas_call(
        body, name="post_even", grid=(T // tm,),
        in_specs=[pl.BlockSpec((tm, D_MODEL), lambda i: (i, 3)), pl.BlockSpec((tm, kvw), lambda i: (i, 16)),
                  pl.BlockSpec((tm, RET_Q), row), pl.BlockSpec((tm, RET_Q), row),
                  pl.BlockSpec((tm, RET_V), row), pl.BlockSpec((tm, RET_V), row),
                  pl.BlockSpec((tm, D_MODEL), row), pl.BlockSpec((tm, kvw), row), pl.BlockSpec((tm, kvw), row),
                  pl.BlockSpec((tm, RET_DK), row), pl.BlockSpec((tm, RET_DK), row),
                  pl.BlockSpec((1, HEAD_DIM), const), pl.BlockSpec((1, HEAD_DIM), const)],
        out_specs=[pl.BlockSpec((tm, EVEN_IN), row), pl.BlockSpec((8, HEAD_DIM), const), pl.BlockSpec((8, HEAD_DIM), const)],
        out_shape=[_sds((T, EVEN_IN), _MXU), _sds((8, HEAD_DIM), F32), _sds((8, HEAD_DIM), F32)],
        compiler_params=_params("arbitrary"),
    )(proj, proj, dqr, dkr, dva, dga, dqn, dkn, dvb, cos, sin, q_gain, k_gain)


def _prep_odd(proj, cos, sin, q_gain, k_gain, *, tm):
    T = proj.shape[0]
    tm = min(tm, T)
    kvw = AX_KV_HEADS * HEAD_DIM

    def body(q_ref, k_ref, v_ref, c_ref, s_ref, qg_ref, kg_ref, qx_ref, kx_ref, vx_ref):
        c, s = c_ref[...], s_ref[...]
        for src, gain, dst, heads in ((q_ref, qg_ref, qx_ref, AX_HEADS), (k_ref, kg_ref, kx_ref, AX_KV_HEADS)):
            for h in range(heads):
                sl = slice(h * HEAD_DIM, (h + 1) * HEAD_DIM)
                xh = src[:, sl]
                r = lax.rsqrt(jnp.mean(xh * xh, axis=-1, keepdims=True) + EPS)
                xn = xh * r * gain[...]
                dst[:, sl] = (xn * c + _swap_halves(xn, HEAD_DIM // 4) * s).astype(dst.dtype)
        vx_ref[...] = v_ref[...].astype(vx_ref.dtype)

    row = lambda i: (i, 0)
    const = lambda i: (0, 0)
    return pl.pallas_call(
        body, name="prep_odd", grid=(T // tm,),
        in_specs=[pl.BlockSpec((tm, D_MODEL), row), pl.BlockSpec((tm, kvw), lambda i: (i, 4)), pl.BlockSpec((tm, kvw), lambda i: (i, 5)),
                  pl.BlockSpec((tm, HEAD_DIM), row), pl.BlockSpec((tm, HEAD_DIM), row),
                  pl.BlockSpec((1, HEAD_DIM), const), pl.BlockSpec((1, HEAD_DIM), const)],
        out_specs=[pl.BlockSpec((tm, D_MODEL), row), pl.BlockSpec((tm, kvw), row), pl.BlockSpec((tm, kvw), row)],
        out_shape=[_sds((T, D_MODEL), _MXU), _sds((T, kvw), _MXU), _sds((T, kvw), _MXU)],
        compiler_params=_params("parallel"),
    )(proj, proj, proj, cos, sin, q_gain, k_gain)


def _post_odd(proj, dqx, dkx, dvx, cos, sin, q_gain, k_gain, *, tm):
    T = proj.shape[0]
    tm = min(tm, T)
    kvw = AX_KV_HEADS * HEAD_DIM

    def body(q_ref, k_ref, dq_ref, dk_ref, dv_ref, c_ref, s_ref, qg_ref, kg_ref, dp_ref, dqg_ref, dkg_ref):
        @pl.when(pl.program_id(0) == 0)
        def _():
            dqg_ref[...] = jnp.zeros_like(dqg_ref)
            dkg_ref[...] = jnp.zeros_like(dkg_ref)
        c, s = c_ref[...], s_ref[...]
        for src, dsrc, gain, dgain, heads, base in ((q_ref, dq_ref, qg_ref, dqg_ref, AX_HEADS, 0),
                                                    (k_ref, dk_ref, kg_ref, dkg_ref, AX_KV_HEADS, D_MODEL)):
            for h in range(heads):
                sl = slice(h * HEAD_DIM, (h + 1) * HEAD_DIM)
                d = dsrc[:, sl]
                dn = d * c + _swap_halves(d * s, HEAD_DIM // 4)
                dx, dgx = _headnorm_bwd(src[:, sl], dn, gain[...])
                dp_ref[:, base + h * HEAD_DIM:base + (h + 1) * HEAD_DIM] = dx.astype(dp_ref.dtype)
                dgain[...] += _rowsum8(dgx)
        dp_ref[:, D_MODEL + kvw:] = dv_ref[...].astype(dp_ref.dtype)

    row = lambda i: (i, 0)
    const = lambda i: (0, 0)
    return pl.pallas_call(
        body, name="post_odd", grid=(T // tm,),
        in_specs=[pl.BlockSpec((tm, D_MODEL), row), pl.BlockSpec((tm, kvw), lambda i: (i, 4)),
                  pl.BlockSpec((tm, D_MODEL), row), pl.BlockSpec((tm, kvw), row), pl.BlockSpec((tm, kvw), row),
                  pl.BlockSpec((tm, HEAD_DIM), row), pl.BlockSpec((tm, HEAD_DIM), row),
                  pl.BlockSpec((1, HEAD_DIM), const), pl.BlockSpec((1, HEAD_DIM), const)],
        out_specs=[pl.BlockSpec((tm, ODD_IN), row), pl.BlockSpec((8, HEAD_DIM), const), pl.BlockSpec((8, HEAD_DIM), const)],
        out_shape=[_sds((T, ODD_IN), _MXU), _sds((8, HEAD_DIM), F32), _sds((8, HEAD_DIM), F32)],
        compiler_params=_params("arbitrary"),
    )(proj, proj, dqx, dkx, dvx, cos, sin, q_gain, k_gain)


def _flash_fwd(qx, kx, vx, *, tq, tk):
    T = qx.shape[0]
    tq, tk = min(tq, T), min(tk, T)
    group = AX_HEADS // AX_KV_HEADS

    def body(q_ref, k_ref, v_ref, o_ref, lse_ref):
        q = q_ref[...]

        def step(j, carry):
            m, l, acc = carry
            off = pl.multiple_of(j * tk, tk)
            k, v = k_ref[pl.ds(off, tk), :], v_ref[pl.ds(off, tk), :]
            s = _dot(q, k, _NT) * ATT_SCALE
            m_new = jnp.maximum(m, jnp.max(s, axis=-1, keepdims=True))
            p = jnp.exp(s - m_new)
            alpha = jnp.exp(m - m_new)
            return m_new, alpha * l + jnp.sum(p, axis=-1, keepdims=True), alpha * acc + _dot(p, v)

        init = (jnp.full((tq, 1), NEG_INF, F32), jnp.zeros((tq, 1), F32), jnp.zeros((tq, HEAD_DIM), F32))
        m, l, acc = lax.fori_loop(0, T // tk, step, init)
        o_ref[...] = (acc / l).astype(o_ref.dtype)
        lse_ref[0] = m + jnp.log(l)

    return pl.pallas_call(
        body, name="flash_fwd", grid=(AX_HEADS, T // tq),
        in_specs=[pl.BlockSpec((tq, HEAD_DIM), lambda h, i: (i, h)),
                  pl.BlockSpec((T, HEAD_DIM), lambda h, i: (0, h // group)), pl.BlockSpec((T, HEAD_DIM), lambda h, i: (0, h // group))],
        out_specs=[pl.BlockSpec((tq, HEAD_DIM), lambda h, i: (i, h)), pl.BlockSpec((1, tq, 1), lambda h, i: (h, i, 0))],
        out_shape=[_sds((T, D_MODEL), _MXU), _sds((AX_HEADS, T, 1), F32)],
        compiler_params=_params("parallel", "parallel"),
    )(qx, kx, vx)


def _flash_bwd_dq(qx, kx, vx, o, do, lse, *, tq, tk):
    T = qx.shape[0]
    tq, tk = min(tq, T), min(tk, T)
    group = AX_HEADS // AX_KV_HEADS

    def body(q_ref, k_ref, v_ref, o_ref, do_ref, lse_ref, dq_ref, delta_ref):
        q, do_blk = q_ref[...], do_ref[...]
        delta = jnp.sum(do_blk.astype(F32) * o_ref[...].astype(F32), axis=-1, keepdims=True)
        lse_blk = lse_ref[0]

        def step(j, dq):
            off = pl.multiple_of(j * tk, tk)
            k, v = k_ref[pl.ds(off, tk), :], v_ref[pl.ds(off, tk), :]
            p = jnp.exp(_dot(q, k, _NT) * ATT_SCALE - lse_blk)
            ds = p * (_dot(do_blk, v, _NT) - delta) * ATT_SCALE
            return dq + _dot(ds, k)

        dq_ref[...] = lax.fori_loop(0, T // tk, step, jnp.zeros((tq, HEAD_DIM), F32))
        delta_ref[0] = delta

    blk = lambda h, i: (i, h)
    stat = lambda h, i: (h, i, 0)
    return pl.pallas_call(
        body, name="flash_bwd_dq", grid=(AX_HEADS, T // tq),
        in_specs=[pl.BlockSpec((tq, HEAD_DIM), blk),
                  pl.BlockSpec((T, HEAD_DIM), lambda h, i: (0, h // group)), pl.BlockSpec((T, HEAD_DIM), lambda h, i: (0, h // group)),
                  pl.BlockSpec((tq, HEAD_DIM), blk), pl.BlockSpec((tq, HEAD_DIM), blk), pl.BlockSpec((1, tq, 1), stat)],
        out_specs=[pl.BlockSpec((tq, HEAD_DIM), blk), pl.BlockSpec((1, tq, 1), stat)],
        out_shape=[_sds((T, D_MODEL), F32), _sds((AX_HEADS, T, 1), F32)],
        compiler_params=_params("parallel", "parallel"),
    )(qx, kx, vx, o, do, lse)


def _flash_bwd_dkv(qx, kx, vx, do, lse, delta, *, tq, tk):
    T = qx.shape[0]
    tq, tk = min(tq, T), min(tk, T)
    nq = T // tq
    group = AX_HEADS // AX_KV_HEADS
    lse_rows = lse.reshape(AX_HEADS, nq, 1, tq)
    delta_rows = delta.reshape(AX_HEADS, nq, 1, tq)

    def body(k_ref, v_ref, q_ref, do_ref, lse_ref, delta_ref, dk_ref, dv_ref):
        @pl.when(pl.program_id(2) == 0)
        def _():
            dk_ref[...] = jnp.zeros_like(dk_ref)
            dv_ref[...] = jnp.zeros_like(dv_ref)
        k, v = k_ref[...], v_ref[...]

        def step(i, carry):
            dk, dv = carry
            off = pl.multiple_of(i * tq, tq)
            q, do_blk = q_ref[pl.ds(off, tq), :], do_ref[pl.ds(off, tq), :]
            pt = jnp.exp(_dot(k, q, _NT) * ATT_SCALE - lse_ref[0, i])
            dv = dv + _dot(pt, do_blk)
            dst = pt * (_dot(v, do_blk, _NT) - delta_ref[0, i]) * ATT_SCALE
            return dk + _dot(dst, q), dv

        zero = jnp.zeros((tk, HEAD_DIM), F32)
        dk, dv = lax.fori_loop(0, nq, step, (zero, zero))
        dk_ref[...] += dk
        dv_ref[...] += dv

    kv = lambda g, j, h: (j, g)
    qh = lambda g, j, h: (0, g * group + h)
    st = lambda g, j, h: (g * group + h, 0, 0, 0)
    return pl.pallas_call(
        body, name="flash_bwd_dkv", grid=(AX_KV_HEADS, T // tk, group),
        in_specs=[pl.BlockSpec((tk, HEAD_DIM), kv), pl.BlockSpec((tk, HEAD_DIM), kv),
                  pl.BlockSpec((T, HEAD_DIM), qh), pl.BlockSpec((T, HEAD_DIM), qh),
                  pl.BlockSpec((1, nq, 1, tq), st), pl.BlockSpec((1, nq, 1, tq), st)],
        out_specs=[pl.BlockSpec((tk, HEAD_DIM), kv), pl.BlockSpec((tk, HEAD_DIM), kv)],
        out_shape=[_sds((T, AX_KV_HEADS * HEAD_DIM), F32), _sds((T, AX_KV_HEADS * HEAD_DIM), F32)],
        compiler_params=_params("parallel", "parallel", "arbitrary"),
    )(kx, vx, qx, do, lse_rows, delta_rows)


TM = 512
TM_WIDE = 256


def _mlp_fwd(tag, x, gain, w_up, w_down, target=None):
    u, h = _norm_matmul(f"mlp_up{tag}", x, gain, w_up, tm=TM, tn=1024, out_dtype=F32)
    out = _matmul_res(f"mlp_down{tag}", [u], w_down, x, tm=TM_WIDE, relu2=True, target=target)
    return out, (x, u, h)


def _mlp_bwd(tag, saved, gain, w_up, w_down, dy):
    x, u, h = saved
    du = _matmul_nt(f"mlp_down{tag}_bwd", dy, w_down, tm=TM, tn=1024, out_dtype=_MXU, relu_of=u)
    dw_down = _matmul_tn(f"mlp_down{tag}_dw", u, dy, tk=1024, tn=1024, tt=1024, relu2=True)
    dx, dgain = _matmul_nt_normbwd(f"mlp_up{tag}_bwd", du, w_up, x, gain, dy, tm=TM_WIDE)
    dw_up = _matmul_tn(f"mlp_up{tag}_dw", h, du, tk=1024, tn=1024, tt=1024)
    return dx, dgain, dw_up, dw_down


def _local_step(x, target, p, wf):
    T = x.shape[0]
    cos_r, sin_r = _ret_rope_tables(T)
    cos_a, sin_a = _axial_rope_tables(T)
    tabs, rw, log_gamma = _retention_tables(p["ret_decay_logit"][0])
    bias = _swa_bias(p["t5_table"])
    sink = p["swa_sink"][0][:, None] * jnp.ones((1, HEAD_DIM), F32)
    nm, nl = p["norm_mix"], p["norm_mlp"]

    proj0, h0 = _norm_matmul("in_even", x, nm[0:1], wf["w_in_even"], tm=TM, tn=1152, out_dtype=F32)
    qr, kr, qn, kn = _prep_even(proj0, cos_r, sin_r, p["swa_q_norm"], p["swa_k_norm"], tm=TM)
    sf, sb = _ret_scan("ret_scan_fwd", kr, proj0, 1, rw["cf"], rw["dec_f"], rw["cb"], rw["dec_b"])
    ret_o, ya = _ret_out(qr, kr, proj0, sf, sb, tabs[:, (TAB_D, TAB_A, TAB_B)], p["ret_norm"])
    yb = _swa_fwd(qn, kn, proj0, bias, sink)
    x1 = _matmul_res("out_even", [ya, yb], wf["w_out_even"], x, tm=TM)
    x2, mlp0 = _mlp_fwd(0, x1, nl[0:1], wf["w_mlp_up"][0], wf["w_mlp_down"][0])
    proj1, h1 = _norm_matmul("in_odd", x2, nm[1:2], wf["w_in_odd"], tm=TM, tn=768, out_dtype=F32)
    qx, kx, vx = _prep_odd(proj1, cos_a, sin_a, p["ax_q_norm"], p["ax_k_norm"], tm=TM)
    o, lse = _flash_fwd(qx, kx, vx, tq=512, tk=1024)
    x3 = _matmul_res("out_odd", [o], wf["w_out_odd"], x2, tm=TM)
    (g4, loss_part), mlp1 = _mlp_fwd(1, x3, nl[1:2], wf["w_mlp_up"][1], wf["w_mlp_down"][1], target=target)

    dx3, dnl1, dw_up1, dw_down1 = _mlp_bwd(1, mlp1, nl[1:2], wf["w_mlp_up"][1], wf["w_mlp_down"][1], g4)
    do = _matmul_nt("out_odd_bwd", dx3, wf["w_out_odd"], tm=TM, tn=1024, out_dtype=_MXU)
    dw_out_odd = _matmul_tn("out_odd_dw", o, dx3, tk=1024, tn=1024, tt=1024)
    dqx, delta = _flash_bwd_dq(qx, kx, vx, o, do, lse, tq=512, tk=1024)
    dkx, dvx = _flash_bwd_dkv(qx, kx, vx, do, lse, delta, tq=1024, tk=512)
    dproj1, dqg1, dkg1 = _post_odd(proj1, dqx, dkx, dvx, cos_a, sin_a, p["ax_q_norm"], p["ax_k_norm"], tm=TM)
    dx2, dnm1 = _matmul_nt_normbwd("in_odd_bwd", dproj1, wf["w_in_odd"], x2, nm[1:2], dx3, tm=TM_WIDE)
    dw_in_odd = _matmul_tn("in_odd_dw", h1, dproj1, tk=1024, tn=768, tt=1024)
    dx1, dnl0, dw_up0, dw_down0 = _mlp_bwd(0, mlp0, nl[0:1], wf["w_mlp_up"][0], wf["w_mlp_down"][0], dx2)
    dycat = _matmul_nt("out_even_bwd", dx1, wf["w_out_even"], tm=TM, tn=1024, out_dtype=F32)
    dw_out_even = jnp.concatenate([_matmul_tn("out_even_dw_ret", ya, dx1, tk=1024, tn=1024, tt=1024),
                                   _matmul_tn("out_even_dw_swa", yb, dx1, tk=1024, tn=1024, tt=1024)], axis=0)
    g_out, dga, dretg = _ret_gate_bwd(dycat, proj0, ret_o, p["ret_norm"], tm=TM)
    rb, rf = _ret_scan("ret_scan_bwd", qr, g_out, 0, rw["b"], rw["dec_b"], rw["a"], rw["dec_f"])
    dqr, dkr, dva, dlog = _ret_bwd(qr, kr, proj0, g_out, sf, sb, rf, rb, tabs)
    dqn, dkn, dvb, dbias, dsink = _swa_bwd(qn, kn, proj0, dycat, bias, sink)
    dt5 = _t5_bucket_reduce(dbias, _t5_bucket(_swa_rel()).astype(jnp.int32))
    dproj0, dqg0, dkg0 = _post_even(proj0, dqr, dkr, dva, dga, dqn, dkn, dvb, cos_r, sin_r,
                                    p["swa_q_norm"], p["swa_k_norm"], tm=TM_WIDE)
    dx0, dnm0 = _matmul_nt_normbwd("in_even_bwd", dproj0, wf["w_in_even"], x, nm[0:1], dx1, tm=TM_WIDE)
    dw_in_even = _matmul_tn("in_even_dw", h0, dproj0, tk=1024, tn=1152, tt=1024)

    big = {
        "w_in_even": dw_in_even[None], "w_out_even": dw_out_even[None],
        "w_in_odd": dw_in_odd[None], "w_out_odd": dw_out_odd[None],
        "w_mlp_up": jnp.stack([dw_up0, dw_up1]), "w_mlp_down": jnp.stack([dw_down0, dw_down1]),
    }
    fold = lambda part: jnp.sum(part, axis=0)
    dlam = jnp.sum(dlog, axis=1).reshape(RET_HEADS, 2).T
    small = {
        "norm_mix": jnp.stack([fold(dnm0), fold(dnm1)]),
        "norm_mlp": jnp.stack([fold(dnl0), fold(dnl1)]),
        "ret_decay_logit": (dlam * (1.0 - jnp.exp(log_gamma)))[None],
        "ret_norm": fold(dretg)[None],
        "swa_q_norm": fold(dqg0)[None], "swa_k_norm": fold(dkg0)[None],
        "swa_sink": dsink[:, 0][None],
        "t5_table": dt5[:, :T5_BUCKETS].T,
        "ax_q_norm": fold(dqg1)[None], "ax_k_norm": fold(dkg1)[None],
    }
    return loss_part, dx0, big, small


BIG = ("w_in_even", "w_out_even", "w_in_odd", "w_out_odd", "w_mlp_up", "w_mlp_down")
SMALL = ("norm_mix", "norm_mlp", "ret_decay_logit", "ret_norm", "swa_q_norm", "swa_k_norm", "swa_sink", "t5_table",
         "ax_q_norm", "ax_k_norm")
WEIGHTS = ("norm_mix", "norm_mlp", "w_in_even", "w_out_even", "ret_decay_logit", "ret_norm", "swa_q_norm", "swa_k_norm",
           "swa_sink", "t5_table", "w_in_odd", "w_out_odd", "ax_q_norm", "ax_k_norm", "w_mlp_up", "w_mlp_down")
SHARD_AXIS = {"w_in_even": 2, "w_out_even": 1, "w_in_odd": 2, "w_out_odd": 1, "w_mlp_up": 2, "w_mlp_down": 1}
HALF_AXIS = {"w_in_even": 1, "w_out_even": 1, "w_in_odd": 1, "w_out_odd": 1, "w_mlp_up": 0, "w_mlp_down": 0}
N_CHIPS = 4
ANY = pl.BlockSpec(memory_space=pl.ANY)


def _mesh_pos():
    return lax.axis_index("x"), lax.axis_index("y"), lax.axis_index("c")


def _window(ref, axis, start, size):
    idx = [slice(None)] * len(ref.shape)
    idx[axis] = pl.ds(start, size)
    return ref.at[tuple(idx)]


def _cast_mxu(name, w, *, tr=256):
    R, C = w.shape
    tr = min(tr, R)

    def body(w_ref, o_ref):
        o_ref[...] = w_ref[...].astype(o_ref.dtype)

    return pl.pallas_call(body, name=name, grid=(R // tr,), in_specs=[pl.BlockSpec((tr, C), lambda i: (i, 0))],
                          out_specs=pl.BlockSpec((tr, C), lambda i: (i, 0)), out_shape=_sds((R, C), _MXU),
                          compiler_params=_params("parallel"))(w)


def _allgather_weights(shards):
    names = list(shards)
    n = len(names)
    sizes = [shards[k].shape[SHARD_AXIS[k]] for k in names]

    def body(*refs):
        ins, outs = refs[:n], refs[n:2 * n]
        local_sems, send_sems, recv_sems = refs[2 * n:]
        x, y, c = _mesh_pos()
        chips = [(1 - x, y), (x, 1 - y), (1 - x, 1 - y)]
        slot = lambda t, px, py: _window(outs[t], SHARD_AXIS[names[t]], pl.multiple_of((2 * px + py) * sizes[t], 128), sizes[t])
        local, remote = [], []
        for t in range(n):
            local.append(pltpu.make_async_copy(ins[t], slot(t, x, y), local_sems.at[t]))
            local[-1].start()
            for k, (px, py) in enumerate(chips):
                remote.append(pltpu.make_async_remote_copy(ins[t], slot(t, x, y), send_sems.at[3 * t + k], recv_sems.at[3 * t + k],
                                                           device_id=(px, py, c), device_id_type=MESH))
                remote[-1].start()
        for t in range(n):
            for k, (px, py) in enumerate(chips):
                pltpu.make_async_remote_copy(ins[t], slot(t, px, py), send_sems.at[3 * t + k], recv_sems.at[3 * t + k],
                                             device_id=(px, py, c), device_id_type=MESH).wait_recv()
        for cp in remote:
            cp.wait_send()
        for cp in local:
            cp.wait()

    full_shape = lambda k: tuple(d * (N_CHIPS if a == SHARD_AXIS[k] else 1) for a, d in enumerate(shards[k].shape))
    outs = pl.pallas_call(
        body, name="allgather_weights", in_specs=[ANY] * n, out_specs=[ANY] * n,
        out_shape=[_sds(full_shape(k), shards[k].dtype) for k in names],
        scratch_shapes=[pltpu.SemaphoreType.DMA((n,)), pltpu.SemaphoreType.DMA((3 * n,)), pltpu.SemaphoreType.DMA((3 * n,))],
    )(*[shards[k] for k in names])
    return dict(zip(names, outs))


FLIPS = [(a, b, d) for a in (0, 1) for b in (0, 1) for d in (0, 1) if (a, b, d) != (0, 0, 0)]


def _flip(pos, f):
    return tuple(1 - p if fi else p for p, fi in zip(pos, f))


def _piece_shape(name, shape):
    out = list(shape)
    out[SHARD_AXIS[name]] //= N_CHIPS
    out[HALF_AXIS[name]] //= 2
    return tuple(out)


def _piece(ref, name, chip, core, shard_size, half_size):
    sa, ha = SHARD_AXIS[name], HALF_AXIS[name]
    if sa == ha:
        return _window(ref, sa, pl.multiple_of(chip * shard_size + core * half_size, 8), half_size)
    half_start = pl.multiple_of(core * half_size, 8) if ha else core * half_size
    return _window(_window(ref, sa, pl.multiple_of(chip * shard_size, 128), shard_size), ha, half_start, half_size)


def _scatter_gradients(grads):
    names = list(grads)
    n = len(names)
    n_peer = len(FLIPS)
    pieces = [_piece_shape(k, grads[k].shape) for k in names]
    shard_sizes = [grads[k].shape[SHARD_AXIS[k]] // N_CHIPS for k in names]
    half_sizes = [p[HALF_AXIS[k]] for k, p in zip(names, pieces)]

    def body(*refs):
        ins, outs = refs[:n], refs[n:2 * n]
        local_sems, send_sems, recv_sems = refs[2 * n:]
        pos = _mesh_pos()
        ident = lambda p: 4 * p[0] + 2 * p[1] + p[2]
        me = ident(pos)
        src = lambda t, p: _piece(ins[t], names[t], 2 * p[0] + p[1], p[2], shard_sizes[t], half_sizes[t])
        local, remote = [], []
        for t in range(n):
            local.append(pltpu.make_async_copy(src(t, pos), outs[t].at[me], local_sems.at[t]))
            local[-1].start()
            for k, f in enumerate(FLIPS):
                peer = _flip(pos, f)
                remote.append(pltpu.make_async_remote_copy(src(t, peer), outs[t].at[me], send_sems.at[n_peer * t + k],
                                                           recv_sems.at[n_peer * t + k], device_id=peer, device_id_type=MESH))
                remote[-1].start()
        for t in range(n):
            for k, f in enumerate(FLIPS):
                peer = _flip(pos, f)
                pltpu.make_async_remote_copy(src(t, pos), outs[t].at[ident(peer)], send_sems.at[n_peer * t + k],
                                             recv_sems.at[n_peer * t + k], device_id=peer, device_id_type=MESH).wait_recv()
        for cp in remote:
            cp.wait_send()
        for cp in local:
            cp.wait()

    outs = pl.pallas_call(
        body, name="scatter_gradients", in_specs=[ANY] * n, out_specs=[ANY] * n,
        out_shape=[_sds((N_DEV,) + p, F32) for p in pieces],
        scratch_shapes=[pltpu.SemaphoreType.DMA((n,)), pltpu.SemaphoreType.DMA((n_peer * n,)), pltpu.SemaphoreType.DMA((n_peer * n,))],
    )(*[grads[k] for k in names])
    return dict(zip(names, outs))


def _sum_slots(name, buf, *, tr=128):
    _, L, R, C = buf.shape
    tr = min(tr, R)

    def body(b_ref, o_ref):
        acc = b_ref[0]
        for s in range(1, N_DEV):
            acc = acc + b_ref[s]
        o_ref[...] = acc

    return pl.pallas_call(body, name=name, grid=(L, R // tr),
                          in_specs=[pl.BlockSpec((N_DEV, 1, tr, C), lambda l, i: (0, l, i, 0))],
                          out_specs=pl.BlockSpec((1, tr, C), lambda l, i: (l, i, 0)), out_shape=_sds((L, R, C), F32),
                          compiler_params=_params("parallel", "parallel"))(buf)


def _exchange_halves(halves, shard_shapes):
    names = list(halves)
    n = len(names)
    half_sizes = [halves[k].shape[HALF_AXIS[k]] for k in names]

    def body(*refs):
        ins, outs = refs[:n], refs[n:2 * n]
        local_sems, send_sems, recv_sems = refs[2 * n:]
        x, y, c = _mesh_pos()
        half = lambda t, core: _window(outs[t], HALF_AXIS[names[t]], pl.multiple_of(core * half_sizes[t], 8)
                                       if HALF_AXIS[names[t]] else core * half_sizes[t], half_sizes[t])
        local, remote = [], []
        for t in range(n):
            local.append(pltpu.make_async_copy(ins[t], half(t, c), local_sems.at[t]))
            local[-1].start()
            remote.append(pltpu.make_async_remote_copy(ins[t], half(t, c), send_sems.at[t], recv_sems.at[t],
                                                       device_id=(x, y, 1 - c), device_id_type=MESH))
            remote[-1].start()
        for t in range(n):
            pltpu.make_async_remote_copy(ins[t], half(t, 1 - c), send_sems.at[t], recv_sems.at[t],
                                         device_id=(x, y, 1 - c), device_id_type=MESH).wait_recv()
        for cp in remote:
            cp.wait_send()
        for cp in local:
            cp.wait()

    outs = pl.pallas_call(
        body, name="exchange_halves", in_specs=[ANY] * n, out_specs=[ANY] * n,
        out_shape=[_sds(shard_shapes[k], F32) for k in names],
        scratch_shapes=[pltpu.SemaphoreType.DMA((n,)), pltpu.SemaphoreType.DMA((n,)), pltpu.SemaphoreType.DMA((n,))],
    )(*[halves[k] for k in names])
    return dict(zip(names, outs))


def _adamw_math(w, g, m, v):
    m = ADAM_B1 * m + (1.0 - ADAM_B1) * g
    v = ADAM_B2 * v + (1.0 - ADAM_B2) * jnp.square(g)
    m_hat = m / (1.0 - ADAM_B1 ** ADAM_STEP)
    v_hat = v / (1.0 - ADAM_B2 ** ADAM_STEP)
    return -ADAM_LR * (m_hat / (jnp.sqrt(v_hat) + ADAM_EPS) + ADAM_WD * w), m, v


def _adamw(name, w, g, m, v, *, tr=256):
    R, C = w.shape
    tr = min(tr, R)

    def body(w_ref, g_ref, m_ref, v_ref, d_ref, mo_ref, vo_ref):
        d_ref[...], mo_ref[...], vo_ref[...] = _adamw_math(w_ref[...], g_ref[...], m_ref[...], v_ref[...])

    spec = pl.BlockSpec((tr, C), lambda i: (i, 0))
    return pl.pallas_call(body, name=name, grid=(R // tr,), in_specs=[spec] * 4, out_specs=[spec] * 3,
                          out_shape=[_sds((R, C), F32)] * 3, compiler_params=_params("parallel"))(w, g, m, v)


SLAB_ROWS = 8
LOSS_ROW = 7


def _pack_small(d):
    pad = lambda a, width: jnp.pad(a.reshape(-1), (0, width - a.size))
    row5 = jnp.concatenate([d["swa_q_norm"].reshape(-1), d["swa_k_norm"].reshape(-1), d["ax_q_norm"].reshape(-1),
                            d["ax_k_norm"].reshape(-1), pad(d["swa_sink"], HEAD_DIM), pad(d["ret_decay_logit"], HEAD_DIM),
                            jnp.zeros((2 * HEAD_DIM,), F32)])
    return jnp.concatenate([d["norm_mix"], d["norm_mlp"], d["ret_norm"], row5[None], pad(d["t5_table"], D_MODEL)[None],
                            jnp.zeros((1, D_MODEL), F32)], axis=0)


def _unpack_small(slab):
    r5 = slab[5]
    return {
        "norm_mix": slab[0:2], "norm_mlp": slab[2:4], "ret_norm": slab[4:5],
        "swa_q_norm": r5[None, 0:128], "swa_k_norm": r5[None, 128:256], "ax_q_norm": r5[None, 256:384],
        "ax_k_norm": r5[None, 384:512], "swa_sink": r5[None, 512:512 + SWA_HEADS],
        "ret_decay_logit": r5[640:640 + 2 * RET_HEADS].reshape(1, 2, RET_HEADS),
        "t5_table": slab[6, :T5_BUCKETS * SWA_HEADS].reshape(T5_BUCKETS, SWA_HEADS),
    }


def _small_allreduce_adamw(g_slab, w_slab, m_slab, v_slab, loss_part):
    def body(g_ref, w_ref, m_ref, v_ref, lp_ref, go_ref, d_ref, mo_ref, vo_ref, gath, send_sems, recv_sems):
        pos = _mesh_pos()
        ident = lambda p: 4 * p[0] + 2 * p[1] + p[2]
        me = ident(pos)
        row = lax.broadcasted_iota(jnp.int32, (SLAB_ROWS, D_MODEL), 0)
        lane = lax.broadcasted_iota(jnp.int32, (SLAB_ROWS, D_MODEL), 1)
        loss = jnp.sum(jnp.sum(lp_ref[...], axis=0, keepdims=True), axis=1, keepdims=True) * (0.5 / D_MODEL)
        gath[me] = jnp.where(jnp.logical_and(row == LOSS_ROW, lane == 0), loss, g_ref[...])
        sends = []
        for k, f in enumerate(FLIPS):
            sends.append(pltpu.make_async_remote_copy(gath.at[me], gath.at[me], send_sems.at[k], recv_sems.at[k],
                                                      device_id=_flip(pos, f), device_id_type=MESH))
            sends[-1].start()
        for k, f in enumerate(FLIPS):
            peer = _flip(pos, f)
            pltpu.make_async_remote_copy(gath.at[me], gath.at[ident(peer)], send_sems.at[k], recv_sems.at[k],
                                         device_id=peer, device_id_type=MESH).wait_recv()
        for cp in sends:
            cp.wait_send()
        total = gath[0]
        for s in range(1, N_DEV):
            total = total + gath[s]
        go_ref[...] = total
        d_ref[...], mo_ref[...], vo_ref[...] = _adamw_math(w_ref[...], total, m_ref[...], v_ref[...])

    vmem = pl.BlockSpec(memory_space=pltpu.VMEM)
    return pl.pallas_call(
        body, name="small_allreduce_adamw", in_specs=[vmem] * 5, out_specs=[vmem] * 4,
        out_shape=[_sds((SLAB_ROWS, D_MODEL), F32)] * 4,
        scratch_shapes=[pltpu.VMEM((N_DEV, SLAB_ROWS, D_MODEL), F32),
                        pltpu.SemaphoreType.DMA((len(FLIPS),)), pltpu.SemaphoreType.DMA((len(FLIPS),))],
    )(g_slab, w_slab, m_slab, v_slab, loss_part)


def kernel(x, norm_mix, norm_mlp, w_in_even, w_out_even, ret_decay_logit, ret_norm, swa_q_norm, swa_k_norm, swa_sink, t5_table, w_in_odd, w_out_odd, ax_q_norm, ax_k_norm, w_mlp_up, w_mlp_down, loss_target, m_norm_mix, m_norm_mlp, m_w_in_even, m_w_out_even, m_ret_decay_logit, m_ret_norm, m_swa_q_norm, m_swa_k_norm, m_swa_sink, m_t5_table, m_w_in_odd, m_w_out_odd, m_ax_q_norm, m_ax_k_norm, m_w_mlp_up, m_w_mlp_down, v_norm_mix, v_norm_mlp, v_w_in_even, v_w_out_even, v_ret_decay_logit, v_ret_norm, v_swa_q_norm, v_swa_k_norm, v_swa_sink, v_t5_table, v_w_in_odd, v_w_out_odd, v_ax_q_norm, v_ax_k_norm, v_w_mlp_up, v_w_mlp_down):
    w = dict(zip(WEIGHTS, (norm_mix, norm_mlp, w_in_even, w_out_even, ret_decay_logit, ret_norm, swa_q_norm, swa_k_norm,
                           swa_sink, t5_table, w_in_odd, w_out_odd, ax_q_norm, ax_k_norm, w_mlp_up, w_mlp_down)))
    m = dict(zip(WEIGHTS, (m_norm_mix, m_norm_mlp, m_w_in_even, m_w_out_even, m_ret_decay_logit, m_ret_norm, m_swa_q_norm,
                           m_swa_k_norm, m_swa_sink, m_t5_table, m_w_in_odd, m_w_out_odd, m_ax_q_norm, m_ax_k_norm,
                           m_w_mlp_up, m_w_mlp_down)))
    v = dict(zip(WEIGHTS, (v_norm_mix, v_norm_mlp, v_w_in_even, v_w_out_even, v_ret_decay_logit, v_ret_norm, v_swa_q_norm,
                           v_swa_k_norm, v_swa_sink, v_t5_table, v_w_in_odd, v_w_out_odd, v_ax_q_norm, v_ax_k_norm,
                           v_w_mlp_up, v_w_mlp_down)))
    flat = lambda a: a.reshape(-1, a.shape[-1])

    shards = {k: _cast_mxu(f"cast_{k}", flat(w[k])).reshape(w[k].shape) for k in BIG}
    whole = _allgather_weights(shards)
    wf = {k: (whole[k] if k.startswith("w_mlp") else whole[k][0]) for k in BIG}

    loss_part, dx, big_g, small_g = _local_step(x[0], loss_target[0], {k: w[k] for k in SMALL}, wf)

    slots = _scatter_gradients(big_g)
    halves = {k: _sum_slots(f"sum_{k}", slots[k]) for k in BIG}
    grad = _exchange_halves(halves, {k: w[k].shape for k in BIG})
    delta, new_m, new_v = {}, {}, {}
    for k in BIG:
        d_k, m_k, v_k = _adamw(f"adamw_{k}", flat(w[k]), flat(grad[k]), flat(m[k]), flat(v[k]))
        delta[k], new_m[k], new_v[k] = d_k.reshape(w[k].shape), m_k.reshape(w[k].shape), v_k.reshape(w[k].shape)

    slabs = _small_allreduce_adamw(_pack_small(small_g), _pack_small({k: w[k] for k in SMALL}),
                                   _pack_small({k: m[k] for k in SMALL}), _pack_small({k: v[k] for k in SMALL}), loss_part)
    loss = slabs[0][LOSS_ROW, 0]
    for out, slab in zip((grad, delta, new_m, new_v), slabs):
        out.update(_unpack_small(slab))

    return (loss, dx[None], *[grad[k] for k in WEIGHTS], *[delta[k] for k in WEIGHTS],
            *[new_m[k] for k in WEIGHTS], *[new_v[k] for k in WEIGHTS])
```

```python
import functools
import math

import jax
import jax.numpy as jnp
from jax import lax
from jax.experimental import pallas as pl
from jax.experimental.pallas import tpu as pltpu

F32 = jnp.float32
BF16 = jnp.bfloat16
_MXU = BF16
_WIRE = BF16

D_MODEL = 1024
HEAD_DIM = 128
EPS = 1e-6
NEG_INF = -1e30
CHUNK = 128
GRID_W = 64
RET_HEADS, RET_DK, RET_DV = 4, 128, 256
RET_Q, RET_V = RET_HEADS * RET_DK, RET_HEADS * RET_DV
RET_THETA = 10000.0
SWA_HEADS, SWA_KV_HEADS = 8, 2
T5_BUCKETS, T5_MAX_DIST = 32, 128
AX_HEADS, AX_KV_HEADS = 8, 2
AX_THETA = 10000.0
D_FF = 4 * D_MODEL
EVEN_IN = 2 * RET_Q + 2 * RET_V + D_MODEL + 2 * SWA_KV_HEADS * HEAD_DIM
ODD_IN = D_MODEL + 2 * AX_KV_HEADS * HEAD_DIM
ATT_SCALE = HEAD_DIM ** -0.5

ADAM_LR, ADAM_B1, ADAM_B2, ADAM_EPS, ADAM_WD, ADAM_STEP = 0.001, 0.9, 0.999, 1e-08, 0.01, 10

N_DEV = 8
VMEM_LIMIT_BYTES = 56 << 20
MESH = pl.DeviceIdType.MESH

_NN = (((1,), (0,)), ((), ()))
_NT = (((1,), (1,)), ((), ()))
_TN = (((0,), (0,)), ((), ()))


def _dot(a, b, dn=_NN):
    return lax.dot_general(a.astype(_MXU), b.astype(_MXU), dn, preferred_element_type=F32)


def _params(*sem):
    return pltpu.CompilerParams(dimension_semantics=sem, vmem_limit_bytes=VMEM_LIMIT_BYTES)


def _sds(shape, dtype):
    return jax.ShapeDtypeStruct(tuple(shape), dtype)


def _rowsum8(x):
    return jnp.sum(x.reshape(x.shape[0] // 8, 8, x.shape[1]), axis=0)


def _swap_halves(x, half):
    width = x.shape[1]
    lane = lax.broadcasted_iota(jnp.int32, x.shape, 1)
    up = pltpu.roll(x, width - half, axis=1)
    down = pltpu.roll(x, half, axis=1)
    return jnp.where((lane & (2 * half - 1)) < half, up, down)


def _sigmoid(x):
    return 1.0 / (1.0 + jnp.exp(-x))


def _norm_matmul(name, x, gain, w, *, tm, tn, out_dtype):
    T, K = x.shape
    N = w.shape[1]
    tm, tn = min(tm, T), min(tn, N)

    def body(x_ref, g_ref, w_ref, y_ref, h_ref, h_sc):
        @pl.when(pl.program_id(1) == 0)
        def _():
            xv = x_ref[...]
            r = lax.rsqrt(jnp.mean(xv * xv, axis=-1, keepdims=True) + EPS)
            h = (xv * r * g_ref[...]).astype(_MXU)
            h_sc[...] = h
            h_ref[...] = h
        y_ref[...] = jnp.dot(h_sc[...], w_ref[...], preferred_element_type=F32).astype(y_ref.dtype)

    return pl.pallas_call(
        body, name=name, grid=(T // tm, N // tn),
        in_specs=[pl.BlockSpec((tm, K), lambda i, j: (i, 0)),
                  pl.BlockSpec((1, K), lambda i, j: (0, 0)),
                  pl.BlockSpec((K, tn), lambda i, j: (0, j))],
        out_specs=[pl.BlockSpec((tm, tn), lambda i, j: (i, j)),
                   pl.BlockSpec((tm, K), lambda i, j: (i, 0))],
        out_shape=[_sds((T, N), out_dtype), _sds((T, K), _MXU)],
        scratch_shapes=[pltpu.VMEM((tm, K), _MXU)],
        compiler_params=_params("parallel", "arbitrary"),
    )(x, gain, w)


def _matmul_res(name, a_list, w, res, *, tm, relu2=False, target=None):
    T = res.shape[0]
    N = w.shape[1]
    K = a_list[0].shape[1]
    n_a = len(a_list)
    tm = min(tm, T)
    with_loss = target is not None

    def body(*refs):
        a_refs = refs[:n_a]
        w_refs = refs[n_a:2 * n_a]
        res_ref = refs[2 * n_a]
        acc = res_ref[...]
        for a_ref, w_ref in zip(a_refs, w_refs):
            a = a_ref[...]
            if relu2:
                a = jnp.square(jnp.maximum(a.astype(F32), 0.0))
            acc = acc + _dot(a, w_ref[...])
        if with_loss:
            tgt_ref, g_ref, loss_ref = refs[2 * n_a + 1:]
            diff = acc - tgt_ref[...]
            g_ref[...] = diff * (1.0 / N)

            @pl.when(pl.program_id(0) == 0)
            def _():
                loss_ref[...] = jnp.zeros_like(loss_ref)
            loss_ref[...] += _rowsum8(diff * diff)
        else:
            refs[2 * n_a + 1][...] = acc

    row = lambda i: (i, 0)
    in_specs = [pl.BlockSpec((tm, K), row) for _ in a_list]
    in_specs += [pl.BlockSpec((K, N), functools.partial(lambda i, b: (b, 0), b=b)) for b in range(n_a)]
    in_specs += [pl.BlockSpec((tm, N), row)]
    args = list(a_list) + [w] * n_a + [res]
    if with_loss:
        in_specs.append(pl.BlockSpec((tm, N), row))
        args.append(target)
        out_specs = [pl.BlockSpec((tm, N), row), pl.BlockSpec((8, N), lambda i: (0, 0))]
        out_shape = [_sds((T, N), F32), _sds((8, N), F32)]
        sem = "arbitrary"
    else:
        out_specs = pl.BlockSpec((tm, N), row)
        out_shape = _sds((T, N), F32)
        sem = "parallel"
    return pl.pallas_call(body, name=name, grid=(T // tm,), in_specs=in_specs, out_specs=out_specs,
                          out_shape=out_shape, compiler_params=_params(sem))(*args)


def _matmul_nt(name, a, w, *, tm, tn, out_dtype, relu_of=None):
    T, K = a.shape
    N = w.shape[0]
    tm, tn = min(tm, T), min(tn, N)

    def body(*refs):
        if relu_of is None:
            a_ref, w_ref, o_ref = refs
            o_ref[...] = _dot(a_ref[...], w_ref[...], _NT).astype(o_ref.dtype)
        else:
            a_ref, w_ref, u_ref, o_ref = refs
            da = _dot(a_ref[...], w_ref[...], _NT)
            o_ref[...] = (da * (2.0 * jnp.maximum(u_ref[...].astype(F32), 0.0))).astype(o_ref.dtype)

    in_specs = [pl.BlockSpec((tm, K), lambda i, j: (i, 0)), pl.BlockSpec((tn, K), lambda i, j: (j, 0))]
    args = [a, w]
    if relu_of is not None:
        in_specs.append(pl.BlockSpec((tm, tn), lambda i, j: (i, j)))
        args.append(relu_of)
    return pl.pallas_call(body, name=name, grid=(T // tm, N // tn), in_specs=in_specs,
                          out_specs=pl.BlockSpec((tm, tn), lambda i, j: (i, j)),
                          out_shape=_sds((T, N), out_dtype),
                          compiler_params=_params("parallel", "parallel"))(*args)


def _matmul_nt_normbwd(name, dy, w, x, gain, dres, *, tm):
    T, K = dy.shape
    N = w.shape[0]
    tm = min(tm, T)

    def body(dy_ref, w_ref, x_ref, g_ref, dres_ref, dx_ref, dg_ref):
        dh = _dot(dy_ref[...], w_ref[...], _NT)
        xv = x_ref[...]
        r = lax.rsqrt(jnp.mean(xv * xv, axis=-1, keepdims=True) + EPS)
        xhat = xv * r
        dxhat = dh * g_ref[...]
        dx_ref[...] = dres_ref[...] + r * (dxhat - xhat * jnp.mean(dxhat * xhat, axis=-1, keepdims=True))

        @pl.when(pl.program_id(0) == 0)
        def _():
            dg_ref[...] = jnp.zeros_like(dg_ref)
        dg_ref[...] += _rowsum8(dh * xhat)

    row = lambda i: (i, 0)
    return pl.pallas_call(
        body, name=name, grid=(T // tm,),
        in_specs=[pl.BlockSpec((tm, K), row), pl.BlockSpec((N, K), lambda i: (0, 0)),
                  pl.BlockSpec((tm, N), row), pl.BlockSpec((1, N), lambda i: (0, 0)), pl.BlockSpec((tm, N), row)],
        out_specs=[pl.BlockSpec((tm, N), row), pl.BlockSpec((8, N), lambda i: (0, 0))],
        out_shape=[_sds((T, N), F32), _sds((8, N), F32)],
        compiler_params=_params("arbitrary"),
    )(dy, w, x, gain, dres)


def _matmul_tn(name, a, b, *, tk, tn, tt, out_dtype, relu2=False):
    T, Ka = a.shape
    Nb = b.shape[1]
    tk, tn, tt = min(tk, Ka), min(tn, Nb), min(tt, T)
    nt = T // tt

    def body(a_ref, b_ref, o_ref, acc):
        t = pl.program_id(2)

        @pl.when(t == 0)
        def _():
            acc[...] = jnp.zeros_like(acc)
        av = a_ref[...]
        if relu2:
            av = jnp.square(jnp.maximum(av.astype(F32), 0.0))
        acc[...] += _dot(av, b_ref[...], _TN)

        @pl.when(t == nt - 1)
        def _():
            o_ref[...] = acc[...].astype(o_ref.dtype)

    return pl.pallas_call(
        body, name=name, grid=(Ka // tk, Nb // tn, nt),
        in_specs=[pl.BlockSpec((tt, tk), lambda i, j, t: (t, i)), pl.BlockSpec((tt, tn), lambda i, j, t: (t, j))],
        out_specs=pl.BlockSpec((tk, tn), lambda i, j, t: (i, j)),
        out_shape=_sds((Ka, Nb), out_dtype),
        scratch_shapes=[pltpu.VMEM((tk, tn), F32)],
        compiler_params=_params("parallel", "parallel", "arbitrary"),
    )(a, b)


def _rope_angles(pos, dim, theta):
    inv = theta ** (-jnp.arange(0, dim, 2, dtype=F32) / dim)
    return pos.astype(F32)[:, None] * inv[None, :]


def _ret_rope_tables(T):
    ang = _rope_angles(jnp.arange(T), RET_DK, RET_THETA)
    c, s = jnp.cos(ang), jnp.sin(ang)
    return jnp.concatenate([c, c], axis=1), jnp.concatenate([-s, s], axis=1)


def _axial_rope_tables(T):
    rows = T // GRID_W
    row = jnp.repeat(jnp.arange(rows), GRID_W)
    col = jnp.tile(jnp.arange(GRID_W), rows)
    ar = _rope_angles(row, HEAD_DIM // 2, AX_THETA)
    ac = _rope_angles(col, HEAD_DIM // 2, AX_THETA)
    cos = jnp.concatenate([jnp.cos(ar), jnp.cos(ar), jnp.cos(ac), jnp.cos(ac)], axis=1)
    sin = jnp.concatenate([-jnp.sin(ar), jnp.sin(ar), -jnp.sin(ac), jnp.sin(ac)], axis=1)
    return cos, sin


(TAB_D, TAB_DT, TAB_EF, TAB_EB, TAB_A, TAB_B, TAB_CF, TAB_CB,
 TAB_RA, TAB_RB, TAB_RCF, TAB_RCB, TAB_KF, TAB_KB) = range(14)


def _retention_tables(decay_logit):
    lg = jax.nn.log_sigmoid(decay_logit.astype(F32))
    lam, mu = lg[0][:, None, None], lg[1][:, None, None]
    idx = jnp.arange(CHUNK, dtype=F32)
    diff = (idx[:, None] - idx[None, :])[None]
    df = jnp.where(diff >= 0, jnp.exp(jnp.maximum(diff, 0.0) * lam), 0.0)
    db = jnp.where(diff < 0, jnp.exp(jnp.maximum(-diff, 0.0) * mu), 0.0)
    d = df + db
    r = idx[None, :, None]
    ones = jnp.ones((1, 1, CHUNK), F32)
    a = jnp.exp((r + 1.0) * lam) * ones
    b = jnp.exp((CHUNK - r) * mu) * ones
    cf = jnp.exp((CHUNK - 1.0 - r) * lam) * ones
    cb = jnp.exp(r * mu) * ones
    full = jnp.ones((1, CHUNK, CHUNK), F32)
    kf = CHUNK * jnp.exp(CHUNK * lam) * full
    kb = CHUNK * jnp.exp(CHUNK * mu) * full
    tabs = jnp.stack([d, jnp.swapaxes(d, 1, 2), diff * df, -diff * db, a, b, cf, cb,
                      (r + 1.0) * a, (CHUNK - r) * b, (CHUNK - 1.0 - r) * cf, r * cb, kf, kb], axis=1)

    def lanes(tab):
        return jnp.transpose(tab, (1, 0, 2)).reshape(CHUNK, RET_HEADS * CHUNK)

    def dec(l):
        return jnp.exp(CHUNK * l)[:, 0, :] * jnp.ones((1, RET_DV), F32)

    weights = dict(a=lanes(a), b=lanes(b), cf=lanes(cf), cb=lanes(cb), dec_f=dec(lam), dec_b=dec(mu))
    return tabs, weights, lg


def _t5_bucket(rel):
    nb = T5_BUCKETS // 2
    max_exact = nb // 2
    ret = jnp.where(rel > 0, nb, 0)
    n = jnp.abs(rel)
    nf = jnp.maximum(n, 1).astype(F32)
    large = max_exact + (jnp.log(nf / max_exact) / math.log(T5_MAX_DIST / max_exact)
                         * (nb - max_exact)).astype(jnp.int32)
    large = jnp.minimum(large, nb - 1)
    return ret + jnp.where(n < max_exact, n, large)


def _swa_rel():
    r = jnp.arange(CHUNK)
    j = jnp.arange(3 * CHUNK)
    return j[None, :] - CHUNK - r[:, None]


def _swa_bias(t5_table):
    rel = _swa_rel()
    bucket = jnp.where(jnp.abs(rel) <= CHUNK, _t5_bucket(rel), -1).astype(jnp.int32)

    def body(tab_ref, bk_ref, o_ref):
        bk = bk_ref[...]
        for h in range(SWA_HEADS):
            pick = lambda b, acc, h=h: jnp.where(bk == b, tab_ref[b, h], acc)
            o_ref[h] = lax.fori_loop(0, T5_BUCKETS, pick, jnp.full(bk.shape, NEG_INF, F32))

    return pl.pallas_call(
        body, name="t5_bias",
        in_specs=[pl.BlockSpec(memory_space=pltpu.SMEM), pl.BlockSpec(memory_space=pltpu.VMEM)],
        out_specs=pl.BlockSpec(memory_space=pltpu.VMEM),
        out_shape=_sds((SWA_HEADS, CHUNK, 3 * CHUNK), F32),
    )(t5_table.astype(F32), bucket)


def _prep_even(proj, cos, sin, q_gain, k_gain, *, tm):
    T = proj.shape[0]
    tm = min(tm, T)

    def body(qa_ref, ka_ref, qb_ref, kb_ref, c_ref, s_ref, qg_ref, kg_ref, qr_ref, kr_ref, qn_ref, kn_ref):
        c = jnp.concatenate([c_ref[...]] * RET_HEADS, axis=1)
        s = jnp.concatenate([s_ref[...]] * RET_HEADS, axis=1)
        qa = qa_ref[...]
        qr_ref[...] = (qa * c + _swap_halves(qa, RET_DK // 2) * s).astype(qr_ref.dtype)
        ka = ka_ref[...]
        kr_ref[...] = ((ka * c + _swap_halves(ka, RET_DK // 2) * s) * (RET_DK ** -0.5)).astype(kr_ref.dtype)
        for src, gain, dst, heads in ((qb_ref, qg_ref, qn_ref, SWA_HEADS), (kb_ref, kg_ref, kn_ref, SWA_KV_HEADS)):
            for h in range(heads):
                sl = slice(h * HEAD_DIM, (h + 1) * HEAD_DIM)
                xh = src[:, sl]
                r = lax.rsqrt(jnp.mean(xh * xh, axis=-1, keepdims=True) + EPS)
                dst[:, sl] = (xh * r * gain[...]).astype(dst.dtype)

    row = lambda i: (i, 0)
    const = lambda i: (0, 0)
    return pl.pallas_call(
        body, name="prep_even", grid=(T // tm,),
        in_specs=[pl.BlockSpec((tm, RET_Q), lambda i: (i, 0)), pl.BlockSpec((tm, RET_Q), lambda i: (i, 1)),
                  pl.BlockSpec((tm, D_MODEL), lambda i: (i, 3)), pl.BlockSpec((tm, 256), lambda i: (i, 16)),
                  pl.BlockSpec((tm, RET_DK), row), pl.BlockSpec((tm, RET_DK), row),
                  pl.BlockSpec((1, HEAD_DIM), const), pl.BlockSpec((1, HEAD_DIM), const)],
        out_specs=[pl.BlockSpec((tm, RET_Q), row), pl.BlockSpec((tm, RET_Q), row),
                   pl.BlockSpec((tm, D_MODEL), row), pl.BlockSpec((tm, 256), row)],
        out_shape=[_sds((T, RET_Q), _MXU), _sds((T, RET_Q), _MXU), _sds((T, D_MODEL), _MXU), _sds((T, 256), _MXU)],
        compiler_params=_params("parallel"),
    )(proj, proj, proj, proj, cos, sin, q_gain, k_gain)


def _ret_scan(name, x, y, y_col, w_asc, dec_asc, w_desc, dec_desc):
    T = x.shape[0]
    nc = T // CHUNK

    def body(xa_ref, ya_ref, xd_ref, yd_ref, wa_ref, da_ref, wd_ref, dd_ref, sa_out, sd_out, sa, sd):
        @pl.when(pl.program_id(0) == 0)
        def _():
            sa[...] = jnp.zeros_like(sa)
            sd[...] = jnp.zeros_like(sd)
        sa_out[0] = sa[...].astype(sa_out.dtype)
        sd_out[0] = sd[...].astype(sd_out.dtype)
        for x_ref, y_ref, w_ref, d_ref, st in ((xa_ref, ya_ref, wa_ref, da_ref, sa), (xd_ref, yd_ref, wd_ref, dd_ref, sd)):
            for h in range(RET_HEADS):
                ks = slice(h * RET_DK, (h + 1) * RET_DK)
                vs = slice(h * RET_DV, (h + 1) * RET_DV)
                u = _dot(x_ref[:, ks].astype(F32) * w_ref[:, ks], y_ref[:, vs], _TN)
                st[ks, :] = st[ks, :] * d_ref[h:h + 1, :] + u

    asc = lambda i: (i, 0)
    desc = lambda i: (nc - 1 - i, 0)
    const = lambda i: (0, 0)
    return pl.pallas_call(
        body, name=name, grid=(nc,),
        in_specs=[pl.BlockSpec((CHUNK, RET_Q), asc), pl.BlockSpec((CHUNK, RET_V), lambda i: (i, y_col)),
                  pl.BlockSpec((CHUNK, RET_Q), desc), pl.BlockSpec((CHUNK, RET_V), lambda i: (nc - 1 - i, y_col)),
                  pl.BlockSpec((CHUNK, RET_Q), const), pl.BlockSpec((RET_HEADS, RET_DV), const),
                  pl.BlockSpec((CHUNK, RET_Q), const), pl.BlockSpec((RET_HEADS, RET_DV), const)],
        out_specs=[pl.BlockSpec((1, RET_Q, RET_DV), lambda i: (i, 0, 0)),
                   pl.BlockSpec((1, RET_Q, RET_DV), lambda i: (nc - 1 - i, 0, 0))],
        out_shape=[_sds((nc, RET_Q, RET_DV), _MXU), _sds((nc, RET_Q, RET_DV), _MXU)],
        scratch_shapes=[pltpu.VMEM((RET_Q, RET_DV), F32), pltpu.VMEM((RET_Q, RET_DV), F32)],
        compiler_params=_params("arbitrary"),
    )(x, y, x, y, w_asc, dec_asc, w_desc, dec_desc)


def _ret_out(qr, kr, proj, sf, sb, tabs, gain):
    T = qr.shape[0]
    nc = T // CHUNK

    def body(q_ref, k_ref, v_ref, g_ref, sf_ref, sb_ref, tab_ref, gain_ref, o_ref, y_ref):
        for h in range(RET_HEADS):
            ks = slice(h * RET_DK, (h + 1) * RET_DK)
            vs = slice(h * RET_DV, (h + 1) * RET_DV)
            q, k, v = q_ref[:, ks], k_ref[:, ks], v_ref[:, vs]
            qf = q.astype(F32)
            a_mat = _dot(q, k, _NT) * tab_ref[h, 0]
            o = (_dot(a_mat, v) + _dot(qf * tab_ref[h, 1], sf_ref[0, ks, :]) + _dot(qf * tab_ref[h, 2], sb_ref[0, ks, :]))
            o_ref[:, vs] = o
            r = lax.rsqrt(jnp.mean(o * o, axis=-1, keepdims=True) + EPS)
            g = g_ref[:, vs]
            y_ref[:, vs] = (g * _sigmoid(g) * (o * r * gain_ref[:, vs])).astype(y_ref.dtype)

    row = lambda i: (i, 0)
    return pl.pallas_call(
        body, name="ret_out", grid=(nc,),
        in_specs=[pl.BlockSpec((CHUNK, RET_Q), row), pl.BlockSpec((CHUNK, RET_Q), row),
                  pl.BlockSpec((CHUNK, RET_V), lambda i: (i, 1)), pl.BlockSpec((CHUNK, RET_V), lambda i: (i, 2)),
                  pl.BlockSpec((1, RET_Q, RET_DV), lambda i: (i, 0, 0)), pl.BlockSpec((1, RET_Q, RET_DV), lambda i: (i, 0, 0)),
                  pl.BlockSpec((RET_HEADS, 3, CHUNK, CHUNK), lambda i: (0, 0, 0, 0)),
                  pl.BlockSpec((1, RET_V), lambda i: (0, 0))],
        out_specs=[pl.BlockSpec((CHUNK, RET_V), row), pl.BlockSpec((CHUNK, RET_V), row)],
        out_shape=[_sds((T, RET_V), F32), _sds((T, RET_V), _MXU)],
        compiler_params=_params("parallel"),
    )(qr, kr, proj, proj, sf, sb, tabs, gain)


def _ret_gate_bwd(dycat, proj, ret_o, gain, *, tm):
    T = ret_o.shape[0]
    tm = min(tm, T)

    def body(dy_ref, g_ref, o_ref, gain_ref, do_ref, dg_ref, dgain_ref):
        @pl.when(pl.program_id(0) == 0)
        def _():
            dgain_ref[...] = jnp.zeros_like(dgain_ref)
        for h in range(RET_HEADS):
            vs = slice(h * RET_DV, (h + 1) * RET_DV)
            o, g, dya, gn = o_ref[:, vs], g_ref[:, vs], dy_ref[:, vs], gain_ref[:, vs]
            r = lax.rsqrt(jnp.mean(o * o, axis=-1, keepdims=True) + EPS)
            ohat = o * r
            sg = _sigmoid(g)
            dy = dya * (g * sg)
            dg_ref[:, vs] = (dya * (ohat * gn) * (sg * (1.0 + g * (1.0 - sg)))).astype(dg_ref.dtype)
            dyg = dy * gn
            do_ref[:, vs] = (r * (dyg - ohat * jnp.mean(dyg * ohat, axis=-1, keepdims=True))).astype(do_ref.dtype)
            dgain_ref[:, vs] += _rowsum8(dy * ohat)

    row = lambda i: (i, 0)
    return pl.pallas_call(
        body, name="ret_gate_bwd", grid=(T // tm,),
        in_specs=[pl.BlockSpec((tm, RET_V), row), pl.BlockSpec((tm, RET_V), lambda i: (i, 2)),
                  pl.BlockSpec((tm, RET_V), row), pl.BlockSpec((1, RET_V), lambda i: (0, 0))],
        out_specs=[pl.BlockSpec((tm, RET_V), row), pl.BlockSpec((tm, RET_V), row), pl.BlockSpec((8, RET_V), lambda i: (0, 0))],
        out_shape=[_sds((T, RET_V), _MXU), _sds((T, RET_V), _MXU), _sds((8, RET_V), F32)],
        compiler_params=_params("arbitrary"),
    )(dycat, proj, ret_o, gain)


def _ret_bwd(qr, kr, proj, g_out, sf, sb, rf, rb, tabs):
    T = qr.shape[0]
    nc = T // CHUNK

    def body(q_ref, k_ref, v_ref, g_ref, sf_ref, sb_ref, rf_ref, rb_ref, tab_ref, dq_ref, dk_ref, dv_ref, dl_ref):
        @pl.when(pl.program_id(0) == 0)
        def _():
            dl_ref[...] = jnp.zeros_like(dl_ref)
        for h in range(RET_HEADS):
            ks = slice(h * RET_DK, (h + 1) * RET_DK)
            vs = slice(h * RET_DV, (h + 1) * RET_DV)
            q, k, v, g = q_ref[:, ks], k_ref[:, ks], v_ref[:, vs], g_ref[:, vs]
            s_f, s_b, r_f, r_b = sf_ref[0, ks, :], sb_ref[0, ks, :], rf_ref[0, ks, :], rb_ref[0, ks, :]
            tab = lambda t: tab_ref[h, t]
            qf, kf = q.astype(F32), k.astype(F32)
            qk = _dot(q, k, _NT)
            da_raw = _dot(g, v, _NT)
            x_f, x_b = _dot(g, s_f, _NT), _dot(g, s_b, _NT)
            dq_ref[:, ks] = _dot(da_raw * tab(TAB_D), k) + tab(TAB_A) * x_f + tab(TAB_B) * x_b
            at = _dot(k, q, _NT) * tab(TAB_DT)
            dat = _dot(v, g, _NT) * tab(TAB_DT)
            y_f, y_b = _dot(v, r_f, _NT), _dot(v, r_b, _NT)
            dk_ref[:, ks] = _dot(dat, q) + tab(TAB_CF) * y_f + tab(TAB_CB) * y_b
            dv_ref[:, vs] = (_dot(at, g) + _dot(kf * tab(TAB_CF), r_f) + _dot(kf * tab(TAB_CB), r_b)).astype(dv_ref.dtype)
            inner = da_raw * qk
            rs_f = r_f.astype(F32) * s_f.astype(F32)
            rs_b = r_b.astype(F32) * s_b.astype(F32)
            dl_f = (inner * tab(TAB_EF) + tab(TAB_RA) * qf * x_f + tab(TAB_RCF) * kf * y_f
                    + tab(TAB_KF) * (rs_f[:, :CHUNK] + rs_f[:, CHUNK:]))
            dl_b = (inner * tab(TAB_EB) + tab(TAB_RB) * qf * x_b + tab(TAB_RCB) * kf * y_b
                    + tab(TAB_KB) * (rs_b[:, :CHUNK] + rs_b[:, CHUNK:]))
            dl_ref[2 * h:2 * h + 1, :] += jnp.sum(dl_f, axis=0, keepdims=True)
            dl_ref[2 * h + 1:2 * h + 2, :] += jnp.sum(dl_b, axis=0, keepdims=True)

    row = lambda i: (i, 0)
    st = lambda i: (i, 0, 0)
    return pl.pallas_call(
        body, name="ret_bwd", grid=(nc,),
        in_specs=[pl.BlockSpec((CHUNK, RET_Q), row), pl.BlockSpec((CHUNK, RET_Q), row),
                  pl.BlockSpec((CHUNK, RET_V), lambda i: (i, 1)), pl.BlockSpec((CHUNK, RET_V), row),
                  pl.BlockSpec((1, RET_Q, RET_DV), st), pl.BlockSpec((1, RET_Q, RET_DV), st),
                  pl.BlockSpec((1, RET_Q, RET_DV), st), pl.BlockSpec((1, RET_Q, RET_DV), st),
                  pl.BlockSpec((RET_HEADS, 14, CHUNK, CHUNK), lambda i: (0, 0, 0, 0))],
        out_specs=[pl.BlockSpec((CHUNK, RET_Q), row), pl.BlockSpec((CHUNK, RET_Q), row),
                   pl.BlockSpec((CHUNK, RET_V), row), pl.BlockSpec((8, CHUNK), lambda i: (0, 0))],
        out_shape=[_sds((T, RET_Q), F32), _sds((T, RET_Q), F32), _sds((T, RET_V), _MXU), _sds((8, CHUNK), F32)],
        compiler_params=_params("arbitrary"),
    )(qr, kr, proj, g_out, sf, sb, rf, rb, tabs)


def _swa_probs(q, k_win, bias, sink, valid):
    s = _dot(q, k_win, _NT) * ATT_SCALE + bias
    s = jnp.where(valid, s, NEG_INF)
    m = jnp.maximum(jnp.max(s, axis=-1, keepdims=True), sink)
    p = jnp.exp(s - m)
    e_sink = jnp.exp(sink - m)
    inv = 1.0 / (jnp.sum(p, axis=-1, keepdims=True) + e_sink)
    return p * inv, e_sink * inv


def _swa_valid(i, nb):
    col = lax.broadcasted_iota(jnp.int32, (1, 3 * CHUNK), 1)
    return jnp.logical_and(jnp.logical_or(col >= CHUNK, i > 0), jnp.logical_or(col < 2 * CHUNK, i < nb - 1))


def _swa_window_specs(nb, width, col_block, clamp):
    prev = lambda i: (jnp.maximum(clamp(i) - 1, 0), col_block)
    cur = lambda i: (clamp(i), col_block)
    nxt = lambda i: (jnp.minimum(clamp(i) + 1, nb - 1), col_block)
    return [pl.BlockSpec((CHUNK, width), f) for f in (prev, cur, nxt)]


def _swa_fwd(qn, kn, proj, bias, sink):
    T = qn.shape[0]
    nb = T // CHUNK
    kvw = SWA_KV_HEADS * HEAD_DIM
    group = SWA_HEADS // SWA_KV_HEADS

    def body(q_ref, k0, k1, k2, v0, v1, v2, bias_ref, sink_ref, y_ref):
        i = pl.program_id(0)
        valid = _swa_valid(i, nb)
        for g in range(SWA_KV_HEADS):
            gs = slice(g * HEAD_DIM, (g + 1) * HEAD_DIM)
            k_win = jnp.concatenate([k0[:, gs], k1[:, gs], k2[:, gs]], axis=0)
            v_win = jnp.concatenate([v0[:, gs], v1[:, gs], v2[:, gs]], axis=0).astype(_MXU)
            for hh in range(group):
                h = g * group + hh
                hs = slice(h * HEAD_DIM, (h + 1) * HEAD_DIM)
                p, _ = _swa_probs(q_ref[:, hs], k_win, bias_ref[h], sink_ref[h:h + 1, 0:1], valid)
                y_ref[:, hs] = _dot(p, v_win).astype(y_ref.dtype)

    ident = lambda i: i
    return pl.pallas_call(
        body, name="swa_fwd", grid=(nb,),
        in_specs=[pl.BlockSpec((CHUNK, D_MODEL), lambda i: (i, 0))]
        + _swa_window_specs(nb, kvw, 0, ident) + _swa_window_specs(nb, kvw, 17, ident)
        + [pl.BlockSpec((SWA_HEADS, CHUNK, 3 * CHUNK), lambda i: (0, 0, 0)), pl.BlockSpec((SWA_HEADS, HEAD_DIM), lambda i: (0, 0))],
        out_specs=pl.BlockSpec((CHUNK, D_MODEL), lambda i: (i, 0)),
        out_shape=_sds((T, D_MODEL), _MXU),
        compiler_params=_params("parallel"),
    )(qn, kn, kn, kn, proj, proj, proj, bias, sink)


def _swa_bwd(qn, kn, proj, dycat, bias, sink):
    T = qn.shape[0]
    nb = T // CHUNK
    kvw = SWA_KV_HEADS * HEAD_DIM
    group = SWA_HEADS // SWA_KV_HEADS

    def body(q_ref, k0, k1, k2, v0, v1, v2, dy_ref, bias_ref, sink_ref,
             dq_ref, dk_ref, dv_ref, dbias_ref, dsink_ref, acc_a, acc_b):
        i = pl.program_id(0)

        @pl.when(i == 0)
        def _():
            dbias_ref[...] = jnp.zeros_like(dbias_ref)
            dsink_ref[...] = jnp.zeros_like(dsink_ref)
            acc_a[...] = jnp.zeros_like(acc_a)
            acc_b[...] = jnp.zeros_like(acc_b)

        @pl.when(i < nb)
        def _():
            valid = _swa_valid(i, nb)
            for g in range(SWA_KV_HEADS):
                gs = slice(g * HEAD_DIM, (g + 1) * HEAD_DIM)
                k_win = jnp.concatenate([k0[:, gs], k1[:, gs], k2[:, gs]], axis=0)
                v_win = jnp.concatenate([v0[:, gs], v1[:, gs], v2[:, gs]], axis=0).astype(_MXU)
                dk_win = jnp.zeros((3 * CHUNK, HEAD_DIM), F32)
                dv_win = jnp.zeros((3 * CHUNK, HEAD_DIM), F32)
                for hh in range(group):
                    h = g * group + hh
                    hs = slice(h * HEAD_DIM, (h + 1) * HEAD_DIM)
                    q, dy = q_ref[:, hs], dy_ref[:, hs]
                    p, p_sink = _swa_probs(q, k_win, bias_ref[h], sink_ref[h:h + 1, 0:1], valid)
                    dp = _dot(dy, v_win, _NT)
                    delta = jnp.sum(p * dp, axis=-1, keepdims=True)
                    ds = p * (dp - delta)
                    dbias_ref[h] += ds
                    dsink_ref[h:h + 1, :] += jnp.sum(-p_sink * delta, axis=0, keepdims=True) * jnp.ones((1, HEAD_DIM), F32)
                    dq_ref[:, hs] = _dot(ds, k_win) * ATT_SCALE
                    dk_win = dk_win + _dot(ds, q, _TN) * ATT_SCALE
                    dv_win = dv_win + _dot(p, dy, _TN)
                for win, out_ref, col0 in ((dk_win, dk_ref, 0), (dv_win, dv_ref, kvw)):
                    cs = slice(col0 + g * HEAD_DIM, col0 + (g + 1) * HEAD_DIM)
                    out_ref[:, gs] = acc_a[:, cs] + win[:CHUNK]
                    acc_a[:, cs] = acc_b[:, cs] + win[CHUNK:2 * CHUNK]
                    acc_b[:, cs] = win[2 * CHUNK:]

        @pl.when(i == nb)
        def _():
            dk_ref[...] = acc_a[:, :kvw]
            dv_ref[...] = acc_a[:, kvw:]

    clamp = lambda i: jnp.minimum(i, nb - 1)
    late = lambda i: (jnp.maximum(i - 1, 0), 0)
    return pl.pallas_call(
        body, name="swa_bwd", grid=(nb + 1,),
        in_specs=[pl.BlockSpec((CHUNK, D_MODEL), lambda i: (clamp(i), 0))]
        + _swa_window_specs(nb, kvw, 0, clamp) + _swa_window_specs(nb, kvw, 17, clamp)
        + [pl.BlockSpec((CHUNK, D_MODEL), lambda i: (clamp(i), 1)),
           pl.BlockSpec((SWA_HEADS, CHUNK, 3 * CHUNK), lambda i: (0, 0, 0)), pl.BlockSpec((SWA_HEADS, HEAD_DIM), lambda i: (0, 0))],
        out_specs=[pl.BlockSpec((CHUNK, D_MODEL), lambda i: (clamp(i), 0)),
                   pl.BlockSpec((CHUNK, kvw), late), pl.BlockSpec((CHUNK, kvw), late),
                   pl.BlockSpec((SWA_HEADS, CHUNK, 3 * CHUNK), lambda i: (0, 0, 0)), pl.BlockSpec((SWA_HEADS, HEAD_DIM), lambda i: (0, 0))],
        out_shape=[_sds((T, D_MODEL), F32), _sds((T, kvw), F32), _sds((T, kvw), F32),
                   _sds((SWA_HEADS, CHUNK, 3 * CHUNK), F32), _sds((SWA_HEADS, HEAD_DIM), F32)],
        scratch_shapes=[pltpu.VMEM((CHUNK, 2 * kvw), F32), pltpu.VMEM((CHUNK, 2 * kvw), F32)],
        compiler_params=_params("arbitrary"),
    )(qn, kn, kn, kn, proj, proj, proj, dycat, bias, sink)


def _t5_bucket_reduce(dbias, bucket):
    def body(db_ref, bk_ref, o_ref):
        bk = bk_ref[...]
        row = lax.broadcasted_iota(jnp.int32, (SWA_HEADS, HEAD_DIM), 0)
        lane = lax.broadcasted_iota(jnp.int32, (SWA_HEADS, HEAD_DIM), 1)

        def per_bucket(b, acc):
            mask = bk == b
            for h in range(SWA_HEADS):
                tot = jnp.sum(jnp.sum(jnp.where(mask, db_ref[h], 0.0), axis=0, keepdims=True), axis=1, keepdims=True)
                acc = acc + jnp.where(jnp.logical_and(row == h, lane == b), tot, 0.0)
            return acc

        o_ref[...] = lax.fori_loop(0, T5_BUCKETS, per_bucket, jnp.zeros((SWA_HEADS, HEAD_DIM), F32))

    return pl.pallas_call(body, name="t5_bucket_reduce", out_shape=_sds((SWA_HEADS, HEAD_DIM), F32),
                          compiler_params=pltpu.CompilerParams(vmem_limit_bytes=VMEM_LIMIT_BYTES))(dbias, bucket)


def _headnorm_bwd(x, dy, gain):
    r = lax.rsqrt(jnp.mean(x * x, axis=-1, keepdims=True) + EPS)
    xhat = x * r
    dyg = dy * gain
    return r * (dyg - xhat * jnp.mean(dyg * xhat, axis=-1, keepdims=True)), dy * xhat


def _post_even(proj, dqr, dkr, dva, dga, dqn, dkn, dvb, cos, sin, q_gain, k_gain, *, tm):
    T = proj.shape[0]
    tm = min(tm, T)
    kvw = SWA_KV_HEADS * HEAD_DIM

    def body(qb_ref, kb_ref, dqr_ref, dkr_ref, dva_ref, dga_ref, dqn_ref, dkn_ref, dvb_ref, c_ref, s_ref, qg_ref, kg_ref,
             dp_ref, dqg_ref, dkg_ref):
        @pl.when(pl.program_id(0) == 0)
        def _():
            dqg_ref[...] = jnp.zeros_like(dqg_ref)
            dkg_ref[...] = jnp.zeros_like(dkg_ref)
        c = jnp.concatenate([c_ref[...]] * RET_HEADS, axis=1)
        s = jnp.concatenate([s_ref[...]] * RET_HEADS, axis=1)
        dq = dqr_ref[...]
        dp_ref[:, 0:RET_Q] = (dq * c + _swap_halves(dq * s, RET_DK // 2)).astype(dp_ref.dtype)
        dk = dkr_ref[...] * (RET_DK ** -0.5)
        dp_ref[:, RET_Q:2 * RET_Q] = (dk * c + _swap_halves(dk * s, RET_DK // 2)).astype(dp_ref.dtype)
        off = 2 * RET_Q
        dp_ref[:, off:off + RET_V] = dva_ref[...].astype(dp_ref.dtype)
        dp_ref[:, off + RET_V:off + 2 * RET_V] = dga_ref[...].astype(dp_ref.dtype)
        off += 2 * RET_V
        for src, dsrc, gain, dgain, heads, base in ((qb_ref, dqn_ref, qg_ref, dqg_ref, SWA_HEADS, off),
                                                    (kb_ref, dkn_ref, kg_ref, dkg_ref, SWA_KV_HEADS, off + D_MODEL)):
            for h in range(heads):
                sl = slice(h * HEAD_DIM, (h + 1) * HEAD_DIM)
                dx, dgx = _headnorm_bwd(src[:, sl], dsrc[:, sl], gain[...])
                dp_ref[:, base + h * HEAD_DIM:base + (h + 1) * HEAD_DIM] = dx.astype(dp_ref.dtype)
                dgain[...] += _rowsum8(dgx)
        dp_ref[:, off + D_MODEL + kvw:] = dvb_ref[...].astype(dp_ref.dtype)

    row = lambda i: (i, 0)
    const = lambda i: (0, 0)
    return pl.pallas_call(
        body, name="post_even", grid=(T // tm,),
        in_specs=[pl.BlockSpec((tm, D_MODEL), lambda i: (i, 3)), pl.BlockSpec((tm, kvw), lambda i: (i, 16)),
                  pl.BlockSpec((tm, RET_Q), row), pl.BlockSpec((tm, RET_Q), row),
                  pl.BlockSpec((tm, RET_V), row), pl.BlockSpec((tm, RET_V), row),
                  pl.BlockSpec((tm, D_MODEL), row), pl.BlockSpec((tm, kvw), row), pl.BlockSpec((tm, kvw), row),
                  pl.BlockSpec((tm, RET_DK), row), pl.BlockSpec((tm, RET_DK), row),
                  pl.BlockSpec((1, HEAD_DIM), const), pl.BlockSpec((1, HEAD_DIM), const)],
        out_specs=[pl.BlockSpec((tm, EVEN_IN), row), pl.BlockSpec((8, HEAD_DIM), const), pl.BlockSpec((8, HEAD_DIM), const)],
        out_shape=[_sds((T, EVEN_IN), _MXU), _sds((8, HEAD_DIM), F32), _sds((8, HEAD_DIM), F32)],
        compiler_params=_params("arbitrary"),
    )(proj, proj, dqr, dkr, dva, dga, dqn, dkn, dvb, cos, sin, q_gain, k_gain)


def _prep_odd(proj, cos, sin, q_gain, k_gain, *, tm):
    T = proj.shape[0]
    tm = min(tm, T)
    kvw = AX_KV_HEADS * HEAD_DIM

    def body(q_ref, k_ref, v_ref, c_ref, s_ref, qg_ref, kg_ref, qx_ref, kx_ref, vx_ref):
        c, s = c_ref[...], s_ref[...]
        for src, gain, dst, heads in ((q_ref, qg_ref, qx_ref, AX_HEADS), (k_ref, kg_ref, kx_ref, AX_KV_HEADS)):
            for h in range(heads):
                sl = slice(h * HEAD_DIM, (h + 1) * HEAD_DIM)
                xh = src[:, sl]
                r = lax.rsqrt(jnp.mean(xh * xh, axis=-1, keepdims=True) + EPS)
                xn = xh * r * gain[...]
                dst[:, sl] = (xn * c + _swap_halves(xn, HEAD_DIM // 4) * s).astype(dst.dtype)
        vx_ref[...] = v_ref[...].astype(vx_ref.dtype)

    row = lambda i: (i, 0)
    const = lambda i: (0, 0)
    return pl.pallas_call(
        body, name="prep_odd", grid=(T // tm,),
        in_specs=[pl.BlockSpec((tm, D_MODEL), row), pl.BlockSpec((tm, kvw), lambda i: (i, 4)), pl.BlockSpec((tm, kvw), lambda i: (i, 5)),
                  pl.BlockSpec((tm, HEAD_DIM), row), pl.BlockSpec((tm, HEAD_DIM), row),
                  pl.BlockSpec((1, HEAD_DIM), const), pl.BlockSpec((1, HEAD_DIM), const)],
        out_specs=[pl.BlockSpec((tm, D_MODEL), row), pl.BlockSpec((tm, kvw), row), pl.BlockSpec((tm, kvw), row)],
        out_shape=[_sds((T, D_MODEL), _MXU), _sds((T, kvw), _MXU), _sds((T, kvw), _MXU)],
        compiler_params=_params("parallel"),
    )(proj, proj, proj, cos, sin, q_gain, k_gain)


def _post_odd(proj, dqx, dkx, dvx, cos, sin, q_gain, k_gain, *, tm):
    T = proj.shape[0]
    tm = min(tm, T)
    kvw = AX_KV_HEADS * HEAD_DIM

    def body(q_ref, k_ref, dq_ref, dk_ref, dv_ref, c_ref, s_ref, qg_ref, kg_ref, dp_ref, dqg_ref, dkg_ref):
        @pl.when(pl.program_id(0) == 0)
        def _():
            dqg_ref[...] = jnp.zeros_like(dqg_ref)
            dkg_ref[...] = jnp.zeros_like(dkg_ref)
        c, s = c_ref[...], s_ref[...]
        for src, dsrc, gain, dgain, heads, base in ((q_ref, dq_ref, qg_ref, dqg_ref, AX_HEADS, 0),
                                                    (k_ref, dk_ref, kg_ref, dkg_ref, AX_KV_HEADS, D_MODEL)):
            for h in range(heads):
                sl = slice(h * HEAD_DIM, (h + 1) * HEAD_DIM)
                d = dsrc[:, sl]
                dn = d * c + _swap_halves(d * s, HEAD_DIM // 4)
                dx, dgx = _headnorm_bwd(src[:, sl], dn, gain[...])
                dp_ref[:, base + h * HEAD_DIM:base + (h + 1) * HEAD_DIM] = dx.astype(dp_ref.dtype)
                dgain[...] += _rowsum8(dgx)
        dp_ref[:, D_MODEL + kvw:] = dv_ref[...].astype(dp_ref.dtype)

    row = lambda i: (i, 0)
    const = lambda i: (0, 0)
    return pl.pallas_call(
        body, name="post_odd", grid=(T // tm,),
        in_specs=[pl.BlockSpec((tm, D_MODEL), row), pl.BlockSpec((tm, kvw), lambda i: (i, 4)),
                  pl.BlockSpec((tm, D_MODEL), row), pl.BlockSpec((tm, kvw), row), pl.BlockSpec((tm, kvw), row),
                  pl.BlockSpec((tm, HEAD_DIM), row), pl.BlockSpec((tm, HEAD_DIM), row),
                  pl.BlockSpec((1, HEAD_DIM), const), pl.BlockSpec((1, HEAD_DIM), const)],
        out_specs=[pl.BlockSpec((tm, ODD_IN), row), pl.BlockSpec((8, HEAD_DIM), const), pl.BlockSpec((8, HEAD_DIM), const)],
        out_shape=[_sds((T, ODD_IN), _MXU), _sds((8, HEAD_DIM), F32), _sds((8, HEAD_DIM), F32)],
        compiler_params=_params("arbitrary"),
    )(proj, proj, dqx, dkx, dvx, cos, sin, q_gain, k_gain)


def _flash_fwd(qx, kx, vx, *, tq, tk):
    T = qx.shape[0]
    tq, tk = min(tq, T), min(tk, T)
    group = AX_HEADS // AX_KV_HEADS

    def body(q_ref, k_ref, v_ref, o_ref, lse_ref):
        q = q_ref[...]

        def step(j, carry):
            m, l, acc = carry
            off = pl.multiple_of(j * tk, tk)
            k, v = k_ref[pl.ds(off, tk), :], v_ref[pl.ds(off, tk), :]
            s = _dot(q, k, _NT) * ATT_SCALE
            m_new = jnp.maximum(m, jnp.max(s, axis=-1, keepdims=True))
            p = jnp.exp(s - m_new)
            alpha = jnp.exp(m - m_new)
            return m_new, alpha * l + jnp.sum(p, axis=-1, keepdims=True), alpha * acc + _dot(p, v)

        init = (jnp.full((tq, 1), NEG_INF, F32), jnp.zeros((tq, 1), F32), jnp.zeros((tq, HEAD_DIM), F32))
        m, l, acc = lax.fori_loop(0, T // tk, step, init)
        o_ref[...] = (acc / l).astype(o_ref.dtype)
        lse_ref[0] = m + jnp.log(l)

    return pl.pallas_call(
        body, name="flash_fwd", grid=(AX_HEADS, T // tq),
        in_specs=[pl.BlockSpec((tq, HEAD_DIM), lambda h, i: (i, h)),
                  pl.BlockSpec((T, HEAD_DIM), lambda h, i: (0, h // group)), pl.BlockSpec((T, HEAD_DIM), lambda h, i: (0, h // group))],
        out_specs=[pl.BlockSpec((tq, HEAD_DIM), lambda h, i: (i, h)), pl.BlockSpec((1, tq, 1), lambda h, i: (h, i, 0))],
        out_shape=[_sds((T, D_MODEL), _MXU), _sds((AX_HEADS, T, 1), F32)],
        compiler_params=_params("parallel", "parallel"),
    )(qx, kx, vx)


def _flash_bwd_dq(qx, kx, vx, o, do, lse, *, tq, tk):
    T = qx.shape[0]
    tq, tk = min(tq, T), min(tk, T)
    group = AX_HEADS // AX_KV_HEADS

    def body(q_ref, k_ref, v_ref, o_ref, do_ref, lse_ref, dq_ref, delta_ref):
        q, do_blk = q_ref[...], do_ref[...]
        delta = jnp.sum(do_blk.astype(F32) * o_ref[...].astype(F32), axis=-1, keepdims=True)
        lse_blk = lse_ref[0]

        def step(j, dq):
            off = pl.multiple_of(j * tk, tk)
            k, v = k_ref[pl.ds(off, tk), :], v_ref[pl.ds(off, tk), :]
            p = jnp.exp(_dot(q, k, _NT) * ATT_SCALE - lse_blk)
            ds = p * (_dot(do_blk, v, _NT) - delta) * ATT_SCALE
            return dq + _dot(ds, k)

        dq_ref[...] = lax.fori_loop(0, T // tk, step, jnp.zeros((tq, HEAD_DIM), F32))
        delta_ref[0] = delta

    blk = lambda h, i: (i, h)
    stat = lambda h, i: (h, i, 0)
    return pl.pallas_call(
        body, name="flash_bwd_dq", grid=(AX_HEADS, T // tq),
        in_specs=[pl.BlockSpec((tq, HEAD_DIM), blk),
                  pl.BlockSpec((T, HEAD_DIM), lambda h, i: (0, h // group)), pl.BlockSpec((T, HEAD_DIM), lambda h, i: (0, h // group)),
                  pl.BlockSpec((tq, HEAD_DIM), blk), pl.BlockSpec((tq, HEAD_DIM), blk), pl.BlockSpec((1, tq, 1), stat)],
        out_specs=[pl.BlockSpec((tq, HEAD_DIM), blk), pl.BlockSpec((1, tq, 1), stat)],
        out_shape=[_sds((T, D_MODEL), F32), _sds((AX_HEADS, T, 1), F32)],
        compiler_params=_params("parallel", "parallel"),
    )(qx, kx, vx, o, do, lse)


def _flash_bwd_dkv(qx, kx, vx, do, lse, delta, *, tq, tk):
    T = qx.shape[0]
    tq, tk = min(tq, T), min(tk, T)
    nq = T // tq
    group = AX_HEADS // AX_KV_HEADS
    lse_rows = lse.reshape(AX_HEADS, nq, 1, tq)
    delta_rows = delta.reshape(AX_HEADS, nq, 1, tq)

    def body(k_ref, v_ref, q_ref, do_ref, lse_ref, delta_ref, dk_ref, dv_ref):
        @pl.when(pl.program_id(2) == 0)
        def _():
            dk_ref[...] = jnp.zeros_like(dk_ref)
            dv_ref[...] = jnp.zeros_like(dv_ref)
        k, v = k_ref[...], v_ref[...]

        def step(i, carry):
            dk, dv = carry
            off = pl.multiple_of(i * tq, tq)
            q, do_blk = q_ref[pl.ds(off, tq), :], do_ref[pl.ds(off, tq), :]
            pt = jnp.exp(_dot(k, q, _NT) * ATT_SCALE - lse_ref[0, i])
            dv = dv + _dot(pt, do_blk)
            dst = pt * (_dot(v, do_blk, _NT) - delta_ref[0, i]) * ATT_SCALE
            return dk + _dot(dst, q), dv

        zero = jnp.zeros((tk, HEAD_DIM), F32)
        dk, dv = lax.fori_loop(0, nq, step, (zero, zero))
        dk_ref[...] += dk
        dv_ref[...] += dv

    kv = lambda g, j, h: (j, g)
    qh = lambda g, j, h: (0, g * group + h)
    st = lambda g, j, h: (g * group + h, 0, 0, 0)
    return pl.pallas_call(
        body, name="flash_bwd_dkv", grid=(AX_KV_HEADS, T // tk, group),
        in_specs=[pl.BlockSpec((tk, HEAD_DIM), kv), pl.BlockSpec((tk, HEAD_DIM), kv),
                  pl.BlockSpec((T, HEAD_DIM), qh), pl.BlockSpec((T, HEAD_DIM), qh),
                  pl.BlockSpec((1, nq, 1, tq), st), pl.BlockSpec((1, nq, 1, tq), st)],
        out_specs=[pl.BlockSpec((tk, HEAD_DIM), kv), pl.BlockSpec((tk, HEAD_DIM), kv)],
        out_shape=[_sds((T, AX_KV_HEADS * HEAD_DIM), F32), _sds((T, AX_KV_HEADS * HEAD_DIM), F32)],
        compiler_params=_params("parallel", "parallel", "arbitrary"),
    )(kx, vx, qx, do, lse_rows, delta_rows)


TM = 1024
TM_WIDE = 512


def _mlp_fwd(tag, x, gain, w_up, w_down, target=None):
    u, h = _norm_matmul(f"mlp_up{tag}", x, gain, w_up, tm=TM, tn=1024, out_dtype=F32)
    out = _matmul_res(f"mlp_down{tag}", [u], w_down, x, tm=TM_WIDE, relu2=True, target=target)
    return out, (x, u, h)


def _mlp_bwd(tag, saved, gain, w_up, w_down, dy):
    x, u, h = saved
    du = _matmul_nt(f"mlp_down{tag}_bwd", dy, w_down, tm=TM, tn=1024, out_dtype=_MXU, relu_of=u)
    dw_down = _matmul_tn(f"mlp_down{tag}_dw", u, dy, tk=1024, tn=1024, tt=1024, out_dtype=_WIRE, relu2=True)
    dx, dgain = _matmul_nt_normbwd(f"mlp_up{tag}_bwd", du, w_up, x, gain, dy, tm=TM_WIDE)
    dw_up = _matmul_tn(f"mlp_up{tag}_dw", h, du, tk=1024, tn=1024, tt=1024, out_dtype=_WIRE)
    return dx, dgain, dw_up, dw_down


def _local_step(x, target, p, wf):
    T = x.shape[0]
    cos_r, sin_r = _ret_rope_tables(T)
    cos_a, sin_a = _axial_rope_tables(T)
    tabs, rw, log_gamma = _retention_tables(p["ret_decay_logit"][0])
    bias = _swa_bias(p["t5_table"])
    sink = p["swa_sink"][0][:, None] * jnp.ones((1, HEAD_DIM), F32)
    nm, nl = p["norm_mix"], p["norm_mlp"]

    proj0, h0 = _norm_matmul("in_even", x, nm[0:1], wf["w_in_even"], tm=TM, tn=1152, out_dtype=F32)
    qr, kr, qn, kn = _prep_even(proj0, cos_r, sin_r, p["swa_q_norm"], p["swa_k_norm"], tm=TM)
    sf, sb = _ret_scan("ret_scan_fwd", kr, proj0, 1, rw["cf"], rw["dec_f"], rw["cb"], rw["dec_b"])
    ret_o, ya = _ret_out(qr, kr, proj0, sf, sb, tabs[:, (TAB_D, TAB_A, TAB_B)], p["ret_norm"])
    yb = _swa_fwd(qn, kn, proj0, bias, sink)
    x1 = _matmul_res("out_even", [ya, yb], wf["w_out_even"], x, tm=TM)
    x2, mlp0 = _mlp_fwd(0, x1, nl[0:1], wf["w_mlp_up"][0], wf["w_mlp_down"][0])
    proj1, h1 = _norm_matmul("in_odd", x2, nm[1:2], wf["w_in_odd"], tm=TM, tn=768, out_dtype=F32)
    qx, kx, vx = _prep_odd(proj1, cos_a, sin_a, p["ax_q_norm"], p["ax_k_norm"], tm=TM)
    o, lse = _flash_fwd(qx, kx, vx, tq=512, tk=1024)
    x3 = _matmul_res("out_odd", [o], wf["w_out_odd"], x2, tm=TM)
    (g4, loss_part), mlp1 = _mlp_fwd(1, x3, nl[1:2], wf["w_mlp_up"][1], wf["w_mlp_down"][1], target=target)

    dx3, dnl1, dw_up1, dw_down1 = _mlp_bwd(1, mlp1, nl[1:2], wf["w_mlp_up"][1], wf["w_mlp_down"][1], g4)
    do = _matmul_nt("out_odd_bwd", dx3, wf["w_out_odd"], tm=TM, tn=1024, out_dtype=_MXU)
    dw_out_odd = _matmul_tn("out_odd_dw", o, dx3, tk=1024, tn=1024, tt=1024, out_dtype=_WIRE)
    dqx, delta = _flash_bwd_dq(qx, kx, vx, o, do, lse, tq=512, tk=1024)
    dkx, dvx = _flash_bwd_dkv(qx, kx, vx, do, lse, delta, tq=1024, tk=512)
    dproj1, dqg1, dkg1 = _post_odd(proj1, dqx, dkx, dvx, cos_a, sin_a, p["ax_q_norm"], p["ax_k_norm"], tm=TM)
    dx2, dnm1 = _matmul_nt_normbwd("in_odd_bwd", dproj1, wf["w_in_odd"], x2, nm[1:2], dx3, tm=TM_WIDE)
    dw_in_odd = _matmul_tn("in_odd_dw", h1, dproj1, tk=1024, tn=768, tt=1024, out_dtype=_WIRE)
    dx1, dnl0, dw_up0, dw_down0 = _mlp_bwd(0, mlp0, nl[0:1], wf["w_mlp_up"][0], wf["w_mlp_down"][0], dx2)
    dycat = _matmul_nt("out_even_bwd", dx1, wf["w_out_even"], tm=TM, tn=1024, out_dtype=F32)
    dw_out_even = jnp.concatenate([_matmul_tn("out_even_dw_ret", ya, dx1, tk=1024, tn=1024, tt=1024, out_dtype=_WIRE),
                                   _matmul_tn("out_even_dw_swa", yb, dx1, tk=1024, tn=1024, tt=1024, out_dtype=_WIRE)], axis=0)
    g_out, dga, dretg = _ret_gate_bwd(dycat, proj0, ret_o, p["ret_norm"], tm=TM)
    rb, rf = _ret_scan("ret_scan_bwd", qr, g_out, 0, rw["b"], rw["dec_b"], rw["a"], rw["dec_f"])
    dqr, dkr, dva, dlog = _ret_bwd(qr, kr, proj0, g_out, sf, sb, rf, rb, tabs)
    dqn, dkn, dvb, dbias, dsink = _swa_bwd(qn, kn, proj0, dycat, bias, sink)
    dt5 = _t5_bucket_reduce(dbias, _t5_bucket(_swa_rel()).astype(jnp.int32))
    dproj0, dqg0, dkg0 = _post_even(proj0, dqr, dkr, dva, dga, dqn, dkn, dvb, cos_r, sin_r,
                                    p["swa_q_norm"], p["swa_k_norm"], tm=TM_WIDE)
    dx0, dnm0 = _matmul_nt_normbwd("in_even_bwd", dproj0, wf["w_in_even"], x, nm[0:1], dx1, tm=TM_WIDE)
    dw_in_even = _matmul_tn("in_even_dw", h0, dproj0, tk=1024, tn=1152, tt=1024, out_dtype=_WIRE)

    big = {
        "w_in_even": dw_in_even[None], "w_out_even": dw_out_even[None],
        "w_in_odd": dw_in_odd[None], "w_out_odd": dw_out_odd[None],
        "w_mlp_up": jnp.stack([dw_up0, dw_up1]), "w_mlp_down": jnp.stack([dw_down0, dw_down1]),
    }
    fold = lambda part: jnp.sum(part, axis=0)
    dlam = jnp.sum(dlog, axis=1).reshape(RET_HEADS, 2).T
    small = {
        "norm_mix": jnp.stack([fold(dnm0), fold(dnm1)]),
        "norm_mlp": jnp.stack([fold(dnl0), fold(dnl1)]),
        "ret_decay_logit": (dlam * (1.0 - jnp.exp(log_gamma)))[None],
        "ret_norm": fold(dretg)[None],
        "swa_q_norm": fold(dqg0)[None], "swa_k_norm": fold(dkg0)[None],
        "swa_sink": dsink[:, 0][None],
        "t5_table": dt5[:, :T5_BUCKETS].T,
        "ax_q_norm": fold(dqg1)[None], "ax_k_norm": fold(dkg1)[None],
    }
    return loss_part, dx0, big, small


BIG = ("w_in_even", "w_out_even", "w_in_odd", "w_out_odd", "w_mlp_up", "w_mlp_down")
SMALL = ("norm_mix", "norm_mlp", "ret_decay_logit", "ret_norm", "swa_q_norm", "swa_k_norm", "swa_sink", "t5_table",
         "ax_q_norm", "ax_k_norm")
WEIGHTS = ("norm_mix", "norm_mlp", "w_in_even", "w_out_even", "ret_decay_logit", "ret_norm", "swa_q_norm", "swa_k_norm",
           "swa_sink", "t5_table", "w_in_odd", "w_out_odd", "ax_q_norm", "ax_k_norm", "w_mlp_up", "w_mlp_down")
SHARD_AXIS = {"w_in_even": 2, "w_out_even": 1, "w_in_odd": 2, "w_out_odd": 1, "w_mlp_up": 2, "w_mlp_down": 1}
HALF_AXIS = {"w_in_even": 1, "w_out_even": 1, "w_in_odd": 1, "w_out_odd": 1, "w_mlp_up": 0, "w_mlp_down": 0}
N_CHIPS = 4
ANY = pl.BlockSpec(memory_space=pl.ANY)


def _mesh_pos():
    return lax.axis_index("x"), lax.axis_index("y"), lax.axis_index("c")


def _window(ref, axis, start, size):
    idx = [slice(None)] * len(ref.shape)
    idx[axis] = pl.ds(start, size)
    return ref.at[tuple(idx)]


def _cast_mxu(name, w, *, tr=256):
    R, C = w.shape
    tr = min(tr, R)

    def body(w_ref, o_ref):
        o_ref[...] = w_ref[...].astype(o_ref.dtype)

    return pl.pallas_call(body, name=name, grid=(R // tr,), in_specs=[pl.BlockSpec((tr, C), lambda i: (i, 0))],
                          out_specs=pl.BlockSpec((tr, C), lambda i: (i, 0)), out_shape=_sds((R, C), _MXU),
                          compiler_params=_params("parallel"))(w)


def _allgather_weights(shards):
    names = list(shards)
    n = len(names)
    sizes = [shards[k].shape[SHARD_AXIS[k]] for k in names]

    def body(*refs):
        ins, outs = refs[:n], refs[n:2 * n]
        local_sems, send_sems, recv_sems = refs[2 * n:]
        x, y, c = _mesh_pos()
        chips = [(1 - x, y), (x, 1 - y), (1 - x, 1 - y)]
        slot = lambda t, px, py: _window(outs[t], SHARD_AXIS[names[t]], pl.multiple_of((2 * px + py) * sizes[t], 128), sizes[t])
        local, remote = [], []
        for t in range(n):
            local.append(pltpu.make_async_copy(ins[t], slot(t, x, y), local_sems.at[t]))
            local[-1].start()
            for k, (px, py) in enumerate(chips):
                remote.append(pltpu.make_async_remote_copy(ins[t], slot(t, x, y), send_sems.at[3 * t + k], recv_sems.at[3 * t + k],
                                                           device_id=(px, py, c), device_id_type=MESH))
                remote[-1].start()
        for t in range(n):
            for k, (px, py) in enumerate(chips):
                pltpu.make_async_remote_copy(ins[t], slot(t, px, py), send_sems.at[3 * t + k], recv_sems.at[3 * t + k],
                                             device_id=(px, py, c), device_id_type=MESH).wait_recv()
        for cp in remote:
            cp.wait_send()
        for cp in local:
            cp.wait()

    full_shape = lambda k: tuple(d * (N_CHIPS if a == SHARD_AXIS[k] else 1) for a, d in enumerate(shards[k].shape))
    outs = pl.pallas_call(
        body, name="allgather_weights", in_specs=[ANY] * n, out_specs=[ANY] * n,
        out_shape=[_sds(full_shape(k), shards[k].dtype) for k in names],
        scratch_shapes=[pltpu.SemaphoreType.DMA((n,)), pltpu.SemaphoreType.DMA((3 * n,)), pltpu.SemaphoreType.DMA((3 * n,))],
    )(*[shards[k] for k in names])
    return dict(zip(names, outs))


FLIPS = [(a, b, d) for a in (0, 1) for b in (0, 1) for d in (0, 1) if (a, b, d) != (0, 0, 0)]


def _flip(pos, f):
    return tuple(1 - p if fi else p for p, fi in zip(pos, f))


def _piece_shape(name, shape):
    out = list(shape)
    out[SHARD_AXIS[name]] //= N_CHIPS
    out[HALF_AXIS[name]] //= 2
    return tuple(out)


def _piece(ref, name, chip, core, shard_size, half_size):
    sa, ha = SHARD_AXIS[name], HALF_AXIS[name]
    if sa == ha:
        return _window(ref, sa, pl.multiple_of(chip * shard_size + core * half_size, 8), half_size)
    half_start = pl.multiple_of(core * half_size, 8) if ha else core * half_size
    return _window(_window(ref, sa, pl.multiple_of(chip * shard_size, 128), shard_size), ha, half_start, half_size)


def _scatter_gradients(grads):
    names = list(grads)
    n = len(names)
    n_peer = len(FLIPS)
    pieces = [_piece_shape(k, grads[k].shape) for k in names]
    shard_sizes = [grads[k].shape[SHARD_AXIS[k]] // N_CHIPS for k in names]
    half_sizes = [p[HALF_AXIS[k]] for k, p in zip(names, pieces)]

    def body(*refs):
        ins, outs = refs[:n], refs[n:2 * n]
        local_sems, send_sems, recv_sems = refs[2 * n:]
        pos = _mesh_pos()
        ident = lambda p: 4 * p[0] + 2 * p[1] + p[2]
        me = ident(pos)
        src = lambda t, p: _piece(ins[t], names[t], 2 * p[0] + p[1], p[2], shard_sizes[t], half_sizes[t])
        local, remote = [], []
        for t in range(n):
            local.append(pltpu.make_async_copy(src(t, pos), outs[t].at[me], local_sems.at[t]))
            local[-1].start()
            for k, f in enumerate(FLIPS):
                peer = _flip(pos, f)
                remote.append(pltpu.make_async_remote_copy(src(t, peer), outs[t].at[me], send_sems.at[n_peer * t + k],
                                                           recv_sems.at[n_peer * t + k], device_id=peer, device_id_type=MESH))
                remote[-1].start()
        for t in range(n):
            for k, f in enumerate(FLIPS):
                peer = _flip(pos, f)
                pltpu.make_async_remote_copy(src(t, pos), outs[t].at[ident(peer)], send_sems.at[n_peer * t + k],
                                             recv_sems.at[n_peer * t + k], device_id=peer, device_id_type=MESH).wait_recv()
        for cp in remote:
            cp.wait_send()
        for cp in local:
            cp.wait()

    outs = pl.pallas_call(
        body, name="scatter_gradients", in_specs=[ANY] * n, out_specs=[ANY] * n,
        out_shape=[_sds((N_DEV,) + p, grads[k].dtype) for k, p in zip(names, pieces)],
        scratch_shapes=[pltpu.SemaphoreType.DMA((n,)), pltpu.SemaphoreType.DMA((n_peer * n,)), pltpu.SemaphoreType.DMA((n_peer * n,))],
    )(*[grads[k] for k in names])
    return dict(zip(names, outs))


def _sum_slots(name, buf, *, tr=128):
    _, L, R, C = buf.shape
    tr = min(tr, R)

    def body(b_ref, o_ref):
        acc = b_ref[0].astype(F32)
        for s in range(1, N_DEV):
            acc = acc + b_ref[s].astype(F32)
        o_ref[...] = acc

    return pl.pallas_call(body, name=name, grid=(L, R // tr),
                          in_specs=[pl.BlockSpec((N_DEV, 1, tr, C), lambda l, i: (0, l, i, 0))],
                          out_specs=pl.BlockSpec((1, tr, C), lambda l, i: (l, i, 0)), out_shape=_sds((L, R, C), F32),
                          compiler_params=_params("parallel", "parallel"))(buf)


def _exchange_halves(halves, shard_shapes):
    names = list(halves)
    n = len(names)
    half_sizes = [halves[k].shape[HALF_AXIS[k]] for k in names]

    def body(*refs):
        ins, outs = refs[:n], refs[n:2 * n]
        local_sems, send_sems, recv_sems = refs[2 * n:]
        x, y, c = _mesh_pos()
        half = lambda t, core: _window(outs[t], HALF_AXIS[names[t]], pl.multiple_of(core * half_sizes[t], 8)
                                       if HALF_AXIS[names[t]] else core * half_sizes[t], half_sizes[t])
        local, remote = [], []
        for t in range(n):
            local.append(pltpu.make_async_copy(ins[t], half(t, c), local_sems.at[t]))
            local[-1].start()
            remote.append(pltpu.make_async_remote_copy(ins[t], half(t, c), send_sems.at[t], recv_sems.at[t],
                                                       device_id=(x, y, 1 - c), device_id_type=MESH))
            remote[-1].start()
        for t in range(n):
            pltpu.make_async_remote_copy(ins[t], half(t, 1 - c), send_sems.at[t], recv_sems.at[t],
                                         device_id=(x, y, 1 - c), device_id_type=MESH).wait_recv()
        for cp in remote:
            cp.wait_send()
        for cp in local:
            cp.wait()

    outs = pl.pallas_call(
        body, name="exchange_halves", in_specs=[ANY] * n, out_specs=[ANY] * n,
        out_shape=[_sds(shard_shapes[k], F32) for k in names],
        scratch_shapes=[pltpu.SemaphoreType.DMA((n,)), pltpu.SemaphoreType.DMA((n,)), pltpu.SemaphoreType.DMA((n,))],
    )(*[halves[k] for k in names])
    return dict(zip(names, outs))


def _adamw_math(w, g, m, v):
    m = ADAM_B1 * m + (1.0 - ADAM_B1) * g
    v = ADAM_B2 * v + (1.0 - ADAM_B2) * jnp.square(g)
    m_hat = m / (1.0 - ADAM_B1 ** ADAM_STEP)
    v_hat = v / (1.0 - ADAM_B2 ** ADAM_STEP)
    return -ADAM_LR * (m_hat / (jnp.sqrt(v_hat) + ADAM_EPS) + ADAM_WD * w), m, v


def _adamw(name, w, g, m, v, *, tr=256):
    R, C = w.shape
    tr = min(tr, R)

    def body(w_ref, g_ref, m_ref, v_ref, d_ref, mo_ref, vo_ref):
        d_ref[...], mo_ref[...], vo_ref[...] = _adamw_math(w_ref[...], g_ref[...], m_ref[...], v_ref[...])

    spec = pl.BlockSpec((tr, C), lambda i: (i, 0))
    return pl.pallas_call(body, name=name, grid=(R // tr,), in_specs=[spec] * 4, out_specs=[spec] * 3,
                          out_shape=[_sds((R, C), F32)] * 3, compiler_params=_params("parallel"))(w, g, m, v)


SLAB_ROWS = 8
LOSS_ROW = 7


def _pack_small(d):
    pad = lambda a, width: jnp.pad(a.reshape(-1), (0, width - a.size))
    row5 = jnp.concatenate([d["swa_q_norm"].reshape(-1), d["swa_k_norm"].reshape(-1), d["ax_q_norm"].reshape(-1),
                            d["ax_k_norm"].reshape(-1), pad(d["swa_sink"], HEAD_DIM), pad(d["ret_decay_logit"], HEAD_DIM),
                            jnp.zeros((2 * HEAD_DIM,), F32)])
    return jnp.concatenate([d["norm_mix"], d["norm_mlp"], d["ret_norm"], row5[None], pad(d["t5_table"], D_MODEL)[None],
                            jnp.zeros((1, D_MODEL), F32)], axis=0)


def _unpack_small(slab):
    r5 = slab[5]
    return {
        "norm_mix": slab[0:2], "norm_mlp": slab[2:4], "ret_norm": slab[4:5],
        "swa_q_norm": r5[None, 0:128], "swa_k_norm": r5[None, 128:256], "ax_q_norm": r5[None, 256:384],
        "ax_k_norm": r5[None, 384:512], "swa_sink": r5[None, 512:512 + SWA_HEADS],
        "ret_decay_logit": r5[640:640 + 2 * RET_HEADS].reshape(1, 2, RET_HEADS),
        "t5_table": slab[6, :T5_BUCKETS * SWA_HEADS].reshape(T5_BUCKETS, SWA_HEADS),
    }


def _small_allreduce_adamw(g_slab, w_slab, m_slab, v_slab, loss_part):
    def body(g_ref, w_ref, m_ref, v_ref, lp_ref, go_ref, d_ref, mo_ref, vo_ref, gath, send_sems, recv_sems):
        pos = _mesh_pos()
        ident = lambda p: 4 * p[0] + 2 * p[1] + p[2]
        me = ident(pos)
        row = lax.broadcasted_iota(jnp.int32, (SLAB_ROWS, D_MODEL), 0)
        lane = lax.broadcasted_iota(jnp.int32, (SLAB_ROWS, D_MODEL), 1)
        loss = jnp.sum(jnp.sum(lp_ref[...], axis=0, keepdims=True), axis=1, keepdims=True) * (0.5 / D_MODEL)
        gath[me] = jnp.where(jnp.logical_and(row == LOSS_ROW, lane == 0), loss, g_ref[...])
        sends = []
        for k, f in enumerate(FLIPS):
            sends.append(pltpu.make_async_remote_copy(gath.at[me], gath.at[me], send_sems.at[k], recv_sems.at[k],
                                                      device_id=_flip(pos, f), device_id_type=MESH))
            sends[-1].start()
        for k, f in enumerate(FLIPS):
            peer = _flip(pos, f)
            pltpu.make_async_remote_copy(gath.at[me], gath.at[ident(peer)], send_sems.at[k], recv_sems.at[k],
                                         device_id=peer, device_id_type=MESH).wait_recv()
        for cp in sends:
            cp.wait_send()
        total = gath[0]
        for s in range(1, N_DEV):
            total = total + gath[s]
        go_ref[...] = total
        d_ref[...], mo_ref[...], vo_ref[...] = _adamw_math(w_ref[...], total, m_ref[...], v_ref[...])

    vmem = pl.BlockSpec(memory_space=pltpu.VMEM)
    return pl.pallas_call(
        body, name="small_allreduce_adamw", in_specs=[vmem] * 5, out_specs=[vmem] * 4,
        out_shape=[_sds((SLAB_ROWS, D_MODEL), F32)] * 4,
        scratch_shapes=[pltpu.VMEM((N_DEV, SLAB_ROWS, D_MODEL), F32),
                        pltpu.SemaphoreType.DMA((len(FLIPS),)), pltpu.SemaphoreType.DMA((len(FLIPS),))],
    )(g_slab, w_slab, m_slab, v_slab, loss_part)


def kernel(x, norm_mix, norm_mlp, w_in_even, w_out_even, ret_decay_logit, ret_norm, swa_q_norm, swa_k_norm, swa_sink, t5_table, w_in_odd, w_out_odd, ax_q_norm, ax_k_norm, w_mlp_up, w_mlp_down, loss_target, m_norm_mix, m_norm_mlp, m_w_in_even, m_w_out_even, m_ret_decay_logit, m_ret_norm, m_swa_q_norm, m_swa_k_norm, m_swa_sink, m_t5_table, m_w_in_odd, m_w_out_odd, m_ax_q_norm, m_ax_k_norm, m_w_mlp_up, m_w_mlp_down, v_norm_mix, v_norm_mlp, v_w_in_even, v_w_out_even, v_ret_decay_logit, v_ret_norm, v_swa_q_norm, v_swa_k_norm, v_swa_sink, v_t5_table, v_w_in_odd, v_w_out_odd, v_ax_q_norm, v_ax_k_norm, v_w_mlp_up, v_w_mlp_down):
    w = dict(zip(WEIGHTS, (norm_mix, norm_mlp, w_in_even, w_out_even, ret_decay_logit, ret_norm, swa_q_norm, swa_k_norm,
                           swa_sink, t5_table, w_in_odd, w_out_odd, ax_q_norm, ax_k_norm, w_mlp_up, w_mlp_down)))
    m = dict(zip(WEIGHTS, (m_norm_mix, m_norm_mlp, m_w_in_even, m_w_out_even, m_ret_decay_logit, m_ret_norm, m_swa_q_norm,
                           m_swa_k_norm, m_swa_sink, m_t5_table, m_w_in_odd, m_w_out_odd, m_ax_q_norm, m_ax_k_norm,
                           m_w_mlp_up, m_w_mlp_down)))
    v = dict(zip(WEIGHTS, (v_norm_mix, v_norm_mlp, v_w_in_even, v_w_out_even, v_ret_decay_logit, v_ret_norm, v_swa_q_norm,
                           v_swa_k_norm, v_swa_sink, v_t5_table, v_w_in_odd, v_w_out_odd, v_ax_q_norm, v_ax_k_norm,
                           v_w_mlp_up, v_w_mlp_down)))
    flat = lambda a: a.reshape(-1, a.shape[-1])

    shards = {k: _cast_mxu(f"cast_{k}", flat(w[k])).reshape(w[k].shape) for k in BIG}
    whole = _allgather_weights(shards)
    wf = {k: (whole[k] if k.startswith("w_mlp") else whole[k][0]) for k in BIG}

    loss_part, dx, big_g, small_g = _local_step(x[0], loss_target[0], {k: w[k] for k in SMALL}, wf)

    slots = _scatter_gradients(big_g)
    halves = {k: _sum_slots(f"sum_{k}", slots[k]) for k in BIG}
    grad = _exchange_halves(halves, {k: w[k].shape for k in BIG})
    delta, new_m, new_v = {}, {}, {}
    for k in BIG:
        d_k, m_k, v_k = _adamw(f"adamw_{k}", flat(w[k]), flat(grad[k]), flat(m[k]), flat(v[k]))
        delta[k], new_m[k], new_v[k] = d_k.reshape(w[k].shape), m_k.reshape(w[k].shape), v_k.reshape(w[k].shape)

    slabs = _small_allreduce_adamw(_pack_small(small_g), _pack_small({k: w[k] for k in SMALL}),
                                   _pack_small({k: m[k] for k in SMALL}), _pack_small({k: v[k] for k in SMALL}), loss_part)
    loss = slabs[0][LOSS_ROW, 0]
    for out, slab in zip((grad, delta, new_m, new_v), slabs):
        out.update(_unpack_small(slab))

    return (loss, dx[None], *[grad[k] for k in WEIGHTS], *[delta[k] for k in WEIGHTS],
            *[new_m[k] for k in WEIGHTS], *[new_v[k] for k in WEIGHTS])
```

```python
import functools
import math

import jax
import jax.numpy as jnp
from jax import lax
from jax.experimental import pallas as pl
from jax.experimental.pallas import tpu as pltpu

F32 = jnp.float32
BF16 = jnp.bfloat16
_MXU = BF16
_WIRE = BF16

D_MODEL = 1024
HEAD_DIM = 128
EPS = 1e-6
NEG_INF = -1e30
CHUNK = 128
GRID_W = 64
RET_HEADS, RET_DK, RET_DV = 4, 128, 256
RET_Q, RET_V = RET_HEADS * RET_DK, RET_HEADS * RET_DV
RET_THETA = 10000.0
SWA_HEADS, SWA_KV_HEADS = 8, 2
T5_BUCKETS, T5_MAX_DIST = 32, 128
AX_HEADS, AX_KV_HEADS = 8, 2
AX_THETA = 10000.0
D_FF = 4 * D_MODEL
EVEN_IN = 2 * RET_Q + 2 * RET_V + D_MODEL + 2 * SWA_KV_HEADS * HEAD_DIM
ODD_IN = D_MODEL + 2 * AX_KV_HEADS * HEAD_DIM
ATT_SCALE = HEAD_DIM ** -0.5

ADAM_LR, ADAM_B1, ADAM_B2, ADAM_EPS, ADAM_WD, ADAM_STEP = 0.001, 0.9, 0.999, 1e-08, 0.01, 10

N_DEV = 8
VMEM_LIMIT_BYTES = 56 << 20
MESH = pl.DeviceIdType.MESH

_NN = (((1,), (0,)), ((), ()))
_NT = (((1,), (1,)), ((), ()))
_TN = (((0,), (0,)), ((), ()))


def _dot(a, b, dn=_NN):
    return lax.dot_general(a.astype(_MXU), b.astype(_MXU), dn, preferred_element_type=F32)


def _params(*sem):
    return pltpu.CompilerParams(dimension_semantics=sem, vmem_limit_bytes=VMEM_LIMIT_BYTES)


def _sds(shape, dtype):
    return jax.ShapeDtypeStruct(tuple(shape), dtype)


def _rowsum8(x):
    return jnp.sum(x.reshape(x.shape[0] // 8, 8, x.shape[1]), axis=0)


def _swap_halves(x, half):
    width = x.shape[1]
    lane = lax.broadcasted_iota(jnp.int32, x.shape, 1)
    up = pltpu.roll(x, width - half, axis=1)
    down = pltpu.roll(x, half, axis=1)
    return jnp.where((lane & (2 * half - 1)) < half, up, down)


def _sigmoid(x):
    return 1.0 / (1.0 + jnp.exp(-x))


def _norm_matmul(name, x, gain, w, *, tm, tn, out_dtype):
    T, K = x.shape
    N = w.shape[1]
    tm, tn = min(tm, T), min(tn, N)

    def body(x_ref, g_ref, w_ref, y_ref, h_ref, h_sc):
        @pl.when(pl.program_id(1) == 0)
        def _():
            xv = x_ref[...]
            r = lax.rsqrt(jnp.mean(xv * xv, axis=-1, keepdims=True) + EPS)
            h = (xv * r * g_ref[...]).astype(_MXU)
            h_sc[...] = h
            h_ref[...] = h
        y_ref[...] = jnp.dot(h_sc[...], w_ref[...], preferred_element_type=F32).astype(y_ref.dtype)

    return pl.pallas_call(
        body, name=name, grid=(T // tm, N // tn),
        in_specs=[pl.BlockSpec((tm, K), lambda i, j: (i, 0)),
                  pl.BlockSpec((1, K), lambda i, j: (0, 0)),
                  pl.BlockSpec((K, tn), lambda i, j: (0, j))],
        out_specs=[pl.BlockSpec((tm, tn), lambda i, j: (i, j)),
                   pl.BlockSpec((tm, K), lambda i, j: (i, 0))],
        out_shape=[_sds((T, N), out_dtype), _sds((T, K), _MXU)],
        scratch_shapes=[pltpu.VMEM((tm, K), _MXU)],
        compiler_params=_params("parallel", "arbitrary"),
    )(x, gain, w)


def _matmul_res(name, a_list, w, res, *, tm, relu2=False, target=None):
    T = res.shape[0]
    N = w.shape[1]
    K = a_list[0].shape[1]
    n_a = len(a_list)
    tm = min(tm, T)
    with_loss = target is not None

    def body(*refs):
        a_refs = refs[:n_a]
        w_refs = refs[n_a:2 * n_a]
        res_ref = refs[2 * n_a]
        acc = res_ref[...]
        for a_ref, w_ref in zip(a_refs, w_refs):
            a = a_ref[...]
            if relu2:
                a = jnp.square(jnp.maximum(a.astype(F32), 0.0))
            acc = acc + _dot(a, w_ref[...])
        if with_loss:
            tgt_ref, g_ref, loss_ref = refs[2 * n_a + 1:]
            diff = acc - tgt_ref[...]
            g_ref[...] = diff * (1.0 / N)

            @pl.when(pl.program_id(0) == 0)
            def _():
                loss_ref[...] = jnp.zeros_like(loss_ref)
            loss_ref[...] += _rowsum8(diff * diff)
        else:
            refs[2 * n_a + 1][...] = acc

    row = lambda i: (i, 0)
    in_specs = [pl.BlockSpec((tm, K), row) for _ in a_list]
    in_specs += [pl.BlockSpec((K, N), functools.partial(lambda i, b: (b, 0), b=b)) for b in range(n_a)]
    in_specs += [pl.BlockSpec((tm, N), row)]
    args = list(a_list) + [w] * n_a + [res]
    if with_loss:
        in_specs.append(pl.BlockSpec((tm, N), row))
        args.append(target)
        out_specs = [pl.BlockSpec((tm, N), row), pl.BlockSpec((8, N), lambda i: (0, 0))]
        out_shape = [_sds((T, N), F32), _sds((8, N), F32)]
        sem = "arbitrary"
    else:
        out_specs = pl.BlockSpec((tm, N), row)
        out_shape = _sds((T, N), F32)
        sem = "parallel"
    return pl.pallas_call(body, name=name, grid=(T // tm,), in_specs=in_specs, out_specs=out_specs,
                          out_shape=out_shape, compiler_params=_params(sem))(*args)


def _matmul_nt(name, a, w, *, tm, tn, out_dtype, relu_of=None):
    T, K = a.shape
    N = w.shape[0]
    tm, tn = min(tm, T), min(tn, N)

    def body(*refs):
        if relu_of is None:
            a_ref, w_ref, o_ref = refs
            o_ref[...] = _dot(a_ref[...], w_ref[...], _NT).astype(o_ref.dtype)
        else:
            a_ref, w_ref, u_ref, o_ref = refs
            da = _dot(a_ref[...], w_ref[...], _NT)
            o_ref[...] = (da * (2.0 * jnp.maximum(u_ref[...].astype(F32), 0.0))).astype(o_ref.dtype)

    in_specs = [pl.BlockSpec((tm, K), lambda i, j: (i, 0)), pl.BlockSpec((tn, K), lambda i, j: (j, 0))]
    args = [a, w]
    if relu_of is not None:
        in_specs.append(pl.BlockSpec((tm, tn), lambda i, j: (i, j)))
        args.append(relu_of)
    return pl.pallas_call(body, name=name, grid=(T // tm, N // tn), in_specs=in_specs,
                          out_specs=pl.BlockSpec((tm, tn), lambda i, j: (i, j)),
                          out_shape=_sds((T, N), out_dtype),
                          compiler_params=_params("parallel", "parallel"))(*args)


def _matmul_nt_normbwd(name, dy, w, x, gain, dres, *, tm):
    T, K = dy.shape
    N = w.shape[0]
    tm = min(tm, T)

    def body(dy_ref, w_ref, x_ref, g_ref, dres_ref, dx_ref, dg_ref):
        dh = _dot(dy_ref[...], w_ref[...], _NT)
        xv = x_ref[...]
        r = lax.rsqrt(jnp.mean(xv * xv, axis=-1, keepdims=True) + EPS)
        xhat = xv * r
        dxhat = dh * g_ref[...]
        dx_ref[...] = dres_ref[...] + r * (dxhat - xhat * jnp.mean(dxhat * xhat, axis=-1, keepdims=True))

        @pl.when(pl.program_id(0) == 0)
        def _():
            dg_ref[...] = jnp.zeros_like(dg_ref)
        dg_ref[...] += _rowsum8(dh * xhat)

    row = lambda i: (i, 0)
    return pl.pallas_call(
        body, name=name, grid=(T // tm,),
        in_specs=[pl.BlockSpec((tm, K), row), pl.BlockSpec((N, K), lambda i: (0, 0)),
                  pl.BlockSpec((tm, N), row), pl.BlockSpec((1, N), lambda i: (0, 0)), pl.BlockSpec((tm, N), row)],
        out_specs=[pl.BlockSpec((tm, N), row), pl.BlockSpec((8, N), lambda i: (0, 0))],
        out_shape=[_sds((T, N), F32), _sds((8, N), F32)],
        compiler_params=_params("arbitrary"),
    )(dy, w, x, gain, dres)


def _matmul_tn(name, a, b, *, tk, tn, tt, out_dtype, relu2=False):
    T, Ka = a.shape
    Nb = b.shape[1]
    tk, tn, tt = min(tk, Ka), min(tn, Nb), min(tt, T)
    nt = T // tt

    def body(a_ref, b_ref, o_ref, acc):
        t = pl.program_id(2)

        @pl.when(t == 0)
        def _():
            acc[...] = jnp.zeros_like(acc)
        av = a_ref[...]
        if relu2:
            av = jnp.square(jnp.maximum(av.astype(F32), 0.0))
        acc[...] += _dot(av, b_ref[...], _TN)

        @pl.when(t == nt - 1)
        def _():
            o_ref[...] = acc[...].astype(o_ref.dtype)

    return pl.pallas_call(
        body, name=name, grid=(Ka // tk, Nb // tn, nt),
        in_specs=[pl.BlockSpec((tt, tk), lambda i, j, t: (t, i)), pl.BlockSpec((tt, tn), lambda i, j, t: (t, j))],
        out_specs=pl.BlockSpec((tk, tn), lambda i, j, t: (i, j)),
        out_shape=_sds((Ka, Nb), out_dtype),
        scratch_shapes=[pltpu.VMEM((tk, tn), F32)],
        compiler_params=_params("parallel", "parallel", "arbitrary"),
    )(a, b)


def _rope_angles(pos, dim, theta):
    inv = theta ** (-jnp.arange(0, dim, 2, dtype=F32) / dim)
    return pos.astype(F32)[:, None] * inv[None, :]


def _ret_rope_tables(T):
    ang = _rope_angles(jnp.arange(T), RET_DK, RET_THETA)
    c, s = jnp.cos(ang), jnp.sin(ang)
    return jnp.concatenate([c, c], axis=1), jnp.concatenate([-s, s], axis=1)


def _axial_rope_tables(T):
    rows = T // GRID_W
    row = jnp.repeat(jnp.arange(rows), GRID_W)
    col = jnp.tile(jnp.arange(GRID_W), rows)
    ar = _rope_angles(row, HEAD_DIM // 2, AX_THETA)
    ac = _rope_angles(col, HEAD_DIM // 2, AX_THETA)
    cos = jnp.concatenate([jnp.cos(ar), jnp.cos(ar), jnp.cos(ac), jnp.cos(ac)], axis=1)
    sin = jnp.concatenate([-jnp.sin(ar), jnp.sin(ar), -jnp.sin(ac), jnp.sin(ac)], axis=1)
    return cos, sin


(TAB_D, TAB_DT, TAB_EF, TAB_EB, TAB_A, TAB_B, TAB_CF, TAB_CB,
 TAB_RA, TAB_RB, TAB_RCF, TAB_RCB, TAB_KF, TAB_KB) = range(14)


def _retention_tables(decay_logit):
    lg = jax.nn.log_sigmoid(decay_logit.astype(F32))
    lam, mu = lg[0][:, None, None], lg[1][:, None, None]
    idx = jnp.arange(CHUNK, dtype=F32)
    diff = (idx[:, None] - idx[None, :])[None]
    df = jnp.where(diff >= 0, jnp.exp(jnp.maximum(diff, 0.0) * lam), 0.0)
    db = jnp.where(diff < 0, jnp.exp(jnp.maximum(-diff, 0.0) * mu), 0.0)
    d = df + db
    r = idx[None, :, None]
    ones = jnp.ones((1, 1, CHUNK), F32)
    a = jnp.exp((r + 1.0) * lam) * ones
    b = jnp.exp((CHUNK - r) * mu) * ones
    cf = jnp.exp((CHUNK - 1.0 - r) * lam) * ones
    cb = jnp.exp(r * mu) * ones
    full = jnp.ones((1, CHUNK, CHUNK), F32)
    kf = CHUNK * jnp.exp(CHUNK * lam) * full
    kb = CHUNK * jnp.exp(CHUNK * mu) * full
    tabs = jnp.stack([d, jnp.swapaxes(d, 1, 2), diff * df, -diff * db, a, b, cf, cb,
                      (r + 1.0) * a, (CHUNK - r) * b, (CHUNK - 1.0 - r) * cf, r * cb, kf, kb], axis=1)

    def lanes(tab):
        return jnp.transpose(tab, (1, 0, 2)).reshape(CHUNK, RET_HEADS * CHUNK)

    def dec(l):
        return jnp.exp(CHUNK * l)[:, 0, :] * jnp.ones((1, RET_DV), F32)

    weights = dict(a=lanes(a), b=lanes(b), cf=lanes(cf), cb=lanes(cb), dec_f=dec(lam), dec_b=dec(mu))
    return tabs, weights, lg


def _t5_bucket(rel):
    nb = T5_BUCKETS // 2
    max_exact = nb // 2
    ret = jnp.where(rel > 0, nb, 0)
    n = jnp.abs(rel)
    nf = jnp.maximum(n, 1).astype(F32)
    large = max_exact + (jnp.log(nf / max_exact) / math.log(T5_MAX_DIST / max_exact)
                         * (nb - max_exact)).astype(jnp.int32)
    large = jnp.minimum(large, nb - 1)
    return ret + jnp.where(n < max_exact, n, large)


def _swa_rel():
    r = jnp.arange(CHUNK)
    j = jnp.arange(3 * CHUNK)
    return j[None, :] - CHUNK - r[:, None]


def _swa_bias(t5_table):
    rel = _swa_rel()
    bucket = jnp.where(jnp.abs(rel) <= CHUNK, _t5_bucket(rel), -1).astype(jnp.int32)

    def body(tab_ref, bk_ref, o_ref):
        bk = bk_ref[...]
        for h in range(SWA_HEADS):
            pick = lambda b, acc, h=h: jnp.where(bk == b, tab_ref[b, h], acc)
            o_ref[h] = lax.fori_loop(0, T5_BUCKETS, pick, jnp.full(bk.shape, NEG_INF, F32))

    return pl.pallas_call(
        body, name="t5_bias",
        in_specs=[pl.BlockSpec(memory_space=pltpu.SMEM), pl.BlockSpec(memory_space=pltpu.VMEM)],
        out_specs=pl.BlockSpec(memory_space=pltpu.VMEM),
        out_shape=_sds((SWA_HEADS, CHUNK, 3 * CHUNK), F32),
    )(t5_table.astype(F32), bucket)


def _prep_even(proj, cos, sin, q_gain, k_gain, *, tm):
    T = proj.shape[0]
    tm = min(tm, T)

    def body(qa_ref, ka_ref, qb_ref, kb_ref, c_ref, s_ref, qg_ref, kg_ref, qr_ref, kr_ref, qn_ref, kn_ref):
        c = jnp.concatenate([c_ref[...]] * RET_HEADS, axis=1)
        s = jnp.concatenate([s_ref[...]] * RET_HEADS, axis=1)
        qa = qa_ref[...]
        qr_ref[...] = (qa * c + _swap_halves(qa, RET_DK // 2) * s).astype(qr_ref.dtype)
        ka = ka_ref[...]
        kr_ref[...] = ((ka * c + _swap_halves(ka, RET_DK // 2) * s) * (RET_DK ** -0.5)).astype(kr_ref.dtype)
        for src, gain, dst, heads in ((qb_ref, qg_ref, qn_ref, SWA_HEADS), (kb_ref, kg_ref, kn_ref, SWA_KV_HEADS)):
            for h in range(heads):
                sl = slice(h * HEAD_DIM, (h + 1) * HEAD_DIM)
                xh = src[:, sl]
                r = lax.rsqrt(jnp.mean(xh * xh, axis=-1, keepdims=True) + EPS)
                dst[:, sl] = (xh * r * gain[...]).astype(dst.dtype)

    row = lambda i: (i, 0)
    const = lambda i: (0, 0)
    return pl.pallas_call(
        body, name="prep_even", grid=(T // tm,),
        in_specs=[pl.BlockSpec((tm, RET_Q), lambda i: (i, 0)), pl.BlockSpec((tm, RET_Q), lambda i: (i, 1)),
                  pl.BlockSpec((tm, D_MODEL), lambda i: (i, 3)), pl.BlockSpec((tm, 256), lambda i: (i, 16)),
                  pl.BlockSpec((tm, RET_DK), row), pl.BlockSpec((tm, RET_DK), row),
                  pl.BlockSpec((1, HEAD_DIM), const), pl.BlockSpec((1, HEAD_DIM), const)],
        out_specs=[pl.BlockSpec((tm, RET_Q), row), pl.BlockSpec((tm, RET_Q), row),
                   pl.BlockSpec((tm, D_MODEL), row), pl.BlockSpec((tm, 256), row)],
        out_shape=[_sds((T, RET_Q), _MXU), _sds((T, RET_Q), _MXU), _sds((T, D_MODEL), _MXU), _sds((T, 256), _MXU)],
        compiler_params=_params("parallel"),
    )(proj, proj, proj, proj, cos, sin, q_gain, k_gain)


def _ret_scan(name, x, y, y_col, w_asc, dec_asc, w_desc, dec_desc):
    T = x.shape[0]
    nc = T // CHUNK

    def body(xa_ref, ya_ref, xd_ref, yd_ref, wa_ref, da_ref, wd_ref, dd_ref, sa_out, sd_out, sa, sd):
        @pl.when(pl.program_id(0) == 0)
        def _():
            sa[...] = jnp.zeros_like(sa)
            sd[...] = jnp.zeros_like(sd)
        sa_out[0] = sa[...].astype(sa_out.dtype)
        sd_out[0] = sd[...].astype(sd_out.dtype)
        for x_ref, y_ref, w_ref, d_ref, st in ((xa_ref, ya_ref, wa_ref, da_ref, sa), (xd_ref, yd_ref, wd_ref, dd_ref, sd)):
            for h in range(RET_HEADS):
                ks = slice(h * RET_DK, (h + 1) * RET_DK)
                vs = slice(h * RET_DV, (h + 1) * RET_DV)
                u = _dot(x_ref[:, ks].astype(F32) * w_ref[:, ks], y_ref[:, vs], _TN)
                st[ks, :] = st[ks, :] * d_ref[h:h + 1, :] + u

    asc = lambda i: (i, 0)
    desc = lambda i: (nc - 1 - i, 0)
    const = lambda i: (0, 0)
    return pl.pallas_call(
        body, name=name, grid=(nc,),
        in_specs=[pl.BlockSpec((CHUNK, RET_Q), asc), pl.BlockSpec((CHUNK, RET_V), lambda i: (i, y_col)),
                  pl.BlockSpec((CHUNK, RET_Q), desc), pl.BlockSpec((CHUNK, RET_V), lambda i: (nc - 1 - i, y_col)),
                  pl.BlockSpec((CHUNK, RET_Q), const), pl.BlockSpec((RET_HEADS, RET_DV), const),
                  pl.BlockSpec((CHUNK, RET_Q), const), pl.BlockSpec((RET_HEADS, RET_DV), const)],
        out_specs=[pl.BlockSpec((1, RET_Q, RET_DV), lambda i: (i, 0, 0)),
                   pl.BlockSpec((1, RET_Q, RET_DV), lambda i: (nc - 1 - i, 0, 0))],
        out_shape=[_sds((nc, RET_Q, RET_DV), _MXU), _sds((nc, RET_Q, RET_DV), _MXU)],
        scratch_shapes=[pltpu.VMEM((RET_Q, RET_DV), F32), pltpu.VMEM((RET_Q, RET_DV), F32)],
        compiler_params=_params("arbitrary"),
    )(x, y, x, y, w_asc, dec_asc, w_desc, dec_desc)


def _ret_out(qr, kr, proj, sf, sb, tabs, gain):
    T = qr.shape[0]
    nc = T // CHUNK

    def body(q_ref, k_ref, v_ref, g_ref, sf_ref, sb_ref, tab_ref, gain_ref, o_ref, y_ref):
        for h in range(RET_HEADS):
            ks = slice(h * RET_DK, (h + 1) * RET_DK)
            vs = slice(h * RET_DV, (h + 1) * RET_DV)
            q, k, v = q_ref[:, ks], k_ref[:, ks], v_ref[:, vs]
            qf = q.astype(F32)
            a_mat = _dot(q, k, _NT) * tab_ref[h, 0]
            o = (_dot(a_mat, v) + _dot(qf * tab_ref[h, 1], sf_ref[0, ks, :]) + _dot(qf * tab_ref[h, 2], sb_ref[0, ks, :]))
            o_ref[:, vs] = o
            r = lax.rsqrt(jnp.mean(o * o, axis=-1, keepdims=True) + EPS)
            g = g_ref[:, vs]
            y_ref[:, vs] = (g * _sigmoid(g) * (o * r * gain_ref[:, vs])).astype(y_ref.dtype)

    row = lambda i: (i, 0)
    return pl.pallas_call(
        body, name="ret_out", grid=(nc,),
        in_specs=[pl.BlockSpec((CHUNK, RET_Q), row), pl.BlockSpec((CHUNK, RET_Q), row),
                  pl.BlockSpec((CHUNK, RET_V), lambda i: (i, 1)), pl.BlockSpec((CHUNK, RET_V), lambda i: (i, 2)),
                  pl.BlockSpec((1, RET_Q, RET_DV), lambda i: (i, 0, 0)), pl.BlockSpec((1, RET_Q, RET_DV), lambda i: (i, 0, 0)),
                  pl.BlockSpec((RET_HEADS, 3, CHUNK, CHUNK), lambda i: (0, 0, 0, 0)),
                  pl.BlockSpec((1, RET_V), lambda i: (0, 0))],
        out_specs=[pl.BlockSpec((CHUNK, RET_V), row), pl.BlockSpec((CHUNK, RET_V), row)],
        out_shape=[_sds((T, RET_V), F32), _sds((T, RET_V), _MXU)],
        compiler_params=_params("parallel"),
    )(qr, kr, proj, proj, sf, sb, tabs, gain)


def _ret_gate_bwd(dycat, proj, ret_o, gain, *, tm):
    T = ret_o.shape[0]
    tm = min(tm, T)

    def body(dy_ref, g_ref, o_ref, gain_ref, do_ref, dg_ref, dgain_ref):
        @pl.when(pl.program_id(0) == 0)
        def _():
            dgain_ref[...] = jnp.zeros_like(dgain_ref)
        for h in range(RET_HEADS):
            vs = slice(h * RET_DV, (h + 1) * RET_DV)
            o, g, dya, gn = o_ref[:, vs], g_ref[:, vs], dy_ref[:, vs], gain_ref[:, vs]
            r = lax.rsqrt(jnp.mean(o * o, axis=-1, keepdims=True) + EPS)
            ohat = o * r
            sg = _sigmoid(g)
            dy = dya * (g * sg)
            dg_ref[:, vs] = (dya * (ohat * gn) * (sg * (1.0 + g * (1.0 - sg)))).astype(dg_ref.dtype)
            dyg = dy * gn
            do_ref[:, vs] = (r * (dyg - ohat * jnp.mean(dyg * ohat, axis=-1, keepdims=True))).astype(do_ref.dtype)
            dgain_ref[:, vs] += _rowsum8(dy * ohat)

    row = lambda i: (i, 0)
    return pl.pallas_call(
        body, name="ret_gate_bwd", grid=(T // tm,),
        in_specs=[pl.BlockSpec((tm, RET_V), row), pl.BlockSpec((tm, RET_V), lambda i: (i, 2)),
                  pl.BlockSpec((tm, RET_V), row), pl.BlockSpec((1, RET_V), lambda i: (0, 0))],
        out_specs=[pl.BlockSpec((tm, RET_V), row), pl.BlockSpec((tm, RET_V), row), pl.BlockSpec((8, RET_V), lambda i: (0, 0))],
        out_shape=[_sds((T, RET_V), _MXU), _sds((T, RET_V), _MXU), _sds((8, RET_V), F32)],
        compiler_params=_params("arbitrary"),
    )(dycat, proj, ret_o, gain)


def _ret_bwd(qr, kr, proj, g_out, sf, sb, rf, rb, tabs):
    T = qr.shape[0]
    nc = T // CHUNK

    def body(q_ref, k_ref, v_ref, g_ref, sf_ref, sb_ref, rf_ref, rb_ref, tab_ref, dq_ref, dk_ref, dv_ref, dl_ref):
        @pl.when(pl.program_id(0) == 0)
        def _():
            dl_ref[...] = jnp.zeros_like(dl_ref)
        for h in range(RET_HEADS):
            ks = slice(h * RET_DK, (h + 1) * RET_DK)
            vs = slice(h * RET_DV, (h + 1) * RET_DV)
            q, k, v, g = q_ref[:, ks], k_ref[:, ks], v_ref[:, vs], g_ref[:, vs]
            s_f, s_b, r_f, r_b = sf_ref[0, ks, :], sb_ref[0, ks, :], rf_ref[0, ks, :], rb_ref[0, ks, :]
            tab = lambda t: tab_ref[h, t]
            qf, kf = q.astype(F32), k.astype(F32)
            qk = _dot(q, k, _NT)
            da_raw = _dot(g, v, _NT)
            x_f, x_b = _dot(g, s_f, _NT), _dot(g, s_b, _NT)
            dq_ref[:, ks] = _dot(da_raw * tab(TAB_D), k) + tab(TAB_A) * x_f + tab(TAB_B) * x_b
            at = _dot(k, q, _NT) * tab(TAB_DT)
            dat = _dot(v, g, _NT) * tab(TAB_DT)
            y_f, y_b = _dot(v, r_f, _NT), _dot(v, r_b, _NT)
            dk_ref[:, ks] = _dot(dat, q) + tab(TAB_CF) * y_f + tab(TAB_CB) * y_b
            dv_ref[:, vs] = (_dot(at, g) + _dot(kf * tab(TAB_CF), r_f) + _dot(kf * tab(TAB_CB), r_b)).astype(dv_ref.dtype)
            inner = da_raw * qk
            rs_f = r_f.astype(F32) * s_f.astype(F32)
            rs_b = r_b.astype(F32) * s_b.astype(F32)
            dl_f = (inner * tab(TAB_EF) + tab(TAB_RA) * qf * x_f + tab(TAB_RCF) * kf * y_f
                    + tab(TAB_KF) * (rs_f[:, :CHUNK] + rs_f[:, CHUNK:]))
            dl_b = (inner * tab(TAB_EB) + tab(TAB_RB) * qf * x_b + tab(TAB_RCB) * kf * y_b
                    + tab(TAB_KB) * (rs_b[:, :CHUNK] + rs_b[:, CHUNK:]))
            dl_ref[2 * h:2 * h + 1, :] += jnp.sum(dl_f, axis=0, keepdims=True)
            dl_ref[2 * h + 1:2 * h + 2, :] += jnp.sum(dl_b, axis=0, keepdims=True)

    row = lambda i: (i, 0)
    st = lambda i: (i, 0, 0)
    return pl.pallas_call(
        body, name="ret_bwd", grid=(nc,),
        in_specs=[pl.BlockSpec((CHUNK, RET_Q), row), pl.BlockSpec((CHUNK, RET_Q), row),
                  pl.BlockSpec((CHUNK, RET_V), lambda i: (i, 1)), pl.BlockSpec((CHUNK, RET_V), row),
                  pl.BlockSpec((1, RET_Q, RET_DV), st), pl.BlockSpec((1, RET_Q, RET_DV), st),
                  pl.BlockSpec((1, RET_Q, RET_DV), st), pl.BlockSpec((1, RET_Q, RET_DV), st),
                  pl.BlockSpec((RET_HEADS, 14, CHUNK, CHUNK), lambda i: (0, 0, 0, 0))],
        out_specs=[pl.BlockSpec((CHUNK, RET_Q), row), pl.BlockSpec((CHUNK, RET_Q), row),
                   pl.BlockSpec((CHUNK, RET_V), row), pl.BlockSpec((8, CHUNK), lambda i: (0, 0))],
        out_shape=[_sds((T, RET_Q), F32), _sds((T, RET_Q), F32), _sds((T, RET_V), _MXU), _sds((8, CHUNK), F32)],
        compiler_params=_params("arbitrary"),
    )(qr, kr, proj, g_out, sf, sb, rf, rb, tabs)


def _swa_probs(q, k_win, bias, sink, valid):
    s = _dot(q, k_win, _NT) * ATT_SCALE + bias
    s = jnp.where(valid, s, NEG_INF)
    m = jnp.maximum(jnp.max(s, axis=-1, keepdims=True), sink)
    p = jnp.exp(s - m)
    e_sink = jnp.exp(sink - m)
    inv = 1.0 / (jnp.sum(p, axis=-1, keepdims=True) + e_sink)
    return p * inv, e_sink * inv


def _swa_valid(i, nb):
    col = lax.broadcasted_iota(jnp.int32, (1, 3 * CHUNK), 1)
    return jnp.logical_and(jnp.logical_or(col >= CHUNK, i > 0), jnp.logical_or(col < 2 * CHUNK, i < nb - 1))


def _swa_window_specs(nb, width, col_block, clamp):
    prev = lambda i: (jnp.maximum(clamp(i) - 1, 0), col_block)
    cur = lambda i: (clamp(i), col_block)
    nxt = lambda i: (jnp.minimum(clamp(i) + 1, nb - 1), col_block)
    return [pl.BlockSpec((CHUNK, width), f) for f in (prev, cur, nxt)]


def _swa_fwd(qn, kn, proj, bias, sink):
    T = qn.shape[0]
    nb = T // CHUNK
    kvw = SWA_KV_HEADS * HEAD_DIM
    group = SWA_HEADS // SWA_KV_HEADS

    def body(q_ref, k0, k1, k2, v0, v1, v2, bias_ref, sink_ref, y_ref):
        i = pl.program_id(0)
        valid = _swa_valid(i, nb)
        for g in range(SWA_KV_HEADS):
            gs = slice(g * HEAD_DIM, (g + 1) * HEAD_DIM)
            k_win = jnp.concatenate([k0[:, gs], k1[:, gs], k2[:, gs]], axis=0)
            v_win = jnp.concatenate([v0[:, gs], v1[:, gs], v2[:, gs]], axis=0).astype(_MXU)
            for hh in range(group):
                h = g * group + hh
                hs = slice(h * HEAD_DIM, (h + 1) * HEAD_DIM)
                p, _ = _swa_probs(q_ref[:, hs], k_win, bias_ref[h], sink_ref[h:h + 1, 0:1], valid)
                y_ref[:, hs] = _dot(p, v_win).astype(y_ref.dtype)

    ident = lambda i: i
    return pl.pallas_call(
        body, name="swa_fwd", grid=(nb,),
        in_specs=[pl.BlockSpec((CHUNK, D_MODEL), lambda i: (i, 0))]
        + _swa_window_specs(nb, kvw, 0, ident) + _swa_window_specs(nb, kvw, 17, ident)
        + [pl.BlockSpec((SWA_HEADS, CHUNK, 3 * CHUNK), lambda i: (0, 0, 0)), pl.BlockSpec((SWA_HEADS, HEAD_DIM), lambda i: (0, 0))],
        out_specs=pl.BlockSpec((CHUNK, D_MODEL), lambda i: (i, 0)),
        out_shape=_sds((T, D_MODEL), _MXU),
        compiler_params=_params("parallel"),
    )(qn, kn, kn, kn, proj, proj, proj, bias, sink)


def _swa_bwd(qn, kn, proj, dycat, bias, sink):
    T = qn.shape[0]
    nb = T // CHUNK
    kvw = SWA_KV_HEADS * HEAD_DIM
    group = SWA_HEADS // SWA_KV_HEADS

    def body(q_ref, k0, k1, k2, v0, v1, v2, dy_ref, bias_ref, sink_ref,
             dq_ref, dk_ref, dv_ref, dbias_ref, dsink_ref, acc_a, acc_b):
        i = pl.program_id(0)

        @pl.when(i == 0)
        def _():
            dbias_ref[...] = jnp.zeros_like(dbias_ref)
            dsink_ref[...] = jnp.zeros_like(dsink_ref)
            acc_a[...] = jnp.zeros_like(acc_a)
            acc_b[...] = jnp.zeros_like(acc_b)

        @pl.when(i < nb)
        def _():
            valid = _swa_valid(i, nb)
            for g in range(SWA_KV_HEADS):
                gs = slice(g * HEAD_DIM, (g + 1) * HEAD_DIM)
                k_win = jnp.concatenate([k0[:, gs], k1[:, gs], k2[:, gs]], axis=0)
                v_win = jnp.concatenate([v0[:, gs], v1[:, gs], v2[:, gs]], axis=0).astype(_MXU)
                dk_win = jnp.zeros((3 * CHUNK, HEAD_DIM), F32)
                dv_win = jnp.zeros((3 * CHUNK, HEAD_DIM), F32)
                for hh in range(group):
                    h = g * group + hh
                    hs = slice(h * HEAD_DIM, (h + 1) * HEAD_DIM)
                    q, dy = q_ref[:, hs], dy_ref[:, hs]
                    p, p_sink = _swa_probs(q, k_win, bias_ref[h], sink_ref[h:h + 1, 0:1], valid)
                    dp = _dot(dy, v_win, _NT)
                    delta = jnp.sum(p * dp, axis=-1, keepdims=True)
                    ds = p * (dp - delta)
                    dbias_ref[h] += ds
                    dsink_ref[h:h + 1, :] += jnp.sum(-p_sink * delta, axis=0, keepdims=True) * jnp.ones((1, HEAD_DIM), F32)
                    dq_ref[:, hs] = _dot(ds, k_win) * ATT_SCALE
                    dk_win = dk_win + _dot(ds, q, _TN) * ATT_SCALE
                    dv_win = dv_win + _dot(p, dy, _TN)
                for win, out_ref, col0 in ((dk_win, dk_ref, 0), (dv_win, dv_ref, kvw)):
                    cs = slice(col0 + g * HEAD_DIM, col0 + (g + 1) * HEAD_DIM)
                    out_ref[:, gs] = acc_a[:, cs] + win[:CHUNK]
                    acc_a[:, cs] = acc_b[:, cs] + win[CHUNK:2 * CHUNK]
                    acc_b[:, cs] = win[2 * CHUNK:]

        @pl.when(i == nb)
        def _():
            dk_ref[...] = acc_a[:, :kvw]
            dv_ref[...] = acc_a[:, kvw:]

    clamp = lambda i: jnp.minimum(i, nb - 1)
    late = lambda i: (jnp.maximum(i - 1, 0), 0)
    return pl.pallas_call(
        body, name="swa_bwd", grid=(nb + 1,),
        in_specs=[pl.BlockSpec((CHUNK, D_MODEL), lambda i: (clamp(i), 0))]
        + _swa_window_specs(nb, kvw, 0, clamp) + _swa_window_specs(nb, kvw, 17, clamp)
        + [pl.BlockSpec((CHUNK, D_MODEL), lambda i: (clamp(i), 1)),
           pl.BlockSpec((SWA_HEADS, CHUNK, 3 * CHUNK), lambda i: (0, 0, 0)), pl.BlockSpec((SWA_HEADS, HEAD_DIM), lambda i: (0, 0))],
        out_specs=[pl.BlockSpec((CHUNK, D_MODEL), lambda i: (clamp(i), 0)),
                   pl.BlockSpec((CHUNK, kvw), late), pl.BlockSpec((CHUNK, kvw), late),
                   pl.BlockSpec((SWA_HEADS, CHUNK, 3 * CHUNK), lambda i: (0, 0, 0)), pl.BlockSpec((SWA_HEADS, HEAD_DIM), lambda i: (0, 0))],
        out_shape=[_sds((T, D_MODEL), F32), _sds((T, kvw), F32), _sds((T, kvw), F32),
                   _sds((SWA_HEADS, CHUNK, 3 * CHUNK), F32), _sds((SWA_HEADS, HEAD_DIM), F32)],
        scratch_shapes=[pltpu.VMEM((CHUNK, 2 * kvw), F32), pltpu.VMEM((CHUNK, 2 * kvw), F32)],
        compiler_params=_params("arbitrary"),
    )(qn, kn, kn, kn, proj, proj, proj, dycat, bias, sink)


def _t5_bucket_reduce(dbias, bucket):
    def body(db_ref, bk_ref, o_ref):
        bk = bk_ref[...]
        row = lax.broadcasted_iota(jnp.int32, (SWA_HEADS, HEAD_DIM), 0)
        lane = lax.broadcasted_iota(jnp.int32, (SWA_HEADS, HEAD_DIM), 1)

        def per_bucket(b, acc):
            mask = bk == b
            for h in range(SWA_HEADS):
                tot = jnp.sum(jnp.sum(jnp.where(mask, db_ref[h], 0.0), axis=0, keepdims=True), axis=1, keepdims=True)
                acc = acc + jnp.where(jnp.logical_and(row == h, lane == b), tot, 0.0)
            return acc

        o_ref[...] = lax.fori_loop(0, T5_BUCKETS, per_bucket, jnp.zeros((SWA_HEADS, HEAD_DIM), F32))

    return pl.pallas_call(body, name="t5_bucket_reduce", out_shape=_sds((SWA_HEADS, HEAD_DIM), F32),
                          compiler_params=pltpu.CompilerParams(vmem_limit_bytes=VMEM_LIMIT_BYTES))(dbias, bucket)


def _headnorm_bwd(x, dy, gain):
    r = lax.rsqrt(jnp.mean(x * x, axis=-1, keepdims=True) + EPS)
    xhat = x * r
    dyg = dy * gain
    return r * (dyg - xhat * jnp.mean(dyg * xhat, axis=-1, keepdims=True)), dy * xhat


def _post_even(proj, dqr, dkr, dva, dga, dqn, dkn, dvb, cos, sin, q_gain, k_gain, *, tm):
    T = proj.shape[0]
    tm = min(tm, T)
    kvw = SWA_KV_HEADS * HEAD_DIM

    def body(qb_ref, kb_ref, dqr_ref, dkr_ref, dva_ref, dga_ref, dqn_ref, dkn_ref, dvb_ref, c_ref, s_ref, qg_ref, kg_ref,
             dp_ref, dqg_ref, dkg_ref):
        @pl.when(pl.program_id(0) == 0)
        def _():
            dqg_ref[...] = jnp.zeros_like(dqg_ref)
            dkg_ref[...] = jnp.zeros_like(dkg_ref)
        c = jnp.concatenate([c_ref[...]] * RET_HEADS, axis=1)
        s = jnp.concatenate([s_ref[...]] * RET_HEADS, axis=1)
        dq = dqr_ref[...]
        dp_ref[:, 0:RET_Q] = (dq * c + _swap_halves(dq * s, RET_DK // 2)).astype(dp_ref.dtype)
        dk = dkr_ref[...] * (RET_DK ** -0.5)
        dp_ref[:, RET_Q:2 * RET_Q] = (dk * c + _swap_halves(dk * s, RET_DK // 2)).astype(dp_ref.dtype)
        off = 2 * RET_Q
        dp_ref[:, off:off + RET_V] = dva_ref[...].astype(dp_ref.dtype)
        dp_ref[:, off + RET_V:off + 2 * RET_V] = dga_ref[...].astype(dp_ref.dtype)
        off += 2 * RET_V
        for src, dsrc, gain, dgain, heads, base in ((qb_ref, dqn_ref, qg_ref, dqg_ref, SWA_HEADS, off),
                                                    (kb_ref, dkn_ref, kg_ref, dkg_ref, SWA_KV_HEADS, off + D_MODEL)):
            for h in range(heads):
                sl = slice(h * HEAD_DIM, (h + 1) * HEAD_DIM)
                dx, dgx = _headnorm_bwd(src[:, sl], dsrc[:, sl], gain[...])
                dp_ref[:, base + h * HEAD_DIM:base + (h + 1) * HEAD_DIM] = dx.astype(dp_ref.dtype)
                dgain[...] += _rowsum8(dgx)
        dp_ref[:, off + D_MODEL + kvw:] = dvb_ref[...].astype(dp_ref.dtype)

    row = lambda i: (i, 0)
    const = lambda i: (0, 0)
    return pl.pallas_call(
        body, name="post_even", grid=(T // tm,),
        in_specs=[pl.BlockSpec((tm, D_MODEL), lambda i: (i, 3)), pl.BlockSpec((tm, kvw), lambda i: (i, 16)),
                  pl.BlockSpec((tm, RET_Q), row), pl.BlockSpec((tm, RET_Q), row),
                  pl.BlockSpec((tm, RET_V), row), pl.BlockSpec((tm, RET_V), row),
                  pl.BlockSpec((tm, D_MODEL), row), pl.BlockSpec((tm, kvw), row), pl.BlockSpec((tm, kvw), row),
                  pl.BlockSpec((tm, RET_DK), row), pl.BlockSpec((tm, RET_DK), row),
                  pl.BlockSpec((1, HEAD_DIM), const), pl.BlockSpec((1, HEAD_DIM), const)],
        out_specs=[pl.BlockSpec((tm, EVEN_IN), row), pl.BlockSpec((8, HEAD_DIM), const), pl.BlockSpec((8, HEAD_DIM), const)],
        out_shape=[_sds((T, EVEN_IN), _MXU), _sds((8, HEAD_DIM), F32), _sds((8, HEAD_DIM), F32)],
        compiler_params=_params("arbitrary"),
    )(proj, proj, dqr, dkr, dva, dga, dqn, dkn, dvb, cos, sin, q_gain, k_gain)


def _prep_odd(proj, cos, sin, q_gain, k_gain, *, tm):
    T = proj.shape[0]
    tm = min(tm, T)
    kvw = AX_KV_HEADS * HEAD_DIM

    def body(q_ref, k_ref, v_ref, c_ref, s_ref, qg_ref, kg_ref, qx_ref, kx_ref, vx_ref):
        c, s = c_ref[...], s_ref[...]
        for src, gain, dst, heads in ((q_ref, qg_ref, qx_ref, AX_HEADS), (k_ref, kg_ref, kx_ref, AX_KV_HEADS)):
            for h in range(heads):
                sl = slice(h * HEAD_DIM, (h + 1) * HEAD_DIM)
                xh = src[:, sl]
                r = lax.rsqrt(jnp.mean(xh * xh, axis=-1, keepdims=True) + EPS)
                xn = xh * r * gain[...]
                dst[:, sl] = (xn * c + _swap_halves(xn, HEAD_DIM // 4) * s).astype(dst.dtype)
        vx_ref[...] = v_ref[...].astype(vx_ref.dtype)

    row = lambda i: (i, 0)
    const = lambda i: (0, 0)
    return pl.pallas_call(
        body, name="prep_odd", grid=(T // tm,),
        in_specs=[pl.BlockSpec((tm, D_MODEL), row), pl.BlockSpec((tm, kvw), lambda i: (i, 4)), pl.BlockSpec((tm, kvw), lambda i: (i, 5)),
                  pl.BlockSpec((tm, HEAD_DIM), row), pl.BlockSpec((tm, HEAD_DIM), row),
                  pl.BlockSpec((1, HEAD_DIM), const), pl.BlockSpec((1, HEAD_DIM), const)],
        out_specs=[pl.BlockSpec((tm, D_MODEL), row), pl.BlockSpec((tm, kvw), row), pl.BlockSpec((tm, kvw), row)],
        out_shape=[_sds((T, D_MODEL), _MXU), _sds((T, kvw), _MXU), _sds((T, kvw), _MXU)],
        compiler_params=_params("parallel"),
    )(proj, proj, proj, cos, sin, q_gain, k_gain)


def _post_odd(proj, dqxt, dkx, dvx, cos, sin, q_gain, k_gain, *, tm):
    T = proj.shape[0]
    tm = min(tm, T)
    kvw = AX_KV_HEADS * HEAD_DIM

    def body(q_ref, k_ref, dqt_ref, dk_ref, dv_ref, c_ref, s_ref, qg_ref, kg_ref, dp_ref, dqg_ref, dkg_ref):
        @pl.when(pl.program_id(0) == 0)
        def _():
            dqg_ref[...] = jnp.zeros_like(dqg_ref)
            dkg_ref[...] = jnp.zeros_like(dkg_ref)
        c, s = c_ref[...], s_ref[...]
        for src, dsrc, gain, dgain, heads, base in ((q_ref, dqt_ref, qg_ref, dqg_ref, AX_HEADS, 0),
                                                    (k_ref, dk_ref, kg_ref, dkg_ref, AX_KV_HEADS, D_MODEL)):
            for h in range(heads):
                sl = slice(h * HEAD_DIM, (h + 1) * HEAD_DIM)
                d = dsrc[sl, :].T if dsrc is dqt_ref else dsrc[:, sl]
                dn = d * c + _swap_halves(d * s, HEAD_DIM // 4)
                dx, dgx = _headnorm_bwd(src[:, sl], dn, gain[...])
                dp_ref[:, base + h * HEAD_DIM:base + (h + 1) * HEAD_DIM] = dx.astype(dp_ref.dtype)
                dgain[...] += _rowsum8(dgx)
        dp_ref[:, D_MODEL + kvw:] = dv_ref[...].astype(dp_ref.dtype)

    row = lambda i: (i, 0)
    const = lambda i: (0, 0)
    return pl.pallas_call(
        body, name="post_odd", grid=(T // tm,),
        in_specs=[pl.BlockSpec((tm, D_MODEL), row), pl.BlockSpec((tm, kvw), lambda i: (i, 4)),
                  pl.BlockSpec((D_MODEL, tm), lambda i: (0, i)), pl.BlockSpec((tm, kvw), row), pl.BlockSpec((tm, kvw), row),
                  pl.BlockSpec((tm, HEAD_DIM), row), pl.BlockSpec((tm, HEAD_DIM), row),
                  pl.BlockSpec((1, HEAD_DIM), const), pl.BlockSpec((1, HEAD_DIM), const)],
        out_specs=[pl.BlockSpec((tm, ODD_IN), row), pl.BlockSpec((8, HEAD_DIM), const), pl.BlockSpec((8, HEAD_DIM), const)],
        out_shape=[_sds((T, ODD_IN), _MXU), _sds((8, HEAD_DIM), F32), _sds((8, HEAD_DIM), F32)],
        compiler_params=_params("arbitrary"),
    )(proj, proj, dqxt, dkx, dvx, cos, sin, q_gain, k_gain)


SCORE_SCALE_LOG2 = ATT_SCALE * math.log2(math.e)


def _flash_fwd(qx, kx, vx, *, tq, tk):
    T = qx.shape[0]
    tq, tk = min(tq, T), min(tk, T)
    group = AX_HEADS // AX_KV_HEADS

    def body(q_ref, k_ref, v_ref, o_ref, lse_ref):
        q = q_ref[...]

        def step(j, carry):
            m, l, acc = carry
            off = pl.multiple_of(j * tk, tk)
            k, v = k_ref[pl.ds(off, tk), :], v_ref[pl.ds(off, tk), :]
            st = _dot(k, q, _NT) * SCORE_SCALE_LOG2
            m_new = jnp.maximum(m, jnp.max(st, axis=0, keepdims=True))
            p = jnp.exp2(st - m_new)
            alpha = jnp.exp2(m - m_new)
            return m_new, alpha * l + jnp.sum(p, axis=0, keepdims=True), alpha * acc + _dot(v, p, _TN)

        init = (jnp.full((1, tq), NEG_INF, F32), jnp.zeros((1, tq), F32), jnp.zeros((HEAD_DIM, tq), F32))
        m, l, acc = lax.fori_loop(0, T // tk, step, init)
        o_ref[...] = (acc / l).T.astype(o_ref.dtype)
        lse_ref[0] = m + jnp.log2(l)

    return pl.pallas_call(
        body, name="flash_fwd", grid=(AX_HEADS, T // tq),
        in_specs=[pl.BlockSpec((tq, HEAD_DIM), lambda h, i: (i, h)),
                  pl.BlockSpec((T, HEAD_DIM), lambda h, i: (0, h // group)), pl.BlockSpec((T, HEAD_DIM), lambda h, i: (0, h // group))],
        out_specs=[pl.BlockSpec((tq, HEAD_DIM), lambda h, i: (i, h)), pl.BlockSpec((1, 1, tq), lambda h, i: (h, 0, i))],
        out_shape=[_sds((T, D_MODEL), _MXU), _sds((AX_HEADS, 1, T), F32)],
        compiler_params=_params("parallel", "parallel"),
    )(qx, kx, vx)


def _flash_delta(o, do, *, tq):
    T = o.shape[0]
    tq = min(tq, T)

    def body(o_ref, do_ref, delta_ref):
        delta_ref[0] = jnp.sum(do_ref[...].astype(F32) * o_ref[...].astype(F32), axis=-1, keepdims=True)

    blk = lambda h, i: (i, h)
    return pl.pallas_call(
        body, name="flash_delta", grid=(AX_HEADS, T // tq),
        in_specs=[pl.BlockSpec((tq, HEAD_DIM), blk), pl.BlockSpec((tq, HEAD_DIM), blk)],
        out_specs=pl.BlockSpec((1, tq, 1), lambda h, i: (h, i, 0)),
        out_shape=_sds((AX_HEADS, T, 1), F32),
        compiler_params=_params("parallel", "parallel"),
    )(o, do)


def _flash_bwd(qx, kx, vx, do, lse, delta, *, tq, tk):
    T = qx.shape[0]
    tq, tk = min(tq, T), min(tk, T)
    nq = T // tq
    group = AX_HEADS // AX_KV_HEADS
    lse_rows = lse.reshape(AX_HEADS, nq, 1, tq)
    delta_rows = delta.reshape(AX_HEADS, nq, 1, tq)

    def body(k_ref, v_ref, q_ref, do_ref, lse_ref, delta_ref, dqt_ref, dk_ref, dv_ref):
        j = pl.program_id(2)

        @pl.when(jnp.logical_and(pl.program_id(1) == 0, j == 0))
        def _():
            dk_ref[...] = jnp.zeros_like(dk_ref)
            dv_ref[...] = jnp.zeros_like(dv_ref)

        @pl.when(j == 0)
        def _():
            dqt_ref[...] = jnp.zeros_like(dqt_ref)
        k, v = k_ref[...], v_ref[...]

        def step(i, carry):
            dk, dv = carry
            off = pl.multiple_of(i * tq, tq)
            q, do_blk = q_ref[pl.ds(off, tq), :], do_ref[pl.ds(off, tq), :]
            pt = jnp.exp2(_dot(k, q, _NT) * SCORE_SCALE_LOG2 - lse_ref[0, i])
            dst = pt * (_dot(v, do_blk, _NT) - delta_ref[0, i])
            dqt_ref[:, pl.ds(off, tq)] += _dot(k, dst, _TN) * ATT_SCALE
            return dk + _dot(dst, q), dv + _dot(pt, do_blk)

        zero = jnp.zeros((tk, HEAD_DIM), F32)
        dk, dv = lax.fori_loop(0, nq, step, (zero, zero))
        rows = pl.ds(pl.multiple_of(j * tk, tk), tk)
        dk_ref[rows, :] += dk * ATT_SCALE
        dv_ref[rows, :] += dv

    kv = lambda g, h, j: (j, g)
    qh = lambda g, h, j: (0, g * group + h)
    st = lambda g, h, j: (g * group + h, 0, 0, 0)
    acc = lambda g, h, j: (0, g)
    return pl.pallas_call(
        body, name="flash_bwd", grid=(AX_KV_HEADS, group, T // tk),
        in_specs=[pl.BlockSpec((tk, HEAD_DIM), kv), pl.BlockSpec((tk, HEAD_DIM), kv),
                  pl.BlockSpec((T, HEAD_DIM), qh), pl.BlockSpec((T, HEAD_DIM), qh),
                  pl.BlockSpec((1, nq, 1, tq), st), pl.BlockSpec((1, nq, 1, tq), st)],
        out_specs=[pl.BlockSpec((HEAD_DIM, T), lambda g, h, j: (g * group + h, 0)),
                   pl.BlockSpec((T, HEAD_DIM), acc), pl.BlockSpec((T, HEAD_DIM), acc)],
        out_shape=[_sds((D_MODEL, T), F32), _sds((T, AX_KV_HEADS * HEAD_DIM), F32), _sds((T, AX_KV_HEADS * HEAD_DIM), F32)],
        compiler_params=_params("parallel", "arbitrary", "arbitrary"),
    )(kx, vx, qx, do, lse_rows, delta_rows)


TM = 1024
TM_WIDE = 512


def _mlp_fwd(tag, x, gain, w_up, w_down, target=None):
    u, h = _norm_matmul(f"mlp_up{tag}", x, gain, w_up, tm=TM, tn=1024, out_dtype=F32)
    out = _matmul_res(f"mlp_down{tag}", [u], w_down, x, tm=TM_WIDE, relu2=True, target=target)
    return out, (x, u, h)


def _mlp_bwd(tag, saved, gain, w_up, w_down, dy):
    x, u, h = saved
    du = _matmul_nt(f"mlp_down{tag}_bwd", dy, w_down, tm=TM, tn=1024, out_dtype=_MXU, relu_of=u)
    dw_down = _matmul_tn(f"mlp_down{tag}_dw", u, dy, tk=1024, tn=1024, tt=1024, out_dtype=_WIRE, relu2=True)
    dx, dgain = _matmul_nt_normbwd(f"mlp_up{tag}_bwd", du, w_up, x, gain, dy, tm=TM_WIDE)
    dw_up = _matmul_tn(f"mlp_up{tag}_dw", h, du, tk=1024, tn=1024, tt=1024, out_dtype=_WIRE)
    return dx, dgain, dw_up, dw_down


def _local_step(x, target, p, wf):
    T = x.shape[0]
    cos_r, sin_r = _ret_rope_tables(T)
    cos_a, sin_a = _axial_rope_tables(T)
    tabs, rw, log_gamma = _retention_tables(p["ret_decay_logit"][0])
    bias = _swa_bias(p["t5_table"])
    sink = p["swa_sink"][0][:, None] * jnp.ones((1, HEAD_DIM), F32)
    nm, nl = p["norm_mix"], p["norm_mlp"]

    proj0, h0 = _norm_matmul("in_even", x, nm[0:1], wf["w_in_even"], tm=TM, tn=1152, out_dtype=F32)
    qr, kr, qn, kn = _prep_even(proj0, cos_r, sin_r, p["swa_q_norm"], p["swa_k_norm"], tm=TM)
    sf, sb = _ret_scan("ret_scan_fwd", kr, proj0, 1, rw["cf"], rw["dec_f"], rw["cb"], rw["dec_b"])
    ret_o, ya = _ret_out(qr, kr, proj0, sf, sb, tabs[:, (TAB_D, TAB_A, TAB_B)], p["ret_norm"])
    yb = _swa_fwd(qn, kn, proj0, bias, sink)
    x1 = _matmul_res("out_even", [ya, yb], wf["w_out_even"], x, tm=TM)
    x2, mlp0 = _mlp_fwd(0, x1, nl[0:1], wf["w_mlp_up"][0], wf["w_mlp_down"][0])
    proj1, h1 = _norm_matmul("in_odd", x2, nm[1:2], wf["w_in_odd"], tm=TM, tn=768, out_dtype=F32)
    qx, kx, vx = _prep_odd(proj1, cos_a, sin_a, p["ax_q_norm"], p["ax_k_norm"], tm=TM)
    o, lse = _flash_fwd(qx, kx, vx, tq=512, tk=1024)
    x3 = _matmul_res("out_odd", [o], wf["w_out_odd"], x2, tm=TM)
    (g4, loss_part), mlp1 = _mlp_fwd(1, x3, nl[1:2], wf["w_mlp_up"][1], wf["w_mlp_down"][1], target=target)

    dx3, dnl1, dw_up1, dw_down1 = _mlp_bwd(1, mlp1, nl[1:2], wf["w_mlp_up"][1], wf["w_mlp_down"][1], g4)
    do = _matmul_nt("out_odd_bwd", dx3, wf["w_out_odd"], tm=TM, tn=1024, out_dtype=_MXU)
    dw_out_odd = _matmul_tn("out_odd_dw", o, dx3, tk=1024, tn=1024, tt=1024, out_dtype=_WIRE)
    delta = _flash_delta(o, do, tq=1024)
    dqxt, dkx, dvx = _flash_bwd(qx, kx, vx, do, lse, delta, tq=1024, tk=512)
    dproj1, dqg1, dkg1 = _post_odd(proj1, dqxt, dkx, dvx, cos_a, sin_a, p["ax_q_norm"], p["ax_k_norm"], tm=TM)
    dx2, dnm1 = _matmul_nt_normbwd("in_odd_bwd", dproj1, wf["w_in_odd"], x2, nm[1:2], dx3, tm=TM_WIDE)
    dw_in_odd = _matmul_tn("in_odd_dw", h1, dproj1, tk=1024, tn=768, tt=1024, out_dtype=_WIRE)
    dx1, dnl0, dw_up0, dw_down0 = _mlp_bwd(0, mlp0, nl[0:1], wf["w_mlp_up"][0], wf["w_mlp_down"][0], dx2)
    dycat = _matmul_nt("out_even_bwd", dx1, wf["w_out_even"], tm=TM, tn=1024, out_dtype=F32)
    dw_out_even = jnp.concatenate([_matmul_tn("out_even_dw_ret", ya, dx1, tk=1024, tn=1024, tt=1024, out_dtype=_WIRE),
                                   _matmul_tn("out_even_dw_swa", yb, dx1, tk=1024, tn=1024, tt=1024, out_dtype=_WIRE)], axis=0)
    g_out, dga, dretg = _ret_gate_bwd(dycat, proj0, ret_o, p["ret_norm"], tm=TM)
    rb, rf = _ret_scan("ret_scan_bwd", qr, g_out, 0, rw["b"], rw["dec_b"], rw["a"], rw["dec_f"])
    dqr, dkr, dva, dlog = _ret_bwd(qr, kr, proj0, g_out, sf, sb, rf, rb, tabs)
    dqn, dkn, dvb, dbias, dsink = _swa_bwd(qn, kn, proj0, dycat, bias, sink)
    dt5 = _t5_bucket_reduce(dbias, _t5_bucket(_swa_rel()).astype(jnp.int32))
    dproj0, dqg0, dkg0 = _post_even(proj0, dqr, dkr, dva, dga, dqn, dkn, dvb, cos_r, sin_r,
                                    p["swa_q_norm"], p["swa_k_norm"], tm=TM_WIDE)
    dx0, dnm0 = _matmul_nt_normbwd("in_even_bwd", dproj0, wf["w_in_even"], x, nm[0:1], dx1, tm=TM_WIDE)
    dw_in_even = _matmul_tn("in_even_dw", h0, dproj0, tk=1024, tn=1152, tt=1024, out_dtype=_WIRE)

    big = {
        "w_in_even": dw_in_even[None], "w_out_even": dw_out_even[None],
        "w_in_odd": dw_in_odd[None], "w_out_odd": dw_out_odd[None],
        "w_mlp_up": jnp.stack([dw_up0, dw_up1]), "w_mlp_down": jnp.stack([dw_down0, dw_down1]),
    }
    fold = lambda part: jnp.sum(part, axis=0)
    dlam = jnp.sum(dlog, axis=1).reshape(RET_HEADS, 2).T
    small = {
        "norm_mix": jnp.stack([fold(dnm0), fold(dnm1)]),
        "norm_mlp": jnp.stack([fold(dnl0), fold(dnl1)]),
        "ret_decay_logit": (dlam * (1.0 - jnp.exp(log_gamma)))[None],
        "ret_norm": fold(dretg)[None],
        "swa_q_norm": fold(dqg0)[None], "swa_k_norm": fold(dkg0)[None],
        "swa_sink": dsink[:, 0][None],
        "t5_table": dt5[:, :T5_BUCKETS].T,
        "ax_q_norm": fold(dqg1)[None], "ax_k_norm": fold(dkg1)[None],
    }
    return loss_part, dx0, big, small


BIG = ("w_in_even", "w_out_even", "w_in_odd", "w_out_odd", "w_mlp_up", "w_mlp_down")
SMALL = ("norm_mix", "norm_mlp", "ret_decay_logit", "ret_norm", "swa_q_norm", "swa_k_norm", "swa_sink", "t5_table",
         "ax_q_norm", "ax_k_norm")
WEIGHTS = ("norm_mix", "norm_mlp", "w_in_even", "w_out_even", "ret_decay_logit", "ret_norm", "swa_q_norm", "swa_k_norm",
           "swa_sink", "t5_table", "w_in_odd", "w_out_odd", "ax_q_norm", "ax_k_norm", "w_mlp_up", "w_mlp_down")
SHARD_AXIS = {"w_in_even": 2, "w_out_even": 1, "w_in_odd": 2, "w_out_odd": 1, "w_mlp_up": 2, "w_mlp_down": 1}
HALF_AXIS = {"w_in_even": 1, "w_out_even": 1, "w_in_odd": 1, "w_out_odd": 1, "w_mlp_up": 0, "w_mlp_down": 0}
N_CHIPS = 4
EXCHANGE_CHUNKS = 4
ANY = pl.BlockSpec(memory_space=pl.ANY)


def _mesh_pos():
    return lax.axis_index("x"), lax.axis_index("y"), lax.axis_index("c")


def _window(ref, axis, start, size):
    idx = [slice(None)] * len(ref.shape)
    idx[axis] = pl.ds(start, size)
    return ref.at[tuple(idx)]


def _cast_mxu(name, w, *, tr=256):
    R, C = w.shape
    tr = min(tr, R)

    def body(w_ref, o_ref):
        o_ref[...] = w_ref[...].astype(o_ref.dtype)

    return pl.pallas_call(body, name=name, grid=(R // tr,), in_specs=[pl.BlockSpec((tr, C), lambda i: (i, 0))],
                          out_specs=pl.BlockSpec((tr, C), lambda i: (i, 0)), out_shape=_sds((R, C), _MXU),
                          compiler_params=_params("parallel"))(w)


def _allgather_weights(shards):
    names = list(shards)
    n = len(names)
    sizes = [shards[k].shape[SHARD_AXIS[k]] for k in names]

    def body(*refs):
        ins, outs = refs[:n], refs[n:2 * n]
        local_sems, send_sems, recv_sems = refs[2 * n:]
        x, y, c = _mesh_pos()
        chips = [(1 - x, y), (x, 1 - y), (1 - x, 1 - y)]
        slot = lambda t, px, py: _window(outs[t], SHARD_AXIS[names[t]], pl.multiple_of((2 * px + py) * sizes[t], 128), sizes[t])
        local, remote = [], []
        for t in range(n):
            local.append(pltpu.make_async_copy(ins[t], slot(t, x, y), local_sems.at[t]))
            local[-1].start()
            for k, (px, py) in enumerate(chips):
                remote.append(pltpu.make_async_remote_copy(ins[t], slot(t, x, y), send_sems.at[3 * t + k], recv_sems.at[3 * t + k],
                                                           device_id=(px, py, c), device_id_type=MESH))
                remote[-1].start()
        for t in range(n):
            for k, (px, py) in enumerate(chips):
                pltpu.make_async_remote_copy(ins[t], slot(t, px, py), send_sems.at[3 * t + k], recv_sems.at[3 * t + k],
                                             device_id=(px, py, c), device_id_type=MESH).wait_recv()
        for cp in remote:
            cp.wait_send()
        for cp in local:
            cp.wait()

    full_shape = lambda k: tuple(d * (N_CHIPS if a == SHARD_AXIS[k] else 1) for a, d in enumerate(shards[k].shape))
    outs = pl.pallas_call(
        body, name="allgather_weights", in_specs=[ANY] * n, out_specs=[ANY] * n,
        out_shape=[_sds(full_shape(k), shards[k].dtype) for k in names],
        scratch_shapes=[pltpu.SemaphoreType.DMA((n,)), pltpu.SemaphoreType.DMA((3 * n,)), pltpu.SemaphoreType.DMA((3 * n,))],
    )(*[shards[k] for k in names])
    return dict(zip(names, outs))


FLIPS = [(a, b, d) for a in (0, 1) for b in (0, 1) for d in (0, 1) if (a, b, d) != (0, 0, 0)]


def _flip(pos, f):
    return tuple(1 - p if fi else p for p, fi in zip(pos, f))


def _piece_shape(name, shape):
    out = list(shape)
    out[SHARD_AXIS[name]] //= N_CHIPS
    out[HALF_AXIS[name]] //= 2
    return tuple(out)


def _piece(ref, name, chip, core, shard_size, half_size):
    sa, ha = SHARD_AXIS[name], HALF_AXIS[name]
    if sa == ha:
        return _window(ref, sa, pl.multiple_of(chip * shard_size + core * half_size, 8), half_size)
    half_start = pl.multiple_of(core * half_size, 8) if ha else core * half_size
    return _window(_window(ref, sa, pl.multiple_of(chip * shard_size, 128), shard_size), ha, half_start, half_size)


def _scatter_gradients(grads):
    names = list(grads)
    n = len(names)
    n_peer = len(FLIPS)
    pieces = [_piece_shape(k, grads[k].shape) for k in names]
    shard_sizes = [grads[k].shape[SHARD_AXIS[k]] // N_CHIPS for k in names]
    half_sizes = [p[HALF_AXIS[k]] for k, p in zip(names, pieces)]

    def body(*refs):
        ins, outs = refs[:n], refs[n:2 * n]
        local_sems, send_sems, recv_sems = refs[2 * n:]
        pos = _mesh_pos()
        ident = lambda p: 4 * p[0] + 2 * p[1] + p[2]
        me = ident(pos)
        src = lambda t, p: _piece(ins[t], names[t], 2 * p[0] + p[1], p[2], shard_sizes[t], half_sizes[t])
        local, remote = [], []
        for t in range(n):
            local.append(pltpu.make_async_copy(src(t, pos), outs[t].at[me], local_sems.at[t]))
            local[-1].start()
            for k, f in enumerate(FLIPS):
                peer = _flip(pos, f)
                remote.append(pltpu.make_async_remote_copy(src(t, peer), outs[t].at[me], send_sems.at[n_peer * t + k],
                                                           recv_sems.at[n_peer * t + k], device_id=peer, device_id_type=MESH))
                remote[-1].start()
        for t in range(n):
            for k, f in enumerate(FLIPS):
                peer = _flip(pos, f)
                pltpu.make_async_remote_copy(src(t, pos), outs[t].at[ident(peer)], send_sems.at[n_peer * t + k],
                                             recv_sems.at[n_peer * t + k], device_id=peer, device_id_type=MESH).wait_recv()
        for cp in remote:
            cp.wait_send()
        for cp in local:
            cp.wait()

    outs = pl.pallas_call(
        body, name="scatter_gradients", in_specs=[ANY] * n, out_specs=[ANY] * n,
        out_shape=[_sds((N_DEV,) + p, grads[k].dtype) for k, p in zip(names, pieces)],
        scratch_shapes=[pltpu.SemaphoreType.DMA((n,)), pltpu.SemaphoreType.DMA((n_peer * n,)), pltpu.SemaphoreType.DMA((n_peer * n,))],
    )(*[grads[k] for k in names])
    return dict(zip(names, outs))


def _sum_slots(name, buf, *, tr=128):
    _, L, R, C = buf.shape
    tr = min(tr, R)

    def body(b_ref, o_ref):
        acc = b_ref[0].astype(F32)
        for s in range(1, N_DEV):
            acc = acc + b_ref[s].astype(F32)
        o_ref[...] = acc

    return pl.pallas_call(body, name=name, grid=(L, R // tr),
                          in_specs=[pl.BlockSpec((N_DEV, 1, tr, C), lambda l, i: (0, l, i, 0))],
                          out_specs=pl.BlockSpec((1, tr, C), lambda l, i: (l, i, 0)), out_shape=_sds((L, R, C), F32),
                          compiler_params=_params("parallel", "parallel"))(buf)


def _exchange_halves(halves, shard_shapes):
    names = list(halves)
    n = len(names)
    half_sizes = [halves[k].shape[HALF_AXIS[k]] for k in names]

    def body(*refs):
        ins, outs = refs[:n], refs[n:2 * n]
        local_sems, send_sems, recv_sems = refs[2 * n:]
        x, y, c = _mesh_pos()
        half = lambda t, core: _window(outs[t], HALF_AXIS[names[t]], pl.multiple_of(core * half_sizes[t], 8)
                                       if HALF_AXIS[names[t]] else core * half_sizes[t], half_sizes[t])
        rows = lambda t: ins[t].shape[1] // EXCHANGE_CHUNKS
        src = lambda t, q: _window(ins[t], 1, q * rows(t), rows(t))

        def dst(t, core, q):
            if HALF_AXIS[names[t]] == 1:
                return _window(outs[t], 1, pl.multiple_of(core * half_sizes[t] + q * rows(t), 8), rows(t))
            return _window(half(t, core), 1, q * rows(t), rows(t))

        local, remote = [], []
        for t in range(n):
            for q in range(EXCHANGE_CHUNKS):
                s = EXCHANGE_CHUNKS * t + q
                local.append(pltpu.make_async_copy(src(t, q), dst(t, c, q), local_sems.at[s]))
                local[-1].start()
                remote.append(pltpu.make_async_remote_copy(src(t, q), dst(t, c, q), send_sems.at[s], recv_sems.at[s],
                                                           device_id=(x, y, 1 - c), device_id_type=MESH))
                remote[-1].start()
        for t in range(n):
            for q in range(EXCHANGE_CHUNKS):
                s = EXCHANGE_CHUNKS * t + q
                pltpu.make_async_remote_copy(src(t, q), dst(t, 1 - c, q), send_sems.at[s], recv_sems.at[s],
                                             device_id=(x, y, 1 - c), device_id_type=MESH).wait_recv()
        for cp in remote:
            cp.wait_send()
        for cp in local:
            cp.wait()

    n_sem = EXCHANGE_CHUNKS * n
    outs = pl.pallas_call(
        body, name="exchange_halves", in_specs=[ANY] * n, out_specs=[ANY] * n,
        out_shape=[_sds(shard_shapes[k], F32) for k in names],
        scratch_shapes=[pltpu.SemaphoreType.DMA((n_sem,)), pltpu.SemaphoreType.DMA((n_sem,)), pltpu.SemaphoreType.DMA((n_sem,))],
    )(*[halves[k] for k in names])
    return dict(zip(names, outs))


def _adamw_math(w, g, m, v):
    m = ADAM_B1 * m + (1.0 - ADAM_B1) * g
    v = ADAM_B2 * v + (1.0 - ADAM_B2) * jnp.square(g)
    m_hat = m / (1.0 - ADAM_B1 ** ADAM_STEP)
    v_hat = v / (1.0 - ADAM_B2 ** ADAM_STEP)
    return -ADAM_LR * (m_hat / (jnp.sqrt(v_hat) + ADAM_EPS) + ADAM_WD * w), m, v


def _adamw(name, w, g, m, v, *, tr=256):
    R, C = w.shape
    tr = min(tr, R)

    def body(w_ref, g_ref, m_ref, v_ref, d_ref, mo_ref, vo_ref):
        d_ref[...], mo_ref[...], vo_ref[...] = _adamw_math(w_ref[...], g_ref[...], m_ref[...], v_ref[...])

    spec = pl.BlockSpec((tr, C), lambda i: (i, 0))
    return pl.pallas_call(body, name=name, grid=(R // tr,), in_specs=[spec] * 4, out_specs=[spec] * 3,
                          out_shape=[_sds((R, C), F32)] * 3, compiler_params=_params("parallel"))(w, g, m, v)


SLAB_ROWS = 8
LOSS_ROW = 7


def _pack_small(d):
    pad = lambda a, width: jnp.pad(a.reshape(-1), (0, width - a.size))
    row5 = jnp.concatenate([d["swa_q_norm"].reshape(-1), d["swa_k_norm"].reshape(-1), d["ax_q_norm"].reshape(-1),
                            d["ax_k_norm"].reshape(-1), pad(d["swa_sink"], HEAD_DIM), pad(d["ret_decay_logit"], HEAD_DIM),
                            jnp.zeros((2 * HEAD_DIM,), F32)])
    return jnp.concatenate([d["norm_mix"], d["norm_mlp"], d["ret_norm"], row5[None], pad(d["t5_table"], D_MODEL)[None],
                            jnp.zeros((1, D_MODEL), F32)], axis=0)


def _unpack_small(slab):
    r5 = slab[5]
    return {
        "norm_mix": slab[0:2], "norm_mlp": slab[2:4], "ret_norm": slab[4:5],
        "swa_q_norm": r5[None, 0:128], "swa_k_norm": r5[None, 128:256], "ax_q_norm": r5[None, 256:384],
        "ax_k_norm": r5[None, 384:512], "swa_sink": r5[None, 512:512 + SWA_HEADS],
        "ret_decay_logit": r5[640:640 + 2 * RET_HEADS].reshape(1, 2, RET_HEADS),
        "t5_table": slab[6, :T5_BUCKETS * SWA_HEADS].reshape(T5_BUCKETS, SWA_HEADS),
    }


def _small_allreduce_adamw(g_slab, w_slab, m_slab, v_slab, loss_part):
    def body(g_ref, w_ref, m_ref, v_ref, lp_ref, go_ref, d_ref, mo_ref, vo_ref, gath, send_sems, recv_sems):
        pos = _mesh_pos()
        ident = lambda p: 4 * p[0] + 2 * p[1] + p[2]
        me = ident(pos)
        row = lax.broadcasted_iota(jnp.int32, (SLAB_ROWS, D_MODEL), 0)
        lane = lax.broadcasted_iota(jnp.int32, (SLAB_ROWS, D_MODEL), 1)
        loss = jnp.sum(jnp.sum(lp_ref[...], axis=0, keepdims=True), axis=1, keepdims=True) * (0.5 / D_MODEL)
        gath[me] = jnp.where(jnp.logical_and(row == LOSS_ROW, lane == 0), loss, g_ref[...])
        sends = []
        for k, f in enumerate(FLIPS):
            sends.append(pltpu.make_async_remote_copy(gath.at[me], gath.at[me], send_sems.at[k], recv_sems.at[k],
                                                      device_id=_flip(pos, f), device_id_type=MESH))
            sends[-1].start()
        for k, f in enumerate(FLIPS):
            peer = _flip(pos, f)
            pltpu.make_async_remote_copy(gath.at[me], gath.at[ident(peer)], send_sems.at[k], recv_sems.at[k],
                                         device_id=peer, device_id_type=MESH).wait_recv()
        for cp in sends:
            cp.wait_send()
        total = gath[0]
        for s in range(1, N_DEV):
            total = total + gath[s]
        go_ref[...] = total
        d_ref[...], mo_ref[...], vo_ref[...] = _adamw_math(w_ref[...], total, m_ref[...], v_ref[...])

    vmem = pl.BlockSpec(memory_space=pltpu.VMEM)
    return pl.pallas_call(
        body, name="small_allreduce_adamw", in_specs=[vmem] * 5, out_specs=[vmem] * 4,
        out_shape=[_sds((SLAB_ROWS, D_MODEL), F32)] * 4,
        scratch_shapes=[pltpu.VMEM((N_DEV, SLAB_ROWS, D_MODEL), F32),
                        pltpu.SemaphoreType.DMA((len(FLIPS),)), pltpu.SemaphoreType.DMA((len(FLIPS),))],
    )(g_slab, w_slab, m_slab, v_slab, loss_part)


def kernel(x, norm_mix, norm_mlp, w_in_even, w_out_even, ret_decay_logit, ret_norm, swa_q_norm, swa_k_norm, swa_sink, t5_table, w_in_odd, w_out_odd, ax_q_norm, ax_k_norm, w_mlp_up, w_mlp_down, loss_target, m_norm_mix, m_norm_mlp, m_w_in_even, m_w_out_even, m_ret_decay_logit, m_ret_norm, m_swa_q_norm, m_swa_k_norm, m_swa_sink, m_t5_table, m_w_in_odd, m_w_out_odd, m_ax_q_norm, m_ax_k_norm, m_w_mlp_up, m_w_mlp_down, v_norm_mix, v_norm_mlp, v_w_in_even, v_w_out_even, v_ret_decay_logit, v_ret_norm, v_swa_q_norm, v_swa_k_norm, v_swa_sink, v_t5_table, v_w_in_odd, v_w_out_odd, v_ax_q_norm, v_ax_k_norm, v_w_mlp_up, v_w_mlp_down):
    w = dict(zip(WEIGHTS, (norm_mix, norm_mlp, w_in_even, w_out_even, ret_decay_logit, ret_norm, swa_q_norm, swa_k_norm,
                           swa_sink, t5_table, w_in_odd, w_out_odd, ax_q_norm, ax_k_norm, w_mlp_up, w_mlp_down)))
    m = dict(zip(WEIGHTS, (m_norm_mix, m_norm_mlp, m_w_in_even, m_w_out_even, m_ret_decay_logit, m_ret_norm, m_swa_q_norm,
                           m_swa_k_norm, m_swa_sink, m_t5_table, m_w_in_odd, m_w_out_odd, m_ax_q_norm, m_ax_k_norm,
                           m_w_mlp_up, m_w_mlp_down)))
    v = dict(zip(WEIGHTS, (v_norm_mix, v_norm_mlp, v_w_in_even, v_w_out_even, v_ret_decay_logit, v_ret_norm, v_swa_q_norm,
                           v_swa_k_norm, v_swa_sink, v_t5_table, v_w_in_odd, v_w_out_odd, v_ax_q_norm, v_ax_k_norm,
                           v_w_mlp_up, v_w_mlp_down)))
    flat = lambda a: a.reshape(-1, a.shape[-1])

    shards = {k: _cast_mxu(f"cast_{k}", flat(w[k])).reshape(w[k].shape) for k in BIG}
    whole = _allgather_weights(shards)
    wf = {k: (whole[k] if k.startswith("w_mlp") else whole[k][0]) for k in BIG}

    loss_part, dx, big_g, small_g = _local_step(x[0], loss_target[0], {k: w[k] for k in SMALL}, wf)

    slots = _scatter_gradients(big_g)
    halves = {k: _sum_slots(f"sum_{k}", slots[k]) for k in BIG}
    grad = _exchange_halves(halves, {k: w[k].shape for k in BIG})
    delta, new_m, new_v = {}, {}, {}
    for k in BIG:
        d_k, m_k, v_k = _adamw(f"adamw_{k}", flat(w[k]), flat(grad[k]), flat(m[k]), flat(v[k]))
        delta[k], new_m[k], new_v[k] = d_k.reshape(w[k].shape), m_k.reshape(w[k].shape), v_k.reshape(w[k].shape)

    slabs = _small_allreduce_adamw(_pack_small(small_g), _pack_small({k: w[k] for k in SMALL}),
                                   _pack_small({k: m[k] for k in SMALL}), _pack_small({k: v[k] for k in SMALL}), loss_part)
    loss = slabs[0][LOSS_ROW, 0]
    for out, slab in zip((grad, delta, new_m, new_v), slabs):
        out.update(_unpack_small(slab))

    return (loss, dx[None], *[grad[k] for k in WEIGHTS], *[delta[k] for k in WEIGHTS],
            *[new_m[k] for k in WEIGHTS], *[new_v[k] for k in WEIGHTS])
```

```python
import functools
import math

import jax
import jax.numpy as jnp
from jax import lax
from jax.experimental import pallas as pl
from jax.experimental.pallas import tpu as pltpu

F32 = jnp.float32
BF16 = jnp.bfloat16
_MXU = BF16
_WIRE = BF16

D_MODEL = 1024
HEAD_DIM = 128
EPS = 1e-6
NEG_INF = -1e30
CHUNK = 128
GRID_W = 64
RET_HEADS, RET_DK, RET_DV = 4, 128, 256
RET_Q, RET_V = RET_HEADS * RET_DK, RET_HEADS * RET_DV
RET_THETA = 10000.0
SWA_HEADS, SWA_KV_HEADS = 8, 2
T5_BUCKETS, T5_MAX_DIST = 32, 128
AX_HEADS, AX_KV_HEADS = 8, 2
AX_THETA = 10000.0
D_FF = 4 * D_MODEL
EVEN_IN = 2 * RET_Q + 2 * RET_V + D_MODEL + 2 * SWA_KV_HEADS * HEAD_DIM
ODD_IN = D_MODEL + 2 * AX_KV_HEADS * HEAD_DIM
ATT_SCALE = HEAD_DIM ** -0.5

ADAM_LR, ADAM_B1, ADAM_B2, ADAM_EPS, ADAM_WD, ADAM_STEP = 0.001, 0.9, 0.999, 1e-08, 0.01, 10

N_DEV = 8
VMEM_LIMIT_BYTES = 56 << 20
MESH = pl.DeviceIdType.MESH

_NN = (((1,), (0,)), ((), ()))
_NT = (((1,), (1,)), ((), ()))
_TN = (((0,), (0,)), ((), ()))


def _dot(a, b, dn=_NN):
    return lax.dot_general(a.astype(_MXU), b.astype(_MXU), dn, preferred_element_type=F32)


def _params(*sem):
    return pltpu.CompilerParams(dimension_semantics=sem, vmem_limit_bytes=VMEM_LIMIT_BYTES)


def _sds(shape, dtype):
    return jax.ShapeDtypeStruct(tuple(shape), dtype)


def _rowsum8(x):
    return jnp.sum(x.reshape(x.shape[0] // 8, 8, x.shape[1]), axis=0)


def _swap_halves(x, half):
    width = x.shape[1]
    lane = lax.broadcasted_iota(jnp.int32, x.shape, 1)
    up = pltpu.roll(x, width - half, axis=1)
    down = pltpu.roll(x, half, axis=1)
    return jnp.where((lane & (2 * half - 1)) < half, up, down)


def _sigmoid(x):
    return 1.0 / (1.0 + jnp.exp(-x))


def _norm_matmul(name, x, gain, w, *, tm, tn, out_dtype):
    T, K = x.shape
    N = w.shape[1]
    tm, tn = min(tm, T), min(tn, N)

    def body(x_ref, g_ref, w_ref, y_ref, h_ref, h_sc):
        @pl.when(pl.program_id(1) == 0)
        def _():
            xv = x_ref[...]
            r = lax.rsqrt(jnp.mean(xv * xv, axis=-1, keepdims=True) + EPS)
            h = (xv * r * g_ref[...]).astype(_MXU)
            h_sc[...] = h
            h_ref[...] = h
        y_ref[...] = jnp.dot(h_sc[...], w_ref[...], preferred_element_type=F32).astype(y_ref.dtype)

    return pl.pallas_call(
        body, name=name, grid=(T // tm, N // tn),
        in_specs=[pl.BlockSpec((tm, K), lambda i, j: (i, 0)),
                  pl.BlockSpec((1, K), lambda i, j: (0, 0)),
                  pl.BlockSpec((K, tn), lambda i, j: (0, j))],
        out_specs=[pl.BlockSpec((tm, tn), lambda i, j: (i, j)),
                   pl.BlockSpec((tm, K), lambda i, j: (i, 0))],
        out_shape=[_sds((T, N), out_dtype), _sds((T, K), _MXU)],
        scratch_shapes=[pltpu.VMEM((tm, K), _MXU)],
        compiler_params=_params("parallel", "arbitrary"),
    )(x, gain, w)


def _matmul_res(name, a_list, w, res, *, tm, relu2=False, target=None):
    T = res.shape[0]
    N = w.shape[1]
    K = a_list[0].shape[1]
    n_a = len(a_list)
    tm = min(tm, T)
    with_loss = target is not None

    def body(*refs):
        a_refs = refs[:n_a]
        w_refs = refs[n_a:2 * n_a]
        res_ref = refs[2 * n_a]
        acc = res_ref[...]
        for a_ref, w_ref in zip(a_refs, w_refs):
            a = a_ref[...]
            if relu2:
                a = jnp.square(jnp.maximum(a.astype(F32), 0.0))
            acc = acc + _dot(a, w_ref[...])
        if with_loss:
            tgt_ref, g_ref, loss_ref = refs[2 * n_a + 1:]
            diff = acc - tgt_ref[...]
            g_ref[...] = diff * (1.0 / N)

            @pl.when(pl.program_id(0) == 0)
            def _():
                loss_ref[...] = jnp.zeros_like(loss_ref)
            loss_ref[...] += _rowsum8(diff * diff)
        else:
            refs[2 * n_a + 1][...] = acc

    row = lambda i: (i, 0)
    in_specs = [pl.BlockSpec((tm, K), row) for _ in a_list]
    in_specs += [pl.BlockSpec((K, N), functools.partial(lambda i, b: (b, 0), b=b)) for b in range(n_a)]
    in_specs += [pl.BlockSpec((tm, N), row)]
    args = list(a_list) + [w] * n_a + [res]
    if with_loss:
        in_specs.append(pl.BlockSpec((tm, N), row))
        args.append(target)
        out_specs = [pl.BlockSpec((tm, N), row), pl.BlockSpec((8, N), lambda i: (0, 0))]
        out_shape = [_sds((T, N), F32), _sds((8, N), F32)]
        sem = "arbitrary"
    else:
        out_specs = pl.BlockSpec((tm, N), row)
        out_shape = _sds((T, N), F32)
        sem = "parallel"
    return pl.pallas_call(body, name=name, grid=(T // tm,), in_specs=in_specs, out_specs=out_specs,
                          out_shape=out_shape, compiler_params=_params(sem))(*args)


def _matmul_nt(name, a, w, *, tm, tn, out_dtype, relu_of=None):
    T, K = a.shape
    N = w.shape[0]
    tm, tn = min(tm, T), min(tn, N)

    def body(*refs):
        if relu_of is None:
            a_ref, w_ref, o_ref = refs
            o_ref[...] = _dot(a_ref[...], w_ref[...], _NT).astype(o_ref.dtype)
        else:
            a_ref, w_ref, u_ref, o_ref = refs
            da = _dot(a_ref[...], w_ref[...], _NT)
            o_ref[...] = (da * (2.0 * jnp.maximum(u_ref[...].astype(F32), 0.0))).astype(o_ref.dtype)

    in_specs = [pl.BlockSpec((tm, K), lambda i, j: (i, 0)), pl.BlockSpec((tn, K), lambda i, j: (j, 0))]
    args = [a, w]
    if relu_of is not None:
        in_specs.append(pl.BlockSpec((tm, tn), lambda i, j: (i, j)))
        args.append(relu_of)
    return pl.pallas_call(body, name=name, grid=(T // tm, N // tn), in_specs=in_specs,
                          out_specs=pl.BlockSpec((tm, tn), lambda i, j: (i, j)),
                          out_shape=_sds((T, N), out_dtype),
                          compiler_params=_params("parallel", "parallel"))(*args)


def _matmul_nt_normbwd(name, dy, w, x, gain, dres, *, tm):
    T, K = dy.shape
    N = w.shape[0]
    tm = min(tm, T)

    def body(dy_ref, w_ref, x_ref, g_ref, dres_ref, dx_ref, dg_ref):
        dh = _dot(dy_ref[...], w_ref[...], _NT)
        xv = x_ref[...]
        r = lax.rsqrt(jnp.mean(xv * xv, axis=-1, keepdims=True) + EPS)
        xhat = xv * r
        dxhat = dh * g_ref[...]
        dx_ref[...] = dres_ref[...] + r * (dxhat - xhat * jnp.mean(dxhat * xhat, axis=-1, keepdims=True))

        @pl.when(pl.program_id(0) == 0)
        def _():
            dg_ref[...] = jnp.zeros_like(dg_ref)
        dg_ref[...] += _rowsum8(dh * xhat)

    row = lambda i: (i, 0)
    return pl.pallas_call(
        body, name=name, grid=(T // tm,),
        in_specs=[pl.BlockSpec((tm, K), row), pl.BlockSpec((N, K), lambda i: (0, 0)),
                  pl.BlockSpec((tm, N), row), pl.BlockSpec((1, N), lambda i: (0, 0)), pl.BlockSpec((tm, N), row)],
        out_specs=[pl.BlockSpec((tm, N), row), pl.BlockSpec((8, N), lambda i: (0, 0))],
        out_shape=[_sds((T, N), F32), _sds((8, N), F32)],
        compiler_params=_params("arbitrary"),
    )(dy, w, x, gain, dres)


def _matmul_tn(name, a, b, *, tk, tn, tt, out_dtype, relu2=False):
    T, Ka = a.shape
    Nb = b.shape[1]
    tk, tn, tt = min(tk, Ka), min(tn, Nb), min(tt, T)
    nt = T // tt

    def body(a_ref, b_ref, o_ref, acc):
        t = pl.program_id(2)

        @pl.when(t == 0)
        def _():
            acc[...] = jnp.zeros_like(acc)
        av = a_ref[...]
        if relu2:
            av = jnp.square(jnp.maximum(av.astype(F32), 0.0))
        acc[...] += _dot(av, b_ref[...], _TN)

        @pl.when(t == nt - 1)
        def _():
            o_ref[...] = acc[...].astype(o_ref.dtype)

    return pl.pallas_call(
        body, name=name, grid=(Ka // tk, Nb // tn, nt),
        in_specs=[pl.BlockSpec((tt, tk), lambda i, j, t: (t, i)), pl.BlockSpec((tt, tn), lambda i, j, t: (t, j))],
        out_specs=pl.BlockSpec((tk, tn), lambda i, j, t: (i, j)),
        out_shape=_sds((Ka, Nb), out_dtype),
        scratch_shapes=[pltpu.VMEM((tk, tn), F32)],
        compiler_params=_params("parallel", "parallel", "arbitrary"),
    )(a, b)


def _rope_angles(pos, dim, theta):
    inv = theta ** (-jnp.arange(0, dim, 2, dtype=F32) / dim)
    return pos.astype(F32)[:, None] * inv[None, :]


def _ret_rope_tables(T):
    ang = _rope_angles(jnp.arange(T), RET_DK, RET_THETA)
    c, s = jnp.cos(ang), jnp.sin(ang)
    return jnp.concatenate([c, c], axis=1), jnp.concatenate([-s, s], axis=1)


def _axial_rope_tables(T):
    rows = T // GRID_W
    row = jnp.repeat(jnp.arange(rows), GRID_W)
    col = jnp.tile(jnp.arange(GRID_W), rows)
    ar = _rope_angles(row, HEAD_DIM // 2, AX_THETA)
    ac = _rope_angles(col, HEAD_DIM // 2, AX_THETA)
    cos = jnp.concatenate([jnp.cos(ar), jnp.cos(ar), jnp.cos(ac), jnp.cos(ac)], axis=1)
    sin = jnp.concatenate([-jnp.sin(ar), jnp.sin(ar), -jnp.sin(ac), jnp.sin(ac)], axis=1)
    return cos, sin


(TAB_D, TAB_DT, TAB_EF, TAB_EB, TAB_A, TAB_B, TAB_CF, TAB_CB,
 TAB_RA, TAB_RB, TAB_RCF, TAB_RCB, TAB_KF, TAB_KB) = range(14)


def _retention_tables(decay_logit):
    lg = jax.nn.log_sigmoid(decay_logit.astype(F32))
    lam, mu = lg[0][:, None, None], lg[1][:, None, None]
    idx = jnp.arange(CHUNK, dtype=F32)
    diff = (idx[:, None] - idx[None, :])[None]
    df = jnp.where(diff >= 0, jnp.exp(jnp.maximum(diff, 0.0) * lam), 0.0)
    db = jnp.where(diff < 0, jnp.exp(jnp.maximum(-diff, 0.0) * mu), 0.0)
    d = df + db
    r = idx[None, :, None]
    ones = jnp.ones((1, 1, CHUNK), F32)
    a = jnp.exp((r + 1.0) * lam) * ones
    b = jnp.exp((CHUNK - r) * mu) * ones
    cf = jnp.exp((CHUNK - 1.0 - r) * lam) * ones
    cb = jnp.exp(r * mu) * ones
    full = jnp.ones((1, CHUNK, CHUNK), F32)
    kf = CHUNK * jnp.exp(CHUNK * lam) * full
    kb = CHUNK * jnp.exp(CHUNK * mu) * full
    tabs = jnp.stack([d, jnp.swapaxes(d, 1, 2), diff * df, -diff * db, a, b, cf, cb,
                      (r + 1.0) * a, (CHUNK - r) * b, (CHUNK - 1.0 - r) * cf, r * cb, kf, kb], axis=1)

    def lanes(tab):
        return jnp.transpose(tab, (1, 0, 2)).reshape(CHUNK, RET_HEADS * CHUNK)

    def dec(l):
        return jnp.exp(CHUNK * l)[:, 0, :] * jnp.ones((1, RET_DV), F32)

    weights = dict(a=lanes(a), b=lanes(b), cf=lanes(cf), cb=lanes(cb), dec_f=dec(lam), dec_b=dec(mu))
    return tabs, weights, lg


def _t5_bucket(rel):
    nb = T5_BUCKETS // 2
    max_exact = nb // 2
    ret = jnp.where(rel > 0, nb, 0)
    n = jnp.abs(rel)
    nf = jnp.maximum(n, 1).astype(F32)
    large = max_exact + (jnp.log(nf / max_exact) / math.log(T5_MAX_DIST / max_exact)
                         * (nb - max_exact)).astype(jnp.int32)
    large = jnp.minimum(large, nb - 1)
    return ret + jnp.where(n < max_exact, n, large)


def _swa_rel():
    r = jnp.arange(CHUNK)
    j = jnp.arange(3 * CHUNK)
    return j[None, :] - CHUNK - r[:, None]


def _swa_bias(t5_table):
    rel = _swa_rel()
    bucket = jnp.where(jnp.abs(rel) <= CHUNK, _t5_bucket(rel), -1).astype(jnp.int32)

    def body(tab_ref, bk_ref, o_ref):
        bk = bk_ref[...]
        for h in range(SWA_HEADS):
            pick = lambda b, acc, h=h: jnp.where(bk == b, tab_ref[b, h], acc)
            o_ref[h] = lax.fori_loop(0, T5_BUCKETS, pick, jnp.full(bk.shape, NEG_INF, F32))

    return pl.pallas_call(
        body, name="t5_bias",
        in_specs=[pl.BlockSpec(memory_space=pltpu.SMEM), pl.BlockSpec(memory_space=pltpu.VMEM)],
        out_specs=pl.BlockSpec(memory_space=pltpu.VMEM),
        out_shape=_sds((SWA_HEADS, CHUNK, 3 * CHUNK), F32),
    )(t5_table.astype(F32), bucket)


def _prep_even(proj, cos, sin, q_gain, k_gain, *, tm):
    T = proj.shape[0]
    tm = min(tm, T)

    def body(qa_ref, ka_ref, qb_ref, kb_ref, c_ref, s_ref, qg_ref, kg_ref, qr_ref, kr_ref, qn_ref, kn_ref):
        c = jnp.concatenate([c_ref[...]] * RET_HEADS, axis=1)
        s = jnp.concatenate([s_ref[...]] * RET_HEADS, axis=1)
        qa = qa_ref[...]
        qr_ref[...] = (qa * c + _swap_halves(qa, RET_DK // 2) * s).astype(qr_ref.dtype)
        ka = ka_ref[...]
        kr_ref[...] = ((ka * c + _swap_halves(ka, RET_DK // 2) * s) * (RET_DK ** -0.5)).astype(kr_ref.dtype)
        for src, gain, dst, heads in ((qb_ref, qg_ref, qn_ref, SWA_HEADS), (kb_ref, kg_ref, kn_ref, SWA_KV_HEADS)):
            for h in range(heads):
                sl = slice(h * HEAD_DIM, (h + 1) * HEAD_DIM)
                xh = src[:, sl]
                r = lax.rsqrt(jnp.mean(xh * xh, axis=-1, keepdims=True) + EPS)
                dst[:, sl] = (xh * r * gain[...]).astype(dst.dtype)

    row = lambda i: (i, 0)
    const = lambda i: (0, 0)
    return pl.pallas_call(
        body, name="prep_even", grid=(T // tm,),
        in_specs=[pl.BlockSpec((tm, RET_Q), lambda i: (i, 0)), pl.BlockSpec((tm, RET_Q), lambda i: (i, 1)),
                  pl.BlockSpec((tm, D_MODEL), lambda i: (i, 3)), pl.BlockSpec((tm, 256), lambda i: (i, 16)),
                  pl.BlockSpec((tm, RET_DK), row), pl.BlockSpec((tm, RET_DK), row),
                  pl.BlockSpec((1, HEAD_DIM), const), pl.BlockSpec((1, HEAD_DIM), const)],
        out_specs=[pl.BlockSpec((tm, RET_Q), row), pl.BlockSpec((tm, RET_Q), row),
                   pl.BlockSpec((tm, D_MODEL), row), pl.BlockSpec((tm, 256), row)],
        out_shape=[_sds((T, RET_Q), _MXU), _sds((T, RET_Q), _MXU), _sds((T, D_MODEL), _MXU), _sds((T, 256), _MXU)],
        compiler_params=_params("parallel"),
    )(proj, proj, proj, proj, cos, sin, q_gain, k_gain)


def _ret_scan(name, x, y, y_col, w_asc, dec_asc, w_desc, dec_desc):
    T = x.shape[0]
    nc = T // CHUNK

    def body(xa_ref, ya_ref, xd_ref, yd_ref, wa_ref, da_ref, wd_ref, dd_ref, sa_out, sd_out, sa, sd):
        @pl.when(pl.program_id(0) == 0)
        def _():
            sa[...] = jnp.zeros_like(sa)
            sd[...] = jnp.zeros_like(sd)
        sa_out[0] = sa[...].astype(sa_out.dtype)
        sd_out[0] = sd[...].astype(sd_out.dtype)
        for x_ref, y_ref, w_ref, d_ref, st in ((xa_ref, ya_ref, wa_ref, da_ref, sa), (xd_ref, yd_ref, wd_ref, dd_ref, sd)):
            for h in range(RET_HEADS):
                ks = slice(h * RET_DK, (h + 1) * RET_DK)
                vs = slice(h * RET_DV, (h + 1) * RET_DV)
                u = _dot(x_ref[:, ks].astype(F32) * w_ref[:, ks], y_ref[:, vs], _TN)
                st[ks, :] = st[ks, :] * d_ref[h:h + 1, :] + u

    asc = lambda i: (i, 0)
    desc = lambda i: (nc - 1 - i, 0)
    const = lambda i: (0, 0)
    return pl.pallas_call(
        body, name=name, grid=(nc,),
        in_specs=[pl.BlockSpec((CHUNK, RET_Q), asc), pl.BlockSpec((CHUNK, RET_V), lambda i: (i, y_col)),
                  pl.BlockSpec((CHUNK, RET_Q), desc), pl.BlockSpec((CHUNK, RET_V), lambda i: (nc - 1 - i, y_col)),
                  pl.BlockSpec((CHUNK, RET_Q), const), pl.BlockSpec((RET_HEADS, RET_DV), const),
                  pl.BlockSpec((CHUNK, RET_Q), const), pl.BlockSpec((RET_HEADS, RET_DV), const)],
        out_specs=[pl.BlockSpec((1, RET_Q, RET_DV), lambda i: (i, 0, 0)),
                   pl.BlockSpec((1, RET_Q, RET_DV), lambda i: (nc - 1 - i, 0, 0))],
        out_shape=[_sds((nc, RET_Q, RET_DV), _MXU), _sds((nc, RET_Q, RET_DV), _MXU)],
        scratch_shapes=[pltpu.VMEM((RET_Q, RET_DV), F32), pltpu.VMEM((RET_Q, RET_DV), F32)],
        compiler_params=_params("arbitrary"),
    )(x, y, x, y, w_asc, dec_asc, w_desc, dec_desc)


def _ret_out(qr, kr, proj, sf, sb, tabs, gain):
    T = qr.shape[0]
    nc = T // CHUNK

    def body(q_ref, k_ref, v_ref, g_ref, sf_ref, sb_ref, tab_ref, gain_ref, o_ref, y_ref):
        for h in range(RET_HEADS):
            ks = slice(h * RET_DK, (h + 1) * RET_DK)
            vs = slice(h * RET_DV, (h + 1) * RET_DV)
            q, k, v = q_ref[:, ks], k_ref[:, ks], v_ref[:, vs]
            qf = q.astype(F32)
            a_mat = _dot(q, k, _NT) * tab_ref[h, 0]
            o = (_dot(a_mat, v) + _dot(qf * tab_ref[h, 1], sf_ref[0, ks, :]) + _dot(qf * tab_ref[h, 2], sb_ref[0, ks, :]))
            o_ref[:, vs] = o
            r = lax.rsqrt(jnp.mean(o * o, axis=-1, keepdims=True) + EPS)
            g = g_ref[:, vs]
            y_ref[:, vs] = (g * _sigmoid(g) * (o * r * gain_ref[:, vs])).astype(y_ref.dtype)

    row = lambda i: (i, 0)
    return pl.pallas_call(
        body, name="ret_out", grid=(nc,),
        in_specs=[pl.BlockSpec((CHUNK, RET_Q), row), pl.BlockSpec((CHUNK, RET_Q), row),
                  pl.BlockSpec((CHUNK, RET_V), lambda i: (i, 1)), pl.BlockSpec((CHUNK, RET_V), lambda i: (i, 2)),
                  pl.BlockSpec((1, RET_Q, RET_DV), lambda i: (i, 0, 0)), pl.BlockSpec((1, RET_Q, RET_DV), lambda i: (i, 0, 0)),
                  pl.BlockSpec((RET_HEADS, 3, CHUNK, CHUNK), lambda i: (0, 0, 0, 0)),
                  pl.BlockSpec((1, RET_V), lambda i: (0, 0))],
        out_specs=[pl.BlockSpec((CHUNK, RET_V), row), pl.BlockSpec((CHUNK, RET_V), row)],
        out_shape=[_sds((T, RET_V), F32), _sds((T, RET_V), _MXU)],
        compiler_params=_params("parallel"),
    )(qr, kr, proj, proj, sf, sb, tabs, gain)


def _ret_gate_bwd(dycat, proj, ret_o, gain, *, tm):
    T = ret_o.shape[0]
    tm = min(tm, T)

    def body(dy_ref, g_ref, o_ref, gain_ref, do_ref, dg_ref, dgain_ref):
        @pl.when(pl.program_id(0) == 0)
        def _():
            dgain_ref[...] = jnp.zeros_like(dgain_ref)
        for h in range(RET_HEADS):
            vs = slice(h * RET_DV, (h + 1) * RET_DV)
            o, g, dya, gn = o_ref[:, vs], g_ref[:, vs], dy_ref[:, vs], gain_ref[:, vs]
            r = lax.rsqrt(jnp.mean(o * o, axis=-1, keepdims=True) + EPS)
            ohat = o * r
            sg = _sigmoid(g)
            dy = dya * (g * sg)
            dg_ref[:, vs] = (dya * (ohat * gn) * (sg * (1.0 + g * (1.0 - sg)))).astype(dg_ref.dtype)
            dyg = dy * gn
            do_ref[:, vs] = (r * (dyg - ohat * jnp.mean(dyg * ohat, axis=-1, keepdims=True))).astype(do_ref.dtype)
            dgain_ref[:, vs] += _rowsum8(dy * ohat)

    row = lambda i: (i, 0)
    return pl.pallas_call(
        body, name="ret_gate_bwd", grid=(T // tm,),
        in_specs=[pl.BlockSpec((tm, RET_V), row), pl.BlockSpec((tm, RET_V), lambda i: (i, 2)),
                  pl.BlockSpec((tm, RET_V), row), pl.BlockSpec((1, RET_V), lambda i: (0, 0))],
        out_specs=[pl.BlockSpec((tm, RET_V), row), pl.BlockSpec((tm, RET_V), row), pl.BlockSpec((8, RET_V), lambda i: (0, 0))],
        out_shape=[_sds((T, RET_V), _MXU), _sds((T, RET_V), _MXU), _sds((8, RET_V), F32)],
        compiler_params=_params("arbitrary"),
    )(dycat, proj, ret_o, gain)


def _ret_bwd(qr, kr, proj, g_out, sf, sb, rf, rb, tabs):
    T = qr.shape[0]
    nc = T // CHUNK

    def body(q_ref, k_ref, v_ref, g_ref, sf_ref, sb_ref, rf_ref, rb_ref, tab_ref, dq_ref, dk_ref, dv_ref, dl_ref):
        @pl.when(pl.program_id(0) == 0)
        def _():
            dl_ref[...] = jnp.zeros_like(dl_ref)
        for h in range(RET_HEADS):
            ks = slice(h * RET_DK, (h + 1) * RET_DK)
            vs = slice(h * RET_DV, (h + 1) * RET_DV)
            q, k, v, g = q_ref[:, ks], k_ref[:, ks], v_ref[:, vs], g_ref[:, vs]
            s_f, s_b, r_f, r_b = sf_ref[0, ks, :], sb_ref[0, ks, :], rf_ref[0, ks, :], rb_ref[0, ks, :]
            tab = lambda t: tab_ref[h, t]
            qf, kf = q.astype(F32), k.astype(F32)
            qk = _dot(q, k, _NT)
            da_raw = _dot(g, v, _NT)
            x_f, x_b = _dot(g, s_f, _NT), _dot(g, s_b, _NT)
            dq_ref[:, ks] = _dot(da_raw * tab(TAB_D), k) + tab(TAB_A) * x_f + tab(TAB_B) * x_b
            at = _dot(k, q, _NT) * tab(TAB_DT)
            dat = _dot(v, g, _NT) * tab(TAB_DT)
            y_f, y_b = _dot(v, r_f, _NT), _dot(v, r_b, _NT)
            dk_ref[:, ks] = _dot(dat, q) + tab(TAB_CF) * y_f + tab(TAB_CB) * y_b
            dv_ref[:, vs] = (_dot(at, g) + _dot(kf * tab(TAB_CF), r_f) + _dot(kf * tab(TAB_CB), r_b)).astype(dv_ref.dtype)
            inner = da_raw * qk
            rs_f = r_f.astype(F32) * s_f.astype(F32)
            rs_b = r_b.astype(F32) * s_b.astype(F32)
            dl_f = (inner * tab(TAB_EF) + tab(TAB_RA) * qf * x_f + tab(TAB_RCF) * kf * y_f
                    + tab(TAB_KF) * (rs_f[:, :CHUNK] + rs_f[:, CHUNK:]))
            dl_b = (inner * tab(TAB_EB) + tab(TAB_RB) * qf * x_b + tab(TAB_RCB) * kf * y_b
                    + tab(TAB_KB) * (rs_b[:, :CHUNK] + rs_b[:, CHUNK:]))
            dl_ref[2 * h:2 * h + 1, :] += jnp.sum(dl_f, axis=0, keepdims=True)
            dl_ref[2 * h + 1:2 * h + 2, :] += jnp.sum(dl_b, axis=0, keepdims=True)

    row = lambda i: (i, 0)
    st = lambda i: (i, 0, 0)
    return pl.pallas_call(
        body, name="ret_bwd", grid=(nc,),
        in_specs=[pl.BlockSpec((CHUNK, RET_Q), row), pl.BlockSpec((CHUNK, RET_Q), row),
                  pl.BlockSpec((CHUNK, RET_V), lambda i: (i, 1)), pl.BlockSpec((CHUNK, RET_V), row),
                  pl.BlockSpec((1, RET_Q, RET_DV), st), pl.BlockSpec((1, RET_Q, RET_DV), st),
                  pl.BlockSpec((1, RET_Q, RET_DV), st), pl.BlockSpec((1, RET_Q, RET_DV), st),
                  pl.BlockSpec((RET_HEADS, 14, CHUNK, CHUNK), lambda i: (0, 0, 0, 0))],
        out_specs=[pl.BlockSpec((CHUNK, RET_Q), row), pl.BlockSpec((CHUNK, RET_Q), row),
                   pl.BlockSpec((CHUNK, RET_V), row), pl.BlockSpec((8, CHUNK), lambda i: (0, 0))],
        out_shape=[_sds((T, RET_Q), F32), _sds((T, RET_Q), F32), _sds((T, RET_V), _MXU), _sds((8, CHUNK), F32)],
        compiler_params=_params("arbitrary"),
    )(qr, kr, proj, g_out, sf, sb, rf, rb, tabs)


def _swa_probs(q, k_win, bias, sink, valid):
    s = _dot(q, k_win, _NT) * ATT_SCALE + bias
    s = jnp.where(valid, s, NEG_INF)
    m = jnp.maximum(jnp.max(s, axis=-1, keepdims=True), sink)
    p = jnp.exp(s - m)
    e_sink = jnp.exp(sink - m)
    inv = 1.0 / (jnp.sum(p, axis=-1, keepdims=True) + e_sink)
    return p * inv, e_sink * inv


def _swa_valid(i, nb):
    col = lax.broadcasted_iota(jnp.int32, (1, 3 * CHUNK), 1)
    return jnp.logical_and(jnp.logical_or(col >= CHUNK, i > 0), jnp.logical_or(col < 2 * CHUNK, i < nb - 1))


def _swa_window_specs(nb, width, col_block, clamp):
    prev = lambda i: (jnp.maximum(clamp(i) - 1, 0), col_block)
    cur = lambda i: (clamp(i), col_block)
    nxt = lambda i: (jnp.minimum(clamp(i) + 1, nb - 1), col_block)
    return [pl.BlockSpec((CHUNK, width), f) for f in (prev, cur, nxt)]


def _swa_fwd(qn, kn, proj, bias, sink):
    T = qn.shape[0]
    nb = T // CHUNK
    kvw = SWA_KV_HEADS * HEAD_DIM
    group = SWA_HEADS // SWA_KV_HEADS

    def body(q_ref, k0, k1, k2, v0, v1, v2, bias_ref, sink_ref, y_ref):
        i = pl.program_id(0)
        valid = _swa_valid(i, nb)
        for g in range(SWA_KV_HEADS):
            gs = slice(g * HEAD_DIM, (g + 1) * HEAD_DIM)
            k_win = jnp.concatenate([k0[:, gs], k1[:, gs], k2[:, gs]], axis=0)
            v_win = jnp.concatenate([v0[:, gs], v1[:, gs], v2[:, gs]], axis=0).astype(_MXU)
            for hh in range(group):
                h = g * group + hh
                hs = slice(h * HEAD_DIM, (h + 1) * HEAD_DIM)
                p, _ = _swa_probs(q_ref[:, hs], k_win, bias_ref[h], sink_ref[h:h + 1, 0:1], valid)
                y_ref[:, hs] = _dot(p, v_win).astype(y_ref.dtype)

    ident = lambda i: i
    return pl.pallas_call(
        body, name="swa_fwd", grid=(nb,),
        in_specs=[pl.BlockSpec((CHUNK, D_MODEL), lambda i: (i, 0))]
        + _swa_window_specs(nb, kvw, 0, ident) + _swa_window_specs(nb, kvw, 17, ident)
        + [pl.BlockSpec((SWA_HEADS, CHUNK, 3 * CHUNK), lambda i: (0, 0, 0)), pl.BlockSpec((SWA_HEADS, HEAD_DIM), lambda i: (0, 0))],
        out_specs=pl.BlockSpec((CHUNK, D_MODEL), lambda i: (i, 0)),
        out_shape=_sds((T, D_MODEL), _MXU),
        compiler_params=_params("parallel"),
    )(qn, kn, kn, kn, proj, proj, proj, bias, sink)


def _swa_bwd(qn, kn, proj, dycat, bias, sink):
    T = qn.shape[0]
    nb = T // CHUNK
    kvw = SWA_KV_HEADS * HEAD_DIM
    group = SWA_HEADS // SWA_KV_HEADS

    def body(q_ref, k0, k1, k2, v0, v1, v2, dy_ref, bias_ref, sink_ref,
             dq_ref, dk_ref, dv_ref, dbias_ref, dsink_ref, acc_a, acc_b):
        i = pl.program_id(0)

        @pl.when(i == 0)
        def _():
            dbias_ref[...] = jnp.zeros_like(dbias_ref)
            dsink_ref[...] = jnp.zeros_like(dsink_ref)
            acc_a[...] = jnp.zeros_like(acc_a)
            acc_b[...] = jnp.zeros_like(acc_b)

        @pl.when(i < nb)
        def _():
            valid = _swa_valid(i, nb)
            for g in range(SWA_KV_HEADS):
                gs = slice(g * HEAD_DIM, (g + 1) * HEAD_DIM)
                k_win = jnp.concatenate([k0[:, gs], k1[:, gs], k2[:, gs]], axis=0)
                v_win = jnp.concatenate([v0[:, gs], v1[:, gs], v2[:, gs]], axis=0).astype(_MXU)
                dk_win = jnp.zeros((3 * CHUNK, HEAD_DIM), F32)
                dv_win = jnp.zeros((3 * CHUNK, HEAD_DIM), F32)
                for hh in range(group):
                    h = g * group + hh
                    hs = slice(h * HEAD_DIM, (h + 1) * HEAD_DIM)
                    q, dy = q_ref[:, hs], dy_ref[:, hs]
                    p, p_sink = _swa_probs(q, k_win, bias_ref[h], sink_ref[h:h + 1, 0:1], valid)
                    dp = _dot(dy, v_win, _NT)
                    delta = jnp.sum(p * dp, axis=-1, keepdims=True)
                    ds = p * (dp - delta)
                    dbias_ref[h] += ds
                    dsink_ref[h:h + 1, :] += jnp.sum(-p_sink * delta, axis=0, keepdims=True) * jnp.ones((1, HEAD_DIM), F32)
                    dq_ref[:, hs] = _dot(ds, k_win) * ATT_SCALE
                    dk_win = dk_win + _dot(ds, q, _TN) * ATT_SCALE
                    dv_win = dv_win + _dot(p, dy, _TN)
                for win, out_ref, col0 in ((dk_win, dk_ref, 0), (dv_win, dv_ref, kvw)):
                    cs = slice(col0 + g * HEAD_DIM, col0 + (g + 1) * HEAD_DIM)
                    out_ref[:, gs] = acc_a[:, cs] + win[:CHUNK]
                    acc_a[:, cs] = acc_b[:, cs] + win[CHUNK:2 * CHUNK]
                    acc_b[:, cs] = win[2 * CHUNK:]

        @pl.when(i == nb)
        def _():
            dk_ref[...] = acc_a[:, :kvw]
            dv_ref[...] = acc_a[:, kvw:]

    clamp = lambda i: jnp.minimum(i, nb - 1)
    late = lambda i: (jnp.maximum(i - 1, 0), 0)
    return pl.pallas_call(
        body, name="swa_bwd", grid=(nb + 1,),
        in_specs=[pl.BlockSpec((CHUNK, D_MODEL), lambda i: (clamp(i), 0))]
        + _swa_window_specs(nb, kvw, 0, clamp) + _swa_window_specs(nb, kvw, 17, clamp)
        + [pl.BlockSpec((CHUNK, D_MODEL), lambda i: (clamp(i), 1)),
           pl.BlockSpec((SWA_HEADS, CHUNK, 3 * CHUNK), lambda i: (0, 0, 0)), pl.BlockSpec((SWA_HEADS, HEAD_DIM), lambda i: (0, 0))],
        out_specs=[pl.BlockSpec((CHUNK, D_MODEL), lambda i: (clamp(i), 0)),
                   pl.BlockSpec((CHUNK, kvw), late), pl.BlockSpec((CHUNK, kvw), late),
                   pl.BlockSpec((SWA_HEADS, CHUNK, 3 * CHUNK), lambda i: (0, 0, 0)), pl.BlockSpec((SWA_HEADS, HEAD_DIM), lambda i: (0, 0))],
        out_shape=[_sds((T, D_MODEL), F32), _sds((T, kvw), F32), _sds((T, kvw), F32),
                   _sds((SWA_HEADS, CHUNK, 3 * CHUNK), F32), _sds((SWA_HEADS, HEAD_DIM), F32)],
        scratch_shapes=[pltpu.VMEM((CHUNK, 2 * kvw), F32), pltpu.VMEM((CHUNK, 2 * kvw), F32)],
        compiler_params=_params("arbitrary"),
    )(qn, kn, kn, kn, proj, proj, proj, dycat, bias, sink)


def _t5_bucket_reduce(dbias, bucket):
    def body(db_ref, bk_ref, o_ref):
        bk = bk_ref[...]
        row = lax.broadcasted_iota(jnp.int32, (SWA_HEADS, HEAD_DIM), 0)
        lane = lax.broadcasted_iota(jnp.int32, (SWA_HEADS, HEAD_DIM), 1)

        def per_bucket(b, acc):
            mask = bk == b
            for h in range(SWA_HEADS):
                tot = jnp.sum(jnp.sum(jnp.where(mask, db_ref[h], 0.0), axis=0, keepdims=True), axis=1, keepdims=True)
                acc = acc + jnp.where(jnp.logical_and(row == h, lane == b), tot, 0.0)
            return acc

        o_ref[...] = lax.fori_loop(0, T5_BUCKETS, per_bucket, jnp.zeros((SWA_HEADS, HEAD_DIM), F32))

    return pl.pallas_call(body, name="t5_bucket_reduce", out_shape=_sds((SWA_HEADS, HEAD_DIM), F32),
                          compiler_params=pltpu.CompilerParams(vmem_limit_bytes=VMEM_LIMIT_BYTES))(dbias, bucket)


def _headnorm_bwd(x, dy, gain):
    r = lax.rsqrt(jnp.mean(x * x, axis=-1, keepdims=True) + EPS)
    xhat = x * r
    dyg = dy * gain
    return r * (dyg - xhat * jnp.mean(dyg * xhat, axis=-1, keepdims=True)), dy * xhat


def _post_even(proj, dqr, dkr, dva, dga, dqn, dkn, dvb, cos, sin, q_gain, k_gain, *, tm):
    T = proj.shape[0]
    tm = min(tm, T)
    kvw = SWA_KV_HEADS * HEAD_DIM

    def body(qb_ref, kb_ref, dqr_ref, dkr_ref, dva_ref, dga_ref, dqn_ref, dkn_ref, dvb_ref, c_ref, s_ref, qg_ref, kg_ref,
             dp_ref, dqg_ref, dkg_ref):
        @pl.when(pl.program_id(0) == 0)
        def _():
            dqg_ref[...] = jnp.zeros_like(dqg_ref)
            dkg_ref[...] = jnp.zeros_like(dkg_ref)
        c = jnp.concatenate([c_ref[...]] * RET_HEADS, axis=1)
        s = jnp.concatenate([s_ref[...]] * RET_HEADS, axis=1)
        dq = dqr_ref[...]
        dp_ref[:, 0:RET_Q] = (dq * c + _swap_halves(dq * s, RET_DK // 2)).astype(dp_ref.dtype)
        dk = dkr_ref[...] * (RET_DK ** -0.5)
        dp_ref[:, RET_Q:2 * RET_Q] = (dk * c + _swap_halves(dk * s, RET_DK // 2)).astype(dp_ref.dtype)
        off = 2 * RET_Q
        dp_ref[:, off:off + RET_V] = dva_ref[...].astype(dp_ref.dtype)
        dp_ref[:, off + RET_V:off + 2 * RET_V] = dga_ref[...].astype(dp_ref.dtype)
        off += 2 * RET_V
        for src, dsrc, gain, dgain, heads, base in ((qb_ref, dqn_ref, qg_ref, dqg_ref, SWA_HEADS, off),
                                                    (kb_ref, dkn_ref, kg_ref, dkg_ref, SWA_KV_HEADS, off + D_MODEL)):
            for h in range(heads):
                sl = slice(h * HEAD_DIM, (h + 1) * HEAD_DIM)
                dx, dgx = _headnorm_bwd(src[:, sl], dsrc[:, sl], gain[...])
                dp_ref[:, base + h * HEAD_DIM:base + (h + 1) * HEAD_DIM] = dx.astype(dp_ref.dtype)
                dgain[...] += _rowsum8(dgx)
        dp_ref[:, off + D_MODEL + kvw:] = dvb_ref[...].astype(dp_ref.dtype)

    row = lambda i: (i, 0)
    const = lambda i: (0, 0)
    return pl.pallas_call(
        body, name="post_even", grid=(T // tm,),
        in_specs=[pl.BlockSpec((tm, D_MODEL), lambda i: (i, 3)), pl.BlockSpec((tm, kvw), lambda i: (i, 16)),
                  pl.BlockSpec((tm, RET_Q), row), pl.BlockSpec((tm, RET_Q), row),
                  pl.BlockSpec((tm, RET_V), row), pl.BlockSpec((tm, RET_V), row),
                  pl.BlockSpec((tm, D_MODEL), row), pl.BlockSpec((tm, kvw), row), pl.BlockSpec((tm, kvw), row),
                  pl.BlockSpec((tm, RET_DK), row), pl.BlockSpec((tm, RET_DK), row),
                  pl.BlockSpec((1, HEAD_DIM), const), pl.BlockSpec((1, HEAD_DIM), const)],
        out_specs=[pl.BlockSpec((tm, EVEN_IN), row), pl.BlockSpec((8, HEAD_DIM), const), pl.BlockSpec((8, HEAD_DIM), const)],
        out_shape=[_sds((T, EVEN_IN), _MXU), _sds((8, HEAD_DIM), F32), _sds((8, HEAD_DIM), F32)],
        compiler_params=_params("arbitrary"),
    )(proj, proj, dqr, dkr, dva, dga, dqn, dkn, dvb, cos, sin, q_gain, k_gain)


def _prep_odd(proj, cos, sin, q_gain, k_gain, *, tm):
    T = proj.shape[0]
    tm = min(tm, T)
    kvw = AX_KV_HEADS * HEAD_DIM

    def body(q_ref, k_ref, v_ref, c_ref, s_ref, qg_ref, kg_ref, qx_ref, kx_ref, vx_ref):
        c, s = c_ref[...], s_ref[...]
        for src, gain, dst, heads in ((q_ref, qg_ref, qx_ref, AX_HEADS), (k_ref, kg_ref, kx_ref, AX_KV_HEADS)):
            for h in range(heads):
                sl = slice(h * HEAD_DIM, (h + 1) * HEAD_DIM)
                xh = src[:, sl]
                r = lax.rsqrt(jnp.mean(xh * xh, axis=-1, keepdims=True) + EPS)
                xn = xh * r * gain[...]
                dst[:, sl] = (xn * c + _swap_halves(xn, HEAD_DIM // 4) * s).astype(dst.dtype)
        vx_ref[...] = v_ref[...].astype(vx_ref.dtype)

    row = lambda i: (i, 0)
    const = lambda i: (0, 0)
    return pl.pallas_call(
        body, name="prep_odd", grid=(T // tm,),
        in_specs=[pl.BlockSpec((tm, D_MODEL), row), pl.BlockSpec((tm, kvw), lambda i: (i, 4)), pl.BlockSpec((tm, kvw), lambda i: (i, 5)),
                  pl.BlockSpec((tm, HEAD_DIM), row), pl.BlockSpec((tm, HEAD_DIM), row),
                  pl.BlockSpec((1, HEAD_DIM), const), pl.BlockSpec((1, HEAD_DIM), const)],
        out_specs=[pl.BlockSpec((tm, D_MODEL), row), pl.BlockSpec((tm, kvw), row), pl.BlockSpec((tm, kvw), row)],
        out_shape=[_sds((T, D_MODEL), _MXU), _sds((T, kvw), _MXU), _sds((T, kvw), _MXU)],
        compiler_params=_params("parallel"),
    )(proj, proj, proj, cos, sin, q_gain, k_gain)


def _post_odd(proj, dqxt, dkx, dvx, cos, sin, q_gain, k_gain, *, tm):
    T = proj.shape[0]
    tm = min(tm, T)
    kvw = AX_KV_HEADS * HEAD_DIM

    def body(q_ref, k_ref, dqt_ref, dk_ref, dv_ref, c_ref, s_ref, qg_ref, kg_ref, dp_ref, dqg_ref, dkg_ref):
        @pl.when(pl.program_id(0) == 0)
        def _():
            dqg_ref[...] = jnp.zeros_like(dqg_ref)
            dkg_ref[...] = jnp.zeros_like(dkg_ref)
        c, s = c_ref[...], s_ref[...]
        for src, dsrc, gain, dgain, heads, base in ((q_ref, dqt_ref, qg_ref, dqg_ref, AX_HEADS, 0),
                                                    (k_ref, dk_ref, kg_ref, dkg_ref, AX_KV_HEADS, D_MODEL)):
            for h in range(heads):
                sl = slice(h * HEAD_DIM, (h + 1) * HEAD_DIM)
                d = dsrc[sl, :].T if dsrc is dqt_ref else dsrc[:, sl]
                dn = d * c + _swap_halves(d * s, HEAD_DIM // 4)
                dx, dgx = _headnorm_bwd(src[:, sl], dn, gain[...])
                dp_ref[:, base + h * HEAD_DIM:base + (h + 1) * HEAD_DIM] = dx.astype(dp_ref.dtype)
                dgain[...] += _rowsum8(dgx)
        dp_ref[:, D_MODEL + kvw:] = dv_ref[...].astype(dp_ref.dtype)

    row = lambda i: (i, 0)
    const = lambda i: (0, 0)
    return pl.pallas_call(
        body, name="post_odd", grid=(T // tm,),
        in_specs=[pl.BlockSpec((tm, D_MODEL), row), pl.BlockSpec((tm, kvw), lambda i: (i, 4)),
                  pl.BlockSpec((D_MODEL, tm), lambda i: (0, i)), pl.BlockSpec((tm, kvw), row), pl.BlockSpec((tm, kvw), row),
                  pl.BlockSpec((tm, HEAD_DIM), row), pl.BlockSpec((tm, HEAD_DIM), row),
                  pl.BlockSpec((1, HEAD_DIM), const), pl.BlockSpec((1, HEAD_DIM), const)],
        out_specs=[pl.BlockSpec((tm, ODD_IN), row), pl.BlockSpec((8, HEAD_DIM), const), pl.BlockSpec((8, HEAD_DIM), const)],
        out_shape=[_sds((T, ODD_IN), _MXU), _sds((8, HEAD_DIM), F32), _sds((8, HEAD_DIM), F32)],
        compiler_params=_params("arbitrary"),
    )(proj, proj, dqxt, dkx, dvx, cos, sin, q_gain, k_gain)


SCORE_SCALE_LOG2 = ATT_SCALE * math.log2(math.e)


def _flash_fwd(qx, kx, vx, *, tq, tk):
    T = qx.shape[0]
    tq, tk = min(tq, T), min(tk, T)
    nq, nk = T // tq, T // tk
    group = AX_HEADS // AX_KV_HEADS

    def body(k_ref, v_ref, q_ref, o_ref, lse_ref, acc_sc, m_sc, l_sc):
        j = pl.program_id(2)

        @pl.when(j == 0)
        def _():
            m_sc[...] = jnp.full(m_sc.shape, NEG_INF, F32)
            l_sc[...] = jnp.zeros_like(l_sc)
            acc_sc[...] = jnp.zeros_like(acc_sc)
        k, v = k_ref[...], v_ref[...]

        def step(i, carry):
            cols = pl.ds(pl.multiple_of(i * tq, tq), tq)
            st = _dot(k, q_ref[cols, :], _NT) * SCORE_SCALE_LOG2
            m_old = m_sc[i]
            m_new = jnp.maximum(m_old, jnp.max(st, axis=0, keepdims=True))
            p = jnp.exp2(st - m_new)
            alpha = jnp.exp2(m_old - m_new)
            m_sc[i] = m_new
            l_sc[i] = alpha * l_sc[i] + jnp.sum(p, axis=0, keepdims=True)
            acc_sc[:, cols] = alpha * acc_sc[:, cols] + _dot(v, p, _TN)
            return carry

        lax.fori_loop(0, nq, step, 0)

        @pl.when(j == nk - 1)
        def _():
            def finish(i, carry):
                cols = pl.ds(pl.multiple_of(i * tq, tq), tq)
                o_ref[cols, :] = (acc_sc[:, cols] / l_sc[i]).T.astype(o_ref.dtype)
                lse_ref[0, i] = m_sc[i] + jnp.log2(l_sc[i])
                return carry

            lax.fori_loop(0, nq, finish, 0)

    kv = lambda g, h, j: (j, g)
    qh = lambda g, h, j: (0, g * group + h)
    o, lse = pl.pallas_call(
        body, name="flash_fwd", grid=(AX_KV_HEADS, group, nk),
        in_specs=[pl.BlockSpec((tk, HEAD_DIM), kv), pl.BlockSpec((tk, HEAD_DIM), kv), pl.BlockSpec((T, HEAD_DIM), qh)],
        out_specs=[pl.BlockSpec((T, HEAD_DIM), qh), pl.BlockSpec((1, nq, 1, tq), lambda g, h, j: (g * group + h, 0, 0, 0))],
        out_shape=[_sds((T, D_MODEL), _MXU), _sds((AX_HEADS, nq, 1, tq), F32)],
        scratch_shapes=[pltpu.VMEM((HEAD_DIM, T), F32), pltpu.VMEM((nq, 1, tq), F32), pltpu.VMEM((nq, 1, tq), F32)],
        compiler_params=_params("parallel", "arbitrary", "arbitrary"),
    )(kx, vx, qx)
    return o, lse.reshape(AX_HEADS, 1, T)


def _flash_delta(o, do, *, tq):
    T = o.shape[0]
    tq = min(tq, T)

    def body(o_ref, do_ref, delta_ref):
        delta_ref[0] = jnp.sum(do_ref[...].astype(F32) * o_ref[...].astype(F32), axis=-1, keepdims=True)

    blk = lambda h, i: (i, h)
    return pl.pallas_call(
        body, name="flash_delta", grid=(AX_HEADS, T // tq),
        in_specs=[pl.BlockSpec((tq, HEAD_DIM), blk), pl.BlockSpec((tq, HEAD_DIM), blk)],
        out_specs=pl.BlockSpec((1, tq, 1), lambda h, i: (h, i, 0)),
        out_shape=_sds((AX_HEADS, T, 1), F32),
        compiler_params=_params("parallel", "parallel"),
    )(o, do)


def _flash_bwd(qx, kx, vx, do, lse, delta, *, tq, tk):
    T = qx.shape[0]
    tq, tk = min(tq, T), min(tk, T)
    nq = T // tq
    group = AX_HEADS // AX_KV_HEADS
    lse_rows = lse.reshape(AX_HEADS, nq, 1, tq)
    delta_rows = delta.reshape(AX_HEADS, nq, 1, tq)

    def body(k_ref, v_ref, q_ref, do_ref, lse_ref, delta_ref, dqt_ref, dk_ref, dv_ref):
        j = pl.program_id(2)

        @pl.when(jnp.logical_and(pl.program_id(1) == 0, j == 0))
        def _():
            dk_ref[...] = jnp.zeros_like(dk_ref)
            dv_ref[...] = jnp.zeros_like(dv_ref)

        @pl.when(j == 0)
        def _():
            dqt_ref[...] = jnp.zeros_like(dqt_ref)
        k, v = k_ref[...], v_ref[...]

        def step(i, carry):
            dk, dv = carry
            off = pl.multiple_of(i * tq, tq)
            q, do_blk = q_ref[pl.ds(off, tq), :], do_ref[pl.ds(off, tq), :]
            pt = jnp.exp2(_dot(k, q, _NT) * SCORE_SCALE_LOG2 - lse_ref[0, i])
            dst = pt * (_dot(v, do_blk, _NT) - delta_ref[0, i])
            dqt_ref[:, pl.ds(off, tq)] += _dot(k, dst, _TN) * ATT_SCALE
            return dk + _dot(dst, q), dv + _dot(pt, do_blk)

        zero = jnp.zeros((tk, HEAD_DIM), F32)
        dk, dv = lax.fori_loop(0, nq, step, (zero, zero))
        rows = pl.ds(pl.multiple_of(j * tk, tk), tk)
        dk_ref[rows, :] += dk * ATT_SCALE
        dv_ref[rows, :] += dv

    kv = lambda g, h, j: (j, g)
    qh = lambda g, h, j: (0, g * group + h)
    st = lambda g, h, j: (g * group + h, 0, 0, 0)
    acc = lambda g, h, j: (0, g)
    return pl.pallas_call(
        body, name="flash_bwd", grid=(AX_KV_HEADS, group, T // tk),
        in_specs=[pl.BlockSpec((tk, HEAD_DIM), kv), pl.BlockSpec((tk, HEAD_DIM), kv),
                  pl.BlockSpec((T, HEAD_DIM), qh), pl.BlockSpec((T, HEAD_DIM), qh),
                  pl.BlockSpec((1, nq, 1, tq), st), pl.BlockSpec((1, nq, 1, tq), st)],
        out_specs=[pl.BlockSpec((HEAD_DIM, T), lambda g, h, j: (g * group + h, 0)),
                   pl.BlockSpec((T, HEAD_DIM), acc), pl.BlockSpec((T, HEAD_DIM), acc)],
        out_shape=[_sds((D_MODEL, T), F32), _sds((T, AX_KV_HEADS * HEAD_DIM), F32), _sds((T, AX_KV_HEADS * HEAD_DIM), F32)],
        compiler_params=_params("parallel", "arbitrary", "arbitrary"),
    )(kx, vx, qx, do, lse_rows, delta_rows)


TM = 1024
TM_WIDE = 512


def _mlp_fwd(tag, x, gain, w_up, w_down, target=None):
    u, h = _norm_matmul(f"mlp_up{tag}", x, gain, w_up, tm=TM, tn=1024, out_dtype=F32)
    out = _matmul_res(f"mlp_down{tag}", [u], w_down, x, tm=TM_WIDE, relu2=True, target=target)
    return out, (x, u, h)


def _mlp_bwd(tag, saved, gain, w_up, w_down, dy):
    x, u, h = saved
    du = _matmul_nt(f"mlp_down{tag}_bwd", dy, w_down, tm=TM, tn=1024, out_dtype=_MXU, relu_of=u)
    dw_down = _matmul_tn(f"mlp_down{tag}_dw", u, dy, tk=1024, tn=1024, tt=1024, out_dtype=_WIRE, relu2=True)
    dx, dgain = _matmul_nt_normbwd(f"mlp_up{tag}_bwd", du, w_up, x, gain, dy, tm=TM_WIDE)
    dw_up = _matmul_tn(f"mlp_up{tag}_dw", h, du, tk=1024, tn=1024, tt=1024, out_dtype=_WIRE)
    return dx, dgain, dw_up, dw_down


def _local_step(x, target, p, wf):
    T = x.shape[0]
    cos_r, sin_r = _ret_rope_tables(T)
    cos_a, sin_a = _axial_rope_tables(T)
    tabs, rw, log_gamma = _retention_tables(p["ret_decay_logit"][0])
    bias = _swa_bias(p["t5_table"])
    sink = p["swa_sink"][0][:, None] * jnp.ones((1, HEAD_DIM), F32)
    nm, nl = p["norm_mix"], p["norm_mlp"]

    proj0, h0 = _norm_matmul("in_even", x, nm[0:1], wf["w_in_even"], tm=TM, tn=1152, out_dtype=F32)
    qr, kr, qn, kn = _prep_even(proj0, cos_r, sin_r, p["swa_q_norm"], p["swa_k_norm"], tm=TM)
    sf, sb = _ret_scan("ret_scan_fwd", kr, proj0, 1, rw["cf"], rw["dec_f"], rw["cb"], rw["dec_b"])
    ret_o, ya = _ret_out(qr, kr, proj0, sf, sb, tabs[:, (TAB_D, TAB_A, TAB_B)], p["ret_norm"])
    yb = _swa_fwd(qn, kn, proj0, bias, sink)
    x1 = _matmul_res("out_even", [ya, yb], wf["w_out_even"], x, tm=TM)
    x2, mlp0 = _mlp_fwd(0, x1, nl[0:1], wf["w_mlp_up"][0], wf["w_mlp_down"][0])
    proj1, h1 = _norm_matmul("in_odd", x2, nm[1:2], wf["w_in_odd"], tm=TM, tn=768, out_dtype=F32)
    qx, kx, vx = _prep_odd(proj1, cos_a, sin_a, p["ax_q_norm"], p["ax_k_norm"], tm=TM)
    o, lse = _flash_fwd(qx, kx, vx, tq=1024, tk=512)
    x3 = _matmul_res("out_odd", [o], wf["w_out_odd"], x2, tm=TM)
    (g4, loss_part), mlp1 = _mlp_fwd(1, x3, nl[1:2], wf["w_mlp_up"][1], wf["w_mlp_down"][1], target=target)

    dx3, dnl1, dw_up1, dw_down1 = _mlp_bwd(1, mlp1, nl[1:2], wf["w_mlp_up"][1], wf["w_mlp_down"][1], g4)
    do = _matmul_nt("out_odd_bwd", dx3, wf["w_out_odd"], tm=TM, tn=1024, out_dtype=_MXU)
    dw_out_odd = _matmul_tn("out_odd_dw", o, dx3, tk=1024, tn=1024, tt=1024, out_dtype=_WIRE)
    delta = _flash_delta(o, do, tq=1024)
    dqxt, dkx, dvx = _flash_bwd(qx, kx, vx, do, lse, delta, tq=1024, tk=512)
    dproj1, dqg1, dkg1 = _post_odd(proj1, dqxt, dkx, dvx, cos_a, sin_a, p["ax_q_norm"], p["ax_k_norm"], tm=TM)
    dx2, dnm1 = _matmul_nt_normbwd("in_odd_bwd", dproj1, wf["w_in_odd"], x2, nm[1:2], dx3, tm=TM_WIDE)
    dw_in_odd = _matmul_tn("in_odd_dw", h1, dproj1, tk=1024, tn=768, tt=1024, out_dtype=_WIRE)
    dx1, dnl0, dw_up0, dw_down0 = _mlp_bwd(0, mlp0, nl[0:1], wf["w_mlp_up"][0], wf["w_mlp_down"][0], dx2)
    dycat = _matmul_nt("out_even_bwd", dx1, wf["w_out_even"], tm=TM, tn=1024, out_dtype=F32)
    dw_out_even = jnp.concatenate([_matmul_tn("out_even_dw_ret", ya, dx1, tk=1024, tn=1024, tt=1024, out_dtype=_WIRE),
                                   _matmul_tn("out_even_dw_swa", yb, dx1, tk=1024, tn=1024, tt=1024, out_dtype=_WIRE)], axis=0)
    g_out, dga, dretg = _ret_gate_bwd(dycat, proj0, ret_o, p["ret_norm"], tm=TM)
    rb, rf = _ret_scan("ret_scan_bwd", qr, g_out, 0, rw["b"], rw["dec_b"], rw["a"], rw["dec_f"])
    dqr, dkr, dva, dlog = _ret_bwd(qr, kr, proj0, g_out, sf, sb, rf, rb, tabs)
    dqn, dkn, dvb, dbias, dsink = _swa_bwd(qn, kn, proj0, dycat, bias, sink)
    dt5 = _t5_bucket_reduce(dbias, _t5_bucket(_swa_rel()).astype(jnp.int32))
    dproj0, dqg0, dkg0 = _post_even(proj0, dqr, dkr, dva, dga, dqn, dkn, dvb, cos_r, sin_r,
                                    p["swa_q_norm"], p["swa_k_norm"], tm=TM_WIDE)
    dx0, dnm0 = _matmul_nt_normbwd("in_even_bwd", dproj0, wf["w_in_even"], x, nm[0:1], dx1, tm=TM_WIDE)
    dw_in_even = _matmul_tn("in_even_dw", h0, dproj0, tk=1024, tn=1152, tt=1024, out_dtype=_WIRE)

    big = {
        "w_in_even": dw_in_even[None], "w_out_even": dw_out_even[None],
        "w_in_odd": dw_in_odd[None], "w_out_odd": dw_out_odd[None],
        "w_mlp_up": jnp.stack([dw_up0, dw_up1]), "w_mlp_down": jnp.stack([dw_down0, dw_down1]),
    }
    fold = lambda part: jnp.sum(part, axis=0)
    dlam = jnp.sum(dlog, axis=1).reshape(RET_HEADS, 2).T
    small = {
        "norm_mix": jnp.stack([fold(dnm0), fold(dnm1)]),
        "norm_mlp": jnp.stack([fold(dnl0), fold(dnl1)]),
        "ret_decay_logit": (dlam * (1.0 - jnp.exp(log_gamma)))[None],
        "ret_norm": fold(dretg)[None],
        "swa_q_norm": fold(dqg0)[None], "swa_k_norm": fold(dkg0)[None],
        "swa_sink": dsink[:, 0][None],
        "t5_table": dt5[:, :T5_BUCKETS].T,
        "ax_q_norm": fold(dqg1)[None], "ax_k_norm": fold(dkg1)[None],
    }
    return loss_part, dx0, big, small


BIG = ("w_in_even", "w_out_even", "w_in_odd", "w_out_odd", "w_mlp_up", "w_mlp_down")
SMALL = ("norm_mix", "norm_mlp", "ret_decay_logit", "ret_norm", "swa_q_norm", "swa_k_norm", "swa_sink", "t5_table",
         "ax_q_norm", "ax_k_norm")
WEIGHTS = ("norm_mix", "norm_mlp", "w_in_even", "w_out_even", "ret_decay_logit", "ret_norm", "swa_q_norm", "swa_k_norm",
           "swa_sink", "t5_table", "w_in_odd", "w_out_odd", "ax_q_norm", "ax_k_norm", "w_mlp_up", "w_mlp_down")
SHARD_AXIS = {"w_in_even": 2, "w_out_even": 1, "w_in_odd": 2, "w_out_odd": 1, "w_mlp_up": 2, "w_mlp_down": 1}
HALF_AXIS = {"w_in_even": 1, "w_out_even": 1, "w_in_odd": 1, "w_out_odd": 1, "w_mlp_up": 0, "w_mlp_down": 0}
N_CHIPS = 4
ANY = pl.BlockSpec(memory_space=pl.ANY)


def _mesh_pos():
    return lax.axis_index("x"), lax.axis_index("y"), lax.axis_index("c")


def _window(ref, axis, start, size):
    idx = [slice(None)] * len(ref.shape)
    idx[axis] = pl.ds(start, size)
    return ref.at[tuple(idx)]


def _cast_mxu(name, w, *, tr=256):
    R, C = w.shape
    tr = min(tr, R)

    def body(w_ref, o_ref):
        o_ref[...] = w_ref[...].astype(o_ref.dtype)

    return pl.pallas_call(body, name=name, grid=(R // tr,), in_specs=[pl.BlockSpec((tr, C), lambda i: (i, 0))],
                          out_specs=pl.BlockSpec((tr, C), lambda i: (i, 0)), out_shape=_sds((R, C), _MXU),
                          compiler_params=_params("parallel"))(w)


def _allgather_weights(shards):
    names = list(shards)
    n = len(names)
    sizes = [shards[k].shape[SHARD_AXIS[k]] for k in names]

    def body(*refs):
        ins, outs = refs[:n], refs[n:2 * n]
        local_sems, send_sems, recv_sems = refs[2 * n:]
        x, y, c = _mesh_pos()
        chips = [(1 - x, y), (x, 1 - y), (1 - x, 1 - y)]
        slot = lambda t, px, py: _window(outs[t], SHARD_AXIS[names[t]], pl.multiple_of((2 * px + py) * sizes[t], 128), sizes[t])
        local, remote = [], []
        for t in range(n):
            local.append(pltpu.make_async_copy(ins[t], slot(t, x, y), local_sems.at[t]))
            local[-1].start()
            for k, (px, py) in enumerate(chips):
                remote.append(pltpu.make_async_remote_copy(ins[t], slot(t, x, y), send_sems.at[3 * t + k], recv_sems.at[3 * t + k],
                                                           device_id=(px, py, c), device_id_type=MESH))
                remote[-1].start()
        for t in range(n):
            for k, (px, py) in enumerate(chips):
                pltpu.make_async_remote_copy(ins[t], slot(t, px, py), send_sems.at[3 * t + k], recv_sems.at[3 * t + k],
                                             device_id=(px, py, c), device_id_type=MESH).wait_recv()
        for cp in remote:
            cp.wait_send()
        for cp in local:
            cp.wait()

    full_shape = lambda k: tuple(d * (N_CHIPS if a == SHARD_AXIS[k] else 1) for a, d in enumerate(shards[k].shape))
    outs = pl.pallas_call(
        body, name="allgather_weights", in_specs=[ANY] * n, out_specs=[ANY] * n,
        out_shape=[_sds(full_shape(k), shards[k].dtype) for k in names],
        scratch_shapes=[pltpu.SemaphoreType.DMA((n,)), pltpu.SemaphoreType.DMA((3 * n,)), pltpu.SemaphoreType.DMA((3 * n,))],
    )(*[shards[k] for k in names])
    return dict(zip(names, outs))


FLIPS = [(a, b, d) for a in (0, 1) for b in (0, 1) for d in (0, 1) if (a, b, d) != (0, 0, 0)]


def _flip(pos, f):
    return tuple(1 - p if fi else p for p, fi in zip(pos, f))


def _piece_shape(name, shape):
    out = list(shape)
    out[SHARD_AXIS[name]] //= N_CHIPS
    out[HALF_AXIS[name]] //= 2
    return tuple(out)


def _piece(ref, name, chip, core, shard_size, half_size):
    sa, ha = SHARD_AXIS[name], HALF_AXIS[name]
    if sa == ha:
        return _window(ref, sa, pl.multiple_of(chip * shard_size + core * half_size, 8), half_size)
    half_start = pl.multiple_of(core * half_size, 8) if ha else core * half_size
    return _window(_window(ref, sa, pl.multiple_of(chip * shard_size, 128), shard_size), ha, half_start, half_size)


def _scatter_gradients(grads):
    names = list(grads)
    n = len(names)
    n_peer = len(FLIPS)
    pieces = [_piece_shape(k, grads[k].shape) for k in names]
    shard_sizes = [grads[k].shape[SHARD_AXIS[k]] // N_CHIPS for k in names]
    half_sizes = [p[HALF_AXIS[k]] for k, p in zip(names, pieces)]

    def body(*refs):
        ins, outs = refs[:n], refs[n:2 * n]
        local_sems, send_sems, recv_sems = refs[2 * n:]
        pos = _mesh_pos()
        ident = lambda p: 4 * p[0] + 2 * p[1] + p[2]
        me = ident(pos)
        src = lambda t, p: _piece(ins[t], names[t], 2 * p[0] + p[1], p[2], shard_sizes[t], half_sizes[t])
        local, remote = [], []
        for t in range(n):
            local.append(pltpu.make_async_copy(src(t, pos), outs[t].at[me], local_sems.at[t]))
            local[-1].start()
            for k, f in enumerate(FLIPS):
                peer = _flip(pos, f)
                remote.append(pltpu.make_async_remote_copy(src(t, peer), outs[t].at[me], send_sems.at[n_peer * t + k],
                                                           recv_sems.at[n_peer * t + k], device_id=peer, device_id_type=MESH))
                remote[-1].start()
        for t in range(n):
            for k, f in enumerate(FLIPS):
                peer = _flip(pos, f)
                pltpu.make_async_remote_copy(src(t, pos), outs[t].at[ident(peer)], send_sems.at[n_peer * t + k],
                                             recv_sems.at[n_peer * t + k], device_id=peer, device_id_type=MESH).wait_recv()
        for cp in remote:
            cp.wait_send()
        for cp in local:
            cp.wait()

    outs = pl.pallas_call(
        body, name="scatter_gradients", in_specs=[ANY] * n, out_specs=[ANY] * n,
        out_shape=[_sds((N_DEV,) + p, grads[k].dtype) for k, p in zip(names, pieces)],
        scratch_shapes=[pltpu.SemaphoreType.DMA((n,)), pltpu.SemaphoreType.DMA((n_peer * n,)), pltpu.SemaphoreType.DMA((n_peer * n,))],
    )(*[grads[k] for k in names])
    return dict(zip(names, outs))


def _sum_slots(name, key, buf, shard_shape, core, *, tr=128):
    _, L, R, C = buf.shape
    tr = min(tr, R)
    layer_half = HALF_AXIS[key] == 0

    def body(core_ref, b_ref, o_ref):
        acc = b_ref[0].astype(F32)
        for s in range(1, N_DEV):
            acc = acc + b_ref[s].astype(F32)
        o_ref[...] = acc

    if layer_half:
        out_map = lambda l, i, core_ref: (l + core_ref[0], i, 0)
    else:
        out_map = lambda l, i, core_ref: (l, i + core_ref[0] * (R // tr), 0)
    grid_spec = pltpu.PrefetchScalarGridSpec(
        num_scalar_prefetch=1, grid=(L, R // tr),
        in_specs=[pl.BlockSpec((N_DEV, 1, tr, C), lambda l, i, core_ref: (0, l, i, 0))],
        out_specs=pl.BlockSpec((1, tr, C), out_map))
    return pl.pallas_call(body, name=name, grid_spec=grid_spec, out_shape=_sds(shard_shape, F32),
                          compiler_params=_params("parallel", "parallel"))(core, buf)


def _exchange_halves(shards):
    names = list(shards)
    n = len(names)
    half_sizes = [shards[k].shape[HALF_AXIS[k]] // 2 for k in names]

    def body(*refs):
        outs = refs[n:2 * n]
        send_sems, recv_sems = refs[2 * n:]
        x, y, c = _mesh_pos()
        half = lambda t, core: _window(outs[t], HALF_AXIS[names[t]], pl.multiple_of(core * half_sizes[t], 8)
                                       if HALF_AXIS[names[t]] else core * half_sizes[t], half_sizes[t])
        sends = []
        for t in range(n):
            sends.append(pltpu.make_async_remote_copy(half(t, c), half(t, c), send_sems.at[t], recv_sems.at[t],
                                                      device_id=(x, y, 1 - c), device_id_type=MESH))
            sends[-1].start()
        for t in range(n):
            pltpu.make_async_remote_copy(half(t, c), half(t, 1 - c), send_sems.at[t], recv_sems.at[t],
                                         device_id=(x, y, 1 - c), device_id_type=MESH).wait_recv()
        for cp in sends:
            cp.wait_send()

    outs = pl.pallas_call(
        body, name="exchange_halves", in_specs=[ANY] * n, out_specs=[ANY] * n,
        out_shape=[_sds(shards[k].shape, F32) for k in names],
        input_output_aliases={t: t for t in range(n)},
        scratch_shapes=[pltpu.SemaphoreType.DMA((n,)), pltpu.SemaphoreType.DMA((n,))],
    )(*[shards[k] for k in names])
    return dict(zip(names, outs))


def _adamw_math(w, g, m, v):
    m = ADAM_B1 * m + (1.0 - ADAM_B1) * g
    v = ADAM_B2 * v + (1.0 - ADAM_B2) * jnp.square(g)
    m_hat = m / (1.0 - ADAM_B1 ** ADAM_STEP)
    v_hat = v / (1.0 - ADAM_B2 ** ADAM_STEP)
    return -ADAM_LR * (m_hat / (jnp.sqrt(v_hat) + ADAM_EPS) + ADAM_WD * w), m, v


def _adamw(name, w, g, m, v, *, tr=256):
    R, C = w.shape
    tr = min(tr, R)

    def body(w_ref, g_ref, m_ref, v_ref, d_ref, mo_ref, vo_ref):
        d_ref[...], mo_ref[...], vo_ref[...] = _adamw_math(w_ref[...], g_ref[...], m_ref[...], v_ref[...])

    spec = pl.BlockSpec((tr, C), lambda i: (i, 0))
    return pl.pallas_call(body, name=name, grid=(R // tr,), in_specs=[spec] * 4, out_specs=[spec] * 3,
                          out_shape=[_sds((R, C), F32)] * 3, compiler_params=_params("parallel"))(w, g, m, v)


SLAB_ROWS = 8
LOSS_ROW = 7


def _pack_small(d):
    pad = lambda a, width: jnp.pad(a.reshape(-1), (0, width - a.size))
    row5 = jnp.concatenate([d["swa_q_norm"].reshape(-1), d["swa_k_norm"].reshape(-1), d["ax_q_norm"].reshape(-1),
                            d["ax_k_norm"].reshape(-1), pad(d["swa_sink"], HEAD_DIM), pad(d["ret_decay_logit"], HEAD_DIM),
                            jnp.zeros((2 * HEAD_DIM,), F32)])
    return jnp.concatenate([d["norm_mix"], d["norm_mlp"], d["ret_norm"], row5[None], pad(d["t5_table"], D_MODEL)[None],
                            jnp.zeros((1, D_MODEL), F32)], axis=0)


def _unpack_small(slab):
    r5 = slab[5]
    return {
        "norm_mix": slab[0:2], "norm_mlp": slab[2:4], "ret_norm": slab[4:5],
        "swa_q_norm": r5[None, 0:128], "swa_k_norm": r5[None, 128:256], "ax_q_norm": r5[None, 256:384],
        "ax_k_norm": r5[None, 384:512], "swa_sink": r5[None, 512:512 + SWA_HEADS],
        "ret_decay_logit": r5[640:640 + 2 * RET_HEADS].reshape(1, 2, RET_HEADS),
        "t5_table": slab[6, :T5_BUCKETS * SWA_HEADS].reshape(T5_BUCKETS, SWA_HEADS),
    }


def _small_allreduce_adamw(g_slab, w_slab, m_slab, v_slab, loss_part):
    def body(g_ref, w_ref, m_ref, v_ref, lp_ref, go_ref, d_ref, mo_ref, vo_ref, gath, send_sems, recv_sems):
        pos = _mesh_pos()
        ident = lambda p: 4 * p[0] + 2 * p[1] + p[2]
        me = ident(pos)
        row = lax.broadcasted_iota(jnp.int32, (SLAB_ROWS, D_MODEL), 0)
        lane = lax.broadcasted_iota(jnp.int32, (SLAB_ROWS, D_MODEL), 1)
        loss = jnp.sum(jnp.sum(lp_ref[...], axis=0, keepdims=True), axis=1, keepdims=True) * (0.5 / D_MODEL)
        gath[me] = jnp.where(jnp.logical_and(row == LOSS_ROW, lane == 0), loss, g_ref[...])
        sends = []
        for k, f in enumerate(FLIPS):
            sends.append(pltpu.make_async_remote_copy(gath.at[me], gath.at[me], send_sems.at[k], recv_sems.at[k],
                                                      device_id=_flip(pos, f), device_id_type=MESH))
            sends[-1].start()
        for k, f in enumerate(FLIPS):
            peer = _flip(pos, f)
            pltpu.make_async_remote_copy(gath.at[me], gath.at[ident(peer)], send_sems.at[k], recv_sems.at[k],
                                         device_id=peer, device_id_type=MESH).wait_recv()
        for cp in sends:
            cp.wait_send()
        total = gath[0]
        for s in range(1, N_DEV):
            total = total + gath[s]
        go_ref[...] = total
        d_ref[...], mo_ref[...], vo_ref[...] = _adamw_math(w_ref[...], total, m_ref[...], v_ref[...])

    vmem = pl.BlockSpec(memory_space=pltpu.VMEM)
    return pl.pallas_call(
        body, name="small_allreduce_adamw", in_specs=[vmem] * 5, out_specs=[vmem] * 4,
        out_shape=[_sds((SLAB_ROWS, D_MODEL), F32)] * 4,
        scratch_shapes=[pltpu.VMEM((N_DEV, SLAB_ROWS, D_MODEL), F32),
                        pltpu.SemaphoreType.DMA((len(FLIPS),)), pltpu.SemaphoreType.DMA((len(FLIPS),))],
    )(g_slab, w_slab, m_slab, v_slab, loss_part)


def kernel(x, norm_mix, norm_mlp, w_in_even, w_out_even, ret_decay_logit, ret_norm, swa_q_norm, swa_k_norm, swa_sink, t5_table, w_in_odd, w_out_odd, ax_q_norm, ax_k_norm, w_mlp_up, w_mlp_down, loss_target, m_norm_mix, m_norm_mlp, m_w_in_even, m_w_out_even, m_ret_decay_logit, m_ret_norm, m_swa_q_norm, m_swa_k_norm, m_swa_sink, m_t5_table, m_w_in_odd, m_w_out_odd, m_ax_q_norm, m_ax_k_norm, m_w_mlp_up, m_w_mlp_down, v_norm_mix, v_norm_mlp, v_w_in_even, v_w_out_even, v_ret_decay_logit, v_ret_norm, v_swa_q_norm, v_swa_k_norm, v_swa_sink, v_t5_table, v_w_in_odd, v_w_out_odd, v_ax_q_norm, v_ax_k_norm, v_w_mlp_up, v_w_mlp_down):
    w = dict(zip(WEIGHTS, (norm_mix, norm_mlp, w_in_even, w_out_even, ret_decay_logit, ret_norm, swa_q_norm, swa_k_norm,
                           swa_sink, t5_table, w_in_odd, w_out_odd, ax_q_norm, ax_k_norm, w_mlp_up, w_mlp_down)))
    m = dict(zip(WEIGHTS, (m_norm_mix, m_norm_mlp, m_w_in_even, m_w_out_even, m_ret_decay_logit, m_ret_norm, m_swa_q_norm,
                           m_swa_k_norm, m_swa_sink, m_t5_table, m_w_in_odd, m_w_out_odd, m_ax_q_norm, m_ax_k_norm,
                           m_w_mlp_up, m_w_mlp_down)))
    v = dict(zip(WEIGHTS, (v_norm_mix, v_norm_mlp, v_w_in_even, v_w_out_even, v_ret_decay_logit, v_ret_norm, v_swa_q_norm,
                           v_swa_k_norm, v_swa_sink, v_t5_table, v_w_in_odd, v_w_out_odd, v_ax_q_norm, v_ax_k_norm,
                           v_w_mlp_up, v_w_mlp_down)))
    flat = lambda a: a.reshape(-1, a.shape[-1])

    shards = {k: _cast_mxu(f"cast_{k}", flat(w[k])).reshape(w[k].shape) for k in BIG}
    whole = _allgather_weights(shards)
    wf = {k: (whole[k] if k.startswith("w_mlp") else whole[k][0]) for k in BIG}

    loss_part, dx, big_g, small_g = _local_step(x[0], loss_target[0], {k: w[k] for k in SMALL}, wf)

    slots = _scatter_gradients(big_g)
    core = lax.axis_index("c").astype(jnp.int32).reshape(1)
    grad = _exchange_halves({k: _sum_slots(f"sum_{k}", k, slots[k], w[k].shape, core) for k in BIG})
    delta, new_m, new_v = {}, {}, {}
    for k in BIG:
        d_k, m_k, v_k = _adamw(f"adamw_{k}", flat(w[k]), flat(grad[k]), flat(m[k]), flat(v[k]))
        delta[k], new_m[k], new_v[k] = d_k.reshape(w[k].shape), m_k.reshape(w[k].shape), v_k.reshape(w[k].shape)

    slabs = _small_allreduce_adamw(_pack_small(small_g), _pack_small({k: w[k] for k in SMALL}),
                                   _pack_small({k: m[k] for k in SMALL}), _pack_small({k: v[k] for k in SMALL}), loss_part)
    loss = slabs[0][LOSS_ROW, 0]
    for out, slab in zip((grad, delta, new_m, new_v), slabs):
        out.update(_unpack_small(slab))

    return (loss, dx[None], *[grad[k] for k in WEIGHTS], *[delta[k] for k in WEIGHTS],
            *[new_m[k] for k in WEIGHTS], *[new_v[k] for k in WEIGHTS])
```

```python
import functools
import math

import jax
import jax.numpy as jnp
from jax import lax
from jax.experimental import pallas as pl
from jax.experimental.pallas import tpu as pltpu

F32 = jnp.float32
BF16 = jnp.bfloat16
_MXU = BF16
_WIRE = BF16

D_MODEL = 1024
HEAD_DIM = 128
EPS = 1e-6
NEG_INF = -1e30
CHUNK = 128
GRID_W = 64
RET_HEADS, RET_DK, RET_DV = 4, 128, 256
RET_Q, RET_V = RET_HEADS * RET_DK, RET_HEADS * RET_DV
RET_THETA = 10000.0
SWA_HEADS, SWA_KV_HEADS = 8, 2
T5_BUCKETS, T5_MAX_DIST = 32, 128
AX_HEADS, AX_KV_HEADS = 8, 2
AX_THETA = 10000.0
D_FF = 4 * D_MODEL
EVEN_IN = 2 * RET_Q + 2 * RET_V + D_MODEL + 2 * SWA_KV_HEADS * HEAD_DIM
ODD_IN = D_MODEL + 2 * AX_KV_HEADS * HEAD_DIM
ATT_SCALE = HEAD_DIM ** -0.5

ADAM_LR, ADAM_B1, ADAM_B2, ADAM_EPS, ADAM_WD, ADAM_STEP = 0.001, 0.9, 0.999, 1e-08, 0.01, 10

N_DEV = 8
VMEM_LIMIT_BYTES = 56 << 20
MESH = pl.DeviceIdType.MESH

_NN = (((1,), (0,)), ((), ()))
_NT = (((1,), (1,)), ((), ()))
_TN = (((0,), (0,)), ((), ()))


def _dot(a, b, dn=_NN):
    return lax.dot_general(a.astype(_MXU), b.astype(_MXU), dn, preferred_element_type=F32)


def _params(*sem):
    return pltpu.CompilerParams(dimension_semantics=sem, vmem_limit_bytes=VMEM_LIMIT_BYTES)


def _sds(shape, dtype):
    return jax.ShapeDtypeStruct(tuple(shape), dtype)


def _rowsum8(x):
    return jnp.sum(x.reshape(x.shape[0] // 8, 8, x.shape[1]), axis=0)


def _swap_halves(x, half):
    width = x.shape[1]
    lane = lax.broadcasted_iota(jnp.int32, x.shape, 1)
    up = pltpu.roll(x, width - half, axis=1)
    down = pltpu.roll(x, half, axis=1)
    return jnp.where((lane & (2 * half - 1)) < half, up, down)


def _sigmoid(x):
    return 1.0 / (1.0 + jnp.exp(-x))


def _norm_matmul(name, x, gain, w, *, tm, tn, out_dtype):
    T, K = x.shape
    N = w.shape[1]
    tm, tn = min(tm, T), min(tn, N)

    def body(x_ref, g_ref, w_ref, y_ref, h_ref, h_sc):
        @pl.when(pl.program_id(1) == 0)
        def _():
            xv = x_ref[...]
            r = lax.rsqrt(jnp.mean(xv * xv, axis=-1, keepdims=True) + EPS)
            h = (xv * r * g_ref[...]).astype(_MXU)
            h_sc[...] = h
            h_ref[...] = h
        y_ref[...] = jnp.dot(h_sc[...], w_ref[...], preferred_element_type=F32).astype(y_ref.dtype)

    return pl.pallas_call(
        body, name=name, grid=(T // tm, N // tn),
        in_specs=[pl.BlockSpec((tm, K), lambda i, j: (i, 0)),
                  pl.BlockSpec((1, K), lambda i, j: (0, 0)),
                  pl.BlockSpec((K, tn), lambda i, j: (0, j))],
        out_specs=[pl.BlockSpec((tm, tn), lambda i, j: (i, j)),
                   pl.BlockSpec((tm, K), lambda i, j: (i, 0))],
        out_shape=[_sds((T, N), out_dtype), _sds((T, K), _MXU)],
        scratch_shapes=[pltpu.VMEM((tm, K), _MXU)],
        compiler_params=_params("parallel", "arbitrary"),
    )(x, gain, w)


def _matmul_res(name, a_list, w, res, *, tm, relu2=False, target=None):
    T = res.shape[0]
    N = w.shape[1]
    K = a_list[0].shape[1]
    n_a = len(a_list)
    tm = min(tm, T)
    with_loss = target is not None

    def body(*refs):
        a_refs = refs[:n_a]
        w_refs = refs[n_a:2 * n_a]
        res_ref = refs[2 * n_a]
        acc = res_ref[...]
        for a_ref, w_ref in zip(a_refs, w_refs):
            a = a_ref[...]
            if relu2:
                a = jnp.square(jnp.maximum(a.astype(F32), 0.0))
            acc = acc + _dot(a, w_ref[...])
        if with_loss:
            tgt_ref, g_ref, loss_ref = refs[2 * n_a + 1:]
            diff = acc - tgt_ref[...]
            g_ref[...] = diff * (1.0 / N)

            @pl.when(pl.program_id(0) == 0)
            def _():
                loss_ref[...] = jnp.zeros_like(loss_ref)
            loss_ref[...] += _rowsum8(diff * diff)
        else:
            refs[2 * n_a + 1][...] = acc

    row = lambda i: (i, 0)
    in_specs = [pl.BlockSpec((tm, K), row) for _ in a_list]
    in_specs += [pl.BlockSpec((K, N), functools.partial(lambda i, b: (b, 0), b=b)) for b in range(n_a)]
    in_specs += [pl.BlockSpec((tm, N), row)]
    args = list(a_list) + [w] * n_a + [res]
    if with_loss:
        in_specs.append(pl.BlockSpec((tm, N), row))
        args.append(target)
        out_specs = [pl.BlockSpec((tm, N), row), pl.BlockSpec((8, N), lambda i: (0, 0))]
        out_shape = [_sds((T, N), F32), _sds((8, N), F32)]
        sem = "arbitrary"
    else:
        out_specs = pl.BlockSpec((tm, N), row)
        out_shape = _sds((T, N), F32)
        sem = "parallel"
    return pl.pallas_call(body, name=name, grid=(T // tm,), in_specs=in_specs, out_specs=out_specs,
                          out_shape=out_shape, compiler_params=_params(sem))(*args)


def _matmul_nt(name, a, w, *, tm, tn, out_dtype, relu_of=None):
    T, K = a.shape
    N = w.shape[0]
    tm, tn = min(tm, T), min(tn, N)

    def body(*refs):
        if relu_of is None:
            a_ref, w_ref, o_ref = refs
            o_ref[...] = _dot(a_ref[...], w_ref[...], _NT).astype(o_ref.dtype)
        else:
            a_ref, w_ref, u_ref, o_ref = refs
            da = _dot(a_ref[...], w_ref[...], _NT)
            o_ref[...] = (da * (2.0 * jnp.maximum(u_ref[...].astype(F32), 0.0))).astype(o_ref.dtype)

    in_specs = [pl.BlockSpec((tm, K), lambda i, j: (i, 0)), pl.BlockSpec((tn, K), lambda i, j: (j, 0))]
    args = [a, w]
    if relu_of is not None:
        in_specs.append(pl.BlockSpec((tm, tn), lambda i, j: (i, j)))
        args.append(relu_of)
    return pl.pallas_call(body, name=name, grid=(T // tm, N // tn), in_specs=in_specs,
                          out_specs=pl.BlockSpec((tm, tn), lambda i, j: (i, j)),
                          out_shape=_sds((T, N), out_dtype),
                          compiler_params=_params("parallel", "parallel"))(*args)


def _matmul_nt_normbwd(name, dy, w, x, gain, dres, *, tm):
    T, K = dy.shape
    N = w.shape[0]
    tm = min(tm, T)

    def body(dy_ref, w_ref, x_ref, g_ref, dres_ref, dx_ref, dg_ref):
        dh = _dot(dy_ref[...], w_ref[...], _NT)
        xv = x_ref[...]
        r = lax.rsqrt(jnp.mean(xv * xv, axis=-1, keepdims=True) + EPS)
        xhat = xv * r
        dxhat = dh * g_ref[...]
        dx_ref[...] = dres_ref[...] + r * (dxhat - xhat * jnp.mean(dxhat * xhat, axis=-1, keepdims=True))

        @pl.when(pl.program_id(0) == 0)
        def _():
            dg_ref[...] = jnp.zeros_like(dg_ref)
        dg_ref[...] += _rowsum8(dh * xhat)

    row = lambda i: (i, 0)
    return pl.pallas_call(
        body, name=name, grid=(T // tm,),
        in_specs=[pl.BlockSpec((tm, K), row), pl.BlockSpec((N, K), lambda i: (0, 0)),
                  pl.BlockSpec((tm, N), row), pl.BlockSpec((1, N), lambda i: (0, 0)), pl.BlockSpec((tm, N), row)],
        out_specs=[pl.BlockSpec((tm, N), row), pl.BlockSpec((8, N), lambda i: (0, 0))],
        out_shape=[_sds((T, N), F32), _sds((8, N), F32)],
        compiler_params=_params("arbitrary"),
    )(dy, w, x, gain, dres)


def _matmul_tn(name, a, b, *, tk, tn, tt, out_dtype, relu2=False):
    T, Ka = a.shape
    Nb = b.shape[1]
    tk, tn, tt = min(tk, Ka), min(tn, Nb), min(tt, T)
    nt = T // tt

    def body(a_ref, b_ref, o_ref, acc):
        t = pl.program_id(2)

        @pl.when(t == 0)
        def _():
            acc[...] = jnp.zeros_like(acc)
        av = a_ref[...]
        if relu2:
            av = jnp.square(jnp.maximum(av.astype(F32), 0.0))
        acc[...] += _dot(av, b_ref[...], _TN)

        @pl.when(t == nt - 1)
        def _():
            o_ref[...] = acc[...].astype(o_ref.dtype)

    return pl.pallas_call(
        body, name=name, grid=(Ka // tk, Nb // tn, nt),
        in_specs=[pl.BlockSpec((tt, tk), lambda i, j, t: (t, i)), pl.BlockSpec((tt, tn), lambda i, j, t: (t, j))],
        out_specs=pl.BlockSpec((tk, tn), lambda i, j, t: (i, j)),
        out_shape=_sds((Ka, Nb), out_dtype),
        scratch_shapes=[pltpu.VMEM((tk, tn), F32)],
        compiler_params=_params("parallel", "parallel", "arbitrary"),
    )(a, b)


def _rope_angles(pos, dim, theta):
    inv = theta ** (-jnp.arange(0, dim, 2, dtype=F32) / dim)
    return pos.astype(F32)[:, None] * inv[None, :]


def _ret_rope_tables(T):
    ang = _rope_angles(jnp.arange(T), RET_DK, RET_THETA)
    c, s = jnp.cos(ang), jnp.sin(ang)
    return jnp.concatenate([c, c], axis=1), jnp.concatenate([-s, s], axis=1)


def _axial_rope_tables(T):
    rows = T // GRID_W
    row = jnp.repeat(jnp.arange(rows), GRID_W)
    col = jnp.tile(jnp.arange(GRID_W), rows)
    ar = _rope_angles(row, HEAD_DIM // 2, AX_THETA)
    ac = _rope_angles(col, HEAD_DIM // 2, AX_THETA)
    cos = jnp.concatenate([jnp.cos(ar), jnp.cos(ar), jnp.cos(ac), jnp.cos(ac)], axis=1)
    sin = jnp.concatenate([-jnp.sin(ar), jnp.sin(ar), -jnp.sin(ac), jnp.sin(ac)], axis=1)
    return cos, sin


(TAB_D, TAB_DT, TAB_EF, TAB_EB, TAB_A, TAB_B, TAB_CF, TAB_CB,
 TAB_RA, TAB_RB, TAB_RCF, TAB_RCB, TAB_KF, TAB_KB) = range(14)


def _retention_tables(decay_logit):
    lg = jax.nn.log_sigmoid(decay_logit.astype(F32))
    lam, mu = lg[0][:, None, None], lg[1][:, None, None]
    idx = jnp.arange(CHUNK, dtype=F32)
    diff = (idx[:, None] - idx[None, :])[None]
    df = jnp.where(diff >= 0, jnp.exp(jnp.maximum(diff, 0.0) * lam), 0.0)
    db = jnp.where(diff < 0, jnp.exp(jnp.maximum(-diff, 0.0) * mu), 0.0)
    d = df + db
    r = idx[None, :, None]
    ones = jnp.ones((1, 1, CHUNK), F32)
    a = jnp.exp((r + 1.0) * lam) * ones
    b = jnp.exp((CHUNK - r) * mu) * ones
    cf = jnp.exp((CHUNK - 1.0 - r) * lam) * ones
    cb = jnp.exp(r * mu) * ones
    full = jnp.ones((1, CHUNK, CHUNK), F32)
    kf = CHUNK * jnp.exp(CHUNK * lam) * full
    kb = CHUNK * jnp.exp(CHUNK * mu) * full
    tabs = jnp.stack([d, jnp.swapaxes(d, 1, 2), diff * df, -diff * db, a, b, cf, cb,
                      (r + 1.0) * a, (CHUNK - r) * b, (CHUNK - 1.0 - r) * cf, r * cb, kf, kb], axis=1)

    def lanes(tab):
        return jnp.transpose(tab, (1, 0, 2)).reshape(CHUNK, RET_HEADS * CHUNK)

    def dec(l):
        return jnp.exp(CHUNK * l)[:, 0, :] * jnp.ones((1, RET_DV), F32)

    weights = dict(a=lanes(a), b=lanes(b), cf=lanes(cf), cb=lanes(cb), dec_f=dec(lam), dec_b=dec(mu))
    return tabs, weights, lg


def _t5_bucket(rel):
    nb = T5_BUCKETS // 2
    max_exact = nb // 2
    ret = jnp.where(rel > 0, nb, 0)
    n = jnp.abs(rel)
    nf = jnp.maximum(n, 1).astype(F32)
    large = max_exact + (jnp.log(nf / max_exact) / math.log(T5_MAX_DIST / max_exact)
                         * (nb - max_exact)).astype(jnp.int32)
    large = jnp.minimum(large, nb - 1)
    return ret + jnp.where(n < max_exact, n, large)


def _swa_rel():
    r = jnp.arange(CHUNK)
    j = jnp.arange(3 * CHUNK)
    return j[None, :] - CHUNK - r[:, None]


def _swa_bias(t5_table):
    rel = _swa_rel()
    bucket = jnp.where(jnp.abs(rel) <= CHUNK, _t5_bucket(rel), -1).astype(jnp.int32)

    def body(tab_ref, bk_ref, o_ref):
        bk = bk_ref[...]
        for h in range(SWA_HEADS):
            pick = lambda b, acc, h=h: jnp.where(bk == b, tab_ref[b, h], acc)
            o_ref[h] = lax.fori_loop(0, T5_BUCKETS, pick, jnp.full(bk.shape, NEG_INF, F32))

    return pl.pallas_call(
        body, name="t5_bias",
        in_specs=[pl.BlockSpec(memory_space=pltpu.SMEM), pl.BlockSpec(memory_space=pltpu.VMEM)],
        out_specs=pl.BlockSpec(memory_space=pltpu.VMEM),
        out_shape=_sds((SWA_HEADS, CHUNK, 3 * CHUNK), F32),
    )(t5_table.astype(F32), bucket)


def _prep_even(proj, cos, sin, q_gain, k_gain, *, tm):
    T = proj.shape[0]
    tm = min(tm, T)

    def body(qa_ref, ka_ref, qb_ref, kb_ref, c_ref, s_ref, qg_ref, kg_ref, qr_ref, kr_ref, qn_ref, kn_ref):
        c = jnp.concatenate([c_ref[...]] * RET_HEADS, axis=1)
        s = jnp.concatenate([s_ref[...]] * RET_HEADS, axis=1)
        qa = qa_ref[...]
        qr_ref[...] = (qa * c + _swap_halves(qa, RET_DK // 2) * s).astype(qr_ref.dtype)
        ka = ka_ref[...]
        kr_ref[...] = ((ka * c + _swap_halves(ka, RET_DK // 2) * s) * (RET_DK ** -0.5)).astype(kr_ref.dtype)
        for src, gain, dst, heads in ((qb_ref, qg_ref, qn_ref, SWA_HEADS), (kb_ref, kg_ref, kn_ref, SWA_KV_HEADS)):
            for h in range(heads):
                sl = slice(h * HEAD_DIM, (h + 1) * HEAD_DIM)
                xh = src[:, sl]
                r = lax.rsqrt(jnp.mean(xh * xh, axis=-1, keepdims=True) + EPS)
                dst[:, sl] = (xh * r * gain[...]).astype(dst.dtype)

    row = lambda i: (i, 0)
    const = lambda i: (0, 0)
    return pl.pallas_call(
        body, name="prep_even", grid=(T // tm,),
        in_specs=[pl.BlockSpec((tm, RET_Q), lambda i: (i, 0)), pl.BlockSpec((tm, RET_Q), lambda i: (i, 1)),
                  pl.BlockSpec((tm, D_MODEL), lambda i: (i, 3)), pl.BlockSpec((tm, 256), lambda i: (i, 16)),
                  pl.BlockSpec((tm, RET_DK), row), pl.BlockSpec((tm, RET_DK), row),
                  pl.BlockSpec((1, HEAD_DIM), const), pl.BlockSpec((1, HEAD_DIM), const)],
        out_specs=[pl.BlockSpec((tm, RET_Q), row), pl.BlockSpec((tm, RET_Q), row),
                   pl.BlockSpec((tm, D_MODEL), row), pl.BlockSpec((tm, 256), row)],
        out_shape=[_sds((T, RET_Q), _MXU), _sds((T, RET_Q), _MXU), _sds((T, D_MODEL), _MXU), _sds((T, 256), _MXU)],
        compiler_params=_params("parallel"),
    )(proj, proj, proj, proj, cos, sin, q_gain, k_gain)


def _ret_scan(name, x, y, y_col, w_asc, dec_asc, w_desc, dec_desc):
    T = x.shape[0]
    nc = T // CHUNK

    def body(xa_ref, ya_ref, xd_ref, yd_ref, wa_ref, da_ref, wd_ref, dd_ref, sa_out, sd_out, sa, sd):
        @pl.when(pl.program_id(0) == 0)
        def _():
            sa[...] = jnp.zeros_like(sa)
            sd[...] = jnp.zeros_like(sd)
        sa_out[0] = sa[...].astype(sa_out.dtype)
        sd_out[0] = sd[...].astype(sd_out.dtype)
        for x_ref, y_ref, w_ref, d_ref, st in ((xa_ref, ya_ref, wa_ref, da_ref, sa), (xd_ref, yd_ref, wd_ref, dd_ref, sd)):
            for h in range(RET_HEADS):
                ks = slice(h * RET_DK, (h + 1) * RET_DK)
                vs = slice(h * RET_DV, (h + 1) * RET_DV)
                u = _dot(x_ref[:, ks].astype(F32) * w_ref[:, ks], y_ref[:, vs], _TN)
                st[ks, :] = st[ks, :] * d_ref[h:h + 1, :] + u

    asc = lambda i: (i, 0)
    desc = lambda i: (nc - 1 - i, 0)
    const = lambda i: (0, 0)
    return pl.pallas_call(
        body, name=name, grid=(nc,),
        in_specs=[pl.BlockSpec((CHUNK, RET_Q), asc), pl.BlockSpec((CHUNK, RET_V), lambda i: (i, y_col)),
                  pl.BlockSpec((CHUNK, RET_Q), desc), pl.BlockSpec((CHUNK, RET_V), lambda i: (nc - 1 - i, y_col)),
                  pl.BlockSpec((CHUNK, RET_Q), const), pl.BlockSpec((RET_HEADS, RET_DV), const),
                  pl.BlockSpec((CHUNK, RET_Q), const), pl.BlockSpec((RET_HEADS, RET_DV), const)],
        out_specs=[pl.BlockSpec((1, RET_Q, RET_DV), lambda i: (i, 0, 0)),
                   pl.BlockSpec((1, RET_Q, RET_DV), lambda i: (nc - 1 - i, 0, 0))],
        out_shape=[_sds((nc, RET_Q, RET_DV), _MXU), _sds((nc, RET_Q, RET_DV), _MXU)],
        scratch_shapes=[pltpu.VMEM((RET_Q, RET_DV), F32), pltpu.VMEM((RET_Q, RET_DV), F32)],
        compiler_params=_params("arbitrary"),
    )(x, y, x, y, w_asc, dec_asc, w_desc, dec_desc)


def _ret_out(qr, kr, proj, sf, sb, tabs, gain):
    T = qr.shape[0]
    nc = T // CHUNK

    def body(q_ref, k_ref, v_ref, g_ref, sf_ref, sb_ref, tab_ref, gain_ref, o_ref, y_ref):
        for h in range(RET_HEADS):
            ks = slice(h * RET_DK, (h + 1) * RET_DK)
            vs = slice(h * RET_DV, (h + 1) * RET_DV)
            q, k, v = q_ref[:, ks], k_ref[:, ks], v_ref[:, vs]
            qf = q.astype(F32)
            a_mat = _dot(q, k, _NT) * tab_ref[h, 0]
            o = (_dot(a_mat, v) + _dot(qf * tab_ref[h, 1], sf_ref[0, ks, :]) + _dot(qf * tab_ref[h, 2], sb_ref[0, ks, :]))
            o_ref[:, vs] = o
            r = lax.rsqrt(jnp.mean(o * o, axis=-1, keepdims=True) + EPS)
            g = g_ref[:, vs]
            y_ref[:, vs] = (g * _sigmoid(g) * (o * r * gain_ref[:, vs])).astype(y_ref.dtype)

    row = lambda i: (i, 0)
    return pl.pallas_call(
        body, name="ret_out", grid=(nc,),
        in_specs=[pl.BlockSpec((CHUNK, RET_Q), row), pl.BlockSpec((CHUNK, RET_Q), row),
                  pl.BlockSpec((CHUNK, RET_V), lambda i: (i, 1)), pl.BlockSpec((CHUNK, RET_V), lambda i: (i, 2)),
                  pl.BlockSpec((1, RET_Q, RET_DV), lambda i: (i, 0, 0)), pl.BlockSpec((1, RET_Q, RET_DV), lambda i: (i, 0, 0)),
                  pl.BlockSpec((RET_HEADS, 3, CHUNK, CHUNK), lambda i: (0, 0, 0, 0)),
                  pl.BlockSpec((1, RET_V), lambda i: (0, 0))],
        out_specs=[pl.BlockSpec((CHUNK, RET_V), row), pl.BlockSpec((CHUNK, RET_V), row)],
        out_shape=[_sds((T, RET_V), F32), _sds((T, RET_V), _MXU)],
        compiler_params=_params("parallel"),
    )(qr, kr, proj, proj, sf, sb, tabs, gain)


def _ret_gate_bwd(dycat, proj, ret_o, gain, *, tm):
    T = ret_o.shape[0]
    tm = min(tm, T)

    def body(dy_ref, g_ref, o_ref, gain_ref, do_ref, dg_ref, dgain_ref):
        @pl.when(pl.program_id(0) == 0)
        def _():
            dgain_ref[...] = jnp.zeros_like(dgain_ref)
        for h in range(RET_HEADS):
            vs = slice(h * RET_DV, (h + 1) * RET_DV)
            o, g, dya, gn = o_ref[:, vs], g_ref[:, vs], dy_ref[:, vs], gain_ref[:, vs]
            r = lax.rsqrt(jnp.mean(o * o, axis=-1, keepdims=True) + EPS)
            ohat = o * r
            sg = _sigmoid(g)
            dy = dya * (g * sg)
            dg_ref[:, vs] = (dya * (ohat * gn) * (sg * (1.0 + g * (1.0 - sg)))).astype(dg_ref.dtype)
            dyg = dy * gn
            do_ref[:, vs] = (r * (dyg - ohat * jnp.mean(dyg * ohat, axis=-1, keepdims=True))).astype(do_ref.dtype)
            dgain_ref[:, vs] += _rowsum8(dy * ohat)

    row = lambda i: (i, 0)
    return pl.pallas_call(
        body, name="ret_gate_bwd", grid=(T // tm,),
        in_specs=[pl.BlockSpec((tm, RET_V), row), pl.BlockSpec((tm, RET_V), lambda i: (i, 2)),
                  pl.BlockSpec((tm, RET_V), row), pl.BlockSpec((1, RET_V), lambda i: (0, 0))],
        out_specs=[pl.BlockSpec((tm, RET_V), row), pl.BlockSpec((tm, RET_V), row), pl.BlockSpec((8, RET_V), lambda i: (0, 0))],
        out_shape=[_sds((T, RET_V), _MXU), _sds((T, RET_V), _MXU), _sds((8, RET_V), F32)],
        compiler_params=_params("arbitrary"),
    )(dycat, proj, ret_o, gain)


def _ret_bwd(qr, kr, proj, g_out, sf, sb, rf, rb, tabs):
    T = qr.shape[0]
    nc = T // CHUNK

    def body(q_ref, k_ref, v_ref, g_ref, sf_ref, sb_ref, rf_ref, rb_ref, tab_ref, dq_ref, dk_ref, dv_ref, dl_ref):
        @pl.when(pl.program_id(0) == 0)
        def _():
            dl_ref[...] = jnp.zeros_like(dl_ref)
        for h in range(RET_HEADS):
            ks = slice(h * RET_DK, (h + 1) * RET_DK)
            vs = slice(h * RET_DV, (h + 1) * RET_DV)
            q, k, v, g = q_ref[:, ks], k_ref[:, ks], v_ref[:, vs], g_ref[:, vs]
            s_f, s_b, r_f, r_b = sf_ref[0, ks, :], sb_ref[0, ks, :], rf_ref[0, ks, :], rb_ref[0, ks, :]
            tab = lambda t: tab_ref[h, t]
            qf, kf = q.astype(F32), k.astype(F32)
            qk = _dot(q, k, _NT)
            da_raw = _dot(g, v, _NT)
            x_f, x_b = _dot(g, s_f, _NT), _dot(g, s_b, _NT)
            dq_ref[:, ks] = _dot(da_raw * tab(TAB_D), k) + tab(TAB_A) * x_f + tab(TAB_B) * x_b
            at = _dot(k, q, _NT) * tab(TAB_DT)
            dat = _dot(v, g, _NT) * tab(TAB_DT)
            y_f, y_b = _dot(v, r_f, _NT), _dot(v, r_b, _NT)
            dk_ref[:, ks] = _dot(dat, q) + tab(TAB_CF) * y_f + tab(TAB_CB) * y_b
            dv_ref[:, vs] = (_dot(at, g) + _dot(kf * tab(TAB_CF), r_f) + _dot(kf * tab(TAB_CB), r_b)).astype(dv_ref.dtype)
            inner = da_raw * qk
            rs_f = r_f.astype(F32) * s_f.astype(F32)
            rs_b = r_b.astype(F32) * s_b.astype(F32)
            dl_f = (inner * tab(TAB_EF) + tab(TAB_RA) * qf * x_f + tab(TAB_RCF) * kf * y_f
                    + tab(TAB_KF) * (rs_f[:, :CHUNK] + rs_f[:, CHUNK:]))
            dl_b = (inner * tab(TAB_EB) + tab(TAB_RB) * qf * x_b + tab(TAB_RCB) * kf * y_b
                    + tab(TAB_KB) * (rs_b[:, :CHUNK] + rs_b[:, CHUNK:]))
            dl_ref[2 * h:2 * h + 1, :] += jnp.sum(dl_f, axis=0, keepdims=True)
            dl_ref[2 * h + 1:2 * h + 2, :] += jnp.sum(dl_b, axis=0, keepdims=True)

    row = lambda i: (i, 0)
    st = lambda i: (i, 0, 0)
    return pl.pallas_call(
        body, name="ret_bwd", grid=(nc,),
        in_specs=[pl.BlockSpec((CHUNK, RET_Q), row), pl.BlockSpec((CHUNK, RET_Q), row),
                  pl.BlockSpec((CHUNK, RET_V), lambda i: (i, 1)), pl.BlockSpec((CHUNK, RET_V), row),
                  pl.BlockSpec((1, RET_Q, RET_DV), st), pl.BlockSpec((1, RET_Q, RET_DV), st),
                  pl.BlockSpec((1, RET_Q, RET_DV), st), pl.BlockSpec((1, RET_Q, RET_DV), st),
                  pl.BlockSpec((RET_HEADS, 14, CHUNK, CHUNK), lambda i: (0, 0, 0, 0))],
        out_specs=[pl.BlockSpec((CHUNK, RET_Q), row), pl.BlockSpec((CHUNK, RET_Q), row),
                   pl.BlockSpec((CHUNK, RET_V), row), pl.BlockSpec((8, CHUNK), lambda i: (0, 0))],
        out_shape=[_sds((T, RET_Q), F32), _sds((T, RET_Q), F32), _sds((T, RET_V), _MXU), _sds((8, CHUNK), F32)],
        compiler_params=_params("arbitrary"),
    )(qr, kr, proj, g_out, sf, sb, rf, rb, tabs)


def _swa_probs(q, k_win, bias, sink, valid):
    s = _dot(q, k_win, _NT) * ATT_SCALE + bias
    s = jnp.where(valid, s, NEG_INF)
    m = jnp.maximum(jnp.max(s, axis=-1, keepdims=True), sink)
    p = jnp.exp(s - m)
    e_sink = jnp.exp(sink - m)
    inv = 1.0 / (jnp.sum(p, axis=-1, keepdims=True) + e_sink)
    return p * inv, e_sink * inv


def _swa_valid(i, nb):
    col = lax.broadcasted_iota(jnp.int32, (1, 3 * CHUNK), 1)
    return jnp.logical_and(jnp.logical_or(col >= CHUNK, i > 0), jnp.logical_or(col < 2 * CHUNK, i < nb - 1))


def _swa_window_specs(nb, width, col_block, clamp):
    prev = lambda i: (jnp.maximum(clamp(i) - 1, 0), col_block)
    cur = lambda i: (clamp(i), col_block)
    nxt = lambda i: (jnp.minimum(clamp(i) + 1, nb - 1), col_block)
    return [pl.BlockSpec((CHUNK, width), f) for f in (prev, cur, nxt)]


def _swa_fwd(qn, kn, proj, bias, sink):
    T = qn.shape[0]
    nb = T // CHUNK
    kvw = SWA_KV_HEADS * HEAD_DIM
    group = SWA_HEADS // SWA_KV_HEADS

    def body(q_ref, k0, k1, k2, v0, v1, v2, bias_ref, sink_ref, y_ref):
        i = pl.program_id(0)
        valid = _swa_valid(i, nb)
        for g in range(SWA_KV_HEADS):
            gs = slice(g * HEAD_DIM, (g + 1) * HEAD_DIM)
            k_win = jnp.concatenate([k0[:, gs], k1[:, gs], k2[:, gs]], axis=0)
            v_win = jnp.concatenate([v0[:, gs], v1[:, gs], v2[:, gs]], axis=0).astype(_MXU)
            for hh in range(group):
                h = g * group + hh
                hs = slice(h * HEAD_DIM, (h + 1) * HEAD_DIM)
                p, _ = _swa_probs(q_ref[:, hs], k_win, bias_ref[h], sink_ref[h:h + 1, 0:1], valid)
                y_ref[:, hs] = _dot(p, v_win).astype(y_ref.dtype)

    ident = lambda i: i
    return pl.pallas_call(
        body, name="swa_fwd", grid=(nb,),
        in_specs=[pl.BlockSpec((CHUNK, D_MODEL), lambda i: (i, 0))]
        + _swa_window_specs(nb, kvw, 0, ident) + _swa_window_specs(nb, kvw, 17, ident)
        + [pl.BlockSpec((SWA_HEADS, CHUNK, 3 * CHUNK), lambda i: (0, 0, 0)), pl.BlockSpec((SWA_HEADS, HEAD_DIM), lambda i: (0, 0))],
        out_specs=pl.BlockSpec((CHUNK, D_MODEL), lambda i: (i, 0)),
        out_shape=_sds((T, D_MODEL), _MXU),
        compiler_params=_params("parallel"),
    )(qn, kn, kn, kn, proj, proj, proj, bias, sink)


def _swa_bwd(qn, kn, proj, dycat, bias, sink):
    T = qn.shape[0]
    nb = T // CHUNK
    kvw = SWA_KV_HEADS * HEAD_DIM
    group = SWA_HEADS // SWA_KV_HEADS

    def body(q_ref, k0, k1, k2, v0, v1, v2, dy_ref, bias_ref, sink_ref,
             dq_ref, dk_ref, dv_ref, dbias_ref, dsink_ref, acc_a, acc_b):
        i = pl.program_id(0)

        @pl.when(i == 0)
        def _():
            dbias_ref[...] = jnp.zeros_like(dbias_ref)
            dsink_ref[...] = jnp.zeros_like(dsink_ref)
            acc_a[...] = jnp.zeros_like(acc_a)
            acc_b[...] = jnp.zeros_like(acc_b)

        @pl.when(i < nb)
        def _():
            valid = _swa_valid(i, nb)
            for g in range(SWA_KV_HEADS):
                gs = slice(g * HEAD_DIM, (g + 1) * HEAD_DIM)
                k_win = jnp.concatenate([k0[:, gs], k1[:, gs], k2[:, gs]], axis=0)
                v_win = jnp.concatenate([v0[:, gs], v1[:, gs], v2[:, gs]], axis=0).astype(_MXU)
                dk_win = jnp.zeros((3 * CHUNK, HEAD_DIM), F32)
                dv_win = jnp.zeros((3 * CHUNK, HEAD_DIM), F32)
                for hh in range(group):
                    h = g * group + hh
                    hs = slice(h * HEAD_DIM, (h + 1) * HEAD_DIM)
                    q, dy = q_ref[:, hs], dy_ref[:, hs]
                    p, p_sink = _swa_probs(q, k_win, bias_ref[h], sink_ref[h:h + 1, 0:1], valid)
                    dp = _dot(dy, v_win, _NT)
                    delta = jnp.sum(p * dp, axis=-1, keepdims=True)
                    ds = p * (dp - delta)
                    dbias_ref[h] += ds
                    dsink_ref[h:h + 1, :] += jnp.sum(-p_sink * delta, axis=0, keepdims=True) * jnp.ones((1, HEAD_DIM), F32)
                    dq_ref[:, hs] = _dot(ds, k_win) * ATT_SCALE
                    dk_win = dk_win + _dot(ds, q, _TN) * ATT_SCALE
                    dv_win = dv_win + _dot(p, dy, _TN)
                for win, out_ref, col0 in ((dk_win, dk_ref, 0), (dv_win, dv_ref, kvw)):
                    cs = slice(col0 + g * HEAD_DIM, col0 + (g + 1) * HEAD_DIM)
                    out_ref[:, gs] = acc_a[:, cs] + win[:CHUNK]
                    acc_a[:, cs] = acc_b[:, cs] + win[CHUNK:2 * CHUNK]
                    acc_b[:, cs] = win[2 * CHUNK:]

        @pl.when(i == nb)
        def _():
            dk_ref[...] = acc_a[:, :kvw]
            dv_ref[...] = acc_a[:, kvw:]

    clamp = lambda i: jnp.minimum(i, nb - 1)
    late = lambda i: (jnp.maximum(i - 1, 0), 0)
    return pl.pallas_call(
        body, name="swa_bwd", grid=(nb + 1,),
        in_specs=[pl.BlockSpec((CHUNK, D_MODEL), lambda i: (clamp(i), 0))]
        + _swa_window_specs(nb, kvw, 0, clamp) + _swa_window_specs(nb, kvw, 17, clamp)
        + [pl.BlockSpec((CHUNK, D_MODEL), lambda i: (clamp(i), 1)),
           pl.BlockSpec((SWA_HEADS, CHUNK, 3 * CHUNK), lambda i: (0, 0, 0)), pl.BlockSpec((SWA_HEADS, HEAD_DIM), lambda i: (0, 0))],
        out_specs=[pl.BlockSpec((CHUNK, D_MODEL), lambda i: (clamp(i), 0)),
                   pl.BlockSpec((CHUNK, kvw), late), pl.BlockSpec((CHUNK, kvw), late),
                   pl.BlockSpec((SWA_HEADS, CHUNK, 3 * CHUNK), lambda i: (0, 0, 0)), pl.BlockSpec((SWA_HEADS, HEAD_DIM), lambda i: (0, 0))],
        out_shape=[_sds((T, D_MODEL), F32), _sds((T, kvw), F32), _sds((T, kvw), F32),
                   _sds((SWA_HEADS, CHUNK, 3 * CHUNK), F32), _sds((SWA_HEADS, HEAD_DIM), F32)],
        scratch_shapes=[pltpu.VMEM((CHUNK, 2 * kvw), F32), pltpu.VMEM((CHUNK, 2 * kvw), F32)],
        compiler_params=_params("arbitrary"),
    )(qn, kn, kn, kn, proj, proj, proj, dycat, bias, sink)


def _t5_bucket_reduce(dbias, bucket):
    def body(db_ref, bk_ref, o_ref):
        bk = bk_ref[...]
        row = lax.broadcasted_iota(jnp.int32, (SWA_HEADS, HEAD_DIM), 0)
        lane = lax.broadcasted_iota(jnp.int32, (SWA_HEADS, HEAD_DIM), 1)

        def per_bucket(b, acc):
            mask = bk == b
            for h in range(SWA_HEADS):
                tot = jnp.sum(jnp.sum(jnp.where(mask, db_ref[h], 0.0), axis=0, keepdims=True), axis=1, keepdims=True)
                acc = acc + jnp.where(jnp.logical_and(row == h, lane == b), tot, 0.0)
            return acc

        o_ref[...] = lax.fori_loop(0, T5_BUCKETS, per_bucket, jnp.zeros((SWA_HEADS, HEAD_DIM), F32))

    return pl.pallas_call(body, name="t5_bucket_reduce", out_shape=_sds((SWA_HEADS, HEAD_DIM), F32),
                          compiler_params=pltpu.CompilerParams(vmem_limit_bytes=VMEM_LIMIT_BYTES))(dbias, bucket)


def _headnorm_bwd(x, dy, gain):
    r = lax.rsqrt(jnp.mean(x * x, axis=-1, keepdims=True) + EPS)
    xhat = x * r
    dyg = dy * gain
    return r * (dyg - xhat * jnp.mean(dyg * xhat, axis=-1, keepdims=True)), dy * xhat


def _post_even(proj, dqr, dkr, dva, dga, dqn, dkn, dvb, cos, sin, q_gain, k_gain, *, tm):
    T = proj.shape[0]
    tm = min(tm, T)
    kvw = SWA_KV_HEADS * HEAD_DIM

    def body(qb_ref, kb_ref, dqr_ref, dkr_ref, dva_ref, dga_ref, dqn_ref, dkn_ref, dvb_ref, c_ref, s_ref, qg_ref, kg_ref,
             dp_ref, dqg_ref, dkg_ref):
        @pl.when(pl.program_id(0) == 0)
        def _():
            dqg_ref[...] = jnp.zeros_like(dqg_ref)
            dkg_ref[...] = jnp.zeros_like(dkg_ref)
        c = jnp.concatenate([c_ref[...]] * RET_HEADS, axis=1)
        s = jnp.concatenate([s_ref[...]] * RET_HEADS, axis=1)
        dq = dqr_ref[...]
        dp_ref[:, 0:RET_Q] = (dq * c + _swap_halves(dq * s, RET_DK // 2)).astype(dp_ref.dtype)
        dk = dkr_ref[...] * (RET_DK ** -0.5)
        dp_ref[:, RET_Q:2 * RET_Q] = (dk * c + _swap_halves(dk * s, RET_DK // 2)).astype(dp_ref.dtype)
        off = 2 * RET_Q
        dp_ref[:, off:off + RET_V] = dva_ref[...].astype(dp_ref.dtype)
        dp_ref[:, off + RET_V:off + 2 * RET_V] = dga_ref[...].astype(dp_ref.dtype)
        off += 2 * RET_V
        for src, dsrc, gain, dgain, heads, base in ((qb_ref, dqn_ref, qg_ref, dqg_ref, SWA_HEADS, off),
                                                    (kb_ref, dkn_ref, kg_ref, dkg_ref, SWA_KV_HEADS, off + D_MODEL)):
            for h in range(heads):
                sl = slice(h * HEAD_DIM, (h + 1) * HEAD_DIM)
                dx, dgx = _headnorm_bwd(src[:, sl], dsrc[:, sl], gain[...])
                dp_ref[:, base + h * HEAD_DIM:base + (h + 1) * HEAD_DIM] = dx.astype(dp_ref.dtype)
                dgain[...] += _rowsum8(dgx)
        dp_ref[:, off + D_MODEL + kvw:] = dvb_ref[...].astype(dp_ref.dtype)

    row = lambda i: (i, 0)
    const = lambda i: (0, 0)
    return pl.pallas_call(
        body, name="post_even", grid=(T // tm,),
        in_specs=[pl.BlockSpec((tm, D_MODEL), lambda i: (i, 3)), pl.BlockSpec((tm, kvw), lambda i: (i, 16)),
                  pl.BlockSpec((tm, RET_Q), row), pl.BlockSpec((tm, RET_Q), row),
                  pl.BlockSpec((tm, RET_V), row), pl.BlockSpec((tm, RET_V), row),
                  pl.BlockSpec((tm, D_MODEL), row), pl.BlockSpec((tm, kvw), row), pl.BlockSpec((tm, kvw), row),
                  pl.BlockSpec((tm, RET_DK), row), pl.BlockSpec((tm, RET_DK), row),
                  pl.BlockSpec((1, HEAD_DIM), const), pl.BlockSpec((1, HEAD_DIM), const)],
        out_specs=[pl.BlockSpec((tm, EVEN_IN), row), pl.BlockSpec((8, HEAD_DIM), const), pl.BlockSpec((8, HEAD_DIM), const)],
        out_shape=[_sds((T, EVEN_IN), _MXU), _sds((8, HEAD_DIM), F32), _sds((8, HEAD_DIM), F32)],
        compiler_params=_params("arbitrary"),
    )(proj, proj, dqr, dkr, dva, dga, dqn, dkn, dvb, cos, sin, q_gain, k_gain)


def _prep_odd(proj, cos, sin, q_gain, k_gain, *, tm):
    T = proj.shape[0]
    tm = min(tm, T)
    kvw = AX_KV_HEADS * HEAD_DIM

    def body(q_ref, k_ref, v_ref, c_ref, s_ref, qg_ref, kg_ref, qx_ref, kx_ref, vx_ref):
        c, s = c_ref[...], s_ref[...]
        for src, gain, dst, heads in ((q_ref, qg_ref, qx_ref, AX_HEADS), (k_ref, kg_ref, kx_ref, AX_KV_HEADS)):
            for h in range(heads):
                sl = slice(h * HEAD_DIM, (h + 1) * HEAD_DIM)
                xh = src[:, sl]
                r = lax.rsqrt(jnp.mean(xh * xh, axis=-1, keepdims=True) + EPS)
                xn = xh * r * gain[...]
                dst[:, sl] = (xn * c + _swap_halves(xn, HEAD_DIM // 4) * s).astype(dst.dtype)
        vx_ref[...] = v_ref[...].astype(vx_ref.dtype)

    row = lambda i: (i, 0)
    const = lambda i: (0, 0)
    return pl.pallas_call(
        body, name="prep_odd", grid=(T // tm,),
        in_specs=[pl.BlockSpec((tm, D_MODEL), row), pl.BlockSpec((tm, kvw), lambda i: (i, 4)), pl.BlockSpec((tm, kvw), lambda i: (i, 5)),
                  pl.BlockSpec((tm, HEAD_DIM), row), pl.BlockSpec((tm, HEAD_DIM), row),
                  pl.BlockSpec((1, HEAD_DIM), const), pl.BlockSpec((1, HEAD_DIM), const)],
        out_specs=[pl.BlockSpec((tm, D_MODEL), row), pl.BlockSpec((tm, kvw), row), pl.BlockSpec((tm, kvw), row)],
        out_shape=[_sds((T, D_MODEL), _MXU), _sds((T, kvw), _MXU), _sds((T, kvw), _MXU)],
        compiler_params=_params("parallel"),
    )(proj, proj, proj, cos, sin, q_gain, k_gain)


def _post_odd(proj, dqxt, dkx, dvx, cos, sin, q_gain, k_gain, *, tm):
    T = proj.shape[0]
    tm = min(tm, T)
    kvw = AX_KV_HEADS * HEAD_DIM

    def body(q_ref, k_ref, dqt_ref, dk_ref, dv_ref, c_ref, s_ref, qg_ref, kg_ref, dp_ref, dqg_ref, dkg_ref):
        @pl.when(pl.program_id(0) == 0)
        def _():
            dqg_ref[...] = jnp.zeros_like(dqg_ref)
            dkg_ref[...] = jnp.zeros_like(dkg_ref)
        c, s = c_ref[...], s_ref[...]
        for src, dsrc, gain, dgain, heads, base in ((q_ref, dqt_ref, qg_ref, dqg_ref, AX_HEADS, 0),
                                                    (k_ref, dk_ref, kg_ref, dkg_ref, AX_KV_HEADS, D_MODEL)):
            for h in range(heads):
                sl = slice(h * HEAD_DIM, (h + 1) * HEAD_DIM)
                d = dsrc[sl, :].T if dsrc is dqt_ref else dsrc[:, sl]
                dn = d * c + _swap_halves(d * s, HEAD_DIM // 4)
                dx, dgx = _headnorm_bwd(src[:, sl], dn, gain[...])
                dp_ref[:, base + h * HEAD_DIM:base + (h + 1) * HEAD_DIM] = dx.astype(dp_ref.dtype)
                dgain[...] += _rowsum8(dgx)
        dp_ref[:, D_MODEL + kvw:] = dv_ref[...].astype(dp_ref.dtype)

    row = lambda i: (i, 0)
    const = lambda i: (0, 0)
    return pl.pallas_call(
        body, name="post_odd", grid=(T // tm,),
        in_specs=[pl.BlockSpec((tm, D_MODEL), row), pl.BlockSpec((tm, kvw), lambda i: (i, 4)),
                  pl.BlockSpec((D_MODEL, tm), lambda i: (0, i)), pl.BlockSpec((tm, kvw), row), pl.BlockSpec((tm, kvw), row),
                  pl.BlockSpec((tm, HEAD_DIM), row), pl.BlockSpec((tm, HEAD_DIM), row),
                  pl.BlockSpec((1, HEAD_DIM), const), pl.BlockSpec((1, HEAD_DIM), const)],
        out_specs=[pl.BlockSpec((tm, ODD_IN), row), pl.BlockSpec((8, HEAD_DIM), const), pl.BlockSpec((8, HEAD_DIM), const)],
        out_shape=[_sds((T, ODD_IN), _MXU), _sds((8, HEAD_DIM), F32), _sds((8, HEAD_DIM), F32)],
        compiler_params=_params("arbitrary"),
    )(proj, proj, dqxt, dkx, dvx, cos, sin, q_gain, k_gain)


SCORE_SCALE_LOG2 = ATT_SCALE * math.log2(math.e)


def _flash_fwd(qx, kx, vx, *, tq, tk):
    T = qx.shape[0]
    tq, tk = min(tq, T), min(tk, T)
    nq, nk = T // tq, T // tk
    group = AX_HEADS // AX_KV_HEADS

    def body(k_ref, v_ref, q_ref, o_ref, lse_ref, acc_sc, m_sc, l_sc):
        j = pl.program_id(2)

        @pl.when(j == 0)
        def _():
            m_sc[...] = jnp.full(m_sc.shape, NEG_INF, F32)
            l_sc[...] = jnp.zeros_like(l_sc)
            acc_sc[...] = jnp.zeros_like(acc_sc)
        k, v = k_ref[...], v_ref[...]

        def step(i, carry):
            cols = pl.ds(pl.multiple_of(i * tq, tq), tq)
            st = _dot(k, q_ref[cols, :], _NT) * SCORE_SCALE_LOG2
            m_old = m_sc[i]
            m_new = jnp.maximum(m_old, jnp.max(st, axis=0, keepdims=True))
            p = jnp.exp2(st - m_new)
            alpha = jnp.exp2(m_old - m_new)
            m_sc[i] = m_new
            l_sc[i] = alpha * l_sc[i] + jnp.sum(p, axis=0, keepdims=True)
            acc_sc[:, cols] = alpha * acc_sc[:, cols] + _dot(v, p, _TN)
            return carry

        lax.fori_loop(0, nq, step, 0)

        @pl.when(j == nk - 1)
        def _():
            def finish(i, carry):
                cols = pl.ds(pl.multiple_of(i * tq, tq), tq)
                o_ref[cols, :] = (acc_sc[:, cols] / l_sc[i]).T.astype(o_ref.dtype)
                lse_ref[0, i] = m_sc[i] + jnp.log2(l_sc[i])
                return carry

            lax.fori_loop(0, nq, finish, 0)

    kv = lambda g, h, j: (j, g)
    qh = lambda g, h, j: (0, g * group + h)
    o, lse = pl.pallas_call(
        body, name="flash_fwd", grid=(AX_KV_HEADS, group, nk),
        in_specs=[pl.BlockSpec((tk, HEAD_DIM), kv), pl.BlockSpec((tk, HEAD_DIM), kv), pl.BlockSpec((T, HEAD_DIM), qh)],
        out_specs=[pl.BlockSpec((T, HEAD_DIM), qh), pl.BlockSpec((1, nq, 1, tq), lambda g, h, j: (g * group + h, 0, 0, 0))],
        out_shape=[_sds((T, D_MODEL), _MXU), _sds((AX_HEADS, nq, 1, tq), F32)],
        scratch_shapes=[pltpu.VMEM((HEAD_DIM, T), F32), pltpu.VMEM((nq, 1, tq), F32), pltpu.VMEM((nq, 1, tq), F32)],
        compiler_params=_params("parallel", "arbitrary", "arbitrary"),
    )(kx, vx, qx)
    return o, lse.reshape(AX_HEADS, 1, T)


def _flash_delta(o, do, *, tq):
    T = o.shape[0]
    tq = min(tq, T)

    def body(o_ref, do_ref, delta_ref):
        delta_ref[0] = jnp.sum(do_ref[...].astype(F32) * o_ref[...].astype(F32), axis=-1, keepdims=True)

    blk = lambda h, i: (i, h)
    return pl.pallas_call(
        body, name="flash_delta", grid=(AX_HEADS, T // tq),
        in_specs=[pl.BlockSpec((tq, HEAD_DIM), blk), pl.BlockSpec((tq, HEAD_DIM), blk)],
        out_specs=pl.BlockSpec((1, tq, 1), lambda h, i: (h, i, 0)),
        out_shape=_sds((AX_HEADS, T, 1), F32),
        compiler_params=_params("parallel", "parallel"),
    )(o, do)


def _flash_bwd(qx, kx, vx, do, lse, delta, *, tq, tk):
    T = qx.shape[0]
    tq, tk = min(tq, T), min(tk, T)
    nq = T // tq
    group = AX_HEADS // AX_KV_HEADS
    lse_rows = lse.reshape(AX_HEADS, nq, 1, tq)
    delta_rows = delta.reshape(AX_HEADS, nq, 1, tq)

    def body(k_ref, v_ref, q_ref, do_ref, lse_ref, delta_ref, dqt_ref, dk_ref, dv_ref):
        j = pl.program_id(2)

        @pl.when(jnp.logical_and(pl.program_id(1) == 0, j == 0))
        def _():
            dk_ref[...] = jnp.zeros_like(dk_ref)
            dv_ref[...] = jnp.zeros_like(dv_ref)

        @pl.when(j == 0)
        def _():
            dqt_ref[...] = jnp.zeros_like(dqt_ref)
        k, v = k_ref[...], v_ref[...]

        def step(i, carry):
            dk, dv = carry
            off = pl.multiple_of(i * tq, tq)
            q, do_blk = q_ref[pl.ds(off, tq), :], do_ref[pl.ds(off, tq), :]
            pt = jnp.exp2(_dot(k, q, _NT) * SCORE_SCALE_LOG2 - lse_ref[0, i])
            dst = pt * (_dot(v, do_blk, _NT) - delta_ref[0, i])
            dqt_ref[:, pl.ds(off, tq)] += _dot(k, dst, _TN) * ATT_SCALE
            return dk + _dot(dst, q), dv + _dot(pt, do_blk)

        zero = jnp.zeros((tk, HEAD_DIM), F32)
        dk, dv = lax.fori_loop(0, nq, step, (zero, zero))
        rows = pl.ds(pl.multiple_of(j * tk, tk), tk)
        dk_ref[rows, :] += dk * ATT_SCALE
        dv_ref[rows, :] += dv

    kv = lambda g, h, j: (j, g)
    qh = lambda g, h, j: (0, g * group + h)
    st = lambda g, h, j: (g * group + h, 0, 0, 0)
    acc = lambda g, h, j: (0, g)
    return pl.pallas_call(
        body, name="flash_bwd", grid=(AX_KV_HEADS, group, T // tk),
        in_specs=[pl.BlockSpec((tk, HEAD_DIM), kv), pl.BlockSpec((tk, HEAD_DIM), kv),
                  pl.BlockSpec((T, HEAD_DIM), qh), pl.BlockSpec((T, HEAD_DIM), qh),
                  pl.BlockSpec((1, nq, 1, tq), st), pl.BlockSpec((1, nq, 1, tq), st)],
        out_specs=[pl.BlockSpec((HEAD_DIM, T), lambda g, h, j: (g * group + h, 0)),
                   pl.BlockSpec((T, HEAD_DIM), acc), pl.BlockSpec((T, HEAD_DIM), acc)],
        out_shape=[_sds((D_MODEL, T), F32), _sds((T, AX_KV_HEADS * HEAD_DIM), F32), _sds((T, AX_KV_HEADS * HEAD_DIM), F32)],
        compiler_params=_params("parallel", "arbitrary", "arbitrary"),
    )(kx, vx, qx, do, lse_rows, delta_rows)


TM = 1024
TM_WIDE = 512


def _mlp_fwd(tag, x, gain, w_up, w_down, target=None):
    u, h = _norm_matmul(f"mlp_up{tag}", x, gain, w_up, tm=TM, tn=1024, out_dtype=F32)
    out = _matmul_res(f"mlp_down{tag}", [u], w_down, x, tm=TM_WIDE, relu2=True, target=target)
    return out, (x, u, h)


def _local_step(x, target, p, w_first, fetch_rest, push, tokens=()):
    T = x.shape[0]
    cos_r, sin_r = _ret_rope_tables(T)
    cos_a, sin_a = _axial_rope_tables(T)
    tabs, rw, log_gamma = _retention_tables(p["ret_decay_logit"][0])
    bias = _swa_bias(p["t5_table"])
    sink = p["swa_sink"][0][:, None] * jnp.ones((1, HEAD_DIM), F32)
    nm, nl = p["norm_mix"], p["norm_mlp"]
    pending = [t for t in tokens if t is not None]

    def send(tag, weight, dw):
        token = push(tag, weight, dw[None])
        if token is not None:
            pending.append(token)

    def tied(operand):
        while pending:
            operand = operand + pending.pop()[0:1, 0:1]
        return operand

    def mlp_bwd(tag, saved, gain, w_up, w_down, dy):
        xs, u, h = saved
        du = _matmul_nt(f"mlp_down{tag}_bwd", dy, w_down, tm=TM, tn=1024, out_dtype=_MXU, relu_of=u)
        send(f"mlp_down{tag}", "w_mlp_down", _matmul_tn(f"mlp_down{tag}_dw", u, dy, tk=1024, tn=1024, tt=1024, out_dtype=_WIRE, relu2=True))
        dx, dgain = _matmul_nt_normbwd(f"mlp_up{tag}_bwd", du, w_up, xs, tied(gain), dy, tm=TM_WIDE)
        send(f"mlp_up{tag}", "w_mlp_up", _matmul_tn(f"mlp_up{tag}_dw", h, du, tk=1024, tn=1024, tt=1024, out_dtype=_WIRE))
        return dx, dgain

    proj0, h0 = _norm_matmul("in_even", x, tied(nm[0:1]), w_first["w_in_even"], tm=TM, tn=1152, out_dtype=F32)
    qr, kr, qn, kn = _prep_even(proj0, cos_r, sin_r, p["swa_q_norm"], p["swa_k_norm"], tm=TM)
    sf, sb = _ret_scan("ret_scan_fwd", kr, proj0, 1, rw["cf"], rw["dec_f"], rw["cb"], rw["dec_b"])
    ret_o, ya = _ret_out(qr, kr, proj0, sf, sb, tabs[:, (TAB_D, TAB_A, TAB_B)], p["ret_norm"])
    yb = _swa_fwd(qn, kn, proj0, bias, sink)
    wf = {**w_first, **fetch_rest(yb)}
    x1 = _matmul_res("out_even", [ya, yb], wf["w_out_even"], x, tm=TM)
    x2, mlp0 = _mlp_fwd(0, x1, nl[0:1], wf["w_mlp_up"][0], wf["w_mlp_down"][0])
    proj1, h1 = _norm_matmul("in_odd", x2, nm[1:2], wf["w_in_odd"], tm=TM, tn=768, out_dtype=F32)
    qx, kx, vx = _prep_odd(proj1, cos_a, sin_a, p["ax_q_norm"], p["ax_k_norm"], tm=TM)
    o, lse = _flash_fwd(qx, kx, vx, tq=1024, tk=512)
    x3 = _matmul_res("out_odd", [o], wf["w_out_odd"], x2, tm=TM)
    (g4, loss_part), mlp1 = _mlp_fwd(1, x3, nl[1:2], wf["w_mlp_up"][1], wf["w_mlp_down"][1], target=target)

    dx3, dnl1 = mlp_bwd(1, mlp1, nl[1:2], wf["w_mlp_up"][1], wf["w_mlp_down"][1], g4)
    do = _matmul_nt("out_odd_bwd", dx3, wf["w_out_odd"], tm=TM, tn=1024, out_dtype=_MXU)
    send("out_odd", "w_out_odd", _matmul_tn("out_odd_dw", o, dx3, tk=1024, tn=1024, tt=1024, out_dtype=_WIRE))
    delta = _flash_delta(o, do, tq=1024)
    dqxt, dkx, dvx = _flash_bwd(qx, kx, vx, do, lse, delta, tq=1024, tk=512)
    dproj1, dqg1, dkg1 = _post_odd(proj1, dqxt, dkx, dvx, cos_a, sin_a, tied(p["ax_q_norm"]), p["ax_k_norm"], tm=TM)
    send("in_odd", "w_in_odd", _matmul_tn("in_odd_dw", h1, dproj1, tk=1024, tn=768, tt=1024, out_dtype=_WIRE))
    dx2, dnm1 = _matmul_nt_normbwd("in_odd_bwd", dproj1, wf["w_in_odd"], x2, tied(nm[1:2]), dx3, tm=TM_WIDE)
    dx1, dnl0 = mlp_bwd(0, mlp0, nl[0:1], wf["w_mlp_up"][0], wf["w_mlp_down"][0], dx2)
    dycat = _matmul_nt("out_even_bwd", dx1, wf["w_out_even"], tm=TM, tn=1024, out_dtype=F32)
    send("out_even", "w_out_even", jnp.concatenate([
        _matmul_tn("out_even_dw_ret", ya, dx1, tk=1024, tn=1024, tt=1024, out_dtype=_WIRE),
        _matmul_tn("out_even_dw_swa", yb, dx1, tk=1024, tn=1024, tt=1024, out_dtype=_WIRE)], axis=0))
    g_out, dga, dretg = _ret_gate_bwd(dycat, proj0, ret_o, tied(p["ret_norm"]), tm=TM)
    rb, rf = _ret_scan("ret_scan_bwd", qr, g_out, 0, rw["b"], rw["dec_b"], rw["a"], rw["dec_f"])
    dqr, dkr, dva, dlog = _ret_bwd(qr, kr, proj0, g_out, sf, sb, rf, rb, tabs)
    dqn, dkn, dvb, dbias, dsink = _swa_bwd(qn, kn, proj0, dycat, bias, sink)
    dt5 = _t5_bucket_reduce(dbias, _t5_bucket(_swa_rel()).astype(jnp.int32))
    dproj0, dqg0, dkg0 = _post_even(proj0, dqr, dkr, dva, dga, dqn, dkn, dvb, cos_r, sin_r,
                                    p["swa_q_norm"], p["swa_k_norm"], tm=TM_WIDE)
    send("in_even", "w_in_even", _matmul_tn("in_even_dw", h0, dproj0, tk=1024, tn=1152, tt=1024, out_dtype=_WIRE))
    dx0, dnm0 = _matmul_nt_normbwd("in_even_bwd", dproj0, w_first["w_in_even"], x, tied(nm[0:1]), dx1, tm=TM_WIDE)

    fold = lambda part: jnp.sum(part, axis=0)
    dlam = jnp.sum(dlog, axis=1).reshape(RET_HEADS, 2).T
    small = {
        "norm_mix": jnp.stack([fold(dnm0), fold(dnm1)]),
        "norm_mlp": jnp.stack([fold(dnl0), fold(dnl1)]),
        "ret_decay_logit": (dlam * (1.0 - jnp.exp(log_gamma)))[None],
        "ret_norm": fold(dretg)[None],
        "swa_q_norm": fold(dqg0)[None], "swa_k_norm": fold(dkg0)[None],
        "swa_sink": dsink[:, 0][None],
        "t5_table": dt5[:, :T5_BUCKETS].T,
        "ax_q_norm": fold(dqg1)[None], "ax_k_norm": fold(dkg1)[None],
    }
    return loss_part, dx0, small


BIG = ("w_in_even", "w_out_even", "w_in_odd", "w_out_odd", "w_mlp_up", "w_mlp_down")
SMALL = ("norm_mix", "norm_mlp", "ret_decay_logit", "ret_norm", "swa_q_norm", "swa_k_norm", "swa_sink", "t5_table",
         "ax_q_norm", "ax_k_norm")
WEIGHTS = ("norm_mix", "norm_mlp", "w_in_even", "w_out_even", "ret_decay_logit", "ret_norm", "swa_q_norm", "swa_k_norm",
           "swa_sink", "t5_table", "w_in_odd", "w_out_odd", "ax_q_norm", "ax_k_norm", "w_mlp_up", "w_mlp_down")
SHARD_AXIS = {"w_in_even": 2, "w_out_even": 1, "w_in_odd": 2, "w_out_odd": 1, "w_mlp_up": 2, "w_mlp_down": 1}
N_CHIPS = 4
ANY = pl.BlockSpec(memory_space=pl.ANY)
HBM = pl.BlockSpec(memory_space=pltpu.HBM)
SEM = pl.BlockSpec(memory_space=pltpu.SEMAPHORE)
SPLIT_COPY = pltpu.CompilerParams(has_side_effects=pltpu.SideEffectType.DATAFLOW_SIDE_EFFECTING)


def _in_hbm(a):
    return pltpu.with_memory_space_constraint(a, pltpu.HBM)


def _mesh_pos():
    return lax.axis_index("x"), lax.axis_index("y"), lax.axis_index("c")


def _window(ref, axis, start, size):
    idx = [slice(None)] * len(ref.shape)
    idx[axis] = pl.ds(start, size)
    return ref.at[tuple(idx)]


def _cast_place(key, shard, chip, *, tr=256):
    L, R, C = shard.shape
    tr = min(tr, R)
    axis = SHARD_AXIS[key]
    whole = tuple(d * (N_CHIPS if a == axis else 1) for a, d in enumerate(shard.shape))

    def body(chip_ref, s_ref, o_ref):
        o_ref[...] = s_ref[...].astype(o_ref.dtype)

    if axis == 2:
        out_map = lambda l, i, chip_ref: (l, i, chip_ref[0])
    else:
        out_map = lambda l, i, chip_ref: (l, i + chip_ref[0] * (R // tr), 0)
    grid_spec = pltpu.PrefetchScalarGridSpec(
        num_scalar_prefetch=1, grid=(L, R // tr),
        in_specs=[pl.BlockSpec((1, tr, C), lambda l, i, chip_ref: (l, i, 0))],
        out_specs=pl.BlockSpec((1, tr, C), out_map))
    return pl.pallas_call(body, name=f"cast_place_{key}", grid_spec=grid_spec, out_shape=_sds(whole, _MXU),
                          compiler_params=_params("parallel", "parallel"))(chip, shard)


def _gather_copies(names, refs, send_sems, recv_sems):
    x, y, c = _mesh_pos()
    chips = [(1 - x, y), (x, 1 - y), (1 - x, 1 - y)]
    outgoing, incoming = [], []
    for t, key in enumerate(names):
        size = refs[t].shape[SHARD_AXIS[key]] // N_CHIPS
        slot = lambda px, py: _window(refs[t], SHARD_AXIS[key], pl.multiple_of((2 * px + py) * size, 128), size)
        for k, (px, py) in enumerate(chips):
            sems = dict(send_sem=send_sems.at[3 * t + k], recv_sem=recv_sems.at[3 * t + k], device_id=(px, py, c), device_id_type=MESH)
            outgoing.append(pltpu.make_async_remote_copy(slot(x, y), slot(x, y), **sems))
            incoming.append(pltpu.make_async_remote_copy(slot(x, y), slot(px, py), **sems))
    return outgoing, incoming


def _allgather_now(wholes):
    names = list(wholes)
    n = len(names)

    def body(*refs):
        send_sems, recv_sems = refs[2 * n:]
        outgoing, incoming = _gather_copies(names, refs[n:2 * n], send_sems, recv_sems)
        for cp in outgoing:
            cp.start()
        for cp in incoming:
            cp.wait_recv()
        for cp in outgoing:
            cp.wait_send()

    outs = pl.pallas_call(
        body, name="allgather_now", in_specs=[ANY] * n, out_specs=[ANY] * n,
        out_shape=[_sds(wholes[k].shape, wholes[k].dtype) for k in names],
        input_output_aliases={t: t for t in range(n)},
        scratch_shapes=[pltpu.SemaphoreType.DMA((3 * n,)), pltpu.SemaphoreType.DMA((3 * n,))],
    )(*[wholes[k] for k in names])
    return dict(zip(names, outs))


def _allgather_start(wholes):
    names = list(wholes)
    n = len(names)

    def body(*refs):
        send_sems, recv_sems = refs[n:n + 2]
        outgoing, _ = _gather_copies(names, refs[:n], send_sems, recv_sems)
        for cp in outgoing:
            cp.start()
        token = refs[-1]
        token[...] = jnp.zeros_like(token)

    outs = pl.pallas_call(
        body, name="allgather_start", in_specs=[HBM] * n,
        out_specs=[SEM, SEM] + [HBM] * n + [pl.BlockSpec(memory_space=pltpu.VMEM)],
        out_shape=[pltpu.SemaphoreType.DMA((3 * n,)), pltpu.SemaphoreType.DMA((3 * n,))]
        + [pltpu.HBM(wholes[k].shape, wholes[k].dtype) for k in names] + [_sds((8, HEAD_DIM), F32)],
        input_output_aliases={t: 2 + t for t in range(n)},
        compiler_params=SPLIT_COPY,
    )(*[_in_hbm(wholes[k]) for k in names])
    return (names, outs[0], outs[1], outs[2:2 + n]), outs[-1]


def _allgather_wait(state, after):
    names, send_sems, recv_sems, thru = state
    n = len(names)

    def body(*refs):
        send_ref, recv_ref = refs[n:n + 2]
        outgoing, incoming = _gather_copies(names, refs[:n], send_ref, recv_ref)
        for cp in outgoing:
            cp.wait_send()
        for cp in incoming:
            cp.wait_recv()

    outs = pl.pallas_call(
        body, name="allgather_wait", in_specs=[HBM] * n + [SEM, SEM, ANY], out_specs=[HBM] * n,
        out_shape=[pltpu.HBM(t.shape, t.dtype) for t in thru],
        input_output_aliases={t: t for t in range(n)},
        compiler_params=SPLIT_COPY,
    )(*thru, send_sems, recv_sems, after)
    return dict(zip(names, outs))


FLIPS = [(a, b, d) for a in (0, 1) for b in (0, 1) for d in (0, 1) if (a, b, d) != (0, 0, 0)]


def _flip(pos, f):
    return tuple(1 - p if fi else p for p, fi in zip(pos, f))


def _piece_shape(weight, shape):
    out = list(shape)
    out[SHARD_AXIS[weight]] //= N_CHIPS
    out[1] //= 2
    return tuple(out)


def _piece(ref, weight, chip, core):
    piece = _piece_shape(weight, ref.shape)
    if SHARD_AXIS[weight] == 1:
        return _window(ref, 1, pl.multiple_of((2 * chip + core) * piece[1], 8), piece[1])
    return _window(_window(ref, 2, pl.multiple_of(chip * piece[2], 128), piece[2]), 1, pl.multiple_of(core * piece[1], 8), piece[1])


def _scatter_copies(weight, grad_ref, land_ref, send_sems, recv_sems):
    pos = _mesh_pos()
    outgoing, incoming = [], []
    for k, f in enumerate(FLIPS):
        peer = _flip(pos, f)
        sems = dict(send_sem=send_sems.at[k], recv_sem=recv_sems.at[k], device_id=peer, device_id_type=MESH)
        outgoing.append(pltpu.make_async_remote_copy(_piece(grad_ref, weight, 2 * peer[0] + peer[1], peer[2]), land_ref.at[k], **sems))
        incoming.append(pltpu.make_async_remote_copy(_piece(grad_ref, weight, 2 * pos[0] + pos[1], pos[2]), land_ref.at[k], **sems))
    return outgoing, incoming


def _scatter_start(tag, weight, grad):
    n_peer = len(FLIPS)
    land = lax.empty((n_peer,) + _piece_shape(weight, grad.shape), grad.dtype)

    def body(grad_ref, land_ref, send_sems, recv_sems, grad_thru, land_thru, token):
        outgoing, _ = _scatter_copies(weight, grad_ref, land_ref, send_sems, recv_sems)
        for cp in outgoing:
            cp.start()
        token[...] = jnp.zeros_like(token)

    outs = pl.pallas_call(
        body, name=f"scatter_start_{tag}", in_specs=[HBM, HBM],
        out_specs=[SEM, SEM, HBM, HBM, pl.BlockSpec(memory_space=pltpu.VMEM)],
        out_shape=[pltpu.SemaphoreType.DMA((n_peer,)), pltpu.SemaphoreType.DMA((n_peer,)),
                   pltpu.HBM(grad.shape, grad.dtype), pltpu.HBM(land.shape, land.dtype), _sds((8, HEAD_DIM), F32)],
        input_output_aliases={0: 2, 1: 3},
        compiler_params=SPLIT_COPY,
    )(_in_hbm(grad), _in_hbm(land))
    return (tag, weight, outs[:4]), outs[4]


def _scatter_wait(state, after):
    tag, weight, (send_sems, recv_sems, grad_thru, land_thru) = state

    def body(grad_ref, land_ref, send_ref, recv_ref, after_ref, grad_out, land_out):
        outgoing, incoming = _scatter_copies(weight, grad_ref, land_ref, send_ref, recv_ref)
        for cp in outgoing:
            cp.wait_send()
        for cp in incoming:
            cp.wait_recv()

    return pl.pallas_call(
        body, name=f"scatter_wait_{tag}", in_specs=[HBM, HBM, SEM, SEM, ANY], out_specs=[HBM, HBM],
        out_shape=[pltpu.HBM(grad_thru.shape, grad_thru.dtype), pltpu.HBM(land_thru.shape, land_thru.dtype)],
        input_output_aliases={0: 0, 1: 1},
        compiler_params=SPLIT_COPY,
    )(grad_thru, land_thru, send_sems, recv_sems, after)


def _sum_pieces(tag, weight, grad, land, where, *, tr=128):
    _, R, C = _piece_shape(weight, grad.shape)
    tr = min(tr, R)
    nr = R // tr

    def body(where_ref, g_ref, l_ref, o_ref):
        acc = g_ref[...].astype(F32)
        for s in range(len(FLIPS)):
            acc = acc + l_ref[s].astype(F32)
        o_ref[...] = acc

    if SHARD_AXIS[weight] == 1:
        own = lambda i, where_ref: (0, (2 * where_ref[0] + where_ref[1]) * nr + i, 0)
    else:
        own = lambda i, where_ref: (0, where_ref[1] * nr + i, where_ref[0])
    grid_spec = pltpu.PrefetchScalarGridSpec(
        num_scalar_prefetch=1, grid=(nr,),
        in_specs=[pl.BlockSpec((1, tr, C), own), pl.BlockSpec((len(FLIPS), 1, tr, C), lambda i, where_ref: (0, 0, i, 0))],
        out_specs=pl.BlockSpec((1, tr, C), lambda i, where_ref: (0, where_ref[1] * nr + i, 0)))
    return pl.pallas_call(body, name=f"sum_{tag}", grid_spec=grid_spec, out_shape=_sds((1, 2 * R, C), F32),
                          compiler_params=_params("parallel"))(where, grad, land)


def _exchange_halves(shards):
    names = list(shards)
    n = len(names)
    half_sizes = [shards[k].shape[1] // 2 for k in names]

    def body(*refs):
        outs = refs[n:2 * n]
        send_sems, recv_sems = refs[2 * n:]
        x, y, c = _mesh_pos()
        half = lambda t, core: _window(outs[t], 1, pl.multiple_of(core * half_sizes[t], 8), half_sizes[t])
        sends = []
        for t in range(n):
            sends.append(pltpu.make_async_remote_copy(half(t, c), half(t, c), send_sems.at[t], recv_sems.at[t],
                                                      device_id=(x, y, 1 - c), device_id_type=MESH))
            sends[-1].start()
        for t in range(n):
            pltpu.make_async_remote_copy(half(t, c), half(t, 1 - c), send_sems.at[t], recv_sems.at[t],
                                         device_id=(x, y, 1 - c), device_id_type=MESH).wait_recv()
        for cp in sends:
            cp.wait_send()

    outs = pl.pallas_call(
        body, name="exchange_halves", in_specs=[ANY] * n, out_specs=[ANY] * n,
        out_shape=[_sds(shards[k].shape, F32) for k in names],
        input_output_aliases={t: t for t in range(n)},
        scratch_shapes=[pltpu.SemaphoreType.DMA((n,)), pltpu.SemaphoreType.DMA((n,))],
    )(*[shards[k] for k in names])
    return dict(zip(names, outs))


def _adamw_math(w, g, m, v):
    m = ADAM_B1 * m + (1.0 - ADAM_B1) * g
    v = ADAM_B2 * v + (1.0 - ADAM_B2) * jnp.square(g)
    m_hat = m / (1.0 - ADAM_B1 ** ADAM_STEP)
    v_hat = v / (1.0 - ADAM_B2 ** ADAM_STEP)
    return -ADAM_LR * (m_hat / (jnp.sqrt(v_hat) + ADAM_EPS) + ADAM_WD * w), m, v


def _adamw(name, w, g, m, v, *, tr=256):
    R, C = w.shape
    tr = min(tr, R)

    def body(w_ref, g_ref, m_ref, v_ref, d_ref, mo_ref, vo_ref):
        d_ref[...], mo_ref[...], vo_ref[...] = _adamw_math(w_ref[...], g_ref[...], m_ref[...], v_ref[...])

    spec = pl.BlockSpec((tr, C), lambda i: (i, 0))
    return pl.pallas_call(body, name=name, grid=(R // tr,), in_specs=[spec] * 4, out_specs=[spec] * 3,
                          out_shape=[_sds((R, C), F32)] * 3, compiler_params=_params("parallel"))(w, g, m, v)


SLAB_ROWS = 8
LOSS_ROW = 7


def _pack_small(d):
    pad = lambda a, width: jnp.pad(a.reshape(-1), (0, width - a.size))
    row5 = jnp.concatenate([d["swa_q_norm"].reshape(-1), d["swa_k_norm"].reshape(-1), d["ax_q_norm"].reshape(-1),
                            d["ax_k_norm"].reshape(-1), pad(d["swa_sink"], HEAD_DIM), pad(d["ret_decay_logit"], HEAD_DIM),
                            jnp.zeros((2 * HEAD_DIM,), F32)])
    return jnp.concatenate([d["norm_mix"], d["norm_mlp"], d["ret_norm"], row5[None], pad(d["t5_table"], D_MODEL)[None],
                            jnp.zeros((1, D_MODEL), F32)], axis=0)


def _unpack_small(slab):
    r5 = slab[5]
    return {
        "norm_mix": slab[0:2], "norm_mlp": slab[2:4], "ret_norm": slab[4:5],
        "swa_q_norm": r5[None, 0:128], "swa_k_norm": r5[None, 128:256], "ax_q_norm": r5[None, 256:384],
        "ax_k_norm": r5[None, 384:512], "swa_sink": r5[None, 512:512 + SWA_HEADS],
        "ret_decay_logit": r5[640:640 + 2 * RET_HEADS].reshape(1, 2, RET_HEADS),
        "t5_table": slab[6, :T5_BUCKETS * SWA_HEADS].reshape(T5_BUCKETS, SWA_HEADS),
    }


def _small_allreduce_adamw(g_slab, w_slab, m_slab, v_slab, loss_part):
    def body(g_ref, w_ref, m_ref, v_ref, lp_ref, go_ref, d_ref, mo_ref, vo_ref, gath, send_sems, recv_sems):
        pos = _mesh_pos()
        ident = lambda p: 4 * p[0] + 2 * p[1] + p[2]
        me = ident(pos)
        row = lax.broadcasted_iota(jnp.int32, (SLAB_ROWS, D_MODEL), 0)
        lane = lax.broadcasted_iota(jnp.int32, (SLAB_ROWS, D_MODEL), 1)
        loss = jnp.sum(jnp.sum(lp_ref[...], axis=0, keepdims=True), axis=1, keepdims=True) * (0.5 / D_MODEL)
        gath[me] = jnp.where(jnp.logical_and(row == LOSS_ROW, lane == 0), loss, g_ref[...])
        sends = []
        for k, f in enumerate(FLIPS):
            sends.append(pltpu.make_async_remote_copy(gath.at[me], gath.at[me], send_sems.at[k], recv_sems.at[k],
                                                      device_id=_flip(pos, f), device_id_type=MESH))
            sends[-1].start()
        for k, f in enumerate(FLIPS):
            peer = _flip(pos, f)
            pltpu.make_async_remote_copy(gath.at[me], gath.at[ident(peer)], send_sems.at[k], recv_sems.at[k],
                                         device_id=peer, device_id_type=MESH).wait_recv()
        for cp in sends:
            cp.wait_send()
        total = gath[0]
        for s in range(1, N_DEV):
            total = total + gath[s]
        go_ref[...] = total
        d_ref[...], mo_ref[...], vo_ref[...] = _adamw_math(w_ref[...], total, m_ref[...], v_ref[...])

    vmem = pl.BlockSpec(memory_space=pltpu.VMEM)
    return pl.pallas_call(
        body, name="small_allreduce_adamw", in_specs=[vmem] * 5, out_specs=[vmem] * 4,
        out_shape=[_sds((SLAB_ROWS, D_MODEL), F32)] * 4,
        scratch_shapes=[pltpu.VMEM((N_DEV, SLAB_ROWS, D_MODEL), F32),
                        pltpu.SemaphoreType.DMA((len(FLIPS),)), pltpu.SemaphoreType.DMA((len(FLIPS),))],
    )(g_slab, w_slab, m_slab, v_slab, loss_part)


def kernel(x, norm_mix, norm_mlp, w_in_even, w_out_even, ret_decay_logit, ret_norm, swa_q_norm, swa_k_norm, swa_sink, t5_table, w_in_odd, w_out_odd, ax_q_norm, ax_k_norm, w_mlp_up, w_mlp_down, loss_target, m_norm_mix, m_norm_mlp, m_w_in_even, m_w_out_even, m_ret_decay_logit, m_ret_norm, m_swa_q_norm, m_swa_k_norm, m_swa_sink, m_t5_table, m_w_in_odd, m_w_out_odd, m_ax_q_norm, m_ax_k_norm, m_w_mlp_up, m_w_mlp_down, v_norm_mix, v_norm_mlp, v_w_in_even, v_w_out_even, v_ret_decay_logit, v_ret_norm, v_swa_q_norm, v_swa_k_norm, v_swa_sink, v_t5_table, v_w_in_odd, v_w_out_odd, v_ax_q_norm, v_ax_k_norm, v_w_mlp_up, v_w_mlp_down):
    w = dict(zip(WEIGHTS, (norm_mix, norm_mlp, w_in_even, w_out_even, ret_decay_logit, ret_norm, swa_q_norm, swa_k_norm,
                           swa_sink, t5_table, w_in_odd, w_out_odd, ax_q_norm, ax_k_norm, w_mlp_up, w_mlp_down)))
    m = dict(zip(WEIGHTS, (m_norm_mix, m_norm_mlp, m_w_in_even, m_w_out_even, m_ret_decay_logit, m_ret_norm, m_swa_q_norm,
                           m_swa_k_norm, m_swa_sink, m_t5_table, m_w_in_odd, m_w_out_odd, m_ax_q_norm, m_ax_k_norm,
                           m_w_mlp_up, m_w_mlp_down)))
    v = dict(zip(WEIGHTS, (v_norm_mix, v_norm_mlp, v_w_in_even, v_w_out_even, v_ret_decay_logit, v_ret_norm, v_swa_q_norm,
                           v_swa_k_norm, v_swa_sink, v_t5_table, v_w_in_odd, v_w_out_odd, v_ax_q_norm, v_ax_k_norm,
                           v_w_mlp_up, v_w_mlp_down)))
    flat = lambda a: a.reshape(-1, a.shape[-1])

    chip = (2 * lax.axis_index("x") + lax.axis_index("y")).astype(jnp.int32)
    where = jnp.stack([chip, lax.axis_index("c").astype(jnp.int32)])

    placed = {k: _cast_place(k, w[k], where[0:1]) for k in BIG}
    first = _allgather_now({"w_in_even": placed["w_in_even"]})
    gather, gather_token = _allgather_start({k: placed[k] for k in BIG if k != "w_in_even"})
    unstack = lambda whole: {k: (a if k.startswith("w_mlp") else a[0]) for k, a in whole.items()}

    in_flight = []

    def push(tag, weight, dw):
        state, token = _scatter_start(tag, weight, dw)
        in_flight.append(state)
        return token

    loss_part, dx, small_g = _local_step(x[0], loss_target[0], {k: w[k] for k in SMALL}, unstack(first),
                                         lambda after: unstack(_allgather_wait(gather, after)), push, (gather_token,))

    halves = {}
    for state in in_flight:
        tag, weight = state[0], state[1]
        dw, land = _scatter_wait(state, dx)
        halves[tag] = _sum_pieces(tag, weight, dw, land, where)
    reduced = _exchange_halves(halves)
    grad = {"w_in_even": reduced["in_even"], "w_out_even": reduced["out_even"],
            "w_in_odd": reduced["in_odd"], "w_out_odd": reduced["out_odd"],
            "w_mlp_up": jnp.concatenate([reduced["mlp_up0"], reduced["mlp_up1"]], axis=0),
            "w_mlp_down": jnp.concatenate([reduced["mlp_down0"], reduced["mlp_down1"]], axis=0)}
    delta, new_m, new_v = {}, {}, {}
    for k in BIG:
        d_k, m_k, v_k = _adamw(f"adamw_{k}", flat(w[k]), flat(grad[k]), flat(m[k]), flat(v[k]))
        delta[k], new_m[k], new_v[k] = d_k.reshape(w[k].shape), m_k.reshape(w[k].shape), v_k.reshape(w[k].shape)

    slabs = _small_allreduce_adamw(_pack_small(small_g), _pack_small({k: w[k] for k in SMALL}),
                                   _pack_small({k: m[k] for k in SMALL}), _pack_small({k: v[k] for k in SMALL}), loss_part)
    loss = slabs[0][LOSS_ROW, 0]
    for out, slab in zip((grad, delta, new_m, new_v), slabs):
        out.update(_unpack_small(slab))

    return (loss, dx[None], *[grad[k] for k in WEIGHTS], *[delta[k] for k in WEIGHTS],
            *[new_m[k] for k in WEIGHTS], *[new_v[k] for k in WEIGHTS])
```

```python
import functools
import math

import jax
import jax.numpy as jnp
from jax import lax
from jax.experimental import pallas as pl
from jax.experimental.pallas import tpu as pltpu

F32 = jnp.float32
BF16 = jnp.bfloat16
_MXU = BF16
_WIRE = BF16

D_MODEL = 1024
HEAD_DIM = 128
EPS = 1e-6
NEG_INF = -1e30
CHUNK = 128
GRID_W = 64
RET_HEADS, RET_DK, RET_DV = 4, 128, 256
RET_Q, RET_V = RET_HEADS * RET_DK, RET_HEADS * RET_DV
RET_THETA = 10000.0
SWA_HEADS, SWA_KV_HEADS = 8, 2
T5_BUCKETS, T5_MAX_DIST = 32, 128
AX_HEADS, AX_KV_HEADS = 8, 2
AX_THETA = 10000.0
D_FF = 4 * D_MODEL
EVEN_IN = 2 * RET_Q + 2 * RET_V + D_MODEL + 2 * SWA_KV_HEADS * HEAD_DIM
ODD_IN = D_MODEL + 2 * AX_KV_HEADS * HEAD_DIM
ATT_SCALE = HEAD_DIM ** -0.5

ADAM_LR, ADAM_B1, ADAM_B2, ADAM_EPS, ADAM_WD, ADAM_STEP = 0.001, 0.9, 0.999, 1e-08, 0.01, 10

N_DEV = 8
VMEM_LIMIT_BYTES = 56 << 20
MESH = pl.DeviceIdType.MESH

_NN = (((1,), (0,)), ((), ()))
_NT = (((1,), (1,)), ((), ()))
_TN = (((0,), (0,)), ((), ()))


def _dot(a, b, dn=_NN):
    return lax.dot_general(a.astype(_MXU), b.astype(_MXU), dn, preferred_element_type=F32)


def _params(*sem):
    return pltpu.CompilerParams(dimension_semantics=sem, vmem_limit_bytes=VMEM_LIMIT_BYTES)


def _sds(shape, dtype):
    return jax.ShapeDtypeStruct(tuple(shape), dtype)


def _rowsum8(x):
    return jnp.sum(x.reshape(x.shape[0] // 8, 8, x.shape[1]), axis=0)


def _swap_halves(x, half):
    width = x.shape[1]
    lane = lax.broadcasted_iota(jnp.int32, x.shape, 1)
    up = pltpu.roll(x, width - half, axis=1)
    down = pltpu.roll(x, half, axis=1)
    return jnp.where((lane & (2 * half - 1)) < half, up, down)


def _sigmoid(x):
    return 1.0 / (1.0 + jnp.exp(-x))


def _norm_matmul(name, x, gain, w, *, tm, tn, out_dtype):
    T, K = x.shape
    N = w.shape[1]
    tm, tn = min(tm, T), min(tn, N)

    def body(x_ref, g_ref, w_ref, y_ref, h_ref, h_sc):
        @pl.when(pl.program_id(1) == 0)
        def _():
            xv = x_ref[...]
            r = lax.rsqrt(jnp.mean(xv * xv, axis=-1, keepdims=True) + EPS)
            h = (xv * r * g_ref[...]).astype(_MXU)
            h_sc[...] = h
            h_ref[...] = h
        y_ref[...] = jnp.dot(h_sc[...], w_ref[...], preferred_element_type=F32).astype(y_ref.dtype)

    return pl.pallas_call(
        body, name=name, grid=(T // tm, N // tn),
        in_specs=[pl.BlockSpec((tm, K), lambda i, j: (i, 0)),
                  pl.BlockSpec((1, K), lambda i, j: (0, 0)),
                  pl.BlockSpec((K, tn), lambda i, j: (0, j))],
        out_specs=[pl.BlockSpec((tm, tn), lambda i, j: (i, j)),
                   pl.BlockSpec((tm, K), lambda i, j: (i, 0))],
        out_shape=[_sds((T, N), out_dtype), _sds((T, K), _MXU)],
        scratch_shapes=[pltpu.VMEM((tm, K), _MXU)],
        compiler_params=_params("parallel", "arbitrary"),
    )(x, gain, w)


def _matmul_res(name, a_list, w, res, *, tm, relu2=False, target=None):
    T = res.shape[0]
    N = w.shape[1]
    K = a_list[0].shape[1]
    n_a = len(a_list)
    tm = min(tm, T)
    with_loss = target is not None

    def body(*refs):
        a_refs = refs[:n_a]
        w_refs = refs[n_a:2 * n_a]
        res_ref = refs[2 * n_a]
        acc = res_ref[...]
        for a_ref, w_ref in zip(a_refs, w_refs):
            a = a_ref[...]
            if relu2:
                a = jnp.square(jnp.maximum(a.astype(F32), 0.0))
            acc = acc + _dot(a, w_ref[...])
        if with_loss:
            tgt_ref, g_ref, loss_ref = refs[2 * n_a + 1:]
            diff = acc - tgt_ref[...]
            g_ref[...] = diff * (1.0 / N)

            @pl.when(pl.program_id(0) == 0)
            def _():
                loss_ref[...] = jnp.zeros_like(loss_ref)
            loss_ref[...] += _rowsum8(diff * diff)
        else:
            refs[2 * n_a + 1][...] = acc

    row = lambda i: (i, 0)
    in_specs = [pl.BlockSpec((tm, K), row) for _ in a_list]
    in_specs += [pl.BlockSpec((K, N), functools.partial(lambda i, b: (b, 0), b=b)) for b in range(n_a)]
    in_specs += [pl.BlockSpec((tm, N), row)]
    args = list(a_list) + [w] * n_a + [res]
    if with_loss:
        in_specs.append(pl.BlockSpec((tm, N), row))
        args.append(target)
        out_specs = [pl.BlockSpec((tm, N), row), pl.BlockSpec((8, N), lambda i: (0, 0))]
        out_shape = [_sds((T, N), F32), _sds((8, N), F32)]
        sem = "arbitrary"
    else:
        out_specs = pl.BlockSpec((tm, N), row)
        out_shape = _sds((T, N), F32)
        sem = "parallel"
    return pl.pallas_call(body, name=name, grid=(T // tm,), in_specs=in_specs, out_specs=out_specs,
                          out_shape=out_shape, compiler_params=_params(sem))(*args)


def _matmul_nt(name, a, w, *, tm, tn, out_dtype, relu_of=None):
    T, K = a.shape
    N = w.shape[0]
    tm, tn = min(tm, T), min(tn, N)

    def body(*refs):
        if relu_of is None:
            a_ref, w_ref, o_ref = refs
            o_ref[...] = _dot(a_ref[...], w_ref[...], _NT).astype(o_ref.dtype)
        else:
            a_ref, w_ref, u_ref, o_ref = refs
            da = _dot(a_ref[...], w_ref[...], _NT)
            o_ref[...] = (da * (2.0 * jnp.maximum(u_ref[...].astype(F32), 0.0))).astype(o_ref.dtype)

    in_specs = [pl.BlockSpec((tm, K), lambda i, j: (i, 0)), pl.BlockSpec((tn, K), lambda i, j: (j, 0))]
    args = [a, w]
    if relu_of is not None:
        in_specs.append(pl.BlockSpec((tm, tn), lambda i, j: (i, j)))
        args.append(relu_of)
    return pl.pallas_call(body, name=name, grid=(T // tm, N // tn), in_specs=in_specs,
                          out_specs=pl.BlockSpec((tm, tn), lambda i, j: (i, j)),
                          out_shape=_sds((T, N), out_dtype),
                          compiler_params=_params("parallel", "parallel"))(*args)


def _matmul_nt_normbwd(name, dy, w, x, gain, dres, *, tm):
    T, K = dy.shape
    N = w.shape[0]
    tm = min(tm, T)

    def body(dy_ref, w_ref, x_ref, g_ref, dres_ref, dx_ref, dg_ref):
        dh = _dot(dy_ref[...], w_ref[...], _NT)
        xv = x_ref[...]
        r = lax.rsqrt(jnp.mean(xv * xv, axis=-1, keepdims=True) + EPS)
        xhat = xv * r
        dxhat = dh * g_ref[...]
        dx_ref[...] = dres_ref[...] + r * (dxhat - xhat * jnp.mean(dxhat * xhat, axis=-1, keepdims=True))

        @pl.when(pl.program_id(0) == 0)
        def _():
            dg_ref[...] = jnp.zeros_like(dg_ref)
        dg_ref[...] += _rowsum8(dh * xhat)

    row = lambda i: (i, 0)
    return pl.pallas_call(
        body, name=name, grid=(T // tm,),
        in_specs=[pl.BlockSpec((tm, K), row), pl.BlockSpec((N, K), lambda i: (0, 0)),
                  pl.BlockSpec((tm, N), row), pl.BlockSpec((1, N), lambda i: (0, 0)), pl.BlockSpec((tm, N), row)],
        out_specs=[pl.BlockSpec((tm, N), row), pl.BlockSpec((8, N), lambda i: (0, 0))],
        out_shape=[_sds((T, N), F32), _sds((8, N), F32)],
        compiler_params=_params("arbitrary"),
    )(dy, w, x, gain, dres)


def _matmul_tn(name, a, b, *, tk, tn, tt, out_dtype, relu2=False):
    T, Ka = a.shape
    Nb = b.shape[1]
    tk, tn, tt = min(tk, Ka), min(tn, Nb), min(tt, T)
    nt = T // tt

    def body(a_ref, b_ref, o_ref, acc):
        t = pl.program_id(2)

        @pl.when(t == 0)
        def _():
            acc[...] = jnp.zeros_like(acc)
        av = a_ref[...]
        if relu2:
            av = jnp.square(jnp.maximum(av.astype(F32), 0.0))
        acc[...] += _dot(av, b_ref[...], _TN)

        @pl.when(t == nt - 1)
        def _():
            o_ref[...] = acc[...].astype(o_ref.dtype)

    return pl.pallas_call(
        body, name=name, grid=(Ka // tk, Nb // tn, nt),
        in_specs=[pl.BlockSpec((tt, tk), lambda i, j, t: (t, i)), pl.BlockSpec((tt, tn), lambda i, j, t: (t, j))],
        out_specs=pl.BlockSpec((tk, tn), lambda i, j, t: (i, j)),
        out_shape=_sds((Ka, Nb), out_dtype),
        scratch_shapes=[pltpu.VMEM((tk, tn), F32)],
        compiler_params=_params("parallel", "parallel", "arbitrary"),
    )(a, b)


def _rope_angles(pos, dim, theta):
    inv = theta ** (-jnp.arange(0, dim, 2, dtype=F32) / dim)
    return pos.astype(F32)[:, None] * inv[None, :]


def _ret_rope_tables(T):
    ang = _rope_angles(jnp.arange(T), RET_DK, RET_THETA)
    c, s = jnp.cos(ang), jnp.sin(ang)
    return jnp.concatenate([c, c], axis=1), jnp.concatenate([-s, s], axis=1)


def _axial_rope_tables(T):
    rows = T // GRID_W
    row = jnp.repeat(jnp.arange(rows), GRID_W)
    col = jnp.tile(jnp.arange(GRID_W), rows)
    ar = _rope_angles(row, HEAD_DIM // 2, AX_THETA)
    ac = _rope_angles(col, HEAD_DIM // 2, AX_THETA)
    cos = jnp.concatenate([jnp.cos(ar), jnp.cos(ar), jnp.cos(ac), jnp.cos(ac)], axis=1)
    sin = jnp.concatenate([-jnp.sin(ar), jnp.sin(ar), -jnp.sin(ac), jnp.sin(ac)], axis=1)
    return cos, sin


(TAB_D, TAB_DT, TAB_EF, TAB_EB, TAB_A, TAB_B, TAB_CF, TAB_CB,
 TAB_RA, TAB_RB, TAB_RCF, TAB_RCB, TAB_KF, TAB_KB) = range(14)


def _retention_tables(decay_logit):
    lg = jax.nn.log_sigmoid(decay_logit.astype(F32))
    lam, mu = lg[0][:, None, None], lg[1][:, None, None]
    idx = jnp.arange(CHUNK, dtype=F32)
    diff = (idx[:, None] - idx[None, :])[None]
    df = jnp.where(diff >= 0, jnp.exp(jnp.maximum(diff, 0.0) * lam), 0.0)
    db = jnp.where(diff < 0, jnp.exp(jnp.maximum(-diff, 0.0) * mu), 0.0)
    d = df + db
    r = idx[None, :, None]
    ones = jnp.ones((1, 1, CHUNK), F32)
    a = jnp.exp((r + 1.0) * lam) * ones
    b = jnp.exp((CHUNK - r) * mu) * ones
    cf = jnp.exp((CHUNK - 1.0 - r) * lam) * ones
    cb = jnp.exp(r * mu) * ones
    full = jnp.ones((1, CHUNK, CHUNK), F32)
    kf = CHUNK * jnp.exp(CHUNK * lam) * full
    kb = CHUNK * jnp.exp(CHUNK * mu) * full
    tabs = jnp.stack([d, jnp.swapaxes(d, 1, 2), diff * df, -diff * db, a, b, cf, cb,
                      (r + 1.0) * a, (CHUNK - r) * b, (CHUNK - 1.0 - r) * cf, r * cb, kf, kb], axis=1)

    def lanes(tab):
        return jnp.transpose(tab, (1, 0, 2)).reshape(CHUNK, RET_HEADS * CHUNK)

    def dec(l):
        return jnp.exp(CHUNK * l)[:, 0, :] * jnp.ones((1, RET_DV), F32)

    weights = dict(a=lanes(a), b=lanes(b), cf=lanes(cf), cb=lanes(cb), dec_f=dec(lam), dec_b=dec(mu))
    return tabs, weights, lg


def _t5_bucket(rel):
    nb = T5_BUCKETS // 2
    max_exact = nb // 2
    ret = jnp.where(rel > 0, nb, 0)
    n = jnp.abs(rel)
    nf = jnp.maximum(n, 1).astype(F32)
    large = max_exact + (jnp.log(nf / max_exact) / math.log(T5_MAX_DIST / max_exact)
                         * (nb - max_exact)).astype(jnp.int32)
    large = jnp.minimum(large, nb - 1)
    return ret + jnp.where(n < max_exact, n, large)


def _swa_rel():
    r = jnp.arange(CHUNK)
    j = jnp.arange(3 * CHUNK)
    return j[None, :] - CHUNK - r[:, None]


def _swa_bias(t5_table):
    rel = _swa_rel()
    bucket = jnp.where(jnp.abs(rel) <= CHUNK, _t5_bucket(rel), -1).astype(jnp.int32)

    def body(tab_ref, bk_ref, o_ref):
        bk = bk_ref[...]
        for h in range(SWA_HEADS):
            pick = lambda b, acc, h=h: jnp.where(bk == b, tab_ref[b, h], acc)
            o_ref[h] = lax.fori_loop(0, T5_BUCKETS, pick, jnp.full(bk.shape, NEG_INF, F32))

    return pl.pallas_call(
        body, name="t5_bias",
        in_specs=[pl.BlockSpec(memory_space=pltpu.SMEM), pl.BlockSpec(memory_space=pltpu.VMEM)],
        out_specs=pl.BlockSpec(memory_space=pltpu.VMEM),
        out_shape=_sds((SWA_HEADS, CHUNK, 3 * CHUNK), F32),
    )(t5_table.astype(F32), bucket)


def _prep_even(proj, cos, sin, q_gain, k_gain, *, tm):
    T = proj.shape[0]
    tm = min(tm, T)

    def body(qa_ref, ka_ref, qb_ref, kb_ref, c_ref, s_ref, qg_ref, kg_ref, qr_ref, kr_ref, qn_ref, kn_ref):
        c = jnp.concatenate([c_ref[...]] * RET_HEADS, axis=1)
        s = jnp.concatenate([s_ref[...]] * RET_HEADS, axis=1)
        qa = qa_ref[...]
        qr_ref[...] = (qa * c + _swap_halves(qa, RET_DK // 2) * s).astype(qr_ref.dtype)
        ka = ka_ref[...]
        kr_ref[...] = ((ka * c + _swap_halves(ka, RET_DK // 2) * s) * (RET_DK ** -0.5)).astype(kr_ref.dtype)
        for src, gain, dst, heads in ((qb_ref, qg_ref, qn_ref, SWA_HEADS), (kb_ref, kg_ref, kn_ref, SWA_KV_HEADS)):
            for h in range(heads):
                sl = slice(h * HEAD_DIM, (h + 1) * HEAD_DIM)
                xh = src[:, sl]
                r = lax.rsqrt(jnp.mean(xh * xh, axis=-1, keepdims=True) + EPS)
                dst[:, sl] = (xh * r * gain[...]).astype(dst.dtype)

    row = lambda i: (i, 0)
    const = lambda i: (0, 0)
    return pl.pallas_call(
        body, name="prep_even", grid=(T // tm,),
        in_specs=[pl.BlockSpec((tm, RET_Q), lambda i: (i, 0)), pl.BlockSpec((tm, RET_Q), lambda i: (i, 1)),
                  pl.BlockSpec((tm, D_MODEL), lambda i: (i, 3)), pl.BlockSpec((tm, 256), lambda i: (i, 16)),
                  pl.BlockSpec((tm, RET_DK), row), pl.BlockSpec((tm, RET_DK), row),
                  pl.BlockSpec((1, HEAD_DIM), const), pl.BlockSpec((1, HEAD_DIM), const)],
        out_specs=[pl.BlockSpec((tm, RET_Q), row), pl.BlockSpec((tm, RET_Q), row),
                   pl.BlockSpec((tm, D_MODEL), row), pl.BlockSpec((tm, 256), row)],
        out_shape=[_sds((T, RET_Q), _MXU), _sds((T, RET_Q), _MXU), _sds((T, D_MODEL), _MXU), _sds((T, 256), _MXU)],
        compiler_params=_params("parallel"),
    )(proj, proj, proj, proj, cos, sin, q_gain, k_gain)


def _ret_scan(name, x, y, y_col, w_asc, dec_asc, w_desc, dec_desc):
    T = x.shape[0]
    nc = T // CHUNK

    def body(xa_ref, ya_ref, xd_ref, yd_ref, wa_ref, da_ref, wd_ref, dd_ref, sa_out, sd_out, sa, sd):
        @pl.when(pl.program_id(0) == 0)
        def _():
            sa[...] = jnp.zeros_like(sa)
            sd[...] = jnp.zeros_like(sd)
        sa_out[0] = sa[...].astype(sa_out.dtype)
        sd_out[0] = sd[...].astype(sd_out.dtype)
        for x_ref, y_ref, w_ref, d_ref, st in ((xa_ref, ya_ref, wa_ref, da_ref, sa), (xd_ref, yd_ref, wd_ref, dd_ref, sd)):
            for h in range(RET_HEADS):
                ks = slice(h * RET_DK, (h + 1) * RET_DK)
                vs = slice(h * RET_DV, (h + 1) * RET_DV)
                u = _dot(x_ref[:, ks].astype(F32) * w_ref[:, ks], y_ref[:, vs], _TN)
                st[ks, :] = st[ks, :] * d_ref[h:h + 1, :] + u

    asc = lambda i: (i, 0)
    desc = lambda i: (nc - 1 - i, 0)
    const = lambda i: (0, 0)
    return pl.pallas_call(
        body, name=name, grid=(nc,),
        in_specs=[pl.BlockSpec((CHUNK, RET_Q), asc), pl.BlockSpec((CHUNK, RET_V), lambda i: (i, y_col)),
                  pl.BlockSpec((CHUNK, RET_Q), desc), pl.BlockSpec((CHUNK, RET_V), lambda i: (nc - 1 - i, y_col)),
                  pl.BlockSpec((CHUNK, RET_Q), const), pl.BlockSpec((RET_HEADS, RET_DV), const),
                  pl.BlockSpec((CHUNK, RET_Q), const), pl.BlockSpec((RET_HEADS, RET_DV), const)],
        out_specs=[pl.BlockSpec((1, RET_Q, RET_DV), lambda i: (i, 0, 0)),
                   pl.BlockSpec((1, RET_Q, RET_DV), lambda i: (nc - 1 - i, 0, 0))],
        out_shape=[_sds((nc, RET_Q, RET_DV), _MXU), _sds((nc, RET_Q, RET_DV), _MXU)],
        scratch_shapes=[pltpu.VMEM((RET_Q, RET_DV), F32), pltpu.VMEM((RET_Q, RET_DV), F32)],
        compiler_params=_params("arbitrary"),
    )(x, y, x, y, w_asc, dec_asc, w_desc, dec_desc)


def _ret_out(qr, kr, proj, sf, sb, tabs, gain):
    T = qr.shape[0]
    nc = T // CHUNK

    def body(q_ref, k_ref, v_ref, g_ref, sf_ref, sb_ref, tab_ref, gain_ref, o_ref, y_ref):
        for h in range(RET_HEADS):
            ks = slice(h * RET_DK, (h + 1) * RET_DK)
            vs = slice(h * RET_DV, (h + 1) * RET_DV)
            q, k, v = q_ref[:, ks], k_ref[:, ks], v_ref[:, vs]
            qf = q.astype(F32)
            a_mat = _dot(q, k, _NT) * tab_ref[h, 0]
            o = (_dot(a_mat, v) + _dot(qf * tab_ref[h, 1], sf_ref[0, ks, :]) + _dot(qf * tab_ref[h, 2], sb_ref[0, ks, :]))
            o_ref[:, vs] = o
            r = lax.rsqrt(jnp.mean(o * o, axis=-1, keepdims=True) + EPS)
            g = g_ref[:, vs]
            y_ref[:, vs] = (g * _sigmoid(g) * (o * r * gain_ref[:, vs])).astype(y_ref.dtype)

    row = lambda i: (i, 0)
    return pl.pallas_call(
        body, name="ret_out", grid=(nc,),
        in_specs=[pl.BlockSpec((CHUNK, RET_Q), row), pl.BlockSpec((CHUNK, RET_Q), row),
                  pl.BlockSpec((CHUNK, RET_V), lambda i: (i, 1)), pl.BlockSpec((CHUNK, RET_V), lambda i: (i, 2)),
                  pl.BlockSpec((1, RET_Q, RET_DV), lambda i: (i, 0, 0)), pl.BlockSpec((1, RET_Q, RET_DV), lambda i: (i, 0, 0)),
                  pl.BlockSpec((RET_HEADS, 3, CHUNK, CHUNK), lambda i: (0, 0, 0, 0)),
                  pl.BlockSpec((1, RET_V), lambda i: (0, 0))],
        out_specs=[pl.BlockSpec((CHUNK, RET_V), row), pl.BlockSpec((CHUNK, RET_V), row)],
        out_shape=[_sds((T, RET_V), F32), _sds((T, RET_V), _MXU)],
        compiler_params=_params("parallel"),
    )(qr, kr, proj, proj, sf, sb, tabs, gain)


def _ret_gate_bwd(dycat, proj, ret_o, gain, *, tm):
    T = ret_o.shape[0]
    tm = min(tm, T)

    def body(dy_ref, g_ref, o_ref, gain_ref, do_ref, dg_ref, dgain_ref):
        @pl.when(pl.program_id(0) == 0)
        def _():
            dgain_ref[...] = jnp.zeros_like(dgain_ref)
        for h in range(RET_HEADS):
            vs = slice(h * RET_DV, (h + 1) * RET_DV)
            o, g, dya, gn = o_ref[:, vs], g_ref[:, vs], dy_ref[:, vs], gain_ref[:, vs]
            r = lax.rsqrt(jnp.mean(o * o, axis=-1, keepdims=True) + EPS)
            ohat = o * r
            sg = _sigmoid(g)
            dy = dya * (g * sg)
            dg_ref[:, vs] = (dya * (ohat * gn) * (sg * (1.0 + g * (1.0 - sg)))).astype(dg_ref.dtype)
            dyg = dy * gn
            do_ref[:, vs] = (r * (dyg - ohat * jnp.mean(dyg * ohat, axis=-1, keepdims=True))).astype(do_ref.dtype)
            dgain_ref[:, vs] += _rowsum8(dy * ohat)

    row = lambda i: (i, 0)
    return pl.pallas_call(
        body, name="ret_gate_bwd", grid=(T // tm,),
        in_specs=[pl.BlockSpec((tm, RET_V), row), pl.BlockSpec((tm, RET_V), lambda i: (i, 2)),
                  pl.BlockSpec((tm, RET_V), row), pl.BlockSpec((1, RET_V), lambda i: (0, 0))],
        out_specs=[pl.BlockSpec((tm, RET_V), row), pl.BlockSpec((tm, RET_V), row), pl.BlockSpec((8, RET_V), lambda i: (0, 0))],
        out_shape=[_sds((T, RET_V), _MXU), _sds((T, RET_V), _MXU), _sds((8, RET_V), F32)],
        compiler_params=_params("arbitrary"),
    )(dycat, proj, ret_o, gain)


def _ret_bwd(qr, kr, proj, g_out, sf, sb, rf, rb, tabs):
    T = qr.shape[0]
    nc = T // CHUNK

    def body(q_ref, k_ref, v_ref, g_ref, sf_ref, sb_ref, rf_ref, rb_ref, tab_ref, dq_ref, dk_ref, dv_ref, dl_ref):
        @pl.when(pl.program_id(0) == 0)
        def _():
            dl_ref[...] = jnp.zeros_like(dl_ref)
        for h in range(RET_HEADS):
            ks = slice(h * RET_DK, (h + 1) * RET_DK)
            vs = slice(h * RET_DV, (h + 1) * RET_DV)
            q, k, v, g = q_ref[:, ks], k_ref[:, ks], v_ref[:, vs], g_ref[:, vs]
            s_f, s_b, r_f, r_b = sf_ref[0, ks, :], sb_ref[0, ks, :], rf_ref[0, ks, :], rb_ref[0, ks, :]
            tab = lambda t: tab_ref[h, t]
            qf, kf = q.astype(F32), k.astype(F32)
            qk = _dot(q, k, _NT)
            da_raw = _dot(g, v, _NT)
            x_f, x_b = _dot(g, s_f, _NT), _dot(g, s_b, _NT)
            dq_ref[:, ks] = _dot(da_raw * tab(TAB_D), k) + tab(TAB_A) * x_f + tab(TAB_B) * x_b
            at = _dot(k, q, _NT) * tab(TAB_DT)
            dat = _dot(v, g, _NT) * tab(TAB_DT)
            y_f, y_b = _dot(v, r_f, _NT), _dot(v, r_b, _NT)
            dk_ref[:, ks] = _dot(dat, q) + tab(TAB_CF) * y_f + tab(TAB_CB) * y_b
            dv_ref[:, vs] = (_dot(at, g) + _dot(kf * tab(TAB_CF), r_f) + _dot(kf * tab(TAB_CB), r_b)).astype(dv_ref.dtype)
            inner = da_raw * qk
            rs_f = r_f.astype(F32) * s_f.astype(F32)
            rs_b = r_b.astype(F32) * s_b.astype(F32)
            dl_f = (inner * tab(TAB_EF) + tab(TAB_RA) * qf * x_f + tab(TAB_RCF) * kf * y_f
                    + tab(TAB_KF) * (rs_f[:, :CHUNK] + rs_f[:, CHUNK:]))
            dl_b = (inner * tab(TAB_EB) + tab(TAB_RB) * qf * x_b + tab(TAB_RCB) * kf * y_b
                    + tab(TAB_KB) * (rs_b[:, :CHUNK] + rs_b[:, CHUNK:]))
            dl_ref[2 * h:2 * h + 1, :] += jnp.sum(dl_f, axis=0, keepdims=True)
            dl_ref[2 * h + 1:2 * h + 2, :] += jnp.sum(dl_b, axis=0, keepdims=True)

    row = lambda i: (i, 0)
    st = lambda i: (i, 0, 0)
    return pl.pallas_call(
        body, name="ret_bwd", grid=(nc,),
        in_specs=[pl.BlockSpec((CHUNK, RET_Q), row), pl.BlockSpec((CHUNK, RET_Q), row),
                  pl.BlockSpec((CHUNK, RET_V), lambda i: (i, 1)), pl.BlockSpec((CHUNK, RET_V), row),
                  pl.BlockSpec((1, RET_Q, RET_DV), st), pl.BlockSpec((1, RET_Q, RET_DV), st),
                  pl.BlockSpec((1, RET_Q, RET_DV), st), pl.BlockSpec((1, RET_Q, RET_DV), st),
                  pl.BlockSpec((RET_HEADS, 14, CHUNK, CHUNK), lambda i: (0, 0, 0, 0))],
        out_specs=[pl.BlockSpec((CHUNK, RET_Q), row), pl.BlockSpec((CHUNK, RET_Q), row),
                   pl.BlockSpec((CHUNK, RET_V), row), pl.BlockSpec((8, CHUNK), lambda i: (0, 0))],
        out_shape=[_sds((T, RET_Q), F32), _sds((T, RET_Q), F32), _sds((T, RET_V), _MXU), _sds((8, CHUNK), F32)],
        compiler_params=_params("arbitrary"),
    )(qr, kr, proj, g_out, sf, sb, rf, rb, tabs)


def _swa_probs(q, k_win, bias, sink, valid):
    s = _dot(q, k_win, _NT) * ATT_SCALE + bias
    s = jnp.where(valid, s, NEG_INF)
    m = jnp.maximum(jnp.max(s, axis=-1, keepdims=True), sink)
    p = jnp.exp(s - m)
    e_sink = jnp.exp(sink - m)
    inv = 1.0 / (jnp.sum(p, axis=-1, keepdims=True) + e_sink)
    return p * inv, e_sink * inv


def _swa_group(g, q_ref, bias_ref, sink_ref):
    group = SWA_HEADS // SWA_KV_HEADS
    heads = range(g * group, (g + 1) * group)
    q = jnp.concatenate([q_ref[:, h * HEAD_DIM:(h + 1) * HEAD_DIM] for h in heads], axis=0)
    bias = bias_ref[g * group:(g + 1) * group].reshape(group * CHUNK, 3 * CHUNK)
    sink = jnp.concatenate([jnp.broadcast_to(sink_ref[h:h + 1, 0:1], (CHUNK, 1)) for h in heads], axis=0)
    return q, bias, sink


def _swa_valid(i, nb):
    col = lax.broadcasted_iota(jnp.int32, (1, 3 * CHUNK), 1)
    return jnp.logical_and(jnp.logical_or(col >= CHUNK, i > 0), jnp.logical_or(col < 2 * CHUNK, i < nb - 1))


def _swa_window_specs(nb, width, col_block, clamp):
    prev = lambda i: (jnp.maximum(clamp(i) - 1, 0), col_block)
    cur = lambda i: (clamp(i), col_block)
    nxt = lambda i: (jnp.minimum(clamp(i) + 1, nb - 1), col_block)
    return [pl.BlockSpec((CHUNK, width), f) for f in (prev, cur, nxt)]


def _swa_fwd(qn, kn, proj, bias, sink):
    T = qn.shape[0]
    nb = T // CHUNK
    kvw = SWA_KV_HEADS * HEAD_DIM
    group = SWA_HEADS // SWA_KV_HEADS

    def body(q_ref, k0, k1, k2, v0, v1, v2, bias_ref, sink_ref, y_ref):
        i = pl.program_id(0)
        valid = _swa_valid(i, nb)
        for g in range(SWA_KV_HEADS):
            gs = slice(g * HEAD_DIM, (g + 1) * HEAD_DIM)
            k_win = jnp.concatenate([k0[:, gs], k1[:, gs], k2[:, gs]], axis=0)
            v_win = jnp.concatenate([v0[:, gs], v1[:, gs], v2[:, gs]], axis=0).astype(_MXU)
            q, bias, sink = _swa_group(g, q_ref, bias_ref, sink_ref)
            p, _ = _swa_probs(q, k_win, bias, sink, valid)
            o = _dot(p, v_win)
            for hh in range(group):
                h = g * group + hh
                y_ref[:, h * HEAD_DIM:(h + 1) * HEAD_DIM] = o[hh * CHUNK:(hh + 1) * CHUNK].astype(y_ref.dtype)

    ident = lambda i: i
    return pl.pallas_call(
        body, name="swa_fwd", grid=(nb,),
        in_specs=[pl.BlockSpec((CHUNK, D_MODEL), lambda i: (i, 0))]
        + _swa_window_specs(nb, kvw, 0, ident) + _swa_window_specs(nb, kvw, 17, ident)
        + [pl.BlockSpec((SWA_HEADS, CHUNK, 3 * CHUNK), lambda i: (0, 0, 0)), pl.BlockSpec((SWA_HEADS, HEAD_DIM), lambda i: (0, 0))],
        out_specs=pl.BlockSpec((CHUNK, D_MODEL), lambda i: (i, 0)),
        out_shape=_sds((T, D_MODEL), _MXU),
        compiler_params=_params("parallel"),
    )(qn, kn, kn, kn, proj, proj, proj, bias, sink)


def _swa_bwd(qn, kn, proj, dycat, bias, sink):
    T = qn.shape[0]
    nb = T // CHUNK
    kvw = SWA_KV_HEADS * HEAD_DIM
    group = SWA_HEADS // SWA_KV_HEADS

    def body(q_ref, k0, k1, k2, v0, v1, v2, dy_ref, bias_ref, sink_ref,
             dq_ref, dk_ref, dv_ref, dbias_ref, dsink_ref, acc_a, acc_b):
        i = pl.program_id(0)

        @pl.when(i == 0)
        def _():
            dbias_ref[...] = jnp.zeros_like(dbias_ref)
            dsink_ref[...] = jnp.zeros_like(dsink_ref)
            acc_a[...] = jnp.zeros_like(acc_a)
            acc_b[...] = jnp.zeros_like(acc_b)

        @pl.when(i < nb)
        def _():
            valid = _swa_valid(i, nb)
            for g in range(SWA_KV_HEADS):
                gs = slice(g * HEAD_DIM, (g + 1) * HEAD_DIM)
                k_win = jnp.concatenate([k0[:, gs], k1[:, gs], k2[:, gs]], axis=0)
                v_win = jnp.concatenate([v0[:, gs], v1[:, gs], v2[:, gs]], axis=0).astype(_MXU)
                q, bias, sink = _swa_group(g, q_ref, bias_ref, sink_ref)
                dy = jnp.concatenate([dy_ref[:, h * HEAD_DIM:(h + 1) * HEAD_DIM] for h in range(g * group, (g + 1) * group)], axis=0)
                p, p_sink = _swa_probs(q, k_win, bias, sink, valid)
                dp = _dot(dy, v_win, _NT)
                delta = jnp.sum(p * dp, axis=-1, keepdims=True)
                ds = p * (dp - delta)
                dsink = -p_sink * delta
                dq = _dot(ds, k_win) * ATT_SCALE
                for hh in range(group):
                    h = g * group + hh
                    rows = slice(hh * CHUNK, (hh + 1) * CHUNK)
                    dbias_ref[h] += ds[rows]
                    dsink_ref[h:h + 1, :] += jnp.sum(dsink[rows], axis=0, keepdims=True) * jnp.ones((1, HEAD_DIM), F32)
                    dq_ref[:, h * HEAD_DIM:(h + 1) * HEAD_DIM] = dq[rows]
                dk_win = _dot(ds, q, _TN) * ATT_SCALE
                dv_win = _dot(p, dy, _TN)
                for win, out_ref, col0 in ((dk_win, dk_ref, 0), (dv_win, dv_ref, kvw)):
                    cs = slice(col0 + g * HEAD_DIM, col0 + (g + 1) * HEAD_DIM)
                    out_ref[:, gs] = acc_a[:, cs] + win[:CHUNK]
                    acc_a[:, cs] = acc_b[:, cs] + win[CHUNK:2 * CHUNK]
                    acc_b[:, cs] = win[2 * CHUNK:]

        @pl.when(i == nb)
        def _():
            dk_ref[...] = acc_a[:, :kvw]
            dv_ref[...] = acc_a[:, kvw:]

    clamp = lambda i: jnp.minimum(i, nb - 1)
    late = lambda i: (jnp.maximum(i - 1, 0), 0)
    return pl.pallas_call(
        body, name="swa_bwd", grid=(nb + 1,),
        in_specs=[pl.BlockSpec((CHUNK, D_MODEL), lambda i: (clamp(i), 0))]
        + _swa_window_specs(nb, kvw, 0, clamp) + _swa_window_specs(nb, kvw, 17, clamp)
        + [pl.BlockSpec((CHUNK, D_MODEL), lambda i: (clamp(i), 1)),
           pl.BlockSpec((SWA_HEADS, CHUNK, 3 * CHUNK), lambda i: (0, 0, 0)), pl.BlockSpec((SWA_HEADS, HEAD_DIM), lambda i: (0, 0))],
        out_specs=[pl.BlockSpec((CHUNK, D_MODEL), lambda i: (clamp(i), 0)),
                   pl.BlockSpec((CHUNK, kvw), late), pl.BlockSpec((CHUNK, kvw), late),
                   pl.BlockSpec((SWA_HEADS, CHUNK, 3 * CHUNK), lambda i: (0, 0, 0)), pl.BlockSpec((SWA_HEADS, HEAD_DIM), lambda i: (0, 0))],
        out_shape=[_sds((T, D_MODEL), F32), _sds((T, kvw), F32), _sds((T, kvw), F32),
                   _sds((SWA_HEADS, CHUNK, 3 * CHUNK), F32), _sds((SWA_HEADS, HEAD_DIM), F32)],
        scratch_shapes=[pltpu.VMEM((CHUNK, 2 * kvw), F32), pltpu.VMEM((CHUNK, 2 * kvw), F32)],
        compiler_params=_params("arbitrary"),
    )(qn, kn, kn, kn, proj, proj, proj, dycat, bias, sink)


def _t5_bucket_reduce(dbias, bucket):
    def body(db_ref, bk_ref, o_ref):
        bk = bk_ref[...]
        row = lax.broadcasted_iota(jnp.int32, (SWA_HEADS, HEAD_DIM), 0)
        lane = lax.broadcasted_iota(jnp.int32, (SWA_HEADS, HEAD_DIM), 1)

        def per_bucket(b, acc):
            mask = bk == b
            for h in range(SWA_HEADS):
                tot = jnp.sum(jnp.sum(jnp.where(mask, db_ref[h], 0.0), axis=0, keepdims=True), axis=1, keepdims=True)
                acc = acc + jnp.where(jnp.logical_and(row == h, lane == b), tot, 0.0)
            return acc

        o_ref[...] = lax.fori_loop(0, T5_BUCKETS, per_bucket, jnp.zeros((SWA_HEADS, HEAD_DIM), F32))

    return pl.pallas_call(body, name="t5_bucket_reduce", out_shape=_sds((SWA_HEADS, HEAD_DIM), F32),
                          compiler_params=pltpu.CompilerParams(vmem_limit_bytes=VMEM_LIMIT_BYTES))(dbias, bucket)


def _headnorm_bwd(x, dy, gain):
    r = lax.rsqrt(jnp.mean(x * x, axis=-1, keepdims=True) + EPS)
    xhat = x * r
    dyg = dy * gain
    return r * (dyg - xhat * jnp.mean(dyg * xhat, axis=-1, keepdims=True)), dy * xhat


def _post_even(proj, dqr, dkr, dva, dga, dqn, dkn, dvb, cos, sin, q_gain, k_gain, *, tm):
    T = proj.shape[0]
    tm = min(tm, T)
    kvw = SWA_KV_HEADS * HEAD_DIM

    def body(qb_ref, kb_ref, dqr_ref, dkr_ref, dva_ref, dga_ref, dqn_ref, dkn_ref, dvb_ref, c_ref, s_ref, qg_ref, kg_ref,
             dp_ref, dqg_ref, dkg_ref):
        @pl.when(pl.program_id(0) == 0)
        def _():
            dqg_ref[...] = jnp.zeros_like(dqg_ref)
            dkg_ref[...] = jnp.zeros_like(dkg_ref)
        c = jnp.concatenate([c_ref[...]] * RET_HEADS, axis=1)
        s = jnp.concatenate([s_ref[...]] * RET_HEADS, axis=1)
        dq = dqr_ref[...]
        dp_ref[:, 0:RET_Q] = (dq * c + _swap_halves(dq * s, RET_DK // 2)).astype(dp_ref.dtype)
        dk = dkr_ref[...] * (RET_DK ** -0.5)
        dp_ref[:, RET_Q:2 * RET_Q] = (dk * c + _swap_halves(dk * s, RET_DK // 2)).astype(dp_ref.dtype)
        off = 2 * RET_Q
        dp_ref[:, off:off + RET_V] = dva_ref[...].astype(dp_ref.dtype)
        dp_ref[:, off + RET_V:off + 2 * RET_V] = dga_ref[...].astype(dp_ref.dtype)
        off += 2 * RET_V
        for src, dsrc, gain, dgain, heads, base in ((qb_ref, dqn_ref, qg_ref, dqg_ref, SWA_HEADS, off),
                                                    (kb_ref, dkn_ref, kg_ref, dkg_ref, SWA_KV_HEADS, off + D_MODEL)):
            for h in range(heads):
                sl = slice(h * HEAD_DIM, (h + 1) * HEAD_DIM)
                dx, dgx = _headnorm_bwd(src[:, sl], dsrc[:, sl], gain[...])
                dp_ref[:, base + h * HEAD_DIM:base + (h + 1) * HEAD_DIM] = dx.astype(dp_ref.dtype)
                dgain[...] += _rowsum8(dgx)
        dp_ref[:, off + D_MODEL + kvw:] = dvb_ref[...].astype(dp_ref.dtype)

    row = lambda i: (i, 0)
    const = lambda i: (0, 0)
    return pl.pallas_call(
        body, name="post_even", grid=(T // tm,),
        in_specs=[pl.BlockSpec((tm, D_MODEL), lambda i: (i, 3)), pl.BlockSpec((tm, kvw), lambda i: (i, 16)),
                  pl.BlockSpec((tm, RET_Q), row), pl.BlockSpec((tm, RET_Q), row),
                  pl.BlockSpec((tm, RET_V), row), pl.BlockSpec((tm, RET_V), row),
                  pl.BlockSpec((tm, D_MODEL), row), pl.BlockSpec((tm, kvw), row), pl.BlockSpec((tm, kvw), row),
                  pl.BlockSpec((tm, RET_DK), row), pl.BlockSpec((tm, RET_DK), row),
                  pl.BlockSpec((1, HEAD_DIM), const), pl.BlockSpec((1, HEAD_DIM), const)],
        out_specs=[pl.BlockSpec((tm, EVEN_IN), row), pl.BlockSpec((8, HEAD_DIM), const), pl.BlockSpec((8, HEAD_DIM), const)],
        out_shape=[_sds((T, EVEN_IN), _MXU), _sds((8, HEAD_DIM), F32), _sds((8, HEAD_DIM), F32)],
        compiler_params=_params("arbitrary"),
    )(proj, proj, dqr, dkr, dva, dga, dqn, dkn, dvb, cos, sin, q_gain, k_gain)


def _prep_odd(proj, cos, sin, q_gain, k_gain, *, tm):
    T = proj.shape[0]
    tm = min(tm, T)
    kvw = AX_KV_HEADS * HEAD_DIM

    def body(q_ref, k_ref, v_ref, c_ref, s_ref, qg_ref, kg_ref, qx_ref, kx_ref, vx_ref):
        c, s = c_ref[...], s_ref[...]
        for src, gain, dst, heads in ((q_ref, qg_ref, qx_ref, AX_HEADS), (k_ref, kg_ref, kx_ref, AX_KV_HEADS)):
            for h in range(heads):
                sl = slice(h * HEAD_DIM, (h + 1) * HEAD_DIM)
                xh = src[:, sl]
                r = lax.rsqrt(jnp.mean(xh * xh, axis=-1, keepdims=True) + EPS)
                xn = xh * r * gain[...]
                dst[:, sl] = (xn * c + _swap_halves(xn, HEAD_DIM // 4) * s).astype(dst.dtype)
        vx_ref[...] = v_ref[...].astype(vx_ref.dtype)

    row = lambda i: (i, 0)
    const = lambda i: (0, 0)
    return pl.pallas_call(
        body, name="prep_odd", grid=(T // tm,),
        in_specs=[pl.BlockSpec((tm, D_MODEL), row), pl.BlockSpec((tm, kvw), lambda i: (i, 4)), pl.BlockSpec((tm, kvw), lambda i: (i, 5)),
                  pl.BlockSpec((tm, HEAD_DIM), row), pl.BlockSpec((tm, HEAD_DIM), row),
                  pl.BlockSpec((1, HEAD_DIM), const), pl.BlockSpec((1, HEAD_DIM), const)],
        out_specs=[pl.BlockSpec((tm, D_MODEL), row), pl.BlockSpec((tm, kvw), row), pl.BlockSpec((tm, kvw), row)],
        out_shape=[_sds((T, D_MODEL), _MXU), _sds((T, kvw), _MXU), _sds((T, kvw), _MXU)],
        compiler_params=_params("parallel"),
    )(proj, proj, proj, cos, sin, q_gain, k_gain)


def _post_odd(proj, dqxt, dkx, dvx, cos, sin, q_gain, k_gain, *, tm):
    T = proj.shape[0]
    tm = min(tm, T)
    kvw = AX_KV_HEADS * HEAD_DIM

    def body(q_ref, k_ref, dqt_ref, dk_ref, dv_ref, c_ref, s_ref, qg_ref, kg_ref, dp_ref, dqg_ref, dkg_ref):
        @pl.when(pl.program_id(0) == 0)
        def _():
            dqg_ref[...] = jnp.zeros_like(dqg_ref)
            dkg_ref[...] = jnp.zeros_like(dkg_ref)
        c, s = c_ref[...], s_ref[...]
        for src, dsrc, gain, dgain, heads, base in ((q_ref, dqt_ref, qg_ref, dqg_ref, AX_HEADS, 0),
                                                    (k_ref, dk_ref, kg_ref, dkg_ref, AX_KV_HEADS, D_MODEL)):
            for h in range(heads):
                sl = slice(h * HEAD_DIM, (h + 1) * HEAD_DIM)
                d = dsrc[sl, :].T if dsrc is dqt_ref else dsrc[:, sl]
                dn = d * c + _swap_halves(d * s, HEAD_DIM // 4)
                dx, dgx = _headnorm_bwd(src[:, sl], dn, gain[...])
                dp_ref[:, base + h * HEAD_DIM:base + (h + 1) * HEAD_DIM] = dx.astype(dp_ref.dtype)
                dgain[...] += _rowsum8(dgx)
        dp_ref[:, D_MODEL + kvw:] = dv_ref[...].astype(dp_ref.dtype)

    row = lambda i: (i, 0)
    const = lambda i: (0, 0)
    return pl.pallas_call(
        body, name="post_odd", grid=(T // tm,),
        in_specs=[pl.BlockSpec((tm, D_MODEL), row), pl.BlockSpec((tm, kvw), lambda i: (i, 4)),
                  pl.BlockSpec((D_MODEL, tm), lambda i: (0, i)), pl.BlockSpec((tm, kvw), row), pl.BlockSpec((tm, kvw), row),
                  pl.BlockSpec((tm, HEAD_DIM), row), pl.BlockSpec((tm, HEAD_DIM), row),
                  pl.BlockSpec((1, HEAD_DIM), const), pl.BlockSpec((1, HEAD_DIM), const)],
        out_specs=[pl.BlockSpec((tm, ODD_IN), row), pl.BlockSpec((8, HEAD_DIM), const), pl.BlockSpec((8, HEAD_DIM), const)],
        out_shape=[_sds((T, ODD_IN), _MXU), _sds((8, HEAD_DIM), F32), _sds((8, HEAD_DIM), F32)],
        compiler_params=_params("arbitrary"),
    )(proj, proj, dqxt, dkx, dvx, cos, sin, q_gain, k_gain)


SCORE_SCALE_LOG2 = ATT_SCALE * math.log2(math.e)


def _flash_fwd(qx, kx, vx, *, tq, tk):
    T = qx.shape[0]
    tq, tk = min(tq, T), min(tk, T)
    nq, nk = T // tq, T // tk
    group = AX_HEADS // AX_KV_HEADS

    def body(k_ref, v_ref, q_ref, o_ref, lse_ref, acc_sc, m_sc, l_sc):
        j = pl.program_id(2)

        @pl.when(j == 0)
        def _():
            m_sc[...] = jnp.full(m_sc.shape, NEG_INF, F32)
            l_sc[...] = jnp.zeros_like(l_sc)
            acc_sc[...] = jnp.zeros_like(acc_sc)
        k, v = k_ref[...], v_ref[...]

        def step(i, carry):
            cols = pl.ds(pl.multiple_of(i * tq, tq), tq)
            st = _dot(k, q_ref[cols, :], _NT) * SCORE_SCALE_LOG2
            m_old = m_sc[i]
            m_new = jnp.maximum(m_old, jnp.max(st, axis=0, keepdims=True))
            p = jnp.exp2(st - m_new)
            alpha = jnp.exp2(m_old - m_new)
            m_sc[i] = m_new
            l_sc[i] = alpha * l_sc[i] + jnp.sum(p, axis=0, keepdims=True)
            acc_sc[:, cols] = alpha * acc_sc[:, cols] + _dot(v, p, _TN)
            return carry

        lax.fori_loop(0, nq, step, 0)

        @pl.when(j == nk - 1)
        def _():
            def finish(i, carry):
                cols = pl.ds(pl.multiple_of(i * tq, tq), tq)
                o_ref[cols, :] = (acc_sc[:, cols] / l_sc[i]).T.astype(o_ref.dtype)
                lse_ref[0, i] = m_sc[i] + jnp.log2(l_sc[i])
                return carry

            lax.fori_loop(0, nq, finish, 0)

    kv = lambda g, h, j: (j, g)
    qh = lambda g, h, j: (0, g * group + h)
    o, lse = pl.pallas_call(
        body, name="flash_fwd", grid=(AX_KV_HEADS, group, nk),
        in_specs=[pl.BlockSpec((tk, HEAD_DIM), kv), pl.BlockSpec((tk, HEAD_DIM), kv), pl.BlockSpec((T, HEAD_DIM), qh)],
        out_specs=[pl.BlockSpec((T, HEAD_DIM), qh), pl.BlockSpec((1, nq, 1, tq), lambda g, h, j: (g * group + h, 0, 0, 0))],
        out_shape=[_sds((T, D_MODEL), _MXU), _sds((AX_HEADS, nq, 1, tq), F32)],
        scratch_shapes=[pltpu.VMEM((HEAD_DIM, T), F32), pltpu.VMEM((nq, 1, tq), F32), pltpu.VMEM((nq, 1, tq), F32)],
        compiler_params=_params("parallel", "arbitrary", "arbitrary"),
    )(kx, vx, qx)
    return o, lse.reshape(AX_HEADS, 1, T)


def _flash_delta(o, do, *, tq):
    T = o.shape[0]
    tq = min(tq, T)

    def body(o_ref, do_ref, delta_ref):
        delta_ref[0] = jnp.sum(do_ref[...].astype(F32) * o_ref[...].astype(F32), axis=-1, keepdims=True)

    blk = lambda h, i: (i, h)
    return pl.pallas_call(
        body, name="flash_delta", grid=(AX_HEADS, T // tq),
        in_specs=[pl.BlockSpec((tq, HEAD_DIM), blk), pl.BlockSpec((tq, HEAD_DIM), blk)],
        out_specs=pl.BlockSpec((1, tq, 1), lambda h, i: (h, i, 0)),
        out_shape=_sds((AX_HEADS, T, 1), F32),
        compiler_params=_params("parallel", "parallel"),
    )(o, do)


def _flash_bwd(qx, kx, vx, do, lse, delta, *, tq, tk):
    T = qx.shape[0]
    tq, tk = min(tq, T), min(tk, T)
    nq = T // tq
    group = AX_HEADS // AX_KV_HEADS
    lse_rows = lse.reshape(AX_HEADS, nq, 1, tq)
    delta_rows = delta.reshape(AX_HEADS, nq, 1, tq)

    def body(k_ref, v_ref, q_ref, do_ref, lse_ref, delta_ref, dqt_ref, dk_ref, dv_ref):
        j = pl.program_id(2)

        @pl.when(jnp.logical_and(pl.program_id(1) == 0, j == 0))
        def _():
            dk_ref[...] = jnp.zeros_like(dk_ref)
            dv_ref[...] = jnp.zeros_like(dv_ref)

        @pl.when(j == 0)
        def _():
            dqt_ref[...] = jnp.zeros_like(dqt_ref)
        k, v = k_ref[...], v_ref[...]

        def step(i, carry):
            dk, dv = carry
            off = pl.multiple_of(i * tq, tq)
            q, do_blk = q_ref[pl.ds(off, tq), :], do_ref[pl.ds(off, tq), :]
            pt = jnp.exp2(_dot(k, q, _NT) * SCORE_SCALE_LOG2 - lse_ref[0, i])
            dst = pt * (_dot(v, do_blk, _NT) - delta_ref[0, i])
            dqt_ref[:, pl.ds(off, tq)] += _dot(k, dst, _TN) * ATT_SCALE
            return dk + _dot(dst, q), dv + _dot(pt, do_blk)

        zero = jnp.zeros((tk, HEAD_DIM), F32)
        dk, dv = lax.fori_loop(0, nq, step, (zero, zero))
        rows = pl.ds(pl.multiple_of(j * tk, tk), tk)
        dk_ref[rows, :] += dk * ATT_SCALE
        dv_ref[rows, :] += dv

    kv = lambda g, h, j: (j, g)
    qh = lambda g, h, j: (0, g * group + h)
    st = lambda g, h, j: (g * group + h, 0, 0, 0)
    acc = lambda g, h, j: (0, g)
    return pl.pallas_call(
        body, name="flash_bwd", grid=(AX_KV_HEADS, group, T // tk),
        in_specs=[pl.BlockSpec((tk, HEAD_DIM), kv), pl.BlockSpec((tk, HEAD_DIM), kv),
                  pl.BlockSpec((T, HEAD_DIM), qh), pl.BlockSpec((T, HEAD_DIM), qh),
                  pl.BlockSpec((1, nq, 1, tq), st), pl.BlockSpec((1, nq, 1, tq), st)],
        out_specs=[pl.BlockSpec((HEAD_DIM, T), lambda g, h, j: (g * group + h, 0)),
                   pl.BlockSpec((T, HEAD_DIM), acc), pl.BlockSpec((T, HEAD_DIM), acc)],
        out_shape=[_sds((D_MODEL, T), F32), _sds((T, AX_KV_HEADS * HEAD_DIM), F32), _sds((T, AX_KV_HEADS * HEAD_DIM), F32)],
        compiler_params=_params("parallel", "arbitrary", "arbitrary"),
    )(kx, vx, qx, do, lse_rows, delta_rows)


TM = 1024
TM_WIDE = 512


def _mlp_fwd(tag, x, gain, w_up, w_down, target=None):
    u, h = _norm_matmul(f"mlp_up{tag}", x, gain, w_up, tm=TM, tn=1024, out_dtype=F32)
    out = _matmul_res(f"mlp_down{tag}", [u], w_down, x, tm=TM_WIDE, relu2=True, target=target)
    return out, (x, u, h)


def _local_step(x, target, p, w_first, fetch_rest, push, tokens=()):
    T = x.shape[0]
    cos_r, sin_r = _ret_rope_tables(T)
    cos_a, sin_a = _axial_rope_tables(T)
    tabs, rw, log_gamma = _retention_tables(p["ret_decay_logit"][0])
    bias = _swa_bias(p["t5_table"])
    sink = p["swa_sink"][0][:, None] * jnp.ones((1, HEAD_DIM), F32)
    nm, nl = p["norm_mix"], p["norm_mlp"]
    pending = [t for t in tokens if t is not None]

    def send(tag, weight, dw):
        token = push(tag, weight, dw[None])
        if token is not None:
            pending.append(token)

    def tied(operand):
        while pending:
            operand = operand + pending.pop()[0:1, 0:1]
        return operand

    def mlp_bwd(tag, saved, gain, w_up, w_down, dy):
        xs, u, h = saved
        du = _matmul_nt(f"mlp_down{tag}_bwd", dy, w_down, tm=TM, tn=1024, out_dtype=_MXU, relu_of=u)
        send(f"mlp_down{tag}", "w_mlp_down", _matmul_tn(f"mlp_down{tag}_dw", u, dy, tk=1024, tn=1024, tt=1024, out_dtype=_WIRE, relu2=True))
        dx, dgain = _matmul_nt_normbwd(f"mlp_up{tag}_bwd", du, w_up, xs, tied(gain), dy, tm=TM_WIDE)
        send(f"mlp_up{tag}", "w_mlp_up", _matmul_tn(f"mlp_up{tag}_dw", h, du, tk=1024, tn=1024, tt=1024, out_dtype=_WIRE))
        return dx, dgain

    proj0, h0 = _norm_matmul("in_even", x, tied(nm[0:1]), w_first["w_in_even"], tm=TM, tn=1152, out_dtype=F32)
    qr, kr, qn, kn = _prep_even(proj0, cos_r, sin_r, p["swa_q_norm"], p["swa_k_norm"], tm=TM)
    sf, sb = _ret_scan("ret_scan_fwd", kr, proj0, 1, rw["cf"], rw["dec_f"], rw["cb"], rw["dec_b"])
    ret_o, ya = _ret_out(qr, kr, proj0, sf, sb, tabs[:, (TAB_D, TAB_A, TAB_B)], p["ret_norm"])
    yb = _swa_fwd(qn, kn, proj0, bias, sink)
    wf = {**w_first, **fetch_rest(yb)}
    x1 = _matmul_res("out_even", [ya, yb], wf["w_out_even"], x, tm=TM)
    x2, mlp0 = _mlp_fwd(0, x1, nl[0:1], wf["w_mlp_up"][0], wf["w_mlp_down"][0])
    proj1, h1 = _norm_matmul("in_odd", x2, nm[1:2], wf["w_in_odd"], tm=TM, tn=768, out_dtype=F32)
    qx, kx, vx = _prep_odd(proj1, cos_a, sin_a, p["ax_q_norm"], p["ax_k_norm"], tm=TM)
    o, lse = _flash_fwd(qx, kx, vx, tq=2048, tk=1024)
    x3 = _matmul_res("out_odd", [o], wf["w_out_odd"], x2, tm=TM)
    (g4, loss_part), mlp1 = _mlp_fwd(1, x3, nl[1:2], wf["w_mlp_up"][1], wf["w_mlp_down"][1], target=target)

    dx3, dnl1 = mlp_bwd(1, mlp1, nl[1:2], wf["w_mlp_up"][1], wf["w_mlp_down"][1], g4)
    do = _matmul_nt("out_odd_bwd", dx3, wf["w_out_odd"], tm=TM, tn=1024, out_dtype=_MXU)
    send("out_odd", "w_out_odd", _matmul_tn("out_odd_dw", o, dx3, tk=1024, tn=1024, tt=1024, out_dtype=_WIRE))
    delta = _flash_delta(o, do, tq=1024)
    dqxt, dkx, dvx = _flash_bwd(qx, kx, vx, do, lse, delta, tq=1024, tk=512)
    dproj1, dqg1, dkg1 = _post_odd(proj1, dqxt, dkx, dvx, cos_a, sin_a, tied(p["ax_q_norm"]), p["ax_k_norm"], tm=TM)
    send("in_odd", "w_in_odd", _matmul_tn("in_odd_dw", h1, dproj1, tk=1024, tn=768, tt=1024, out_dtype=_WIRE))
    dx2, dnm1 = _matmul_nt_normbwd("in_odd_bwd", dproj1, wf["w_in_odd"], x2, tied(nm[1:2]), dx3, tm=TM_WIDE)
    dx1, dnl0 = mlp_bwd(0, mlp0, nl[0:1], wf["w_mlp_up"][0], wf["w_mlp_down"][0], dx2)
    dycat = _matmul_nt("out_even_bwd", dx1, wf["w_out_even"], tm=TM, tn=1024, out_dtype=F32)
    send("out_even", "w_out_even", jnp.concatenate([
        _matmul_tn("out_even_dw_ret", ya, dx1, tk=1024, tn=1024, tt=1024, out_dtype=_WIRE),
        _matmul_tn("out_even_dw_swa", yb, dx1, tk=1024, tn=1024, tt=1024, out_dtype=_WIRE)], axis=0))
    g_out, dga, dretg = _ret_gate_bwd(dycat, proj0, ret_o, tied(p["ret_norm"]), tm=TM)
    rb, rf = _ret_scan("ret_scan_bwd", qr, g_out, 0, rw["b"], rw["dec_b"], rw["a"], rw["dec_f"])
    dqr, dkr, dva, dlog = _ret_bwd(qr, kr, proj0, g_out, sf, sb, rf, rb, tabs)
    dqn, dkn, dvb, dbias, dsink = _swa_bwd(qn, kn, proj0, dycat, bias, sink)
    dt5 = _t5_bucket_reduce(dbias, _t5_bucket(_swa_rel()).astype(jnp.int32))
    dproj0, dqg0, dkg0 = _post_even(proj0, dqr, dkr, dva, dga, dqn, dkn, dvb, cos_r, sin_r,
                                    p["swa_q_norm"], p["swa_k_norm"], tm=TM_WIDE)
    send("in_even", "w_in_even", _matmul_tn("in_even_dw", h0, dproj0, tk=1024, tn=1152, tt=1024, out_dtype=_WIRE))
    dx0, dnm0 = _matmul_nt_normbwd("in_even_bwd", dproj0, w_first["w_in_even"], x, tied(nm[0:1]), dx1, tm=TM_WIDE)

    fold = lambda part: jnp.sum(part, axis=0)
    dlam = jnp.sum(dlog, axis=1).reshape(RET_HEADS, 2).T
    small = {
        "norm_mix": jnp.stack([fold(dnm0), fold(dnm1)]),
        "norm_mlp": jnp.stack([fold(dnl0), fold(dnl1)]),
        "ret_decay_logit": (dlam * (1.0 - jnp.exp(log_gamma)))[None],
        "ret_norm": fold(dretg)[None],
        "swa_q_norm": fold(dqg0)[None], "swa_k_norm": fold(dkg0)[None],
        "swa_sink": dsink[:, 0][None],
        "t5_table": dt5[:, :T5_BUCKETS].T,
        "ax_q_norm": fold(dqg1)[None], "ax_k_norm": fold(dkg1)[None],
    }
    return loss_part, dx0, small


BIG = ("w_in_even", "w_out_even", "w_in_odd", "w_out_odd", "w_mlp_up", "w_mlp_down")
SMALL = ("norm_mix", "norm_mlp", "ret_decay_logit", "ret_norm", "swa_q_norm", "swa_k_norm", "swa_sink", "t5_table",
         "ax_q_norm", "ax_k_norm")
WEIGHTS = ("norm_mix", "norm_mlp", "w_in_even", "w_out_even", "ret_decay_logit", "ret_norm", "swa_q_norm", "swa_k_norm",
           "swa_sink", "t5_table", "w_in_odd", "w_out_odd", "ax_q_norm", "ax_k_norm", "w_mlp_up", "w_mlp_down")
SHARD_AXIS = {"w_in_even": 2, "w_out_even": 1, "w_in_odd": 2, "w_out_odd": 1, "w_mlp_up": 2, "w_mlp_down": 1}
N_CHIPS = 4
ANY = pl.BlockSpec(memory_space=pl.ANY)
HBM = pl.BlockSpec(memory_space=pltpu.HBM)
SEM = pl.BlockSpec(memory_space=pltpu.SEMAPHORE)
SPLIT_COPY = pltpu.CompilerParams(has_side_effects=pltpu.SideEffectType.DATAFLOW_SIDE_EFFECTING)


def _in_hbm(a):
    return pltpu.with_memory_space_constraint(a, pltpu.HBM)


def _mesh_pos():
    return lax.axis_index("x"), lax.axis_index("y"), lax.axis_index("c")


def _window(ref, axis, start, size):
    idx = [slice(None)] * len(ref.shape)
    idx[axis] = pl.ds(start, size)
    return ref.at[tuple(idx)]


def _cast_place(key, shard, chip, *, tr=256):
    L, R, C = shard.shape
    tr = min(tr, R)
    axis = SHARD_AXIS[key]
    whole = tuple(d * (N_CHIPS if a == axis else 1) for a, d in enumerate(shard.shape))

    def body(chip_ref, s_ref, o_ref):
        o_ref[...] = s_ref[...].astype(o_ref.dtype)

    if axis == 2:
        out_map = lambda l, i, chip_ref: (l, i, chip_ref[0])
    else:
        out_map = lambda l, i, chip_ref: (l, i + chip_ref[0] * (R // tr), 0)
    grid_spec = pltpu.PrefetchScalarGridSpec(
        num_scalar_prefetch=1, grid=(L, R // tr),
        in_specs=[pl.BlockSpec((1, tr, C), lambda l, i, chip_ref: (l, i, 0))],
        out_specs=pl.BlockSpec((1, tr, C), out_map))
    return pl.pallas_call(body, name=f"cast_place_{key}", grid_spec=grid_spec, out_shape=_sds(whole, _MXU),
                          compiler_params=_params("parallel", "parallel"))(chip, shard)


def _gather_copies(names, refs, send_sems, recv_sems, *, outgoing=True, incoming=True):
    x, y, c = _mesh_pos()
    chips = [(1 - x, y), (x, 1 - y), (1 - x, 1 - y)]
    out, inc = [], []
    for t, key in enumerate(names):
        size = refs[t].shape[SHARD_AXIS[key]] // N_CHIPS
        slot = lambda px, py: _window(refs[t], SHARD_AXIS[key], pl.multiple_of((2 * px + py) * size, 128), size)
        for k, (px, py) in enumerate(chips):
            sems = dict(send_sem=send_sems.at[3 * t + k], recv_sem=recv_sems.at[3 * t + k], device_id=(px, py, c), device_id_type=MESH)
            if outgoing:
                out.append(pltpu.make_async_remote_copy(slot(x, y), slot(x, y), **sems))
            if incoming:
                inc.append(pltpu.make_async_remote_copy(slot(x, y), slot(px, py), **sems))
    return out, inc


def _allgather_start(groups):
    names = [list(g) for g in groups]
    flat = [g[k] for g in groups for k in g]
    n, ng = len(flat), len(groups)

    def body(*refs):
        start = 0
        for gi, keys in enumerate(names):
            copies, _ = _gather_copies(keys, refs[start:start + len(keys)], refs[n + 2 * gi], refs[n + 2 * gi + 1], incoming=False)
            for cp in copies:
                cp.start()
            start += len(keys)
        token = refs[-1]
        token[...] = jnp.zeros_like(token)

    sem_shapes = [pltpu.SemaphoreType.DMA((3 * len(keys),)) for keys in names for _ in (0, 1)]
    outs = pl.pallas_call(
        body, name="allgather_start", in_specs=[HBM] * n,
        out_specs=[SEM] * (2 * ng) + [HBM] * n + [pl.BlockSpec(memory_space=pltpu.VMEM)],
        out_shape=sem_shapes + [pltpu.HBM(a.shape, a.dtype) for a in flat] + [_sds((8, HEAD_DIM), F32)],
        input_output_aliases={t: 2 * ng + t for t in range(n)},
        compiler_params=SPLIT_COPY,
    )(*[_in_hbm(a) for a in flat])
    states, start = [], 2 * ng
    for gi, keys in enumerate(names):
        states.append((gi, keys, outs[2 * gi], outs[2 * gi + 1], outs[start:start + len(keys)]))
        start += len(keys)
    return states, outs[-1]


def _allgather_wait(state, after):
    gi, names, send_sems, recv_sems, thru = state
    n = len(names)

    def body(*refs):
        outgoing, incoming = _gather_copies(names, refs[:n], refs[n], refs[n + 1])
        for cp in outgoing:
            cp.wait_send()
        for cp in incoming:
            cp.wait_recv()

    outs = pl.pallas_call(
        body, name=f"allgather_wait_{gi}", in_specs=[HBM] * n + [SEM, SEM, ANY], out_specs=[HBM] * n,
        out_shape=[pltpu.HBM(t.shape, t.dtype) for t in thru],
        input_output_aliases={t: t for t in range(n)},
        compiler_params=SPLIT_COPY,
    )(*thru, send_sems, recv_sems, after)
    return dict(zip(names, outs))


FLIPS = [(a, b, d) for a in (0, 1) for b in (0, 1) for d in (0, 1) if (a, b, d) != (0, 0, 0)]


def _flip(pos, f):
    return tuple(1 - p if fi else p for p, fi in zip(pos, f))


def _piece_shape(weight, shape):
    out = list(shape)
    out[SHARD_AXIS[weight]] //= N_CHIPS
    out[1] //= 2
    return tuple(out)


def _piece(ref, weight, chip, core):
    piece = _piece_shape(weight, ref.shape)
    if SHARD_AXIS[weight] == 1:
        return _window(ref, 1, pl.multiple_of((2 * chip + core) * piece[1], 8), piece[1])
    return _window(_window(ref, 2, pl.multiple_of(chip * piece[2], 128), piece[2]), 1, pl.multiple_of(core * piece[1], 8), piece[1])


def _scatter_copies(weight, grad_ref, land_ref, send_sems, recv_sems, *, outgoing=True, incoming=True):
    pos = _mesh_pos()
    out, inc = [], []
    for k, f in enumerate(FLIPS):
        peer = _flip(pos, f)
        sems = dict(send_sem=send_sems.at[k], recv_sem=recv_sems.at[k], device_id=peer, device_id_type=MESH)
        if outgoing:
            out.append(pltpu.make_async_remote_copy(_piece(grad_ref, weight, 2 * peer[0] + peer[1], peer[2]), land_ref.at[k], **sems))
        if incoming:
            inc.append(pltpu.make_async_remote_copy(_piece(grad_ref, weight, 2 * pos[0] + pos[1], pos[2]), land_ref.at[k], **sems))
    return out, inc


def _scatter_start(tag, weight, grad):
    n_peer = len(FLIPS)
    land = lax.empty((n_peer,) + _piece_shape(weight, grad.shape), grad.dtype)

    def body(grad_ref, land_ref, send_sems, recv_sems, grad_thru, land_thru, token):
        copies, _ = _scatter_copies(weight, grad_ref, land_ref, send_sems, recv_sems, incoming=False)
        for cp in copies:
            cp.start()
        token[...] = jnp.zeros_like(token)

    outs = pl.pallas_call(
        body, name=f"scatter_start_{tag}", in_specs=[HBM, HBM],
        out_specs=[SEM, SEM, HBM, HBM, pl.BlockSpec(memory_space=pltpu.VMEM)],
        out_shape=[pltpu.SemaphoreType.DMA((n_peer,)), pltpu.SemaphoreType.DMA((n_peer,)),
                   pltpu.HBM(grad.shape, grad.dtype), pltpu.HBM(land.shape, land.dtype), _sds((8, HEAD_DIM), F32)],
        input_output_aliases={0: 2, 1: 3},
        compiler_params=SPLIT_COPY,
    )(_in_hbm(grad), _in_hbm(land))
    return (tag, weight, outs[:4]), outs[4]


def _scatter_wait(state, after):
    tag, weight, (send_sems, recv_sems, grad_thru, land_thru) = state

    def body(grad_ref, land_ref, send_ref, recv_ref, after_ref, grad_out, land_out):
        outgoing, incoming = _scatter_copies(weight, grad_ref, land_ref, send_ref, recv_ref)
        for cp in outgoing:
            cp.wait_send()
        for cp in incoming:
            cp.wait_recv()

    return pl.pallas_call(
        body, name=f"scatter_wait_{tag}", in_specs=[HBM, HBM, SEM, SEM, ANY], out_specs=[HBM, HBM],
        out_shape=[pltpu.HBM(grad_thru.shape, grad_thru.dtype), pltpu.HBM(land_thru.shape, land_thru.dtype)],
        input_output_aliases={0: 0, 1: 1},
        compiler_params=SPLIT_COPY,
    )(grad_thru, land_thru, send_sems, recv_sems, after)


def _sum_pieces(tag, weight, grad, land, where, *, tr=128):
    _, R, C = _piece_shape(weight, grad.shape)
    tr = min(tr, R)
    nr = R // tr

    def body(where_ref, g_ref, l_ref, o_ref):
        acc = g_ref[...].astype(F32)
        for s in range(len(FLIPS)):
            acc = acc + l_ref[s].astype(F32)
        o_ref[...] = acc

    if SHARD_AXIS[weight] == 1:
        own = lambda i, where_ref: (0, (2 * where_ref[0] + where_ref[1]) * nr + i, 0)
    else:
        own = lambda i, where_ref: (0, where_ref[1] * nr + i, where_ref[0])
    grid_spec = pltpu.PrefetchScalarGridSpec(
        num_scalar_prefetch=1, grid=(nr,),
        in_specs=[pl.BlockSpec((1, tr, C), own), pl.BlockSpec((len(FLIPS), 1, tr, C), lambda i, where_ref: (0, 0, i, 0))],
        out_specs=pl.BlockSpec((1, tr, C), lambda i, where_ref: (0, where_ref[1] * nr + i, 0)))
    return pl.pallas_call(body, name=f"sum_{tag}", grid_spec=grid_spec, out_shape=_sds((1, 2 * R, C), F32),
                          compiler_params=_params("parallel"))(where, grad, land)


def _exchange_halves(shards):
    names = list(shards)
    n = len(names)
    half_sizes = [shards[k].shape[1] // 2 for k in names]

    def body(*refs):
        outs = refs[n:2 * n]
        send_sems, recv_sems = refs[2 * n:]
        x, y, c = _mesh_pos()
        half = lambda t, core: _window(outs[t], 1, pl.multiple_of(core * half_sizes[t], 8), half_sizes[t])
        sends = []
        for t in range(n):
            sends.append(pltpu.make_async_remote_copy(half(t, c), half(t, c), send_sems.at[t], recv_sems.at[t],
                                                      device_id=(x, y, 1 - c), device_id_type=MESH))
            sends[-1].start()
        for t in range(n):
            pltpu.make_async_remote_copy(half(t, c), half(t, 1 - c), send_sems.at[t], recv_sems.at[t],
                                         device_id=(x, y, 1 - c), device_id_type=MESH).wait_recv()
        for cp in sends:
            cp.wait_send()

    outs = pl.pallas_call(
        body, name="exchange_halves", in_specs=[ANY] * n, out_specs=[ANY] * n,
        out_shape=[_sds(shards[k].shape, F32) for k in names],
        input_output_aliases={t: t for t in range(n)},
        scratch_shapes=[pltpu.SemaphoreType.DMA((n,)), pltpu.SemaphoreType.DMA((n,))],
    )(*[shards[k] for k in names])
    return dict(zip(names, outs))


def _adamw_math(w, g, m, v):
    m = ADAM_B1 * m + (1.0 - ADAM_B1) * g
    v = ADAM_B2 * v + (1.0 - ADAM_B2) * jnp.square(g)
    m_hat = m / (1.0 - ADAM_B1 ** ADAM_STEP)
    v_hat = v / (1.0 - ADAM_B2 ** ADAM_STEP)
    return -ADAM_LR * (m_hat / (jnp.sqrt(v_hat) + ADAM_EPS) + ADAM_WD * w), m, v


def _adamw(name, w, g, m, v, *, tr=256):
    R, C = w.shape
    tr = min(tr, R)

    def body(w_ref, g_ref, m_ref, v_ref, d_ref, mo_ref, vo_ref):
        d_ref[...], mo_ref[...], vo_ref[...] = _adamw_math(w_ref[...], g_ref[...], m_ref[...], v_ref[...])

    spec = pl.BlockSpec((tr, C), lambda i: (i, 0))
    return pl.pallas_call(body, name=name, grid=(R // tr,), in_specs=[spec] * 4, out_specs=[spec] * 3,
                          out_shape=[_sds((R, C), F32)] * 3, compiler_params=_params("parallel"))(w, g, m, v)


SLAB_ROWS = 8
LOSS_ROW = 7


def _pack_small(d):
    pad = lambda a, width: jnp.pad(a.reshape(-1), (0, width - a.size))
    row5 = jnp.concatenate([d["swa_q_norm"].reshape(-1), d["swa_k_norm"].reshape(-1), d["ax_q_norm"].reshape(-1),
                            d["ax_k_norm"].reshape(-1), pad(d["swa_sink"], HEAD_DIM), pad(d["ret_decay_logit"], HEAD_DIM),
                            jnp.zeros((2 * HEAD_DIM,), F32)])
    return jnp.concatenate([d["norm_mix"], d["norm_mlp"], d["ret_norm"], row5[None], pad(d["t5_table"], D_MODEL)[None],
                            jnp.zeros((1, D_MODEL), F32)], axis=0)


def _unpack_small(slab):
    r5 = slab[5]
    return {
        "norm_mix": slab[0:2], "norm_mlp": slab[2:4], "ret_norm": slab[4:5],
        "swa_q_norm": r5[None, 0:128], "swa_k_norm": r5[None, 128:256], "ax_q_norm": r5[None, 256:384],
        "ax_k_norm": r5[None, 384:512], "swa_sink": r5[None, 512:512 + SWA_HEADS],
        "ret_decay_logit": r5[640:640 + 2 * RET_HEADS].reshape(1, 2, RET_HEADS),
        "t5_table": slab[6, :T5_BUCKETS * SWA_HEADS].reshape(T5_BUCKETS, SWA_HEADS),
    }


def _small_allreduce_adamw(g_slab, w_slab, m_slab, v_slab, loss_part):
    def body(g_ref, w_ref, m_ref, v_ref, lp_ref, go_ref, d_ref, mo_ref, vo_ref, gath, send_sems, recv_sems):
        pos = _mesh_pos()
        ident = lambda p: 4 * p[0] + 2 * p[1] + p[2]
        me = ident(pos)
        row = lax.broadcasted_iota(jnp.int32, (SLAB_ROWS, D_MODEL), 0)
        lane = lax.broadcasted_iota(jnp.int32, (SLAB_ROWS, D_MODEL), 1)
        loss = jnp.sum(jnp.sum(lp_ref[...], axis=0, keepdims=True), axis=1, keepdims=True) * (0.5 / D_MODEL)
        gath[me] = jnp.where(jnp.logical_and(row == LOSS_ROW, lane == 0), loss, g_ref[...])
        sends = []
        for k, f in enumerate(FLIPS):
            sends.append(pltpu.make_async_remote_copy(gath.at[me], gath.at[me], send_sems.at[k], recv_sems.at[k],
                                                      device_id=_flip(pos, f), device_id_type=MESH))
            sends[-1].start()
        for k, f in enumerate(FLIPS):
            peer = _flip(pos, f)
            pltpu.make_async_remote_copy(gath.at[me], gath.at[ident(peer)], send_sems.at[k], recv_sems.at[k],
                                         device_id=peer, device_id_type=MESH).wait_recv()
        for cp in sends:
            cp.wait_send()
        total = gath[0]
        for s in range(1, N_DEV):
            total = total + gath[s]
        go_ref[...] = total
        d_ref[...], mo_ref[...], vo_ref[...] = _adamw_math(w_ref[...], total, m_ref[...], v_ref[...])

    vmem = pl.BlockSpec(memory_space=pltpu.VMEM)
    return pl.pallas_call(
        body, name="small_allreduce_adamw", in_specs=[vmem] * 5, out_specs=[vmem] * 4,
        out_shape=[_sds((SLAB_ROWS, D_MODEL), F32)] * 4,
        scratch_shapes=[pltpu.VMEM((N_DEV, SLAB_ROWS, D_MODEL), F32),
                        pltpu.SemaphoreType.DMA((len(FLIPS),)), pltpu.SemaphoreType.DMA((len(FLIPS),))],
    )(g_slab, w_slab, m_slab, v_slab, loss_part)


def kernel(x, norm_mix, norm_mlp, w_in_even, w_out_even, ret_decay_logit, ret_norm, swa_q_norm, swa_k_norm, swa_sink, t5_table, w_in_odd, w_out_odd, ax_q_norm, ax_k_norm, w_mlp_up, w_mlp_down, loss_target, m_norm_mix, m_norm_mlp, m_w_in_even, m_w_out_even, m_ret_decay_logit, m_ret_norm, m_swa_q_norm, m_swa_k_norm, m_swa_sink, m_t5_table, m_w_in_odd, m_w_out_odd, m_ax_q_norm, m_ax_k_norm, m_w_mlp_up, m_w_mlp_down, v_norm_mix, v_norm_mlp, v_w_in_even, v_w_out_even, v_ret_decay_logit, v_ret_norm, v_swa_q_norm, v_swa_k_norm, v_swa_sink, v_t5_table, v_w_in_odd, v_w_out_odd, v_ax_q_norm, v_ax_k_norm, v_w_mlp_up, v_w_mlp_down):
    w = dict(zip(WEIGHTS, (norm_mix, norm_mlp, w_in_even, w_out_even, ret_decay_logit, ret_norm, swa_q_norm, swa_k_norm,
                           swa_sink, t5_table, w_in_odd, w_out_odd, ax_q_norm, ax_k_norm, w_mlp_up, w_mlp_down)))
    m = dict(zip(WEIGHTS, (m_norm_mix, m_norm_mlp, m_w_in_even, m_w_out_even, m_ret_decay_logit, m_ret_norm, m_swa_q_norm,
                           m_swa_k_norm, m_swa_sink, m_t5_table, m_w_in_odd, m_w_out_odd, m_ax_q_norm, m_ax_k_norm,
                           m_w_mlp_up, m_w_mlp_down)))
    v = dict(zip(WEIGHTS, (v_norm_mix, v_norm_mlp, v_w_in_even, v_w_out_even, v_ret_decay_logit, v_ret_norm, v_swa_q_norm,
                           v_swa_k_norm, v_swa_sink, v_t5_table, v_w_in_odd, v_w_out_odd, v_ax_q_norm, v_ax_k_norm,
                           v_w_mlp_up, v_w_mlp_down)))
    flat = lambda a: a.reshape(-1, a.shape[-1])

    chip = (2 * lax.axis_index("x") + lax.axis_index("y")).astype(jnp.int32)
    where = jnp.stack([chip, lax.axis_index("c").astype(jnp.int32)])

    placed = {k: _cast_place(k, w[k], where[0:1]) for k in BIG}
    (gather_first, gather_rest), gather_token = _allgather_start(
        [{"w_in_even": placed["w_in_even"]}, {k: placed[k] for k in BIG if k != "w_in_even"}])
    unstack = lambda whole: {k: (a if k.startswith("w_mlp") else a[0]) for k, a in whole.items()}

    in_flight = []

    def push(tag, weight, dw):
        state, token = _scatter_start(tag, weight, dw)
        in_flight.append(state)
        return token

    loss_part, dx, small_g = _local_step(x[0], loss_target[0], {k: w[k] for k in SMALL},
                                         unstack(_allgather_wait(gather_first, gather_token)),
                                         lambda after: unstack(_allgather_wait(gather_rest, after)), push)

    halves = {}
    for state in in_flight:
        tag, weight = state[0], state[1]
        dw, land = _scatter_wait(state, dx)
        halves[tag] = _sum_pieces(tag, weight, dw, land, where)
    reduced = _exchange_halves(halves)
    grad = {"w_in_even": reduced["in_even"], "w_out_even": reduced["out_even"],
            "w_in_odd": reduced["in_odd"], "w_out_odd": reduced["out_odd"],
            "w_mlp_up": jnp.concatenate([reduced["mlp_up0"], reduced["mlp_up1"]], axis=0),
            "w_mlp_down": jnp.concatenate([reduced["mlp_down0"], reduced["mlp_down1"]], axis=0)}
    delta, new_m, new_v = {}, {}, {}
    for k in BIG:
        d_k, m_k, v_k = _adamw(f"adamw_{k}", flat(w[k]), flat(grad[k]), flat(m[k]), flat(v[k]))
        delta[k], new_m[k], new_v[k] = d_k.reshape(w[k].shape), m_k.reshape(w[k].shape), v_k.reshape(w[k].shape)

    slabs = _small_allreduce_adamw(_pack_small(small_g), _pack_small({k: w[k] for k in SMALL}),
                                   _pack_small({k: m[k] for k in SMALL}), _pack_small({k: v[k] for k in SMALL}), loss_part)
    loss = slabs[0][LOSS_ROW, 0]
    for out, slab in zip((grad, delta, new_m, new_v), slabs):
        out.update(_unpack_small(slab))

    return (loss, dx[None], *[grad[k] for k in WEIGHTS], *[delta[k] for k in WEIGHTS],
            *[new_m[k] for k in WEIGHTS], *[new_v[k] for k in WEIGHTS])
```

```python
import functools
import math

import jax
import jax.numpy as jnp
from jax import lax
from jax.experimental import pallas as pl
from jax.experimental.pallas import tpu as pltpu

F32 = jnp.float32
BF16 = jnp.bfloat16
_MXU = BF16
_WIRE = BF16

D_MODEL = 1024
HEAD_DIM = 128
EPS = 1e-6
NEG_INF = -1e30
CHUNK = 128
GRID_W = 64
RET_HEADS, RET_DK, RET_DV = 4, 128, 256
RET_Q, RET_V = RET_HEADS * RET_DK, RET_HEADS * RET_DV
RET_THETA = 10000.0
SWA_HEADS, SWA_KV_HEADS = 8, 2
T5_BUCKETS, T5_MAX_DIST = 32, 128
AX_HEADS, AX_KV_HEADS = 8, 2
AX_THETA = 10000.0
D_FF = 4 * D_MODEL
EVEN_IN = 2 * RET_Q + 2 * RET_V + D_MODEL + 2 * SWA_KV_HEADS * HEAD_DIM
ODD_IN = D_MODEL + 2 * AX_KV_HEADS * HEAD_DIM
ATT_SCALE = HEAD_DIM ** -0.5

ADAM_LR, ADAM_B1, ADAM_B2, ADAM_EPS, ADAM_WD, ADAM_STEP = 0.001, 0.9, 0.999, 1e-08, 0.01, 10

N_DEV = 8
VMEM_LIMIT_BYTES = 56 << 20
MESH = pl.DeviceIdType.MESH

_NN = (((1,), (0,)), ((), ()))
_NT = (((1,), (1,)), ((), ()))
_TN = (((0,), (0,)), ((), ()))


def _dot(a, b, dn=_NN):
    return lax.dot_general(a.astype(_MXU), b.astype(_MXU), dn, preferred_element_type=F32)


def _params(*sem):
    return pltpu.CompilerParams(dimension_semantics=sem, vmem_limit_bytes=VMEM_LIMIT_BYTES)


def _sds(shape, dtype):
    return jax.ShapeDtypeStruct(tuple(shape), dtype)


def _rowsum8(x):
    return jnp.sum(x.reshape(x.shape[0] // 8, 8, x.shape[1]), axis=0)


def _swap_halves(x, half):
    width = x.shape[1]
    lane = lax.broadcasted_iota(jnp.int32, x.shape, 1)
    up = pltpu.roll(x, width - half, axis=1)
    down = pltpu.roll(x, half, axis=1)
    return jnp.where((lane & (2 * half - 1)) < half, up, down)


def _sigmoid(x):
    return 1.0 / (1.0 + jnp.exp(-x))


def _norm_matmul(name, x, gain, w, *, tm, tn, out_dtype):
    T, K = x.shape
    N = w.shape[1]
    tm, tn = min(tm, T), min(tn, N)

    def body(x_ref, g_ref, w_ref, y_ref, h_ref, h_sc):
        @pl.when(pl.program_id(1) == 0)
        def _():
            xv = x_ref[...]
            r = lax.rsqrt(jnp.mean(xv * xv, axis=-1, keepdims=True) + EPS)
            h = (xv * r * g_ref[...]).astype(_MXU)
            h_sc[...] = h
            h_ref[...] = h
        y_ref[...] = jnp.dot(h_sc[...], w_ref[...], preferred_element_type=F32).astype(y_ref.dtype)

    return pl.pallas_call(
        body, name=name, grid=(T // tm, N // tn),
        in_specs=[pl.BlockSpec((tm, K), lambda i, j: (i, 0)),
                  pl.BlockSpec((1, K), lambda i, j: (0, 0)),
                  pl.BlockSpec((K, tn), lambda i, j: (0, j))],
        out_specs=[pl.BlockSpec((tm, tn), lambda i, j: (i, j)),
                   pl.BlockSpec((tm, K), lambda i, j: (i, 0))],
        out_shape=[_sds((T, N), out_dtype), _sds((T, K), _MXU)],
        scratch_shapes=[pltpu.VMEM((tm, K), _MXU)],
        compiler_params=_params("parallel", "arbitrary"),
    )(x, gain, w)


def _matmul_res(name, a_list, w, res, *, tm, relu2=False, target=None):
    T = res.shape[0]
    N = w.shape[1]
    K = a_list[0].shape[1]
    n_a = len(a_list)
    tm = min(tm, T)
    with_loss = target is not None

    def body(*refs):
        a_refs = refs[:n_a]
        w_refs = refs[n_a:2 * n_a]
        res_ref = refs[2 * n_a]
        acc = res_ref[...]
        for a_ref, w_ref in zip(a_refs, w_refs):
            a = a_ref[...]
            if relu2:
                a = jnp.square(jnp.maximum(a.astype(F32), 0.0))
            acc = acc + _dot(a, w_ref[...])
        if with_loss:
            tgt_ref, g_ref, g16_ref, loss_ref = refs[2 * n_a + 1:]
            diff = acc - tgt_ref[...]
            g = diff * (1.0 / N)
            g_ref[...] = g
            g16_ref[...] = g.astype(g16_ref.dtype)

            @pl.when(pl.program_id(0) == 0)
            def _():
                loss_ref[...] = jnp.zeros_like(loss_ref)
            loss_ref[...] += _rowsum8(diff * diff)
        else:
            refs[2 * n_a + 1][...] = acc

    row = lambda i: (i, 0)
    in_specs = [pl.BlockSpec((tm, K), row) for _ in a_list]
    in_specs += [pl.BlockSpec((K, N), functools.partial(lambda i, b: (b, 0), b=b)) for b in range(n_a)]
    in_specs += [pl.BlockSpec((tm, N), row)]
    args = list(a_list) + [w] * n_a + [res]
    if with_loss:
        in_specs.append(pl.BlockSpec((tm, N), row))
        args.append(target)
        out_specs = [pl.BlockSpec((tm, N), row), pl.BlockSpec((tm, N), row), pl.BlockSpec((8, N), lambda i: (0, 0))]
        out_shape = [_sds((T, N), F32), _sds((T, N), _MXU), _sds((8, N), F32)]
        sem = "arbitrary"
    else:
        out_specs = pl.BlockSpec((tm, N), row)
        out_shape = _sds((T, N), F32)
        sem = "parallel"
    return pl.pallas_call(body, name=name, grid=(T // tm,), in_specs=in_specs, out_specs=out_specs,
                          out_shape=out_shape, compiler_params=_params(sem))(*args)


def _matmul_nt(name, a, w, *, tm, tn, out_dtype, relu_of=None):
    T, K = a.shape
    N = w.shape[0]
    tm, tn = min(tm, T), min(tn, N)

    def body(*refs):
        if relu_of is None:
            a_ref, w_ref, o_ref = refs
            o_ref[...] = _dot(a_ref[...], w_ref[...], _NT).astype(o_ref.dtype)
        else:
            a_ref, w_ref, u_ref, o_ref = refs
            da = _dot(a_ref[...], w_ref[...], _NT)
            o_ref[...] = (da * (2.0 * jnp.maximum(u_ref[...].astype(F32), 0.0))).astype(o_ref.dtype)

    in_specs = [pl.BlockSpec((tm, K), lambda i, j: (i, 0)), pl.BlockSpec((tn, K), lambda i, j: (j, 0))]
    args = [a, w]
    if relu_of is not None:
        in_specs.append(pl.BlockSpec((tm, tn), lambda i, j: (i, j)))
        args.append(relu_of)
    return pl.pallas_call(body, name=name, grid=(T // tm, N // tn), in_specs=in_specs,
                          out_specs=pl.BlockSpec((tm, tn), lambda i, j: (i, j)),
                          out_shape=_sds((T, N), out_dtype),
                          compiler_params=_params("parallel", "parallel"))(*args)


def _matmul_nt_normbwd(name, dy, w, x, gain, dres, *, tm):
    T, K = dy.shape
    N = w.shape[0]
    tm = min(tm, T)

    def body(dy_ref, w_ref, x_ref, g_ref, dres_ref, dx_ref, dx16_ref, dg_ref):
        dh = _dot(dy_ref[...], w_ref[...], _NT)
        xv = x_ref[...]
        r = lax.rsqrt(jnp.mean(xv * xv, axis=-1, keepdims=True) + EPS)
        xhat = xv * r
        dxhat = dh * g_ref[...]
        dx = dres_ref[...] + r * (dxhat - xhat * jnp.mean(dxhat * xhat, axis=-1, keepdims=True))
        dx_ref[...] = dx
        dx16_ref[...] = dx.astype(dx16_ref.dtype)

        @pl.when(pl.program_id(0) == 0)
        def _():
            dg_ref[...] = jnp.zeros_like(dg_ref)
        dg_ref[...] += _rowsum8(dh * xhat)

    row = lambda i: (i, 0)
    return pl.pallas_call(
        body, name=name, grid=(T // tm,),
        in_specs=[pl.BlockSpec((tm, K), row), pl.BlockSpec((N, K), lambda i: (0, 0)),
                  pl.BlockSpec((tm, N), row), pl.BlockSpec((1, N), lambda i: (0, 0)), pl.BlockSpec((tm, N), row)],
        out_specs=[pl.BlockSpec((tm, N), row), pl.BlockSpec((tm, N), row), pl.BlockSpec((8, N), lambda i: (0, 0))],
        out_shape=[_sds((T, N), F32), _sds((T, N), _MXU), _sds((8, N), F32)],
        compiler_params=_params("arbitrary"),
    )(dy, w, x, gain, dres)


def _matmul_tn(name, a, b, *, tk, tn, tt, out_dtype, relu2=False):
    T, Ka = a.shape
    Nb = b.shape[1]
    tk, tn, tt = min(tk, Ka), min(tn, Nb), min(tt, T)
    nt = T // tt

    def body(a_ref, b_ref, o_ref, acc):
        t = pl.program_id(2)

        @pl.when(t == 0)
        def _():
            acc[...] = jnp.zeros_like(acc)
        av = a_ref[...]
        if relu2:
            av = jnp.square(jnp.maximum(av.astype(F32), 0.0))
        acc[...] += _dot(av, b_ref[...], _TN)

        @pl.when(t == nt - 1)
        def _():
            o_ref[...] = acc[...].astype(o_ref.dtype)

    return pl.pallas_call(
        body, name=name, grid=(Ka // tk, Nb // tn, nt),
        in_specs=[pl.BlockSpec((tt, tk), lambda i, j, t: (t, i)), pl.BlockSpec((tt, tn), lambda i, j, t: (t, j))],
        out_specs=pl.BlockSpec((tk, tn), lambda i, j, t: (i, j)),
        out_shape=_sds((Ka, Nb), out_dtype),
        scratch_shapes=[pltpu.VMEM((tk, tn), F32)],
        compiler_params=_params("parallel", "parallel", "arbitrary"),
    )(a, b)


def _rope_angles(pos, dim, theta):
    inv = theta ** (-jnp.arange(0, dim, 2, dtype=F32) / dim)
    return pos.astype(F32)[:, None] * inv[None, :]


def _ret_rope_tables(T):
    ang = _rope_angles(jnp.arange(T), RET_DK, RET_THETA)
    c, s = jnp.cos(ang), jnp.sin(ang)
    return jnp.concatenate([c, c], axis=1), jnp.concatenate([-s, s], axis=1)


def _axial_rope_tables(T):
    rows = T // GRID_W
    ar = _rope_angles(jnp.arange(rows), HEAD_DIM // 2, AX_THETA)
    ac = _rope_angles(jnp.arange(GRID_W), HEAD_DIM // 2, AX_THETA)
    by_row = lambda a: jnp.repeat(a, GRID_W, axis=0)
    by_col = lambda a: jnp.tile(a, (rows, 1))
    cos = jnp.concatenate([by_row(jnp.cos(ar)), by_row(jnp.cos(ar)), by_col(jnp.cos(ac)), by_col(jnp.cos(ac))], axis=1)
    sin = jnp.concatenate([by_row(-jnp.sin(ar)), by_row(jnp.sin(ar)), by_col(-jnp.sin(ac)), by_col(jnp.sin(ac))], axis=1)
    return cos, sin


(TAB_D, TAB_DT, TAB_EF, TAB_EB, TAB_A, TAB_B, TAB_CF, TAB_CB,
 TAB_RA, TAB_RB, TAB_RCF, TAB_RCB, TAB_KF, TAB_KB) = range(14)


def _retention_tables(decay_logit):
    lg = jax.nn.log_sigmoid(decay_logit.astype(F32))
    lam, mu = lg[0][:, None, None], lg[1][:, None, None]
    idx = jnp.arange(CHUNK, dtype=F32)
    diff = (idx[:, None] - idx[None, :])[None]
    df = jnp.where(diff >= 0, jnp.exp(jnp.maximum(diff, 0.0) * lam), 0.0)
    db = jnp.where(diff < 0, jnp.exp(jnp.maximum(-diff, 0.0) * mu), 0.0)
    d = df + db
    r = idx[None, :, None]
    ones = jnp.ones((1, 1, CHUNK), F32)
    a = jnp.exp((r + 1.0) * lam) * ones
    b = jnp.exp((CHUNK - r) * mu) * ones
    cf = jnp.exp((CHUNK - 1.0 - r) * lam) * ones
    cb = jnp.exp(r * mu) * ones
    full = jnp.ones((1, CHUNK, CHUNK), F32)
    kf = CHUNK * jnp.exp(CHUNK * lam) * full
    kb = CHUNK * jnp.exp(CHUNK * mu) * full
    tabs = jnp.stack([d, jnp.swapaxes(d, 1, 2), diff * df, -diff * db, a, b, cf, cb,
                      (r + 1.0) * a, (CHUNK - r) * b, (CHUNK - 1.0 - r) * cf, r * cb, kf, kb], axis=1)

    def lanes(tab):
        return jnp.transpose(tab, (1, 0, 2)).reshape(CHUNK, RET_HEADS * CHUNK)

    def dec(l):
        return jnp.exp(CHUNK * l)[:, 0, :] * jnp.ones((1, RET_DV), F32)

    weights = dict(a=lanes(a), b=lanes(b), cf=lanes(cf), cb=lanes(cb), dec_f=dec(lam), dec_b=dec(mu))
    return tabs, weights, lg


def _t5_bucket(rel):
    nb = T5_BUCKETS // 2
    max_exact = nb // 2
    ret = jnp.where(rel > 0, nb, 0)
    n = jnp.abs(rel)
    nf = jnp.maximum(n, 1).astype(F32)
    large = max_exact + (jnp.log(nf / max_exact) / math.log(T5_MAX_DIST / max_exact)
                         * (nb - max_exact)).astype(jnp.int32)
    large = jnp.minimum(large, nb - 1)
    return ret + jnp.where(n < max_exact, n, large)


def _swa_rel():
    r = jnp.arange(CHUNK)
    j = jnp.arange(3 * CHUNK)
    return j[None, :] - CHUNK - r[:, None]


def _swa_bias(t5_table):
    rel = _swa_rel()
    bucket = jnp.where(jnp.abs(rel) <= CHUNK, _t5_bucket(rel), -1).astype(jnp.int32)

    def body(tab_ref, bk_ref, o_ref):
        bk = bk_ref[...]
        for h in range(SWA_HEADS):
            pick = lambda b, acc, h=h: jnp.where(bk == b, tab_ref[b, h], acc)
            o_ref[h] = lax.fori_loop(0, T5_BUCKETS, pick, jnp.full(bk.shape, NEG_INF, F32))

    return pl.pallas_call(
        body, name="t5_bias",
        in_specs=[pl.BlockSpec(memory_space=pltpu.SMEM), pl.BlockSpec(memory_space=pltpu.VMEM)],
        out_specs=pl.BlockSpec(memory_space=pltpu.VMEM),
        out_shape=_sds((SWA_HEADS, CHUNK, 3 * CHUNK), F32),
    )(t5_table.astype(F32), bucket)


def _prep_even(proj, cos, sin, q_gain, k_gain, *, tm):
    T = proj.shape[0]
    tm = min(tm, T)

    def body(qa_ref, ka_ref, qb_ref, kb_ref, c_ref, s_ref, qg_ref, kg_ref, qr_ref, kr_ref, qn_ref, kn_ref):
        c = jnp.concatenate([c_ref[...]] * RET_HEADS, axis=1)
        s = jnp.concatenate([s_ref[...]] * RET_HEADS, axis=1)
        qa = qa_ref[...]
        qr_ref[...] = (qa * c + _swap_halves(qa, RET_DK // 2) * s).astype(qr_ref.dtype)
        ka = ka_ref[...]
        kr_ref[...] = ((ka * c + _swap_halves(ka, RET_DK // 2) * s) * (RET_DK ** -0.5)).astype(kr_ref.dtype)
        for src, gain, dst, heads in ((qb_ref, qg_ref, qn_ref, SWA_HEADS), (kb_ref, kg_ref, kn_ref, SWA_KV_HEADS)):
            for h in range(heads):
                sl = slice(h * HEAD_DIM, (h + 1) * HEAD_DIM)
                xh = src[:, sl]
                r = lax.rsqrt(jnp.mean(xh * xh, axis=-1, keepdims=True) + EPS)
                dst[:, sl] = (xh * r * gain[...]).astype(dst.dtype)

    row = lambda i: (i, 0)
    const = lambda i: (0, 0)
    return pl.pallas_call(
        body, name="prep_even", grid=(T // tm,),
        in_specs=[pl.BlockSpec((tm, RET_Q), lambda i: (i, 0)), pl.BlockSpec((tm, RET_Q), lambda i: (i, 1)),
                  pl.BlockSpec((tm, D_MODEL), lambda i: (i, 3)), pl.BlockSpec((tm, 256), lambda i: (i, 16)),
                  pl.BlockSpec((tm, RET_DK), row), pl.BlockSpec((tm, RET_DK), row),
                  pl.BlockSpec((1, HEAD_DIM), const), pl.BlockSpec((1, HEAD_DIM), const)],
        out_specs=[pl.BlockSpec((tm, RET_Q), row), pl.BlockSpec((tm, RET_Q), row),
                   pl.BlockSpec((tm, D_MODEL), row), pl.BlockSpec((tm, 256), row)],
        out_shape=[_sds((T, RET_Q), _MXU), _sds((T, RET_Q), _MXU), _sds((T, D_MODEL), _MXU), _sds((T, 256), _MXU)],
        compiler_params=_params("parallel"),
    )(proj, proj, proj, proj, cos, sin, q_gain, k_gain)


def _ret_scan(name, x, y, y_col, w_asc, dec_asc, w_desc, dec_desc):
    T = x.shape[0]
    nc = T // CHUNK

    def body(xa_ref, ya_ref, xd_ref, yd_ref, wa_ref, da_ref, wd_ref, dd_ref, sa_out, sd_out, sa, sd):
        @pl.when(pl.program_id(0) == 0)
        def _():
            sa[...] = jnp.zeros_like(sa)
            sd[...] = jnp.zeros_like(sd)
        sa_out[0] = sa[...].astype(sa_out.dtype)
        sd_out[0] = sd[...].astype(sd_out.dtype)
        for x_ref, y_ref, w_ref, d_ref, st in ((xa_ref, ya_ref, wa_ref, da_ref, sa), (xd_ref, yd_ref, wd_ref, dd_ref, sd)):
            for h in range(RET_HEADS):
                ks = slice(h * RET_DK, (h + 1) * RET_DK)
                vs = slice(h * RET_DV, (h + 1) * RET_DV)
                u = _dot(x_ref[:, ks].astype(F32) * w_ref[:, ks], y_ref[:, vs], _TN)
                st[ks, :] = st[ks, :] * d_ref[h:h + 1, :] + u

    asc = lambda i: (i, 0)
    desc = lambda i: (nc - 1 - i, 0)
    const = lambda i: (0, 0)
    return pl.pallas_call(
        body, name=name, grid=(nc,),
        in_specs=[pl.BlockSpec((CHUNK, RET_Q), asc), pl.BlockSpec((CHUNK, RET_V), lambda i: (i, y_col)),
                  pl.BlockSpec((CHUNK, RET_Q), desc), pl.BlockSpec((CHUNK, RET_V), lambda i: (nc - 1 - i, y_col)),
                  pl.BlockSpec((CHUNK, RET_Q), const), pl.BlockSpec((RET_HEADS, RET_DV), const),
                  pl.BlockSpec((CHUNK, RET_Q), const), pl.BlockSpec((RET_HEADS, RET_DV), const)],
        out_specs=[pl.BlockSpec((1, RET_Q, RET_DV), lambda i: (i, 0, 0)),
                   pl.BlockSpec((1, RET_Q, RET_DV), lambda i: (nc - 1 - i, 0, 0))],
        out_shape=[_sds((nc, RET_Q, RET_DV), _MXU), _sds((nc, RET_Q, RET_DV), _MXU)],
        scratch_shapes=[pltpu.VMEM((RET_Q, RET_DV), F32), pltpu.VMEM((RET_Q, RET_DV), F32)],
        compiler_params=_params("arbitrary"),
    )(x, y, x, y, w_asc, dec_asc, w_desc, dec_desc)


def _ret_out(qr, kr, proj, sf, sb, tabs, gain):
    T = qr.shape[0]
    nc = T // CHUNK

    def body(q_ref, k_ref, v_ref, g_ref, sf_ref, sb_ref, tab_ref, gain_ref, o_ref, y_ref):
        for h in range(RET_HEADS):
            ks = slice(h * RET_DK, (h + 1) * RET_DK)
            vs = slice(h * RET_DV, (h + 1) * RET_DV)
            q, k, v = q_ref[:, ks], k_ref[:, ks], v_ref[:, vs]
            qf = q.astype(F32)
            a_mat = _dot(q, k, _NT) * tab_ref[h, 0]
            o = (_dot(a_mat, v) + _dot(qf * tab_ref[h, 1], sf_ref[0, ks, :]) + _dot(qf * tab_ref[h, 2], sb_ref[0, ks, :]))
            o_ref[:, vs] = o
            r = lax.rsqrt(jnp.mean(o * o, axis=-1, keepdims=True) + EPS)
            g = g_ref[:, vs]
            y_ref[:, vs] = (g * _sigmoid(g) * (o * r * gain_ref[:, vs])).astype(y_ref.dtype)

    row = lambda i: (i, 0)
    return pl.pallas_call(
        body, name="ret_out", grid=(nc,),
        in_specs=[pl.BlockSpec((CHUNK, RET_Q), row), pl.BlockSpec((CHUNK, RET_Q), row),
                  pl.BlockSpec((CHUNK, RET_V), lambda i: (i, 1)), pl.BlockSpec((CHUNK, RET_V), lambda i: (i, 2)),
                  pl.BlockSpec((1, RET_Q, RET_DV), lambda i: (i, 0, 0)), pl.BlockSpec((1, RET_Q, RET_DV), lambda i: (i, 0, 0)),
                  pl.BlockSpec((RET_HEADS, 3, CHUNK, CHUNK), lambda i: (0, 0, 0, 0)),
                  pl.BlockSpec((1, RET_V), lambda i: (0, 0))],
        out_specs=[pl.BlockSpec((CHUNK, RET_V), row), pl.BlockSpec((CHUNK, RET_V), row)],
        out_shape=[_sds((T, RET_V), F32), _sds((T, RET_V), _MXU)],
        compiler_params=_params("parallel"),
    )(qr, kr, proj, proj, sf, sb, tabs, gain)


def _ret_gate_bwd(dycat, proj, ret_o, gain, *, tm):
    T = ret_o.shape[0]
    tm = min(tm, T)

    def body(dy_ref, g_ref, o_ref, gain_ref, do_ref, dg_ref, dgain_ref):
        @pl.when(pl.program_id(0) == 0)
        def _():
            dgain_ref[...] = jnp.zeros_like(dgain_ref)
        for h in range(RET_HEADS):
            vs = slice(h * RET_DV, (h + 1) * RET_DV)
            o, g, dya, gn = o_ref[:, vs], g_ref[:, vs], dy_ref[:, vs], gain_ref[:, vs]
            r = lax.rsqrt(jnp.mean(o * o, axis=-1, keepdims=True) + EPS)
            ohat = o * r
            sg = _sigmoid(g)
            dy = dya * (g * sg)
            dg_ref[:, vs] = (dya * (ohat * gn) * (sg * (1.0 + g * (1.0 - sg)))).astype(dg_ref.dtype)
            dyg = dy * gn
            do_ref[:, vs] = (r * (dyg - ohat * jnp.mean(dyg * ohat, axis=-1, keepdims=True))).astype(do_ref.dtype)
            dgain_ref[:, vs] += _rowsum8(dy * ohat)

    row = lambda i: (i, 0)
    return pl.pallas_call(
        body, name="ret_gate_bwd", grid=(T // tm,),
        in_specs=[pl.BlockSpec((tm, RET_V), row), pl.BlockSpec((tm, RET_V), lambda i: (i, 2)),
                  pl.BlockSpec((tm, RET_V), row), pl.BlockSpec((1, RET_V), lambda i: (0, 0))],
        out_specs=[pl.BlockSpec((tm, RET_V), row), pl.BlockSpec((tm, RET_V), row), pl.BlockSpec((8, RET_V), lambda i: (0, 0))],
        out_shape=[_sds((T, RET_V), _MXU), _sds((T, RET_V), _MXU), _sds((8, RET_V), F32)],
        compiler_params=_params("arbitrary"),
    )(dycat, proj, ret_o, gain)


def _ret_bwd(qr, kr, proj, g_out, sf, sb, rf, rb, tabs):
    T = qr.shape[0]
    nc = T // CHUNK

    def body(q_ref, k_ref, v_ref, g_ref, sf_ref, sb_ref, rf_ref, rb_ref, tab_ref, dq_ref, dk_ref, dv_ref, dl_ref):
        @pl.when(pl.program_id(0) == 0)
        def _():
            dl_ref[...] = jnp.zeros_like(dl_ref)
        for h in range(RET_HEADS):
            ks = slice(h * RET_DK, (h + 1) * RET_DK)
            vs = slice(h * RET_DV, (h + 1) * RET_DV)
            q, k, v, g = q_ref[:, ks], k_ref[:, ks], v_ref[:, vs], g_ref[:, vs]
            s_f, s_b, r_f, r_b = sf_ref[0, ks, :], sb_ref[0, ks, :], rf_ref[0, ks, :], rb_ref[0, ks, :]
            tab = lambda t: tab_ref[h, t]
            qf, kf = q.astype(F32), k.astype(F32)
            qk = _dot(q, k, _NT)
            da_raw = _dot(g, v, _NT)
            x_f, x_b = _dot(g, s_f, _NT), _dot(g, s_b, _NT)
            dq_ref[:, ks] = _dot(da_raw * tab(TAB_D), k) + tab(TAB_A) * x_f + tab(TAB_B) * x_b
            at = _dot(k, q, _NT) * tab(TAB_DT)
            dat = _dot(v, g, _NT) * tab(TAB_DT)
            y_f, y_b = _dot(v, r_f, _NT), _dot(v, r_b, _NT)
            dk_ref[:, ks] = _dot(dat, q) + tab(TAB_CF) * y_f + tab(TAB_CB) * y_b
            dv_ref[:, vs] = (_dot(at, g) + _dot(kf * tab(TAB_CF), r_f) + _dot(kf * tab(TAB_CB), r_b)).astype(dv_ref.dtype)
            inner = da_raw * qk
            rs_f = r_f.astype(F32) * s_f.astype(F32)
            rs_b = r_b.astype(F32) * s_b.astype(F32)
            dl_f = (inner * tab(TAB_EF) + tab(TAB_RA) * qf * x_f + tab(TAB_RCF) * kf * y_f
                    + tab(TAB_KF) * (rs_f[:, :CHUNK] + rs_f[:, CHUNK:]))
            dl_b = (inner * tab(TAB_EB) + tab(TAB_RB) * qf * x_b + tab(TAB_RCB) * kf * y_b
                    + tab(TAB_KB) * (rs_b[:, :CHUNK] + rs_b[:, CHUNK:]))
            dl_ref[2 * h:2 * h + 1, :] += jnp.sum(dl_f, axis=0, keepdims=True)
            dl_ref[2 * h + 1:2 * h + 2, :] += jnp.sum(dl_b, axis=0, keepdims=True)

    row = lambda i: (i, 0)
    st = lambda i: (i, 0, 0)
    return pl.pallas_call(
        body, name="ret_bwd", grid=(nc,),
        in_specs=[pl.BlockSpec((CHUNK, RET_Q), row), pl.BlockSpec((CHUNK, RET_Q), row),
                  pl.BlockSpec((CHUNK, RET_V), lambda i: (i, 1)), pl.BlockSpec((CHUNK, RET_V), row),
                  pl.BlockSpec((1, RET_Q, RET_DV), st), pl.BlockSpec((1, RET_Q, RET_DV), st),
                  pl.BlockSpec((1, RET_Q, RET_DV), st), pl.BlockSpec((1, RET_Q, RET_DV), st),
                  pl.BlockSpec((RET_HEADS, 14, CHUNK, CHUNK), lambda i: (0, 0, 0, 0))],
        out_specs=[pl.BlockSpec((CHUNK, RET_Q), row), pl.BlockSpec((CHUNK, RET_Q), row),
                   pl.BlockSpec((CHUNK, RET_V), row), pl.BlockSpec((8, CHUNK), lambda i: (0, 0))],
        out_shape=[_sds((T, RET_Q), F32), _sds((T, RET_Q), F32), _sds((T, RET_V), _MXU), _sds((8, CHUNK), F32)],
        compiler_params=_params("arbitrary"),
    )(qr, kr, proj, g_out, sf, sb, rf, rb, tabs)


def _swa_probs(q, k_win, bias, sink, valid):
    s = _dot(q, k_win, _NT) * ATT_SCALE + bias
    s = jnp.where(valid, s, NEG_INF)
    m = jnp.maximum(jnp.max(s, axis=-1, keepdims=True), sink)
    p = jnp.exp(s - m)
    e_sink = jnp.exp(sink - m)
    inv = 1.0 / (jnp.sum(p, axis=-1, keepdims=True) + e_sink)
    return p * inv, e_sink * inv


def _swa_group(g, q_ref, bias_ref, sink_ref):
    group = SWA_HEADS // SWA_KV_HEADS
    heads = range(g * group, (g + 1) * group)
    q = jnp.concatenate([q_ref[:, h * HEAD_DIM:(h + 1) * HEAD_DIM] for h in heads], axis=0)
    bias = bias_ref[g * group:(g + 1) * group].reshape(group * CHUNK, 3 * CHUNK)
    sink = jnp.concatenate([jnp.broadcast_to(sink_ref[h:h + 1, 0:1], (CHUNK, 1)) for h in heads], axis=0)
    return q, bias, sink


def _swa_valid(i, nb):
    col = lax.broadcasted_iota(jnp.int32, (1, 3 * CHUNK), 1)
    return jnp.logical_and(jnp.logical_or(col >= CHUNK, i > 0), jnp.logical_or(col < 2 * CHUNK, i < nb - 1))


def _swa_window_specs(nb, width, col_block, clamp):
    prev = lambda i: (jnp.maximum(clamp(i) - 1, 0), col_block)
    cur = lambda i: (clamp(i), col_block)
    nxt = lambda i: (jnp.minimum(clamp(i) + 1, nb - 1), col_block)
    return [pl.BlockSpec((CHUNK, width), f) for f in (prev, cur, nxt)]


def _swa_fwd(qn, kn, proj, bias, sink):
    T = qn.shape[0]
    nb = T // CHUNK
    kvw = SWA_KV_HEADS * HEAD_DIM
    group = SWA_HEADS // SWA_KV_HEADS

    def body(q_ref, k0, k1, k2, v0, v1, v2, bias_ref, sink_ref, y_ref):
        i = pl.program_id(0)
        valid = _swa_valid(i, nb)
        for g in range(SWA_KV_HEADS):
            gs = slice(g * HEAD_DIM, (g + 1) * HEAD_DIM)
            k_win = jnp.concatenate([k0[:, gs], k1[:, gs], k2[:, gs]], axis=0)
            v_win = jnp.concatenate([v0[:, gs], v1[:, gs], v2[:, gs]], axis=0).astype(_MXU)
            q, bias, sink = _swa_group(g, q_ref, bias_ref, sink_ref)
            p, _ = _swa_probs(q, k_win, bias, sink, valid)
            o = _dot(p, v_win)
            for hh in range(group):
                h = g * group + hh
                y_ref[:, h * HEAD_DIM:(h + 1) * HEAD_DIM] = o[hh * CHUNK:(hh + 1) * CHUNK].astype(y_ref.dtype)

    ident = lambda i: i
    return pl.pallas_call(
        body, name="swa_fwd", grid=(nb,),
        in_specs=[pl.BlockSpec((CHUNK, D_MODEL), lambda i: (i, 0))]
        + _swa_window_specs(nb, kvw, 0, ident) + _swa_window_specs(nb, kvw, 17, ident)
        + [pl.BlockSpec((SWA_HEADS, CHUNK, 3 * CHUNK), lambda i: (0, 0, 0)), pl.BlockSpec((SWA_HEADS, HEAD_DIM), lambda i: (0, 0))],
        out_specs=pl.BlockSpec((CHUNK, D_MODEL), lambda i: (i, 0)),
        out_shape=_sds((T, D_MODEL), _MXU),
        compiler_params=_params("parallel"),
    )(qn, kn, kn, kn, proj, proj, proj, bias, sink)


def _swa_bwd(qn, kn, proj, dycat, bias, sink):
    T = qn.shape[0]
    nb = T // CHUNK
    kvw = SWA_KV_HEADS * HEAD_DIM
    group = SWA_HEADS // SWA_KV_HEADS

    def body(q_ref, k0, k1, k2, v0, v1, v2, dy_ref, bias_ref, sink_ref,
             dq_ref, dk_ref, dv_ref, dbias_ref, dsink_ref, acc_a, acc_b):
        i = pl.program_id(0)

        @pl.when(i == 0)
        def _():
            dbias_ref[...] = jnp.zeros_like(dbias_ref)
            dsink_ref[...] = jnp.zeros_like(dsink_ref)
            acc_a[...] = jnp.zeros_like(acc_a)
            acc_b[...] = jnp.zeros_like(acc_b)

        @pl.when(i < nb)
        def _():
            valid = _swa_valid(i, nb)
            for g in range(SWA_KV_HEADS):
                gs = slice(g * HEAD_DIM, (g + 1) * HEAD_DIM)
                k_win = jnp.concatenate([k0[:, gs], k1[:, gs], k2[:, gs]], axis=0)
                v_win = jnp.concatenate([v0[:, gs], v1[:, gs], v2[:, gs]], axis=0).astype(_MXU)
                q, bias, sink = _swa_group(g, q_ref, bias_ref, sink_ref)
                dy = jnp.concatenate([dy_ref[:, h * HEAD_DIM:(h + 1) * HEAD_DIM] for h in range(g * group, (g + 1) * group)], axis=0)
                p, p_sink = _swa_probs(q, k_win, bias, sink, valid)
                dp = _dot(dy, v_win, _NT)
                delta = jnp.sum(p * dp, axis=-1, keepdims=True)
                ds = p * (dp - delta)
                dsink = -p_sink * delta
                dq = _dot(ds, k_win) * ATT_SCALE
                for hh in range(group):
                    h = g * group + hh
                    rows = slice(hh * CHUNK, (hh + 1) * CHUNK)
                    dbias_ref[h] += ds[rows]
                    dsink_ref[h:h + 1, :] += jnp.sum(dsink[rows], axis=0, keepdims=True) * jnp.ones((1, HEAD_DIM), F32)
                    dq_ref[:, h * HEAD_DIM:(h + 1) * HEAD_DIM] = dq[rows]
                dk_win = _dot(ds, q, _TN) * ATT_SCALE
                dv_win = _dot(p, dy, _TN)
                for win, out_ref, col0 in ((dk_win, dk_ref, 0), (dv_win, dv_ref, kvw)):
                    cs = slice(col0 + g * HEAD_DIM, col0 + (g + 1) * HEAD_DIM)
                    out_ref[:, gs] = acc_a[:, cs] + win[:CHUNK]
                    acc_a[:, cs] = acc_b[:, cs] + win[CHUNK:2 * CHUNK]
                    acc_b[:, cs] = win[2 * CHUNK:]

        @pl.when(i == nb)
        def _():
            dk_ref[...] = acc_a[:, :kvw]
            dv_ref[...] = acc_a[:, kvw:]

    clamp = lambda i: jnp.minimum(i, nb - 1)
    late = lambda i: (jnp.maximum(i - 1, 0), 0)
    return pl.pallas_call(
        body, name="swa_bwd", grid=(nb + 1,),
        in_specs=[pl.BlockSpec((CHUNK, D_MODEL), lambda i: (clamp(i), 0))]
        + _swa_window_specs(nb, kvw, 0, clamp) + _swa_window_specs(nb, kvw, 17, clamp)
        + [pl.BlockSpec((CHUNK, D_MODEL), lambda i: (clamp(i), 1)),
           pl.BlockSpec((SWA_HEADS, CHUNK, 3 * CHUNK), lambda i: (0, 0, 0)), pl.BlockSpec((SWA_HEADS, HEAD_DIM), lambda i: (0, 0))],
        out_specs=[pl.BlockSpec((CHUNK, D_MODEL), lambda i: (clamp(i), 0)),
                   pl.BlockSpec((CHUNK, kvw), late), pl.BlockSpec((CHUNK, kvw), late),
                   pl.BlockSpec((SWA_HEADS, CHUNK, 3 * CHUNK), lambda i: (0, 0, 0)), pl.BlockSpec((SWA_HEADS, HEAD_DIM), lambda i: (0, 0))],
        out_shape=[_sds((T, D_MODEL), F32), _sds((T, kvw), F32), _sds((T, kvw), F32),
                   _sds((SWA_HEADS, CHUNK, 3 * CHUNK), F32), _sds((SWA_HEADS, HEAD_DIM), F32)],
        scratch_shapes=[pltpu.VMEM((CHUNK, 2 * kvw), F32), pltpu.VMEM((CHUNK, 2 * kvw), F32)],
        compiler_params=_params("arbitrary"),
    )(qn, kn, kn, kn, proj, proj, proj, dycat, bias, sink)


def _t5_bucket_reduce(dbias, bucket):
    def body(db_ref, bk_ref, o_ref):
        bk = bk_ref[...]
        row = lax.broadcasted_iota(jnp.int32, (SWA_HEADS, HEAD_DIM), 0)
        lane = lax.broadcasted_iota(jnp.int32, (SWA_HEADS, HEAD_DIM), 1)

        def per_bucket(b, acc):
            mask = bk == b
            for h in range(SWA_HEADS):
                tot = jnp.sum(jnp.sum(jnp.where(mask, db_ref[h], 0.0), axis=0, keepdims=True), axis=1, keepdims=True)
                acc = acc + jnp.where(jnp.logical_and(row == h, lane == b), tot, 0.0)
            return acc

        o_ref[...] = lax.fori_loop(0, T5_BUCKETS, per_bucket, jnp.zeros((SWA_HEADS, HEAD_DIM), F32))

    return pl.pallas_call(body, name="t5_bucket_reduce", out_shape=_sds((SWA_HEADS, HEAD_DIM), F32),
                          compiler_params=pltpu.CompilerParams(vmem_limit_bytes=VMEM_LIMIT_BYTES))(dbias, bucket)


def _headnorm_bwd(x, dy, gain):
    r = lax.rsqrt(jnp.mean(x * x, axis=-1, keepdims=True) + EPS)
    xhat = x * r
    dyg = dy * gain
    return r * (dyg - xhat * jnp.mean(dyg * xhat, axis=-1, keepdims=True)), dy * xhat


def _post_even(proj, dqr, dkr, dva, dga, dqn, dkn, dvb, cos, sin, q_gain, k_gain, *, tm):
    T = proj.shape[0]
    tm = min(tm, T)
    kvw = SWA_KV_HEADS * HEAD_DIM

    def body(qb_ref, kb_ref, dqr_ref, dkr_ref, dva_ref, dga_ref, dqn_ref, dkn_ref, dvb_ref, c_ref, s_ref, qg_ref, kg_ref,
             dp_ref, dqg_ref, dkg_ref):
        @pl.when(pl.program_id(0) == 0)
        def _():
            dqg_ref[...] = jnp.zeros_like(dqg_ref)
            dkg_ref[...] = jnp.zeros_like(dkg_ref)
        c = jnp.concatenate([c_ref[...]] * RET_HEADS, axis=1)
        s = jnp.concatenate([s_ref[...]] * RET_HEADS, axis=1)
        dq = dqr_ref[...]
        dp_ref[:, 0:RET_Q] = (dq * c + _swap_halves(dq * s, RET_DK // 2)).astype(dp_ref.dtype)
        dk = dkr_ref[...] * (RET_DK ** -0.5)
        dp_ref[:, RET_Q:2 * RET_Q] = (dk * c + _swap_halves(dk * s, RET_DK // 2)).astype(dp_ref.dtype)
        off = 2 * RET_Q
        dp_ref[:, off:off + RET_V] = dva_ref[...].astype(dp_ref.dtype)
        dp_ref[:, off + RET_V:off + 2 * RET_V] = dga_ref[...].astype(dp_ref.dtype)
        off += 2 * RET_V
        for src, dsrc, gain, dgain, heads, base in ((qb_ref, dqn_ref, qg_ref, dqg_ref, SWA_HEADS, off),
                                                    (kb_ref, dkn_ref, kg_ref, dkg_ref, SWA_KV_HEADS, off + D_MODEL)):
            for h in range(heads):
                sl = slice(h * HEAD_DIM, (h + 1) * HEAD_DIM)
                dx, dgx = _headnorm_bwd(src[:, sl], dsrc[:, sl], gain[...])
                dp_ref[:, base + h * HEAD_DIM:base + (h + 1) * HEAD_DIM] = dx.astype(dp_ref.dtype)
                dgain[...] += _rowsum8(dgx)
        dp_ref[:, off + D_MODEL + kvw:] = dvb_ref[...].astype(dp_ref.dtype)

    row = lambda i: (i, 0)
    const = lambda i: (0, 0)
    return pl.pallas_call(
        body, name="post_even", grid=(T // tm,),
        in_specs=[pl.BlockSpec((tm, D_MODEL), lambda i: (i, 3)), pl.BlockSpec((tm, kvw), lambda i: (i, 16)),
                  pl.BlockSpec((tm, RET_Q), row), pl.BlockSpec((tm, RET_Q), row),
                  pl.BlockSpec((tm, RET_V), row), pl.BlockSpec((tm, RET_V), row),
                  pl.BlockSpec((tm, D_MODEL), row), pl.BlockSpec((tm, kvw), row), pl.BlockSpec((tm, kvw), row),
                  pl.BlockSpec((tm, RET_DK), row), pl.BlockSpec((tm, RET_DK), row),
                  pl.BlockSpec((1, HEAD_DIM), const), pl.BlockSpec((1, HEAD_DIM), const)],
        out_specs=[pl.BlockSpec((tm, EVEN_IN), row), pl.BlockSpec((8, HEAD_DIM), const), pl.BlockSpec((8, HEAD_DIM), const)],
        out_shape=[_sds((T, EVEN_IN), _MXU), _sds((8, HEAD_DIM), F32), _sds((8, HEAD_DIM), F32)],
        compiler_params=_params("arbitrary"),
    )(proj, proj, dqr, dkr, dva, dga, dqn, dkn, dvb, cos, sin, q_gain, k_gain)


def _prep_odd(proj, cos, sin, q_gain, k_gain, *, tm):
    T = proj.shape[0]
    tm = min(tm, T)
    kvw = AX_KV_HEADS * HEAD_DIM

    def body(q_ref, k_ref, v_ref, c_ref, s_ref, qg_ref, kg_ref, qx_ref, kx_ref, vx_ref):
        c, s = c_ref[...], s_ref[...]
        for src, gain, dst, heads in ((q_ref, qg_ref, qx_ref, AX_HEADS), (k_ref, kg_ref, kx_ref, AX_KV_HEADS)):
            for h in range(heads):
                sl = slice(h * HEAD_DIM, (h + 1) * HEAD_DIM)
                xh = src[:, sl]
                r = lax.rsqrt(jnp.mean(xh * xh, axis=-1, keepdims=True) + EPS)
                xn = xh * r * gain[...]
                dst[:, sl] = (xn * c + _swap_halves(xn, HEAD_DIM // 4) * s).astype(dst.dtype)
        vx_ref[...] = v_ref[...].astype(vx_ref.dtype)

    row = lambda i: (i, 0)
    const = lambda i: (0, 0)
    return pl.pallas_call(
        body, name="prep_odd", grid=(T // tm,),
        in_specs=[pl.BlockSpec((tm, D_MODEL), row), pl.BlockSpec((tm, kvw), lambda i: (i, 4)), pl.BlockSpec((tm, kvw), lambda i: (i, 5)),
                  pl.BlockSpec((tm, HEAD_DIM), row), pl.BlockSpec((tm, HEAD_DIM), row),
                  pl.BlockSpec((1, HEAD_DIM), const), pl.BlockSpec((1, HEAD_DIM), const)],
        out_specs=[pl.BlockSpec((tm, D_MODEL), row), pl.BlockSpec((tm, kvw), row), pl.BlockSpec((tm, kvw), row)],
        out_shape=[_sds((T, D_MODEL), _MXU), _sds((T, kvw), _MXU), _sds((T, kvw), _MXU)],
        compiler_params=_params("parallel"),
    )(proj, proj, proj, cos, sin, q_gain, k_gain)


def _post_odd(proj, dqxt, dkx, dvx, cos, sin, q_gain, k_gain, *, tm):
    T = proj.shape[0]
    tm = min(tm, T)
    kvw = AX_KV_HEADS * HEAD_DIM

    def body(q_ref, k_ref, dqt_ref, dk_ref, dv_ref, c_ref, s_ref, qg_ref, kg_ref, dp_ref, dqg_ref, dkg_ref):
        @pl.when(pl.program_id(0) == 0)
        def _():
            dqg_ref[...] = jnp.zeros_like(dqg_ref)
            dkg_ref[...] = jnp.zeros_like(dkg_ref)
        c, s = c_ref[...], s_ref[...]
        for src, dsrc, gain, dgain, heads, base in ((q_ref, dqt_ref, qg_ref, dqg_ref, AX_HEADS, 0),
                                                    (k_ref, dk_ref, kg_ref, dkg_ref, AX_KV_HEADS, D_MODEL)):
            for h in range(heads):
                sl = slice(h * HEAD_DIM, (h + 1) * HEAD_DIM)
                d = dsrc[sl, :].T if dsrc is dqt_ref else dsrc[:, sl]
                dn = d * c + _swap_halves(d * s, HEAD_DIM // 4)
                dx, dgx = _headnorm_bwd(src[:, sl], dn, gain[...])
                dp_ref[:, base + h * HEAD_DIM:base + (h + 1) * HEAD_DIM] = dx.astype(dp_ref.dtype)
                dgain[...] += _rowsum8(dgx)
        dp_ref[:, D_MODEL + kvw:] = dv_ref[...].astype(dp_ref.dtype)

    row = lambda i: (i, 0)
    const = lambda i: (0, 0)
    return pl.pallas_call(
        body, name="post_odd", grid=(T // tm,),
        in_specs=[pl.BlockSpec((tm, D_MODEL), row), pl.BlockSpec((tm, kvw), lambda i: (i, 4)),
                  pl.BlockSpec((D_MODEL, tm), lambda i: (0, i)), pl.BlockSpec((tm, kvw), row), pl.BlockSpec((tm, kvw), row),
                  pl.BlockSpec((tm, HEAD_DIM), row), pl.BlockSpec((tm, HEAD_DIM), row),
                  pl.BlockSpec((1, HEAD_DIM), const), pl.BlockSpec((1, HEAD_DIM), const)],
        out_specs=[pl.BlockSpec((tm, ODD_IN), row), pl.BlockSpec((8, HEAD_DIM), const), pl.BlockSpec((8, HEAD_DIM), const)],
        out_shape=[_sds((T, ODD_IN), _MXU), _sds((8, HEAD_DIM), F32), _sds((8, HEAD_DIM), F32)],
        compiler_params=_params("arbitrary"),
    )(proj, proj, dqxt, dkx, dvx, cos, sin, q_gain, k_gain)


SCORE_SCALE_LOG2 = ATT_SCALE * math.log2(math.e)


def _flash_fwd(qx, kx, vx, *, tq, tk):
    T = qx.shape[0]
    tq, tk = min(tq, T), min(tk, T)
    nq, nk = T // tq, T // tk
    group = AX_HEADS // AX_KV_HEADS

    def body(k_ref, v_ref, q_ref, o_ref, lse_ref, acc_sc, m_sc, l_sc):
        j = pl.program_id(2)

        @pl.when(j == 0)
        def _():
            m_sc[...] = jnp.full(m_sc.shape, NEG_INF, F32)
            l_sc[...] = jnp.zeros_like(l_sc)
            acc_sc[...] = jnp.zeros_like(acc_sc)
        k, v = k_ref[...], v_ref[...]

        def step(i, carry):
            cols = pl.ds(pl.multiple_of(i * tq, tq), tq)
            st = _dot(k, q_ref[cols, :], _NT) * SCORE_SCALE_LOG2
            m_old = m_sc[i]
            m_new = jnp.maximum(m_old, jnp.max(st, axis=0, keepdims=True))
            p = jnp.exp2(st - m_new)
            alpha = jnp.exp2(m_old - m_new)
            m_sc[i] = m_new
            l_sc[i] = alpha * l_sc[i] + jnp.sum(p, axis=0, keepdims=True)
            acc_sc[:, cols] = alpha * acc_sc[:, cols] + _dot(v, p, _TN)
            return carry

        lax.fori_loop(0, nq, step, 0)

        @pl.when(j == nk - 1)
        def _():
            def finish(i, carry):
                cols = pl.ds(pl.multiple_of(i * tq, tq), tq)
                o_ref[cols, :] = (acc_sc[:, cols] / l_sc[i]).T.astype(o_ref.dtype)
                lse_ref[0, i] = m_sc[i] + jnp.log2(l_sc[i])
                return carry

            lax.fori_loop(0, nq, finish, 0)

    kv = lambda g, h, j: (j, g)
    qh = lambda g, h, j: (0, g * group + h)
    o, lse = pl.pallas_call(
        body, name="flash_fwd", grid=(AX_KV_HEADS, group, nk),
        in_specs=[pl.BlockSpec((tk, HEAD_DIM), kv), pl.BlockSpec((tk, HEAD_DIM), kv), pl.BlockSpec((T, HEAD_DIM), qh)],
        out_specs=[pl.BlockSpec((T, HEAD_DIM), qh), pl.BlockSpec((1, nq, 1, tq), lambda g, h, j: (g * group + h, 0, 0, 0))],
        out_shape=[_sds((T, D_MODEL), _MXU), _sds((AX_HEADS, nq, 1, tq), F32)],
        scratch_shapes=[pltpu.VMEM((HEAD_DIM, T), F32), pltpu.VMEM((nq, 1, tq), F32), pltpu.VMEM((nq, 1, tq), F32)],
        compiler_params=_params("parallel", "arbitrary", "arbitrary"),
    )(kx, vx, qx)
    return o, lse.reshape(AX_HEADS, 1, T)


def _flash_bwd(qx, kx, vx, o, do, lse, *, tq, tk):
    T = qx.shape[0]
    tq, tk = min(tq, T), min(tk, T)
    nq = T // tq
    group = AX_HEADS // AX_KV_HEADS
    lse_rows = lse.reshape(AX_HEADS, nq, 1, tq)

    def body(k_ref, v_ref, q_ref, o_ref, do_ref, lse_ref, dqt_ref, dk_ref, dv_ref, delta_sc):
        j = pl.program_id(2)

        @pl.when(jnp.logical_and(pl.program_id(1) == 0, j == 0))
        def _():
            dk_ref[...] = jnp.zeros_like(dk_ref)
            dv_ref[...] = jnp.zeros_like(dv_ref)

        @pl.when(j == 0)
        def _():
            dqt_ref[...] = jnp.zeros_like(dqt_ref)

            def row_delta(i, carry):
                rows = pl.ds(pl.multiple_of(i * tq, tq), tq)
                prod = do_ref[rows, :].astype(F32) * o_ref[rows, :].astype(F32)
                delta_sc[i] = jnp.sum(prod.T, axis=0, keepdims=True)
                return carry

            lax.fori_loop(0, nq, row_delta, 0)
        k, v = k_ref[...], v_ref[...]

        def step(i, carry):
            dk, dv = carry
            off = pl.multiple_of(i * tq, tq)
            q, do_blk = q_ref[pl.ds(off, tq), :], do_ref[pl.ds(off, tq), :]
            pt = jnp.exp2(_dot(k, q, _NT) * SCORE_SCALE_LOG2 - lse_ref[0, i])
            dst = pt * (_dot(v, do_blk, _NT) - delta_sc[i])
            dqt_ref[:, pl.ds(off, tq)] += _dot(k, dst, _TN) * ATT_SCALE
            return dk + _dot(dst, q), dv + _dot(pt, do_blk)

        zero = jnp.zeros((tk, HEAD_DIM), F32)
        dk, dv = lax.fori_loop(0, nq, step, (zero, zero))
        rows = pl.ds(pl.multiple_of(j * tk, tk), tk)
        dk_ref[rows, :] += dk * ATT_SCALE
        dv_ref[rows, :] += dv

    kv = lambda g, h, j: (j, g)
    qh = lambda g, h, j: (0, g * group + h)
    st = lambda g, h, j: (g * group + h, 0, 0, 0)
    acc = lambda g, h, j: (0, g)
    return pl.pallas_call(
        body, name="flash_bwd", grid=(AX_KV_HEADS, group, T // tk),
        in_specs=[pl.BlockSpec((tk, HEAD_DIM), kv), pl.BlockSpec((tk, HEAD_DIM), kv),
                  pl.BlockSpec((T, HEAD_DIM), qh), pl.BlockSpec((T, HEAD_DIM), qh), pl.BlockSpec((T, HEAD_DIM), qh),
                  pl.BlockSpec((1, nq, 1, tq), st)],
        out_specs=[pl.BlockSpec((HEAD_DIM, T), lambda g, h, j: (g * group + h, 0)),
                   pl.BlockSpec((T, HEAD_DIM), acc), pl.BlockSpec((T, HEAD_DIM), acc)],
        out_shape=[_sds((D_MODEL, T), F32), _sds((T, AX_KV_HEADS * HEAD_DIM), F32), _sds((T, AX_KV_HEADS * HEAD_DIM), F32)],
        scratch_shapes=[pltpu.VMEM((nq, 1, tq), F32)],
        compiler_params=_params("parallel", "arbitrary", "arbitrary"),
    )(kx, vx, qx, o, do, lse_rows)


TM = 1024
TM_WIDE = 512


def _mlp_fwd(tag, x, gain, w_up, w_down, target=None):
    u, h = _norm_matmul(f"mlp_up{tag}", x, gain, w_up, tm=TM, tn=1024, out_dtype=_MXU)
    out = _matmul_res(f"mlp_down{tag}", [u], w_down, x, tm=TM_WIDE, relu2=True, target=target)
    return out, (x, u, h)


def _local_step(x, target, p, w_first, fetch_rest, push, tokens=()):
    T = x.shape[0]
    cos_r, sin_r = _ret_rope_tables(T)
    cos_a, sin_a = _axial_rope_tables(T)
    tabs, rw, log_gamma = _retention_tables(p["ret_decay_logit"][0])
    bias = _swa_bias(p["t5_table"])
    sink = p["swa_sink"][0][:, None] * jnp.ones((1, HEAD_DIM), F32)
    nm, nl = p["norm_mix"], p["norm_mlp"]
    pending = [t for t in tokens if t is not None]

    def send(tag, weight, dw):
        token = push(tag, weight, dw[None])
        if token is not None:
            pending.append(token)

    def tied(operand):
        while pending:
            operand = operand + pending.pop()[0:1, 0:1]
        return operand

    def mlp_bwd(tag, saved, gain, w_up, w_down, dy, dy16):
        xs, u, h = saved
        du = _matmul_nt(f"mlp_down{tag}_bwd", dy16, w_down, tm=TM, tn=1024, out_dtype=_MXU, relu_of=u)
        send(f"mlp_down{tag}", "w_mlp_down", _matmul_tn(f"mlp_down{tag}_dw", u, dy16, tk=1024, tn=1024, tt=1024, out_dtype=_WIRE, relu2=True))
        dx, dx16, dgain = _matmul_nt_normbwd(f"mlp_up{tag}_bwd", du, w_up, xs, tied(gain), dy, tm=TM_WIDE)
        send(f"mlp_up{tag}", "w_mlp_up", _matmul_tn(f"mlp_up{tag}_dw", h, du, tk=1024, tn=1024, tt=1024, out_dtype=_WIRE))
        return dx, dx16, dgain

    proj0, h0 = _norm_matmul("in_even", x, tied(nm[0:1]), w_first["w_in_even"], tm=TM, tn=1152, out_dtype=F32)
    qr, kr, qn, kn = _prep_even(proj0, cos_r, sin_r, p["swa_q_norm"], p["swa_k_norm"], tm=TM)
    sf, sb = _ret_scan("ret_scan_fwd", kr, proj0, 1, rw["cf"], rw["dec_f"], rw["cb"], rw["dec_b"])
    ret_o, ya = _ret_out(qr, kr, proj0, sf, sb, tabs[:, (TAB_D, TAB_A, TAB_B)], p["ret_norm"])
    yb = _swa_fwd(qn, kn, proj0, bias, sink)
    wf = {**w_first, **fetch_rest(yb)}
    x1 = _matmul_res("out_even", [ya, yb], wf["w_out_even"], x, tm=TM)
    x2, mlp0 = _mlp_fwd(0, x1, nl[0:1], wf["w_mlp_up"][0], wf["w_mlp_down"][0])
    proj1, h1 = _norm_matmul("in_odd", x2, nm[1:2], wf["w_in_odd"], tm=TM, tn=768, out_dtype=F32)
    qx, kx, vx = _prep_odd(proj1, cos_a, sin_a, p["ax_q_norm"], p["ax_k_norm"], tm=TM)
    o, lse = _flash_fwd(qx, kx, vx, tq=2048, tk=1024)
    x3 = _matmul_res("out_odd", [o], wf["w_out_odd"], x2, tm=TM)
    (g4, g4_16, loss_part), mlp1 = _mlp_fwd(1, x3, nl[1:2], wf["w_mlp_up"][1], wf["w_mlp_down"][1], target=target)

    dx3, dx3_16, dnl1 = mlp_bwd(1, mlp1, nl[1:2], wf["w_mlp_up"][1], wf["w_mlp_down"][1], g4, g4_16)
    do = _matmul_nt("out_odd_bwd", dx3_16, wf["w_out_odd"], tm=TM, tn=1024, out_dtype=_MXU)
    send("out_odd", "w_out_odd", _matmul_tn("out_odd_dw", o, dx3_16, tk=1024, tn=1024, tt=1024, out_dtype=_WIRE))
    dqxt, dkx, dvx = _flash_bwd(qx, kx, vx, o, do, lse, tq=1024, tk=512)
    dproj1, dqg1, dkg1 = _post_odd(proj1, dqxt, dkx, dvx, cos_a, sin_a, tied(p["ax_q_norm"]), p["ax_k_norm"], tm=TM)
    send("in_odd", "w_in_odd", _matmul_tn("in_odd_dw", h1, dproj1, tk=1024, tn=768, tt=1024, out_dtype=_WIRE))
    dx2, dx2_16, dnm1 = _matmul_nt_normbwd("in_odd_bwd", dproj1, wf["w_in_odd"], x2, tied(nm[1:2]), dx3, tm=TM_WIDE)
    dx1, dx1_16, dnl0 = mlp_bwd(0, mlp0, nl[0:1], wf["w_mlp_up"][0], wf["w_mlp_down"][0], dx2, dx2_16)
    dycat = _matmul_nt("out_even_bwd", dx1_16, wf["w_out_even"], tm=TM, tn=1024, out_dtype=F32)
    send("out_even", "w_out_even", jnp.concatenate([
        _matmul_tn("out_even_dw_ret", ya, dx1_16, tk=1024, tn=1024, tt=1024, out_dtype=_WIRE),
        _matmul_tn("out_even_dw_swa", yb, dx1_16, tk=1024, tn=1024, tt=1024, out_dtype=_WIRE)], axis=0))
    g_out, dga, dretg = _ret_gate_bwd(dycat, proj0, ret_o, tied(p["ret_norm"]), tm=TM)
    rb, rf = _ret_scan("ret_scan_bwd", qr, g_out, 0, rw["b"], rw["dec_b"], rw["a"], rw["dec_f"])
    dqr, dkr, dva, dlog = _ret_bwd(qr, kr, proj0, g_out, sf, sb, rf, rb, tabs)
    dqn, dkn, dvb, dbias, dsink = _swa_bwd(qn, kn, proj0, dycat, bias, sink)
    dt5 = _t5_bucket_reduce(dbias, _t5_bucket(_swa_rel()).astype(jnp.int32))
    dproj0, dqg0, dkg0 = _post_even(proj0, dqr, dkr, dva, dga, dqn, dkn, dvb, cos_r, sin_r,
                                    p["swa_q_norm"], p["swa_k_norm"], tm=TM_WIDE)
    send("in_even", "w_in_even", _matmul_tn("in_even_dw", h0, dproj0, tk=1024, tn=1152, tt=1024, out_dtype=_WIRE))
    dx0, _, dnm0 = _matmul_nt_normbwd("in_even_bwd", dproj0, w_first["w_in_even"], x, tied(nm[0:1]), dx1, tm=TM_WIDE)

    fold = lambda part: jnp.sum(part, axis=0)
    dlam = jnp.sum(dlog, axis=1).reshape(RET_HEADS, 2).T
    small = {
        "norm_mix": jnp.stack([fold(dnm0), fold(dnm1)]),
        "norm_mlp": jnp.stack([fold(dnl0), fold(dnl1)]),
        "ret_decay_logit": (dlam * (1.0 - jnp.exp(log_gamma)))[None],
        "ret_norm": fold(dretg)[None],
        "swa_q_norm": fold(dqg0)[None], "swa_k_norm": fold(dkg0)[None],
        "swa_sink": dsink[:, 0][None],
        "t5_table": dt5[:, :T5_BUCKETS].T,
        "ax_q_norm": fold(dqg1)[None], "ax_k_norm": fold(dkg1)[None],
    }
    return loss_part, dx0, small


BIG = ("w_in_even", "w_out_even", "w_in_odd", "w_out_odd", "w_mlp_up", "w_mlp_down")
SMALL = ("norm_mix", "norm_mlp", "ret_decay_logit", "ret_norm", "swa_q_norm", "swa_k_norm", "swa_sink", "t5_table",
         "ax_q_norm", "ax_k_norm")
WEIGHTS = ("norm_mix", "norm_mlp", "w_in_even", "w_out_even", "ret_decay_logit", "ret_norm", "swa_q_norm", "swa_k_norm",
           "swa_sink", "t5_table", "w_in_odd", "w_out_odd", "ax_q_norm", "ax_k_norm", "w_mlp_up", "w_mlp_down")
SHARD_AXIS = {"w_in_even": 2, "w_out_even": 1, "w_in_odd": 2, "w_out_odd": 1, "w_mlp_up": 2, "w_mlp_down": 1}
N_CHIPS = 4
ANY = pl.BlockSpec(memory_space=pl.ANY)
HBM = pl.BlockSpec(memory_space=pltpu.HBM)
SEM = pl.BlockSpec(memory_space=pltpu.SEMAPHORE)
SPLIT_COPY = pltpu.CompilerParams(has_side_effects=pltpu.SideEffectType.DATAFLOW_SIDE_EFFECTING)


def _in_hbm(a):
    return pltpu.with_memory_space_constraint(a, pltpu.HBM)


def _mesh_pos():
    return lax.axis_index("x"), lax.axis_index("y"), lax.axis_index("c")


def _window(ref, axis, start, size):
    idx = [slice(None)] * len(ref.shape)
    idx[axis] = pl.ds(start, size)
    return ref.at[tuple(idx)]


def _cast_place(key, shard, chip, *, tr=256):
    L, R, C = shard.shape
    tr = min(tr, R)
    axis = SHARD_AXIS[key]
    whole = tuple(d * (N_CHIPS if a == axis else 1) for a, d in enumerate(shard.shape))

    def body(chip_ref, s_ref, o_ref):
        o_ref[...] = s_ref[...].astype(o_ref.dtype)

    if axis == 2:
        out_map = lambda l, i, chip_ref: (l, i, chip_ref[0])
    else:
        out_map = lambda l, i, chip_ref: (l, i + chip_ref[0] * (R // tr), 0)
    grid_spec = pltpu.PrefetchScalarGridSpec(
        num_scalar_prefetch=1, grid=(L, R // tr),
        in_specs=[pl.BlockSpec((1, tr, C), lambda l, i, chip_ref: (l, i, 0))],
        out_specs=pl.BlockSpec((1, tr, C), out_map))
    return pl.pallas_call(body, name=f"cast_place_{key}", grid_spec=grid_spec, out_shape=_sds(whole, _MXU),
                          compiler_params=_params("parallel", "parallel"))(chip, shard)


def _gather_copies(names, refs, send_sems, recv_sems, *, outgoing=True, incoming=True):
    x, y, c = _mesh_pos()
    chips = [(1 - x, y), (x, 1 - y), (1 - x, 1 - y)]
    out, inc = [], []
    for t, key in enumerate(names):
        size = refs[t].shape[SHARD_AXIS[key]] // N_CHIPS
        slot = lambda px, py: _window(refs[t], SHARD_AXIS[key], pl.multiple_of((2 * px + py) * size, 128), size)
        for k, (px, py) in enumerate(chips):
            sems = dict(send_sem=send_sems.at[3 * t + k], recv_sem=recv_sems.at[3 * t + k], device_id=(px, py, c), device_id_type=MESH)
            if outgoing:
                out.append(pltpu.make_async_remote_copy(slot(x, y), slot(x, y), **sems))
            if incoming:
                inc.append(pltpu.make_async_remote_copy(slot(x, y), slot(px, py), **sems))
    return out, inc


def _allgather_start(groups):
    names = [list(g) for g in groups]
    flat = [g[k] for g in groups for k in g]
    n, ng = len(flat), len(groups)

    def body(*refs):
        start = 0
        for gi, keys in enumerate(names):
            copies, _ = _gather_copies(keys, refs[start:start + len(keys)], refs[n + 2 * gi], refs[n + 2 * gi + 1], incoming=False)
            for cp in copies:
                cp.start()
            start += len(keys)
        token = refs[-1]
        token[...] = jnp.zeros_like(token)

    sem_shapes = [pltpu.SemaphoreType.DMA((3 * len(keys),)) for keys in names for _ in (0, 1)]
    outs = pl.pallas_call(
        body, name="allgather_start", in_specs=[HBM] * n,
        out_specs=[SEM] * (2 * ng) + [HBM] * n + [pl.BlockSpec(memory_space=pltpu.VMEM)],
        out_shape=sem_shapes + [pltpu.HBM(a.shape, a.dtype) for a in flat] + [_sds((8, HEAD_DIM), F32)],
        input_output_aliases={t: 2 * ng + t for t in range(n)},
        compiler_params=SPLIT_COPY,
    )(*[_in_hbm(a) for a in flat])
    states, start = [], 2 * ng
    for gi, keys in enumerate(names):
        states.append((gi, keys, outs[2 * gi], outs[2 * gi + 1], outs[start:start + len(keys)]))
        start += len(keys)
    return states, outs[-1]


def _allgather_wait(state, after):
    gi, names, send_sems, recv_sems, thru = state
    n = len(names)

    def body(*refs):
        outgoing, incoming = _gather_copies(names, refs[:n], refs[n], refs[n + 1])
        for cp in outgoing:
            cp.wait_send()
        for cp in incoming:
            cp.wait_recv()

    outs = pl.pallas_call(
        body, name=f"allgather_wait_{gi}", in_specs=[HBM] * n + [SEM, SEM, ANY], out_specs=[HBM] * n,
        out_shape=[pltpu.HBM(t.shape, t.dtype) for t in thru],
        input_output_aliases={t: t for t in range(n)},
        compiler_params=SPLIT_COPY,
    )(*thru, send_sems, recv_sems, after)
    return dict(zip(names, outs))


FLIPS = [(a, b, d) for a in (0, 1) for b in (0, 1) for d in (0, 1) if (a, b, d) != (0, 0, 0)]


def _flip(pos, f):
    return tuple(1 - p if fi else p for p, fi in zip(pos, f))


def _piece_shape(weight, shape):
    out = list(shape)
    out[SHARD_AXIS[weight]] //= N_CHIPS
    out[1] //= 2
    return tuple(out)


def _piece(ref, weight, chip, core):
    piece = _piece_shape(weight, ref.shape)
    if SHARD_AXIS[weight] == 1:
        return _window(ref, 1, pl.multiple_of((2 * chip + core) * piece[1], 8), piece[1])
    return _window(_window(ref, 2, pl.multiple_of(chip * piece[2], 128), piece[2]), 1, pl.multiple_of(core * piece[1], 8), piece[1])


def _scatter_copies(weight, grad_ref, land_ref, send_sems, recv_sems, *, outgoing=True, incoming=True):
    pos = _mesh_pos()
    out, inc = [], []
    for k, f in enumerate(FLIPS):
        peer = _flip(pos, f)
        sems = dict(send_sem=send_sems.at[k], recv_sem=recv_sems.at[k], device_id=peer, device_id_type=MESH)
        if outgoing:
            out.append(pltpu.make_async_remote_copy(_piece(grad_ref, weight, 2 * peer[0] + peer[1], peer[2]), land_ref.at[k], **sems))
        if incoming:
            inc.append(pltpu.make_async_remote_copy(_piece(grad_ref, weight, 2 * pos[0] + pos[1], pos[2]), land_ref.at[k], **sems))
    return out, inc


def _scatter_start(tag, weight, grad):
    n_peer = len(FLIPS)
    land = lax.empty((n_peer,) + _piece_shape(weight, grad.shape), grad.dtype)

    def body(grad_ref, land_ref, send_sems, recv_sems, grad_thru, land_thru, token):
        copies, _ = _scatter_copies(weight, grad_ref, land_ref, send_sems, recv_sems, incoming=False)
        for cp in copies:
            cp.start()
        token[...] = jnp.zeros_like(token)

    outs = pl.pallas_call(
        body, name=f"scatter_start_{tag}", in_specs=[HBM, HBM],
        out_specs=[SEM, SEM, HBM, HBM, pl.BlockSpec(memory_space=pltpu.VMEM)],
        out_shape=[pltpu.SemaphoreType.DMA((n_peer,)), pltpu.SemaphoreType.DMA((n_peer,)),
                   pltpu.HBM(grad.shape, grad.dtype), pltpu.HBM(land.shape, land.dtype), _sds((8, HEAD_DIM), F32)],
        input_output_aliases={0: 2, 1: 3},
        compiler_params=SPLIT_COPY,
    )(_in_hbm(grad), _in_hbm(land))
    return (tag, weight, outs[:4]), outs[4]


def _scatter_wait(state, after):
    tag, weight, (send_sems, recv_sems, grad_thru, land_thru) = state

    def body(grad_ref, land_ref, send_ref, recv_ref, after_ref, grad_out, land_out):
        outgoing, incoming = _scatter_copies(weight, grad_ref, land_ref, send_ref, recv_ref)
        for cp in outgoing:
            cp.wait_send()
        for cp in incoming:
            cp.wait_recv()

    return pl.pallas_call(
        body, name=f"scatter_wait_{tag}", in_specs=[HBM, HBM, SEM, SEM, ANY], out_specs=[HBM, HBM],
        out_shape=[pltpu.HBM(grad_thru.shape, grad_thru.dtype), pltpu.HBM(land_thru.shape, land_thru.dtype)],
        input_output_aliases={0: 0, 1: 1},
        compiler_params=SPLIT_COPY,
    )(grad_thru, land_thru, send_sems, recv_sems, after)


def _sum_pieces(tag, weight, grad, land, where, *, tr=128):
    _, R, C = _piece_shape(weight, grad.shape)
    tr = min(tr, R)
    nr = R // tr

    def body(where_ref, g_ref, l_ref, o_ref):
        acc = g_ref[...].astype(F32)
        for s in range(len(FLIPS)):
            acc = acc + l_ref[s].astype(F32)
        o_ref[...] = acc

    if SHARD_AXIS[weight] == 1:
        own = lambda i, where_ref: (0, (2 * where_ref[0] + where_ref[1]) * nr + i, 0)
    else:
        own = lambda i, where_ref: (0, where_ref[1] * nr + i, where_ref[0])
    grid_spec = pltpu.PrefetchScalarGridSpec(
        num_scalar_prefetch=1, grid=(nr,),
        in_specs=[pl.BlockSpec((1, tr, C), own), pl.BlockSpec((len(FLIPS), 1, tr, C), lambda i, where_ref: (0, 0, i, 0))],
        out_specs=pl.BlockSpec((1, tr, C), lambda i, where_ref: (0, where_ref[1] * nr + i, 0)))
    return pl.pallas_call(body, name=f"sum_{tag}", grid_spec=grid_spec, out_shape=_sds((1, 2 * R, C), F32),
                          compiler_params=_params("parallel"))(where, grad, land)


def _exchange_halves(shards):
    names = list(shards)
    n = len(names)
    half_sizes = [shards[k].shape[1] // 2 for k in names]

    def body(*refs):
        outs = refs[n:2 * n]
        send_sems, recv_sems = refs[2 * n:]
        x, y, c = _mesh_pos()
        half = lambda t, core: _window(outs[t], 1, pl.multiple_of(core * half_sizes[t], 8), half_sizes[t])
        sends = []
        for t in range(n):
            sends.append(pltpu.make_async_remote_copy(half(t, c), half(t, c), send_sems.at[t], recv_sems.at[t],
                                                      device_id=(x, y, 1 - c), device_id_type=MESH))
            sends[-1].start()
        for t in range(n):
            pltpu.make_async_remote_copy(half(t, c), half(t, 1 - c), send_sems.at[t], recv_sems.at[t],
                                         device_id=(x, y, 1 - c), device_id_type=MESH).wait_recv()
        for cp in sends:
            cp.wait_send()

    outs = pl.pallas_call(
        body, name="exchange_halves", in_specs=[ANY] * n, out_specs=[ANY] * n,
        out_shape=[_sds(shards[k].shape, F32) for k in names],
        input_output_aliases={t: t for t in range(n)},
        scratch_shapes=[pltpu.SemaphoreType.DMA((n,)), pltpu.SemaphoreType.DMA((n,))],
    )(*[shards[k] for k in names])
    return dict(zip(names, outs))


def _adamw_math(w, g, m, v):
    m = ADAM_B1 * m + (1.0 - ADAM_B1) * g
    v = ADAM_B2 * v + (1.0 - ADAM_B2) * jnp.square(g)
    m_hat = m / (1.0 - ADAM_B1 ** ADAM_STEP)
    v_hat = v / (1.0 - ADAM_B2 ** ADAM_STEP)
    return -ADAM_LR * (m_hat / (jnp.sqrt(v_hat) + ADAM_EPS) + ADAM_WD * w), m, v


def _adamw(name, w, g, m, v, *, tr=256):
    R, C = w.shape
    tr = min(tr, R)

    def body(w_ref, g_ref, m_ref, v_ref, d_ref, mo_ref, vo_ref):
        d_ref[...], mo_ref[...], vo_ref[...] = _adamw_math(w_ref[...], g_ref[...], m_ref[...], v_ref[...])

    spec = pl.BlockSpec((tr, C), lambda i: (i, 0))
    return pl.pallas_call(body, name=name, grid=(R // tr,), in_specs=[spec] * 4, out_specs=[spec] * 3,
                          out_shape=[_sds((R, C), F32)] * 3, compiler_params=_params("parallel"))(w, g, m, v)


SLAB_ROWS = 8
LOSS_ROW = 7


def _pack_small(d):
    pad = lambda a, width: jnp.pad(a.reshape(-1), (0, width - a.size))
    row5 = jnp.concatenate([d["swa_q_norm"].reshape(-1), d["swa_k_norm"].reshape(-1), d["ax_q_norm"].reshape(-1),
                            d["ax_k_norm"].reshape(-1), pad(d["swa_sink"], HEAD_DIM), pad(d["ret_decay_logit"], HEAD_DIM),
                            jnp.zeros((2 * HEAD_DIM,), F32)])
    return jnp.concatenate([d["norm_mix"], d["norm_mlp"], d["ret_norm"], row5[None], pad(d["t5_table"], D_MODEL)[None],
                            jnp.zeros((1, D_MODEL), F32)], axis=0)


def _unpack_small(slab):
    r5 = slab[5]
    return {
        "norm_mix": slab[0:2], "norm_mlp": slab[2:4], "ret_norm": slab[4:5],
        "swa_q_norm": r5[None, 0:128], "swa_k_norm": r5[None, 128:256], "ax_q_norm": r5[None, 256:384],
        "ax_k_norm": r5[None, 384:512], "swa_sink": r5[None, 512:512 + SWA_HEADS],
        "ret_decay_logit": r5[640:640 + 2 * RET_HEADS].reshape(1, 2, RET_HEADS),
        "t5_table": slab[6, :T5_BUCKETS * SWA_HEADS].reshape(T5_BUCKETS, SWA_HEADS),
    }


def _small_allreduce_adamw(g_slab, w_slab, m_slab, v_slab, loss_part):
    def body(g_ref, w_ref, m_ref, v_ref, lp_ref, go_ref, d_ref, mo_ref, vo_ref, gath, send_sems, recv_sems):
        pos = _mesh_pos()
        ident = lambda p: 4 * p[0] + 2 * p[1] + p[2]
        me = ident(pos)
        row = lax.broadcasted_iota(jnp.int32, (SLAB_ROWS, D_MODEL), 0)
        lane = lax.broadcasted_iota(jnp.int32, (SLAB_ROWS, D_MODEL), 1)
        loss = jnp.sum(jnp.sum(lp_ref[...], axis=0, keepdims=True), axis=1, keepdims=True) * (0.5 / D_MODEL)
        gath[me] = jnp.where(jnp.logical_and(row == LOSS_ROW, lane == 0), loss, g_ref[...])
        sends = []
        for k, f in enumerate(FLIPS):
            sends.append(pltpu.make_async_remote_copy(gath.at[me], gath.at[me], send_sems.at[k], recv_sems.at[k],
                                                      device_id=_flip(pos, f), device_id_type=MESH))
            sends[-1].start()
        for k, f in enumerate(FLIPS):
            peer = _flip(pos, f)
            pltpu.make_async_remote_copy(gath.at[me], gath.at[ident(peer)], send_sems.at[k], recv_sems.at[k],
                                         device_id=peer, device_id_type=MESH).wait_recv()
        for cp in sends:
            cp.wait_send()
        total = gath[0]
        for s in range(1, N_DEV):
            total = total + gath[s]
        go_ref[...] = total
        d_ref[...], mo_ref[...], vo_ref[...] = _adamw_math(w_ref[...], total, m_ref[...], v_ref[...])

    vmem = pl.BlockSpec(memory_space=pltpu.VMEM)
    return pl.pallas_call(
        body, name="small_allreduce_adamw", in_specs=[vmem] * 5, out_specs=[vmem] * 4,
        out_shape=[_sds((SLAB_ROWS, D_MODEL), F32)] * 4,
        scratch_shapes=[pltpu.VMEM((N_DEV, SLAB_ROWS, D_MODEL), F32),
                        pltpu.SemaphoreType.DMA((len(FLIPS),)), pltpu.SemaphoreType.DMA((len(FLIPS),))],
    )(g_slab, w_slab, m_slab, v_slab, loss_part)


def kernel(x, norm_mix, norm_mlp, w_in_even, w_out_even, ret_decay_logit, ret_norm, swa_q_norm, swa_k_norm, swa_sink, t5_table, w_in_odd, w_out_odd, ax_q_norm, ax_k_norm, w_mlp_up, w_mlp_down, loss_target, m_norm_mix, m_norm_mlp, m_w_in_even, m_w_out_even, m_ret_decay_logit, m_ret_norm, m_swa_q_norm, m_swa_k_norm, m_swa_sink, m_t5_table, m_w_in_odd, m_w_out_odd, m_ax_q_norm, m_ax_k_norm, m_w_mlp_up, m_w_mlp_down, v_norm_mix, v_norm_mlp, v_w_in_even, v_w_out_even, v_ret_decay_logit, v_ret_norm, v_swa_q_norm, v_swa_k_norm, v_swa_sink, v_t5_table, v_w_in_odd, v_w_out_odd, v_ax_q_norm, v_ax_k_norm, v_w_mlp_up, v_w_mlp_down):
    w = dict(zip(WEIGHTS, (norm_mix, norm_mlp, w_in_even, w_out_even, ret_decay_logit, ret_norm, swa_q_norm, swa_k_norm,
                           swa_sink, t5_table, w_in_odd, w_out_odd, ax_q_norm, ax_k_norm, w_mlp_up, w_mlp_down)))
    m = dict(zip(WEIGHTS, (m_norm_mix, m_norm_mlp, m_w_in_even, m_w_out_even, m_ret_decay_logit, m_ret_norm, m_swa_q_norm,
                           m_swa_k_norm, m_swa_sink, m_t5_table, m_w_in_odd, m_w_out_odd, m_ax_q_norm, m_ax_k_norm,
                           m_w_mlp_up, m_w_mlp_down)))
    v = dict(zip(WEIGHTS, (v_norm_mix, v_norm_mlp, v_w_in_even, v_w_out_even, v_ret_decay_logit, v_ret_norm, v_swa_q_norm,
                           v_swa_k_norm, v_swa_sink, v_t5_table, v_w_in_odd, v_w_out_odd, v_ax_q_norm, v_ax_k_norm,
                           v_w_mlp_up, v_w_mlp_down)))
    flat = lambda a: a.reshape(-1, a.shape[-1])

    chip = (2 * lax.axis_index("x") + lax.axis_index("y")).astype(jnp.int32)
    where = jnp.stack([chip, lax.axis_index("c").astype(jnp.int32)])

    placed = {k: _cast_place(k, w[k], where[0:1]) for k in BIG}
    (gather_first, gather_rest), gather_token = _allgather_start(
        [{"w_in_even": placed["w_in_even"]}, {k: placed[k] for k in BIG if k != "w_in_even"}])
    unstack = lambda whole: {k: (a if k.startswith("w_mlp") else a[0]) for k, a in whole.items()}

    in_flight = []

    def push(tag, weight, dw):
        state, token = _scatter_start(tag, weight, dw)
        in_flight.append(state)
        return token

    loss_part, dx, small_g = _local_step(x[0], loss_target[0], {k: w[k] for k in SMALL},
                                         unstack(_allgather_wait(gather_first, gather_token)),
                                         lambda after: unstack(_allgather_wait(gather_rest, after)), push)

    halves = {}
    for state in in_flight:
        tag, weight = state[0], state[1]
        dw, land = _scatter_wait(state, dx)
        halves[tag] = _sum_pieces(tag, weight, dw, land, where)
    reduced = _exchange_halves(halves)
    grad = {"w_in_even": reduced["in_even"], "w_out_even": reduced["out_even"],
            "w_in_odd": reduced["in_odd"], "w_out_odd": reduced["out_odd"],
            "w_mlp_up": jnp.concatenate([reduced["mlp_up0"], reduced["mlp_up1"]], axis=0),
            "w_mlp_down": jnp.concatenate([reduced["mlp_down0"], reduced["mlp_down1"]], axis=0)}
    delta, new_m, new_v = {}, {}, {}
    for k in BIG:
        d_k, m_k, v_k = _adamw(f"adamw_{k}", flat(w[k]), flat(grad[k]), flat(m[k]), flat(v[k]))
        delta[k], new_m[k], new_v[k] = d_k.reshape(w[k].shape), m_k.reshape(w[k].shape), v_k.reshape(w[k].shape)

    slabs = _small_allreduce_adamw(_pack_small(small_g), _pack_small({k: w[k] for k in SMALL}),
                                   _pack_small({k: m[k] for k in SMALL}), _pack_small({k: v[k] for k in SMALL}), loss_part)
    loss = slabs[0][LOSS_ROW, 0]
    for out, slab in zip((grad, delta, new_m, new_v), slabs):
        out.update(_unpack_small(slab))

    return (loss, dx[None], *[grad[k] for k in WEIGHTS], *[delta[k] for k in WEIGHTS],
            *[new_m[k] for k in WEIGHTS], *[new_v[k] for k in WEIGHTS])
```

```python
import functools
import math

import jax
import jax.numpy as jnp
from jax import lax
from jax.experimental import pallas as pl
from jax.experimental.pallas import tpu as pltpu

F32 = jnp.float32
BF16 = jnp.bfloat16
_MXU = BF16
_WIRE = BF16

D_MODEL = 1024
HEAD_DIM = 128
EPS = 1e-6
NEG_INF = -1e30
CHUNK = 128
GRID_W = 64
RET_HEADS, RET_DK, RET_DV = 4, 128, 256
RET_Q, RET_V = RET_HEADS * RET_DK, RET_HEADS * RET_DV
RET_THETA = 10000.0
SWA_HEADS, SWA_KV_HEADS = 8, 2
T5_BUCKETS, T5_MAX_DIST = 32, 128
AX_HEADS, AX_KV_HEADS = 8, 2
AX_THETA = 10000.0
D_FF = 4 * D_MODEL
EVEN_IN = 2 * RET_Q + 2 * RET_V + D_MODEL + 2 * SWA_KV_HEADS * HEAD_DIM
ODD_IN = D_MODEL + 2 * AX_KV_HEADS * HEAD_DIM
ATT_SCALE = HEAD_DIM ** -0.5
SCORE_SCALE_LOG2 = ATT_SCALE * math.log2(math.e)

ADAM_LR, ADAM_B1, ADAM_B2, ADAM_EPS, ADAM_WD, ADAM_STEP = 0.001, 0.9, 0.999, 1e-08, 0.01, 10

N_DEV = 8
VMEM_LIMIT_BYTES = 56 << 20
MESH = pl.DeviceIdType.MESH

_NN = (((1,), (0,)), ((), ()))
_NT = (((1,), (1,)), ((), ()))
_TN = (((0,), (0,)), ((), ()))


def _dot(a, b, dn=_NN):
    return lax.dot_general(a.astype(_MXU), b.astype(_MXU), dn, preferred_element_type=F32)


def _params(*sem):
    return pltpu.CompilerParams(dimension_semantics=sem, vmem_limit_bytes=VMEM_LIMIT_BYTES)


def _sds(shape, dtype):
    return jax.ShapeDtypeStruct(tuple(shape), dtype)


def _rowsum8(x):
    return jnp.sum(x.reshape(x.shape[0] // 8, 8, x.shape[1]), axis=0)


def _swap_halves(x, half):
    width = x.shape[1]
    lane = lax.broadcasted_iota(jnp.int32, x.shape, 1)
    up = pltpu.roll(x, width - half, axis=1)
    down = pltpu.roll(x, half, axis=1)
    return jnp.where((lane & (2 * half - 1)) < half, up, down)


def _sigmoid(x):
    return 1.0 / (1.0 + jnp.exp(-x))


def _norm_matmul(name, x, gain, w, *, tm, tn, out_dtype):
    T, K = x.shape
    N = w.shape[1]
    tm, tn = min(tm, T), min(tn, N)

    def body(x_ref, g_ref, w_ref, y_ref, h_ref, h_sc):
        @pl.when(pl.program_id(1) == 0)
        def _():
            xv = x_ref[...]
            r = lax.rsqrt(jnp.mean(xv * xv, axis=-1, keepdims=True) + EPS)
            h = (xv * r * g_ref[...]).astype(_MXU)
            h_sc[...] = h
            h_ref[...] = h
        y_ref[...] = jnp.dot(h_sc[...], w_ref[...], preferred_element_type=F32).astype(y_ref.dtype)

    return pl.pallas_call(
        body, name=name, grid=(T // tm, N // tn),
        in_specs=[pl.BlockSpec((tm, K), lambda i, j: (i, 0)),
                  pl.BlockSpec((1, K), lambda i, j: (0, 0)),
                  pl.BlockSpec((K, tn), lambda i, j: (0, j))],
        out_specs=[pl.BlockSpec((tm, tn), lambda i, j: (i, j)),
                   pl.BlockSpec((tm, K), lambda i, j: (i, 0))],
        out_shape=[_sds((T, N), out_dtype), _sds((T, K), _MXU)],
        scratch_shapes=[pltpu.VMEM((tm, K), _MXU)],
        compiler_params=_params("parallel", "arbitrary"),
    )(x, gain, w)


def _matmul_res(name, a_list, w, res, *, tm, relu2=False, target=None):
    T = res.shape[0]
    N = w.shape[1]
    K = a_list[0].shape[1]
    n_a = len(a_list)
    tm = min(tm, T)
    with_loss = target is not None

    def body(*refs):
        a_refs = refs[:n_a]
        w_refs = refs[n_a:2 * n_a]
        res_ref = refs[2 * n_a]
        acc = res_ref[...]
        for a_ref, w_ref in zip(a_refs, w_refs):
            a = a_ref[...]
            if relu2:
                a = jnp.square(jnp.maximum(a.astype(F32), 0.0))
            acc = acc + _dot(a, w_ref[...])
        if with_loss:
            tgt_ref, g_ref, g16_ref, loss_ref = refs[2 * n_a + 1:]
            diff = acc - tgt_ref[...]
            g = diff * (1.0 / N)
            g_ref[...] = g
            g16_ref[...] = g.astype(g16_ref.dtype)

            @pl.when(pl.program_id(0) == 0)
            def _():
                loss_ref[...] = jnp.zeros_like(loss_ref)
            loss_ref[...] += _rowsum8(diff * diff)
        else:
            refs[2 * n_a + 1][...] = acc

    row = lambda i: (i, 0)
    in_specs = [pl.BlockSpec((tm, K), row) for _ in a_list]
    in_specs += [pl.BlockSpec((K, N), functools.partial(lambda i, b: (b, 0), b=b)) for b in range(n_a)]
    in_specs += [pl.BlockSpec((tm, N), row)]
    args = list(a_list) + [w] * n_a + [res]
    if with_loss:
        in_specs.append(pl.BlockSpec((tm, N), row))
        args.append(target)
        out_specs = [pl.BlockSpec((tm, N), row), pl.BlockSpec((tm, N), row), pl.BlockSpec((8, N), lambda i: (0, 0))]
        out_shape = [_sds((T, N), F32), _sds((T, N), _MXU), _sds((8, N), F32)]
        sem = "arbitrary"
    else:
        out_specs = pl.BlockSpec((tm, N), row)
        out_shape = _sds((T, N), F32)
        sem = "parallel"
    return pl.pallas_call(body, name=name, grid=(T // tm,), in_specs=in_specs, out_specs=out_specs,
                          out_shape=out_shape, compiler_params=_params(sem))(*args)


def _matmul_nt(name, a, w, *, tm, tn, out_dtype, relu_of=None):
    T, K = a.shape
    N = w.shape[0]
    tm, tn = min(tm, T), min(tn, N)

    def body(*refs):
        if relu_of is None:
            a_ref, w_ref, o_ref = refs
            o_ref[...] = _dot(a_ref[...], w_ref[...], _NT).astype(o_ref.dtype)
        else:
            a_ref, w_ref, u_ref, o_ref = refs
            da = _dot(a_ref[...], w_ref[...], _NT)
            o_ref[...] = (da * (2.0 * jnp.maximum(u_ref[...].astype(F32), 0.0))).astype(o_ref.dtype)

    in_specs = [pl.BlockSpec((tm, K), lambda i, j: (i, 0)), pl.BlockSpec((tn, K), lambda i, j: (j, 0))]
    args = [a, w]
    if relu_of is not None:
        in_specs.append(pl.BlockSpec((tm, tn), lambda i, j: (i, j)))
        args.append(relu_of)
    return pl.pallas_call(body, name=name, grid=(T // tm, N // tn), in_specs=in_specs,
                          out_specs=pl.BlockSpec((tm, tn), lambda i, j: (i, j)),
                          out_shape=_sds((T, N), out_dtype),
                          compiler_params=_params("parallel", "parallel"))(*args)


def _matmul_nt_normbwd(name, dy, w, x, gain, dres, *, tm):
    T, K = dy.shape
    N = w.shape[0]
    tm = min(tm, T)

    def body(dy_ref, w_ref, x_ref, g_ref, dres_ref, dx_ref, dx16_ref, dg_ref):
        dh = _dot(dy_ref[...], w_ref[...], _NT)
        xv = x_ref[...]
        r = lax.rsqrt(jnp.mean(xv * xv, axis=-1, keepdims=True) + EPS)
        xhat = xv * r
        dxhat = dh * g_ref[...]
        dx = dres_ref[...] + r * (dxhat - xhat * jnp.mean(dxhat * xhat, axis=-1, keepdims=True))
        dx_ref[...] = dx
        dx16_ref[...] = dx.astype(dx16_ref.dtype)

        @pl.when(pl.program_id(0) == 0)
        def _():
            dg_ref[...] = jnp.zeros_like(dg_ref)
        dg_ref[...] += _rowsum8(dh * xhat)

    row = lambda i: (i, 0)
    return pl.pallas_call(
        body, name=name, grid=(T // tm,),
        in_specs=[pl.BlockSpec((tm, K), row), pl.BlockSpec((N, K), lambda i: (0, 0)),
                  pl.BlockSpec((tm, N), row), pl.BlockSpec((1, N), lambda i: (0, 0)), pl.BlockSpec((tm, N), row)],
        out_specs=[pl.BlockSpec((tm, N), row), pl.BlockSpec((tm, N), row), pl.BlockSpec((8, N), lambda i: (0, 0))],
        out_shape=[_sds((T, N), F32), _sds((T, N), _MXU), _sds((8, N), F32)],
        compiler_params=_params("arbitrary"),
    )(dy, w, x, gain, dres)


def _matmul_tn(name, a, b, *, tk, tn, tt, out_dtype, relu2=False):
    T, Ka = a.shape
    Nb = b.shape[1]
    tk, tn, tt = min(tk, Ka), min(tn, Nb), min(tt, T)
    nt = T // tt

    def body(a_ref, b_ref, o_ref, acc):
        t = pl.program_id(2)

        @pl.when(t == 0)
        def _():
            acc[...] = jnp.zeros_like(acc)
        av = a_ref[...]
        if relu2:
            av = jnp.square(jnp.maximum(av.astype(F32), 0.0))
        acc[...] += _dot(av, b_ref[...], _TN)

        @pl.when(t == nt - 1)
        def _():
            o_ref[...] = acc[...].astype(o_ref.dtype)

    return pl.pallas_call(
        body, name=name, grid=(Ka // tk, Nb // tn, nt),
        in_specs=[pl.BlockSpec((tt, tk), lambda i, j, t: (t, i)), pl.BlockSpec((tt, tn), lambda i, j, t: (t, j))],
        out_specs=pl.BlockSpec((tk, tn), lambda i, j, t: (i, j)),
        out_shape=_sds((Ka, Nb), out_dtype),
        scratch_shapes=[pltpu.VMEM((tk, tn), F32)],
        compiler_params=_params("parallel", "parallel", "arbitrary"),
    )(a, b)


def _rope_angles(pos, dim, theta):
    inv = theta ** (-jnp.arange(0, dim, 2, dtype=F32) / dim)
    return pos.astype(F32)[:, None] * inv[None, :]


def _ret_rope_tables(T):
    ang = _rope_angles(jnp.arange(T), RET_DK, RET_THETA)
    c, s = jnp.cos(ang), jnp.sin(ang)
    return jnp.concatenate([c, c], axis=1), jnp.concatenate([-s, s], axis=1)


def _axial_rope_tables(T):
    rows = T // GRID_W
    ar = _rope_angles(jnp.arange(rows), HEAD_DIM // 2, AX_THETA)
    ac = _rope_angles(jnp.arange(GRID_W), HEAD_DIM // 2, AX_THETA)
    by_row = lambda a: jnp.repeat(a, GRID_W, axis=0)
    by_col = lambda a: jnp.tile(a, (rows, 1))
    cos = jnp.concatenate([by_row(jnp.cos(ar)), by_row(jnp.cos(ar)), by_col(jnp.cos(ac)), by_col(jnp.cos(ac))], axis=1)
    sin = jnp.concatenate([by_row(-jnp.sin(ar)), by_row(jnp.sin(ar)), by_col(-jnp.sin(ac)), by_col(jnp.sin(ac))], axis=1)
    return cos, sin


(TAB_D, TAB_DT, TAB_EF, TAB_EB, TAB_A, TAB_B, TAB_CF, TAB_CB,
 TAB_RA, TAB_RB, TAB_RCF, TAB_RCB, TAB_KF, TAB_KB) = range(14)


def _retention_tables(decay_logit):
    lg = jax.nn.log_sigmoid(decay_logit.astype(F32))
    lam, mu = lg[0][:, None, None], lg[1][:, None, None]
    idx = jnp.arange(CHUNK, dtype=F32)
    diff = (idx[:, None] - idx[None, :])[None]
    df = jnp.where(diff >= 0, jnp.exp(jnp.maximum(diff, 0.0) * lam), 0.0)
    db = jnp.where(diff < 0, jnp.exp(jnp.maximum(-diff, 0.0) * mu), 0.0)
    d = df + db
    r = idx[None, :, None]
    ones = jnp.ones((1, 1, CHUNK), F32)
    a = jnp.exp((r + 1.0) * lam) * ones
    b = jnp.exp((CHUNK - r) * mu) * ones
    cf = jnp.exp((CHUNK - 1.0 - r) * lam) * ones
    cb = jnp.exp(r * mu) * ones
    full = jnp.ones((1, CHUNK, CHUNK), F32)
    kf = CHUNK * jnp.exp(CHUNK * lam) * full
    kb = CHUNK * jnp.exp(CHUNK * mu) * full
    tabs = jnp.stack([d, jnp.swapaxes(d, 1, 2), diff * df, -diff * db, a, b, cf, cb,
                      (r + 1.0) * a, (CHUNK - r) * b, (CHUNK - 1.0 - r) * cf, r * cb, kf, kb], axis=1)

    def lanes(tab):
        return jnp.transpose(tab, (1, 0, 2)).reshape(CHUNK, RET_HEADS * CHUNK)

    def dec(l):
        return jnp.exp(CHUNK * l)[:, 0, :] * jnp.ones((1, RET_DV), F32)

    weights = dict(a=lanes(a), b=lanes(b), cf=lanes(cf), cb=lanes(cb), dec_f=dec(lam), dec_b=dec(mu))
    return tabs, weights, lg


def _t5_bucket(rel):
    nb = T5_BUCKETS // 2
    max_exact = nb // 2
    ret = jnp.where(rel > 0, nb, 0)
    n = jnp.abs(rel)
    nf = jnp.maximum(n, 1).astype(F32)
    large = max_exact + (jnp.log(nf / max_exact) / math.log(T5_MAX_DIST / max_exact)
                         * (nb - max_exact)).astype(jnp.int32)
    large = jnp.minimum(large, nb - 1)
    return ret + jnp.where(n < max_exact, n, large)


def _swa_rel():
    r = jnp.arange(CHUNK)
    j = jnp.arange(3 * CHUNK)
    return j[None, :] - CHUNK - r[:, None]


def _swa_bias(t5_table):
    rel = _swa_rel()
    bucket = jnp.where(jnp.abs(rel) <= CHUNK, _t5_bucket(rel), -1).astype(jnp.int32)

    def body(tab_ref, bk_ref, o_ref):
        bk = bk_ref[...]
        for h in range(SWA_HEADS):
            pick = lambda b, acc, h=h: jnp.where(bk == b, tab_ref[b, h], acc)
            o_ref[h] = lax.fori_loop(0, T5_BUCKETS, pick, jnp.full(bk.shape, NEG_INF, F32))

    return pl.pallas_call(
        body, name="t5_bias",
        in_specs=[pl.BlockSpec(memory_space=pltpu.SMEM), pl.BlockSpec(memory_space=pltpu.VMEM)],
        out_specs=pl.BlockSpec(memory_space=pltpu.VMEM),
        out_shape=_sds((SWA_HEADS, CHUNK, 3 * CHUNK), F32),
    )(t5_table.astype(F32), bucket)


def _prep_even(proj, cos, sin, q_gain, k_gain, *, tm):
    T = proj.shape[0]
    tm = min(tm, T)

    def body(qa_ref, ka_ref, qb_ref, kb_ref, c_ref, s_ref, qg_ref, kg_ref, qr_ref, kr_ref, qn_ref, kn_ref):
        c = jnp.concatenate([c_ref[...]] * RET_HEADS, axis=1)
        s = jnp.concatenate([s_ref[...]] * RET_HEADS, axis=1)
        qa = qa_ref[...]
        qr_ref[...] = (qa * c + _swap_halves(qa, RET_DK // 2) * s).astype(qr_ref.dtype)
        ka = ka_ref[...]
        kr_ref[...] = ((ka * c + _swap_halves(ka, RET_DK // 2) * s) * (RET_DK ** -0.5)).astype(kr_ref.dtype)
        for src, gain, dst, heads in ((qb_ref, qg_ref, qn_ref, SWA_HEADS), (kb_ref, kg_ref, kn_ref, SWA_KV_HEADS)):
            for h in range(heads):
                sl = slice(h * HEAD_DIM, (h + 1) * HEAD_DIM)
                xh = src[:, sl]
                r = lax.rsqrt(jnp.mean(xh * xh, axis=-1, keepdims=True) + EPS)
                dst[:, sl] = (xh * r * gain[...]).astype(dst.dtype)

    row = lambda i: (i, 0)
    const = lambda i: (0, 0)
    return pl.pallas_call(
        body, name="prep_even", grid=(T // tm,),
        in_specs=[pl.BlockSpec((tm, RET_Q), lambda i: (i, 0)), pl.BlockSpec((tm, RET_Q), lambda i: (i, 1)),
                  pl.BlockSpec((tm, D_MODEL), lambda i: (i, 3)), pl.BlockSpec((tm, 256), lambda i: (i, 16)),
                  pl.BlockSpec((tm, RET_DK), row), pl.BlockSpec((tm, RET_DK), row),
                  pl.BlockSpec((1, HEAD_DIM), const), pl.BlockSpec((1, HEAD_DIM), const)],
        out_specs=[pl.BlockSpec((tm, RET_Q), row), pl.BlockSpec((tm, RET_Q), row),
                   pl.BlockSpec((tm, D_MODEL), row), pl.BlockSpec((tm, 256), row)],
        out_shape=[_sds((T, RET_Q), _MXU), _sds((T, RET_Q), _MXU), _sds((T, D_MODEL), _MXU), _sds((T, 256), _MXU)],
        compiler_params=_params("parallel"),
    )(proj, proj, proj, proj, cos, sin, q_gain, k_gain)


def _ret_scan(name, x, y, y_col, w_asc, dec_asc, w_desc, dec_desc):
    T = x.shape[0]
    nc = T // CHUNK

    def body(xa_ref, ya_ref, xd_ref, yd_ref, wa_ref, da_ref, wd_ref, dd_ref, sa_out, sd_out, sa, sd):
        @pl.when(pl.program_id(0) == 0)
        def _():
            sa[...] = jnp.zeros_like(sa)
            sd[...] = jnp.zeros_like(sd)
        sa_out[0] = sa[...].astype(sa_out.dtype)
        sd_out[0] = sd[...].astype(sd_out.dtype)
        for x_ref, y_ref, w_ref, d_ref, st in ((xa_ref, ya_ref, wa_ref, da_ref, sa), (xd_ref, yd_ref, wd_ref, dd_ref, sd)):
            for h in range(RET_HEADS):
                ks = slice(h * RET_DK, (h + 1) * RET_DK)
                vs = slice(h * RET_DV, (h + 1) * RET_DV)
                u = _dot(x_ref[:, ks].astype(F32) * w_ref[:, ks], y_ref[:, vs], _TN)
                st[ks, :] = st[ks, :] * d_ref[h:h + 1, :] + u

    asc = lambda i: (i, 0)
    desc = lambda i: (nc - 1 - i, 0)
    const = lambda i: (0, 0)
    return pl.pallas_call(
        body, name=name, grid=(nc,),
        in_specs=[pl.BlockSpec((CHUNK, RET_Q), asc), pl.BlockSpec((CHUNK, RET_V), lambda i: (i, y_col)),
                  pl.BlockSpec((CHUNK, RET_Q), desc), pl.BlockSpec((CHUNK, RET_V), lambda i: (nc - 1 - i, y_col)),
                  pl.BlockSpec((CHUNK, RET_Q), const), pl.BlockSpec((RET_HEADS, RET_DV), const),
                  pl.BlockSpec((CHUNK, RET_Q), const), pl.BlockSpec((RET_HEADS, RET_DV), const)],
        out_specs=[pl.BlockSpec((1, RET_Q, RET_DV), lambda i: (i, 0, 0)),
                   pl.BlockSpec((1, RET_Q, RET_DV), lambda i: (nc - 1 - i, 0, 0))],
        out_shape=[_sds((nc, RET_Q, RET_DV), _MXU), _sds((nc, RET_Q, RET_DV), _MXU)],
        scratch_shapes=[pltpu.VMEM((RET_Q, RET_DV), F32), pltpu.VMEM((RET_Q, RET_DV), F32)],
        compiler_params=_params("arbitrary"),
    )(x, y, x, y, w_asc, dec_asc, w_desc, dec_desc)


def _ret_out(qr, kr, proj, sf, sb, tabs, gain):
    T = qr.shape[0]
    nc = T // CHUNK

    def body(q_ref, k_ref, v_ref, g_ref, sf_ref, sb_ref, tab_ref, gain_ref, o_ref, y_ref):
        for h in range(RET_HEADS):
            ks = slice(h * RET_DK, (h + 1) * RET_DK)
            vs = slice(h * RET_DV, (h + 1) * RET_DV)
            q, k, v = q_ref[:, ks], k_ref[:, ks], v_ref[:, vs]
            qf = q.astype(F32)
            a_mat = _dot(q, k, _NT) * tab_ref[h, 0]
            o = (_dot(a_mat, v) + _dot(qf * tab_ref[h, 1], sf_ref[0, ks, :]) + _dot(qf * tab_ref[h, 2], sb_ref[0, ks, :]))
            o_ref[:, vs] = o
            r = lax.rsqrt(jnp.mean(o * o, axis=-1, keepdims=True) + EPS)
            g = g_ref[:, vs]
            y_ref[:, vs] = (g * _sigmoid(g) * (o * r * gain_ref[:, vs])).astype(y_ref.dtype)

    row = lambda i: (i, 0)
    return pl.pallas_call(
        body, name="ret_out", grid=(nc,),
        in_specs=[pl.BlockSpec((CHUNK, RET_Q), row), pl.BlockSpec((CHUNK, RET_Q), row),
                  pl.BlockSpec((CHUNK, RET_V), lambda i: (i, 1)), pl.BlockSpec((CHUNK, RET_V), lambda i: (i, 2)),
                  pl.BlockSpec((1, RET_Q, RET_DV), lambda i: (i, 0, 0)), pl.BlockSpec((1, RET_Q, RET_DV), lambda i: (i, 0, 0)),
                  pl.BlockSpec((RET_HEADS, 3, CHUNK, CHUNK), lambda i: (0, 0, 0, 0)),
                  pl.BlockSpec((1, RET_V), lambda i: (0, 0))],
        out_specs=[pl.BlockSpec((CHUNK, RET_V), row), pl.BlockSpec((CHUNK, RET_V), row)],
        out_shape=[_sds((T, RET_V), F32), _sds((T, RET_V), _MXU)],
        compiler_params=_params("parallel"),
    )(qr, kr, proj, proj, sf, sb, tabs, gain)


def _ret_gate_bwd(dycat, proj, ret_o, gain, *, tm):
    T = ret_o.shape[0]
    tm = min(tm, T)

    def body(dy_ref, g_ref, o_ref, gain_ref, do_ref, dg_ref, dgain_ref):
        @pl.when(pl.program_id(0) == 0)
        def _():
            dgain_ref[...] = jnp.zeros_like(dgain_ref)
        for h in range(RET_HEADS):
            vs = slice(h * RET_DV, (h + 1) * RET_DV)
            o, g, dya, gn = o_ref[:, vs], g_ref[:, vs], dy_ref[:, vs], gain_ref[:, vs]
            r = lax.rsqrt(jnp.mean(o * o, axis=-1, keepdims=True) + EPS)
            ohat = o * r
            sg = _sigmoid(g)
            dy = dya * (g * sg)
            dg_ref[:, vs] = (dya * (ohat * gn) * (sg * (1.0 + g * (1.0 - sg)))).astype(dg_ref.dtype)
            dyg = dy * gn
            do_ref[:, vs] = (r * (dyg - ohat * jnp.mean(dyg * ohat, axis=-1, keepdims=True))).astype(do_ref.dtype)
            dgain_ref[:, vs] += _rowsum8(dy * ohat)

    row = lambda i: (i, 0)
    return pl.pallas_call(
        body, name="ret_gate_bwd", grid=(T // tm,),
        in_specs=[pl.BlockSpec((tm, RET_V), row), pl.BlockSpec((tm, RET_V), lambda i: (i, 2)),
                  pl.BlockSpec((tm, RET_V), row), pl.BlockSpec((1, RET_V), lambda i: (0, 0))],
        out_specs=[pl.BlockSpec((tm, RET_V), row), pl.BlockSpec((tm, RET_V), row), pl.BlockSpec((8, RET_V), lambda i: (0, 0))],
        out_shape=[_sds((T, RET_V), _MXU), _sds((T, RET_V), _MXU), _sds((8, RET_V), F32)],
        compiler_params=_params("arbitrary"),
    )(dycat, proj, ret_o, gain)


def _ret_bwd(qr, kr, proj, g_out, sf, sb, rf, rb, tabs):
    T = qr.shape[0]
    nc = T // CHUNK

    def body(q_ref, k_ref, v_ref, g_ref, sf_ref, sb_ref, rf_ref, rb_ref, tab_ref, dq_ref, dk_ref, dv_ref, dl_ref):
        @pl.when(pl.program_id(0) == 0)
        def _():
            dl_ref[...] = jnp.zeros_like(dl_ref)
        for h in range(RET_HEADS):
            ks = slice(h * RET_DK, (h + 1) * RET_DK)
            vs = slice(h * RET_DV, (h + 1) * RET_DV)
            q, k, v, g = q_ref[:, ks], k_ref[:, ks], v_ref[:, vs], g_ref[:, vs]
            s_f, s_b, r_f, r_b = sf_ref[0, ks, :], sb_ref[0, ks, :], rf_ref[0, ks, :], rb_ref[0, ks, :]
            tab = lambda t: tab_ref[h, t]
            qf, kf = q.astype(F32), k.astype(F32)
            qk = _dot(q, k, _NT)
            da_raw = _dot(g, v, _NT)
            x_f, x_b = _dot(g, s_f, _NT), _dot(g, s_b, _NT)
            dq_ref[:, ks] = _dot(da_raw * tab(TAB_D), k) + tab(TAB_A) * x_f + tab(TAB_B) * x_b
            at = _dot(k, q, _NT) * tab(TAB_DT)
            dat = _dot(v, g, _NT) * tab(TAB_DT)
            y_f, y_b = _dot(v, r_f, _NT), _dot(v, r_b, _NT)
            dk_ref[:, ks] = _dot(dat, q) + tab(TAB_CF) * y_f + tab(TAB_CB) * y_b
            dv_ref[:, vs] = (_dot(at, g) + _dot(kf * tab(TAB_CF), r_f) + _dot(kf * tab(TAB_CB), r_b)).astype(dv_ref.dtype)
            inner = da_raw * qk
            rs_f = r_f.astype(F32) * s_f.astype(F32)
            rs_b = r_b.astype(F32) * s_b.astype(F32)
            dl_f = (inner * tab(TAB_EF) + tab(TAB_RA) * qf * x_f + tab(TAB_RCF) * kf * y_f
                    + tab(TAB_KF) * (rs_f[:, :CHUNK] + rs_f[:, CHUNK:]))
            dl_b = (inner * tab(TAB_EB) + tab(TAB_RB) * qf * x_b + tab(TAB_RCB) * kf * y_b
                    + tab(TAB_KB) * (rs_b[:, :CHUNK] + rs_b[:, CHUNK:]))
            dl_ref[2 * h:2 * h + 1, :] += jnp.sum(dl_f, axis=0, keepdims=True)
            dl_ref[2 * h + 1:2 * h + 2, :] += jnp.sum(dl_b, axis=0, keepdims=True)

    row = lambda i: (i, 0)
    st = lambda i: (i, 0, 0)
    return pl.pallas_call(
        body, name="ret_bwd", grid=(nc,),
        in_specs=[pl.BlockSpec((CHUNK, RET_Q), row), pl.BlockSpec((CHUNK, RET_Q), row),
                  pl.BlockSpec((CHUNK, RET_V), lambda i: (i, 1)), pl.BlockSpec((CHUNK, RET_V), row),
                  pl.BlockSpec((1, RET_Q, RET_DV), st), pl.BlockSpec((1, RET_Q, RET_DV), st),
                  pl.BlockSpec((1, RET_Q, RET_DV), st), pl.BlockSpec((1, RET_Q, RET_DV), st),
                  pl.BlockSpec((RET_HEADS, 14, CHUNK, CHUNK), lambda i: (0, 0, 0, 0))],
        out_specs=[pl.BlockSpec((CHUNK, RET_Q), row), pl.BlockSpec((CHUNK, RET_Q), row),
                   pl.BlockSpec((CHUNK, RET_V), row), pl.BlockSpec((8, CHUNK), lambda i: (0, 0))],
        out_shape=[_sds((T, RET_Q), F32), _sds((T, RET_Q), F32), _sds((T, RET_V), _MXU), _sds((8, CHUNK), F32)],
        compiler_params=_params("arbitrary"),
    )(qr, kr, proj, g_out, sf, sb, rf, rb, tabs)


def _swa_probs(q, k_win, bias, sink, valid):
    s = _dot(q, k_win, _NT) * ATT_SCALE + bias
    s = jnp.where(valid, s, NEG_INF)
    m = jnp.maximum(jnp.max(s, axis=-1, keepdims=True), sink)
    p = jnp.exp(s - m)
    e_sink = jnp.exp(sink - m)
    inv = 1.0 / (jnp.sum(p, axis=-1, keepdims=True) + e_sink)
    return p * inv, e_sink * inv


def _swa_group(g, q_ref, bias_ref, sink_ref):
    group = SWA_HEADS // SWA_KV_HEADS
    heads = range(g * group, (g + 1) * group)
    q = jnp.concatenate([q_ref[:, h * HEAD_DIM:(h + 1) * HEAD_DIM] for h in heads], axis=0)
    bias = bias_ref[g * group:(g + 1) * group].reshape(group * CHUNK, 3 * CHUNK)
    sink = jnp.concatenate([jnp.broadcast_to(sink_ref[h:h + 1, 0:1], (CHUNK, 1)) for h in heads], axis=0)
    return q, bias, sink


def _swa_valid(i, nb):
    col = lax.broadcasted_iota(jnp.int32, (1, 3 * CHUNK), 1)
    return jnp.logical_and(jnp.logical_or(col >= CHUNK, i > 0), jnp.logical_or(col < 2 * CHUNK, i < nb - 1))


def _swa_window_specs(nb, width, col_block, clamp):
    prev = lambda i: (jnp.maximum(clamp(i) - 1, 0), col_block)
    cur = lambda i: (clamp(i), col_block)
    nxt = lambda i: (jnp.minimum(clamp(i) + 1, nb - 1), col_block)
    return [pl.BlockSpec((CHUNK, width), f) for f in (prev, cur, nxt)]


def _swa_fwd(qn, kn, proj, bias, sink):
    T = qn.shape[0]
    nb = T // CHUNK
    kvw = SWA_KV_HEADS * HEAD_DIM
    group = SWA_HEADS // SWA_KV_HEADS

    def body(q_ref, k0, k1, k2, v0, v1, v2, bias_ref, sink_ref, y_ref):
        i = pl.program_id(0)
        valid = _swa_valid(i, nb)
        for g in range(SWA_KV_HEADS):
            gs = slice(g * HEAD_DIM, (g + 1) * HEAD_DIM)
            k_win = jnp.concatenate([k0[:, gs], k1[:, gs], k2[:, gs]], axis=0)
            v_win = jnp.concatenate([v0[:, gs], v1[:, gs], v2[:, gs]], axis=0).astype(_MXU)
            q, bias, sink = _swa_group(g, q_ref, bias_ref, sink_ref)
            p, _ = _swa_probs(q, k_win, bias, sink, valid)
            o = _dot(p, v_win)
            for hh in range(group):
                h = g * group + hh
                y_ref[:, h * HEAD_DIM:(h + 1) * HEAD_DIM] = o[hh * CHUNK:(hh + 1) * CHUNK].astype(y_ref.dtype)

    ident = lambda i: i
    return pl.pallas_call(
        body, name="swa_fwd", grid=(nb,),
        in_specs=[pl.BlockSpec((CHUNK, D_MODEL), lambda i: (i, 0))]
        + _swa_window_specs(nb, kvw, 0, ident) + _swa_window_specs(nb, kvw, 17, ident)
        + [pl.BlockSpec((SWA_HEADS, CHUNK, 3 * CHUNK), lambda i: (0, 0, 0)), pl.BlockSpec((SWA_HEADS, HEAD_DIM), lambda i: (0, 0))],
        out_specs=pl.BlockSpec((CHUNK, D_MODEL), lambda i: (i, 0)),
        out_shape=_sds((T, D_MODEL), _MXU),
        compiler_params=_params("parallel"),
    )(qn, kn, kn, kn, proj, proj, proj, bias, sink)


def _swa_bwd(qn, kn, proj, dycat, bias, sink):
    T = qn.shape[0]
    nb = T // CHUNK
    kvw = SWA_KV_HEADS * HEAD_DIM
    group = SWA_HEADS // SWA_KV_HEADS

    def body(q_ref, k0, k1, k2, v0, v1, v2, dy_ref, bias_ref, sink_ref,
             dq_ref, dk_ref, dv_ref, dbias_ref, dsink_ref, acc_a, acc_b):
        i = pl.program_id(0)

        @pl.when(i == 0)
        def _():
            dbias_ref[...] = jnp.zeros_like(dbias_ref)
            dsink_ref[...] = jnp.zeros_like(dsink_ref)
            acc_a[...] = jnp.zeros_like(acc_a)
            acc_b[...] = jnp.zeros_like(acc_b)

        @pl.when(i < nb)
        def _():
            valid = _swa_valid(i, nb)
            for g in range(SWA_KV_HEADS):
                gs = slice(g * HEAD_DIM, (g + 1) * HEAD_DIM)
                k_win = jnp.concatenate([k0[:, gs], k1[:, gs], k2[:, gs]], axis=0)
                v_win = jnp.concatenate([v0[:, gs], v1[:, gs], v2[:, gs]], axis=0).astype(_MXU)
                q, bias, sink = _swa_group(g, q_ref, bias_ref, sink_ref)
                dy = jnp.concatenate([dy_ref[:, h * HEAD_DIM:(h + 1) * HEAD_DIM] for h in range(g * group, (g + 1) * group)], axis=0)
                p, p_sink = _swa_probs(q, k_win, bias, sink, valid)
                dp = _dot(dy, v_win, _NT)
                delta = jnp.sum(p * dp, axis=-1, keepdims=True)
                ds = p * (dp - delta)
                dsink = -p_sink * delta
                dq = _dot(ds, k_win) * ATT_SCALE
                for hh in range(group):
                    h = g * group + hh
                    rows = slice(hh * CHUNK, (hh + 1) * CHUNK)
                    dbias_ref[h] += ds[rows]
                    dsink_ref[h:h + 1, :] += jnp.sum(dsink[rows], axis=0, keepdims=True) * jnp.ones((1, HEAD_DIM), F32)
                    dq_ref[:, h * HEAD_DIM:(h + 1) * HEAD_DIM] = dq[rows]
                dk_win = _dot(ds, q, _TN) * ATT_SCALE
                dv_win = _dot(p, dy, _TN)
                for win, out_ref, col0 in ((dk_win, dk_ref, 0), (dv_win, dv_ref, kvw)):
                    cs = slice(col0 + g * HEAD_DIM, col0 + (g + 1) * HEAD_DIM)
                    out_ref[:, gs] = acc_a[:, cs] + win[:CHUNK]
                    acc_a[:, cs] = acc_b[:, cs] + win[CHUNK:2 * CHUNK]
                    acc_b[:, cs] = win[2 * CHUNK:]

        @pl.when(i == nb)
        def _():
            dk_ref[...] = acc_a[:, :kvw]
            dv_ref[...] = acc_a[:, kvw:]

    clamp = lambda i: jnp.minimum(i, nb - 1)
    late = lambda i: (jnp.maximum(i - 1, 0), 0)
    return pl.pallas_call(
        body, name="swa_bwd", grid=(nb + 1,),
        in_specs=[pl.BlockSpec((CHUNK, D_MODEL), lambda i: (clamp(i), 0))]
        + _swa_window_specs(nb, kvw, 0, clamp) + _swa_window_specs(nb, kvw, 17, clamp)
        + [pl.BlockSpec((CHUNK, D_MODEL), lambda i: (clamp(i), 1)),
           pl.BlockSpec((SWA_HEADS, CHUNK, 3 * CHUNK), lambda i: (0, 0, 0)), pl.BlockSpec((SWA_HEADS, HEAD_DIM), lambda i: (0, 0))],
        out_specs=[pl.BlockSpec((CHUNK, D_MODEL), lambda i: (clamp(i), 0)),
                   pl.BlockSpec((CHUNK, kvw), late), pl.BlockSpec((CHUNK, kvw), late),
                   pl.BlockSpec((SWA_HEADS, CHUNK, 3 * CHUNK), lambda i: (0, 0, 0)), pl.BlockSpec((SWA_HEADS, HEAD_DIM), lambda i: (0, 0))],
        out_shape=[_sds((T, D_MODEL), F32), _sds((T, kvw), F32), _sds((T, kvw), F32),
                   _sds((SWA_HEADS, CHUNK, 3 * CHUNK), F32), _sds((SWA_HEADS, HEAD_DIM), F32)],
        scratch_shapes=[pltpu.VMEM((CHUNK, 2 * kvw), F32), pltpu.VMEM((CHUNK, 2 * kvw), F32)],
        compiler_params=_params("arbitrary"),
    )(qn, kn, kn, kn, proj, proj, proj, dycat, bias, sink)


def _t5_bucket_reduce(dbias, bucket):
    def body(db_ref, bk_ref, o_ref):
        bk = bk_ref[...]
        row = lax.broadcasted_iota(jnp.int32, (SWA_HEADS, HEAD_DIM), 0)
        lane = lax.broadcasted_iota(jnp.int32, (SWA_HEADS, HEAD_DIM), 1)

        def per_bucket(b, acc):
            mask = bk == b
            for h in range(SWA_HEADS):
                tot = jnp.sum(jnp.sum(jnp.where(mask, db_ref[h], 0.0), axis=0, keepdims=True), axis=1, keepdims=True)
                acc = acc + jnp.where(jnp.logical_and(row == h, lane == b), tot, 0.0)
            return acc

        o_ref[...] = lax.fori_loop(0, T5_BUCKETS, per_bucket, jnp.zeros((SWA_HEADS, HEAD_DIM), F32))

    return pl.pallas_call(body, name="t5_bucket_reduce", out_shape=_sds((SWA_HEADS, HEAD_DIM), F32),
                          compiler_params=pltpu.CompilerParams(vmem_limit_bytes=VMEM_LIMIT_BYTES))(dbias, bucket)


def _headnorm_bwd(x, dy, gain):
    r = lax.rsqrt(jnp.mean(x * x, axis=-1, keepdims=True) + EPS)
    xhat = x * r
    dyg = dy * gain
    return r * (dyg - xhat * jnp.mean(dyg * xhat, axis=-1, keepdims=True)), dy * xhat


def _post_even(proj, dqr, dkr, dva, dga, dqn, dkn, dvb, cos, sin, q_gain, k_gain, *, tm):
    T = proj.shape[0]
    tm = min(tm, T)
    kvw = SWA_KV_HEADS * HEAD_DIM

    def body(qb_ref, kb_ref, dqr_ref, dkr_ref, dva_ref, dga_ref, dqn_ref, dkn_ref, dvb_ref, c_ref, s_ref, qg_ref, kg_ref,
             dp_ref, dqg_ref, dkg_ref):
        @pl.when(pl.program_id(0) == 0)
        def _():
            dqg_ref[...] = jnp.zeros_like(dqg_ref)
            dkg_ref[...] = jnp.zeros_like(dkg_ref)
        c = jnp.concatenate([c_ref[...]] * RET_HEADS, axis=1)
        s = jnp.concatenate([s_ref[...]] * RET_HEADS, axis=1)
        dq = dqr_ref[...]
        dp_ref[:, 0:RET_Q] = (dq * c + _swap_halves(dq * s, RET_DK // 2)).astype(dp_ref.dtype)
        dk = dkr_ref[...] * (RET_DK ** -0.5)
        dp_ref[:, RET_Q:2 * RET_Q] = (dk * c + _swap_halves(dk * s, RET_DK // 2)).astype(dp_ref.dtype)
        off = 2 * RET_Q
        dp_ref[:, off:off + RET_V] = dva_ref[...].astype(dp_ref.dtype)
        dp_ref[:, off + RET_V:off + 2 * RET_V] = dga_ref[...].astype(dp_ref.dtype)
        off += 2 * RET_V
        for src, dsrc, gain, dgain, heads, base in ((qb_ref, dqn_ref, qg_ref, dqg_ref, SWA_HEADS, off),
                                                    (kb_ref, dkn_ref, kg_ref, dkg_ref, SWA_KV_HEADS, off + D_MODEL)):
            for h in range(heads):
                sl = slice(h * HEAD_DIM, (h + 1) * HEAD_DIM)
                dx, dgx = _headnorm_bwd(src[:, sl], dsrc[:, sl], gain[...])
                dp_ref[:, base + h * HEAD_DIM:base + (h + 1) * HEAD_DIM] = dx.astype(dp_ref.dtype)
                dgain[...] += _rowsum8(dgx)
        dp_ref[:, off + D_MODEL + kvw:] = dvb_ref[...].astype(dp_ref.dtype)

    row = lambda i: (i, 0)
    const = lambda i: (0, 0)
    return pl.pallas_call(
        body, name="post_even", grid=(T // tm,),
        in_specs=[pl.BlockSpec((tm, D_MODEL), lambda i: (i, 3)), pl.BlockSpec((tm, kvw), lambda i: (i, 16)),
                  pl.BlockSpec((tm, RET_Q), row), pl.BlockSpec((tm, RET_Q), row),
                  pl.BlockSpec((tm, RET_V), row), pl.BlockSpec((tm, RET_V), row),
                  pl.BlockSpec((tm, D_MODEL), row), pl.BlockSpec((tm, kvw), row), pl.BlockSpec((tm, kvw), row),
                  pl.BlockSpec((tm, RET_DK), row), pl.BlockSpec((tm, RET_DK), row),
                  pl.BlockSpec((1, HEAD_DIM), const), pl.BlockSpec((1, HEAD_DIM), const)],
        out_specs=[pl.BlockSpec((tm, EVEN_IN), row), pl.BlockSpec((8, HEAD_DIM), const), pl.BlockSpec((8, HEAD_DIM), const)],
        out_shape=[_sds((T, EVEN_IN), _MXU), _sds((8, HEAD_DIM), F32), _sds((8, HEAD_DIM), F32)],
        compiler_params=_params("arbitrary"),
    )(proj, proj, dqr, dkr, dva, dga, dqn, dkn, dvb, cos, sin, q_gain, k_gain)


def _prep_odd(proj, cos, sin, q_gain, k_gain, *, tm):
    T = proj.shape[0]
    tm = min(tm, T)
    kvw = AX_KV_HEADS * HEAD_DIM

    def body(q_ref, k_ref, v_ref, c_ref, s_ref, qg_ref, kg_ref, qx_ref, kx_ref, vx_ref, v1_ref):
        c, s = c_ref[...], s_ref[...]
        for src, gain, dst, heads, scale in ((q_ref, qg_ref, qx_ref, AX_HEADS, SCORE_SCALE_LOG2), (k_ref, kg_ref, kx_ref, AX_KV_HEADS, 1.0)):
            for h in range(heads):
                sl = slice(h * HEAD_DIM, (h + 1) * HEAD_DIM)
                xh = src[:, sl]
                r = lax.rsqrt(jnp.mean(xh * xh, axis=-1, keepdims=True) + EPS)
                xn = xh * r * gain[...]
                dst[:, sl] = ((xn * c + _swap_halves(xn, HEAD_DIM // 4) * s) * scale).astype(dst.dtype)
        v = v_ref[...].astype(vx_ref.dtype)
        vx_ref[...] = v
        for g in range(AX_KV_HEADS):
            v1_ref[:, 2 * g * HEAD_DIM:(2 * g + 1) * HEAD_DIM] = v[:, g * HEAD_DIM:(g + 1) * HEAD_DIM]
            v1_ref[:, (2 * g + 1) * HEAD_DIM:(2 * g + 2) * HEAD_DIM] = jnp.ones((tm, HEAD_DIM), v1_ref.dtype)

    row = lambda i: (i, 0)
    const = lambda i: (0, 0)
    return pl.pallas_call(
        body, name="prep_odd", grid=(T // tm,),
        in_specs=[pl.BlockSpec((tm, D_MODEL), row), pl.BlockSpec((tm, kvw), lambda i: (i, 4)), pl.BlockSpec((tm, kvw), lambda i: (i, 5)),
                  pl.BlockSpec((tm, HEAD_DIM), row), pl.BlockSpec((tm, HEAD_DIM), row),
                  pl.BlockSpec((1, HEAD_DIM), const), pl.BlockSpec((1, HEAD_DIM), const)],
        out_specs=[pl.BlockSpec((tm, D_MODEL), row), pl.BlockSpec((tm, kvw), row), pl.BlockSpec((tm, kvw), row),
                   pl.BlockSpec((tm, 2 * kvw), row)],
        out_shape=[_sds((T, D_MODEL), _MXU), _sds((T, kvw), _MXU), _sds((T, kvw), _MXU), _sds((T, 2 * kvw), _MXU)],
        compiler_params=_params("parallel"),
    )(proj, proj, proj, cos, sin, q_gain, k_gain)


def _post_odd(proj, dqxt, dkx, dvx, cos, sin, q_gain, k_gain, *, tm):
    T = proj.shape[0]
    tm = min(tm, T)
    kvw = AX_KV_HEADS * HEAD_DIM

    def body(q_ref, k_ref, dqt_ref, dk_ref, dv_ref, c_ref, s_ref, qg_ref, kg_ref, dp_ref, dqg_ref, dkg_ref):
        @pl.when(pl.program_id(0) == 0)
        def _():
            dqg_ref[...] = jnp.zeros_like(dqg_ref)
            dkg_ref[...] = jnp.zeros_like(dkg_ref)
        c, s = c_ref[...], s_ref[...]
        for src, dsrc, gain, dgain, heads, base in ((q_ref, dqt_ref, qg_ref, dqg_ref, AX_HEADS, 0),
                                                    (k_ref, dk_ref, kg_ref, dkg_ref, AX_KV_HEADS, D_MODEL)):
            for h in range(heads):
                sl = slice(h * HEAD_DIM, (h + 1) * HEAD_DIM)
                d = dsrc[sl, :].T if dsrc is dqt_ref else dsrc[:, sl]
                dn = d * c + _swap_halves(d * s, HEAD_DIM // 4)
                dx, dgx = _headnorm_bwd(src[:, sl], dn, gain[...])
                dp_ref[:, base + h * HEAD_DIM:base + (h + 1) * HEAD_DIM] = dx.astype(dp_ref.dtype)
                dgain[...] += _rowsum8(dgx)
        dp_ref[:, D_MODEL + kvw:] = dv_ref[...].astype(dp_ref.dtype)

    row = lambda i: (i, 0)
    const = lambda i: (0, 0)
    return pl.pallas_call(
        body, name="post_odd", grid=(T // tm,),
        in_specs=[pl.BlockSpec((tm, D_MODEL), row), pl.BlockSpec((tm, kvw), lambda i: (i, 4)),
                  pl.BlockSpec((D_MODEL, tm), lambda i: (0, i)), pl.BlockSpec((tm, kvw), row), pl.BlockSpec((tm, kvw), row),
                  pl.BlockSpec((tm, HEAD_DIM), row), pl.BlockSpec((tm, HEAD_DIM), row),
                  pl.BlockSpec((1, HEAD_DIM), const), pl.BlockSpec((1, HEAD_DIM), const)],
        out_specs=[pl.BlockSpec((tm, ODD_IN), row), pl.BlockSpec((8, HEAD_DIM), const), pl.BlockSpec((8, HEAD_DIM), const)],
        out_shape=[_sds((T, ODD_IN), _MXU), _sds((8, HEAD_DIM), F32), _sds((8, HEAD_DIM), F32)],
        compiler_params=_params("arbitrary"),
    )(proj, proj, dqxt, dkx, dvx, cos, sin, q_gain, k_gain)


def _flash_fwd(qx, kx, v1, *, tq, tk):
    T = qx.shape[0]
    tq, tk = min(tq, T), min(tk, T)
    nq, nk = T // tq, T // tk
    group = AX_HEADS // AX_KV_HEADS

    def body(k_ref, v_ref, q_ref, o_ref, lse_ref, acc_sc, m_sc, l_sc):
        j = pl.program_id(2)

        @pl.when(j == 0)
        def _():
            m_sc[...] = jnp.full(m_sc.shape, NEG_INF, F32)
            l_sc[...] = jnp.zeros_like(l_sc)
            acc_sc[...] = jnp.zeros_like(acc_sc)
        k, v = k_ref[...], v_ref[...]

        def step(i, carry):
            cols = pl.ds(pl.multiple_of(i * tq, tq), tq)
            st = _dot(k, q_ref[cols, :], _NT)
            m_old = m_sc[i]
            m_new = jnp.maximum(m_old, jnp.max(st, axis=0, keepdims=True))
            p = jnp.exp2(st - m_new)
            alpha = jnp.exp2(m_old - m_new)
            pv = _dot(v, p, _TN)
            m_sc[i] = m_new
            l_sc[i] = alpha * l_sc[i] + pv[HEAD_DIM:HEAD_DIM + 1]
            acc_sc[:, cols] = alpha * acc_sc[:, cols] + pv[:HEAD_DIM]
            return carry

        lax.fori_loop(0, nq, step, 0)

        @pl.when(j == nk - 1)
        def _():
            def finish(i, carry):
                cols = pl.ds(pl.multiple_of(i * tq, tq), tq)
                o_ref[cols, :] = (acc_sc[:, cols] / l_sc[i]).T.astype(o_ref.dtype)
                lse_ref[0, i] = m_sc[i] + jnp.log2(l_sc[i])
                return carry

            lax.fori_loop(0, nq, finish, 0)

    kv = lambda g, h, j: (j, g)
    qh = lambda g, h, j: (0, g * group + h)
    o, lse = pl.pallas_call(
        body, name="flash_fwd", grid=(AX_KV_HEADS, group, nk),
        in_specs=[pl.BlockSpec((tk, HEAD_DIM), kv), pl.BlockSpec((tk, 2 * HEAD_DIM), kv), pl.BlockSpec((T, HEAD_DIM), qh)],
        out_specs=[pl.BlockSpec((T, HEAD_DIM), qh), pl.BlockSpec((1, nq, 1, tq), lambda g, h, j: (g * group + h, 0, 0, 0))],
        out_shape=[_sds((T, D_MODEL), _MXU), _sds((AX_HEADS, nq, 1, tq), F32)],
        scratch_shapes=[pltpu.VMEM((HEAD_DIM, T), F32), pltpu.VMEM((nq, 1, tq), F32), pltpu.VMEM((nq, 1, tq), F32)],
        compiler_params=_params("parallel", "arbitrary", "arbitrary"),
    )(kx, v1, qx)
    return o, lse.reshape(AX_HEADS, 1, T)


def _flash_bwd(qx, kx, vx, o, do, lse, *, tq, tk):
    T = qx.shape[0]
    tq, tk = min(tq, T), min(tk, T)
    nq = T // tq
    group = AX_HEADS // AX_KV_HEADS
    lse_rows = lse.reshape(AX_HEADS, nq, 1, tq)

    def body(k_ref, v_ref, q_ref, o_ref, do_ref, lse_ref, dqt_ref, dk_ref, dv_ref, delta_sc):
        j = pl.program_id(2)

        @pl.when(jnp.logical_and(pl.program_id(1) == 0, j == 0))
        def _():
            dk_ref[...] = jnp.zeros_like(dk_ref)
            dv_ref[...] = jnp.zeros_like(dv_ref)

        @pl.when(j == 0)
        def _():
            dqt_ref[...] = jnp.zeros_like(dqt_ref)

            def row_delta(i, carry):
                rows = pl.ds(pl.multiple_of(i * tq, tq), tq)
                prod = do_ref[rows, :].astype(F32) * o_ref[rows, :].astype(F32)
                delta_sc[i] = jnp.sum(prod.T, axis=0, keepdims=True)
                return carry

            lax.fori_loop(0, nq, row_delta, 0)
        k, v = k_ref[...], v_ref[...]

        def step(i, carry):
            dk, dv = carry
            off = pl.multiple_of(i * tq, tq)
            q, do_blk = q_ref[pl.ds(off, tq), :], do_ref[pl.ds(off, tq), :]
            pt = jnp.exp2(_dot(k, q, _NT) - lse_ref[0, i])
            dst = pt * (_dot(v, do_blk, _NT) - delta_sc[i])
            dqt_ref[:, pl.ds(off, tq)] += _dot(k, dst, _TN) * ATT_SCALE
            return dk + _dot(dst, q), dv + _dot(pt, do_blk)

        zero = jnp.zeros((tk, HEAD_DIM), F32)
        dk, dv = lax.fori_loop(0, nq, step, (zero, zero))
        rows = pl.ds(pl.multiple_of(j * tk, tk), tk)
        dk_ref[rows, :] += dk * (ATT_SCALE / SCORE_SCALE_LOG2)
        dv_ref[rows, :] += dv

    kv = lambda g, h, j: (j, g)
    qh = lambda g, h, j: (0, g * group + h)
    st = lambda g, h, j: (g * group + h, 0, 0, 0)
    acc = lambda g, h, j: (0, g)
    return pl.pallas_call(
        body, name="flash_bwd", grid=(AX_KV_HEADS, group, T // tk),
        in_specs=[pl.BlockSpec((tk, HEAD_DIM), kv), pl.BlockSpec((tk, HEAD_DIM), kv),
                  pl.BlockSpec((T, HEAD_DIM), qh), pl.BlockSpec((T, HEAD_DIM), qh), pl.BlockSpec((T, HEAD_DIM), qh),
                  pl.BlockSpec((1, nq, 1, tq), st)],
        out_specs=[pl.BlockSpec((HEAD_DIM, T), lambda g, h, j: (g * group + h, 0)),
                   pl.BlockSpec((T, HEAD_DIM), acc), pl.BlockSpec((T, HEAD_DIM), acc)],
        out_shape=[_sds((D_MODEL, T), F32), _sds((T, AX_KV_HEADS * HEAD_DIM), F32), _sds((T, AX_KV_HEADS * HEAD_DIM), F32)],
        scratch_shapes=[pltpu.VMEM((nq, 1, tq), F32)],
        compiler_params=_params("parallel", "arbitrary", "arbitrary"),
    )(kx, vx, qx, o, do, lse_rows)


TM = 1024
TM_WIDE = 512


def _mlp_fwd(tag, x, gain, w_up, w_down, target=None):
    u, h = _norm_matmul(f"mlp_up{tag}", x, gain, w_up, tm=TM, tn=1024, out_dtype=_MXU)
    out = _matmul_res(f"mlp_down{tag}", [u], w_down, x, tm=TM_WIDE, relu2=True, target=target)
    return out, (x, u, h)


def _local_step(x, target, p, w_first, fetch_rest, push, tokens=()):
    T = x.shape[0]
    cos_r, sin_r = _ret_rope_tables(T)
    cos_a, sin_a = _axial_rope_tables(T)
    tabs, rw, log_gamma = _retention_tables(p["ret_decay_logit"][0])
    bias = _swa_bias(p["t5_table"])
    sink = p["swa_sink"][0][:, None] * jnp.ones((1, HEAD_DIM), F32)
    nm, nl = p["norm_mix"], p["norm_mlp"]
    pending = [t for t in tokens if t is not None]

    def send(tag, weight, dw):
        token = push(tag, weight, dw[None])
        if token is not None:
            pending.append(token)

    def tied(operand):
        while pending:
            operand = operand + pending.pop()[0:1, 0:1]
        return operand

    def mlp_bwd(tag, saved, gain, w_up, w_down, dy, dy16):
        xs, u, h = saved
        du = _matmul_nt(f"mlp_down{tag}_bwd", dy16, w_down, tm=TM, tn=1024, out_dtype=_MXU, relu_of=u)
        send(f"mlp_down{tag}", "w_mlp_down", _matmul_tn(f"mlp_down{tag}_dw", u, dy16, tk=1024, tn=1024, tt=1024, out_dtype=_WIRE, relu2=True))
        dx, dx16, dgain = _matmul_nt_normbwd(f"mlp_up{tag}_bwd", du, w_up, xs, tied(gain), dy, tm=TM_WIDE)
        send(f"mlp_up{tag}", "w_mlp_up", _matmul_tn(f"mlp_up{tag}_dw", h, du, tk=1024, tn=1024, tt=1024, out_dtype=_WIRE))
        return dx, dx16, dgain

    proj0, h0 = _norm_matmul("in_even", x, tied(nm[0:1]), w_first["w_in_even"], tm=TM, tn=1152, out_dtype=F32)
    qr, kr, qn, kn = _prep_even(proj0, cos_r, sin_r, p["swa_q_norm"], p["swa_k_norm"], tm=TM)
    sf, sb = _ret_scan("ret_scan_fwd", kr, proj0, 1, rw["cf"], rw["dec_f"], rw["cb"], rw["dec_b"])
    ret_o, ya = _ret_out(qr, kr, proj0, sf, sb, tabs[:, (TAB_D, TAB_A, TAB_B)], p["ret_norm"])
    yb = _swa_fwd(qn, kn, proj0, bias, sink)
    wf = {**w_first, **fetch_rest(yb)}
    x1 = _matmul_res("out_even", [ya, yb], wf["w_out_even"], x, tm=TM)
    x2, mlp0 = _mlp_fwd(0, x1, nl[0:1], wf["w_mlp_up"][0], wf["w_mlp_down"][0])
    proj1, h1 = _norm_matmul("in_odd", x2, nm[1:2], wf["w_in_odd"], tm=TM, tn=768, out_dtype=F32)
    qx, kx, vx, v1 = _prep_odd(proj1, cos_a, sin_a, p["ax_q_norm"], p["ax_k_norm"], tm=TM)
    o, lse = _flash_fwd(qx, kx, v1, tq=2048, tk=1024)
    x3 = _matmul_res("out_odd", [o], wf["w_out_odd"], x2, tm=TM)
    (g4, g4_16, loss_part), mlp1 = _mlp_fwd(1, x3, nl[1:2], wf["w_mlp_up"][1], wf["w_mlp_down"][1], target=target)

    dx3, dx3_16, dnl1 = mlp_bwd(1, mlp1, nl[1:2], wf["w_mlp_up"][1], wf["w_mlp_down"][1], g4, g4_16)
    do = _matmul_nt("out_odd_bwd", dx3_16, wf["w_out_odd"], tm=TM, tn=1024, out_dtype=_MXU)
    send("out_odd", "w_out_odd", _matmul_tn("out_odd_dw", o, dx3_16, tk=1024, tn=1024, tt=1024, out_dtype=_WIRE))
    dqxt, dkx, dvx = _flash_bwd(qx, kx, vx, o, do, lse, tq=1024, tk=512)
    dproj1, dqg1, dkg1 = _post_odd(proj1, dqxt, dkx, dvx, cos_a, sin_a, tied(p["ax_q_norm"]), p["ax_k_norm"], tm=TM)
    send("in_odd", "w_in_odd", _matmul_tn("in_odd_dw", h1, dproj1, tk=1024, tn=768, tt=1024, out_dtype=_WIRE))
    dx2, dx2_16, dnm1 = _matmul_nt_normbwd("in_odd_bwd", dproj1, wf["w_in_odd"], x2, tied(nm[1:2]), dx3, tm=TM_WIDE)
    dx1, dx1_16, dnl0 = mlp_bwd(0, mlp0, nl[0:1], wf["w_mlp_up"][0], wf["w_mlp_down"][0], dx2, dx2_16)
    dycat = _matmul_nt("out_even_bwd", dx1_16, wf["w_out_even"], tm=TM, tn=1024, out_dtype=F32)
    send("out_even", "w_out_even", jnp.concatenate([
        _matmul_tn("out_even_dw_ret", ya, dx1_16, tk=1024, tn=1024, tt=1024, out_dtype=_WIRE),
        _matmul_tn("out_even_dw_swa", yb, dx1_16, tk=1024, tn=1024, tt=1024, out_dtype=_WIRE)], axis=0))
    g_out, dga, dretg = _ret_gate_bwd(dycat, proj0, ret_o, tied(p["ret_norm"]), tm=TM)
    rb, rf = _ret_scan("ret_scan_bwd", qr, g_out, 0, rw["b"], rw["dec_b"], rw["a"], rw["dec_f"])
    dqr, dkr, dva, dlog = _ret_bwd(qr, kr, proj0, g_out, sf, sb, rf, rb, tabs)
    dqn, dkn, dvb, dbias, dsink = _swa_bwd(qn, kn, proj0, dycat, bias, sink)
    dt5 = _t5_bucket_reduce(dbias, _t5_bucket(_swa_rel()).astype(jnp.int32))
    dproj0, dqg0, dkg0 = _post_even(proj0, dqr, dkr, dva, dga, dqn, dkn, dvb, cos_r, sin_r,
                                    p["swa_q_norm"], p["swa_k_norm"], tm=TM_WIDE)
    send("in_even", "w_in_even", _matmul_tn("in_even_dw", h0, dproj0, tk=1024, tn=1152, tt=1024, out_dtype=_WIRE))
    dx0, _, dnm0 = _matmul_nt_normbwd("in_even_bwd", dproj0, w_first["w_in_even"], x, tied(nm[0:1]), dx1, tm=TM_WIDE)

    fold = lambda part: jnp.sum(part, axis=0)
    dlam = jnp.sum(dlog, axis=1).reshape(RET_HEADS, 2).T
    small = {
        "norm_mix": jnp.stack([fold(dnm0), fold(dnm1)]),
        "norm_mlp": jnp.stack([fold(dnl0), fold(dnl1)]),
        "ret_decay_logit": (dlam * (1.0 - jnp.exp(log_gamma)))[None],
        "ret_norm": fold(dretg)[None],
        "swa_q_norm": fold(dqg0)[None], "swa_k_norm": fold(dkg0)[None],
        "swa_sink": dsink[:, 0][None],
        "t5_table": dt5[:, :T5_BUCKETS].T,
        "ax_q_norm": fold(dqg1)[None], "ax_k_norm": fold(dkg1)[None],
    }
    return loss_part, dx0, small


BIG = ("w_in_even", "w_out_even", "w_in_odd", "w_out_odd", "w_mlp_up", "w_mlp_down")
SMALL = ("norm_mix", "norm_mlp", "ret_decay_logit", "ret_norm", "swa_q_norm", "swa_k_norm", "swa_sink", "t5_table",
         "ax_q_norm", "ax_k_norm")
WEIGHTS = ("norm_mix", "norm_mlp", "w_in_even", "w_out_even", "ret_decay_logit", "ret_norm", "swa_q_norm", "swa_k_norm",
           "swa_sink", "t5_table", "w_in_odd", "w_out_odd", "ax_q_norm", "ax_k_norm", "w_mlp_up", "w_mlp_down")
SHARD_AXIS = {"w_in_even": 2, "w_out_even": 1, "w_in_odd": 2, "w_out_odd": 1, "w_mlp_up": 2, "w_mlp_down": 1}
N_CHIPS = 4
ANY = pl.BlockSpec(memory_space=pl.ANY)
HBM = pl.BlockSpec(memory_space=pltpu.HBM)
SEM = pl.BlockSpec(memory_space=pltpu.SEMAPHORE)
SPLIT_COPY = pltpu.CompilerParams(has_side_effects=pltpu.SideEffectType.DATAFLOW_SIDE_EFFECTING)


def _in_hbm(a):
    return pltpu.with_memory_space_constraint(a, pltpu.HBM)


def _mesh_pos():
    return lax.axis_index("x"), lax.axis_index("y"), lax.axis_index("c")


def _window(ref, axis, start, size):
    idx = [slice(None)] * len(ref.shape)
    idx[axis] = pl.ds(start, size)
    return ref.at[tuple(idx)]


def _cast_place(key, shard, chip, *, tr=256):
    L, R, C = shard.shape
    tr = min(tr, R)
    axis = SHARD_AXIS[key]
    whole = tuple(d * (N_CHIPS if a == axis else 1) for a, d in enumerate(shard.shape))

    def body(chip_ref, s_ref, o_ref):
        o_ref[...] = s_ref[...].astype(o_ref.dtype)

    if axis == 2:
        out_map = lambda l, i, chip_ref: (l, i, chip_ref[0])
    else:
        out_map = lambda l, i, chip_ref: (l, i + chip_ref[0] * (R // tr), 0)
    grid_spec = pltpu.PrefetchScalarGridSpec(
        num_scalar_prefetch=1, grid=(L, R // tr),
        in_specs=[pl.BlockSpec((1, tr, C), lambda l, i, chip_ref: (l, i, 0))],
        out_specs=pl.BlockSpec((1, tr, C), out_map))
    return pl.pallas_call(body, name=f"cast_place_{key}", grid_spec=grid_spec, out_shape=_sds(whole, _MXU),
                          compiler_params=_params("parallel", "parallel"))(chip, shard)


def _gather_copies(names, refs, send_sems, recv_sems, *, outgoing=True, incoming=True):
    x, y, c = _mesh_pos()
    chips = [(1 - x, y), (x, 1 - y), (1 - x, 1 - y)]
    out, inc = [], []
    for t, key in enumerate(names):
        size = refs[t].shape[SHARD_AXIS[key]] // N_CHIPS
        slot = lambda px, py: _window(refs[t], SHARD_AXIS[key], pl.multiple_of((2 * px + py) * size, 128), size)
        for k, (px, py) in enumerate(chips):
            sems = dict(send_sem=send_sems.at[3 * t + k], recv_sem=recv_sems.at[3 * t + k], device_id=(px, py, c), device_id_type=MESH)
            if outgoing:
                out.append(pltpu.make_async_remote_copy(slot(x, y), slot(x, y), **sems))
            if incoming:
                inc.append(pltpu.make_async_remote_copy(slot(x, y), slot(px, py), **sems))
    return out, inc


def _allgather_start(groups):
    names = [list(g) for g in groups]
    flat = [g[k] for g in groups for k in g]
    n, ng = len(flat), len(groups)

    def body(*refs):
        start = 0
        for gi, keys in enumerate(names):
            copies, _ = _gather_copies(keys, refs[start:start + len(keys)], refs[n + 2 * gi], refs[n + 2 * gi + 1], incoming=False)
            for cp in copies:
                cp.start()
            start += len(keys)
        token = refs[-1]
        token[...] = jnp.zeros_like(token)

    sem_shapes = [pltpu.SemaphoreType.DMA((3 * len(keys),)) for keys in names for _ in (0, 1)]
    outs = pl.pallas_call(
        body, name="allgather_start", in_specs=[HBM] * n,
        out_specs=[SEM] * (2 * ng) + [HBM] * n + [pl.BlockSpec(memory_space=pltpu.VMEM)],
        out_shape=sem_shapes + [pltpu.HBM(a.shape, a.dtype) for a in flat] + [_sds((8, HEAD_DIM), F32)],
        input_output_aliases={t: 2 * ng + t for t in range(n)},
        compiler_params=SPLIT_COPY,
    )(*[_in_hbm(a) for a in flat])
    states, start = [], 2 * ng
    for gi, keys in enumerate(names):
        states.append((gi, keys, outs[2 * gi], outs[2 * gi + 1], outs[start:start + len(keys)]))
        start += len(keys)
    return states, outs[-1]


def _allgather_wait(state, after):
    gi, names, send_sems, recv_sems, thru = state
    n = len(names)

    def body(*refs):
        outgoing, incoming = _gather_copies(names, refs[:n], refs[n], refs[n + 1])
        for cp in outgoing:
            cp.wait_send()
        for cp in incoming:
            cp.wait_recv()

    outs = pl.pallas_call(
        body, name=f"allgather_wait_{gi}", in_specs=[HBM] * n + [SEM, SEM, ANY], out_specs=[HBM] * n,
        out_shape=[pltpu.HBM(t.shape, t.dtype) for t in thru],
        input_output_aliases={t: t for t in range(n)},
        compiler_params=SPLIT_COPY,
    )(*thru, send_sems, recv_sems, after)
    return dict(zip(names, outs))


FLIPS = [(a, b, d) for a in (0, 1) for b in (0, 1) for d in (0, 1) if (a, b, d) != (0, 0, 0)]


def _flip(pos, f):
    return tuple(1 - p if fi else p for p, fi in zip(pos, f))


def _piece_shape(weight, shape):
    out = list(shape)
    out[SHARD_AXIS[weight]] //= N_CHIPS
    out[1] //= 2
    return tuple(out)


def _piece(ref, weight, chip, core):
    piece = _piece_shape(weight, ref.shape)
    if SHARD_AXIS[weight] == 1:
        return _window(ref, 1, pl.multiple_of((2 * chip + core) * piece[1], 8), piece[1])
    return _window(_window(ref, 2, pl.multiple_of(chip * piece[2], 128), piece[2]), 1, pl.multiple_of(core * piece[1], 8), piece[1])


def _scatter_copies(weight, grad_ref, land_ref, send_sems, recv_sems, *, outgoing=True, incoming=True):
    pos = _mesh_pos()
    out, inc = [], []
    for k, f in enumerate(FLIPS):
        peer = _flip(pos, f)
        sems = dict(send_sem=send_sems.at[k], recv_sem=recv_sems.at[k], device_id=peer, device_id_type=MESH)
        if outgoing:
            out.append(pltpu.make_async_remote_copy(_piece(grad_ref, weight, 2 * peer[0] + peer[1], peer[2]), land_ref.at[k], **sems))
        if incoming:
            inc.append(pltpu.make_async_remote_copy(_piece(grad_ref, weight, 2 * pos[0] + pos[1], pos[2]), land_ref.at[k], **sems))
    return out, inc


def _scatter_start(tag, weight, grad):
    n_peer = len(FLIPS)
    land = lax.empty((n_peer,) + _piece_shape(weight, grad.shape), grad.dtype)

    def body(grad_ref, land_ref, send_sems, recv_sems, grad_thru, land_thru, token):
        copies, _ = _scatter_copies(weight, grad_ref, land_ref, send_sems, recv_sems, incoming=False)
        for cp in copies:
            cp.start()
        token[...] = jnp.zeros_like(token)

    outs = pl.pallas_call(
        body, name=f"scatter_start_{tag}", in_specs=[HBM, HBM],
        out_specs=[SEM, SEM, HBM, HBM, pl.BlockSpec(memory_space=pltpu.VMEM)],
        out_shape=[pltpu.SemaphoreType.DMA((n_peer,)), pltpu.SemaphoreType.DMA((n_peer,)),
                   pltpu.HBM(grad.shape, grad.dtype), pltpu.HBM(land.shape, land.dtype), _sds((8, HEAD_DIM), F32)],
        input_output_aliases={0: 2, 1: 3},
        compiler_params=SPLIT_COPY,
    )(_in_hbm(grad), _in_hbm(land))
    return (tag, weight, outs[:4]), outs[4]


def _scatter_wait(state, after):
    tag, weight, (send_sems, recv_sems, grad_thru, land_thru) = state

    def body(grad_ref, land_ref, send_ref, recv_ref, after_ref, grad_out, land_out):
        outgoing, incoming = _scatter_copies(weight, grad_ref, land_ref, send_ref, recv_ref)
        for cp in outgoing:
            cp.wait_send()
        for cp in incoming:
            cp.wait_recv()

    return pl.pallas_call(
        body, name=f"scatter_wait_{tag}", in_specs=[HBM, HBM, SEM, SEM, ANY], out_specs=[HBM, HBM],
        out_shape=[pltpu.HBM(grad_thru.shape, grad_thru.dtype), pltpu.HBM(land_thru.shape, land_thru.dtype)],
        input_output_aliases={0: 0, 1: 1},
        compiler_params=SPLIT_COPY,
    )(grad_thru, land_thru, send_sems, recv_sems, after)


def _sum_pieces(tag, weight, grad, land, where, *, tr=128):
    _, R, C = _piece_shape(weight, grad.shape)
    tr = min(tr, R)
    nr = R // tr

    def body(where_ref, g_ref, l_ref, o_ref):
        acc = g_ref[...].astype(F32)
        for s in range(len(FLIPS)):
            acc = acc + l_ref[s].astype(F32)
        o_ref[...] = acc

    if SHARD_AXIS[weight] == 1:
        own = lambda i, where_ref: (0, (2 * where_ref[0] + where_ref[1]) * nr + i, 0)
    else:
        own = lambda i, where_ref: (0, where_ref[1] * nr + i, where_ref[0])
    grid_spec = pltpu.PrefetchScalarGridSpec(
        num_scalar_prefetch=1, grid=(nr,),
        in_specs=[pl.BlockSpec((1, tr, C), own), pl.BlockSpec((len(FLIPS), 1, tr, C), lambda i, where_ref: (0, 0, i, 0))],
        out_specs=pl.BlockSpec((1, tr, C), lambda i, where_ref: (0, where_ref[1] * nr + i, 0)))
    return pl.pallas_call(body, name=f"sum_{tag}", grid_spec=grid_spec, out_shape=_sds((1, 2 * R, C), F32),
                          compiler_params=_params("parallel"))(where, grad, land)


def _exchange_halves(shards):
    names = list(shards)
    n = len(names)
    half_sizes = [shards[k].shape[1] // 2 for k in names]

    def body(*refs):
        outs = refs[n:2 * n]
        send_sems, recv_sems = refs[2 * n:]
        x, y, c = _mesh_pos()
        half = lambda t, core: _window(outs[t], 1, pl.multiple_of(core * half_sizes[t], 8), half_sizes[t])
        sends = []
        for t in range(n):
            sends.append(pltpu.make_async_remote_copy(half(t, c), half(t, c), send_sems.at[t], recv_sems.at[t],
                                                      device_id=(x, y, 1 - c), device_id_type=MESH))
            sends[-1].start()
        for t in range(n):
            pltpu.make_async_remote_copy(half(t, c), half(t, 1 - c), send_sems.at[t], recv_sems.at[t],
                                         device_id=(x, y, 1 - c), device_id_type=MESH).wait_recv()
        for cp in sends:
            cp.wait_send()

    outs = pl.pallas_call(
        body, name="exchange_halves", in_specs=[ANY] * n, out_specs=[ANY] * n,
        out_shape=[_sds(shards[k].shape, F32) for k in names],
        input_output_aliases={t: t for t in range(n)},
        scratch_shapes=[pltpu.SemaphoreType.DMA((n,)), pltpu.SemaphoreType.DMA((n,))],
    )(*[shards[k] for k in names])
    return dict(zip(names, outs))


def _adamw_math(w, g, m, v):
    m = ADAM_B1 * m + (1.0 - ADAM_B1) * g
    v = ADAM_B2 * v + (1.0 - ADAM_B2) * jnp.square(g)
    m_hat = m / (1.0 - ADAM_B1 ** ADAM_STEP)
    v_hat = v / (1.0 - ADAM_B2 ** ADAM_STEP)
    return -ADAM_LR * (m_hat / (jnp.sqrt(v_hat) + ADAM_EPS) + ADAM_WD * w), m, v


def _adamw(name, w, g, m, v, *, tr=256):
    R, C = w.shape
    tr = min(tr, R)

    def body(w_ref, g_ref, m_ref, v_ref, d_ref, mo_ref, vo_ref):
        d_ref[...], mo_ref[...], vo_ref[...] = _adamw_math(w_ref[...], g_ref[...], m_ref[...], v_ref[...])

    spec = pl.BlockSpec((tr, C), lambda i: (i, 0))
    return pl.pallas_call(body, name=name, grid=(R // tr,), in_specs=[spec] * 4, out_specs=[spec] * 3,
                          out_shape=[_sds((R, C), F32)] * 3, compiler_params=_params("parallel"))(w, g, m, v)


SLAB_ROWS = 8
LOSS_ROW = 7


def _pack_small(d):
    pad = lambda a, width: jnp.pad(a.reshape(-1), (0, width - a.size))
    row5 = jnp.concatenate([d["swa_q_norm"].reshape(-1), d["swa_k_norm"].reshape(-1), d["ax_q_norm"].reshape(-1),
                            d["ax_k_norm"].reshape(-1), pad(d["swa_sink"], HEAD_DIM), pad(d["ret_decay_logit"], HEAD_DIM),
                            jnp.zeros((2 * HEAD_DIM,), F32)])
    return jnp.concatenate([d["norm_mix"], d["norm_mlp"], d["ret_norm"], row5[None], pad(d["t5_table"], D_MODEL)[None],
                            jnp.zeros((1, D_MODEL), F32)], axis=0)


def _unpack_small(slab):
    r5 = slab[5]
    return {
        "norm_mix": slab[0:2], "norm_mlp": slab[2:4], "ret_norm": slab[4:5],
        "swa_q_norm": r5[None, 0:128], "swa_k_norm": r5[None, 128:256], "ax_q_norm": r5[None, 256:384],
        "ax_k_norm": r5[None, 384:512], "swa_sink": r5[None, 512:512 + SWA_HEADS],
        "ret_decay_logit": r5[640:640 + 2 * RET_HEADS].reshape(1, 2, RET_HEADS),
        "t5_table": slab[6, :T5_BUCKETS * SWA_HEADS].reshape(T5_BUCKETS, SWA_HEADS),
    }


def _small_allreduce_adamw(g_slab, w_slab, m_slab, v_slab, loss_part):
    def body(g_ref, w_ref, m_ref, v_ref, lp_ref, go_ref, d_ref, mo_ref, vo_ref, gath, send_sems, recv_sems):
        pos = _mesh_pos()
        ident = lambda p: 4 * p[0] + 2 * p[1] + p[2]
        me = ident(pos)
        row = lax.broadcasted_iota(jnp.int32, (SLAB_ROWS, D_MODEL), 0)
        lane = lax.broadcasted_iota(jnp.int32, (SLAB_ROWS, D_MODEL), 1)
        loss = jnp.sum(jnp.sum(lp_ref[...], axis=0, keepdims=True), axis=1, keepdims=True) * (0.5 / D_MODEL)
        gath[me] = jnp.where(jnp.logical_and(row == LOSS_ROW, lane == 0), loss, g_ref[...])
        sends = []
        for k, f in enumerate(FLIPS):
            sends.append(pltpu.make_async_remote_copy(gath.at[me], gath.at[me], send_sems.at[k], recv_sems.at[k],
                                                      device_id=_flip(pos, f), device_id_type=MESH))
            sends[-1].start()
        for k, f in enumerate(FLIPS):
            peer = _flip(pos, f)
            pltpu.make_async_remote_copy(gath.at[me], gath.at[ident(peer)], send_sems.at[k], recv_sems.at[k],
                                         device_id=peer, device_id_type=MESH).wait_recv()
        for cp in sends:
            cp.wait_send()
        total = gath[0]
        for s in range(1, N_DEV):
            total = total + gath[s]
        go_ref[...] = total
        d_ref[...], mo_ref[...], vo_ref[...] = _adamw_math(w_ref[...], total, m_ref[...], v_ref[...])

    vmem = pl.BlockSpec(memory_space=pltpu.VMEM)
    return pl.pallas_call(
        body, name="small_allreduce_adamw", in_specs=[vmem] * 5, out_specs=[vmem] * 4,
        out_shape=[_sds((SLAB_ROWS, D_MODEL), F32)] * 4,
        scratch_shapes=[pltpu.VMEM((N_DEV, SLAB_ROWS, D_MODEL), F32),
                        pltpu.SemaphoreType.DMA((len(FLIPS),)), pltpu.SemaphoreType.DMA((len(FLIPS),))],
    )(g_slab, w_slab, m_slab, v_slab, loss_part)


def kernel(x, norm_mix, norm_mlp, w_in_even, w_out_even, ret_decay_logit, ret_norm, swa_q_norm, swa_k_norm, swa_sink, t5_table, w_in_odd, w_out_odd, ax_q_norm, ax_k_norm, w_mlp_up, w_mlp_down, loss_target, m_norm_mix, m_norm_mlp, m_w_in_even, m_w_out_even, m_ret_decay_logit, m_ret_norm, m_swa_q_norm, m_swa_k_norm, m_swa_sink, m_t5_table, m_w_in_odd, m_w_out_odd, m_ax_q_norm, m_ax_k_norm, m_w_mlp_up, m_w_mlp_down, v_norm_mix, v_norm_mlp, v_w_in_even, v_w_out_even, v_ret_decay_logit, v_ret_norm, v_swa_q_norm, v_swa_k_norm, v_swa_sink, v_t5_table, v_w_in_odd, v_w_out_odd, v_ax_q_norm, v_ax_k_norm, v_w_mlp_up, v_w_mlp_down):
    w = dict(zip(WEIGHTS, (norm_mix, norm_mlp, w_in_even, w_out_even, ret_decay_logit, ret_norm, swa_q_norm, swa_k_norm,
                           swa_sink, t5_table, w_in_odd, w_out_odd, ax_q_norm, ax_k_norm, w_mlp_up, w_mlp_down)))
    m = dict(zip(WEIGHTS, (m_norm_mix, m_norm_mlp, m_w_in_even, m_w_out_even, m_ret_decay_logit, m_ret_norm, m_swa_q_norm,
                           m_swa_k_norm, m_swa_sink, m_t5_table, m_w_in_odd, m_w_out_odd, m_ax_q_norm, m_ax_k_norm,
                           m_w_mlp_up, m_w_mlp_down)))
    v = dict(zip(WEIGHTS, (v_norm_mix, v_norm_mlp, v_w_in_even, v_w_out_even, v_ret_decay_logit, v_ret_norm, v_swa_q_norm,
                           v_swa_k_norm, v_swa_sink, v_t5_table, v_w_in_odd, v_w_out_odd, v_ax_q_norm, v_ax_k_norm,
                           v_w_mlp_up, v_w_mlp_down)))
    flat = lambda a: a.reshape(-1, a.shape[-1])

    chip = (2 * lax.axis_index("x") + lax.axis_index("y")).astype(jnp.int32)
    where = jnp.stack([chip, lax.axis_index("c").astype(jnp.int32)])

    placed = {k: _cast_place(k, w[k], where[0:1]) for k in BIG}
    (gather_first, gather_rest), gather_token = _allgather_start(
        [{"w_in_even": placed["w_in_even"]}, {k: placed[k] for k in BIG if k != "w_in_even"}])
    unstack = lambda whole: {k: (a if k.startswith("w_mlp") else a[0]) for k, a in whole.items()}

    in_flight = []

    def push(tag, weight, dw):
        state, token = _scatter_start(tag, weight, dw)
        in_flight.append(state)
        return token

    loss_part, dx, small_g = _local_step(x[0], loss_target[0], {k: w[k] for k in SMALL},
                                         unstack(_allgather_wait(gather_first, gather_token)),
                                         lambda after: unstack(_allgather_wait(gather_rest, after)), push)

    halves = {}
    for state in in_flight:
        tag, weight = state[0], state[1]
        dw, land = _scatter_wait(state, dx)
        halves[tag] = _sum_pieces(tag, weight, dw, land, where)
    reduced = _exchange_halves(halves)
    grad = {"w_in_even": reduced["in_even"], "w_out_even": reduced["out_even"],
            "w_in_odd": reduced["in_odd"], "w_out_odd": reduced["out_odd"],
            "w_mlp_up": jnp.concatenate([reduced["mlp_up0"], reduced["mlp_up1"]], axis=0),
            "w_mlp_down": jnp.concatenate([reduced["mlp_down0"], reduced["mlp_down1"]], axis=0)}
    delta, new_m, new_v = {}, {}, {}
    for k in BIG:
        d_k, m_k, v_k = _adamw(f"adamw_{k}", flat(w[k]), flat(grad[k]), flat(m[k]), flat(v[k]))
        delta[k], new_m[k], new_v[k] = d_k.reshape(w[k].shape), m_k.reshape(w[k].shape), v_k.reshape(w[k].shape)

    slabs = _small_allreduce_adamw(_pack_small(small_g), _pack_small({k: w[k] for k in SMALL}),
                                   _pack_small({k: m[k] for k in SMALL}), _pack_small({k: v[k] for k in SMALL}), loss_part)
    loss = slabs[0][LOSS_ROW, 0]
    for out, slab in zip((grad, delta, new_m, new_v), slabs):
        out.update(_unpack_small(slab))

    return (loss, dx[None], *[grad[k] for k in WEIGHTS], *[delta[k] for k in WEIGHTS],
            *[new_m[k] for k in WEIGHTS], *[new_v[k] for k in WEIGHTS])
```

```python
import functools
import math

import jax
import jax.numpy as jnp
from jax import lax
from jax.experimental import pallas as pl
from jax.experimental.pallas import tpu as pltpu

F32 = jnp.float32
BF16 = jnp.bfloat16
_MXU = BF16
_WIRE = BF16

D_MODEL = 1024
HEAD_DIM = 128
EPS = 1e-6
NEG_INF = -1e30
CHUNK = 128
RET_CHUNKS_PER_STEP = 4
GRID_W = 64
RET_HEADS, RET_DK, RET_DV = 4, 128, 256
RET_Q, RET_V = RET_HEADS * RET_DK, RET_HEADS * RET_DV
RET_THETA = 10000.0
SWA_HEADS, SWA_KV_HEADS = 8, 2
T5_BUCKETS, T5_MAX_DIST = 32, 128
AX_HEADS, AX_KV_HEADS = 8, 2
AX_THETA = 10000.0
D_FF = 4 * D_MODEL
EVEN_IN = 2 * RET_Q + 2 * RET_V + D_MODEL + 2 * SWA_KV_HEADS * HEAD_DIM
ODD_IN = D_MODEL + 2 * AX_KV_HEADS * HEAD_DIM
ATT_SCALE = HEAD_DIM ** -0.5
SCORE_SCALE_LOG2 = ATT_SCALE * math.log2(math.e)

ADAM_LR, ADAM_B1, ADAM_B2, ADAM_EPS, ADAM_WD, ADAM_STEP = 0.001, 0.9, 0.999, 1e-08, 0.01, 10

N_DEV = 8
VMEM_LIMIT_BYTES = 56 << 20
MESH = pl.DeviceIdType.MESH

_NN = (((1,), (0,)), ((), ()))
_NT = (((1,), (1,)), ((), ()))
_TN = (((0,), (0,)), ((), ()))


def _dot(a, b, dn=_NN):
    return lax.dot_general(a.astype(_MXU), b.astype(_MXU), dn, preferred_element_type=F32)


def _params(*sem):
    return pltpu.CompilerParams(dimension_semantics=sem, vmem_limit_bytes=VMEM_LIMIT_BYTES)


def _sds(shape, dtype):
    return jax.ShapeDtypeStruct(tuple(shape), dtype)


def _rowsum8(x):
    return jnp.sum(x.reshape(x.shape[0] // 8, 8, x.shape[1]), axis=0)


def _swap_halves(x, half):
    width = x.shape[1]
    lane = lax.broadcasted_iota(jnp.int32, x.shape, 1)
    up = pltpu.roll(x, width - half, axis=1)
    down = pltpu.roll(x, half, axis=1)
    return jnp.where((lane & (2 * half - 1)) < half, up, down)


def _sigmoid(x):
    return 1.0 / (1.0 + jnp.exp(-x))


def _norm_matmul(name, x, gain, w, *, tm, tn, out_dtype):
    T, K = x.shape
    N = w.shape[1]
    tm, tn = min(tm, T), min(tn, N)

    def body(x_ref, g_ref, w_ref, y_ref, h_ref, h_sc):
        @pl.when(pl.program_id(1) == 0)
        def _():
            xv = x_ref[...]
            r = lax.rsqrt(jnp.mean(xv * xv, axis=-1, keepdims=True) + EPS)
            h = (xv * r * g_ref[...]).astype(_MXU)
            h_sc[...] = h
            h_ref[...] = h
        y_ref[...] = jnp.dot(h_sc[...], w_ref[...], preferred_element_type=F32).astype(y_ref.dtype)

    return pl.pallas_call(
        body, name=name, grid=(T // tm, N // tn),
        in_specs=[pl.BlockSpec((tm, K), lambda i, j: (i, 0)),
                  pl.BlockSpec((1, K), lambda i, j: (0, 0)),
                  pl.BlockSpec((K, tn), lambda i, j: (0, j))],
        out_specs=[pl.BlockSpec((tm, tn), lambda i, j: (i, j)),
                   pl.BlockSpec((tm, K), lambda i, j: (i, 0))],
        out_shape=[_sds((T, N), out_dtype), _sds((T, K), _MXU)],
        scratch_shapes=[pltpu.VMEM((tm, K), _MXU)],
        compiler_params=_params("parallel", "arbitrary"),
    )(x, gain, w)


def _matmul_res(name, a_list, w, res, *, tm, relu2=False, target=None):
    T = res.shape[0]
    N = w.shape[1]
    K = a_list[0].shape[1]
    n_a = len(a_list)
    tm = min(tm, T)
    with_loss = target is not None

    def body(*refs):
        a_refs = refs[:n_a]
        w_refs = refs[n_a:2 * n_a]
        res_ref = refs[2 * n_a]
        acc = res_ref[...]
        for a_ref, w_ref in zip(a_refs, w_refs):
            a = a_ref[...]
            if relu2:
                a = jnp.square(jnp.maximum(a.astype(F32), 0.0))
            acc = acc + _dot(a, w_ref[...])
        if with_loss:
            tgt_ref, g_ref, g16_ref, loss_ref = refs[2 * n_a + 1:]
            diff = acc - tgt_ref[...]
            g = diff * (1.0 / N)
            g_ref[...] = g
            g16_ref[...] = g.astype(g16_ref.dtype)

            @pl.when(pl.program_id(0) == 0)
            def _():
                loss_ref[...] = jnp.zeros_like(loss_ref)
            loss_ref[...] += _rowsum8(diff * diff)
        else:
            refs[2 * n_a + 1][...] = acc

    row = lambda i: (i, 0)
    in_specs = [pl.BlockSpec((tm, K), row) for _ in a_list]
    in_specs += [pl.BlockSpec((K, N), functools.partial(lambda i, b: (b, 0), b=b)) for b in range(n_a)]
    in_specs += [pl.BlockSpec((tm, N), row)]
    args = list(a_list) + [w] * n_a + [res]
    if with_loss:
        in_specs.append(pl.BlockSpec((tm, N), row))
        args.append(target)
        out_specs = [pl.BlockSpec((tm, N), row), pl.BlockSpec((tm, N), row), pl.BlockSpec((8, N), lambda i: (0, 0))]
        out_shape = [_sds((T, N), F32), _sds((T, N), _MXU), _sds((8, N), F32)]
        sem = "arbitrary"
    else:
        out_specs = pl.BlockSpec((tm, N), row)
        out_shape = _sds((T, N), F32)
        sem = "parallel"
    return pl.pallas_call(body, name=name, grid=(T // tm,), in_specs=in_specs, out_specs=out_specs,
                          out_shape=out_shape, compiler_params=_params(sem))(*args)


def _matmul_nt(name, a, w, *, tm, tn, out_dtype, relu_of=None):
    T, K = a.shape
    N = w.shape[0]
    tm, tn = min(tm, T), min(tn, N)

    def body(*refs):
        if relu_of is None:
            a_ref, w_ref, o_ref = refs
            o_ref[...] = _dot(a_ref[...], w_ref[...], _NT).astype(o_ref.dtype)
        else:
            a_ref, w_ref, u_ref, o_ref = refs
            da = _dot(a_ref[...], w_ref[...], _NT)
            o_ref[...] = (da * (2.0 * jnp.maximum(u_ref[...].astype(F32), 0.0))).astype(o_ref.dtype)

    in_specs = [pl.BlockSpec((tm, K), lambda i, j: (i, 0)), pl.BlockSpec((tn, K), lambda i, j: (j, 0))]
    args = [a, w]
    if relu_of is not None:
        in_specs.append(pl.BlockSpec((tm, tn), lambda i, j: (i, j)))
        args.append(relu_of)
    return pl.pallas_call(body, name=name, grid=(T // tm, N // tn), in_specs=in_specs,
                          out_specs=pl.BlockSpec((tm, tn), lambda i, j: (i, j)),
                          out_shape=_sds((T, N), out_dtype),
                          compiler_params=_params("parallel", "parallel"))(*args)


def _matmul_nt_normbwd(name, dy, w, x, gain, dres, *, tm):
    T, K = dy.shape
    N = w.shape[0]
    tm = min(tm, T)

    def body(dy_ref, w_ref, x_ref, g_ref, dres_ref, dx_ref, dx16_ref, dg_ref):
        dh = _dot(dy_ref[...], w_ref[...], _NT)
        xv = x_ref[...]
        r = lax.rsqrt(jnp.mean(xv * xv, axis=-1, keepdims=True) + EPS)
        xhat = xv * r
        dxhat = dh * g_ref[...]
        dx = dres_ref[...] + r * (dxhat - xhat * jnp.mean(dxhat * xhat, axis=-1, keepdims=True))
        dx_ref[...] = dx
        dx16_ref[...] = dx.astype(dx16_ref.dtype)

        @pl.when(pl.program_id(0) == 0)
        def _():
            dg_ref[...] = jnp.zeros_like(dg_ref)
        dg_ref[...] += _rowsum8(dh * xhat)

    row = lambda i: (i, 0)
    return pl.pallas_call(
        body, name=name, grid=(T // tm,),
        in_specs=[pl.BlockSpec((tm, K), row), pl.BlockSpec((N, K), lambda i: (0, 0)),
                  pl.BlockSpec((tm, N), row), pl.BlockSpec((1, N), lambda i: (0, 0)), pl.BlockSpec((tm, N), row)],
        out_specs=[pl.BlockSpec((tm, N), row), pl.BlockSpec((tm, N), row), pl.BlockSpec((8, N), lambda i: (0, 0))],
        out_shape=[_sds((T, N), F32), _sds((T, N), _MXU), _sds((8, N), F32)],
        compiler_params=_params("arbitrary"),
    )(dy, w, x, gain, dres)


def _matmul_tn(name, a, b, *, tk, tn, tt, out_dtype, relu2=False):
    T, Ka = a.shape
    Nb = b.shape[1]
    tk, tn, tt = min(tk, Ka), min(tn, Nb), min(tt, T)
    nt = T // tt

    def body(a_ref, b_ref, o_ref, acc):
        t = pl.program_id(2)

        @pl.when(t == 0)
        def _():
            acc[...] = jnp.zeros_like(acc)
        av = a_ref[...]
        if relu2:
            av = jnp.square(jnp.maximum(av.astype(F32), 0.0))
        acc[...] += _dot(av, b_ref[...], _TN)

        @pl.when(t == nt - 1)
        def _():
            o_ref[...] = acc[...].astype(o_ref.dtype)

    return pl.pallas_call(
        body, name=name, grid=(Ka // tk, Nb // tn, nt),
        in_specs=[pl.BlockSpec((tt, tk), lambda i, j, t: (t, i)), pl.BlockSpec((tt, tn), lambda i, j, t: (t, j))],
        out_specs=pl.BlockSpec((tk, tn), lambda i, j, t: (i, j)),
        out_shape=_sds((Ka, Nb), out_dtype),
        scratch_shapes=[pltpu.VMEM((tk, tn), F32)],
        compiler_params=_params("parallel", "parallel", "arbitrary"),
    )(a, b)


def _rope_angles(pos, dim, theta):
    inv = theta ** (-jnp.arange(0, dim, 2, dtype=F32) / dim)
    return pos.astype(F32)[:, None] * inv[None, :]


def _ret_rope_tables(T):
    ang = _rope_angles(jnp.arange(T), RET_DK, RET_THETA)
    c, s = jnp.cos(ang), jnp.sin(ang)
    return jnp.concatenate([c, c], axis=1), jnp.concatenate([-s, s], axis=1)


def _axial_rope_tables(T):
    rows = T // GRID_W
    ar = _rope_angles(jnp.arange(rows), HEAD_DIM // 2, AX_THETA)
    ac = _rope_angles(jnp.arange(GRID_W), HEAD_DIM // 2, AX_THETA)
    by_row = lambda a: jnp.repeat(a, GRID_W, axis=0)
    by_col = lambda a: jnp.tile(a, (rows, 1))
    cos = jnp.concatenate([by_row(jnp.cos(ar)), by_row(jnp.cos(ar)), by_col(jnp.cos(ac)), by_col(jnp.cos(ac))], axis=1)
    sin = jnp.concatenate([by_row(-jnp.sin(ar)), by_row(jnp.sin(ar)), by_col(-jnp.sin(ac)), by_col(jnp.sin(ac))], axis=1)
    return cos, sin


(TAB_D, TAB_DT, TAB_EF, TAB_EB, TAB_A, TAB_B, TAB_CF, TAB_CB,
 TAB_RA, TAB_RB, TAB_RCF, TAB_RCB, TAB_KF, TAB_KB) = range(14)


def _retention_tables(decay_logit):
    lg = jax.nn.log_sigmoid(decay_logit.astype(F32))
    lam, mu = lg[0][:, None, None], lg[1][:, None, None]
    idx = jnp.arange(CHUNK, dtype=F32)
    diff = (idx[:, None] - idx[None, :])[None]
    df = jnp.where(diff >= 0, jnp.exp(jnp.maximum(diff, 0.0) * lam), 0.0)
    db = jnp.where(diff < 0, jnp.exp(jnp.maximum(-diff, 0.0) * mu), 0.0)
    d = df + db
    r = idx[None, :, None]
    ones = jnp.ones((1, 1, CHUNK), F32)
    a = jnp.exp((r + 1.0) * lam) * ones
    b = jnp.exp((CHUNK - r) * mu) * ones
    cf = jnp.exp((CHUNK - 1.0 - r) * lam) * ones
    cb = jnp.exp(r * mu) * ones
    full = jnp.ones((1, CHUNK, CHUNK), F32)
    kf = CHUNK * jnp.exp(CHUNK * lam) * full
    kb = CHUNK * jnp.exp(CHUNK * mu) * full
    tabs = jnp.stack([d, jnp.swapaxes(d, 1, 2), diff * df, -diff * db, a, b, cf, cb,
                      (r + 1.0) * a, (CHUNK - r) * b, (CHUNK - 1.0 - r) * cf, r * cb, kf, kb], axis=1)

    def lanes(tab):
        return jnp.transpose(tab, (1, 0, 2)).reshape(CHUNK, RET_HEADS * CHUNK)

    def dec(l):
        return jnp.exp(CHUNK * l)[:, 0, :] * jnp.ones((1, RET_DV), F32)

    weights = dict(a=lanes(a), b=lanes(b), cf=lanes(cf), cb=lanes(cb), dec_f=dec(lam), dec_b=dec(mu))
    return tabs, weights, lg


def _t5_bucket(rel):
    nb = T5_BUCKETS // 2
    max_exact = nb // 2
    ret = jnp.where(rel > 0, nb, 0)
    n = jnp.abs(rel)
    nf = jnp.maximum(n, 1).astype(F32)
    large = max_exact + (jnp.log(nf / max_exact) / math.log(T5_MAX_DIST / max_exact)
                         * (nb - max_exact)).astype(jnp.int32)
    large = jnp.minimum(large, nb - 1)
    return ret + jnp.where(n < max_exact, n, large)


def _swa_rel():
    r = jnp.arange(CHUNK)
    j = jnp.arange(3 * CHUNK)
    return j[None, :] - CHUNK - r[:, None]


def _swa_bias(t5_table):
    rel = _swa_rel()
    bucket = jnp.where(jnp.abs(rel) <= CHUNK, _t5_bucket(rel), -1).astype(jnp.int32)

    def body(tab_ref, bk_ref, o_ref):
        bk = bk_ref[...]
        for h in range(SWA_HEADS):
            pick = lambda b, acc, h=h: jnp.where(bk == b, tab_ref[b, h], acc)
            o_ref[h] = lax.fori_loop(0, T5_BUCKETS, pick, jnp.full(bk.shape, NEG_INF, F32))

    return pl.pallas_call(
        body, name="t5_bias",
        in_specs=[pl.BlockSpec(memory_space=pltpu.SMEM), pl.BlockSpec(memory_space=pltpu.VMEM)],
        out_specs=pl.BlockSpec(memory_space=pltpu.VMEM),
        out_shape=_sds((SWA_HEADS, CHUNK, 3 * CHUNK), F32),
    )(t5_table.astype(F32), bucket)


def _prep_even(proj, cos, sin, q_gain, k_gain, *, tm):
    T = proj.shape[0]
    tm = min(tm, T)

    def body(qa_ref, ka_ref, qb_ref, kb_ref, c_ref, s_ref, qg_ref, kg_ref, qr_ref, kr_ref, qn_ref, kn_ref):
        c = jnp.concatenate([c_ref[...]] * RET_HEADS, axis=1)
        s = jnp.concatenate([s_ref[...]] * RET_HEADS, axis=1)
        qa = qa_ref[...]
        qr_ref[...] = (qa * c + _swap_halves(qa, RET_DK // 2) * s).astype(qr_ref.dtype)
        ka = ka_ref[...]
        kr_ref[...] = ((ka * c + _swap_halves(ka, RET_DK // 2) * s) * (RET_DK ** -0.5)).astype(kr_ref.dtype)
        for src, gain, dst, heads in ((qb_ref, qg_ref, qn_ref, SWA_HEADS), (kb_ref, kg_ref, kn_ref, SWA_KV_HEADS)):
            for h in range(heads):
                sl = slice(h * HEAD_DIM, (h + 1) * HEAD_DIM)
                xh = src[:, sl]
                r = lax.rsqrt(jnp.mean(xh * xh, axis=-1, keepdims=True) + EPS)
                dst[:, sl] = (xh * r * gain[...]).astype(dst.dtype)

    row = lambda i: (i, 0)
    const = lambda i: (0, 0)
    return pl.pallas_call(
        body, name="prep_even", grid=(T // tm,),
        in_specs=[pl.BlockSpec((tm, RET_Q), lambda i: (i, 0)), pl.BlockSpec((tm, RET_Q), lambda i: (i, 1)),
                  pl.BlockSpec((tm, D_MODEL), lambda i: (i, 3)), pl.BlockSpec((tm, 256), lambda i: (i, 16)),
                  pl.BlockSpec((tm, RET_DK), row), pl.BlockSpec((tm, RET_DK), row),
                  pl.BlockSpec((1, HEAD_DIM), const), pl.BlockSpec((1, HEAD_DIM), const)],
        out_specs=[pl.BlockSpec((tm, RET_Q), row), pl.BlockSpec((tm, RET_Q), row),
                   pl.BlockSpec((tm, D_MODEL), row), pl.BlockSpec((tm, 256), row)],
        out_shape=[_sds((T, RET_Q), _MXU), _sds((T, RET_Q), _MXU), _sds((T, D_MODEL), _MXU), _sds((T, 256), _MXU)],
        compiler_params=_params("parallel"),
    )(proj, proj, proj, proj, cos, sin, q_gain, k_gain)


def _ret_scan(name, x, y, y_col, w_asc, dec_asc, w_desc, dec_desc):
    T = x.shape[0]
    nc = T // CHUNK
    per = min(RET_CHUNKS_PER_STEP, nc)
    nb = nc // per
    rows_per = per * CHUNK

    def body(xa_ref, ya_ref, xd_ref, yd_ref, wa_ref, da_ref, wd_ref, dd_ref, sa_out, sd_out, sa, sd):
        @pl.when(pl.program_id(0) == 0)
        def _():
            sa[...] = jnp.zeros_like(sa)
            sd[...] = jnp.zeros_like(sd)
        for step in range(per):
            for c, x_ref, y_ref, w_ref, d_ref, st, out in ((step, xa_ref, ya_ref, wa_ref, da_ref, sa, sa_out),
                                                       (per - 1 - step, xd_ref, yd_ref, wd_ref, dd_ref, sd, sd_out)):
                rows = slice(c * CHUNK, (c + 1) * CHUNK)
                out[c] = st[...].astype(out.dtype)
                for h in range(RET_HEADS):
                    ks = slice(h * RET_DK, (h + 1) * RET_DK)
                    vs = slice(h * RET_DV, (h + 1) * RET_DV)
                    u = _dot(x_ref[rows, ks].astype(F32) * w_ref[:, ks], y_ref[rows, vs], _TN)
                    st[ks, :] = st[ks, :] * d_ref[h:h + 1, :] + u

    asc = lambda i: (i, 0)
    desc = lambda i: (nb - 1 - i, 0)
    const = lambda i: (0, 0)
    return pl.pallas_call(
        body, name=name, grid=(nb,),
        in_specs=[pl.BlockSpec((rows_per, RET_Q), asc), pl.BlockSpec((rows_per, RET_V), lambda i: (i, y_col)),
                  pl.BlockSpec((rows_per, RET_Q), desc), pl.BlockSpec((rows_per, RET_V), lambda i: (nb - 1 - i, y_col)),
                  pl.BlockSpec((CHUNK, RET_Q), const), pl.BlockSpec((RET_HEADS, RET_DV), const),
                  pl.BlockSpec((CHUNK, RET_Q), const), pl.BlockSpec((RET_HEADS, RET_DV), const)],
        out_specs=[pl.BlockSpec((per, RET_Q, RET_DV), lambda i: (i, 0, 0)),
                   pl.BlockSpec((per, RET_Q, RET_DV), lambda i: (nb - 1 - i, 0, 0))],
        out_shape=[_sds((nc, RET_Q, RET_DV), _MXU), _sds((nc, RET_Q, RET_DV), _MXU)],
        scratch_shapes=[pltpu.VMEM((RET_Q, RET_DV), F32), pltpu.VMEM((RET_Q, RET_DV), F32)],
        compiler_params=_params("arbitrary"),
    )(x, y, x, y, w_asc, dec_asc, w_desc, dec_desc)


def _ret_out(qr, kr, proj, sf, sb, tabs, gain):
    T = qr.shape[0]
    nc = T // CHUNK
    per = min(RET_CHUNKS_PER_STEP, nc)
    rows_per = per * CHUNK

    def body(q_ref, k_ref, v_ref, g_ref, sf_ref, sb_ref, tab_ref, gain_ref, o_ref, y_ref):
        for c in range(per):
            rows = slice(c * CHUNK, (c + 1) * CHUNK)
            for h in range(RET_HEADS):
                ks = slice(h * RET_DK, (h + 1) * RET_DK)
                vs = slice(h * RET_DV, (h + 1) * RET_DV)
                q, k, v = q_ref[rows, ks], k_ref[rows, ks], v_ref[rows, vs]
                qf = q.astype(F32)
                a_mat = _dot(q, k, _NT) * tab_ref[h, 0]
                o = (_dot(a_mat, v) + _dot(qf * tab_ref[h, 1], sf_ref[c, ks, :]) + _dot(qf * tab_ref[h, 2], sb_ref[c, ks, :]))
                o_ref[rows, vs] = o
                r = lax.rsqrt(jnp.mean(o * o, axis=-1, keepdims=True) + EPS)
                g = g_ref[rows, vs]
                y_ref[rows, vs] = (g * _sigmoid(g) * (o * r * gain_ref[:, vs])).astype(y_ref.dtype)

    row = lambda i: (i, 0)
    st = lambda i: (i, 0, 0)
    return pl.pallas_call(
        body, name="ret_out", grid=(nc // per,),
        in_specs=[pl.BlockSpec((rows_per, RET_Q), row), pl.BlockSpec((rows_per, RET_Q), row),
                  pl.BlockSpec((rows_per, RET_V), lambda i: (i, 1)), pl.BlockSpec((rows_per, RET_V), lambda i: (i, 2)),
                  pl.BlockSpec((per, RET_Q, RET_DV), st), pl.BlockSpec((per, RET_Q, RET_DV), st),
                  pl.BlockSpec((RET_HEADS, 3, CHUNK, CHUNK), lambda i: (0, 0, 0, 0)),
                  pl.BlockSpec((1, RET_V), lambda i: (0, 0))],
        out_specs=[pl.BlockSpec((rows_per, RET_V), row), pl.BlockSpec((rows_per, RET_V), row)],
        out_shape=[_sds((T, RET_V), F32), _sds((T, RET_V), _MXU)],
        compiler_params=_params("parallel"),
    )(qr, kr, proj, proj, sf, sb, tabs, gain)


def _ret_gate_bwd(dycat, proj, ret_o, gain, *, tm):
    T = ret_o.shape[0]
    tm = min(tm, T)

    def body(dy_ref, g_ref, o_ref, gain_ref, do_ref, dg_ref, dgain_ref):
        @pl.when(pl.program_id(0) == 0)
        def _():
            dgain_ref[...] = jnp.zeros_like(dgain_ref)
        for h in range(RET_HEADS):
            vs = slice(h * RET_DV, (h + 1) * RET_DV)
            o, g, dya, gn = o_ref[:, vs], g_ref[:, vs], dy_ref[:, vs], gain_ref[:, vs]
            r = lax.rsqrt(jnp.mean(o * o, axis=-1, keepdims=True) + EPS)
            ohat = o * r
            sg = _sigmoid(g)
            dy = dya * (g * sg)
            dg_ref[:, vs] = (dya * (ohat * gn) * (sg * (1.0 + g * (1.0 - sg)))).astype(dg_ref.dtype)
            dyg = dy * gn
            do_ref[:, vs] = (r * (dyg - ohat * jnp.mean(dyg * ohat, axis=-1, keepdims=True))).astype(do_ref.dtype)
            dgain_ref[:, vs] += _rowsum8(dy * ohat)

    row = lambda i: (i, 0)
    return pl.pallas_call(
        body, name="ret_gate_bwd", grid=(T // tm,),
        in_specs=[pl.BlockSpec((tm, RET_V), row), pl.BlockSpec((tm, RET_V), lambda i: (i, 2)),
                  pl.BlockSpec((tm, RET_V), row), pl.BlockSpec((1, RET_V), lambda i: (0, 0))],
        out_specs=[pl.BlockSpec((tm, RET_V), row), pl.BlockSpec((tm, RET_V), row), pl.BlockSpec((8, RET_V), lambda i: (0, 0))],
        out_shape=[_sds((T, RET_V), _MXU), _sds((T, RET_V), _MXU), _sds((8, RET_V), F32)],
        compiler_params=_params("arbitrary"),
    )(dycat, proj, ret_o, gain)


def _ret_bwd(qr, kr, proj, g_out, sf, sb, rf, rb, tabs):
    T = qr.shape[0]
    nc = T // CHUNK
    per = min(RET_CHUNKS_PER_STEP, nc)
    rows_per = per * CHUNK

    def body(q_ref, k_ref, v_ref, g_ref, sf_ref, sb_ref, rf_ref, rb_ref, tab_ref, dq_ref, dk_ref, dv_ref, dl_ref):
        @pl.when(pl.program_id(0) == 0)
        def _():
            dl_ref[...] = jnp.zeros_like(dl_ref)
        for c in range(per):
            rows = slice(c * CHUNK, (c + 1) * CHUNK)
            for h in range(RET_HEADS):
                ks = slice(h * RET_DK, (h + 1) * RET_DK)
                vs = slice(h * RET_DV, (h + 1) * RET_DV)
                q, k, v, g = q_ref[rows, ks], k_ref[rows, ks], v_ref[rows, vs], g_ref[rows, vs]
                s_f, s_b, r_f, r_b = sf_ref[c, ks, :], sb_ref[c, ks, :], rf_ref[c, ks, :], rb_ref[c, ks, :]
                tab = lambda t, h=h: tab_ref[h, t]
                qf, kf = q.astype(F32), k.astype(F32)
                qk = _dot(q, k, _NT)
                da_raw = _dot(g, v, _NT)
                x_f, x_b = _dot(g, s_f, _NT), _dot(g, s_b, _NT)
                dq_ref[rows, ks] = _dot(da_raw * tab(TAB_D), k) + tab(TAB_A) * x_f + tab(TAB_B) * x_b
                at = _dot(k, q, _NT) * tab(TAB_DT)
                dat = _dot(v, g, _NT) * tab(TAB_DT)
                y_f, y_b = _dot(v, r_f, _NT), _dot(v, r_b, _NT)
                dk_ref[rows, ks] = _dot(dat, q) + tab(TAB_CF) * y_f + tab(TAB_CB) * y_b
                dv_ref[rows, vs] = (_dot(at, g) + _dot(kf * tab(TAB_CF), r_f) + _dot(kf * tab(TAB_CB), r_b)).astype(dv_ref.dtype)
                inner = da_raw * qk
                rs_f = r_f.astype(F32) * s_f.astype(F32)
                rs_b = r_b.astype(F32) * s_b.astype(F32)
                dl_f = (inner * tab(TAB_EF) + tab(TAB_RA) * qf * x_f + tab(TAB_RCF) * kf * y_f
                        + tab(TAB_KF) * (rs_f[:, :CHUNK] + rs_f[:, CHUNK:]))
                dl_b = (inner * tab(TAB_EB) + tab(TAB_RB) * qf * x_b + tab(TAB_RCB) * kf * y_b
                        + tab(TAB_KB) * (rs_b[:, :CHUNK] + rs_b[:, CHUNK:]))
                dl_ref[2 * h:2 * h + 1, :] += jnp.sum(dl_f, axis=0, keepdims=True)
                dl_ref[2 * h + 1:2 * h + 2, :] += jnp.sum(dl_b, axis=0, keepdims=True)

    row = lambda i: (i, 0)
    st = lambda i: (i, 0, 0)
    return pl.pallas_call(
        body, name="ret_bwd", grid=(nc // per,),
        in_specs=[pl.BlockSpec((rows_per, RET_Q), row), pl.BlockSpec((rows_per, RET_Q), row),
                  pl.BlockSpec((rows_per, RET_V), lambda i: (i, 1)), pl.BlockSpec((rows_per, RET_V), row),
                  pl.BlockSpec((per, RET_Q, RET_DV), st), pl.BlockSpec((per, RET_Q, RET_DV), st),
                  pl.BlockSpec((per, RET_Q, RET_DV), st), pl.BlockSpec((per, RET_Q, RET_DV), st),
                  pl.BlockSpec((RET_HEADS, 14, CHUNK, CHUNK), lambda i: (0, 0, 0, 0))],
        out_specs=[pl.BlockSpec((rows_per, RET_Q), row), pl.BlockSpec((rows_per, RET_Q), row),
                   pl.BlockSpec((rows_per, RET_V), row), pl.BlockSpec((8, CHUNK), lambda i: (0, 0))],
        out_shape=[_sds((T, RET_Q), F32), _sds((T, RET_Q), F32), _sds((T, RET_V), _MXU), _sds((8, CHUNK), F32)],
        compiler_params=_params("arbitrary"),
    )(qr, kr, proj, g_out, sf, sb, rf, rb, tabs)


def _swa_probs(q, k_win, bias, sink, valid):
    s = _dot(q, k_win, _NT) * ATT_SCALE + bias
    s = jnp.where(valid, s, NEG_INF)
    m = jnp.maximum(jnp.max(s, axis=-1, keepdims=True), sink)
    p = jnp.exp(s - m)
    e_sink = jnp.exp(sink - m)
    inv = 1.0 / (jnp.sum(p, axis=-1, keepdims=True) + e_sink)
    return p * inv, e_sink * inv


def _swa_group(g, q_ref, bias_ref, sink_ref):
    group = SWA_HEADS // SWA_KV_HEADS
    heads = range(g * group, (g + 1) * group)
    q = jnp.concatenate([q_ref[:, h * HEAD_DIM:(h + 1) * HEAD_DIM] for h in heads], axis=0)
    bias = bias_ref[g * group:(g + 1) * group].reshape(group * CHUNK, 3 * CHUNK)
    sink = jnp.concatenate([jnp.broadcast_to(sink_ref[h:h + 1, 0:1], (CHUNK, 1)) for h in heads], axis=0)
    return q, bias, sink


def _swa_valid(i, nb):
    col = lax.broadcasted_iota(jnp.int32, (1, 3 * CHUNK), 1)
    return jnp.logical_and(jnp.logical_or(col >= CHUNK, i > 0), jnp.logical_or(col < 2 * CHUNK, i < nb - 1))


def _swa_window_specs(nb, width, col_block, clamp):
    prev = lambda i: (jnp.maximum(clamp(i) - 1, 0), col_block)
    cur = lambda i: (clamp(i), col_block)
    nxt = lambda i: (jnp.minimum(clamp(i) + 1, nb - 1), col_block)
    return [pl.BlockSpec((CHUNK, width), f) for f in (prev, cur, nxt)]


def _swa_fwd(qn, kn, proj, bias, sink):
    T = qn.shape[0]
    nb = T // CHUNK
    kvw = SWA_KV_HEADS * HEAD_DIM
    group = SWA_HEADS // SWA_KV_HEADS

    def body(q_ref, k0, k1, k2, v0, v1, v2, bias_ref, sink_ref, y_ref):
        i = pl.program_id(0)
        valid = _swa_valid(i, nb)
        for g in range(SWA_KV_HEADS):
            gs = slice(g * HEAD_DIM, (g + 1) * HEAD_DIM)
            k_win = jnp.concatenate([k0[:, gs], k1[:, gs], k2[:, gs]], axis=0)
            v_win = jnp.concatenate([v0[:, gs], v1[:, gs], v2[:, gs]], axis=0).astype(_MXU)
            q, bias, sink = _swa_group(g, q_ref, bias_ref, sink_ref)
            p, _ = _swa_probs(q, k_win, bias, sink, valid)
            o = _dot(p, v_win)
            for hh in range(group):
                h = g * group + hh
                y_ref[:, h * HEAD_DIM:(h + 1) * HEAD_DIM] = o[hh * CHUNK:(hh + 1) * CHUNK].astype(y_ref.dtype)

    ident = lambda i: i
    return pl.pallas_call(
        body, name="swa_fwd", grid=(nb,),
        in_specs=[pl.BlockSpec((CHUNK, D_MODEL), lambda i: (i, 0))]
        + _swa_window_specs(nb, kvw, 0, ident) + _swa_window_specs(nb, kvw, 17, ident)
        + [pl.BlockSpec((SWA_HEADS, CHUNK, 3 * CHUNK), lambda i: (0, 0, 0)), pl.BlockSpec((SWA_HEADS, HEAD_DIM), lambda i: (0, 0))],
        out_specs=pl.BlockSpec((CHUNK, D_MODEL), lambda i: (i, 0)),
        out_shape=_sds((T, D_MODEL), _MXU),
        compiler_params=_params("parallel"),
    )(qn, kn, kn, kn, proj, proj, proj, bias, sink)


def _swa_bwd(qn, kn, proj, dycat, bias, sink):
    T = qn.shape[0]
    nb = T // CHUNK
    kvw = SWA_KV_HEADS * HEAD_DIM
    group = SWA_HEADS // SWA_KV_HEADS

    def body(q_ref, k0, k1, k2, v0, v1, v2, dy_ref, bias_ref, sink_ref,
             dq_ref, dk_ref, dv_ref, dbias_ref, dsink_ref, acc_a, acc_b):
        i = pl.program_id(0)

        @pl.when(i == 0)
        def _():
            dbias_ref[...] = jnp.zeros_like(dbias_ref)
            dsink_ref[...] = jnp.zeros_like(dsink_ref)
            acc_a[...] = jnp.zeros_like(acc_a)
            acc_b[...] = jnp.zeros_like(acc_b)

        @pl.when(i < nb)
        def _():
            valid = _swa_valid(i, nb)
            for g in range(SWA_KV_HEADS):
                gs = slice(g * HEAD_DIM, (g + 1) * HEAD_DIM)
                k_win = jnp.concatenate([k0[:, gs], k1[:, gs], k2[:, gs]], axis=0)
                v_win = jnp.concatenate([v0[:, gs], v1[:, gs], v2[:, gs]], axis=0).astype(_MXU)
                q, bias, sink = _swa_group(g, q_ref, bias_ref, sink_ref)
                dy = jnp.concatenate([dy_ref[:, h * HEAD_DIM:(h + 1) * HEAD_DIM] for h in range(g * group, (g + 1) * group)], axis=0)
                p, p_sink = _swa_probs(q, k_win, bias, sink, valid)
                dp = _dot(dy, v_win, _NT)
                delta = jnp.sum(p * dp, axis=-1, keepdims=True)
                ds = p * (dp - delta)
                dsink = -p_sink * delta
                dq = _dot(ds, k_win) * ATT_SCALE
                for hh in range(group):
                    h = g * group + hh
                    rows = slice(hh * CHUNK, (hh + 1) * CHUNK)
                    dbias_ref[h] += ds[rows]
                    dsink_ref[h:h + 1, :] += jnp.sum(dsink[rows], axis=0, keepdims=True) * jnp.ones((1, HEAD_DIM), F32)
                    dq_ref[:, h * HEAD_DIM:(h + 1) * HEAD_DIM] = dq[rows]
                dk_win = _dot(ds, q, _TN) * ATT_SCALE
                dv_win = _dot(p, dy, _TN)
                for win, out_ref, col0 in ((dk_win, dk_ref, 0), (dv_win, dv_ref, kvw)):
                    cs = slice(col0 + g * HEAD_DIM, col0 + (g + 1) * HEAD_DIM)
                    out_ref[:, gs] = acc_a[:, cs] + win[:CHUNK]
                    acc_a[:, cs] = acc_b[:, cs] + win[CHUNK:2 * CHUNK]
                    acc_b[:, cs] = win[2 * CHUNK:]

        @pl.when(i == nb)
        def _():
            dk_ref[...] = acc_a[:, :kvw]
            dv_ref[...] = acc_a[:, kvw:]

    clamp = lambda i: jnp.minimum(i, nb - 1)
    late = lambda i: (jnp.maximum(i - 1, 0), 0)
    return pl.pallas_call(
        body, name="swa_bwd", grid=(nb + 1,),
        in_specs=[pl.BlockSpec((CHUNK, D_MODEL), lambda i: (clamp(i), 0))]
        + _swa_window_specs(nb, kvw, 0, clamp) + _swa_window_specs(nb, kvw, 17, clamp)
        + [pl.BlockSpec((CHUNK, D_MODEL), lambda i: (clamp(i), 1)),
           pl.BlockSpec((SWA_HEADS, CHUNK, 3 * CHUNK), lambda i: (0, 0, 0)), pl.BlockSpec((SWA_HEADS, HEAD_DIM), lambda i: (0, 0))],
        out_specs=[pl.BlockSpec((CHUNK, D_MODEL), lambda i: (clamp(i), 0)),
                   pl.BlockSpec((CHUNK, kvw), late), pl.BlockSpec((CHUNK, kvw), late),
                   pl.BlockSpec((SWA_HEADS, CHUNK, 3 * CHUNK), lambda i: (0, 0, 0)), pl.BlockSpec((SWA_HEADS, HEAD_DIM), lambda i: (0, 0))],
        out_shape=[_sds((T, D_MODEL), F32), _sds((T, kvw), F32), _sds((T, kvw), F32),
                   _sds((SWA_HEADS, CHUNK, 3 * CHUNK), F32), _sds((SWA_HEADS, HEAD_DIM), F32)],
        scratch_shapes=[pltpu.VMEM((CHUNK, 2 * kvw), F32), pltpu.VMEM((CHUNK, 2 * kvw), F32)],
        compiler_params=_params("arbitrary"),
    )(qn, kn, kn, kn, proj, proj, proj, dycat, bias, sink)


def _t5_bucket_reduce(dbias, bucket):
    def body(db_ref, bk_ref, o_ref):
        bk = bk_ref[...]
        row = lax.broadcasted_iota(jnp.int32, (SWA_HEADS, HEAD_DIM), 0)
        lane = lax.broadcasted_iota(jnp.int32, (SWA_HEADS, HEAD_DIM), 1)

        def per_bucket(b, acc):
            mask = bk == b
            for h in range(SWA_HEADS):
                tot = jnp.sum(jnp.sum(jnp.where(mask, db_ref[h], 0.0), axis=0, keepdims=True), axis=1, keepdims=True)
                acc = acc + jnp.where(jnp.logical_and(row == h, lane == b), tot, 0.0)
            return acc

        o_ref[...] = lax.fori_loop(0, T5_BUCKETS, per_bucket, jnp.zeros((SWA_HEADS, HEAD_DIM), F32))

    return pl.pallas_call(body, name="t5_bucket_reduce", out_shape=_sds((SWA_HEADS, HEAD_DIM), F32),
                          compiler_params=pltpu.CompilerParams(vmem_limit_bytes=VMEM_LIMIT_BYTES))(dbias, bucket)


def _headnorm_bwd(x, dy, gain):
    r = lax.rsqrt(jnp.mean(x * x, axis=-1, keepdims=True) + EPS)
    xhat = x * r
    dyg = dy * gain
    return r * (dyg - xhat * jnp.mean(dyg * xhat, axis=-1, keepdims=True)), dy * xhat


def _post_even(proj, dqr, dkr, dva, dga, dqn, dkn, dvb, cos, sin, q_gain, k_gain, *, tm):
    T = proj.shape[0]
    tm = min(tm, T)
    kvw = SWA_KV_HEADS * HEAD_DIM

    def body(qb_ref, kb_ref, dqr_ref, dkr_ref, dva_ref, dga_ref, dqn_ref, dkn_ref, dvb_ref, c_ref, s_ref, qg_ref, kg_ref,
             dp_ref, dqg_ref, dkg_ref):
        @pl.when(pl.program_id(0) == 0)
        def _():
            dqg_ref[...] = jnp.zeros_like(dqg_ref)
            dkg_ref[...] = jnp.zeros_like(dkg_ref)
        c = jnp.concatenate([c_ref[...]] * RET_HEADS, axis=1)
        s = jnp.concatenate([s_ref[...]] * RET_HEADS, axis=1)
        dq = dqr_ref[...]
        dp_ref[:, 0:RET_Q] = (dq * c + _swap_halves(dq * s, RET_DK // 2)).astype(dp_ref.dtype)
        dk = dkr_ref[...] * (RET_DK ** -0.5)
        dp_ref[:, RET_Q:2 * RET_Q] = (dk * c + _swap_halves(dk * s, RET_DK // 2)).astype(dp_ref.dtype)
        off = 2 * RET_Q
        dp_ref[:, off:off + RET_V] = dva_ref[...].astype(dp_ref.dtype)
        dp_ref[:, off + RET_V:off + 2 * RET_V] = dga_ref[...].astype(dp_ref.dtype)
        off += 2 * RET_V
        for src, dsrc, gain, dgain, heads, base in ((qb_ref, dqn_ref, qg_ref, dqg_ref, SWA_HEADS, off),
                                                    (kb_ref, dkn_ref, kg_ref, dkg_ref, SWA_KV_HEADS, off + D_MODEL)):
            for h in range(heads):
                sl = slice(h * HEAD_DIM, (h + 1) * HEAD_DIM)
                dx, dgx = _headnorm_bwd(src[:, sl], dsrc[:, sl], gain[...])
                dp_ref[:, base + h * HEAD_DIM:base + (h + 1) * HEAD_DIM] = dx.astype(dp_ref.dtype)
                dgain[...] += _rowsum8(dgx)
        dp_ref[:, off + D_MODEL + kvw:] = dvb_ref[...].astype(dp_ref.dtype)

    row = lambda i: (i, 0)
    const = lambda i: (0, 0)
    return pl.pallas_call(
        body, name="post_even", grid=(T // tm,),
        in_specs=[pl.BlockSpec((tm, D_MODEL), lambda i: (i, 3)), pl.BlockSpec((tm, kvw), lambda i: (i, 16)),
                  pl.BlockSpec((tm, RET_Q), row), pl.BlockSpec((tm, RET_Q), row),
                  pl.BlockSpec((tm, RET_V), row), pl.BlockSpec((tm, RET_V), row),
                  pl.BlockSpec((tm, D_MODEL), row), pl.BlockSpec((tm, kvw), row), pl.BlockSpec((tm, kvw), row),
                  pl.BlockSpec((tm, RET_DK), row), pl.BlockSpec((tm, RET_DK), row),
                  pl.BlockSpec((1, HEAD_DIM), const), pl.BlockSpec((1, HEAD_DIM), const)],
        out_specs=[pl.BlockSpec((tm, EVEN_IN), row), pl.BlockSpec((8, HEAD_DIM), const), pl.BlockSpec((8, HEAD_DIM), const)],
        out_shape=[_sds((T, EVEN_IN), _MXU), _sds((8, HEAD_DIM), F32), _sds((8, HEAD_DIM), F32)],
        compiler_params=_params("arbitrary"),
    )(proj, proj, dqr, dkr, dva, dga, dqn, dkn, dvb, cos, sin, q_gain, k_gain)


def _prep_odd(proj, cos, sin, q_gain, k_gain, *, tm):
    T = proj.shape[0]
    tm = min(tm, T)
    kvw = AX_KV_HEADS * HEAD_DIM

    def body(q_ref, k_ref, v_ref, c_ref, s_ref, qg_ref, kg_ref, qx_ref, kx_ref, vx_ref, v1_ref):
        c, s = c_ref[...], s_ref[...]
        for src, gain, dst, heads, scale in ((q_ref, qg_ref, qx_ref, AX_HEADS, SCORE_SCALE_LOG2), (k_ref, kg_ref, kx_ref, AX_KV_HEADS, 1.0)):
            for h in range(heads):
                sl = slice(h * HEAD_DIM, (h + 1) * HEAD_DIM)
                xh = src[:, sl]
                r = lax.rsqrt(jnp.mean(xh * xh, axis=-1, keepdims=True) + EPS)
                xn = xh * r * gain[...]
                dst[:, sl] = ((xn * c + _swap_halves(xn, HEAD_DIM // 4) * s) * scale).astype(dst.dtype)
        v = v_ref[...].astype(vx_ref.dtype)
        vx_ref[...] = v
        for g in range(AX_KV_HEADS):
            v1_ref[:, 2 * g * HEAD_DIM:(2 * g + 1) * HEAD_DIM] = v[:, g * HEAD_DIM:(g + 1) * HEAD_DIM]
            v1_ref[:, (2 * g + 1) * HEAD_DIM:(2 * g + 2) * HEAD_DIM] = jnp.ones((tm, HEAD_DIM), v1_ref.dtype)

    row = lambda i: (i, 0)
    const = lambda i: (0, 0)
    return pl.pallas_call(
        body, name="prep_odd", grid=(T // tm,),
        in_specs=[pl.BlockSpec((tm, D_MODEL), row), pl.BlockSpec((tm, kvw), lambda i: (i, 4)), pl.BlockSpec((tm, kvw), lambda i: (i, 5)),
                  pl.BlockSpec((tm, HEAD_DIM), row), pl.BlockSpec((tm, HEAD_DIM), row),
                  pl.BlockSpec((1, HEAD_DIM), const), pl.BlockSpec((1, HEAD_DIM), const)],
        out_specs=[pl.BlockSpec((tm, D_MODEL), row), pl.BlockSpec((tm, kvw), row), pl.BlockSpec((tm, kvw), row),
                   pl.BlockSpec((tm, 2 * kvw), row)],
        out_shape=[_sds((T, D_MODEL), _MXU), _sds((T, kvw), _MXU), _sds((T, kvw), _MXU), _sds((T, 2 * kvw), _MXU)],
        compiler_params=_params("parallel"),
    )(proj, proj, proj, cos, sin, q_gain, k_gain)


def _post_odd(proj, dqxt, dkx, dvx, cos, sin, q_gain, k_gain, *, tm):
    T = proj.shape[0]
    tm = min(tm, T)
    kvw = AX_KV_HEADS * HEAD_DIM

    def body(q_ref, k_ref, dqt_ref, dk_ref, dv_ref, c_ref, s_ref, qg_ref, kg_ref, dp_ref, dqg_ref, dkg_ref):
        @pl.when(pl.program_id(0) == 0)
        def _():
            dqg_ref[...] = jnp.zeros_like(dqg_ref)
            dkg_ref[...] = jnp.zeros_like(dkg_ref)
        c, s = c_ref[...], s_ref[...]
        for src, dsrc, gain, dgain, heads, base in ((q_ref, dqt_ref, qg_ref, dqg_ref, AX_HEADS, 0),
                                                    (k_ref, dk_ref, kg_ref, dkg_ref, AX_KV_HEADS, D_MODEL)):
            for h in range(heads):
                sl = slice(h * HEAD_DIM, (h + 1) * HEAD_DIM)
                d = dsrc[sl, :].T if dsrc is dqt_ref else dsrc[:, sl]
                dn = d * c + _swap_halves(d * s, HEAD_DIM // 4)
                dx, dgx = _headnorm_bwd(src[:, sl], dn, gain[...])
                dp_ref[:, base + h * HEAD_DIM:base + (h + 1) * HEAD_DIM] = dx.astype(dp_ref.dtype)
                dgain[...] += _rowsum8(dgx)
        dp_ref[:, D_MODEL + kvw:] = dv_ref[...].astype(dp_ref.dtype)

    row = lambda i: (i, 0)
    const = lambda i: (0, 0)
    return pl.pallas_call(
        body, name="post_odd", grid=(T // tm,),
        in_specs=[pl.BlockSpec((tm, D_MODEL), row), pl.BlockSpec((tm, kvw), lambda i: (i, 4)),
                  pl.BlockSpec((D_MODEL, tm), lambda i: (0, i)), pl.BlockSpec((tm, kvw), row), pl.BlockSpec((tm, kvw), row),
                  pl.BlockSpec((tm, HEAD_DIM), row), pl.BlockSpec((tm, HEAD_DIM), row),
                  pl.BlockSpec((1, HEAD_DIM), const), pl.BlockSpec((1, HEAD_DIM), const)],
        out_specs=[pl.BlockSpec((tm, ODD_IN), row), pl.BlockSpec((8, HEAD_DIM), const), pl.BlockSpec((8, HEAD_DIM), const)],
        out_shape=[_sds((T, ODD_IN), _MXU), _sds((8, HEAD_DIM), F32), _sds((8, HEAD_DIM), F32)],
        compiler_params=_params("arbitrary"),
    )(proj, proj, dqxt, dkx, dvx, cos, sin, q_gain, k_gain)


def _flash_fwd(qx, kx, v1, *, tq, tk):
    T = qx.shape[0]
    tq, tk = min(tq, T), min(tk, T)
    nq, nk = T // tq, T // tk
    group = AX_HEADS // AX_KV_HEADS

    def body(k_ref, v_ref, q_ref, o_ref, lse_ref, acc_sc, m_sc, l_sc):
        j = pl.program_id(2)

        @pl.when(j == 0)
        def _():
            m_sc[...] = jnp.full(m_sc.shape, NEG_INF, F32)
            l_sc[...] = jnp.zeros_like(l_sc)
            acc_sc[...] = jnp.zeros_like(acc_sc)
        k, v = k_ref[...], v_ref[...]

        def step(i, carry):
            cols = pl.ds(pl.multiple_of(i * tq, tq), tq)
            st = _dot(k, q_ref[cols, :], _NT)
            m_old = m_sc[i]
            m_new = jnp.maximum(m_old, jnp.max(st, axis=0, keepdims=True))
            p = jnp.exp2(st - m_new)
            alpha = jnp.exp2(m_old - m_new)
            pv = _dot(v, p, _TN)
            m_sc[i] = m_new
            l_sc[i] = alpha * l_sc[i] + pv[HEAD_DIM:HEAD_DIM + 1]
            acc_sc[:, cols] = alpha * acc_sc[:, cols] + pv[:HEAD_DIM]
            return carry

        lax.fori_loop(0, nq, step, 0)

        @pl.when(j == nk - 1)
        def _():
            def finish(i, carry):
                cols = pl.ds(pl.multiple_of(i * tq, tq), tq)
                o_ref[cols, :] = (acc_sc[:, cols] / l_sc[i]).T.astype(o_ref.dtype)
                lse_ref[0, i] = m_sc[i] + jnp.log2(l_sc[i])
                return carry

            lax.fori_loop(0, nq, finish, 0)

    kv = lambda g, h, j: (j, g)
    qh = lambda g, h, j: (0, g * group + h)
    o, lse = pl.pallas_call(
        body, name="flash_fwd", grid=(AX_KV_HEADS, group, nk),
        in_specs=[pl.BlockSpec((tk, HEAD_DIM), kv), pl.BlockSpec((tk, 2 * HEAD_DIM), kv), pl.BlockSpec((T, HEAD_DIM), qh)],
        out_specs=[pl.BlockSpec((T, HEAD_DIM), qh), pl.BlockSpec((1, nq, 1, tq), lambda g, h, j: (g * group + h, 0, 0, 0))],
        out_shape=[_sds((T, D_MODEL), _MXU), _sds((AX_HEADS, nq, 1, tq), F32)],
        scratch_shapes=[pltpu.VMEM((HEAD_DIM, T), F32), pltpu.VMEM((nq, 1, tq), F32), pltpu.VMEM((nq, 1, tq), F32)],
        compiler_params=_params("parallel", "arbitrary", "arbitrary"),
    )(kx, v1, qx)
    return o, lse.reshape(AX_HEADS, 1, T)


def _flash_bwd(qx, kx, vx, o, do, lse, *, tq, tk):
    T = qx.shape[0]
    tq, tk = min(tq, T), min(tk, T)
    nq = T // tq
    group = AX_HEADS // AX_KV_HEADS
    lse_rows = lse.reshape(AX_HEADS, nq, 1, tq)

    def body(k_ref, v_ref, q_ref, o_ref, do_ref, lse_ref, dqt_ref, dk_ref, dv_ref, delta_sc):
        j = pl.program_id(2)

        @pl.when(jnp.logical_and(pl.program_id(1) == 0, j == 0))
        def _():
            dk_ref[...] = jnp.zeros_like(dk_ref)
            dv_ref[...] = jnp.zeros_like(dv_ref)

        @pl.when(j == 0)
        def _():
            dqt_ref[...] = jnp.zeros_like(dqt_ref)

            def row_delta(i, carry):
                rows = pl.ds(pl.multiple_of(i * tq, tq), tq)
                prod = do_ref[rows, :].astype(F32) * o_ref[rows, :].astype(F32)
                delta_sc[i] = jnp.sum(prod.T, axis=0, keepdims=True)
                return carry

            lax.fori_loop(0, nq, row_delta, 0)
        k, v = k_ref[...], v_ref[...]

        def step(i, carry):
            dk, dv = carry
            off = pl.multiple_of(i * tq, tq)
            q, do_blk = q_ref[pl.ds(off, tq), :], do_ref[pl.ds(off, tq), :]
            pt = jnp.exp2(_dot(k, q, _NT) - lse_ref[0, i])
            dst = pt * (_dot(v, do_blk, _NT) - delta_sc[i])
            dqt_ref[:, pl.ds(off, tq)] += _dot(k, dst, _TN) * ATT_SCALE
            return dk + _dot(dst, q), dv + _dot(pt, do_blk)

        zero = jnp.zeros((tk, HEAD_DIM), F32)
        dk, dv = lax.fori_loop(0, nq, step, (zero, zero))
        rows = pl.ds(pl.multiple_of(j * tk, tk), tk)
        dk_ref[rows, :] += dk * (ATT_SCALE / SCORE_SCALE_LOG2)
        dv_ref[rows, :] += dv

    kv = lambda g, h, j: (j, g)
    qh = lambda g, h, j: (0, g * group + h)
    st = lambda g, h, j: (g * group + h, 0, 0, 0)
    acc = lambda g, h, j: (0, g)
    return pl.pallas_call(
        body, name="flash_bwd", grid=(AX_KV_HEADS, group, T // tk),
        in_specs=[pl.BlockSpec((tk, HEAD_DIM), kv), pl.BlockSpec((tk, HEAD_DIM), kv),
                  pl.BlockSpec((T, HEAD_DIM), qh), pl.BlockSpec((T, HEAD_DIM), qh), pl.BlockSpec((T, HEAD_DIM), qh),
                  pl.BlockSpec((1, nq, 1, tq), st)],
        out_specs=[pl.BlockSpec((HEAD_DIM, T), lambda g, h, j: (g * group + h, 0)),
                   pl.BlockSpec((T, HEAD_DIM), acc), pl.BlockSpec((T, HEAD_DIM), acc)],
        out_shape=[_sds((D_MODEL, T), F32), _sds((T, AX_KV_HEADS * HEAD_DIM), F32), _sds((T, AX_KV_HEADS * HEAD_DIM), F32)],
        scratch_shapes=[pltpu.VMEM((nq, 1, tq), F32)],
        compiler_params=_params("parallel", "arbitrary", "arbitrary"),
    )(kx, vx, qx, o, do, lse_rows)


TM = 1024
TM_WIDE = 512


def _mlp_fwd(tag, x, gain, w_up, w_down, target=None):
    u, h = _norm_matmul(f"mlp_up{tag}", x, gain, w_up, tm=TM, tn=1024, out_dtype=_MXU)
    out = _matmul_res(f"mlp_down{tag}", [u], w_down, x, tm=TM_WIDE, relu2=True, target=target)
    return out, (x, u, h)


def _local_step(x, target, p, w_first, fetch_rest, push, tokens=()):
    T = x.shape[0]
    cos_r, sin_r = _ret_rope_tables(T)
    cos_a, sin_a = _axial_rope_tables(T)
    tabs, rw, log_gamma = _retention_tables(p["ret_decay_logit"][0])
    bias = _swa_bias(p["t5_table"])
    sink = p["swa_sink"][0][:, None] * jnp.ones((1, HEAD_DIM), F32)
    nm, nl = p["norm_mix"], p["norm_mlp"]
    pending = [t for t in tokens if t is not None]

    def send(tag, weight, dw):
        token = push(tag, weight, dw[None])
        if token is not None:
            pending.append(token)

    def tied(operand):
        while pending:
            operand = operand + pending.pop()[0:1, 0:1]
        return operand

    def mlp_bwd(tag, saved, gain, w_up, w_down, dy, dy16):
        xs, u, h = saved
        du = _matmul_nt(f"mlp_down{tag}_bwd", dy16, w_down, tm=TM, tn=1024, out_dtype=_MXU, relu_of=u)
        send(f"mlp_down{tag}", "w_mlp_down", _matmul_tn(f"mlp_down{tag}_dw", u, dy16, tk=1024, tn=1024, tt=1024, out_dtype=_WIRE, relu2=True))
        dx, dx16, dgain = _matmul_nt_normbwd(f"mlp_up{tag}_bwd", du, w_up, xs, tied(gain), dy, tm=TM_WIDE)
        send(f"mlp_up{tag}", "w_mlp_up", _matmul_tn(f"mlp_up{tag}_dw", h, du, tk=1024, tn=1024, tt=1024, out_dtype=_WIRE))
        return dx, dx16, dgain

    proj0, h0 = _norm_matmul("in_even", x, tied(nm[0:1]), w_first["w_in_even"], tm=TM, tn=1152, out_dtype=F32)
    qr, kr, qn, kn = _prep_even(proj0, cos_r, sin_r, p["swa_q_norm"], p["swa_k_norm"], tm=TM)
    sf, sb = _ret_scan("ret_scan_fwd", kr, proj0, 1, rw["cf"], rw["dec_f"], rw["cb"], rw["dec_b"])
    ret_o, ya = _ret_out(qr, kr, proj0, sf, sb, tabs[:, (TAB_D, TAB_A, TAB_B)], p["ret_norm"])
    yb = _swa_fwd(qn, kn, proj0, bias, sink)
    wf = {**w_first, **fetch_rest(yb)}
    x1 = _matmul_res("out_even", [ya, yb], wf["w_out_even"], x, tm=TM)
    x2, mlp0 = _mlp_fwd(0, x1, nl[0:1], wf["w_mlp_up"][0], wf["w_mlp_down"][0])
    proj1, h1 = _norm_matmul("in_odd", x2, nm[1:2], wf["w_in_odd"], tm=TM, tn=768, out_dtype=F32)
    qx, kx, vx, v1 = _prep_odd(proj1, cos_a, sin_a, p["ax_q_norm"], p["ax_k_norm"], tm=TM)
    o, lse = _flash_fwd(qx, kx, v1, tq=2048, tk=1024)
    x3 = _matmul_res("out_odd", [o], wf["w_out_odd"], x2, tm=TM)
    (g4, g4_16, loss_part), mlp1 = _mlp_fwd(1, x3, nl[1:2], wf["w_mlp_up"][1], wf["w_mlp_down"][1], target=target)

    dx3, dx3_16, dnl1 = mlp_bwd(1, mlp1, nl[1:2], wf["w_mlp_up"][1], wf["w_mlp_down"][1], g4, g4_16)
    do = _matmul_nt("out_odd_bwd", dx3_16, wf["w_out_odd"], tm=TM, tn=1024, out_dtype=_MXU)
    send("out_odd", "w_out_odd", _matmul_tn("out_odd_dw", o, dx3_16, tk=1024, tn=1024, tt=1024, out_dtype=_WIRE))
    dqxt, dkx, dvx = _flash_bwd(qx, kx, vx, o, do, lse, tq=2048, tk=512)
    dproj1, dqg1, dkg1 = _post_odd(proj1, dqxt, dkx, dvx, cos_a, sin_a, tied(p["ax_q_norm"]), p["ax_k_norm"], tm=TM)
    send("in_odd", "w_in_odd", _matmul_tn("in_odd_dw", h1, dproj1, tk=1024, tn=768, tt=1024, out_dtype=_WIRE))
    dx2, dx2_16, dnm1 = _matmul_nt_normbwd("in_odd_bwd", dproj1, wf["w_in_odd"], x2, tied(nm[1:2]), dx3, tm=TM_WIDE)
    dx1, dx1_16, dnl0 = mlp_bwd(0, mlp0, nl[0:1], wf["w_mlp_up"][0], wf["w_mlp_down"][0], dx2, dx2_16)
    dycat = _matmul_nt("out_even_bwd", dx1_16, wf["w_out_even"], tm=TM, tn=1024, out_dtype=F32)
    send("out_even", "w_out_even", jnp.concatenate([
        _matmul_tn("out_even_dw_ret", ya, dx1_16, tk=1024, tn=1024, tt=1024, out_dtype=_WIRE),
        _matmul_tn("out_even_dw_swa", yb, dx1_16, tk=1024, tn=1024, tt=1024, out_dtype=_WIRE)], axis=0))
    g_out, dga, dretg = _ret_gate_bwd(dycat, proj0, ret_o, tied(p["ret_norm"]), tm=TM)
    rb, rf = _ret_scan("ret_scan_bwd", qr, g_out, 0, rw["b"], rw["dec_b"], rw["a"], rw["dec_f"])
    dqr, dkr, dva, dlog = _ret_bwd(qr, kr, proj0, g_out, sf, sb, rf, rb, tabs)
    dqn, dkn, dvb, dbias, dsink = _swa_bwd(qn, kn, proj0, dycat, bias, sink)
    dt5 = _t5_bucket_reduce(dbias, _t5_bucket(_swa_rel()).astype(jnp.int32))
    dproj0, dqg0, dkg0 = _post_even(proj0, dqr, dkr, dva, dga, dqn, dkn, dvb, cos_r, sin_r,
                                    p["swa_q_norm"], p["swa_k_norm"], tm=TM_WIDE)
    send("in_even", "w_in_even", _matmul_tn("in_even_dw", h0, dproj0, tk=1024, tn=1152, tt=1024, out_dtype=_WIRE))
    dx0, _, dnm0 = _matmul_nt_normbwd("in_even_bwd", dproj0, w_first["w_in_even"], x, tied(nm[0:1]), dx1, tm=TM_WIDE)

    fold = lambda part: jnp.sum(part, axis=0)
    dlam = jnp.sum(dlog, axis=1).reshape(RET_HEADS, 2).T
    small = {
        "norm_mix": jnp.stack([fold(dnm0), fold(dnm1)]),
        "norm_mlp": jnp.stack([fold(dnl0), fold(dnl1)]),
        "ret_decay_logit": (dlam * (1.0 - jnp.exp(log_gamma)))[None],
        "ret_norm": fold(dretg)[None],
        "swa_q_norm": fold(dqg0)[None], "swa_k_norm": fold(dkg0)[None],
        "swa_sink": dsink[:, 0][None],
        "t5_table": dt5[:, :T5_BUCKETS].T,
        "ax_q_norm": fold(dqg1)[None], "ax_k_norm": fold(dkg1)[None],
    }
    return loss_part, dx0, small


BIG = ("w_in_even", "w_out_even", "w_in_odd", "w_out_odd", "w_mlp_up", "w_mlp_down")
SMALL = ("norm_mix", "norm_mlp", "ret_decay_logit", "ret_norm", "swa_q_norm", "swa_k_norm", "swa_sink", "t5_table",
         "ax_q_norm", "ax_k_norm")
WEIGHTS = ("norm_mix", "norm_mlp", "w_in_even", "w_out_even", "ret_decay_logit", "ret_norm", "swa_q_norm", "swa_k_norm",
           "swa_sink", "t5_table", "w_in_odd", "w_out_odd", "ax_q_norm", "ax_k_norm", "w_mlp_up", "w_mlp_down")
SHARD_AXIS = {"w_in_even": 2, "w_out_even": 1, "w_in_odd": 2, "w_out_odd": 1, "w_mlp_up": 2, "w_mlp_down": 1}
N_CHIPS = 4
ANY = pl.BlockSpec(memory_space=pl.ANY)
HBM = pl.BlockSpec(memory_space=pltpu.HBM)
SEM = pl.BlockSpec(memory_space=pltpu.SEMAPHORE)
SPLIT_COPY = pltpu.CompilerParams(has_side_effects=pltpu.SideEffectType.DATAFLOW_SIDE_EFFECTING)


def _in_hbm(a):
    return pltpu.with_memory_space_constraint(a, pltpu.HBM)


def _mesh_pos():
    return lax.axis_index("x"), lax.axis_index("y"), lax.axis_index("c")


def _window(ref, axis, start, size):
    idx = [slice(None)] * len(ref.shape)
    idx[axis] = pl.ds(start, size)
    return ref.at[tuple(idx)]


def _cast_place(key, shard, chip, *, tr=256):
    L, R, C = shard.shape
    tr = min(tr, R)
    axis = SHARD_AXIS[key]
    whole = tuple(d * (N_CHIPS if a == axis else 1) for a, d in enumerate(shard.shape))

    def body(chip_ref, s_ref, o_ref):
        o_ref[...] = s_ref[...].astype(o_ref.dtype)

    if axis == 2:
        out_map = lambda l, i, chip_ref: (l, i, chip_ref[0])
    else:
        out_map = lambda l, i, chip_ref: (l, i + chip_ref[0] * (R // tr), 0)
    grid_spec = pltpu.PrefetchScalarGridSpec(
        num_scalar_prefetch=1, grid=(L, R // tr),
        in_specs=[pl.BlockSpec((1, tr, C), lambda l, i, chip_ref: (l, i, 0))],
        out_specs=pl.BlockSpec((1, tr, C), out_map))
    return pl.pallas_call(body, name=f"cast_place_{key}", grid_spec=grid_spec, out_shape=_sds(whole, _MXU),
                          compiler_params=_params("parallel", "parallel"))(chip, shard)


def _gather_copies(names, refs, send_sems, recv_sems, *, outgoing=True, incoming=True):
    x, y, c = _mesh_pos()
    chips = [(1 - x, y), (x, 1 - y), (1 - x, 1 - y)]
    out, inc = [], []
    for t, key in enumerate(names):
        size = refs[t].shape[SHARD_AXIS[key]] // N_CHIPS
        slot = lambda px, py: _window(refs[t], SHARD_AXIS[key], pl.multiple_of((2 * px + py) * size, 128), size)
        for k, (px, py) in enumerate(chips):
            sems = dict(send_sem=send_sems.at[3 * t + k], recv_sem=recv_sems.at[3 * t + k], device_id=(px, py, c), device_id_type=MESH)
            if outgoing:
                out.append(pltpu.make_async_remote_copy(slot(x, y), slot(x, y), **sems))
            if incoming:
                inc.append(pltpu.make_async_remote_copy(slot(x, y), slot(px, py), **sems))
    return out, inc


def _allgather_start(groups):
    names = [list(g) for g in groups]
    flat = [g[k] for g in groups for k in g]
    n, ng = len(flat), len(groups)

    def body(*refs):
        start = 0
        for gi, keys in enumerate(names):
            copies, _ = _gather_copies(keys, refs[start:start + len(keys)], refs[n + 2 * gi], refs[n + 2 * gi + 1], incoming=False)
            for cp in copies:
                cp.start()
            start += len(keys)
        token = refs[-1]
        token[...] = jnp.zeros_like(token)

    sem_shapes = [pltpu.SemaphoreType.DMA((3 * len(keys),)) for keys in names for _ in (0, 1)]
    outs = pl.pallas_call(
        body, name="allgather_start", in_specs=[HBM] * n,
        out_specs=[SEM] * (2 * ng) + [HBM] * n + [pl.BlockSpec(memory_space=pltpu.VMEM)],
        out_shape=sem_shapes + [pltpu.HBM(a.shape, a.dtype) for a in flat] + [_sds((8, HEAD_DIM), F32)],
        input_output_aliases={t: 2 * ng + t for t in range(n)},
        compiler_params=SPLIT_COPY,
    )(*[_in_hbm(a) for a in flat])
    states, start = [], 2 * ng
    for gi, keys in enumerate(names):
        states.append((gi, keys, outs[2 * gi], outs[2 * gi + 1], outs[start:start + len(keys)]))
        start += len(keys)
    return states, outs[-1]


def _allgather_wait(state, after):
    gi, names, send_sems, recv_sems, thru = state
    n = len(names)

    def body(*refs):
        outgoing, incoming = _gather_copies(names, refs[:n], refs[n], refs[n + 1])
        for cp in outgoing:
            cp.wait_send()
        for cp in incoming:
            cp.wait_recv()

    outs = pl.pallas_call(
        body, name=f"allgather_wait_{gi}", in_specs=[HBM] * n + [SEM, SEM, ANY], out_specs=[HBM] * n,
        out_shape=[pltpu.HBM(t.shape, t.dtype) for t in thru],
        input_output_aliases={t: t for t in range(n)},
        compiler_params=SPLIT_COPY,
    )(*thru, send_sems, recv_sems, after)
    return dict(zip(names, outs))


FLIPS = [(a, b, d) for a in (0, 1) for b in (0, 1) for d in (0, 1) if (a, b, d) != (0, 0, 0)]


def _flip(pos, f):
    return tuple(1 - p if fi else p for p, fi in zip(pos, f))


def _piece_shape(weight, shape):
    out = list(shape)
    out[SHARD_AXIS[weight]] //= N_CHIPS
    out[1] //= 2
    return tuple(out)


def _piece(ref, weight, chip, core):
    piece = _piece_shape(weight, ref.shape)
    if SHARD_AXIS[weight] == 1:
        return _window(ref, 1, pl.multiple_of((2 * chip + core) * piece[1], 8), piece[1])
    return _window(_window(ref, 2, pl.multiple_of(chip * piece[2], 128), piece[2]), 1, pl.multiple_of(core * piece[1], 8), piece[1])


def _scatter_copies(weight, grad_ref, land_ref, send_sems, recv_sems, *, outgoing=True, incoming=True):
    pos = _mesh_pos()
    out, inc = [], []
    for k, f in enumerate(FLIPS):
        peer = _flip(pos, f)
        sems = dict(send_sem=send_sems.at[k], recv_sem=recv_sems.at[k], device_id=peer, device_id_type=MESH)
        if outgoing:
            out.append(pltpu.make_async_remote_copy(_piece(grad_ref, weight, 2 * peer[0] + peer[1], peer[2]), land_ref.at[k], **sems))
        if incoming:
            inc.append(pltpu.make_async_remote_copy(_piece(grad_ref, weight, 2 * pos[0] + pos[1], pos[2]), land_ref.at[k], **sems))
    return out, inc


def _scatter_start(tag, weight, grad):
    n_peer = len(FLIPS)
    land = lax.empty((n_peer,) + _piece_shape(weight, grad.shape), grad.dtype)

    def body(grad_ref, land_ref, send_sems, recv_sems, grad_thru, land_thru, token):
        copies, _ = _scatter_copies(weight, grad_ref, land_ref, send_sems, recv_sems, incoming=False)
        for cp in copies:
            cp.start()
        token[...] = jnp.zeros_like(token)

    outs = pl.pallas_call(
        body, name=f"scatter_start_{tag}", in_specs=[HBM, HBM],
        out_specs=[SEM, SEM, HBM, HBM, pl.BlockSpec(memory_space=pltpu.VMEM)],
        out_shape=[pltpu.SemaphoreType.DMA((n_peer,)), pltpu.SemaphoreType.DMA((n_peer,)),
                   pltpu.HBM(grad.shape, grad.dtype), pltpu.HBM(land.shape, land.dtype), _sds((8, HEAD_DIM), F32)],
        input_output_aliases={0: 2, 1: 3},
        compiler_params=SPLIT_COPY,
    )(_in_hbm(grad), _in_hbm(land))
    return (tag, weight, outs[:4]), outs[4]


def _scatter_wait(state, after):
    tag, weight, (send_sems, recv_sems, grad_thru, land_thru) = state

    def body(grad_ref, land_ref, send_ref, recv_ref, after_ref, grad_out, land_out):
        outgoing, incoming = _scatter_copies(weight, grad_ref, land_ref, send_ref, recv_ref)
        for cp in outgoing:
            cp.wait_send()
        for cp in incoming:
            cp.wait_recv()

    return pl.pallas_call(
        body, name=f"scatter_wait_{tag}", in_specs=[HBM, HBM, SEM, SEM, ANY], out_specs=[HBM, HBM],
        out_shape=[pltpu.HBM(grad_thru.shape, grad_thru.dtype), pltpu.HBM(land_thru.shape, land_thru.dtype)],
        input_output_aliases={0: 0, 1: 1},
        compiler_params=SPLIT_COPY,
    )(grad_thru, land_thru, send_sems, recv_sems, after)


def _sum_pieces(tag, weight, grad, land, where, *, tr=128):
    _, R, C = _piece_shape(weight, grad.shape)
    tr = min(tr, R)
    nr = R // tr

    def body(where_ref, g_ref, l_ref, o_ref):
        acc = g_ref[...].astype(F32)
        for s in range(len(FLIPS)):
            acc = acc + l_ref[s].astype(F32)
        o_ref[...] = acc

    if SHARD_AXIS[weight] == 1:
        own = lambda i, where_ref: (0, (2 * where_ref[0] + where_ref[1]) * nr + i, 0)
    else:
        own = lambda i, where_ref: (0, where_ref[1] * nr + i, where_ref[0])
    grid_spec = pltpu.PrefetchScalarGridSpec(
        num_scalar_prefetch=1, grid=(nr,),
        in_specs=[pl.BlockSpec((1, tr, C), own), pl.BlockSpec((len(FLIPS), 1, tr, C), lambda i, where_ref: (0, 0, i, 0))],
        out_specs=pl.BlockSpec((1, tr, C), lambda i, where_ref: (0, where_ref[1] * nr + i, 0)))
    return pl.pallas_call(body, name=f"sum_{tag}", grid_spec=grid_spec, out_shape=_sds((1, 2 * R, C), F32),
                          compiler_params=_params("parallel"))(where, grad, land)


def _exchange_halves(shards):
    names = list(shards)
    n = len(names)
    half_sizes = [shards[k].shape[1] // 2 for k in names]

    def body(*refs):
        outs = refs[n:2 * n]
        send_sems, recv_sems = refs[2 * n:]
        x, y, c = _mesh_pos()
        half = lambda t, core: _window(outs[t], 1, pl.multiple_of(core * half_sizes[t], 8), half_sizes[t])
        sends = []
        for t in range(n):
            sends.append(pltpu.make_async_remote_copy(half(t, c), half(t, c), send_sems.at[t], recv_sems.at[t],
                                                      device_id=(x, y, 1 - c), device_id_type=MESH))
            sends[-1].start()
        for t in range(n):
            pltpu.make_async_remote_copy(half(t, c), half(t, 1 - c), send_sems.at[t], recv_sems.at[t],
                                         device_id=(x, y, 1 - c), device_id_type=MESH).wait_recv()
        for cp in sends:
            cp.wait_send()

    outs = pl.pallas_call(
        body, name="exchange_halves", in_specs=[ANY] * n, out_specs=[ANY] * n,
        out_shape=[_sds(shards[k].shape, F32) for k in names],
        input_output_aliases={t: t for t in range(n)},
        scratch_shapes=[pltpu.SemaphoreType.DMA((n,)), pltpu.SemaphoreType.DMA((n,))],
    )(*[shards[k] for k in names])
    return dict(zip(names, outs))


def _adamw_math(w, g, m, v):
    m = ADAM_B1 * m + (1.0 - ADAM_B1) * g
    v = ADAM_B2 * v + (1.0 - ADAM_B2) * jnp.square(g)
    m_hat = m / (1.0 - ADAM_B1 ** ADAM_STEP)
    v_hat = v / (1.0 - ADAM_B2 ** ADAM_STEP)
    return -ADAM_LR * (m_hat / (jnp.sqrt(v_hat) + ADAM_EPS) + ADAM_WD * w), m, v


def _adamw(name, w, g, m, v, *, tr=256):
    R, C = w.shape
    tr = min(tr, R)

    def body(w_ref, g_ref, m_ref, v_ref, d_ref, mo_ref, vo_ref):
        d_ref[...], mo_ref[...], vo_ref[...] = _adamw_math(w_ref[...], g_ref[...], m_ref[...], v_ref[...])

    spec = pl.BlockSpec((tr, C), lambda i: (i, 0))
    return pl.pallas_call(body, name=name, grid=(R // tr,), in_specs=[spec] * 4, out_specs=[spec] * 3,
                          out_shape=[_sds((R, C), F32)] * 3, compiler_params=_params("parallel"))(w, g, m, v)


SLAB_ROWS = 8
LOSS_ROW = 7


def _pack_small(d):
    pad = lambda a, width: jnp.pad(a.reshape(-1), (0, width - a.size))
    row5 = jnp.concatenate([d["swa_q_norm"].reshape(-1), d["swa_k_norm"].reshape(-1), d["ax_q_norm"].reshape(-1),
                            d["ax_k_norm"].reshape(-1), pad(d["swa_sink"], HEAD_DIM), pad(d["ret_decay_logit"], HEAD_DIM),
                            jnp.zeros((2 * HEAD_DIM,), F32)])
    return jnp.concatenate([d["norm_mix"], d["norm_mlp"], d["ret_norm"], row5[None], pad(d["t5_table"], D_MODEL)[None],
                            jnp.zeros((1, D_MODEL), F32)], axis=0)


def _unpack_small(slab):
    r5 = slab[5]
    return {
        "norm_mix": slab[0:2], "norm_mlp": slab[2:4], "ret_norm": slab[4:5],
        "swa_q_norm": r5[None, 0:128], "swa_k_norm": r5[None, 128:256], "ax_q_norm": r5[None, 256:384],
        "ax_k_norm": r5[None, 384:512], "swa_sink": r5[None, 512:512 + SWA_HEADS],
        "ret_decay_logit": r5[640:640 + 2 * RET_HEADS].reshape(1, 2, RET_HEADS),
        "t5_table": slab[6, :T5_BUCKETS * SWA_HEADS].reshape(T5_BUCKETS, SWA_HEADS),
    }


def _small_allreduce_adamw(g_slab, w_slab, m_slab, v_slab, loss_part):
    def body(g_ref, w_ref, m_ref, v_ref, lp_ref, go_ref, d_ref, mo_ref, vo_ref, gath, send_sems, recv_sems):
        pos = _mesh_pos()
        ident = lambda p: 4 * p[0] + 2 * p[1] + p[2]
        me = ident(pos)
        row = lax.broadcasted_iota(jnp.int32, (SLAB_ROWS, D_MODEL), 0)
        lane = lax.broadcasted_iota(jnp.int32, (SLAB_ROWS, D_MODEL), 1)
        loss = jnp.sum(jnp.sum(lp_ref[...], axis=0, keepdims=True), axis=1, keepdims=True) * (0.5 / D_MODEL)
        gath[me] = jnp.where(jnp.logical_and(row == LOSS_ROW, lane == 0), loss, g_ref[...])
        sends = []
        for k, f in enumerate(FLIPS):
            sends.append(pltpu.make_async_remote_copy(gath.at[me], gath.at[me], send_sems.at[k], recv_sems.at[k],
                                                      device_id=_flip(pos, f), device_id_type=MESH))
            sends[-1].start()
        for k, f in enumerate(FLIPS):
            peer = _flip(pos, f)
            pltpu.make_async_remote_copy(gath.at[me], gath.at[ident(peer)], send_sems.at[k], recv_sems.at[k],
                                         device_id=peer, device_id_type=MESH).wait_recv()
        for cp in sends:
            cp.wait_send()
        total = gath[0]
        for s in range(1, N_DEV):
            total = total + gath[s]
        go_ref[...] = total
        d_ref[...], mo_ref[...], vo_ref[...] = _adamw_math(w_ref[...], total, m_ref[...], v_ref[...])

    vmem = pl.BlockSpec(memory_space=pltpu.VMEM)
    return pl.pallas_call(
        body, name="small_allreduce_adamw", in_specs=[vmem] * 5, out_specs=[vmem] * 4,
        out_shape=[_sds((SLAB_ROWS, D_MODEL), F32)] * 4,
        scratch_shapes=[pltpu.VMEM((N_DEV, SLAB_ROWS, D_MODEL), F32),
                        pltpu.SemaphoreType.DMA((len(FLIPS),)), pltpu.SemaphoreType.DMA((len(FLIPS),))],
    )(g_slab, w_slab, m_slab, v_slab, loss_part)


def kernel(x, norm_mix, norm_mlp, w_in_even, w_out_even, ret_decay_logit, ret_norm, swa_q_norm, swa_k_norm, swa_sink, t5_table, w_in_odd, w_out_odd, ax_q_norm, ax_k_norm, w_mlp_up, w_mlp_down, loss_target, m_norm_mix, m_norm_mlp, m_w_in_even, m_w_out_even, m_ret_decay_logit, m_ret_norm, m_swa_q_norm, m_swa_k_norm, m_swa_sink, m_t5_table, m_w_in_odd, m_w_out_odd, m_ax_q_norm, m_ax_k_norm, m_w_mlp_up, m_w_mlp_down, v_norm_mix, v_norm_mlp, v_w_in_even, v_w_out_even, v_ret_decay_logit, v_ret_norm, v_swa_q_norm, v_swa_k_norm, v_swa_sink, v_t5_table, v_w_in_odd, v_w_out_odd, v_ax_q_norm, v_ax_k_norm, v_w_mlp_up, v_w_mlp_down):
    w = dict(zip(WEIGHTS, (norm_mix, norm_mlp, w_in_even, w_out_even, ret_decay_logit, ret_norm, swa_q_norm, swa_k_norm,
                           swa_sink, t5_table, w_in_odd, w_out_odd, ax_q_norm, ax_k_norm, w_mlp_up, w_mlp_down)))
    m = dict(zip(WEIGHTS, (m_norm_mix, m_norm_mlp, m_w_in_even, m_w_out_even, m_ret_decay_logit, m_ret_norm, m_swa_q_norm,
                           m_swa_k_norm, m_swa_sink, m_t5_table, m_w_in_odd, m_w_out_odd, m_ax_q_norm, m_ax_k_norm,
                           m_w_mlp_up, m_w_mlp_down)))
    v = dict(zip(WEIGHTS, (v_norm_mix, v_norm_mlp, v_w_in_even, v_w_out_even, v_ret_decay_logit, v_ret_norm, v_swa_q_norm,
                           v_swa_k_norm, v_swa_sink, v_t5_table, v_w_in_odd, v_w_out_odd, v_ax_q_norm, v_ax_k_norm,
                           v_w_mlp_up, v_w_mlp_down)))
    flat = lambda a: a.reshape(-1, a.shape[-1])

    chip = (2 * lax.axis_index("x") + lax.axis_index("y")).astype(jnp.int32)
    where = jnp.stack([chip, lax.axis_index("c").astype(jnp.int32)])

    placed = {k: _cast_place(k, w[k], where[0:1]) for k in BIG}
    (gather_first, gather_rest), gather_token = _allgather_start(
        [{"w_in_even": placed["w_in_even"]}, {k: placed[k] for k in BIG if k != "w_in_even"}])
    unstack = lambda whole: {k: (a if k.startswith("w_mlp") else a[0]) for k, a in whole.items()}

    in_flight = []

    def push(tag, weight, dw):
        state, token = _scatter_start(tag, weight, dw)
        in_flight.append(state)
        return token

    loss_part, dx, small_g = _local_step(x[0], loss_target[0], {k: w[k] for k in SMALL},
                                         unstack(_allgather_wait(gather_first, gather_token)),
                                         lambda after: unstack(_allgather_wait(gather_rest, after)), push)

    halves = {}
    for state in in_flight:
        tag, weight = state[0], state[1]
        dw, land = _scatter_wait(state, dx)
        halves[tag] = _sum_pieces(tag, weight, dw, land, where)
    reduced = _exchange_halves(halves)
    grad = {"w_in_even": reduced["in_even"], "w_out_even": reduced["out_even"],
            "w_in_odd": reduced["in_odd"], "w_out_odd": reduced["out_odd"],
            "w_mlp_up": jnp.concatenate([reduced["mlp_up0"], reduced["mlp_up1"]], axis=0),
            "w_mlp_down": jnp.concatenate([reduced["mlp_down0"], reduced["mlp_down1"]], axis=0)}
    delta, new_m, new_v = {}, {}, {}
    for k in BIG:
        d_k, m_k, v_k = _adamw(f"adamw_{k}", flat(w[k]), flat(grad[k]), flat(m[k]), flat(v[k]))
        delta[k], new_m[k], new_v[k] = d_k.reshape(w[k].shape), m_k.reshape(w[k].shape), v_k.reshape(w[k].shape)

    slabs = _small_allreduce_adamw(_pack_small(small_g), _pack_small({k: w[k] for k in SMALL}),
                                   _pack_small({k: m[k] for k in SMALL}), _pack_small({k: v[k] for k in SMALL}), loss_part)
    loss = slabs[0][LOSS_ROW, 0]
    for out, slab in zip((grad, delta, new_m, new_v), slabs):
        out.update(_unpack_small(slab))

    return (loss, dx[None], *[grad[k] for k in WEIGHTS], *[delta[k] for k in WEIGHTS],
            *[new_m[k] for k in WEIGHTS], *[new_v[k] for k in WEIGHTS])
```

```python
import functools
import math

import jax
import jax.numpy as jnp
from jax import lax
from jax.experimental import pallas as pl
from jax.experimental.pallas import tpu as pltpu

F32 = jnp.float32
BF16 = jnp.bfloat16
_MXU = BF16
_WIRE = BF16

D_MODEL = 1024
HEAD_DIM = 128
EPS = 1e-6
NEG_INF = -1e30
CHUNK = 128
RET_CHUNKS_PER_STEP = 4
GRID_W = 64
RET_HEADS, RET_DK, RET_DV = 4, 128, 256
RET_Q, RET_V = RET_HEADS * RET_DK, RET_HEADS * RET_DV
RET_THETA = 10000.0
SWA_HEADS, SWA_KV_HEADS = 8, 2
T5_BUCKETS, T5_MAX_DIST = 32, 128
AX_HEADS, AX_KV_HEADS = 8, 2
AX_THETA = 10000.0
D_FF = 4 * D_MODEL
EVEN_IN = 2 * RET_Q + 2 * RET_V + D_MODEL + 2 * SWA_KV_HEADS * HEAD_DIM
ODD_IN = D_MODEL + 2 * AX_KV_HEADS * HEAD_DIM
ATT_SCALE = HEAD_DIM ** -0.5
SCORE_SCALE_LOG2 = ATT_SCALE * math.log2(math.e)

ADAM_LR, ADAM_B1, ADAM_B2, ADAM_EPS, ADAM_WD, ADAM_STEP = 0.001, 0.9, 0.999, 1e-08, 0.01, 10

N_DEV = 8
VMEM_LIMIT_BYTES = 56 << 20
MESH = pl.DeviceIdType.MESH

_NN = (((1,), (0,)), ((), ()))
_NT = (((1,), (1,)), ((), ()))
_TN = (((0,), (0,)), ((), ()))


def _dot(a, b, dn=_NN):
    return lax.dot_general(a.astype(_MXU), b.astype(_MXU), dn, preferred_element_type=F32)


def _params(*sem):
    return pltpu.CompilerParams(dimension_semantics=sem, vmem_limit_bytes=VMEM_LIMIT_BYTES)


def _sds(shape, dtype):
    return jax.ShapeDtypeStruct(tuple(shape), dtype)


def _rowsum8(x):
    return jnp.sum(x.reshape(x.shape[0] // 8, 8, x.shape[1]), axis=0)


def _swap_halves(x, half):
    width = x.shape[1]
    lane = lax.broadcasted_iota(jnp.int32, x.shape, 1)
    up = pltpu.roll(x, width - half, axis=1)
    down = pltpu.roll(x, half, axis=1)
    return jnp.where((lane & (2 * half - 1)) < half, up, down)


def _sigmoid(x):
    return 1.0 / (1.0 + jnp.exp(-x))


def _norm_matmul(name, x, gain, w, *, tm, tn, out_dtype):
    T, K = x.shape
    N = w.shape[1]
    tm, tn = min(tm, T), min(tn, N)

    def body(x_ref, g_ref, w_ref, y_ref, h_ref, h_sc):
        @pl.when(pl.program_id(1) == 0)
        def _():
            xv = x_ref[...]
            r = lax.rsqrt(jnp.mean(xv * xv, axis=-1, keepdims=True) + EPS)
            h = (xv * r * g_ref[...]).astype(_MXU)
            h_sc[...] = h
            h_ref[...] = h
        y_ref[...] = jnp.dot(h_sc[...], w_ref[...], preferred_element_type=F32).astype(y_ref.dtype)

    return pl.pallas_call(
        body, name=name, grid=(T // tm, N // tn),
        in_specs=[pl.BlockSpec((tm, K), lambda i, j: (i, 0)),
                  pl.BlockSpec((1, K), lambda i, j: (0, 0)),
                  pl.BlockSpec((K, tn), lambda i, j: (0, j))],
        out_specs=[pl.BlockSpec((tm, tn), lambda i, j: (i, j)),
                   pl.BlockSpec((tm, K), lambda i, j: (i, 0))],
        out_shape=[_sds((T, N), out_dtype), _sds((T, K), _MXU)],
        scratch_shapes=[pltpu.VMEM((tm, K), _MXU)],
        compiler_params=_params("parallel", "arbitrary"),
    )(x, gain, w)


def _matmul_res(name, a_list, w, res, *, tm, relu2=False, target=None):
    T = res.shape[0]
    N = w.shape[1]
    K = a_list[0].shape[1]
    n_a = len(a_list)
    tm = min(tm, T)
    with_loss = target is not None

    def body(*refs):
        a_refs = refs[:n_a]
        w_refs = refs[n_a:2 * n_a]
        res_ref = refs[2 * n_a]
        acc = res_ref[...]
        for a_ref, w_ref in zip(a_refs, w_refs):
            a = a_ref[...]
            if relu2:
                a = jnp.square(jnp.maximum(a.astype(F32), 0.0))
            acc = acc + _dot(a, w_ref[...])
        if with_loss:
            tgt_ref, g_ref, g16_ref, loss_ref = refs[2 * n_a + 1:]
            diff = acc - tgt_ref[...]
            g = diff * (1.0 / N)
            g_ref[...] = g
            g16_ref[...] = g.astype(g16_ref.dtype)

            @pl.when(pl.program_id(0) == 0)
            def _():
                loss_ref[...] = jnp.zeros_like(loss_ref)
            loss_ref[...] += _rowsum8(diff * diff)
        else:
            refs[2 * n_a + 1][...] = acc

    row = lambda i: (i, 0)
    in_specs = [pl.BlockSpec((tm, K), row) for _ in a_list]
    in_specs += [pl.BlockSpec((K, N), functools.partial(lambda i, b: (b, 0), b=b)) for b in range(n_a)]
    in_specs += [pl.BlockSpec((tm, N), row)]
    args = list(a_list) + [w] * n_a + [res]
    if with_loss:
        in_specs.append(pl.BlockSpec((tm, N), row))
        args.append(target)
        out_specs = [pl.BlockSpec((tm, N), row), pl.BlockSpec((tm, N), row), pl.BlockSpec((8, N), lambda i: (0, 0))]
        out_shape = [_sds((T, N), F32), _sds((T, N), _MXU), _sds((8, N), F32)]
        sem = "arbitrary"
    else:
        out_specs = pl.BlockSpec((tm, N), row)
        out_shape = _sds((T, N), F32)
        sem = "parallel"
    return pl.pallas_call(body, name=name, grid=(T // tm,), in_specs=in_specs, out_specs=out_specs,
                          out_shape=out_shape, compiler_params=_params(sem))(*args)


def _matmul_nt(name, a, w, *, tm, tn, out_dtype, relu_of=None):
    T, K = a.shape
    N = w.shape[0]
    tm, tn = min(tm, T), min(tn, N)

    def body(*refs):
        if relu_of is None:
            a_ref, w_ref, o_ref = refs
            o_ref[...] = _dot(a_ref[...], w_ref[...], _NT).astype(o_ref.dtype)
        else:
            a_ref, w_ref, u_ref, o_ref = refs
            da = _dot(a_ref[...], w_ref[...], _NT)
            o_ref[...] = (da * (2.0 * jnp.maximum(u_ref[...].astype(F32), 0.0))).astype(o_ref.dtype)

    in_specs = [pl.BlockSpec((tm, K), lambda i, j: (i, 0)), pl.BlockSpec((tn, K), lambda i, j: (j, 0))]
    args = [a, w]
    if relu_of is not None:
        in_specs.append(pl.BlockSpec((tm, tn), lambda i, j: (i, j)))
        args.append(relu_of)
    return pl.pallas_call(body, name=name, grid=(T // tm, N // tn), in_specs=in_specs,
                          out_specs=pl.BlockSpec((tm, tn), lambda i, j: (i, j)),
                          out_shape=_sds((T, N), out_dtype),
                          compiler_params=_params("parallel", "parallel"))(*args)


def _matmul_nt_normbwd(name, dy, w, x, gain, dres, *, tm):
    T, K = dy.shape
    N = w.shape[0]
    tm = min(tm, T)

    def body(dy_ref, w_ref, x_ref, g_ref, dres_ref, dx_ref, dx16_ref, dg_ref):
        dh = _dot(dy_ref[...], w_ref[...], _NT)
        xv = x_ref[...]
        r = lax.rsqrt(jnp.mean(xv * xv, axis=-1, keepdims=True) + EPS)
        xhat = xv * r
        dxhat = dh * g_ref[...]
        dx = dres_ref[...] + r * (dxhat - xhat * jnp.mean(dxhat * xhat, axis=-1, keepdims=True))
        dx_ref[...] = dx
        dx16_ref[...] = dx.astype(dx16_ref.dtype)

        @pl.when(pl.program_id(0) == 0)
        def _():
            dg_ref[...] = jnp.zeros_like(dg_ref)
        dg_ref[...] += _rowsum8(dh * xhat)

    row = lambda i: (i, 0)
    return pl.pallas_call(
        body, name=name, grid=(T // tm,),
        in_specs=[pl.BlockSpec((tm, K), row), pl.BlockSpec((N, K), lambda i: (0, 0)),
                  pl.BlockSpec((tm, N), row), pl.BlockSpec((1, N), lambda i: (0, 0)), pl.BlockSpec((tm, N), row)],
        out_specs=[pl.BlockSpec((tm, N), row), pl.BlockSpec((tm, N), row), pl.BlockSpec((8, N), lambda i: (0, 0))],
        out_shape=[_sds((T, N), F32), _sds((T, N), _MXU), _sds((8, N), F32)],
        compiler_params=_params("arbitrary"),
    )(dy, w, x, gain, dres)


def _matmul_tn(name, a, b, *, tk, tn, tt, out_dtype, relu2=False):
    T, Ka = a.shape
    Nb = b.shape[1]
    tk, tn, tt = min(tk, Ka), min(tn, Nb), min(tt, T)
    nt = T // tt

    def body(a_ref, b_ref, o_ref, acc):
        t = pl.program_id(2)

        @pl.when(t == 0)
        def _():
            acc[...] = jnp.zeros_like(acc)
        av = a_ref[...]
        if relu2:
            av = jnp.square(jnp.maximum(av.astype(F32), 0.0))
        acc[...] += _dot(av, b_ref[...], _TN)

        @pl.when(t == nt - 1)
        def _():
            o_ref[...] = acc[...].astype(o_ref.dtype)

    return pl.pallas_call(
        body, name=name, grid=(Ka // tk, Nb // tn, nt),
        in_specs=[pl.BlockSpec((tt, tk), lambda i, j, t: (t, i)), pl.BlockSpec((tt, tn), lambda i, j, t: (t, j))],
        out_specs=pl.BlockSpec((tk, tn), lambda i, j, t: (i, j)),
        out_shape=_sds((Ka, Nb), out_dtype),
        scratch_shapes=[pltpu.VMEM((tk, tn), F32)],
        compiler_params=_params("parallel", "parallel", "arbitrary"),
    )(a, b)


def _rope_angles(pos, dim, theta):
    inv = theta ** (-jnp.arange(0, dim, 2, dtype=F32) / dim)
    return pos.astype(F32)[:, None] * inv[None, :]


def _ret_rope_tables(T):
    ang = _rope_angles(jnp.arange(T), RET_DK, RET_THETA)
    c, s = jnp.cos(ang), jnp.sin(ang)
    return jnp.concatenate([c, c], axis=1), jnp.concatenate([-s, s], axis=1)


def _axial_rope_tables(T):
    rows = T // GRID_W
    ar = _rope_angles(jnp.arange(rows), HEAD_DIM // 2, AX_THETA)
    ac = _rope_angles(jnp.arange(GRID_W), HEAD_DIM // 2, AX_THETA)
    by_row = lambda a: jnp.repeat(a, GRID_W, axis=0)
    by_col = lambda a: jnp.tile(a, (rows, 1))
    cos = jnp.concatenate([by_row(jnp.cos(ar)), by_row(jnp.cos(ar)), by_col(jnp.cos(ac)), by_col(jnp.cos(ac))], axis=1)
    sin = jnp.concatenate([by_row(-jnp.sin(ar)), by_row(jnp.sin(ar)), by_col(-jnp.sin(ac)), by_col(jnp.sin(ac))], axis=1)
    return cos, sin


(TAB_D, TAB_DT, TAB_EF, TAB_EB, TAB_A, TAB_B, TAB_CF, TAB_CB,
 TAB_RA, TAB_RB, TAB_RCF, TAB_RCB, TAB_KF, TAB_KB) = range(14)


def _retention_tables(decay_logit):
    lg = jax.nn.log_sigmoid(decay_logit.astype(F32))
    lam, mu = lg[0][:, None, None], lg[1][:, None, None]
    idx = jnp.arange(CHUNK, dtype=F32)
    diff = (idx[:, None] - idx[None, :])[None]
    df = jnp.where(diff >= 0, jnp.exp(jnp.maximum(diff, 0.0) * lam), 0.0)
    db = jnp.where(diff < 0, jnp.exp(jnp.maximum(-diff, 0.0) * mu), 0.0)
    d = df + db
    r = idx[None, :, None]
    ones = jnp.ones((1, 1, CHUNK), F32)
    a = jnp.exp((r + 1.0) * lam) * ones
    b = jnp.exp((CHUNK - r) * mu) * ones
    cf = jnp.exp((CHUNK - 1.0 - r) * lam) * ones
    cb = jnp.exp(r * mu) * ones
    full = jnp.ones((1, CHUNK, CHUNK), F32)
    kf = CHUNK * jnp.exp(CHUNK * lam) * full
    kb = CHUNK * jnp.exp(CHUNK * mu) * full
    tabs = jnp.stack([d, jnp.swapaxes(d, 1, 2), diff * df, -diff * db, a, b, cf, cb,
                      (r + 1.0) * a, (CHUNK - r) * b, (CHUNK - 1.0 - r) * cf, r * cb, kf, kb], axis=1)

    def lanes(tab):
        return jnp.transpose(tab, (1, 0, 2)).reshape(CHUNK, RET_HEADS * CHUNK)

    def dec(l):
        return jnp.exp(CHUNK * l)[:, 0, :] * jnp.ones((1, RET_DV), F32)

    weights = dict(a=lanes(a), b=lanes(b), cf=lanes(cf), cb=lanes(cb), dec_f=dec(lam), dec_b=dec(mu))
    return tabs, weights, lg


def _t5_bucket(rel):
    nb = T5_BUCKETS // 2
    max_exact = nb // 2
    ret = jnp.where(rel > 0, nb, 0)
    n = jnp.abs(rel)
    nf = jnp.maximum(n, 1).astype(F32)
    large = max_exact + (jnp.log(nf / max_exact) / math.log(T5_MAX_DIST / max_exact)
                         * (nb - max_exact)).astype(jnp.int32)
    large = jnp.minimum(large, nb - 1)
    return ret + jnp.where(n < max_exact, n, large)


def _swa_rel():
    r = jnp.arange(CHUNK)
    j = jnp.arange(3 * CHUNK)
    return j[None, :] - CHUNK - r[:, None]


def _swa_bias(t5_table):
    rel = _swa_rel()
    bucket = jnp.where(jnp.abs(rel) <= CHUNK, _t5_bucket(rel), -1).astype(jnp.int32)

    def body(tab_ref, bk_ref, o_ref):
        bk = bk_ref[...]
        for h in range(SWA_HEADS):
            pick = lambda b, acc, h=h: jnp.where(bk == b, tab_ref[b, h], acc)
            o_ref[h] = lax.fori_loop(0, T5_BUCKETS, pick, jnp.full(bk.shape, NEG_INF, F32))

    return pl.pallas_call(
        body, name="t5_bias",
        in_specs=[pl.BlockSpec(memory_space=pltpu.SMEM), pl.BlockSpec(memory_space=pltpu.VMEM)],
        out_specs=pl.BlockSpec(memory_space=pltpu.VMEM),
        out_shape=_sds((SWA_HEADS, CHUNK, 3 * CHUNK), F32),
    )(t5_table.astype(F32), bucket)


def _prep_even(proj, cos, sin, q_gain, k_gain, *, tm):
    T = proj.shape[0]
    tm = min(tm, T)

    def body(qa_ref, ka_ref, qb_ref, kb_ref, c_ref, s_ref, qg_ref, kg_ref, qr_ref, kr_ref, qn_ref, kn_ref):
        c = jnp.concatenate([c_ref[...]] * RET_HEADS, axis=1)
        s = jnp.concatenate([s_ref[...]] * RET_HEADS, axis=1)
        qa = qa_ref[...]
        qr_ref[...] = (qa * c + _swap_halves(qa, RET_DK // 2) * s).astype(qr_ref.dtype)
        ka = ka_ref[...]
        kr_ref[...] = ((ka * c + _swap_halves(ka, RET_DK // 2) * s) * (RET_DK ** -0.5)).astype(kr_ref.dtype)
        for src, gain, dst, heads in ((qb_ref, qg_ref, qn_ref, SWA_HEADS), (kb_ref, kg_ref, kn_ref, SWA_KV_HEADS)):
            for h in range(heads):
                sl = slice(h * HEAD_DIM, (h + 1) * HEAD_DIM)
                xh = src[:, sl]
                r = lax.rsqrt(jnp.mean(xh * xh, axis=-1, keepdims=True) + EPS)
                dst[:, sl] = (xh * r * gain[...]).astype(dst.dtype)

    row = lambda i: (i, 0)
    const = lambda i: (0, 0)
    return pl.pallas_call(
        body, name="prep_even", grid=(T // tm,),
        in_specs=[pl.BlockSpec((tm, RET_Q), lambda i: (i, 0)), pl.BlockSpec((tm, RET_Q), lambda i: (i, 1)),
                  pl.BlockSpec((tm, D_MODEL), lambda i: (i, 3)), pl.BlockSpec((tm, 256), lambda i: (i, 16)),
                  pl.BlockSpec((tm, RET_DK), row), pl.BlockSpec((tm, RET_DK), row),
                  pl.BlockSpec((1, HEAD_DIM), const), pl.BlockSpec((1, HEAD_DIM), const)],
        out_specs=[pl.BlockSpec((tm, RET_Q), row), pl.BlockSpec((tm, RET_Q), row),
                   pl.BlockSpec((tm, D_MODEL), row), pl.BlockSpec((tm, 256), row)],
        out_shape=[_sds((T, RET_Q), _MXU), _sds((T, RET_Q), _MXU), _sds((T, D_MODEL), _MXU), _sds((T, 256), _MXU)],
        compiler_params=_params("parallel"),
    )(proj, proj, proj, proj, cos, sin, q_gain, k_gain)


def _ret_scan(name, x, y, y_col, w_asc, dec_asc, w_desc, dec_desc):
    T = x.shape[0]
    nc = T // CHUNK
    per = min(RET_CHUNKS_PER_STEP, nc)
    nb = nc // per
    rows_per = per * CHUNK

    def body(xa_ref, ya_ref, xd_ref, yd_ref, wa_ref, da_ref, wd_ref, dd_ref, sa_out, sd_out, sa, sd):
        @pl.when(pl.program_id(0) == 0)
        def _():
            sa[...] = jnp.zeros_like(sa)
            sd[...] = jnp.zeros_like(sd)
        for step in range(per):
            for c, x_ref, y_ref, w_ref, d_ref, st, out in ((step, xa_ref, ya_ref, wa_ref, da_ref, sa, sa_out),
                                                       (per - 1 - step, xd_ref, yd_ref, wd_ref, dd_ref, sd, sd_out)):
                rows = slice(c * CHUNK, (c + 1) * CHUNK)
                out[c] = st[...].astype(out.dtype)
                for h in range(RET_HEADS):
                    ks = slice(h * RET_DK, (h + 1) * RET_DK)
                    vs = slice(h * RET_DV, (h + 1) * RET_DV)
                    u = _dot(x_ref[rows, ks].astype(F32) * w_ref[:, ks], y_ref[rows, vs], _TN)
                    st[ks, :] = st[ks, :] * d_ref[h:h + 1, :] + u

    asc = lambda i: (i, 0)
    desc = lambda i: (nb - 1 - i, 0)
    const = lambda i: (0, 0)
    return pl.pallas_call(
        body, name=name, grid=(nb,),
        in_specs=[pl.BlockSpec((rows_per, RET_Q), asc), pl.BlockSpec((rows_per, RET_V), lambda i: (i, y_col)),
                  pl.BlockSpec((rows_per, RET_Q), desc), pl.BlockSpec((rows_per, RET_V), lambda i: (nb - 1 - i, y_col)),
                  pl.BlockSpec((CHUNK, RET_Q), const), pl.BlockSpec((RET_HEADS, RET_DV), const),
                  pl.BlockSpec((CHUNK, RET_Q), const), pl.BlockSpec((RET_HEADS, RET_DV), const)],
        out_specs=[pl.BlockSpec((per, RET_Q, RET_DV), lambda i: (i, 0, 0)),
                   pl.BlockSpec((per, RET_Q, RET_DV), lambda i: (nb - 1 - i, 0, 0))],
        out_shape=[_sds((nc, RET_Q, RET_DV), _MXU), _sds((nc, RET_Q, RET_DV), _MXU)],
        scratch_shapes=[pltpu.VMEM((RET_Q, RET_DV), F32), pltpu.VMEM((RET_Q, RET_DV), F32)],
        compiler_params=_params("arbitrary"),
    )(x, y, x, y, w_asc, dec_asc, w_desc, dec_desc)


def _ret_out(qr, kr, proj, sf, sb, tabs, gain):
    T = qr.shape[0]
    nc = T // CHUNK
    per = min(RET_CHUNKS_PER_STEP, nc)
    rows_per = per * CHUNK

    def body(q_ref, k_ref, v_ref, g_ref, sf_ref, sb_ref, tab_ref, gain_ref, o_ref, y_ref):
        for c in range(per):
            rows = slice(c * CHUNK, (c + 1) * CHUNK)
            for h in range(RET_HEADS):
                ks = slice(h * RET_DK, (h + 1) * RET_DK)
                vs = slice(h * RET_DV, (h + 1) * RET_DV)
                q, k, v = q_ref[rows, ks], k_ref[rows, ks], v_ref[rows, vs]
                qf = q.astype(F32)
                a_mat = _dot(q, k, _NT) * tab_ref[h, 0]
                o = (_dot(a_mat, v) + _dot(qf * tab_ref[h, 1], sf_ref[c, ks, :]) + _dot(qf * tab_ref[h, 2], sb_ref[c, ks, :]))
                o_ref[rows, vs] = o
                r = lax.rsqrt(jnp.mean(o * o, axis=-1, keepdims=True) + EPS)
                g = g_ref[rows, vs]
                y_ref[rows, vs] = (g * _sigmoid(g) * (o * r * gain_ref[:, vs])).astype(y_ref.dtype)

    row = lambda i: (i, 0)
    st = lambda i: (i, 0, 0)
    return pl.pallas_call(
        body, name="ret_out", grid=(nc // per,),
        in_specs=[pl.BlockSpec((rows_per, RET_Q), row), pl.BlockSpec((rows_per, RET_Q), row),
                  pl.BlockSpec((rows_per, RET_V), lambda i: (i, 1)), pl.BlockSpec((rows_per, RET_V), lambda i: (i, 2)),
                  pl.BlockSpec((per, RET_Q, RET_DV), st), pl.BlockSpec((per, RET_Q, RET_DV), st),
                  pl.BlockSpec((RET_HEADS, 3, CHUNK, CHUNK), lambda i: (0, 0, 0, 0)),
                  pl.BlockSpec((1, RET_V), lambda i: (0, 0))],
        out_specs=[pl.BlockSpec((rows_per, RET_V), row), pl.BlockSpec((rows_per, RET_V), row)],
        out_shape=[_sds((T, RET_V), F32), _sds((T, RET_V), _MXU)],
        compiler_params=_params("parallel"),
    )(qr, kr, proj, proj, sf, sb, tabs, gain)


def _ret_gate_bwd(dycat, proj, ret_o, gain, *, tm):
    T = ret_o.shape[0]
    tm = min(tm, T)

    def body(dy_ref, g_ref, o_ref, gain_ref, do_ref, dg_ref, dgain_ref):
        @pl.when(pl.program_id(0) == 0)
        def _():
            dgain_ref[...] = jnp.zeros_like(dgain_ref)
        for h in range(RET_HEADS):
            vs = slice(h * RET_DV, (h + 1) * RET_DV)
            o, g, dya, gn = o_ref[:, vs], g_ref[:, vs], dy_ref[:, vs], gain_ref[:, vs]
            r = lax.rsqrt(jnp.mean(o * o, axis=-1, keepdims=True) + EPS)
            ohat = o * r
            sg = _sigmoid(g)
            dy = dya * (g * sg)
            dg_ref[:, vs] = (dya * (ohat * gn) * (sg * (1.0 + g * (1.0 - sg)))).astype(dg_ref.dtype)
            dyg = dy * gn
            do_ref[:, vs] = (r * (dyg - ohat * jnp.mean(dyg * ohat, axis=-1, keepdims=True))).astype(do_ref.dtype)
            dgain_ref[:, vs] += _rowsum8(dy * ohat)

    row = lambda i: (i, 0)
    return pl.pallas_call(
        body, name="ret_gate_bwd", grid=(T // tm,),
        in_specs=[pl.BlockSpec((tm, RET_V), row), pl.BlockSpec((tm, RET_V), lambda i: (i, 2)),
                  pl.BlockSpec((tm, RET_V), row), pl.BlockSpec((1, RET_V), lambda i: (0, 0))],
        out_specs=[pl.BlockSpec((tm, RET_V), row), pl.BlockSpec((tm, RET_V), row), pl.BlockSpec((8, RET_V), lambda i: (0, 0))],
        out_shape=[_sds((T, RET_V), _MXU), _sds((T, RET_V), _MXU), _sds((8, RET_V), F32)],
        compiler_params=_params("arbitrary"),
    )(dycat, proj, ret_o, gain)


def _ret_bwd(qr, kr, proj, g_out, sf, sb, rf, rb, tabs):
    T = qr.shape[0]
    nc = T // CHUNK
    per = min(RET_CHUNKS_PER_STEP, nc)
    rows_per = per * CHUNK

    def body(q_ref, k_ref, v_ref, g_ref, sf_ref, sb_ref, rf_ref, rb_ref, tab_ref, dq_ref, dk_ref, dv_ref, dl_ref):
        @pl.when(pl.program_id(0) == 0)
        def _():
            dl_ref[...] = jnp.zeros_like(dl_ref)
        for c in range(per):
            rows = slice(c * CHUNK, (c + 1) * CHUNK)
            for h in range(RET_HEADS):
                ks = slice(h * RET_DK, (h + 1) * RET_DK)
                vs = slice(h * RET_DV, (h + 1) * RET_DV)
                q, k, v, g = q_ref[rows, ks], k_ref[rows, ks], v_ref[rows, vs], g_ref[rows, vs]
                s_f, s_b, r_f, r_b = sf_ref[c, ks, :], sb_ref[c, ks, :], rf_ref[c, ks, :], rb_ref[c, ks, :]
                tab = lambda t, h=h: tab_ref[h, t]
                qf, kf = q.astype(F32), k.astype(F32)
                qk = _dot(q, k, _NT)
                da_raw = _dot(g, v, _NT)
                x_f, x_b = _dot(g, s_f, _NT), _dot(g, s_b, _NT)
                dq_ref[rows, ks] = _dot(da_raw * tab(TAB_D), k) + tab(TAB_A) * x_f + tab(TAB_B) * x_b
                at = _dot(k, q, _NT) * tab(TAB_DT)
                dat = _dot(v, g, _NT) * tab(TAB_DT)
                y_f, y_b = _dot(v, r_f, _NT), _dot(v, r_b, _NT)
                dk_ref[rows, ks] = _dot(dat, q) + tab(TAB_CF) * y_f + tab(TAB_CB) * y_b
                dv_ref[rows, vs] = (_dot(at, g) + _dot(kf * tab(TAB_CF), r_f) + _dot(kf * tab(TAB_CB), r_b)).astype(dv_ref.dtype)
                inner = da_raw * qk
                rs_f = r_f.astype(F32) * s_f.astype(F32)
                rs_b = r_b.astype(F32) * s_b.astype(F32)
                dl_f = (inner * tab(TAB_EF) + tab(TAB_RA) * qf * x_f + tab(TAB_RCF) * kf * y_f
                        + tab(TAB_KF) * (rs_f[:, :CHUNK] + rs_f[:, CHUNK:]))
                dl_b = (inner * tab(TAB_EB) + tab(TAB_RB) * qf * x_b + tab(TAB_RCB) * kf * y_b
                        + tab(TAB_KB) * (rs_b[:, :CHUNK] + rs_b[:, CHUNK:]))
                dl_ref[2 * h:2 * h + 1, :] += jnp.sum(dl_f, axis=0, keepdims=True)
                dl_ref[2 * h + 1:2 * h + 2, :] += jnp.sum(dl_b, axis=0, keepdims=True)

    row = lambda i: (i, 0)
    st = lambda i: (i, 0, 0)
    return pl.pallas_call(
        body, name="ret_bwd", grid=(nc // per,),
        in_specs=[pl.BlockSpec((rows_per, RET_Q), row), pl.BlockSpec((rows_per, RET_Q), row),
                  pl.BlockSpec((rows_per, RET_V), lambda i: (i, 1)), pl.BlockSpec((rows_per, RET_V), row),
                  pl.BlockSpec((per, RET_Q, RET_DV), st), pl.BlockSpec((per, RET_Q, RET_DV), st),
                  pl.BlockSpec((per, RET_Q, RET_DV), st), pl.BlockSpec((per, RET_Q, RET_DV), st),
                  pl.BlockSpec((RET_HEADS, 14, CHUNK, CHUNK), lambda i: (0, 0, 0, 0))],
        out_specs=[pl.BlockSpec((rows_per, RET_Q), row), pl.BlockSpec((rows_per, RET_Q), row),
                   pl.BlockSpec((rows_per, RET_V), row), pl.BlockSpec((8, CHUNK), lambda i: (0, 0))],
        out_shape=[_sds((T, RET_Q), F32), _sds((T, RET_Q), F32), _sds((T, RET_V), _MXU), _sds((8, CHUNK), F32)],
        compiler_params=_params("arbitrary"),
    )(qr, kr, proj, g_out, sf, sb, rf, rb, tabs)


def _swa_probs(q, k_win, bias, sink, valid):
    s = _dot(q, k_win, _NT) * ATT_SCALE + bias
    s = jnp.where(valid, s, NEG_INF)
    m = jnp.maximum(jnp.max(s, axis=-1, keepdims=True), sink)
    p = jnp.exp(s - m)
    e_sink = jnp.exp(sink - m)
    inv = 1.0 / (jnp.sum(p, axis=-1, keepdims=True) + e_sink)
    return p * inv, e_sink * inv


def _swa_group(g, q_ref, bias_ref, sink_ref):
    group = SWA_HEADS // SWA_KV_HEADS
    heads = range(g * group, (g + 1) * group)
    q = jnp.concatenate([q_ref[:, h * HEAD_DIM:(h + 1) * HEAD_DIM] for h in heads], axis=0)
    bias = bias_ref[g * group:(g + 1) * group].reshape(group * CHUNK, 3 * CHUNK)
    sink = jnp.concatenate([jnp.broadcast_to(sink_ref[h:h + 1, 0:1], (CHUNK, 1)) for h in heads], axis=0)
    return q, bias, sink


def _swa_valid(i, nb):
    col = lax.broadcasted_iota(jnp.int32, (1, 3 * CHUNK), 1)
    return jnp.logical_and(jnp.logical_or(col >= CHUNK, i > 0), jnp.logical_or(col < 2 * CHUNK, i < nb - 1))


def _swa_window_specs(nb, width, col_block, clamp):
    prev = lambda i: (jnp.maximum(clamp(i) - 1, 0), col_block)
    cur = lambda i: (clamp(i), col_block)
    nxt = lambda i: (jnp.minimum(clamp(i) + 1, nb - 1), col_block)
    return [pl.BlockSpec((CHUNK, width), f) for f in (prev, cur, nxt)]


def _swa_fwd(qn, kn, proj, bias, sink):
    T = qn.shape[0]
    nb = T // CHUNK
    kvw = SWA_KV_HEADS * HEAD_DIM
    group = SWA_HEADS // SWA_KV_HEADS

    def body(q_ref, k0, k1, k2, v0, v1, v2, bias_ref, sink_ref, y_ref):
        i = pl.program_id(0)
        valid = _swa_valid(i, nb)
        for g in range(SWA_KV_HEADS):
            gs = slice(g * HEAD_DIM, (g + 1) * HEAD_DIM)
            k_win = jnp.concatenate([k0[:, gs], k1[:, gs], k2[:, gs]], axis=0)
            v_win = jnp.concatenate([v0[:, gs], v1[:, gs], v2[:, gs]], axis=0).astype(_MXU)
            q, bias, sink = _swa_group(g, q_ref, bias_ref, sink_ref)
            p, _ = _swa_probs(q, k_win, bias, sink, valid)
            o = _dot(p, v_win)
            for hh in range(group):
                h = g * group + hh
                y_ref[:, h * HEAD_DIM:(h + 1) * HEAD_DIM] = o[hh * CHUNK:(hh + 1) * CHUNK].astype(y_ref.dtype)

    ident = lambda i: i
    return pl.pallas_call(
        body, name="swa_fwd", grid=(nb,),
        in_specs=[pl.BlockSpec((CHUNK, D_MODEL), lambda i: (i, 0))]
        + _swa_window_specs(nb, kvw, 0, ident) + _swa_window_specs(nb, kvw, 17, ident)
        + [pl.BlockSpec((SWA_HEADS, CHUNK, 3 * CHUNK), lambda i: (0, 0, 0)), pl.BlockSpec((SWA_HEADS, HEAD_DIM), lambda i: (0, 0))],
        out_specs=pl.BlockSpec((CHUNK, D_MODEL), lambda i: (i, 0)),
        out_shape=_sds((T, D_MODEL), _MXU),
        compiler_params=_params("parallel"),
    )(qn, kn, kn, kn, proj, proj, proj, bias, sink)


def _swa_bwd(qn, kn, proj, dycat, bias, sink):
    T = qn.shape[0]
    nb = T // CHUNK
    kvw = SWA_KV_HEADS * HEAD_DIM
    group = SWA_HEADS // SWA_KV_HEADS

    def body(q_ref, k0, k1, k2, v0, v1, v2, dy_ref, bias_ref, sink_ref,
             dq_ref, dk_ref, dv_ref, dbias_ref, dsink_ref, acc_a, acc_b):
        i = pl.program_id(0)

        @pl.when(i == 0)
        def _():
            dbias_ref[...] = jnp.zeros_like(dbias_ref)
            dsink_ref[...] = jnp.zeros_like(dsink_ref)
            acc_a[...] = jnp.zeros_like(acc_a)
            acc_b[...] = jnp.zeros_like(acc_b)

        @pl.when(i < nb)
        def _():
            valid = _swa_valid(i, nb)
            for g in range(SWA_KV_HEADS):
                gs = slice(g * HEAD_DIM, (g + 1) * HEAD_DIM)
                k_win = jnp.concatenate([k0[:, gs], k1[:, gs], k2[:, gs]], axis=0)
                v_win = jnp.concatenate([v0[:, gs], v1[:, gs], v2[:, gs]], axis=0).astype(_MXU)
                q, bias, sink = _swa_group(g, q_ref, bias_ref, sink_ref)
                dy = jnp.concatenate([dy_ref[:, h * HEAD_DIM:(h + 1) * HEAD_DIM] for h in range(g * group, (g + 1) * group)], axis=0)
                p, p_sink = _swa_probs(q, k_win, bias, sink, valid)
                dp = _dot(dy, v_win, _NT)
                delta = jnp.sum(p * dp, axis=-1, keepdims=True)
                ds = p * (dp - delta)
                dsink = -p_sink * delta
                dq = _dot(ds, k_win) * ATT_SCALE
                for hh in range(group):
                    h = g * group + hh
                    rows = slice(hh * CHUNK, (hh + 1) * CHUNK)
                    dbias_ref[h] += ds[rows]
                    dsink_ref[h:h + 1, :] += jnp.sum(dsink[rows], axis=0, keepdims=True) * jnp.ones((1, HEAD_DIM), F32)
                    dq_ref[:, h * HEAD_DIM:(h + 1) * HEAD_DIM] = dq[rows]
                dk_win = _dot(ds, q, _TN) * ATT_SCALE
                dv_win = _dot(p, dy, _TN)
                for win, out_ref, col0 in ((dk_win, dk_ref, 0), (dv_win, dv_ref, kvw)):
                    cs = slice(col0 + g * HEAD_DIM, col0 + (g + 1) * HEAD_DIM)
                    out_ref[:, gs] = acc_a[:, cs] + win[:CHUNK]
                    acc_a[:, cs] = acc_b[:, cs] + win[CHUNK:2 * CHUNK]
                    acc_b[:, cs] = win[2 * CHUNK:]

        @pl.when(i == nb)
        def _():
            dk_ref[...] = acc_a[:, :kvw]
            dv_ref[...] = acc_a[:, kvw:]

    clamp = lambda i: jnp.minimum(i, nb - 1)
    late = lambda i: (jnp.maximum(i - 1, 0), 0)
    return pl.pallas_call(
        body, name="swa_bwd", grid=(nb + 1,),
        in_specs=[pl.BlockSpec((CHUNK, D_MODEL), lambda i: (clamp(i), 0))]
        + _swa_window_specs(nb, kvw, 0, clamp) + _swa_window_specs(nb, kvw, 17, clamp)
        + [pl.BlockSpec((CHUNK, D_MODEL), lambda i: (clamp(i), 1)),
           pl.BlockSpec((SWA_HEADS, CHUNK, 3 * CHUNK), lambda i: (0, 0, 0)), pl.BlockSpec((SWA_HEADS, HEAD_DIM), lambda i: (0, 0))],
        out_specs=[pl.BlockSpec((CHUNK, D_MODEL), lambda i: (clamp(i), 0)),
                   pl.BlockSpec((CHUNK, kvw), late), pl.BlockSpec((CHUNK, kvw), late),
                   pl.BlockSpec((SWA_HEADS, CHUNK, 3 * CHUNK), lambda i: (0, 0, 0)), pl.BlockSpec((SWA_HEADS, HEAD_DIM), lambda i: (0, 0))],
        out_shape=[_sds((T, D_MODEL), F32), _sds((T, kvw), F32), _sds((T, kvw), F32),
                   _sds((SWA_HEADS, CHUNK, 3 * CHUNK), F32), _sds((SWA_HEADS, HEAD_DIM), F32)],
        scratch_shapes=[pltpu.VMEM((CHUNK, 2 * kvw), F32), pltpu.VMEM((CHUNK, 2 * kvw), F32)],
        compiler_params=_params("arbitrary"),
    )(qn, kn, kn, kn, proj, proj, proj, dycat, bias, sink)


def _t5_bucket_reduce(dbias, bucket):
    def body(db_ref, bk_ref, o_ref):
        bk = bk_ref[...]
        row = lax.broadcasted_iota(jnp.int32, (SWA_HEADS, HEAD_DIM), 0)
        lane = lax.broadcasted_iota(jnp.int32, (SWA_HEADS, HEAD_DIM), 1)

        def per_bucket(b, acc):
            mask = bk == b
            for h in range(SWA_HEADS):
                tot = jnp.sum(jnp.sum(jnp.where(mask, db_ref[h], 0.0), axis=0, keepdims=True), axis=1, keepdims=True)
                acc = acc + jnp.where(jnp.logical_and(row == h, lane == b), tot, 0.0)
            return acc

        o_ref[...] = lax.fori_loop(0, T5_BUCKETS, per_bucket, jnp.zeros((SWA_HEADS, HEAD_DIM), F32))

    return pl.pallas_call(body, name="t5_bucket_reduce", out_shape=_sds((SWA_HEADS, HEAD_DIM), F32),
                          compiler_params=pltpu.CompilerParams(vmem_limit_bytes=VMEM_LIMIT_BYTES))(dbias, bucket)


def _headnorm_bwd(x, dy, gain):
    r = lax.rsqrt(jnp.mean(x * x, axis=-1, keepdims=True) + EPS)
    xhat = x * r
    dyg = dy * gain
    return r * (dyg - xhat * jnp.mean(dyg * xhat, axis=-1, keepdims=True)), dy * xhat


def _post_even(proj, dqr, dkr, dva, dga, dqn, dkn, dvb, cos, sin, q_gain, k_gain, *, tm):
    T = proj.shape[0]
    tm = min(tm, T)
    kvw = SWA_KV_HEADS * HEAD_DIM

    def body(qb_ref, kb_ref, dqr_ref, dkr_ref, dva_ref, dga_ref, dqn_ref, dkn_ref, dvb_ref, c_ref, s_ref, qg_ref, kg_ref,
             dp_ref, dqg_ref, dkg_ref):
        @pl.when(pl.program_id(0) == 0)
        def _():
            dqg_ref[...] = jnp.zeros_like(dqg_ref)
            dkg_ref[...] = jnp.zeros_like(dkg_ref)
        c = jnp.concatenate([c_ref[...]] * RET_HEADS, axis=1)
        s = jnp.concatenate([s_ref[...]] * RET_HEADS, axis=1)
        dq = dqr_ref[...]
        dp_ref[:, 0:RET_Q] = (dq * c + _swap_halves(dq * s, RET_DK // 2)).astype(dp_ref.dtype)
        dk = dkr_ref[...] * (RET_DK ** -0.5)
        dp_ref[:, RET_Q:2 * RET_Q] = (dk * c + _swap_halves(dk * s, RET_DK // 2)).astype(dp_ref.dtype)
        off = 2 * RET_Q
        dp_ref[:, off:off + RET_V] = dva_ref[...].astype(dp_ref.dtype)
        dp_ref[:, off + RET_V:off + 2 * RET_V] = dga_ref[...].astype(dp_ref.dtype)
        off += 2 * RET_V
        for src, dsrc, gain, dgain, heads, base in ((qb_ref, dqn_ref, qg_ref, dqg_ref, SWA_HEADS, off),
                                                    (kb_ref, dkn_ref, kg_ref, dkg_ref, SWA_KV_HEADS, off + D_MODEL)):
            for h in range(heads):
                sl = slice(h * HEAD_DIM, (h + 1) * HEAD_DIM)
                dx, dgx = _headnorm_bwd(src[:, sl], dsrc[:, sl], gain[...])
                dp_ref[:, base + h * HEAD_DIM:base + (h + 1) * HEAD_DIM] = dx.astype(dp_ref.dtype)
                dgain[...] += _rowsum8(dgx)
        dp_ref[:, off + D_MODEL + kvw:] = dvb_ref[...].astype(dp_ref.dtype)

    row = lambda i: (i, 0)
    const = lambda i: (0, 0)
    return pl.pallas_call(
        body, name="post_even", grid=(T // tm,),
        in_specs=[pl.BlockSpec((tm, D_MODEL), lambda i: (i, 3)), pl.BlockSpec((tm, kvw), lambda i: (i, 16)),
                  pl.BlockSpec((tm, RET_Q), row), pl.BlockSpec((tm, RET_Q), row),
                  pl.BlockSpec((tm, RET_V), row), pl.BlockSpec((tm, RET_V), row),
                  pl.BlockSpec((tm, D_MODEL), row), pl.BlockSpec((tm, kvw), row), pl.BlockSpec((tm, kvw), row),
                  pl.BlockSpec((tm, RET_DK), row), pl.BlockSpec((tm, RET_DK), row),
                  pl.BlockSpec((1, HEAD_DIM), const), pl.BlockSpec((1, HEAD_DIM), const)],
        out_specs=[pl.BlockSpec((tm, EVEN_IN), row), pl.BlockSpec((8, HEAD_DIM), const), pl.BlockSpec((8, HEAD_DIM), const)],
        out_shape=[_sds((T, EVEN_IN), _MXU), _sds((8, HEAD_DIM), F32), _sds((8, HEAD_DIM), F32)],
        compiler_params=_params("arbitrary"),
    )(proj, proj, dqr, dkr, dva, dga, dqn, dkn, dvb, cos, sin, q_gain, k_gain)


def _prep_odd(proj, cos, sin, q_gain, k_gain, *, tm):
    T = proj.shape[0]
    tm = min(tm, T)
    kvw = AX_KV_HEADS * HEAD_DIM

    def body(q_ref, k_ref, v_ref, c_ref, s_ref, qg_ref, kg_ref, qx_ref, kx_ref, vx_ref):
        c, s = c_ref[...], s_ref[...]
        for src, gain, dst, heads, scale in ((q_ref, qg_ref, qx_ref, AX_HEADS, SCORE_SCALE_LOG2), (k_ref, kg_ref, kx_ref, AX_KV_HEADS, 1.0)):
            for h in range(heads):
                sl = slice(h * HEAD_DIM, (h + 1) * HEAD_DIM)
                xh = src[:, sl]
                r = lax.rsqrt(jnp.mean(xh * xh, axis=-1, keepdims=True) + EPS)
                xn = xh * r * gain[...]
                dst[:, sl] = ((xn * c + _swap_halves(xn, HEAD_DIM // 4) * s) * scale).astype(dst.dtype)
        vx_ref[...] = v_ref[...].astype(vx_ref.dtype)

    row = lambda i: (i, 0)
    const = lambda i: (0, 0)
    return pl.pallas_call(
        body, name="prep_odd", grid=(T // tm,),
        in_specs=[pl.BlockSpec((tm, D_MODEL), row), pl.BlockSpec((tm, kvw), lambda i: (i, 4)), pl.BlockSpec((tm, kvw), lambda i: (i, 5)),
                  pl.BlockSpec((tm, HEAD_DIM), row), pl.BlockSpec((tm, HEAD_DIM), row),
                  pl.BlockSpec((1, HEAD_DIM), const), pl.BlockSpec((1, HEAD_DIM), const)],
        out_specs=[pl.BlockSpec((tm, D_MODEL), row), pl.BlockSpec((tm, kvw), row), pl.BlockSpec((tm, kvw), row)],
        out_shape=[_sds((T, D_MODEL), _MXU), _sds((T, kvw), _MXU), _sds((T, kvw), _MXU)],
        compiler_params=_params("parallel"),
    )(proj, proj, proj, cos, sin, q_gain, k_gain)


def _post_odd(proj, dqxt, dkx, dvx, cos, sin, q_gain, k_gain, *, tm):
    T = proj.shape[0]
    tm = min(tm, T)
    kvw = AX_KV_HEADS * HEAD_DIM

    def body(q_ref, k_ref, dqt_ref, dk_ref, dv_ref, c_ref, s_ref, qg_ref, kg_ref, dp_ref, dqg_ref, dkg_ref):
        @pl.when(pl.program_id(0) == 0)
        def _():
            dqg_ref[...] = jnp.zeros_like(dqg_ref)
            dkg_ref[...] = jnp.zeros_like(dkg_ref)
        c, s = c_ref[...], s_ref[...]
        for src, dsrc, gain, dgain, heads, base in ((q_ref, dqt_ref, qg_ref, dqg_ref, AX_HEADS, 0),
                                                    (k_ref, dk_ref, kg_ref, dkg_ref, AX_KV_HEADS, D_MODEL)):
            for h in range(heads):
                sl = slice(h * HEAD_DIM, (h + 1) * HEAD_DIM)
                d = dsrc[sl, :].T if dsrc is dqt_ref else dsrc[:, sl]
                dn = d * c + _swap_halves(d * s, HEAD_DIM // 4)
                dx, dgx = _headnorm_bwd(src[:, sl], dn, gain[...])
                dp_ref[:, base + h * HEAD_DIM:base + (h + 1) * HEAD_DIM] = dx.astype(dp_ref.dtype)
                dgain[...] += _rowsum8(dgx)
        dp_ref[:, D_MODEL + kvw:] = dv_ref[...].astype(dp_ref.dtype)

    row = lambda i: (i, 0)
    const = lambda i: (0, 0)
    return pl.pallas_call(
        body, name="post_odd", grid=(T // tm,),
        in_specs=[pl.BlockSpec((tm, D_MODEL), row), pl.BlockSpec((tm, kvw), lambda i: (i, 4)),
                  pl.BlockSpec((D_MODEL, tm), lambda i: (0, i)), pl.BlockSpec((tm, kvw), row), pl.BlockSpec((tm, kvw), row),
                  pl.BlockSpec((tm, HEAD_DIM), row), pl.BlockSpec((tm, HEAD_DIM), row),
                  pl.BlockSpec((1, HEAD_DIM), const), pl.BlockSpec((1, HEAD_DIM), const)],
        out_specs=[pl.BlockSpec((tm, ODD_IN), row), pl.BlockSpec((8, HEAD_DIM), const), pl.BlockSpec((8, HEAD_DIM), const)],
        out_shape=[_sds((T, ODD_IN), _MXU), _sds((8, HEAD_DIM), F32), _sds((8, HEAD_DIM), F32)],
        compiler_params=_params("arbitrary"),
    )(proj, proj, dqxt, dkx, dvx, cos, sin, q_gain, k_gain)


ONES_ROWS = 16


def _flash_fwd(qx, kx, vx, *, tq, tk):
    v1t = jnp.concatenate([vx.T.reshape(AX_KV_HEADS, HEAD_DIM, vx.shape[0]),
                           jnp.ones((AX_KV_HEADS, ONES_ROWS, vx.shape[0]), vx.dtype)], axis=1)
    T = qx.shape[0]
    tq, tk = min(tq, T), min(tk, T)
    nq, nk = T // tq, T // tk
    group = AX_HEADS // AX_KV_HEADS

    def body(k_ref, v_ref, q_ref, o_ref, lse_ref, acc_sc, m_sc, l_sc):
        j = pl.program_id(2)

        @pl.when(j == 0)
        def _():
            m_sc[...] = jnp.full(m_sc.shape, NEG_INF, F32)
            l_sc[...] = jnp.zeros_like(l_sc)
            acc_sc[...] = jnp.zeros_like(acc_sc)
        k, v = k_ref[...], v_ref[0]

        def step(i, carry):
            cols = pl.ds(pl.multiple_of(i * tq, tq), tq)
            st = _dot(k, q_ref[cols, :], _NT)
            m_old = m_sc[i]
            m_new = jnp.maximum(m_old, jnp.max(st, axis=0, keepdims=True))
            p = jnp.exp2(st - m_new)
            alpha = jnp.exp2(m_old - m_new)
            pv = _dot(v, p)
            m_sc[i] = m_new
            l_sc[i] = alpha * l_sc[i] + pv[HEAD_DIM:HEAD_DIM + 1]
            acc_sc[:, cols] = alpha * acc_sc[:, cols] + pv[:HEAD_DIM]
            return carry

        lax.fori_loop(0, nq, step, 0)

        @pl.when(j == nk - 1)
        def _():
            def finish(i, carry):
                cols = pl.ds(pl.multiple_of(i * tq, tq), tq)
                o_ref[cols, :] = (acc_sc[:, cols] / l_sc[i]).T.astype(o_ref.dtype)
                lse_ref[0, i] = m_sc[i] + jnp.log2(l_sc[i])
                return carry

            lax.fori_loop(0, nq, finish, 0)

    kv = lambda g, h, j: (j, g)
    qh = lambda g, h, j: (0, g * group + h)
    o, lse = pl.pallas_call(
        body, name="flash_fwd", grid=(AX_KV_HEADS, group, nk),
        in_specs=[pl.BlockSpec((tk, HEAD_DIM), kv), pl.BlockSpec((1, HEAD_DIM + ONES_ROWS, tk), lambda g, h, j: (g, 0, j)),
                  pl.BlockSpec((T, HEAD_DIM), qh)],
        out_specs=[pl.BlockSpec((T, HEAD_DIM), qh), pl.BlockSpec((1, nq, 1, tq), lambda g, h, j: (g * group + h, 0, 0, 0))],
        out_shape=[_sds((T, D_MODEL), _MXU), _sds((AX_HEADS, nq, 1, tq), F32)],
        scratch_shapes=[pltpu.VMEM((HEAD_DIM, T), F32), pltpu.VMEM((nq, 1, tq), F32), pltpu.VMEM((nq, 1, tq), F32)],
        compiler_params=_params("parallel", "arbitrary", "arbitrary"),
    )(kx, v1t, qx)
    return o, lse.reshape(AX_HEADS, 1, T)


def _flash_bwd(qx, kx, vx, o, do, lse, *, tq, tk):
    T = qx.shape[0]
    tq, tk = min(tq, T), min(tk, T)
    nq = T // tq
    group = AX_HEADS // AX_KV_HEADS
    lse_rows = lse.reshape(AX_HEADS, nq, 1, tq)

    def body(k_ref, v_ref, q_ref, o_ref, do_ref, lse_ref, dqt_ref, dk_ref, dv_ref, delta_sc):
        j = pl.program_id(2)

        @pl.when(jnp.logical_and(pl.program_id(1) == 0, j == 0))
        def _():
            dk_ref[...] = jnp.zeros_like(dk_ref)
            dv_ref[...] = jnp.zeros_like(dv_ref)

        @pl.when(j == 0)
        def _():
            dqt_ref[...] = jnp.zeros_like(dqt_ref)

            def row_delta(i, carry):
                rows = pl.ds(pl.multiple_of(i * tq, tq), tq)
                prod = do_ref[rows, :].astype(F32) * o_ref[rows, :].astype(F32)
                delta_sc[i] = jnp.sum(prod.T, axis=0, keepdims=True)
                return carry

            lax.fori_loop(0, nq, row_delta, 0)
        k, v = k_ref[...], v_ref[...]

        def step(i, carry):
            dk, dv = carry
            off = pl.multiple_of(i * tq, tq)
            q, do_blk = q_ref[pl.ds(off, tq), :], do_ref[pl.ds(off, tq), :]
            pt = jnp.exp2(_dot(k, q, _NT) - lse_ref[0, i])
            dst = pt * (_dot(v, do_blk, _NT) - delta_sc[i])
            dqt_ref[:, pl.ds(off, tq)] += _dot(k, dst, _TN) * ATT_SCALE
            return dk + _dot(dst, q), dv + _dot(pt, do_blk)

        zero = jnp.zeros((tk, HEAD_DIM), F32)
        dk, dv = lax.fori_loop(0, nq, step, (zero, zero))
        rows = pl.ds(pl.multiple_of(j * tk, tk), tk)
        dk_ref[rows, :] += dk * (ATT_SCALE / SCORE_SCALE_LOG2)
        dv_ref[rows, :] += dv

    kv = lambda g, h, j: (j, g)
    qh = lambda g, h, j: (0, g * group + h)
    st = lambda g, h, j: (g * group + h, 0, 0, 0)
    acc = lambda g, h, j: (0, g)
    return pl.pallas_call(
        body, name="flash_bwd", grid=(AX_KV_HEADS, group, T // tk),
        in_specs=[pl.BlockSpec((tk, HEAD_DIM), kv), pl.BlockSpec((tk, HEAD_DIM), kv),
                  pl.BlockSpec((T, HEAD_DIM), qh), pl.BlockSpec((T, HEAD_DIM), qh), pl.BlockSpec((T, HEAD_DIM), qh),
                  pl.BlockSpec((1, nq, 1, tq), st)],
        out_specs=[pl.BlockSpec((HEAD_DIM, T), lambda g, h, j: (g * group + h, 0)),
                   pl.BlockSpec((T, HEAD_DIM), acc), pl.BlockSpec((T, HEAD_DIM), acc)],
        out_shape=[_sds((D_MODEL, T), F32), _sds((T, AX_KV_HEADS * HEAD_DIM), F32), _sds((T, AX_KV_HEADS * HEAD_DIM), F32)],
        scratch_shapes=[pltpu.VMEM((nq, 1, tq), F32)],
        compiler_params=_params("parallel", "arbitrary", "arbitrary"),
    )(kx, vx, qx, o, do, lse_rows)


TM = 1024
TM_WIDE = 512


def _mlp_fwd(tag, x, gain, w_up, w_down, target=None):
    u, h = _norm_matmul(f"mlp_up{tag}", x, gain, w_up, tm=TM, tn=1024, out_dtype=_MXU)
    out = _matmul_res(f"mlp_down{tag}", [u], w_down, x, tm=TM if target is None else TM_WIDE, relu2=True, target=target)
    return out, (x, u, h)


def _local_step(x, target, p, w_first, fetch_rest, push, tokens=()):
    T = x.shape[0]
    cos_r, sin_r = _ret_rope_tables(T)
    cos_a, sin_a = _axial_rope_tables(T)
    tabs, rw, log_gamma = _retention_tables(p["ret_decay_logit"][0])
    bias = _swa_bias(p["t5_table"])
    sink = p["swa_sink"][0][:, None] * jnp.ones((1, HEAD_DIM), F32)
    nm, nl = p["norm_mix"], p["norm_mlp"]
    pending = [t for t in tokens if t is not None]

    def send(tag, weight, dw):
        token = push(tag, weight, dw[None])
        if token is not None:
            pending.append(token)

    def tied(operand):
        while pending:
            operand = operand + pending.pop()[0:1, 0:1]
        return operand

    def mlp_bwd(tag, saved, gain, w_up, w_down, dy, dy16):
        xs, u, h = saved
        du = _matmul_nt(f"mlp_down{tag}_bwd", dy16, w_down, tm=TM, tn=1024, out_dtype=_MXU, relu_of=u)
        send(f"mlp_down{tag}", "w_mlp_down", _matmul_tn(f"mlp_down{tag}_dw", u, dy16, tk=1024, tn=1024, tt=1024, out_dtype=_WIRE, relu2=True))
        dx, dx16, dgain = _matmul_nt_normbwd(f"mlp_up{tag}_bwd", du, w_up, xs, tied(gain), dy, tm=TM_WIDE)
        send(f"mlp_up{tag}", "w_mlp_up", _matmul_tn(f"mlp_up{tag}_dw", h, du, tk=1024, tn=1024, tt=1024, out_dtype=_WIRE))
        return dx, dx16, dgain

    proj0, h0 = _norm_matmul("in_even", x, tied(nm[0:1]), w_first["w_in_even"], tm=TM, tn=1152, out_dtype=F32)
    qr, kr, qn, kn = _prep_even(proj0, cos_r, sin_r, p["swa_q_norm"], p["swa_k_norm"], tm=TM)
    sf, sb = _ret_scan("ret_scan_fwd", kr, proj0, 1, rw["cf"], rw["dec_f"], rw["cb"], rw["dec_b"])
    ret_o, ya = _ret_out(qr, kr, proj0, sf, sb, tabs[:, (TAB_D, TAB_A, TAB_B)], p["ret_norm"])
    yb = _swa_fwd(qn, kn, proj0, bias, sink)
    wf = {**w_first, **fetch_rest(yb)}
    x1 = _matmul_res("out_even", [ya, yb], wf["w_out_even"], x, tm=TM)
    x2, mlp0 = _mlp_fwd(0, x1, nl[0:1], wf["w_mlp_up"][0], wf["w_mlp_down"][0])
    proj1, h1 = _norm_matmul("in_odd", x2, nm[1:2], wf["w_in_odd"], tm=TM, tn=768, out_dtype=F32)
    qx, kx, vx = _prep_odd(proj1, cos_a, sin_a, p["ax_q_norm"], p["ax_k_norm"], tm=TM)
    o, lse = _flash_fwd(qx, kx, vx, tq=2048, tk=1024)
    x3 = _matmul_res("out_odd", [o], wf["w_out_odd"], x2, tm=TM)
    (g4, g4_16, loss_part), mlp1 = _mlp_fwd(1, x3, nl[1:2], wf["w_mlp_up"][1], wf["w_mlp_down"][1], target=target)

    dx3, dx3_16, dnl1 = mlp_bwd(1, mlp1, nl[1:2], wf["w_mlp_up"][1], wf["w_mlp_down"][1], g4, g4_16)
    do = _matmul_nt("out_odd_bwd", dx3_16, wf["w_out_odd"], tm=TM, tn=1024, out_dtype=_MXU)
    send("out_odd", "w_out_odd", _matmul_tn("out_odd_dw", o, dx3_16, tk=1024, tn=1024, tt=1024, out_dtype=_WIRE))
    dqxt, dkx, dvx = _flash_bwd(qx, kx, vx, o, do, lse, tq=2048, tk=512)
    dproj1, dqg1, dkg1 = _post_odd(proj1, dqxt, dkx, dvx, cos_a, sin_a, tied(p["ax_q_norm"]), p["ax_k_norm"], tm=TM)
    send("in_odd", "w_in_odd", _matmul_tn("in_odd_dw", h1, dproj1, tk=1024, tn=768, tt=1024, out_dtype=_WIRE))
    dx2, dx2_16, dnm1 = _matmul_nt_normbwd("in_odd_bwd", dproj1, wf["w_in_odd"], x2, tied(nm[1:2]), dx3, tm=TM_WIDE)
    dx1, dx1_16, dnl0 = mlp_bwd(0, mlp0, nl[0:1], wf["w_mlp_up"][0], wf["w_mlp_down"][0], dx2, dx2_16)
    dycat = _matmul_nt("out_even_bwd", dx1_16, wf["w_out_even"], tm=TM, tn=1024, out_dtype=F32)
    send("out_even", "w_out_even", jnp.concatenate([
        _matmul_tn("out_even_dw_ret", ya, dx1_16, tk=1024, tn=1024, tt=1024, out_dtype=_WIRE),
        _matmul_tn("out_even_dw_swa", yb, dx1_16, tk=1024, tn=1024, tt=1024, out_dtype=_WIRE)], axis=0))
    g_out, dga, dretg = _ret_gate_bwd(dycat, proj0, ret_o, tied(p["ret_norm"]), tm=TM)
    rb, rf = _ret_scan("ret_scan_bwd", qr, g_out, 0, rw["b"], rw["dec_b"], rw["a"], rw["dec_f"])
    dqr, dkr, dva, dlog = _ret_bwd(qr, kr, proj0, g_out, sf, sb, rf, rb, tabs)
    dqn, dkn, dvb, dbias, dsink = _swa_bwd(qn, kn, proj0, dycat, bias, sink)
    dt5 = _t5_bucket_reduce(dbias, _t5_bucket(_swa_rel()).astype(jnp.int32))
    dproj0, dqg0, dkg0 = _post_even(proj0, dqr, dkr, dva, dga, dqn, dkn, dvb, cos_r, sin_r,
                                    p["swa_q_norm"], p["swa_k_norm"], tm=TM_WIDE)
    send("in_even", "w_in_even", _matmul_tn("in_even_dw", h0, dproj0, tk=1024, tn=1152, tt=1024, out_dtype=_WIRE))
    dx0, _, dnm0 = _matmul_nt_normbwd("in_even_bwd", dproj0, w_first["w_in_even"], x, tied(nm[0:1]), dx1, tm=TM_WIDE)

    fold = lambda part: jnp.sum(part, axis=0)
    dlam = jnp.sum(dlog, axis=1).reshape(RET_HEADS, 2).T
    small = {
        "norm_mix": jnp.stack([fold(dnm0), fold(dnm1)]),
        "norm_mlp": jnp.stack([fold(dnl0), fold(dnl1)]),
        "ret_decay_logit": (dlam * (1.0 - jnp.exp(log_gamma)))[None],
        "ret_norm": fold(dretg)[None],
        "swa_q_norm": fold(dqg0)[None], "swa_k_norm": fold(dkg0)[None],
        "swa_sink": dsink[:, 0][None],
        "t5_table": dt5[:, :T5_BUCKETS].T,
        "ax_q_norm": fold(dqg1)[None], "ax_k_norm": fold(dkg1)[None],
    }
    return loss_part, dx0, small


BIG = ("w_in_even", "w_out_even", "w_in_odd", "w_out_odd", "w_mlp_up", "w_mlp_down")
SMALL = ("norm_mix", "norm_mlp", "ret_decay_logit", "ret_norm", "swa_q_norm", "swa_k_norm", "swa_sink", "t5_table",
         "ax_q_norm", "ax_k_norm")
WEIGHTS = ("norm_mix", "norm_mlp", "w_in_even", "w_out_even", "ret_decay_logit", "ret_norm", "swa_q_norm", "swa_k_norm",
           "swa_sink", "t5_table", "w_in_odd", "w_out_odd", "ax_q_norm", "ax_k_norm", "w_mlp_up", "w_mlp_down")
SHARD_AXIS = {"w_in_even": 2, "w_out_even": 1, "w_in_odd": 2, "w_out_odd": 1, "w_mlp_up": 2, "w_mlp_down": 1}
N_CHIPS = 4
ANY = pl.BlockSpec(memory_space=pl.ANY)
HBM = pl.BlockSpec(memory_space=pltpu.HBM)
SEM = pl.BlockSpec(memory_space=pltpu.SEMAPHORE)
SPLIT_COPY = pltpu.CompilerParams(has_side_effects=pltpu.SideEffectType.DATAFLOW_SIDE_EFFECTING)


def _in_hbm(a):
    return pltpu.with_memory_space_constraint(a, pltpu.HBM)


def _mesh_pos():
    return lax.axis_index("x"), lax.axis_index("y"), lax.axis_index("c")


def _window(ref, axis, start, size):
    idx = [slice(None)] * len(ref.shape)
    idx[axis] = pl.ds(start, size)
    return ref.at[tuple(idx)]


def _cast_place(key, shard, chip, *, tr=256):
    L, R, C = shard.shape
    tr = min(tr, R)
    axis = SHARD_AXIS[key]
    whole = tuple(d * (N_CHIPS if a == axis else 1) for a, d in enumerate(shard.shape))

    def body(chip_ref, s_ref, o_ref):
        o_ref[...] = s_ref[...].astype(o_ref.dtype)

    if axis == 2:
        out_map = lambda l, i, chip_ref: (l, i, chip_ref[0])
    else:
        out_map = lambda l, i, chip_ref: (l, i + chip_ref[0] * (R // tr), 0)
    grid_spec = pltpu.PrefetchScalarGridSpec(
        num_scalar_prefetch=1, grid=(L, R // tr),
        in_specs=[pl.BlockSpec((1, tr, C), lambda l, i, chip_ref: (l, i, 0))],
        out_specs=pl.BlockSpec((1, tr, C), out_map))
    return pl.pallas_call(body, name=f"cast_place_{key}", grid_spec=grid_spec, out_shape=_sds(whole, _MXU),
                          compiler_params=_params("parallel", "parallel"))(chip, shard)


def _gather_copies(names, refs, send_sems, recv_sems, *, outgoing=True, incoming=True):
    x, y, c = _mesh_pos()
    chips = [(1 - x, y), (x, 1 - y), (1 - x, 1 - y)]
    out, inc = [], []
    for t, key in enumerate(names):
        size = refs[t].shape[SHARD_AXIS[key]] // N_CHIPS
        slot = lambda px, py: _window(refs[t], SHARD_AXIS[key], pl.multiple_of((2 * px + py) * size, 128), size)
        for k, (px, py) in enumerate(chips):
            sems = dict(send_sem=send_sems.at[3 * t + k], recv_sem=recv_sems.at[3 * t + k], device_id=(px, py, c), device_id_type=MESH)
            if outgoing:
                out.append(pltpu.make_async_remote_copy(slot(x, y), slot(x, y), **sems))
            if incoming:
                inc.append(pltpu.make_async_remote_copy(slot(x, y), slot(px, py), **sems))
    return out, inc


def _allgather_start(groups):
    names = [list(g) for g in groups]
    flat = [g[k] for g in groups for k in g]
    n, ng = len(flat), len(groups)

    def body(*refs):
        start = 0
        for gi, keys in enumerate(names):
            copies, _ = _gather_copies(keys, refs[start:start + len(keys)], refs[n + 2 * gi], refs[n + 2 * gi + 1], incoming=False)
            for cp in copies:
                cp.start()
            start += len(keys)
        token = refs[-1]
        token[...] = jnp.zeros_like(token)

    sem_shapes = [pltpu.SemaphoreType.DMA((3 * len(keys),)) for keys in names for _ in (0, 1)]
    outs = pl.pallas_call(
        body, name="allgather_start", in_specs=[HBM] * n,
        out_specs=[SEM] * (2 * ng) + [HBM] * n + [pl.BlockSpec(memory_space=pltpu.VMEM)],
        out_shape=sem_shapes + [pltpu.HBM(a.shape, a.dtype) for a in flat] + [_sds((8, HEAD_DIM), F32)],
        input_output_aliases={t: 2 * ng + t for t in range(n)},
        compiler_params=SPLIT_COPY,
    )(*[_in_hbm(a) for a in flat])
    states, start = [], 2 * ng
    for gi, keys in enumerate(names):
        states.append((gi, keys, outs[2 * gi], outs[2 * gi + 1], outs[start:start + len(keys)]))
        start += len(keys)
    return states, outs[-1]


def _allgather_wait(state, after):
    gi, names, send_sems, recv_sems, thru = state
    n = len(names)

    def body(*refs):
        outgoing, incoming = _gather_copies(names, refs[:n], refs[n], refs[n + 1])
        for cp in outgoing:
            cp.wait_send()
        for cp in incoming:
            cp.wait_recv()

    outs = pl.pallas_call(
        body, name=f"allgather_wait_{gi}", in_specs=[HBM] * n + [SEM, SEM, ANY], out_specs=[HBM] * n,
        out_shape=[pltpu.HBM(t.shape, t.dtype) for t in thru],
        input_output_aliases={t: t for t in range(n)},
        compiler_params=SPLIT_COPY,
    )(*thru, send_sems, recv_sems, after)
    return dict(zip(names, outs))


FLIPS = [(a, b, d) for a in (0, 1) for b in (0, 1) for d in (0, 1) if (a, b, d) != (0, 0, 0)]


def _flip(pos, f):
    return tuple(1 - p if fi else p for p, fi in zip(pos, f))


def _piece_shape(weight, shape):
    out = list(shape)
    out[SHARD_AXIS[weight]] //= N_CHIPS
    out[1] //= 2
    return tuple(out)


def _piece(ref, weight, chip, core):
    piece = _piece_shape(weight, ref.shape)
    if SHARD_AXIS[weight] == 1:
        return _window(ref, 1, pl.multiple_of((2 * chip + core) * piece[1], 8), piece[1])
    return _window(_window(ref, 2, pl.multiple_of(chip * piece[2], 128), piece[2]), 1, pl.multiple_of(core * piece[1], 8), piece[1])


def _scatter_copies(weight, grad_ref, land_ref, send_sems, recv_sems, *, outgoing=True, incoming=True):
    pos = _mesh_pos()
    out, inc = [], []
    for k, f in enumerate(FLIPS):
        peer = _flip(pos, f)
        sems = dict(send_sem=send_sems.at[k], recv_sem=recv_sems.at[k], device_id=peer, device_id_type=MESH)
        if outgoing:
            out.append(pltpu.make_async_remote_copy(_piece(grad_ref, weight, 2 * peer[0] + peer[1], peer[2]), land_ref.at[k], **sems))
        if incoming:
            inc.append(pltpu.make_async_remote_copy(_piece(grad_ref, weight, 2 * pos[0] + pos[1], pos[2]), land_ref.at[k], **sems))
    return out, inc


def _scatter_start(tag, weight, grad):
    n_peer = len(FLIPS)
    land = lax.empty((n_peer,) + _piece_shape(weight, grad.shape), grad.dtype)

    def body(grad_ref, land_ref, send_sems, recv_sems, grad_thru, land_thru, token):
        copies, _ = _scatter_copies(weight, grad_ref, land_ref, send_sems, recv_sems, incoming=False)
        for cp in copies:
            cp.start()
        token[...] = jnp.zeros_like(token)

    outs = pl.pallas_call(
        body, name=f"scatter_start_{tag}", in_specs=[HBM, HBM],
        out_specs=[SEM, SEM, HBM, HBM, pl.BlockSpec(memory_space=pltpu.VMEM)],
        out_shape=[pltpu.SemaphoreType.DMA((n_peer,)), pltpu.SemaphoreType.DMA((n_peer,)),
                   pltpu.HBM(grad.shape, grad.dtype), pltpu.HBM(land.shape, land.dtype), _sds((8, HEAD_DIM), F32)],
        input_output_aliases={0: 2, 1: 3},
        compiler_params=SPLIT_COPY,
    )(_in_hbm(grad), _in_hbm(land))
    return (tag, weight, outs[:4]), outs[4]


def _scatter_wait(state, after):
    tag, weight, (send_sems, recv_sems, grad_thru, land_thru) = state

    def body(grad_ref, land_ref, send_ref, recv_ref, after_ref, grad_out, land_out):
        outgoing, incoming = _scatter_copies(weight, grad_ref, land_ref, send_ref, recv_ref)
        for cp in outgoing:
            cp.wait_send()
        for cp in incoming:
            cp.wait_recv()

    return pl.pallas_call(
        body, name=f"scatter_wait_{tag}", in_specs=[HBM, HBM, SEM, SEM, ANY], out_specs=[HBM, HBM],
        out_shape=[pltpu.HBM(grad_thru.shape, grad_thru.dtype), pltpu.HBM(land_thru.shape, land_thru.dtype)],
        input_output_aliases={0: 0, 1: 1},
        compiler_params=SPLIT_COPY,
    )(grad_thru, land_thru, send_sems, recv_sems, after)


def _sum_pieces(tag, weight, grad, land, where, *, tr=128):
    _, R, C = _piece_shape(weight, grad.shape)
    tr = min(tr, R)
    nr = R // tr

    def body(where_ref, g_ref, l_ref, o_ref):
        acc = g_ref[...].astype(F32)
        for s in range(len(FLIPS)):
            acc = acc + l_ref[s].astype(F32)
        o_ref[...] = acc

    if SHARD_AXIS[weight] == 1:
        own = lambda i, where_ref: (0, (2 * where_ref[0] + where_ref[1]) * nr + i, 0)
    else:
        own = lambda i, where_ref: (0, where_ref[1] * nr + i, where_ref[0])
    grid_spec = pltpu.PrefetchScalarGridSpec(
        num_scalar_prefetch=1, grid=(nr,),
        in_specs=[pl.BlockSpec((1, tr, C), own), pl.BlockSpec((len(FLIPS), 1, tr, C), lambda i, where_ref: (0, 0, i, 0))],
        out_specs=pl.BlockSpec((1, tr, C), lambda i, where_ref: (0, where_ref[1] * nr + i, 0)))
    return pl.pallas_call(body, name=f"sum_{tag}", grid_spec=grid_spec, out_shape=_sds((1, 2 * R, C), F32),
                          compiler_params=_params("parallel"))(where, grad, land)


def _exchange_halves(shards):
    names = list(shards)
    n = len(names)
    half_sizes = [shards[k].shape[1] // 2 for k in names]

    def body(*refs):
        outs = refs[n:2 * n]
        send_sems, recv_sems = refs[2 * n:]
        x, y, c = _mesh_pos()
        half = lambda t, core: _window(outs[t], 1, pl.multiple_of(core * half_sizes[t], 8), half_sizes[t])
        sends = []
        for t in range(n):
            sends.append(pltpu.make_async_remote_copy(half(t, c), half(t, c), send_sems.at[t], recv_sems.at[t],
                                                      device_id=(x, y, 1 - c), device_id_type=MESH))
            sends[-1].start()
        for t in range(n):
            pltpu.make_async_remote_copy(half(t, c), half(t, 1 - c), send_sems.at[t], recv_sems.at[t],
                                         device_id=(x, y, 1 - c), device_id_type=MESH).wait_recv()
        for cp in sends:
            cp.wait_send()

    outs = pl.pallas_call(
        body, name="exchange_halves", in_specs=[ANY] * n, out_specs=[ANY] * n,
        out_shape=[_sds(shards[k].shape, F32) for k in names],
        input_output_aliases={t: t for t in range(n)},
        scratch_shapes=[pltpu.SemaphoreType.DMA((n,)), pltpu.SemaphoreType.DMA((n,))],
    )(*[shards[k] for k in names])
    return dict(zip(names, outs))


def _adamw_math(w, g, m, v):
    m = ADAM_B1 * m + (1.0 - ADAM_B1) * g
    v = ADAM_B2 * v + (1.0 - ADAM_B2) * jnp.square(g)
    m_hat = m / (1.0 - ADAM_B1 ** ADAM_STEP)
    v_hat = v / (1.0 - ADAM_B2 ** ADAM_STEP)
    return -ADAM_LR * (m_hat / (jnp.sqrt(v_hat) + ADAM_EPS) + ADAM_WD * w), m, v


def _adamw(name, w, g, m, v, *, tr=256):
    R, C = w.shape
    tr = min(tr, R)

    def body(w_ref, g_ref, m_ref, v_ref, d_ref, mo_ref, vo_ref):
        d_ref[...], mo_ref[...], vo_ref[...] = _adamw_math(w_ref[...], g_ref[...], m_ref[...], v_ref[...])

    spec = pl.BlockSpec((tr, C), lambda i: (i, 0))
    return pl.pallas_call(body, name=name, grid=(R // tr,), in_specs=[spec] * 4, out_specs=[spec] * 3,
                          out_shape=[_sds((R, C), F32)] * 3, compiler_params=_params("parallel"))(w, g, m, v)


SLAB_ROWS = 8
LOSS_ROW = 7


def _pack_small(d):
    pad = lambda a, width: jnp.pad(a.reshape(-1), (0, width - a.size))
    row5 = jnp.concatenate([d["swa_q_norm"].reshape(-1), d["swa_k_norm"].reshape(-1), d["ax_q_norm"].reshape(-1),
                            d["ax_k_norm"].reshape(-1), pad(d["swa_sink"], HEAD_DIM), pad(d["ret_decay_logit"], HEAD_DIM),
                            jnp.zeros((2 * HEAD_DIM,), F32)])
    return jnp.concatenate([d["norm_mix"], d["norm_mlp"], d["ret_norm"], row5[None], pad(d["t5_table"], D_MODEL)[None],
                            jnp.zeros((1, D_MODEL), F32)], axis=0)


def _unpack_small(slab):
    r5 = slab[5]
    return {
        "norm_mix": slab[0:2], "norm_mlp": slab[2:4], "ret_norm": slab[4:5],
        "swa_q_norm": r5[None, 0:128], "swa_k_norm": r5[None, 128:256], "ax_q_norm": r5[None, 256:384],
        "ax_k_norm": r5[None, 384:512], "swa_sink": r5[None, 512:512 + SWA_HEADS],
        "ret_decay_logit": r5[640:640 + 2 * RET_HEADS].reshape(1, 2, RET_HEADS),
        "t5_table": slab[6, :T5_BUCKETS * SWA_HEADS].reshape(T5_BUCKETS, SWA_HEADS),
    }


def _small_allreduce_adamw(g_slab, w_slab, m_slab, v_slab, loss_part):
    def body(g_ref, w_ref, m_ref, v_ref, lp_ref, go_ref, d_ref, mo_ref, vo_ref, gath, send_sems, recv_sems):
        pos = _mesh_pos()
        ident = lambda p: 4 * p[0] + 2 * p[1] + p[2]
        me = ident(pos)
        row = lax.broadcasted_iota(jnp.int32, (SLAB_ROWS, D_MODEL), 0)
        lane = lax.broadcasted_iota(jnp.int32, (SLAB_ROWS, D_MODEL), 1)
        loss = jnp.sum(jnp.sum(lp_ref[...], axis=0, keepdims=True), axis=1, keepdims=True) * (0.5 / D_MODEL)
        gath[me] = jnp.where(jnp.logical_and(row == LOSS_ROW, lane == 0), loss, g_ref[...])
        sends = []
        for k, f in enumerate(FLIPS):
            sends.append(pltpu.make_async_remote_copy(gath.at[me], gath.at[me], send_sems.at[k], recv_sems.at[k],
                                                      device_id=_flip(pos, f), device_id_type=MESH))
            sends[-1].start()
        for k, f in enumerate(FLIPS):
            peer = _flip(pos, f)
            pltpu.make_async_remote_copy(gath.at[me], gath.at[ident(peer)], send_sems.at[k], recv_sems.at[k],
                                         device_id=peer, device_id_type=MESH).wait_recv()
        for cp in sends:
            cp.wait_send()
        total = gath[0]
        for s in range(1, N_DEV):
            total = total + gath[s]
        go_ref[...] = total
        d_ref[...], mo_ref[...], vo_ref[...] = _adamw_math(w_ref[...], total, m_ref[...], v_ref[...])

    vmem = pl.BlockSpec(memory_space=pltpu.VMEM)
    return pl.pallas_call(
        body, name="small_allreduce_adamw", in_specs=[vmem] * 5, out_specs=[vmem] * 4,
        out_shape=[_sds((SLAB_ROWS, D_MODEL), F32)] * 4,
        scratch_shapes=[pltpu.VMEM((N_DEV, SLAB_ROWS, D_MODEL), F32),
                        pltpu.SemaphoreType.DMA((len(FLIPS),)), pltpu.SemaphoreType.DMA((len(FLIPS),))],
    )(g_slab, w_slab, m_slab, v_slab, loss_part)


def kernel(x, norm_mix, norm_mlp, w_in_even, w_out_even, ret_decay_logit, ret_norm, swa_q_norm, swa_k_norm, swa_sink, t5_table, w_in_odd, w_out_odd, ax_q_norm, ax_k_norm, w_mlp_up, w_mlp_down, loss_target, m_norm_mix, m_norm_mlp, m_w_in_even, m_w_out_even, m_ret_decay_logit, m_ret_norm, m_swa_q_norm, m_swa_k_norm, m_swa_sink, m_t5_table, m_w_in_odd, m_w_out_odd, m_ax_q_norm, m_ax_k_norm, m_w_mlp_up, m_w_mlp_down, v_norm_mix, v_norm_mlp, v_w_in_even, v_w_out_even, v_ret_decay_logit, v_ret_norm, v_swa_q_norm, v_swa_k_norm, v_swa_sink, v_t5_table, v_w_in_odd, v_w_out_odd, v_ax_q_norm, v_ax_k_norm, v_w_mlp_up, v_w_mlp_down):
    w = dict(zip(WEIGHTS, (norm_mix, norm_mlp, w_in_even, w_out_even, ret_decay_logit, ret_norm, swa_q_norm, swa_k_norm,
                           swa_sink, t5_table, w_in_odd, w_out_odd, ax_q_norm, ax_k_norm, w_mlp_up, w_mlp_down)))
    m = dict(zip(WEIGHTS, (m_norm_mix, m_norm_mlp, m_w_in_even, m_w_out_even, m_ret_decay_logit, m_ret_norm, m_swa_q_norm,
                           m_swa_k_norm, m_swa_sink, m_t5_table, m_w_in_odd, m_w_out_odd, m_ax_q_norm, m_ax_k_norm,
                           m_w_mlp_up, m_w_mlp_down)))
    v = dict(zip(WEIGHTS, (v_norm_mix, v_norm_mlp, v_w_in_even, v_w_out_even, v_ret_decay_logit, v_ret_norm, v_swa_q_norm,
                           v_swa_k_norm, v_swa_sink, v_t5_table, v_w_in_odd, v_w_out_odd, v_ax_q_norm, v_ax_k_norm,
                           v_w_mlp_up, v_w_mlp_down)))
    flat = lambda a: a.reshape(-1, a.shape[-1])

    chip = (2 * lax.axis_index("x") + lax.axis_index("y")).astype(jnp.int32)
    where = jnp.stack([chip, lax.axis_index("c").astype(jnp.int32)])

    placed = {k: _cast_place(k, w[k], where[0:1]) for k in BIG}
    (gather_first, gather_rest), gather_token = _allgather_start(
        [{"w_in_even": placed["w_in_even"]}, {k: placed[k] for k in BIG if k != "w_in_even"}])
    unstack = lambda whole: {k: (a if k.startswith("w_mlp") else a[0]) for k, a in whole.items()}

    in_flight = []

    def push(tag, weight, dw):
        state, token = _scatter_start(tag, weight, dw)
        in_flight.append(state)
        return token

    loss_part, dx, small_g = _local_step(x[0], loss_target[0], {k: w[k] for k in SMALL},
                                         unstack(_allgather_wait(gather_first, gather_token)),
                                         lambda after: unstack(_allgather_wait(gather_rest, after)), push)

    halves = {}
    for state in in_flight:
        tag, weight = state[0], state[1]
        dw, land = _scatter_wait(state, dx)
        halves[tag] = _sum_pieces(tag, weight, dw, land, where)
    reduced = _exchange_halves(halves)
    grad = {"w_in_even": reduced["in_even"], "w_out_even": reduced["out_even"],
            "w_in_odd": reduced["in_odd"], "w_out_odd": reduced["out_odd"],
            "w_mlp_up": jnp.concatenate([reduced["mlp_up0"], reduced["mlp_up1"]], axis=0),
            "w_mlp_down": jnp.concatenate([reduced["mlp_down0"], reduced["mlp_down1"]], axis=0)}
    delta, new_m, new_v = {}, {}, {}
    for k in BIG:
        d_k, m_k, v_k = _adamw(f"adamw_{k}", flat(w[k]), flat(grad[k]), flat(m[k]), flat(v[k]))
        delta[k], new_m[k], new_v[k] = d_k.reshape(w[k].shape), m_k.reshape(w[k].shape), v_k.reshape(w[k].shape)

    slabs = _small_allreduce_adamw(_pack_small(small_g), _pack_small({k: w[k] for k in SMALL}),
                                   _pack_small({k: m[k] for k in SMALL}), _pack_small({k: v[k] for k in SMALL}), loss_part)
    loss = slabs[0][LOSS_ROW, 0]
    for out, slab in zip((grad, delta, new_m, new_v), slabs):
        out.update(_unpack_small(slab))

    return (loss, dx[None], *[grad[k] for k in WEIGHTS], *[delta[k] for k in WEIGHTS],
            *[new_m[k] for k in WEIGHTS], *[new_v[k] for k in WEIGHTS])
```

```python
import functools
import math

import jax
import jax.numpy as jnp
from jax import lax
from jax.experimental import pallas as pl
from jax.experimental.pallas import tpu as pltpu

F32 = jnp.float32
BF16 = jnp.bfloat16
_MXU = BF16
_WIRE = BF16

D_MODEL = 1024
HEAD_DIM = 128
EPS = 1e-6
NEG_INF = -1e30
CHUNK = 128
RET_CHUNKS_PER_STEP = 4
GRID_W = 64
RET_HEADS, RET_DK, RET_DV = 4, 128, 256
RET_Q, RET_V = RET_HEADS * RET_DK, RET_HEADS * RET_DV
RET_THETA = 10000.0
SWA_HEADS, SWA_KV_HEADS = 8, 2
T5_BUCKETS, T5_MAX_DIST = 32, 128
AX_HEADS, AX_KV_HEADS = 8, 2
AX_THETA = 10000.0
D_FF = 4 * D_MODEL
EVEN_IN = 2 * RET_Q + 2 * RET_V + D_MODEL + 2 * SWA_KV_HEADS * HEAD_DIM
ODD_IN = D_MODEL + 2 * AX_KV_HEADS * HEAD_DIM
ATT_SCALE = HEAD_DIM ** -0.5
SCORE_SCALE_LOG2 = ATT_SCALE * math.log2(math.e)

ADAM_LR, ADAM_B1, ADAM_B2, ADAM_EPS, ADAM_WD, ADAM_STEP = 0.001, 0.9, 0.999, 1e-08, 0.01, 10

N_DEV = 8
VMEM_LIMIT_BYTES = 56 << 20
MESH = pl.DeviceIdType.MESH

_NN = (((1,), (0,)), ((), ()))
_NT = (((1,), (1,)), ((), ()))
_TN = (((0,), (0,)), ((), ()))


def _dot(a, b, dn=_NN):
    return lax.dot_general(a.astype(_MXU), b.astype(_MXU), dn, preferred_element_type=F32)


def _params(*sem):
    return pltpu.CompilerParams(dimension_semantics=sem, vmem_limit_bytes=VMEM_LIMIT_BYTES)


def _sds(shape, dtype):
    return jax.ShapeDtypeStruct(tuple(shape), dtype)


def _rowsum8(x):
    return jnp.sum(x.reshape(x.shape[0] // 8, 8, x.shape[1]), axis=0)


def _swap_halves(x, half):
    width = x.shape[1]
    lane = lax.broadcasted_iota(jnp.int32, x.shape, 1)
    up = pltpu.roll(x, width - half, axis=1)
    down = pltpu.roll(x, half, axis=1)
    return jnp.where((lane & (2 * half - 1)) < half, up, down)


def _sigmoid(x):
    return 1.0 / (1.0 + jnp.exp(-x))


def _norm_matmul(name, x, gain, w, *, tm, tn, out_dtype):
    T, K = x.shape
    N = w.shape[1]
    tm, tn = min(tm, T), min(tn, N)

    def body(x_ref, g_ref, w_ref, y_ref, h_ref, h_sc):
        @pl.when(pl.program_id(1) == 0)
        def _():
            xv = x_ref[...]
            r = lax.rsqrt(jnp.mean(xv * xv, axis=-1, keepdims=True) + EPS)
            h = (xv * r * g_ref[...]).astype(_MXU)
            h_sc[...] = h
            h_ref[...] = h
        y_ref[...] = jnp.dot(h_sc[...], w_ref[...], preferred_element_type=F32).astype(y_ref.dtype)

    return pl.pallas_call(
        body, name=name, grid=(T // tm, N // tn),
        in_specs=[pl.BlockSpec((tm, K), lambda i, j: (i, 0)),
                  pl.BlockSpec((1, K), lambda i, j: (0, 0)),
                  pl.BlockSpec((K, tn), lambda i, j: (0, j))],
        out_specs=[pl.BlockSpec((tm, tn), lambda i, j: (i, j)),
                   pl.BlockSpec((tm, K), lambda i, j: (i, 0))],
        out_shape=[_sds((T, N), out_dtype), _sds((T, K), _MXU)],
        scratch_shapes=[pltpu.VMEM((tm, K), _MXU)],
        compiler_params=_params("parallel", "arbitrary"),
    )(x, gain, w)


def _matmul_res(name, a_list, w, res, *, tm, relu2=False, target=None):
    T = res.shape[0]
    N = w.shape[1]
    K = a_list[0].shape[1]
    n_a = len(a_list)
    tm = min(tm, T)
    with_loss = target is not None

    def body(*refs):
        a_refs = refs[:n_a]
        w_refs = refs[n_a:2 * n_a]
        res_ref = refs[2 * n_a]
        acc = res_ref[...]
        for a_ref, w_ref in zip(a_refs, w_refs):
            a = a_ref[...]
            if relu2:
                a = jnp.square(jnp.maximum(a.astype(F32), 0.0))
            acc = acc + _dot(a, w_ref[...])
        if with_loss:
            tgt_ref, g_ref, g16_ref, loss_ref = refs[2 * n_a + 1:]
            diff = acc - tgt_ref[...]
            g = diff * (1.0 / N)
            g_ref[...] = g
            g16_ref[...] = g.astype(g16_ref.dtype)

            @pl.when(pl.program_id(0) == 0)
            def _():
                loss_ref[...] = jnp.zeros_like(loss_ref)
            loss_ref[...] += _rowsum8(diff * diff)
        else:
            refs[2 * n_a + 1][...] = acc

    row = lambda i: (i, 0)
    in_specs = [pl.BlockSpec((tm, K), row) for _ in a_list]
    in_specs += [pl.BlockSpec((K, N), functools.partial(lambda i, b: (b, 0), b=b)) for b in range(n_a)]
    in_specs += [pl.BlockSpec((tm, N), row)]
    args = list(a_list) + [w] * n_a + [res]
    if with_loss:
        in_specs.append(pl.BlockSpec((tm, N), row))
        args.append(target)
        out_specs = [pl.BlockSpec((tm, N), row), pl.BlockSpec((tm, N), row), pl.BlockSpec((8, N), lambda i: (0, 0))]
        out_shape = [_sds((T, N), F32), _sds((T, N), _MXU), _sds((8, N), F32)]
        sem = "arbitrary"
    else:
        out_specs = pl.BlockSpec((tm, N), row)
        out_shape = _sds((T, N), F32)
        sem = "parallel"
    return pl.pallas_call(body, name=name, grid=(T // tm,), in_specs=in_specs, out_specs=out_specs,
                          out_shape=out_shape, compiler_params=_params(sem))(*args)


def _matmul_nt(name, a, w, *, tm, tn, out_dtype, relu_of=None):
    T, K = a.shape
    N = w.shape[0]
    tm, tn = min(tm, T), min(tn, N)

    def body(*refs):
        if relu_of is None:
            a_ref, w_ref, o_ref = refs
            o_ref[...] = _dot(a_ref[...], w_ref[...], _NT).astype(o_ref.dtype)
        else:
            a_ref, w_ref, u_ref, o_ref = refs
            da = _dot(a_ref[...], w_ref[...], _NT)
            o_ref[...] = (da * (2.0 * jnp.maximum(u_ref[...].astype(F32), 0.0))).astype(o_ref.dtype)

    in_specs = [pl.BlockSpec((tm, K), lambda i, j: (i, 0)), pl.BlockSpec((tn, K), lambda i, j: (j, 0))]
    args = [a, w]
    if relu_of is not None:
        in_specs.append(pl.BlockSpec((tm, tn), lambda i, j: (i, j)))
        args.append(relu_of)
    return pl.pallas_call(body, name=name, grid=(T // tm, N // tn), in_specs=in_specs,
                          out_specs=pl.BlockSpec((tm, tn), lambda i, j: (i, j)),
                          out_shape=_sds((T, N), out_dtype),
                          compiler_params=_params("parallel", "parallel"))(*args)


def _matmul_nt_normbwd(name, dy, w, x, gain, dres, *, tm):
    T, K = dy.shape
    N = w.shape[0]
    tm = min(tm, T)

    def body(dy_ref, w_ref, x_ref, g_ref, dres_ref, dx_ref, dx16_ref, dg_ref):
        dh = _dot(dy_ref[...], w_ref[...], _NT)
        xv = x_ref[...]
        r = lax.rsqrt(jnp.mean(xv * xv, axis=-1, keepdims=True) + EPS)
        xhat = xv * r
        dxhat = dh * g_ref[...]
        dx = dres_ref[...] + r * (dxhat - xhat * jnp.mean(dxhat * xhat, axis=-1, keepdims=True))
        dx_ref[...] = dx
        dx16_ref[...] = dx.astype(dx16_ref.dtype)

        @pl.when(pl.program_id(0) == 0)
        def _():
            dg_ref[...] = jnp.zeros_like(dg_ref)
        dg_ref[...] += _rowsum8(dh * xhat)

    row = lambda i: (i, 0)
    return pl.pallas_call(
        body, name=name, grid=(T // tm,),
        in_specs=[pl.BlockSpec((tm, K), row), pl.BlockSpec((N, K), lambda i: (0, 0)),
                  pl.BlockSpec((tm, N), row), pl.BlockSpec((1, N), lambda i: (0, 0)), pl.BlockSpec((tm, N), row)],
        out_specs=[pl.BlockSpec((tm, N), row), pl.BlockSpec((tm, N), row), pl.BlockSpec((8, N), lambda i: (0, 0))],
        out_shape=[_sds((T, N), F32), _sds((T, N), _MXU), _sds((8, N), F32)],
        compiler_params=_params("arbitrary"),
    )(dy, w, x, gain, dres)


def _matmul_tn(name, a, b, *, tk, tn, tt, out_dtype, relu2=False):
    T, Ka = a.shape
    Nb = b.shape[1]
    tk, tn, tt = min(tk, Ka), min(tn, Nb), min(tt, T)
    nt = T // tt

    def body(a_ref, b_ref, o_ref, acc):
        t = pl.program_id(2)

        @pl.when(t == 0)
        def _():
            acc[...] = jnp.zeros_like(acc)
        av = a_ref[...]
        if relu2:
            av = jnp.square(jnp.maximum(av.astype(F32), 0.0))
        acc[...] += _dot(av, b_ref[...], _TN)

        @pl.when(t == nt - 1)
        def _():
            o_ref[...] = acc[...].astype(o_ref.dtype)

    return pl.pallas_call(
        body, name=name, grid=(Ka // tk, Nb // tn, nt),
        in_specs=[pl.BlockSpec((tt, tk), lambda i, j, t: (t, i)), pl.BlockSpec((tt, tn), lambda i, j, t: (t, j))],
        out_specs=pl.BlockSpec((tk, tn), lambda i, j, t: (i, j)),
        out_shape=_sds((Ka, Nb), out_dtype),
        scratch_shapes=[pltpu.VMEM((tk, tn), F32)],
        compiler_params=_params("parallel", "parallel", "arbitrary"),
    )(a, b)


def _rope_angles(pos, dim, theta):
    inv = theta ** (-jnp.arange(0, dim, 2, dtype=F32) / dim)
    return pos.astype(F32)[:, None] * inv[None, :]


def _ret_rope_tables(T):
    ang = _rope_angles(jnp.arange(T), RET_DK, RET_THETA)
    c, s = jnp.cos(ang), jnp.sin(ang)
    return jnp.concatenate([c, c], axis=1), jnp.concatenate([-s, s], axis=1)


def _axial_rope_tables(T):
    rows = T // GRID_W
    ar = _rope_angles(jnp.arange(rows), HEAD_DIM // 2, AX_THETA)
    ac = _rope_angles(jnp.arange(GRID_W), HEAD_DIM // 2, AX_THETA)
    by_row = lambda a: jnp.repeat(a, GRID_W, axis=0)
    by_col = lambda a: jnp.tile(a, (rows, 1))
    cos = jnp.concatenate([by_row(jnp.cos(ar)), by_row(jnp.cos(ar)), by_col(jnp.cos(ac)), by_col(jnp.cos(ac))], axis=1)
    sin = jnp.concatenate([by_row(-jnp.sin(ar)), by_row(jnp.sin(ar)), by_col(-jnp.sin(ac)), by_col(jnp.sin(ac))], axis=1)
    return cos, sin


(TAB_D, TAB_DT, TAB_EF, TAB_EB, TAB_A, TAB_B, TAB_CF, TAB_CB,
 TAB_RA, TAB_RB, TAB_RCF, TAB_RCB, TAB_KF, TAB_KB) = range(14)


def _retention_tables(decay_logit):
    lg = jax.nn.log_sigmoid(decay_logit.astype(F32))
    lam, mu = lg[0][:, None, None], lg[1][:, None, None]
    idx = jnp.arange(CHUNK, dtype=F32)
    diff = (idx[:, None] - idx[None, :])[None]
    df = jnp.where(diff >= 0, jnp.exp(jnp.maximum(diff, 0.0) * lam), 0.0)
    db = jnp.where(diff < 0, jnp.exp(jnp.maximum(-diff, 0.0) * mu), 0.0)
    d = df + db
    r = idx[None, :, None]
    ones = jnp.ones((1, 1, CHUNK), F32)
    a = jnp.exp((r + 1.0) * lam) * ones
    b = jnp.exp((CHUNK - r) * mu) * ones
    cf = jnp.exp((CHUNK - 1.0 - r) * lam) * ones
    cb = jnp.exp(r * mu) * ones
    full = jnp.ones((1, CHUNK, CHUNK), F32)
    kf = CHUNK * jnp.exp(CHUNK * lam) * full
    kb = CHUNK * jnp.exp(CHUNK * mu) * full
    tabs = jnp.stack([d, jnp.swapaxes(d, 1, 2), diff * df, -diff * db, a, b, cf, cb,
                      (r + 1.0) * a, (CHUNK - r) * b, (CHUNK - 1.0 - r) * cf, r * cb, kf, kb], axis=1)

    def lanes(tab):
        return jnp.transpose(tab, (1, 0, 2)).reshape(CHUNK, RET_HEADS * CHUNK)

    def dec(l):
        return jnp.exp(CHUNK * l)[:, 0, :] * jnp.ones((1, RET_DV), F32)

    weights = dict(a=lanes(a), b=lanes(b), cf=lanes(cf), cb=lanes(cb), dec_f=dec(lam), dec_b=dec(mu))
    return tabs, weights, lg


def _t5_bucket(rel):
    nb = T5_BUCKETS // 2
    max_exact = nb // 2
    ret = jnp.where(rel > 0, nb, 0)
    n = jnp.abs(rel)
    nf = jnp.maximum(n, 1).astype(F32)
    large = max_exact + (jnp.log(nf / max_exact) / math.log(T5_MAX_DIST / max_exact)
                         * (nb - max_exact)).astype(jnp.int32)
    large = jnp.minimum(large, nb - 1)
    return ret + jnp.where(n < max_exact, n, large)


def _swa_rel():
    r = jnp.arange(CHUNK)
    j = jnp.arange(3 * CHUNK)
    return j[None, :] - CHUNK - r[:, None]


def _swa_bias(t5_table):
    rel = _swa_rel()
    bucket = jnp.where(jnp.abs(rel) <= CHUNK, _t5_bucket(rel), -1).astype(jnp.int32)

    def body(tab_ref, bk_ref, o_ref):
        bk = bk_ref[...]
        for h in range(SWA_HEADS):
            pick = lambda b, acc, h=h: jnp.where(bk == b, tab_ref[b, h], acc)
            o_ref[h] = lax.fori_loop(0, T5_BUCKETS, pick, jnp.full(bk.shape, NEG_INF, F32))

    return pl.pallas_call(
        body, name="t5_bias",
        in_specs=[pl.BlockSpec(memory_space=pltpu.SMEM), pl.BlockSpec(memory_space=pltpu.VMEM)],
        out_specs=pl.BlockSpec(memory_space=pltpu.VMEM),
        out_shape=_sds((SWA_HEADS, CHUNK, 3 * CHUNK), F32),
    )(t5_table.astype(F32), bucket)


def _prep_even(proj, cos, sin, q_gain, k_gain, *, tm):
    T = proj.shape[0]
    tm = min(tm, T)

    def body(qa_ref, ka_ref, qb_ref, kb_ref, c_ref, s_ref, qg_ref, kg_ref, qr_ref, kr_ref, qn_ref, kn_ref):
        c = jnp.concatenate([c_ref[...]] * RET_HEADS, axis=1)
        s = jnp.concatenate([s_ref[...]] * RET_HEADS, axis=1)
        qa = qa_ref[...]
        qr_ref[...] = (qa * c + _swap_halves(qa, RET_DK // 2) * s).astype(qr_ref.dtype)
        ka = ka_ref[...]
        kr_ref[...] = ((ka * c + _swap_halves(ka, RET_DK // 2) * s) * (RET_DK ** -0.5)).astype(kr_ref.dtype)
        for src, gain, dst, heads in ((qb_ref, qg_ref, qn_ref, SWA_HEADS), (kb_ref, kg_ref, kn_ref, SWA_KV_HEADS)):
            for h in range(heads):
                sl = slice(h * HEAD_DIM, (h + 1) * HEAD_DIM)
                xh = src[:, sl]
                r = lax.rsqrt(jnp.mean(xh * xh, axis=-1, keepdims=True) + EPS)
                dst[:, sl] = (xh * r * gain[...]).astype(dst.dtype)

    row = lambda i: (i, 0)
    const = lambda i: (0, 0)
    return pl.pallas_call(
        body, name="prep_even", grid=(T // tm,),
        in_specs=[pl.BlockSpec((tm, RET_Q), lambda i: (i, 0)), pl.BlockSpec((tm, RET_Q), lambda i: (i, 1)),
                  pl.BlockSpec((tm, D_MODEL), lambda i: (i, 3)), pl.BlockSpec((tm, 256), lambda i: (i, 16)),
                  pl.BlockSpec((tm, RET_DK), row), pl.BlockSpec((tm, RET_DK), row),
                  pl.BlockSpec((1, HEAD_DIM), const), pl.BlockSpec((1, HEAD_DIM), const)],
        out_specs=[pl.BlockSpec((tm, RET_Q), row), pl.BlockSpec((tm, RET_Q), row),
                   pl.BlockSpec((tm, D_MODEL), row), pl.BlockSpec((tm, 256), row)],
        out_shape=[_sds((T, RET_Q), _MXU), _sds((T, RET_Q), _MXU), _sds((T, D_MODEL), _MXU), _sds((T, 256), _MXU)],
        compiler_params=_params("parallel"),
    )(proj, proj, proj, proj, cos, sin, q_gain, k_gain)


def _ret_scan(name, x, y, y_col, w_asc, dec_asc, w_desc, dec_desc):
    T = x.shape[0]
    nc = T // CHUNK
    per = min(RET_CHUNKS_PER_STEP, nc)
    nb = nc // per
    rows_per = per * CHUNK

    def body(xa_ref, ya_ref, xd_ref, yd_ref, wa_ref, da_ref, wd_ref, dd_ref, sa_out, sd_out, sa, sd):
        @pl.when(pl.program_id(0) == 0)
        def _():
            sa[...] = jnp.zeros_like(sa)
            sd[...] = jnp.zeros_like(sd)
        for step in range(per):
            for c, x_ref, y_ref, w_ref, d_ref, st, out in ((step, xa_ref, ya_ref, wa_ref, da_ref, sa, sa_out),
                                                       (per - 1 - step, xd_ref, yd_ref, wd_ref, dd_ref, sd, sd_out)):
                rows = slice(c * CHUNK, (c + 1) * CHUNK)
                out[c] = st[...].astype(out.dtype)
                for h in range(RET_HEADS):
                    ks = slice(h * RET_DK, (h + 1) * RET_DK)
                    vs = slice(h * RET_DV, (h + 1) * RET_DV)
                    u = _dot(x_ref[rows, ks].astype(F32) * w_ref[:, ks], y_ref[rows, vs], _TN)
                    st[ks, :] = st[ks, :] * d_ref[h:h + 1, :] + u

    asc = lambda i: (i, 0)
    desc = lambda i: (nb - 1 - i, 0)
    const = lambda i: (0, 0)
    return pl.pallas_call(
        body, name=name, grid=(nb,),
        in_specs=[pl.BlockSpec((rows_per, RET_Q), asc), pl.BlockSpec((rows_per, RET_V), lambda i: (i, y_col)),
                  pl.BlockSpec((rows_per, RET_Q), desc), pl.BlockSpec((rows_per, RET_V), lambda i: (nb - 1 - i, y_col)),
                  pl.BlockSpec((CHUNK, RET_Q), const), pl.BlockSpec((RET_HEADS, RET_DV), const),
                  pl.BlockSpec((CHUNK, RET_Q), const), pl.BlockSpec((RET_HEADS, RET_DV), const)],
        out_specs=[pl.BlockSpec((per, RET_Q, RET_DV), lambda i: (i, 0, 0)),
                   pl.BlockSpec((per, RET_Q, RET_DV), lambda i: (nb - 1 - i, 0, 0))],
        out_shape=[_sds((nc, RET_Q, RET_DV), _MXU), _sds((nc, RET_Q, RET_DV), _MXU)],
        scratch_shapes=[pltpu.VMEM((RET_Q, RET_DV), F32), pltpu.VMEM((RET_Q, RET_DV), F32)],
        compiler_params=_params("arbitrary"),
    )(x, y, x, y, w_asc, dec_asc, w_desc, dec_desc)


def _ret_out(qr, kr, proj, sf, sb, tabs, gain):
    T = qr.shape[0]
    nc = T // CHUNK
    per = min(RET_CHUNKS_PER_STEP, nc)
    rows_per = per * CHUNK

    def body(q_ref, k_ref, v_ref, g_ref, sf_ref, sb_ref, tab_ref, gain_ref, o_ref, y_ref):
        for c in range(per):
            rows = slice(c * CHUNK, (c + 1) * CHUNK)
            for h in range(RET_HEADS):
                ks = slice(h * RET_DK, (h + 1) * RET_DK)
                vs = slice(h * RET_DV, (h + 1) * RET_DV)
                q, k, v = q_ref[rows, ks], k_ref[rows, ks], v_ref[rows, vs]
                qf = q.astype(F32)
                a_mat = _dot(q, k, _NT) * tab_ref[h, 0]
                o = (_dot(a_mat, v) + _dot(qf * tab_ref[h, 1], sf_ref[c, ks, :]) + _dot(qf * tab_ref[h, 2], sb_ref[c, ks, :]))
                o_ref[rows, vs] = o
                r = lax.rsqrt(jnp.mean(o * o, axis=-1, keepdims=True) + EPS)
                g = g_ref[rows, vs]
                y_ref[rows, vs] = (g * _sigmoid(g) * (o * r * gain_ref[:, vs])).astype(y_ref.dtype)

    row = lambda i: (i, 0)
    st = lambda i: (i, 0, 0)
    return pl.pallas_call(
        body, name="ret_out", grid=(nc // per,),
        in_specs=[pl.BlockSpec((rows_per, RET_Q), row), pl.BlockSpec((rows_per, RET_Q), row),
                  pl.BlockSpec((rows_per, RET_V), lambda i: (i, 1)), pl.BlockSpec((rows_per, RET_V), lambda i: (i, 2)),
                  pl.BlockSpec((per, RET_Q, RET_DV), st), pl.BlockSpec((per, RET_Q, RET_DV), st),
                  pl.BlockSpec((RET_HEADS, 3, CHUNK, CHUNK), lambda i: (0, 0, 0, 0)),
                  pl.BlockSpec((1, RET_V), lambda i: (0, 0))],
        out_specs=[pl.BlockSpec((rows_per, RET_V), row), pl.BlockSpec((rows_per, RET_V), row)],
        out_shape=[_sds((T, RET_V), F32), _sds((T, RET_V), _MXU)],
        compiler_params=_params("parallel"),
    )(qr, kr, proj, proj, sf, sb, tabs, gain)


def _ret_gate_bwd(dycat, proj, ret_o, gain, *, tm):
    T = ret_o.shape[0]
    tm = min(tm, T)

    def body(dy_ref, g_ref, o_ref, gain_ref, do_ref, dg_ref, dgain_ref):
        @pl.when(pl.program_id(0) == 0)
        def _():
            dgain_ref[...] = jnp.zeros_like(dgain_ref)
        for h in range(RET_HEADS):
            vs = slice(h * RET_DV, (h + 1) * RET_DV)
            o, g, dya, gn = o_ref[:, vs], g_ref[:, vs], dy_ref[:, vs], gain_ref[:, vs]
            r = lax.rsqrt(jnp.mean(o * o, axis=-1, keepdims=True) + EPS)
            ohat = o * r
            sg = _sigmoid(g)
            dy = dya * (g * sg)
            dg_ref[:, vs] = (dya * (ohat * gn) * (sg * (1.0 + g * (1.0 - sg)))).astype(dg_ref.dtype)
            dyg = dy * gn
            do_ref[:, vs] = (r * (dyg - ohat * jnp.mean(dyg * ohat, axis=-1, keepdims=True))).astype(do_ref.dtype)
            dgain_ref[:, vs] += _rowsum8(dy * ohat)

    row = lambda i: (i, 0)
    return pl.pallas_call(
        body, name="ret_gate_bwd", grid=(T // tm,),
        in_specs=[pl.BlockSpec((tm, RET_V), row), pl.BlockSpec((tm, RET_V), lambda i: (i, 2)),
                  pl.BlockSpec((tm, RET_V), row), pl.BlockSpec((1, RET_V), lambda i: (0, 0))],
        out_specs=[pl.BlockSpec((tm, RET_V), row), pl.BlockSpec((tm, RET_V), row), pl.BlockSpec((8, RET_V), lambda i: (0, 0))],
        out_shape=[_sds((T, RET_V), _MXU), _sds((T, RET_V), _MXU), _sds((8, RET_V), F32)],
        compiler_params=_params("arbitrary"),
    )(dycat, proj, ret_o, gain)


def _ret_bwd(qr, kr, proj, g_out, sf, sb, rf, rb, tabs):
    T = qr.shape[0]
    nc = T // CHUNK
    per = min(RET_CHUNKS_PER_STEP, nc)
    rows_per = per * CHUNK

    def body(q_ref, k_ref, v_ref, g_ref, sf_ref, sb_ref, rf_ref, rb_ref, tab_ref, dq_ref, dk_ref, dv_ref, dl_ref):
        @pl.when(pl.program_id(0) == 0)
        def _():
            dl_ref[...] = jnp.zeros_like(dl_ref)
        for c in range(per):
            rows = slice(c * CHUNK, (c + 1) * CHUNK)
            for h in range(RET_HEADS):
                ks = slice(h * RET_DK, (h + 1) * RET_DK)
                vs = slice(h * RET_DV, (h + 1) * RET_DV)
                q, k, v, g = q_ref[rows, ks], k_ref[rows, ks], v_ref[rows, vs], g_ref[rows, vs]
                s_f, s_b, r_f, r_b = sf_ref[c, ks, :], sb_ref[c, ks, :], rf_ref[c, ks, :], rb_ref[c, ks, :]
                tab = lambda t, h=h: tab_ref[h, t]
                qf, kf = q.astype(F32), k.astype(F32)
                qk = _dot(q, k, _NT)
                da_raw = _dot(g, v, _NT)
                x_f, x_b = _dot(g, s_f, _NT), _dot(g, s_b, _NT)
                dq_ref[rows, ks] = _dot(da_raw * tab(TAB_D), k) + tab(TAB_A) * x_f + tab(TAB_B) * x_b
                at = _dot(k, q, _NT) * tab(TAB_DT)
                dat = _dot(v, g, _NT) * tab(TAB_DT)
                y_f, y_b = _dot(v, r_f, _NT), _dot(v, r_b, _NT)
                dk_ref[rows, ks] = _dot(dat, q) + tab(TAB_CF) * y_f + tab(TAB_CB) * y_b
                dv_ref[rows, vs] = (_dot(at, g) + _dot(kf * tab(TAB_CF), r_f) + _dot(kf * tab(TAB_CB), r_b)).astype(dv_ref.dtype)
                inner = da_raw * qk
                rs_f = r_f.astype(F32) * s_f.astype(F32)
                rs_b = r_b.astype(F32) * s_b.astype(F32)
                dl_f = (inner * tab(TAB_EF) + tab(TAB_RA) * qf * x_f + tab(TAB_RCF) * kf * y_f
                        + tab(TAB_KF) * (rs_f[:, :CHUNK] + rs_f[:, CHUNK:]))
                dl_b = (inner * tab(TAB_EB) + tab(TAB_RB) * qf * x_b + tab(TAB_RCB) * kf * y_b
                        + tab(TAB_KB) * (rs_b[:, :CHUNK] + rs_b[:, CHUNK:]))
                dl_ref[2 * h:2 * h + 1, :] += jnp.sum(dl_f, axis=0, keepdims=True)
                dl_ref[2 * h + 1:2 * h + 2, :] += jnp.sum(dl_b, axis=0, keepdims=True)

    row = lambda i: (i, 0)
    st = lambda i: (i, 0, 0)
    return pl.pallas_call(
        body, name="ret_bwd", grid=(nc // per,),
        in_specs=[pl.BlockSpec((rows_per, RET_Q), row), pl.BlockSpec((rows_per, RET_Q), row),
                  pl.BlockSpec((rows_per, RET_V), lambda i: (i, 1)), pl.BlockSpec((rows_per, RET_V), row),
                  pl.BlockSpec((per, RET_Q, RET_DV), st), pl.BlockSpec((per, RET_Q, RET_DV), st),
                  pl.BlockSpec((per, RET_Q, RET_DV), st), pl.BlockSpec((per, RET_Q, RET_DV), st),
                  pl.BlockSpec((RET_HEADS, 14, CHUNK, CHUNK), lambda i: (0, 0, 0, 0))],
        out_specs=[pl.BlockSpec((rows_per, RET_Q), row), pl.BlockSpec((rows_per, RET_Q), row),
                   pl.BlockSpec((rows_per, RET_V), row), pl.BlockSpec((8, CHUNK), lambda i: (0, 0))],
        out_shape=[_sds((T, RET_Q), F32), _sds((T, RET_Q), F32), _sds((T, RET_V), _MXU), _sds((8, CHUNK), F32)],
        compiler_params=_params("arbitrary"),
    )(qr, kr, proj, g_out, sf, sb, rf, rb, tabs)


def _swa_probs(q, k_win, bias, sink, valid):
    s = _dot(q, k_win, _NT) * ATT_SCALE + bias
    s = jnp.where(valid, s, NEG_INF)
    m = jnp.maximum(jnp.max(s, axis=-1, keepdims=True), sink)
    p = jnp.exp(s - m)
    e_sink = jnp.exp(sink - m)
    inv = 1.0 / (jnp.sum(p, axis=-1, keepdims=True) + e_sink)
    return p * inv, e_sink * inv


def _swa_group(g, q_ref, bias_ref, sink_ref):
    group = SWA_HEADS // SWA_KV_HEADS
    heads = range(g * group, (g + 1) * group)
    q = jnp.concatenate([q_ref[:, h * HEAD_DIM:(h + 1) * HEAD_DIM] for h in heads], axis=0)
    bias = bias_ref[g * group:(g + 1) * group].reshape(group * CHUNK, 3 * CHUNK)
    sink = jnp.concatenate([jnp.broadcast_to(sink_ref[h:h + 1, 0:1], (CHUNK, 1)) for h in heads], axis=0)
    return q, bias, sink


def _swa_valid(i, nb):
    col = lax.broadcasted_iota(jnp.int32, (1, 3 * CHUNK), 1)
    return jnp.logical_and(jnp.logical_or(col >= CHUNK, i > 0), jnp.logical_or(col < 2 * CHUNK, i < nb - 1))


def _swa_window_specs(nb, width, col_block, clamp):
    prev = lambda i: (jnp.maximum(clamp(i) - 1, 0), col_block)
    cur = lambda i: (clamp(i), col_block)
    nxt = lambda i: (jnp.minimum(clamp(i) + 1, nb - 1), col_block)
    return [pl.BlockSpec((CHUNK, width), f) for f in (prev, cur, nxt)]


def _swa_fwd(qn, kn, proj, bias, sink):
    T = qn.shape[0]
    nb = T // CHUNK
    kvw = SWA_KV_HEADS * HEAD_DIM
    group = SWA_HEADS // SWA_KV_HEADS

    def body(q_ref, k0, k1, k2, v0, v1, v2, bias_ref, sink_ref, y_ref):
        i = pl.program_id(0)
        valid = _swa_valid(i, nb)
        for g in range(SWA_KV_HEADS):
            gs = slice(g * HEAD_DIM, (g + 1) * HEAD_DIM)
            k_win = jnp.concatenate([k0[:, gs], k1[:, gs], k2[:, gs]], axis=0)
            v_win = jnp.concatenate([v0[:, gs], v1[:, gs], v2[:, gs]], axis=0).astype(_MXU)
            q, bias, sink = _swa_group(g, q_ref, bias_ref, sink_ref)
            p, _ = _swa_probs(q, k_win, bias, sink, valid)
            o = _dot(p, v_win)
            for hh in range(group):
                h = g * group + hh
                y_ref[:, h * HEAD_DIM:(h + 1) * HEAD_DIM] = o[hh * CHUNK:(hh + 1) * CHUNK].astype(y_ref.dtype)

    ident = lambda i: i
    return pl.pallas_call(
        body, name="swa_fwd", grid=(nb,),
        in_specs=[pl.BlockSpec((CHUNK, D_MODEL), lambda i: (i, 0))]
        + _swa_window_specs(nb, kvw, 0, ident) + _swa_window_specs(nb, kvw, 17, ident)
        + [pl.BlockSpec((SWA_HEADS, CHUNK, 3 * CHUNK), lambda i: (0, 0, 0)), pl.BlockSpec((SWA_HEADS, HEAD_DIM), lambda i: (0, 0))],
        out_specs=pl.BlockSpec((CHUNK, D_MODEL), lambda i: (i, 0)),
        out_shape=_sds((T, D_MODEL), _MXU),
        compiler_params=_params("parallel"),
    )(qn, kn, kn, kn, proj, proj, proj, bias, sink)


def _swa_bwd(qn, kn, proj, dycat, bias, sink):
    T = qn.shape[0]
    nb = T // CHUNK
    kvw = SWA_KV_HEADS * HEAD_DIM
    group = SWA_HEADS // SWA_KV_HEADS

    def body(q_ref, k0, k1, k2, v0, v1, v2, dy_ref, bias_ref, sink_ref,
             dq_ref, dk_ref, dv_ref, dbias_ref, dsink_ref, acc_a, acc_b):
        i = pl.program_id(0)

        @pl.when(i == 0)
        def _():
            dbias_ref[...] = jnp.zeros_like(dbias_ref)
            dsink_ref[...] = jnp.zeros_like(dsink_ref)
            acc_a[...] = jnp.zeros_like(acc_a)
            acc_b[...] = jnp.zeros_like(acc_b)

        @pl.when(i < nb)
        def _():
            valid = _swa_valid(i, nb)
            for g in range(SWA_KV_HEADS):
                gs = slice(g * HEAD_DIM, (g + 1) * HEAD_DIM)
                k_win = jnp.concatenate([k0[:, gs], k1[:, gs], k2[:, gs]], axis=0)
                v_win = jnp.concatenate([v0[:, gs], v1[:, gs], v2[:, gs]], axis=0).astype(_MXU)
                q, bias, sink = _swa_group(g, q_ref, bias_ref, sink_ref)
                dy = jnp.concatenate([dy_ref[:, h * HEAD_DIM:(h + 1) * HEAD_DIM] for h in range(g * group, (g + 1) * group)], axis=0)
                p, p_sink = _swa_probs(q, k_win, bias, sink, valid)
                dp = _dot(dy, v_win, _NT)
                delta = jnp.sum(p * dp, axis=-1, keepdims=True)
                ds = p * (dp - delta)
                dsink = -p_sink * delta
                dq = _dot(ds, k_win) * ATT_SCALE
                for hh in range(group):
                    h = g * group + hh
                    rows = slice(hh * CHUNK, (hh + 1) * CHUNK)
                    dbias_ref[h] += ds[rows]
                    dsink_ref[h:h + 1, :] += jnp.sum(dsink[rows], axis=0, keepdims=True) * jnp.ones((1, HEAD_DIM), F32)
                    dq_ref[:, h * HEAD_DIM:(h + 1) * HEAD_DIM] = dq[rows]
                dk_win = _dot(ds, q, _TN) * ATT_SCALE
                dv_win = _dot(p, dy, _TN)
                for win, out_ref, col0 in ((dk_win, dk_ref, 0), (dv_win, dv_ref, kvw)):
                    cs = slice(col0 + g * HEAD_DIM, col0 + (g + 1) * HEAD_DIM)
                    out_ref[:, gs] = acc_a[:, cs] + win[:CHUNK]
                    acc_a[:, cs] = acc_b[:, cs] + win[CHUNK:2 * CHUNK]
                    acc_b[:, cs] = win[2 * CHUNK:]

        @pl.when(i == nb)
        def _():
            dk_ref[...] = acc_a[:, :kvw]
            dv_ref[...] = acc_a[:, kvw:]

    clamp = lambda i: jnp.minimum(i, nb - 1)
    late = lambda i: (jnp.maximum(i - 1, 0), 0)
    return pl.pallas_call(
        body, name="swa_bwd", grid=(nb + 1,),
        in_specs=[pl.BlockSpec((CHUNK, D_MODEL), lambda i: (clamp(i), 0))]
        + _swa_window_specs(nb, kvw, 0, clamp) + _swa_window_specs(nb, kvw, 17, clamp)
        + [pl.BlockSpec((CHUNK, D_MODEL), lambda i: (clamp(i), 1)),
           pl.BlockSpec((SWA_HEADS, CHUNK, 3 * CHUNK), lambda i: (0, 0, 0)), pl.BlockSpec((SWA_HEADS, HEAD_DIM), lambda i: (0, 0))],
        out_specs=[pl.BlockSpec((CHUNK, D_MODEL), lambda i: (clamp(i), 0)),
                   pl.BlockSpec((CHUNK, kvw), late), pl.BlockSpec((CHUNK, kvw), late),
                   pl.BlockSpec((SWA_HEADS, CHUNK, 3 * CHUNK), lambda i: (0, 0, 0)), pl.BlockSpec((SWA_HEADS, HEAD_DIM), lambda i: (0, 0))],
        out_shape=[_sds((T, D_MODEL), F32), _sds((T, kvw), F32), _sds((T, kvw), F32),
                   _sds((SWA_HEADS, CHUNK, 3 * CHUNK), F32), _sds((SWA_HEADS, HEAD_DIM), F32)],
        scratch_shapes=[pltpu.VMEM((CHUNK, 2 * kvw), F32), pltpu.VMEM((CHUNK, 2 * kvw), F32)],
        compiler_params=_params("arbitrary"),
    )(qn, kn, kn, kn, proj, proj, proj, dycat, bias, sink)


def _t5_bucket_reduce(dbias, bucket):
    def body(db_ref, bk_ref, o_ref):
        bk = bk_ref[...]
        row = lax.broadcasted_iota(jnp.int32, (SWA_HEADS, HEAD_DIM), 0)
        lane = lax.broadcasted_iota(jnp.int32, (SWA_HEADS, HEAD_DIM), 1)

        def per_bucket(b, acc):
            mask = bk == b
            for h in range(SWA_HEADS):
                tot = jnp.sum(jnp.sum(jnp.where(mask, db_ref[h], 0.0), axis=0, keepdims=True), axis=1, keepdims=True)
                acc = acc + jnp.where(jnp.logical_and(row == h, lane == b), tot, 0.0)
            return acc

        o_ref[...] = lax.fori_loop(0, T5_BUCKETS, per_bucket, jnp.zeros((SWA_HEADS, HEAD_DIM), F32))

    return pl.pallas_call(body, name="t5_bucket_reduce", out_shape=_sds((SWA_HEADS, HEAD_DIM), F32),
                          compiler_params=pltpu.CompilerParams(vmem_limit_bytes=VMEM_LIMIT_BYTES))(dbias, bucket)


def _headnorm_bwd(x, dy, gain):
    r = lax.rsqrt(jnp.mean(x * x, axis=-1, keepdims=True) + EPS)
    xhat = x * r
    dyg = dy * gain
    return r * (dyg - xhat * jnp.mean(dyg * xhat, axis=-1, keepdims=True)), dy * xhat


def _post_even(proj, dqr, dkr, dva, dga, dqn, dkn, dvb, cos, sin, q_gain, k_gain, *, tm):
    T = proj.shape[0]
    tm = min(tm, T)
    kvw = SWA_KV_HEADS * HEAD_DIM

    def body(qb_ref, kb_ref, dqr_ref, dkr_ref, dva_ref, dga_ref, dqn_ref, dkn_ref, dvb_ref, c_ref, s_ref, qg_ref, kg_ref,
             dp_ref, dqg_ref, dkg_ref):
        @pl.when(pl.program_id(0) == 0)
        def _():
            dqg_ref[...] = jnp.zeros_like(dqg_ref)
            dkg_ref[...] = jnp.zeros_like(dkg_ref)
        c = jnp.concatenate([c_ref[...]] * RET_HEADS, axis=1)
        s = jnp.concatenate([s_ref[...]] * RET_HEADS, axis=1)
        dq = dqr_ref[...]
        dp_ref[:, 0:RET_Q] = (dq * c + _swap_halves(dq * s, RET_DK // 2)).astype(dp_ref.dtype)
        dk = dkr_ref[...] * (RET_DK ** -0.5)
        dp_ref[:, RET_Q:2 * RET_Q] = (dk * c + _swap_halves(dk * s, RET_DK // 2)).astype(dp_ref.dtype)
        off = 2 * RET_Q
        dp_ref[:, off:off + RET_V] = dva_ref[...].astype(dp_ref.dtype)
        dp_ref[:, off + RET_V:off + 2 * RET_V] = dga_ref[...].astype(dp_ref.dtype)
        off += 2 * RET_V
        for src, dsrc, gain, dgain, heads, base in ((qb_ref, dqn_ref, qg_ref, dqg_ref, SWA_HEADS, off),
                                                    (kb_ref, dkn_ref, kg_ref, dkg_ref, SWA_KV_HEADS, off + D_MODEL)):
            for h in range(heads):
                sl = slice(h * HEAD_DIM, (h + 1) * HEAD_DIM)
                dx, dgx = _headnorm_bwd(src[:, sl], dsrc[:, sl], gain[...])
                dp_ref[:, base + h * HEAD_DIM:base + (h + 1) * HEAD_DIM] = dx.astype(dp_ref.dtype)
                dgain[...] += _rowsum8(dgx)
        dp_ref[:, off + D_MODEL + kvw:] = dvb_ref[...].astype(dp_ref.dtype)

    row = lambda i: (i, 0)
    const = lambda i: (0, 0)
    return pl.pallas_call(
        body, name="post_even", grid=(T // tm,),
        in_specs=[pl.BlockSpec((tm, D_MODEL), lambda i: (i, 3)), pl.BlockSpec((tm, kvw), lambda i: (i, 16)),
                  pl.BlockSpec((tm, RET_Q), row), pl.BlockSpec((tm, RET_Q), row),
                  pl.BlockSpec((tm, RET_V), row), pl.BlockSpec((tm, RET_V), row),
                  pl.BlockSpec((tm, D_MODEL), row), pl.BlockSpec((tm, kvw), row), pl.BlockSpec((tm, kvw), row),
                  pl.BlockSpec((tm, RET_DK), row), pl.BlockSpec((tm, RET_DK), row),
                  pl.BlockSpec((1, HEAD_DIM), const), pl.BlockSpec((1, HEAD_DIM), const)],
        out_specs=[pl.BlockSpec((tm, EVEN_IN), row), pl.BlockSpec((8, HEAD_DIM), const), pl.BlockSpec((8, HEAD_DIM), const)],
        out_shape=[_sds((T, EVEN_IN), _MXU), _sds((8, HEAD_DIM), F32), _sds((8, HEAD_DIM), F32)],
        compiler_params=_params("arbitrary"),
    )(proj, proj, dqr, dkr, dva, dga, dqn, dkn, dvb, cos, sin, q_gain, k_gain)


def _prep_odd(proj, cos, sin, q_gain, k_gain, *, tm):
    T = proj.shape[0]
    tm = min(tm, T)
    kvw = AX_KV_HEADS * HEAD_DIM

    def body(q_ref, k_ref, v_ref, c_ref, s_ref, qg_ref, kg_ref, qx_ref, kx_ref, vx_ref):
        c, s = c_ref[...], s_ref[...]
        for src, gain, dst, heads, scale in ((q_ref, qg_ref, qx_ref, AX_HEADS, SCORE_SCALE_LOG2), (k_ref, kg_ref, kx_ref, AX_KV_HEADS, 1.0)):
            for h in range(heads):
                sl = slice(h * HEAD_DIM, (h + 1) * HEAD_DIM)
                xh = src[:, sl]
                r = lax.rsqrt(jnp.mean(xh * xh, axis=-1, keepdims=True) + EPS)
                xn = xh * r * gain[...]
                dst[:, sl] = ((xn * c + _swap_halves(xn, HEAD_DIM // 4) * s) * scale).astype(dst.dtype)
        vx_ref[...] = v_ref[...].astype(vx_ref.dtype)

    row = lambda i: (i, 0)
    const = lambda i: (0, 0)
    return pl.pallas_call(
        body, name="prep_odd", grid=(T // tm,),
        in_specs=[pl.BlockSpec((tm, D_MODEL), row), pl.BlockSpec((tm, kvw), lambda i: (i, 4)), pl.BlockSpec((tm, kvw), lambda i: (i, 5)),
                  pl.BlockSpec((tm, HEAD_DIM), row), pl.BlockSpec((tm, HEAD_DIM), row),
                  pl.BlockSpec((1, HEAD_DIM), const), pl.BlockSpec((1, HEAD_DIM), const)],
        out_specs=[pl.BlockSpec((tm, D_MODEL), row), pl.BlockSpec((tm, kvw), row), pl.BlockSpec((tm, kvw), row)],
        out_shape=[_sds((T, D_MODEL), _MXU), _sds((T, kvw), _MXU), _sds((T, kvw), _MXU)],
        compiler_params=_params("parallel"),
    )(proj, proj, proj, cos, sin, q_gain, k_gain)


def _post_odd(proj, dqxt, dkx, dvx, cos, sin, q_gain, k_gain, *, tm):
    T = proj.shape[0]
    tm = min(tm, T)
    kvw = AX_KV_HEADS * HEAD_DIM

    def body(q_ref, k_ref, dqt_ref, dk_ref, dv_ref, c_ref, s_ref, qg_ref, kg_ref, dp_ref, dqg_ref, dkg_ref):
        @pl.when(pl.program_id(0) == 0)
        def _():
            dqg_ref[...] = jnp.zeros_like(dqg_ref)
            dkg_ref[...] = jnp.zeros_like(dkg_ref)
        c, s = c_ref[...], s_ref[...]
        for src, dsrc, gain, dgain, heads, base in ((q_ref, dqt_ref, qg_ref, dqg_ref, AX_HEADS, 0),
                                                    (k_ref, dk_ref, kg_ref, dkg_ref, AX_KV_HEADS, D_MODEL)):
            for h in range(heads):
                sl = slice(h * HEAD_DIM, (h + 1) * HEAD_DIM)
                d = dsrc[sl, :].T if dsrc is dqt_ref else dsrc[:, sl]
                dn = d * c + _swap_halves(d * s, HEAD_DIM // 4)
                dx, dgx = _headnorm_bwd(src[:, sl], dn, gain[...])
                dp_ref[:, base + h * HEAD_DIM:base + (h + 1) * HEAD_DIM] = dx.astype(dp_ref.dtype)
                dgain[...] += _rowsum8(dgx)
        dp_ref[:, D_MODEL + kvw:] = dv_ref[...].astype(dp_ref.dtype)

    row = lambda i: (i, 0)
    const = lambda i: (0, 0)
    return pl.pallas_call(
        body, name="post_odd", grid=(T // tm,),
        in_specs=[pl.BlockSpec((tm, D_MODEL), row), pl.BlockSpec((tm, kvw), lambda i: (i, 4)),
                  pl.BlockSpec((D_MODEL, tm), lambda i: (0, i)), pl.BlockSpec((tm, kvw), row), pl.BlockSpec((tm, kvw), row),
                  pl.BlockSpec((tm, HEAD_DIM), row), pl.BlockSpec((tm, HEAD_DIM), row),
                  pl.BlockSpec((1, HEAD_DIM), const), pl.BlockSpec((1, HEAD_DIM), const)],
        out_specs=[pl.BlockSpec((tm, ODD_IN), row), pl.BlockSpec((8, HEAD_DIM), const), pl.BlockSpec((8, HEAD_DIM), const)],
        out_shape=[_sds((T, ODD_IN), _MXU), _sds((8, HEAD_DIM), F32), _sds((8, HEAD_DIM), F32)],
        compiler_params=_params("arbitrary"),
    )(proj, proj, dqxt, dkx, dvx, cos, sin, q_gain, k_gain)


ONES_ROWS = 16


def _flash_fwd(qx, kx, vx, *, tq, tk):
    v1t = jnp.concatenate([vx.T.reshape(AX_KV_HEADS, HEAD_DIM, vx.shape[0]),
                           jnp.ones((AX_KV_HEADS, ONES_ROWS, vx.shape[0]), vx.dtype)], axis=1)
    T = qx.shape[0]
    tq, tk = min(tq, T), min(tk, T)
    nq, nk = T // tq, T // tk
    group = AX_HEADS // AX_KV_HEADS

    def body(k_ref, v_ref, q_ref, o_ref, lse_ref, acc_sc, m_sc, l_sc):
        j = pl.program_id(2)

        @pl.when(j == 0)
        def _():
            m_sc[...] = jnp.full(m_sc.shape, NEG_INF, F32)
            l_sc[...] = jnp.zeros_like(l_sc)
            acc_sc[...] = jnp.zeros_like(acc_sc)
        k, v = k_ref[...], v_ref[0]

        def step(i, carry):
            cols = pl.ds(pl.multiple_of(i * tq, tq), tq)
            st = _dot(k, q_ref[cols, :], _NT)
            m_old = m_sc[i]
            m_new = jnp.maximum(m_old, jnp.max(st, axis=0, keepdims=True))
            p = jnp.exp2(st - m_new)
            alpha = jnp.exp2(m_old - m_new)
            pv = _dot(v, p)
            m_sc[i] = m_new
            l_sc[i] = alpha * l_sc[i] + pv[HEAD_DIM:HEAD_DIM + 1]
            acc_sc[:, cols] = alpha * acc_sc[:, cols] + pv[:HEAD_DIM]
            return carry

        lax.fori_loop(0, nq, step, 0)

        @pl.when(j == nk - 1)
        def _():
            def finish(i, carry):
                cols = pl.ds(pl.multiple_of(i * tq, tq), tq)
                o_ref[cols, :] = (acc_sc[:, cols] / l_sc[i]).T.astype(o_ref.dtype)
                lse_ref[0, i] = m_sc[i] + jnp.log2(l_sc[i])
                return carry

            lax.fori_loop(0, nq, finish, 0)

    kv = lambda g, h, j: (j, g)
    qh = lambda g, h, j: (0, g * group + h)
    o, lse = pl.pallas_call(
        body, name="flash_fwd", grid=(AX_KV_HEADS, group, nk),
        in_specs=[pl.BlockSpec((tk, HEAD_DIM), kv), pl.BlockSpec((1, HEAD_DIM + ONES_ROWS, tk), lambda g, h, j: (g, 0, j)),
                  pl.BlockSpec((T, HEAD_DIM), qh)],
        out_specs=[pl.BlockSpec((T, HEAD_DIM), qh), pl.BlockSpec((1, nq, 1, tq), lambda g, h, j: (g * group + h, 0, 0, 0))],
        out_shape=[_sds((T, D_MODEL), _MXU), _sds((AX_HEADS, nq, 1, tq), F32)],
        scratch_shapes=[pltpu.VMEM((HEAD_DIM, T), F32), pltpu.VMEM((nq, 1, tq), F32), pltpu.VMEM((nq, 1, tq), F32)],
        compiler_params=_params("parallel", "arbitrary", "arbitrary"),
    )(kx, v1t, qx)
    return o, lse.reshape(AX_HEADS, 1, T)


def _flash_bwd(qx, kx, vx, o, do, lse, *, tq, tk):
    T = qx.shape[0]
    tq, tk = min(tq, T), min(tk, T)
    nq = T // tq
    group = AX_HEADS // AX_KV_HEADS
    lse_rows = lse.reshape(AX_HEADS, nq, 1, tq)
    kxt = kx.T.reshape(AX_KV_HEADS, HEAD_DIM, T)

    def body(k_ref, kt_ref, v_ref, q_ref, o_ref, do_ref, lse_ref, dqt_ref, dk_ref, dv_ref, delta_sc):
        j = pl.program_id(2)

        @pl.when(jnp.logical_and(pl.program_id(1) == 0, j == 0))
        def _():
            dk_ref[...] = jnp.zeros_like(dk_ref)
            dv_ref[...] = jnp.zeros_like(dv_ref)

        @pl.when(j == 0)
        def _():
            dqt_ref[...] = jnp.zeros_like(dqt_ref)

            def row_delta(i, carry):
                rows = pl.ds(pl.multiple_of(i * tq, tq), tq)
                prod = do_ref[rows, :].astype(F32) * o_ref[rows, :].astype(F32)
                delta_sc[i] = jnp.sum(prod.T, axis=0, keepdims=True)
                return carry

            lax.fori_loop(0, nq, row_delta, 0)
        k, v = k_ref[...], v_ref[...]

        def step(i, carry):
            dk, dv = carry
            off = pl.multiple_of(i * tq, tq)
            q, do_blk = q_ref[pl.ds(off, tq), :], do_ref[pl.ds(off, tq), :]
            pt = jnp.exp2(_dot(k, q, _NT) - lse_ref[0, i])
            dst = pt * (_dot(v, do_blk, _NT) - delta_sc[i])
            dqt_ref[:, pl.ds(off, tq)] += _dot(kt_ref[0], dst) * ATT_SCALE
            return dk + _dot(dst, q), dv + _dot(pt, do_blk)

        zero = jnp.zeros((tk, HEAD_DIM), F32)
        dk, dv = lax.fori_loop(0, nq, step, (zero, zero))
        rows = pl.ds(pl.multiple_of(j * tk, tk), tk)
        dk_ref[rows, :] += dk * (ATT_SCALE / SCORE_SCALE_LOG2)
        dv_ref[rows, :] += dv

    kv = lambda g, h, j: (j, g)
    qh = lambda g, h, j: (0, g * group + h)
    st = lambda g, h, j: (g * group + h, 0, 0, 0)
    acc = lambda g, h, j: (0, g)
    return pl.pallas_call(
        body, name="flash_bwd", grid=(AX_KV_HEADS, group, T // tk),
        in_specs=[pl.BlockSpec((tk, HEAD_DIM), kv), pl.BlockSpec((1, HEAD_DIM, tk), lambda g, h, j: (g, 0, j)),
                  pl.BlockSpec((tk, HEAD_DIM), kv),
                  pl.BlockSpec((T, HEAD_DIM), qh), pl.BlockSpec((T, HEAD_DIM), qh), pl.BlockSpec((T, HEAD_DIM), qh),
                  pl.BlockSpec((1, nq, 1, tq), st)],
        out_specs=[pl.BlockSpec((HEAD_DIM, T), lambda g, h, j: (g * group + h, 0)),
                   pl.BlockSpec((T, HEAD_DIM), acc), pl.BlockSpec((T, HEAD_DIM), acc)],
        out_shape=[_sds((D_MODEL, T), F32), _sds((T, AX_KV_HEADS * HEAD_DIM), F32), _sds((T, AX_KV_HEADS * HEAD_DIM), F32)],
        scratch_shapes=[pltpu.VMEM((nq, 1, tq), F32)],
        compiler_params=_params("parallel", "arbitrary", "arbitrary"),
    )(kx, kxt, vx, qx, o, do, lse_rows)


TM = 1024
TM_WIDE = 512


def _mlp_fwd(tag, x, gain, w_up, w_down, target=None):
    u, h = _norm_matmul(f"mlp_up{tag}", x, gain, w_up, tm=TM_WIDE, tn=D_FF, out_dtype=_MXU)
    out = _matmul_res(f"mlp_down{tag}", [u], w_down, x, tm=TM if target is None else TM_WIDE, relu2=True, target=target)
    return out, (x, u, h)


def _local_step(x, target, p, w_first, fetch_rest, push, tokens=()):
    T = x.shape[0]
    cos_r, sin_r = _ret_rope_tables(T)
    cos_a, sin_a = _axial_rope_tables(T)
    tabs, rw, log_gamma = _retention_tables(p["ret_decay_logit"][0])
    bias = _swa_bias(p["t5_table"])
    sink = p["swa_sink"][0][:, None] * jnp.ones((1, HEAD_DIM), F32)
    nm, nl = p["norm_mix"], p["norm_mlp"]
    pending = [t for t in tokens if t is not None]

    def send(tag, weight, dw):
        token = push(tag, weight, dw[None])
        if token is not None:
            pending.append(token)

    def tied(operand):
        while pending:
            operand = operand + pending.pop()[0:1, 0:1]
        return operand

    def mlp_bwd(tag, saved, gain, w_up, w_down, dy, dy16):
        xs, u, h = saved
        du = _matmul_nt(f"mlp_down{tag}_bwd", dy16, w_down, tm=TM_WIDE, tn=D_FF, out_dtype=_MXU, relu_of=u)
        send(f"mlp_down{tag}", "w_mlp_down", _matmul_tn(f"mlp_down{tag}_dw", u, dy16, tk=2048, tn=1024, tt=1024, out_dtype=_WIRE, relu2=True))
        dx, dx16, dgain = _matmul_nt_normbwd(f"mlp_up{tag}_bwd", du, w_up, xs, tied(gain), dy, tm=TM_WIDE)
        send(f"mlp_up{tag}", "w_mlp_up", _matmul_tn(f"mlp_up{tag}_dw", h, du, tk=1024, tn=2048, tt=1024, out_dtype=_WIRE))
        return dx, dx16, dgain

    proj0, h0 = _norm_matmul("in_even", x, tied(nm[0:1]), w_first["w_in_even"], tm=TM_WIDE, tn=EVEN_IN, out_dtype=F32)
    qr, kr, qn, kn = _prep_even(proj0, cos_r, sin_r, p["swa_q_norm"], p["swa_k_norm"], tm=TM)
    sf, sb = _ret_scan("ret_scan_fwd", kr, proj0, 1, rw["cf"], rw["dec_f"], rw["cb"], rw["dec_b"])
    ret_o, ya = _ret_out(qr, kr, proj0, sf, sb, tabs[:, (TAB_D, TAB_A, TAB_B)], p["ret_norm"])
    yb = _swa_fwd(qn, kn, proj0, bias, sink)
    wf = {**w_first, **fetch_rest(yb)}
    x1 = _matmul_res("out_even", [ya, yb], wf["w_out_even"], x, tm=TM)
    x2, mlp0 = _mlp_fwd(0, x1, nl[0:1], wf["w_mlp_up"][0], wf["w_mlp_down"][0])
    proj1, h1 = _norm_matmul("in_odd", x2, nm[1:2], wf["w_in_odd"], tm=TM, tn=ODD_IN, out_dtype=F32)
    qx, kx, vx = _prep_odd(proj1, cos_a, sin_a, p["ax_q_norm"], p["ax_k_norm"], tm=TM)
    o, lse = _flash_fwd(qx, kx, vx, tq=2048, tk=1024)
    x3 = _matmul_res("out_odd", [o], wf["w_out_odd"], x2, tm=TM)
    (g4, g4_16, loss_part), mlp1 = _mlp_fwd(1, x3, nl[1:2], wf["w_mlp_up"][1], wf["w_mlp_down"][1], target=target)

    dx3, dx3_16, dnl1 = mlp_bwd(1, mlp1, nl[1:2], wf["w_mlp_up"][1], wf["w_mlp_down"][1], g4, g4_16)
    do = _matmul_nt("out_odd_bwd", dx3_16, wf["w_out_odd"], tm=TM, tn=1024, out_dtype=_MXU)
    send("out_odd", "w_out_odd", _matmul_tn("out_odd_dw", o, dx3_16, tk=1024, tn=1024, tt=1024, out_dtype=_WIRE))
    dqxt, dkx, dvx = _flash_bwd(qx, kx, vx, o, do, lse, tq=2048, tk=512)
    dproj1, dqg1, dkg1 = _post_odd(proj1, dqxt, dkx, dvx, cos_a, sin_a, tied(p["ax_q_norm"]), p["ax_k_norm"], tm=TM)
    send("in_odd", "w_in_odd", _matmul_tn("in_odd_dw", h1, dproj1, tk=1024, tn=768, tt=1024, out_dtype=_WIRE))
    dx2, dx2_16, dnm1 = _matmul_nt_normbwd("in_odd_bwd", dproj1, wf["w_in_odd"], x2, tied(nm[1:2]), dx3, tm=TM_WIDE)
    dx1, dx1_16, dnl0 = mlp_bwd(0, mlp0, nl[0:1], wf["w_mlp_up"][0], wf["w_mlp_down"][0], dx2, dx2_16)
    dycat = _matmul_nt("out_even_bwd", dx1_16, wf["w_out_even"], tm=TM, tn=2 * D_MODEL, out_dtype=F32)
    send("out_even", "w_out_even", jnp.concatenate([
        _matmul_tn("out_even_dw_ret", ya, dx1_16, tk=1024, tn=1024, tt=1024, out_dtype=_WIRE),
        _matmul_tn("out_even_dw_swa", yb, dx1_16, tk=1024, tn=1024, tt=1024, out_dtype=_WIRE)], axis=0))
    g_out, dga, dretg = _ret_gate_bwd(dycat, proj0, ret_o, tied(p["ret_norm"]), tm=TM)
    rb, rf = _ret_scan("ret_scan_bwd", qr, g_out, 0, rw["b"], rw["dec_b"], rw["a"], rw["dec_f"])
    dqr, dkr, dva, dlog = _ret_bwd(qr, kr, proj0, g_out, sf, sb, rf, rb, tabs)
    dqn, dkn, dvb, dbias, dsink = _swa_bwd(qn, kn, proj0, dycat, bias, sink)
    dt5 = _t5_bucket_reduce(dbias, _t5_bucket(_swa_rel()).astype(jnp.int32))
    dproj0, dqg0, dkg0 = _post_even(proj0, dqr, dkr, dva, dga, dqn, dkn, dvb, cos_r, sin_r,
                                    p["swa_q_norm"], p["swa_k_norm"], tm=TM_WIDE)
    send("in_even", "w_in_even", _matmul_tn("in_even_dw", h0, dproj0, tk=1024, tn=2304, tt=1024, out_dtype=_WIRE))
    dx0, _, dnm0 = _matmul_nt_normbwd("in_even_bwd", dproj0, w_first["w_in_even"], x, tied(nm[0:1]), dx1, tm=TM_WIDE)

    fold = lambda part: jnp.sum(part, axis=0)
    dlam = jnp.sum(dlog, axis=1).reshape(RET_HEADS, 2).T
    small = {
        "norm_mix": jnp.stack([fold(dnm0), fold(dnm1)]),
        "norm_mlp": jnp.stack([fold(dnl0), fold(dnl1)]),
        "ret_decay_logit": (dlam * (1.0 - jnp.exp(log_gamma)))[None],
        "ret_norm": fold(dretg)[None],
        "swa_q_norm": fold(dqg0)[None], "swa_k_norm": fold(dkg0)[None],
        "swa_sink": dsink[:, 0][None],
        "t5_table": dt5[:, :T5_BUCKETS].T,
        "ax_q_norm": fold(dqg1)[None], "ax_k_norm": fold(dkg1)[None],
    }
    return loss_part, dx0, small


BIG = ("w_in_even", "w_out_even", "w_in_odd", "w_out_odd", "w_mlp_up", "w_mlp_down")
SMALL = ("norm_mix", "norm_mlp", "ret_decay_logit", "ret_norm", "swa_q_norm", "swa_k_norm", "swa_sink", "t5_table",
         "ax_q_norm", "ax_k_norm")
WEIGHTS = ("norm_mix", "norm_mlp", "w_in_even", "w_out_even", "ret_decay_logit", "ret_norm", "swa_q_norm", "swa_k_norm",
           "swa_sink", "t5_table", "w_in_odd", "w_out_odd", "ax_q_norm", "ax_k_norm", "w_mlp_up", "w_mlp_down")
SHARD_AXIS = {"w_in_even": 2, "w_out_even": 1, "w_in_odd": 2, "w_out_odd": 1, "w_mlp_up": 2, "w_mlp_down": 1}
N_CHIPS = 4
ANY = pl.BlockSpec(memory_space=pl.ANY)
HBM = pl.BlockSpec(memory_space=pltpu.HBM)
SEM = pl.BlockSpec(memory_space=pltpu.SEMAPHORE)
SPLIT_COPY = pltpu.CompilerParams(has_side_effects=pltpu.SideEffectType.DATAFLOW_SIDE_EFFECTING)


def _in_hbm(a):
    return pltpu.with_memory_space_constraint(a, pltpu.HBM)


def _mesh_pos():
    return lax.axis_index("x"), lax.axis_index("y"), lax.axis_index("c")


def _window(ref, axis, start, size):
    idx = [slice(None)] * len(ref.shape)
    idx[axis] = pl.ds(start, size)
    return ref.at[tuple(idx)]


def _cast_place(key, shard, chip, *, tr=256):
    L, R, C = shard.shape
    tr = min(tr, R)
    axis = SHARD_AXIS[key]
    whole = tuple(d * (N_CHIPS if a == axis else 1) for a, d in enumerate(shard.shape))

    def body(chip_ref, s_ref, o_ref):
        o_ref[...] = s_ref[...].astype(o_ref.dtype)

    if axis == 2:
        out_map = lambda l, i, chip_ref: (l, i, chip_ref[0])
    else:
        out_map = lambda l, i, chip_ref: (l, i + chip_ref[0] * (R // tr), 0)
    grid_spec = pltpu.PrefetchScalarGridSpec(
        num_scalar_prefetch=1, grid=(L, R // tr),
        in_specs=[pl.BlockSpec((1, tr, C), lambda l, i, chip_ref: (l, i, 0))],
        out_specs=pl.BlockSpec((1, tr, C), out_map))
    return pl.pallas_call(body, name=f"cast_place_{key}", grid_spec=grid_spec, out_shape=_sds(whole, _MXU),
                          compiler_params=_params("parallel", "parallel"))(chip, shard)


def _gather_copies(names, refs, send_sems, recv_sems, *, outgoing=True, incoming=True):
    x, y, c = _mesh_pos()
    chips = [(1 - x, y), (x, 1 - y), (1 - x, 1 - y)]
    out, inc = [], []
    for t, key in enumerate(names):
        size = refs[t].shape[SHARD_AXIS[key]] // N_CHIPS
        slot = lambda px, py: _window(refs[t], SHARD_AXIS[key], pl.multiple_of((2 * px + py) * size, 128), size)
        for k, (px, py) in enumerate(chips):
            sems = dict(send_sem=send_sems.at[3 * t + k], recv_sem=recv_sems.at[3 * t + k], device_id=(px, py, c), device_id_type=MESH)
            if outgoing:
                out.append(pltpu.make_async_remote_copy(slot(x, y), slot(x, y), **sems))
            if incoming:
                inc.append(pltpu.make_async_remote_copy(slot(x, y), slot(px, py), **sems))
    return out, inc


def _allgather_start(groups):
    names = [list(g) for g in groups]
    flat = [g[k] for g in groups for k in g]
    n, ng = len(flat), len(groups)

    def body(*refs):
        start = 0
        for gi, keys in enumerate(names):
            copies, _ = _gather_copies(keys, refs[start:start + len(keys)], refs[n + 2 * gi], refs[n + 2 * gi + 1], incoming=False)
            for cp in copies:
                cp.start()
            start += len(keys)
        token = refs[-1]
        token[...] = jnp.zeros_like(token)

    sem_shapes = [pltpu.SemaphoreType.DMA((3 * len(keys),)) for keys in names for _ in (0, 1)]
    outs = pl.pallas_call(
        body, name="allgather_start", in_specs=[HBM] * n,
        out_specs=[SEM] * (2 * ng) + [HBM] * n + [pl.BlockSpec(memory_space=pltpu.VMEM)],
        out_shape=sem_shapes + [pltpu.HBM(a.shape, a.dtype) for a in flat] + [_sds((8, HEAD_DIM), F32)],
        input_output_aliases={t: 2 * ng + t for t in range(n)},
        compiler_params=SPLIT_COPY,
    )(*[_in_hbm(a) for a in flat])
    states, start = [], 2 * ng
    for gi, keys in enumerate(names):
        states.append((gi, keys, outs[2 * gi], outs[2 * gi + 1], outs[start:start + len(keys)]))
        start += len(keys)
    return states, outs[-1]


def _allgather_wait(state, after):
    gi, names, send_sems, recv_sems, thru = state
    n = len(names)

    def body(*refs):
        outgoing, incoming = _gather_copies(names, refs[:n], refs[n], refs[n + 1])
        for cp in outgoing:
            cp.wait_send()
        for cp in incoming:
            cp.wait_recv()

    outs = pl.pallas_call(
        body, name=f"allgather_wait_{gi}", in_specs=[HBM] * n + [SEM, SEM, ANY], out_specs=[HBM] * n,
        out_shape=[pltpu.HBM(t.shape, t.dtype) for t in thru],
        input_output_aliases={t: t for t in range(n)},
        compiler_params=SPLIT_COPY,
    )(*thru, send_sems, recv_sems, after)
    return dict(zip(names, outs))


FLIPS = [(a, b, d) for a in (0, 1) for b in (0, 1) for d in (0, 1) if (a, b, d) != (0, 0, 0)]


def _flip(pos, f):
    return tuple(1 - p if fi else p for p, fi in zip(pos, f))


def _piece_shape(weight, shape):
    out = list(shape)
    out[SHARD_AXIS[weight]] //= N_CHIPS
    out[1] //= 2
    return tuple(out)


def _piece(ref, weight, chip, core):
    piece = _piece_shape(weight, ref.shape)
    if SHARD_AXIS[weight] == 1:
        return _window(ref, 1, pl.multiple_of((2 * chip + core) * piece[1], 8), piece[1])
    return _window(_window(ref, 2, pl.multiple_of(chip * piece[2], 128), piece[2]), 1, pl.multiple_of(core * piece[1], 8), piece[1])


def _scatter_copies(weight, grad_ref, land_ref, send_sems, recv_sems, *, outgoing=True, incoming=True):
    pos = _mesh_pos()
    out, inc = [], []
    for k, f in enumerate(FLIPS):
        peer = _flip(pos, f)
        sems = dict(send_sem=send_sems.at[k], recv_sem=recv_sems.at[k], device_id=peer, device_id_type=MESH)
        if outgoing:
            out.append(pltpu.make_async_remote_copy(_piece(grad_ref, weight, 2 * peer[0] + peer[1], peer[2]), land_ref.at[k], **sems))
        if incoming:
            inc.append(pltpu.make_async_remote_copy(_piece(grad_ref, weight, 2 * pos[0] + pos[1], pos[2]), land_ref.at[k], **sems))
    return out, inc


def _scatter_start(tag, weight, grad):
    n_peer = len(FLIPS)
    land = lax.empty((n_peer,) + _piece_shape(weight, grad.shape), grad.dtype)

    def body(grad_ref, land_ref, send_sems, recv_sems, grad_thru, land_thru, token):
        copies, _ = _scatter_copies(weight, grad_ref, land_ref, send_sems, recv_sems, incoming=False)
        for cp in copies:
            cp.start()
        token[...] = jnp.zeros_like(token)

    outs = pl.pallas_call(
        body, name=f"scatter_start_{tag}", in_specs=[HBM, HBM],
        out_specs=[SEM, SEM, HBM, HBM, pl.BlockSpec(memory_space=pltpu.VMEM)],
        out_shape=[pltpu.SemaphoreType.DMA((n_peer,)), pltpu.SemaphoreType.DMA((n_peer,)),
                   pltpu.HBM(grad.shape, grad.dtype), pltpu.HBM(land.shape, land.dtype), _sds((8, HEAD_DIM), F32)],
        input_output_aliases={0: 2, 1: 3},
        compiler_params=SPLIT_COPY,
    )(_in_hbm(grad), _in_hbm(land))
    return (tag, weight, outs[:4]), outs[4]


def _scatter_wait(state, after):
    tag, weight, (send_sems, recv_sems, grad_thru, land_thru) = state

    def body(grad_ref, land_ref, send_ref, recv_ref, after_ref, grad_out, land_out):
        outgoing, incoming = _scatter_copies(weight, grad_ref, land_ref, send_ref, recv_ref)
        for cp in outgoing:
            cp.wait_send()
        for cp in incoming:
            cp.wait_recv()

    return pl.pallas_call(
        body, name=f"scatter_wait_{tag}", in_specs=[HBM, HBM, SEM, SEM, ANY], out_specs=[HBM, HBM],
        out_shape=[pltpu.HBM(grad_thru.shape, grad_thru.dtype), pltpu.HBM(land_thru.shape, land_thru.dtype)],
        input_output_aliases={0: 0, 1: 1},
        compiler_params=SPLIT_COPY,
    )(grad_thru, land_thru, send_sems, recv_sems, after)


def _sum_pieces(tag, weight, grad, land, where, *, tr=128):
    _, R, C = _piece_shape(weight, grad.shape)
    tr = min(tr, R)
    nr = R // tr

    def body(where_ref, g_ref, l_ref, o_ref):
        acc = g_ref[...].astype(F32)
        for s in range(len(FLIPS)):
            acc = acc + l_ref[s].astype(F32)
        o_ref[...] = acc

    if SHARD_AXIS[weight] == 1:
        own = lambda i, where_ref: (0, (2 * where_ref[0] + where_ref[1]) * nr + i, 0)
    else:
        own = lambda i, where_ref: (0, where_ref[1] * nr + i, where_ref[0])
    grid_spec = pltpu.PrefetchScalarGridSpec(
        num_scalar_prefetch=1, grid=(nr,),
        in_specs=[pl.BlockSpec((1, tr, C), own), pl.BlockSpec((len(FLIPS), 1, tr, C), lambda i, where_ref: (0, 0, i, 0))],
        out_specs=pl.BlockSpec((1, tr, C), lambda i, where_ref: (0, where_ref[1] * nr + i, 0)))
    return pl.pallas_call(body, name=f"sum_{tag}", grid_spec=grid_spec, out_shape=_sds((1, 2 * R, C), F32),
                          compiler_params=_params("parallel"))(where, grad, land)


def _exchange_halves(shards):
    names = list(shards)
    n = len(names)
    half_sizes = [shards[k].shape[1] // 2 for k in names]

    def body(*refs):
        outs = refs[n:2 * n]
        send_sems, recv_sems = refs[2 * n:]
        x, y, c = _mesh_pos()
        half = lambda t, core: _window(outs[t], 1, pl.multiple_of(core * half_sizes[t], 8), half_sizes[t])
        sends = []
        for t in range(n):
            sends.append(pltpu.make_async_remote_copy(half(t, c), half(t, c), send_sems.at[t], recv_sems.at[t],
                                                      device_id=(x, y, 1 - c), device_id_type=MESH))
            sends[-1].start()
        for t in range(n):
            pltpu.make_async_remote_copy(half(t, c), half(t, 1 - c), send_sems.at[t], recv_sems.at[t],
                                         device_id=(x, y, 1 - c), device_id_type=MESH).wait_recv()
        for cp in sends:
            cp.wait_send()

    outs = pl.pallas_call(
        body, name="exchange_halves", in_specs=[ANY] * n, out_specs=[ANY] * n,
        out_shape=[_sds(shards[k].shape, F32) for k in names],
        input_output_aliases={t: t for t in range(n)},
        scratch_shapes=[pltpu.SemaphoreType.DMA((n,)), pltpu.SemaphoreType.DMA((n,))],
    )(*[shards[k] for k in names])
    return dict(zip(names, outs))


def _adamw_math(w, g, m, v):
    m = ADAM_B1 * m + (1.0 - ADAM_B1) * g
    v = ADAM_B2 * v + (1.0 - ADAM_B2) * jnp.square(g)
    m_hat = m / (1.0 - ADAM_B1 ** ADAM_STEP)
    v_hat = v / (1.0 - ADAM_B2 ** ADAM_STEP)
    return -ADAM_LR * (m_hat / (jnp.sqrt(v_hat) + ADAM_EPS) + ADAM_WD * w), m, v


def _adamw(name, w, g, m, v, *, tr=256):
    R, C = w.shape
    tr = min(tr, R)

    def body(w_ref, g_ref, m_ref, v_ref, d_ref, mo_ref, vo_ref):
        d_ref[...], mo_ref[...], vo_ref[...] = _adamw_math(w_ref[...], g_ref[...], m_ref[...], v_ref[...])

    spec = pl.BlockSpec((tr, C), lambda i: (i, 0))
    return pl.pallas_call(body, name=name, grid=(R // tr,), in_specs=[spec] * 4, out_specs=[spec] * 3,
                          out_shape=[_sds((R, C), F32)] * 3, compiler_params=_params("parallel"))(w, g, m, v)


SLAB_ROWS = 8
LOSS_ROW = 7


def _pack_small(d):
    pad = lambda a, width: jnp.pad(a.reshape(-1), (0, width - a.size))
    row5 = jnp.concatenate([d["swa_q_norm"].reshape(-1), d["swa_k_norm"].reshape(-1), d["ax_q_norm"].reshape(-1),
                            d["ax_k_norm"].reshape(-1), pad(d["swa_sink"], HEAD_DIM), pad(d["ret_decay_logit"], HEAD_DIM),
                            jnp.zeros((2 * HEAD_DIM,), F32)])
    return jnp.concatenate([d["norm_mix"], d["norm_mlp"], d["ret_norm"], row5[None], pad(d["t5_table"], D_MODEL)[None],
                            jnp.zeros((1, D_MODEL), F32)], axis=0)


def _unpack_small(slab):
    r5 = slab[5]
    return {
        "norm_mix": slab[0:2], "norm_mlp": slab[2:4], "ret_norm": slab[4:5],
        "swa_q_norm": r5[None, 0:128], "swa_k_norm": r5[None, 128:256], "ax_q_norm": r5[None, 256:384],
        "ax_k_norm": r5[None, 384:512], "swa_sink": r5[None, 512:512 + SWA_HEADS],
        "ret_decay_logit": r5[640:640 + 2 * RET_HEADS].reshape(1, 2, RET_HEADS),
        "t5_table": slab[6, :T5_BUCKETS * SWA_HEADS].reshape(T5_BUCKETS, SWA_HEADS),
    }


def _small_allreduce_adamw(g_slab, w_slab, m_slab, v_slab, loss_part):
    def body(g_ref, w_ref, m_ref, v_ref, lp_ref, go_ref, d_ref, mo_ref, vo_ref, gath, send_sems, recv_sems):
        pos = _mesh_pos()
        ident = lambda p: 4 * p[0] + 2 * p[1] + p[2]
        me = ident(pos)
        row = lax.broadcasted_iota(jnp.int32, (SLAB_ROWS, D_MODEL), 0)
        lane = lax.broadcasted_iota(jnp.int32, (SLAB_ROWS, D_MODEL), 1)
        loss = jnp.sum(jnp.sum(lp_ref[...], axis=0, keepdims=True), axis=1, keepdims=True) * (0.5 / D_MODEL)
        gath[me] = jnp.where(jnp.logical_and(row == LOSS_ROW, lane == 0), loss, g_ref[...])
        sends = []
        for k, f in enumerate(FLIPS):
            sends.append(pltpu.make_async_remote_copy(gath.at[me], gath.at[me], send_sems.at[k], recv_sems.at[k],
                                                      device_id=_flip(pos, f), device_id_type=MESH))
            sends[-1].start()
        for k, f in enumerate(FLIPS):
            peer = _flip(pos, f)
            pltpu.make_async_remote_copy(gath.at[me], gath.at[ident(peer)], send_sems.at[k], recv_sems.at[k],
                                         device_id=peer, device_id_type=MESH).wait_recv()
        for cp in sends:
            cp.wait_send()
        total = gath[0]
        for s in range(1, N_DEV):
            total = total + gath[s]
        go_ref[...] = total
        d_ref[...], mo_ref[...], vo_ref[...] = _adamw_math(w_ref[...], total, m_ref[...], v_ref[...])

    vmem = pl.BlockSpec(memory_space=pltpu.VMEM)
    return pl.pallas_call(
        body, name="small_allreduce_adamw", in_specs=[vmem] * 5, out_specs=[vmem] * 4,
        out_shape=[_sds((SLAB_ROWS, D_MODEL), F32)] * 4,
        scratch_shapes=[pltpu.VMEM((N_DEV, SLAB_ROWS, D_MODEL), F32),
                        pltpu.SemaphoreType.DMA((len(FLIPS),)), pltpu.SemaphoreType.DMA((len(FLIPS),))],
    )(g_slab, w_slab, m_slab, v_slab, loss_part)


def kernel(x, norm_mix, norm_mlp, w_in_even, w_out_even, ret_decay_logit, ret_norm, swa_q_norm, swa_k_norm, swa_sink, t5_table, w_in_odd, w_out_odd, ax_q_norm, ax_k_norm, w_mlp_up, w_mlp_down, loss_target, m_norm_mix, m_norm_mlp, m_w_in_even, m_w_out_even, m_ret_decay_logit, m_ret_norm, m_swa_q_norm, m_swa_k_norm, m_swa_sink, m_t5_table, m_w_in_odd, m_w_out_odd, m_ax_q_norm, m_ax_k_norm, m_w_mlp_up, m_w_mlp_down, v_norm_mix, v_norm_mlp, v_w_in_even, v_w_out_even, v_ret_decay_logit, v_ret_norm, v_swa_q_norm, v_swa_k_norm, v_swa_sink, v_t5_table, v_w_in_odd, v_w_out_odd, v_ax_q_norm, v_ax_k_norm, v_w_mlp_up, v_w_mlp_down):
    w = dict(zip(WEIGHTS, (norm_mix, norm_mlp, w_in_even, w_out_even, ret_decay_logit, ret_norm, swa_q_norm, swa_k_norm,
                           swa_sink, t5_table, w_in_odd, w_out_odd, ax_q_norm, ax_k_norm, w_mlp_up, w_mlp_down)))
    m = dict(zip(WEIGHTS, (m_norm_mix, m_norm_mlp, m_w_in_even, m_w_out_even, m_ret_decay_logit, m_ret_norm, m_swa_q_norm,
                           m_swa_k_norm, m_swa_sink, m_t5_table, m_w_in_odd, m_w_out_odd, m_ax_q_norm, m_ax_k_norm,
                           m_w_mlp_up, m_w_mlp_down)))
    v = dict(zip(WEIGHTS, (v_norm_mix, v_norm_mlp, v_w_in_even, v_w_out_even, v_ret_decay_logit, v_ret_norm, v_swa_q_norm,
                           v_swa_k_norm, v_swa_sink, v_t5_table, v_w_in_odd, v_w_out_odd, v_ax_q_norm, v_ax_k_norm,
                           v_w_mlp_up, v_w_mlp_down)))
    flat = lambda a: a.reshape(-1, a.shape[-1])

    chip = (2 * lax.axis_index("x") + lax.axis_index("y")).astype(jnp.int32)
    where = jnp.stack([chip, lax.axis_index("c").astype(jnp.int32)])

    placed = {k: _cast_place(k, w[k], where[0:1]) for k in BIG}
    (gather_first, gather_rest), gather_token = _allgather_start(
        [{"w_in_even": placed["w_in_even"]}, {k: placed[k] for k in BIG if k != "w_in_even"}])
    unstack = lambda whole: {k: (a if k.startswith("w_mlp") else a[0]) for k, a in whole.items()}

    in_flight = []

    def push(tag, weight, dw):
        state, token = _scatter_start(tag, weight, dw)
        in_flight.append(state)
        return token

    loss_part, dx, small_g = _local_step(x[0], loss_target[0], {k: w[k] for k in SMALL},
                                         unstack(_allgather_wait(gather_first, gather_token)),
                                         lambda after: unstack(_allgather_wait(gather_rest, after)), push)

    halves = {}
    for state in in_flight:
        tag, weight = state[0], state[1]
        dw, land = _scatter_wait(state, dx)
        halves[tag] = _sum_pieces(tag, weight, dw, land, where)
    reduced = _exchange_halves(halves)
    grad = {"w_in_even": reduced["in_even"], "w_out_even": reduced["out_even"],
            "w_in_odd": reduced["in_odd"], "w_out_odd": reduced["out_odd"],
            "w_mlp_up": jnp.concatenate([reduced["mlp_up0"], reduced["mlp_up1"]], axis=0),
            "w_mlp_down": jnp.concatenate([reduced["mlp_down0"], reduced["mlp_down1"]], axis=0)}
    delta, new_m, new_v = {}, {}, {}
    for k in BIG:
        d_k, m_k, v_k = _adamw(f"adamw_{k}", flat(w[k]), flat(grad[k]), flat(m[k]), flat(v[k]))
        delta[k], new_m[k], new_v[k] = d_k.reshape(w[k].shape), m_k.reshape(w[k].shape), v_k.reshape(w[k].shape)

    slabs = _small_allreduce_adamw(_pack_small(small_g), _pack_small({k: w[k] for k in SMALL}),
                                   _pack_small({k: m[k] for k in SMALL}), _pack_small({k: v[k] for k in SMALL}), loss_part)
    loss = slabs[0][LOSS_ROW, 0]
    for out, slab in zip((grad, delta, new_m, new_v), slabs):
        out.update(_unpack_small(slab))

    return (loss, dx[None], *[grad[k] for k in WEIGHTS], *[delta[k] for k in WEIGHTS],
            *[new_m[k] for k in WEIGHTS], *[new_v[k] for k in WEIGHTS])
```

```python
import functools
import math

import jax
import jax.numpy as jnp
from jax import lax
from jax.experimental import pallas as pl
from jax.experimental.pallas import tpu as pltpu

F32 = jnp.float32
BF16 = jnp.bfloat16
_MXU = BF16
_WIRE = BF16

D_MODEL = 1024
HEAD_DIM = 128
EPS = 1e-6
NEG_INF = -1e30
CHUNK = 128
RET_CHUNKS_PER_STEP = 4
GRID_W = 64
RET_HEADS, RET_DK, RET_DV = 4, 128, 256
RET_Q, RET_V = RET_HEADS * RET_DK, RET_HEADS * RET_DV
RET_THETA = 10000.0
SWA_HEADS, SWA_KV_HEADS = 8, 2
T5_BUCKETS, T5_MAX_DIST = 32, 128
AX_HEADS, AX_KV_HEADS = 8, 2
AX_THETA = 10000.0
D_FF = 4 * D_MODEL
EVEN_IN = 2 * RET_Q + 2 * RET_V + D_MODEL + 2 * SWA_KV_HEADS * HEAD_DIM
ODD_IN = D_MODEL + 2 * AX_KV_HEADS * HEAD_DIM
ATT_SCALE = HEAD_DIM ** -0.5
SCORE_SCALE_LOG2 = ATT_SCALE * math.log2(math.e)

ADAM_LR, ADAM_B1, ADAM_B2, ADAM_EPS, ADAM_WD, ADAM_STEP = 0.001, 0.9, 0.999, 1e-08, 0.01, 10

N_DEV = 8
VMEM_LIMIT_BYTES = 56 << 20
MESH = pl.DeviceIdType.MESH

_NN = (((1,), (0,)), ((), ()))
_NT = (((1,), (1,)), ((), ()))
_TN = (((0,), (0,)), ((), ()))


def _dot(a, b, dn=_NN):
    return lax.dot_general(a.astype(_MXU), b.astype(_MXU), dn, preferred_element_type=F32)


def _params(*sem):
    return pltpu.CompilerParams(dimension_semantics=sem, vmem_limit_bytes=VMEM_LIMIT_BYTES)


def _sds(shape, dtype):
    return jax.ShapeDtypeStruct(tuple(shape), dtype)


def _rowsum8(x):
    return jnp.sum(x.reshape(x.shape[0] // 8, 8, x.shape[1]), axis=0)


def _swap_halves(x, half):
    width = x.shape[1]
    lane = lax.broadcasted_iota(jnp.int32, x.shape, 1)
    up = pltpu.roll(x, width - half, axis=1)
    down = pltpu.roll(x, half, axis=1)
    return jnp.where((lane & (2 * half - 1)) < half, up, down)


def _sigmoid(x):
    return 1.0 / (1.0 + jnp.exp(-x))


def _norm_matmul(name, x, gain, w, *, tm, tn, out_dtype):
    T, K = x.shape
    N = w.shape[1]
    tm, tn = min(tm, T), min(tn, N)

    def body(x_ref, g_ref, w_ref, y_ref, h_ref, h_sc):
        @pl.when(pl.program_id(1) == 0)
        def _():
            xv = x_ref[...]
            r = lax.rsqrt(jnp.mean(xv * xv, axis=-1, keepdims=True) + EPS)
            h = (xv * r * g_ref[...]).astype(_MXU)
            h_sc[...] = h
            h_ref[...] = h
        y_ref[...] = jnp.dot(h_sc[...], w_ref[...], preferred_element_type=F32).astype(y_ref.dtype)

    return pl.pallas_call(
        body, name=name, grid=(T // tm, N // tn),
        in_specs=[pl.BlockSpec((tm, K), lambda i, j: (i, 0)),
                  pl.BlockSpec((1, K), lambda i, j: (0, 0)),
                  pl.BlockSpec((K, tn), lambda i, j: (0, j))],
        out_specs=[pl.BlockSpec((tm, tn), lambda i, j: (i, j)),
                   pl.BlockSpec((tm, K), lambda i, j: (i, 0))],
        out_shape=[_sds((T, N), out_dtype), _sds((T, K), _MXU)],
        scratch_shapes=[pltpu.VMEM((tm, K), _MXU)],
        compiler_params=_params("parallel", "arbitrary"),
    )(x, gain, w)


def _matmul_res(name, a_list, w, res, *, tm, relu2=False, target=None):
    T = res.shape[0]
    N = w.shape[1]
    K = a_list[0].shape[1]
    n_a = len(a_list)
    tm = min(tm, T)
    with_loss = target is not None

    def body(*refs):
        a_refs = refs[:n_a]
        w_refs = refs[n_a:2 * n_a]
        res_ref = refs[2 * n_a]
        acc = res_ref[...]
        for a_ref, w_ref in zip(a_refs, w_refs):
            a = a_ref[...]
            if relu2:
                a = jnp.square(jnp.maximum(a.astype(F32), 0.0))
            acc = acc + _dot(a, w_ref[...])
        if with_loss:
            tgt_ref, g_ref, g16_ref, loss_ref = refs[2 * n_a + 1:]
            diff = acc - tgt_ref[...]
            g = diff * (1.0 / N)
            g_ref[...] = g
            g16_ref[...] = g.astype(g16_ref.dtype)

            @pl.when(pl.program_id(0) == 0)
            def _():
                loss_ref[...] = jnp.zeros_like(loss_ref)
            loss_ref[...] += _rowsum8(diff * diff)
        else:
            refs[2 * n_a + 1][...] = acc

    row = lambda i: (i, 0)
    in_specs = [pl.BlockSpec((tm, K), row) for _ in a_list]
    in_specs += [pl.BlockSpec((K, N), functools.partial(lambda i, b: (b, 0), b=b)) for b in range(n_a)]
    in_specs += [pl.BlockSpec((tm, N), row)]
    args = list(a_list) + [w] * n_a + [res]
    if with_loss:
        in_specs.append(pl.BlockSpec((tm, N), row))
        args.append(target)
        out_specs = [pl.BlockSpec((tm, N), row), pl.BlockSpec((tm, N), row), pl.BlockSpec((8, N), lambda i: (0, 0))]
        out_shape = [_sds((T, N), F32), _sds((T, N), _MXU), _sds((8, N), F32)]
        sem = "arbitrary"
    else:
        out_specs = pl.BlockSpec((tm, N), row)
        out_shape = _sds((T, N), F32)
        sem = "parallel"
    return pl.pallas_call(body, name=name, grid=(T // tm,), in_specs=in_specs, out_specs=out_specs,
                          out_shape=out_shape, compiler_params=_params(sem))(*args)


def _matmul_nt(name, a, w, *, tm, tn, out_dtype, relu_of=None):
    T, K = a.shape
    N = w.shape[0]
    tm, tn = min(tm, T), min(tn, N)

    def body(*refs):
        if relu_of is None:
            a_ref, w_ref, o_ref = refs
            o_ref[...] = _dot(a_ref[...], w_ref[...], _NT).astype(o_ref.dtype)
        else:
            a_ref, w_ref, u_ref, o_ref = refs
            da = _dot(a_ref[...], w_ref[...], _NT)
            o_ref[...] = (da * (2.0 * jnp.maximum(u_ref[...].astype(F32), 0.0))).astype(o_ref.dtype)

    in_specs = [pl.BlockSpec((tm, K), lambda i, j: (i, 0)), pl.BlockSpec((tn, K), lambda i, j: (j, 0))]
    args = [a, w]
    if relu_of is not None:
        in_specs.append(pl.BlockSpec((tm, tn), lambda i, j: (i, j)))
        args.append(relu_of)
    return pl.pallas_call(body, name=name, grid=(T // tm, N // tn), in_specs=in_specs,
                          out_specs=pl.BlockSpec((tm, tn), lambda i, j: (i, j)),
                          out_shape=_sds((T, N), out_dtype),
                          compiler_params=_params("parallel", "parallel"))(*args)


def _matmul_nt_normbwd(name, dy, w, x, gain, dres, *, tm):
    T, K = dy.shape
    N = w.shape[0]
    tm = min(tm, T)

    def body(dy_ref, w_ref, x_ref, g_ref, dres_ref, dx_ref, dx16_ref, dg_ref):
        dh = _dot(dy_ref[...], w_ref[...], _NT)
        xv = x_ref[...]
        r = lax.rsqrt(jnp.mean(xv * xv, axis=-1, keepdims=True) + EPS)
        xhat = xv * r
        dxhat = dh * g_ref[...]
        dx = dres_ref[...] + r * (dxhat - xhat * jnp.mean(dxhat * xhat, axis=-1, keepdims=True))
        dx_ref[...] = dx
        dx16_ref[...] = dx.astype(dx16_ref.dtype)

        @pl.when(pl.program_id(0) == 0)
        def _():
            dg_ref[...] = jnp.zeros_like(dg_ref)
        dg_ref[...] += _rowsum8(dh * xhat)

    row = lambda i: (i, 0)
    return pl.pallas_call(
        body, name=name, grid=(T // tm,),
        in_specs=[pl.BlockSpec((tm, K), row), pl.BlockSpec((N, K), lambda i: (0, 0)),
                  pl.BlockSpec((tm, N), row), pl.BlockSpec((1, N), lambda i: (0, 0)), pl.BlockSpec((tm, N), row)],
        out_specs=[pl.BlockSpec((tm, N), row), pl.BlockSpec((tm, N), row), pl.BlockSpec((8, N), lambda i: (0, 0))],
        out_shape=[_sds((T, N), F32), _sds((T, N), _MXU), _sds((8, N), F32)],
        compiler_params=_params("arbitrary"),
    )(dy, w, x, gain, dres)


def _matmul_tn(name, a, b, *, tk, tn, tt, out_dtype, relu2=False):
    T, Ka = a.shape
    Nb = b.shape[1]
    tk, tn, tt = min(tk, Ka), min(tn, Nb), min(tt, T)
    nt = T // tt

    def body(a_ref, b_ref, o_ref, acc):
        t = pl.program_id(2)

        @pl.when(t == 0)
        def _():
            acc[...] = jnp.zeros_like(acc)
        av = a_ref[...]
        if relu2:
            av = jnp.square(jnp.maximum(av.astype(F32), 0.0))
        acc[...] += _dot(av, b_ref[...], _TN)

        @pl.when(t == nt - 1)
        def _():
            o_ref[...] = acc[...].astype(o_ref.dtype)

    return pl.pallas_call(
        body, name=name, grid=(Ka // tk, Nb // tn, nt),
        in_specs=[pl.BlockSpec((tt, tk), lambda i, j, t: (t, i)), pl.BlockSpec((tt, tn), lambda i, j, t: (t, j))],
        out_specs=pl.BlockSpec((tk, tn), lambda i, j, t: (i, j)),
        out_shape=_sds((Ka, Nb), out_dtype),
        scratch_shapes=[pltpu.VMEM((tk, tn), F32)],
        compiler_params=_params("parallel", "parallel", "arbitrary"),
    )(a, b)


def _rope_angles(pos, dim, theta):
    inv = theta ** (-jnp.arange(0, dim, 2, dtype=F32) / dim)
    return pos.astype(F32)[:, None] * inv[None, :]


def _ret_rope_tables(T):
    ang = _rope_angles(jnp.arange(T), RET_DK, RET_THETA)
    c, s = jnp.cos(ang), jnp.sin(ang)
    return jnp.concatenate([c, c], axis=1), jnp.concatenate([-s, s], axis=1)


def _axial_rope_tables(T):
    rows = T // GRID_W
    ar = _rope_angles(jnp.arange(rows), HEAD_DIM // 2, AX_THETA)
    ac = _rope_angles(jnp.arange(GRID_W), HEAD_DIM // 2, AX_THETA)
    by_row = lambda a: jnp.repeat(a, GRID_W, axis=0)
    by_col = lambda a: jnp.tile(a, (rows, 1))
    cos = jnp.concatenate([by_row(jnp.cos(ar)), by_row(jnp.cos(ar)), by_col(jnp.cos(ac)), by_col(jnp.cos(ac))], axis=1)
    sin = jnp.concatenate([by_row(-jnp.sin(ar)), by_row(jnp.sin(ar)), by_col(-jnp.sin(ac)), by_col(jnp.sin(ac))], axis=1)
    return cos, sin


(TAB_D, TAB_DT, TAB_EF, TAB_EB, TAB_A, TAB_B, TAB_CF, TAB_CB,
 TAB_RA, TAB_RB, TAB_RCF, TAB_RCB, TAB_KF, TAB_KB) = range(14)


def _retention_tables(decay_logit):
    lg = jax.nn.log_sigmoid(decay_logit.astype(F32))
    lam, mu = lg[0][:, None, None], lg[1][:, None, None]
    idx = jnp.arange(CHUNK, dtype=F32)
    diff = (idx[:, None] - idx[None, :])[None]
    df = jnp.where(diff >= 0, jnp.exp(jnp.maximum(diff, 0.0) * lam), 0.0)
    db = jnp.where(diff < 0, jnp.exp(jnp.maximum(-diff, 0.0) * mu), 0.0)
    d = df + db
    r = idx[None, :, None]
    ones = jnp.ones((1, 1, CHUNK), F32)
    a = jnp.exp((r + 1.0) * lam) * ones
    b = jnp.exp((CHUNK - r) * mu) * ones
    cf = jnp.exp((CHUNK - 1.0 - r) * lam) * ones
    cb = jnp.exp(r * mu) * ones
    full = jnp.ones((1, CHUNK, CHUNK), F32)
    kf = CHUNK * jnp.exp(CHUNK * lam) * full
    kb = CHUNK * jnp.exp(CHUNK * mu) * full
    tabs = jnp.stack([d, jnp.swapaxes(d, 1, 2), diff * df, -diff * db, a, b, cf, cb,
                      (r + 1.0) * a, (CHUNK - r) * b, (CHUNK - 1.0 - r) * cf, r * cb, kf, kb], axis=1)

    def lanes(tab):
        return jnp.transpose(tab, (1, 0, 2)).reshape(CHUNK, RET_HEADS * CHUNK)

    def dec(l):
        return jnp.exp(CHUNK * l)[:, 0, :] * jnp.ones((1, RET_DV), F32)

    weights = dict(a=lanes(a), b=lanes(b), cf=lanes(cf), cb=lanes(cb), dec_f=dec(lam), dec_b=dec(mu))
    return tabs, weights, lg


def _t5_bucket(rel):
    nb = T5_BUCKETS // 2
    max_exact = nb // 2
    ret = jnp.where(rel > 0, nb, 0)
    n = jnp.abs(rel)
    nf = jnp.maximum(n, 1).astype(F32)
    large = max_exact + (jnp.log(nf / max_exact) / math.log(T5_MAX_DIST / max_exact)
                         * (nb - max_exact)).astype(jnp.int32)
    large = jnp.minimum(large, nb - 1)
    return ret + jnp.where(n < max_exact, n, large)


def _swa_rel():
    r = jnp.arange(CHUNK)
    j = jnp.arange(3 * CHUNK)
    return j[None, :] - CHUNK - r[:, None]


def _swa_bias(t5_table):
    rel = _swa_rel()
    bucket = jnp.where(jnp.abs(rel) <= CHUNK, _t5_bucket(rel), -1).astype(jnp.int32)

    def body(tab_ref, bk_ref, o_ref):
        bk = bk_ref[...]
        for h in range(SWA_HEADS):
            pick = lambda b, acc, h=h: jnp.where(bk == b, tab_ref[b, h], acc)
            o_ref[h] = lax.fori_loop(0, T5_BUCKETS, pick, jnp.full(bk.shape, NEG_INF, F32))

    return pl.pallas_call(
        body, name="t5_bias",
        in_specs=[pl.BlockSpec(memory_space=pltpu.SMEM), pl.BlockSpec(memory_space=pltpu.VMEM)],
        out_specs=pl.BlockSpec(memory_space=pltpu.VMEM),
        out_shape=_sds((SWA_HEADS, CHUNK, 3 * CHUNK), F32),
    )(t5_table.astype(F32), bucket)


def _prep_even(proj, cos, sin, q_gain, k_gain, *, tm):
    T = proj.shape[0]
    tm = min(tm, T)

    def body(qa_ref, ka_ref, qb_ref, kb_ref, c_ref, s_ref, qg_ref, kg_ref, qr_ref, kr_ref, qn_ref, kn_ref):
        c = jnp.concatenate([c_ref[...]] * RET_HEADS, axis=1)
        s = jnp.concatenate([s_ref[...]] * RET_HEADS, axis=1)
        qa = qa_ref[...]
        qr_ref[...] = (qa * c + _swap_halves(qa, RET_DK // 2) * s).astype(qr_ref.dtype)
        ka = ka_ref[...]
        kr_ref[...] = ((ka * c + _swap_halves(ka, RET_DK // 2) * s) * (RET_DK ** -0.5)).astype(kr_ref.dtype)
        for src, gain, dst, heads in ((qb_ref, qg_ref, qn_ref, SWA_HEADS), (kb_ref, kg_ref, kn_ref, SWA_KV_HEADS)):
            for h in range(heads):
                sl = slice(h * HEAD_DIM, (h + 1) * HEAD_DIM)
                xh = src[:, sl]
                r = lax.rsqrt(jnp.mean(xh * xh, axis=-1, keepdims=True) + EPS)
                dst[:, sl] = (xh * r * gain[...]).astype(dst.dtype)

    row = lambda i: (i, 0)
    const = lambda i: (0, 0)
    return pl.pallas_call(
        body, name="prep_even", grid=(T // tm,),
        in_specs=[pl.BlockSpec((tm, RET_Q), lambda i: (i, 0)), pl.BlockSpec((tm, RET_Q), lambda i: (i, 1)),
                  pl.BlockSpec((tm, D_MODEL), lambda i: (i, 3)), pl.BlockSpec((tm, 256), lambda i: (i, 16)),
                  pl.BlockSpec((tm, RET_DK), row), pl.BlockSpec((tm, RET_DK), row),
                  pl.BlockSpec((1, HEAD_DIM), const), pl.BlockSpec((1, HEAD_DIM), const)],
        out_specs=[pl.BlockSpec((tm, RET_Q), row), pl.BlockSpec((tm, RET_Q), row),
                   pl.BlockSpec((tm, D_MODEL), row), pl.BlockSpec((tm, 256), row)],
        out_shape=[_sds((T, RET_Q), _MXU), _sds((T, RET_Q), _MXU), _sds((T, D_MODEL), _MXU), _sds((T, 256), _MXU)],
        compiler_params=_params("parallel"),
    )(proj, proj, proj, proj, cos, sin, q_gain, k_gain)


def _ret_scan(name, x, y, y_col, w_asc, dec_asc, w_desc, dec_desc):
    T = x.shape[0]
    nc = T // CHUNK
    per = min(RET_CHUNKS_PER_STEP, nc)
    nb = nc // per
    rows_per = per * CHUNK

    def body(xa_ref, ya_ref, xd_ref, yd_ref, wa_ref, da_ref, wd_ref, dd_ref, sa_out, sd_out, sa, sd):
        @pl.when(pl.program_id(0) == 0)
        def _():
            sa[...] = jnp.zeros_like(sa)
            sd[...] = jnp.zeros_like(sd)
        for step in range(per):
            for c, x_ref, y_ref, w_ref, d_ref, st, out in ((step, xa_ref, ya_ref, wa_ref, da_ref, sa, sa_out),
                                                       (per - 1 - step, xd_ref, yd_ref, wd_ref, dd_ref, sd, sd_out)):
                rows = slice(c * CHUNK, (c + 1) * CHUNK)
                out[c] = st[...].astype(out.dtype)
                for h in range(RET_HEADS):
                    ks = slice(h * RET_DK, (h + 1) * RET_DK)
                    vs = slice(h * RET_DV, (h + 1) * RET_DV)
                    u = _dot(x_ref[rows, ks].astype(F32) * w_ref[:, ks], y_ref[rows, vs], _TN)
                    st[ks, :] = st[ks, :] * d_ref[h:h + 1, :] + u

    asc = lambda i: (i, 0)
    desc = lambda i: (nb - 1 - i, 0)
    const = lambda i: (0, 0)
    return pl.pallas_call(
        body, name=name, grid=(nb,),
        in_specs=[pl.BlockSpec((rows_per, RET_Q), asc), pl.BlockSpec((rows_per, RET_V), lambda i: (i, y_col)),
                  pl.BlockSpec((rows_per, RET_Q), desc), pl.BlockSpec((rows_per, RET_V), lambda i: (nb - 1 - i, y_col)),
                  pl.BlockSpec((CHUNK, RET_Q), const), pl.BlockSpec((RET_HEADS, RET_DV), const),
                  pl.BlockSpec((CHUNK, RET_Q), const), pl.BlockSpec((RET_HEADS, RET_DV), const)],
        out_specs=[pl.BlockSpec((per, RET_Q, RET_DV), lambda i: (i, 0, 0)),
                   pl.BlockSpec((per, RET_Q, RET_DV), lambda i: (nb - 1 - i, 0, 0))],
        out_shape=[_sds((nc, RET_Q, RET_DV), _MXU), _sds((nc, RET_Q, RET_DV), _MXU)],
        scratch_shapes=[pltpu.VMEM((RET_Q, RET_DV), F32), pltpu.VMEM((RET_Q, RET_DV), F32)],
        compiler_params=_params("arbitrary"),
    )(x, y, x, y, w_asc, dec_asc, w_desc, dec_desc)


def _ret_out(qr, kr, proj, sf, sb, tabs, gain):
    T = qr.shape[0]
    nc = T // CHUNK
    per = min(RET_CHUNKS_PER_STEP, nc)
    rows_per = per * CHUNK

    def body(q_ref, k_ref, v_ref, g_ref, sf_ref, sb_ref, tab_ref, gain_ref, o_ref, y_ref):
        for c in range(per):
            rows = slice(c * CHUNK, (c + 1) * CHUNK)
            for h in range(RET_HEADS):
                ks = slice(h * RET_DK, (h + 1) * RET_DK)
                vs = slice(h * RET_DV, (h + 1) * RET_DV)
                q, k, v = q_ref[rows, ks], k_ref[rows, ks], v_ref[rows, vs]
                qf = q.astype(F32)
                a_mat = _dot(q, k, _NT) * tab_ref[h, 0]
                o = (_dot(a_mat, v) + _dot(qf * tab_ref[h, 1], sf_ref[c, ks, :]) + _dot(qf * tab_ref[h, 2], sb_ref[c, ks, :]))
                o_ref[rows, vs] = o
                r = lax.rsqrt(jnp.mean(o * o, axis=-1, keepdims=True) + EPS)
                g = g_ref[rows, vs]
                y_ref[rows, vs] = (g * _sigmoid(g) * (o * r * gain_ref[:, vs])).astype(y_ref.dtype)

    row = lambda i: (i, 0)
    st = lambda i: (i, 0, 0)
    return pl.pallas_call(
        body, name="ret_out", grid=(nc // per,),
        in_specs=[pl.BlockSpec((rows_per, RET_Q), row), pl.BlockSpec((rows_per, RET_Q), row),
                  pl.BlockSpec((rows_per, RET_V), lambda i: (i, 1)), pl.BlockSpec((rows_per, RET_V), lambda i: (i, 2)),
                  pl.BlockSpec((per, RET_Q, RET_DV), st), pl.BlockSpec((per, RET_Q, RET_DV), st),
                  pl.BlockSpec((RET_HEADS, 3, CHUNK, CHUNK), lambda i: (0, 0, 0, 0)),
                  pl.BlockSpec((1, RET_V), lambda i: (0, 0))],
        out_specs=[pl.BlockSpec((rows_per, RET_V), row), pl.BlockSpec((rows_per, RET_V), row)],
        out_shape=[_sds((T, RET_V), F32), _sds((T, RET_V), _MXU)],
        compiler_params=_params("parallel"),
    )(qr, kr, proj, proj, sf, sb, tabs, gain)


def _ret_gate_bwd(dycat, proj, ret_o, gain, *, tm):
    T = ret_o.shape[0]
    tm = min(tm, T)

    def body(dy_ref, g_ref, o_ref, gain_ref, do_ref, dg_ref, dgain_ref):
        @pl.when(pl.program_id(0) == 0)
        def _():
            dgain_ref[...] = jnp.zeros_like(dgain_ref)
        for h in range(RET_HEADS):
            vs = slice(h * RET_DV, (h + 1) * RET_DV)
            o, g, dya, gn = o_ref[:, vs], g_ref[:, vs], dy_ref[:, vs], gain_ref[:, vs]
            r = lax.rsqrt(jnp.mean(o * o, axis=-1, keepdims=True) + EPS)
            ohat = o * r
            sg = _sigmoid(g)
            dy = dya * (g * sg)
            dg_ref[:, vs] = (dya * (ohat * gn) * (sg * (1.0 + g * (1.0 - sg)))).astype(dg_ref.dtype)
            dyg = dy * gn
            do_ref[:, vs] = (r * (dyg - ohat * jnp.mean(dyg * ohat, axis=-1, keepdims=True))).astype(do_ref.dtype)
            dgain_ref[:, vs] += _rowsum8(dy * ohat)

    row = lambda i: (i, 0)
    return pl.pallas_call(
        body, name="ret_gate_bwd", grid=(T // tm,),
        in_specs=[pl.BlockSpec((tm, RET_V), row), pl.BlockSpec((tm, RET_V), lambda i: (i, 2)),
                  pl.BlockSpec((tm, RET_V), row), pl.BlockSpec((1, RET_V), lambda i: (0, 0))],
        out_specs=[pl.BlockSpec((tm, RET_V), row), pl.BlockSpec((tm, RET_V), row), pl.BlockSpec((8, RET_V), lambda i: (0, 0))],
        out_shape=[_sds((T, RET_V), _MXU), _sds((T, RET_V), _MXU), _sds((8, RET_V), F32)],
        compiler_params=_params("arbitrary"),
    )(dycat, proj, ret_o, gain)


def _ret_bwd(qr, kr, proj, g_out, sf, sb, rf, rb, tabs):
    T = qr.shape[0]
    nc = T // CHUNK
    per = min(RET_CHUNKS_PER_STEP, nc)
    rows_per = per * CHUNK

    def body(q_ref, k_ref, v_ref, g_ref, sf_ref, sb_ref, rf_ref, rb_ref, tab_ref, dq_ref, dk_ref, dv_ref, dl_ref):
        @pl.when(pl.program_id(0) == 0)
        def _():
            dl_ref[...] = jnp.zeros_like(dl_ref)
        for c in range(per):
            rows = slice(c * CHUNK, (c + 1) * CHUNK)
            for h in range(RET_HEADS):
                ks = slice(h * RET_DK, (h + 1) * RET_DK)
                vs = slice(h * RET_DV, (h + 1) * RET_DV)
                q, k, v, g = q_ref[rows, ks], k_ref[rows, ks], v_ref[rows, vs], g_ref[rows, vs]
                s_f, s_b, r_f, r_b = sf_ref[c, ks, :], sb_ref[c, ks, :], rf_ref[c, ks, :], rb_ref[c, ks, :]
                tab = lambda t, h=h: tab_ref[h, t]
                qf, kf = q.astype(F32), k.astype(F32)
                qk = _dot(q, k, _NT)
                da_raw = _dot(g, v, _NT)
                x_f, x_b = _dot(g, s_f, _NT), _dot(g, s_b, _NT)
                dq_ref[rows, ks] = _dot(da_raw * tab(TAB_D), k) + tab(TAB_A) * x_f + tab(TAB_B) * x_b
                at = _dot(k, q, _NT) * tab(TAB_DT)
                dat = _dot(v, g, _NT) * tab(TAB_DT)
                y_f, y_b = _dot(v, r_f, _NT), _dot(v, r_b, _NT)
                dk_ref[rows, ks] = _dot(dat, q) + tab(TAB_CF) * y_f + tab(TAB_CB) * y_b
                dv_ref[rows, vs] = (_dot(at, g) + _dot(kf * tab(TAB_CF), r_f) + _dot(kf * tab(TAB_CB), r_b)).astype(dv_ref.dtype)
                inner = da_raw * qk
                rs_f = r_f.astype(F32) * s_f.astype(F32)
                rs_b = r_b.astype(F32) * s_b.astype(F32)
                dl_f = (inner * tab(TAB_EF) + tab(TAB_RA) * qf * x_f + tab(TAB_RCF) * kf * y_f
                        + tab(TAB_KF) * (rs_f[:, :CHUNK] + rs_f[:, CHUNK:]))
                dl_b = (inner * tab(TAB_EB) + tab(TAB_RB) * qf * x_b + tab(TAB_RCB) * kf * y_b
                        + tab(TAB_KB) * (rs_b[:, :CHUNK] + rs_b[:, CHUNK:]))
                dl_ref[2 * h:2 * h + 1, :] += jnp.sum(dl_f, axis=0, keepdims=True)
                dl_ref[2 * h + 1:2 * h + 2, :] += jnp.sum(dl_b, axis=0, keepdims=True)

    row = lambda i: (i, 0)
    st = lambda i: (i, 0, 0)
    return pl.pallas_call(
        body, name="ret_bwd", grid=(nc // per,),
        in_specs=[pl.BlockSpec((rows_per, RET_Q), row), pl.BlockSpec((rows_per, RET_Q), row),
                  pl.BlockSpec((rows_per, RET_V), lambda i: (i, 1)), pl.BlockSpec((rows_per, RET_V), row),
                  pl.BlockSpec((per, RET_Q, RET_DV), st), pl.BlockSpec((per, RET_Q, RET_DV), st),
                  pl.BlockSpec((per, RET_Q, RET_DV), st), pl.BlockSpec((per, RET_Q, RET_DV), st),
                  pl.BlockSpec((RET_HEADS, 14, CHUNK, CHUNK), lambda i: (0, 0, 0, 0))],
        out_specs=[pl.BlockSpec((rows_per, RET_Q), row), pl.BlockSpec((rows_per, RET_Q), row),
                   pl.BlockSpec((rows_per, RET_V), row), pl.BlockSpec((8, CHUNK), lambda i: (0, 0))],
        out_shape=[_sds((T, RET_Q), F32), _sds((T, RET_Q), F32), _sds((T, RET_V), _MXU), _sds((8, CHUNK), F32)],
        compiler_params=_params("arbitrary"),
    )(qr, kr, proj, g_out, sf, sb, rf, rb, tabs)


SWA_GROUP = SWA_HEADS // SWA_KV_HEADS
SWA_COLS = SWA_GROUP * CHUNK


def _swa_stack(ref, g):
    return jnp.concatenate([ref[:, h * HEAD_DIM:(h + 1) * HEAD_DIM] for h in range(g * SWA_GROUP, (g + 1) * SWA_GROUP)], axis=0)


def _swa_probs_t(q, k_win, bias_t, sink_row, i, nb):
    st = _dot(k_win, q, _NT) * ATT_SCALE + bias_t
    key = lax.broadcasted_iota(jnp.int32, st.shape, 0)
    valid = jnp.logical_and(jnp.logical_or(key >= CHUNK, i > 0), jnp.logical_or(key < 2 * CHUNK, i < nb - 1))
    st = jnp.where(valid, st, NEG_INF)
    m = jnp.maximum(jnp.max(st, axis=0, keepdims=True), sink_row)
    p = jnp.exp(st - m)
    e_sink = jnp.exp(sink_row - m)
    inv = 1.0 / (jnp.sum(p, axis=0, keepdims=True) + e_sink)
    return p * inv, e_sink * inv


def _swa_layouts(bias, sink):
    bias_t = bias.reshape(SWA_KV_HEADS, SWA_GROUP, CHUNK, 3 * CHUNK).transpose(0, 3, 1, 2).reshape(SWA_KV_HEADS, 3 * CHUNK, SWA_COLS)
    return bias_t, jnp.repeat(sink[:, 0], CHUNK).reshape(SWA_KV_HEADS, SWA_COLS)


def _swa_window_specs(nb, width, col_block, clamp):
    prev = lambda i: (jnp.maximum(clamp(i) - 1, 0), col_block)
    cur = lambda i: (clamp(i), col_block)
    nxt = lambda i: (jnp.minimum(clamp(i) + 1, nb - 1), col_block)
    return [pl.BlockSpec((CHUNK, width), f) for f in (prev, cur, nxt)]


def _swa_fwd(qn, kn, proj, bias, sink):
    T = qn.shape[0]
    nb = T // CHUNK
    kvw = SWA_KV_HEADS * HEAD_DIM
    bias_t, sink_rows = _swa_layouts(bias, sink)

    def body(q_ref, k0, k1, k2, v0, v1, v2, bias_ref, sink_ref, y_ref):
        i = pl.program_id(0)
        for g in range(SWA_KV_HEADS):
            gs = slice(g * HEAD_DIM, (g + 1) * HEAD_DIM)
            k_win = jnp.concatenate([k0[:, gs], k1[:, gs], k2[:, gs]], axis=0)
            v_win = jnp.concatenate([v0[:, gs], v1[:, gs], v2[:, gs]], axis=0).astype(_MXU)
            pt, _ = _swa_probs_t(_swa_stack(q_ref, g), k_win, bias_ref[g], sink_ref[g:g + 1, :], i, nb)
            o = _dot(v_win, pt, _TN).T
            for hh in range(SWA_GROUP):
                h = g * SWA_GROUP + hh
                y_ref[:, h * HEAD_DIM:(h + 1) * HEAD_DIM] = o[hh * CHUNK:(hh + 1) * CHUNK].astype(y_ref.dtype)

    ident = lambda i: i
    return pl.pallas_call(
        body, name="swa_fwd", grid=(nb,),
        in_specs=[pl.BlockSpec((CHUNK, D_MODEL), lambda i: (i, 0))]
        + _swa_window_specs(nb, kvw, 0, ident) + _swa_window_specs(nb, kvw, 17, ident)
        + [pl.BlockSpec((SWA_KV_HEADS, 3 * CHUNK, SWA_COLS), lambda i: (0, 0, 0)), pl.BlockSpec((SWA_KV_HEADS, SWA_COLS), lambda i: (0, 0))],
        out_specs=pl.BlockSpec((CHUNK, D_MODEL), lambda i: (i, 0)),
        out_shape=_sds((T, D_MODEL), _MXU),
        compiler_params=_params("parallel"),
    )(qn, kn, kn, kn, proj, proj, proj, bias_t, sink_rows)


def _swa_bwd(qn, kn, proj, dycat, bias, sink):
    T = qn.shape[0]
    nb = T // CHUNK
    kvw = SWA_KV_HEADS * HEAD_DIM
    bias_t, sink_rows = _swa_layouts(bias, sink)

    def body(q_ref, k0, k1, k2, v0, v1, v2, dy_ref, bias_ref, sink_ref,
             dq_ref, dk_ref, dv_ref, dbias_ref, dsink_ref, acc_a, acc_b):
        i = pl.program_id(0)

        @pl.when(i == 0)
        def _():
            dbias_ref[...] = jnp.zeros_like(dbias_ref)
            dsink_ref[...] = jnp.zeros_like(dsink_ref)
            acc_a[...] = jnp.zeros_like(acc_a)
            acc_b[...] = jnp.zeros_like(acc_b)

        @pl.when(i < nb)
        def _():
            for g in range(SWA_KV_HEADS):
                gs = slice(g * HEAD_DIM, (g + 1) * HEAD_DIM)
                k_win = jnp.concatenate([k0[:, gs], k1[:, gs], k2[:, gs]], axis=0)
                v_win = jnp.concatenate([v0[:, gs], v1[:, gs], v2[:, gs]], axis=0).astype(_MXU)
                q, dy = _swa_stack(q_ref, g), _swa_stack(dy_ref, g)
                pt, p_sink = _swa_probs_t(q, k_win, bias_ref[g], sink_ref[g:g + 1, :], i, nb)
                dpt = _dot(v_win, dy, _NT)
                delta = jnp.sum(pt * dpt, axis=0, keepdims=True)
                dst = pt * (dpt - delta)
                dbias_ref[g] += dst
                dsink_ref[g:g + 1, :] += -p_sink * delta
                dq = (_dot(k_win, dst, _TN) * ATT_SCALE).T
                for hh in range(SWA_GROUP):
                    h = g * SWA_GROUP + hh
                    dq_ref[:, h * HEAD_DIM:(h + 1) * HEAD_DIM] = dq[hh * CHUNK:(hh + 1) * CHUNK]
                dk_win = _dot(dst, q) * ATT_SCALE
                dv_win = _dot(pt, dy)
                for win, out_ref, col0 in ((dk_win, dk_ref, 0), (dv_win, dv_ref, kvw)):
                    cs = slice(col0 + g * HEAD_DIM, col0 + (g + 1) * HEAD_DIM)
                    out_ref[:, gs] = acc_a[:, cs] + win[:CHUNK]
                    acc_a[:, cs] = acc_b[:, cs] + win[CHUNK:2 * CHUNK]
                    acc_b[:, cs] = win[2 * CHUNK:]

        @pl.when(i == nb)
        def _():
            dk_ref[...] = acc_a[:, :kvw]
            dv_ref[...] = acc_a[:, kvw:]

    clamp = lambda i: jnp.minimum(i, nb - 1)
    late = lambda i: (jnp.maximum(i - 1, 0), 0)
    bias_spec = pl.BlockSpec((SWA_KV_HEADS, 3 * CHUNK, SWA_COLS), lambda i: (0, 0, 0))
    sink_spec = pl.BlockSpec((SWA_KV_HEADS, SWA_COLS), lambda i: (0, 0))
    dq, dk, dv, dbias_t, dsink_rows = pl.pallas_call(
        body, name="swa_bwd", grid=(nb + 1,),
        in_specs=[pl.BlockSpec((CHUNK, D_MODEL), lambda i: (clamp(i), 0))]
        + _swa_window_specs(nb, kvw, 0, clamp) + _swa_window_specs(nb, kvw, 17, clamp)
        + [pl.BlockSpec((CHUNK, D_MODEL), lambda i: (clamp(i), 1)), bias_spec, sink_spec],
        out_specs=[pl.BlockSpec((CHUNK, D_MODEL), lambda i: (clamp(i), 0)),
                   pl.BlockSpec((CHUNK, kvw), late), pl.BlockSpec((CHUNK, kvw), late), bias_spec, sink_spec],
        out_shape=[_sds((T, D_MODEL), F32), _sds((T, kvw), F32), _sds((T, kvw), F32),
                   _sds((SWA_KV_HEADS, 3 * CHUNK, SWA_COLS), F32), _sds((SWA_KV_HEADS, SWA_COLS), F32)],
        scratch_shapes=[pltpu.VMEM((CHUNK, 2 * kvw), F32), pltpu.VMEM((CHUNK, 2 * kvw), F32)],
        compiler_params=_params("arbitrary"),
    )(qn, kn, kn, kn, proj, proj, proj, dycat, bias_t, sink_rows)
    dbias = dbias_t.reshape(SWA_KV_HEADS, 3 * CHUNK, SWA_GROUP, CHUNK).transpose(0, 2, 3, 1).reshape(SWA_HEADS, CHUNK, 3 * CHUNK)
    dsink = jnp.sum(dsink_rows.reshape(SWA_HEADS, CHUNK), axis=1, keepdims=True) * jnp.ones((1, HEAD_DIM), F32)
    return dq, dk, dv, dbias, dsink


def _t5_bucket_reduce(dbias, bucket):
    def body(db_ref, bk_ref, o_ref):
        bk = bk_ref[...]
        row = lax.broadcasted_iota(jnp.int32, (SWA_HEADS, HEAD_DIM), 0)
        lane = lax.broadcasted_iota(jnp.int32, (SWA_HEADS, HEAD_DIM), 1)

        def per_bucket(b, acc):
            mask = bk == b
            for h in range(SWA_HEADS):
                tot = jnp.sum(jnp.sum(jnp.where(mask, db_ref[h], 0.0), axis=0, keepdims=True), axis=1, keepdims=True)
                acc = acc + jnp.where(jnp.logical_and(row == h, lane == b), tot, 0.0)
            return acc

        o_ref[...] = lax.fori_loop(0, T5_BUCKETS, per_bucket, jnp.zeros((SWA_HEADS, HEAD_DIM), F32))

    return pl.pallas_call(body, name="t5_bucket_reduce", out_shape=_sds((SWA_HEADS, HEAD_DIM), F32),
                          compiler_params=pltpu.CompilerParams(vmem_limit_bytes=VMEM_LIMIT_BYTES))(dbias, bucket)


def _headnorm_bwd(x, dy, gain):
    r = lax.rsqrt(jnp.mean(x * x, axis=-1, keepdims=True) + EPS)
    xhat = x * r
    dyg = dy * gain
    return r * (dyg - xhat * jnp.mean(dyg * xhat, axis=-1, keepdims=True)), dy * xhat


def _post_even(proj, dqr, dkr, dva, dga, dqn, dkn, dvb, cos, sin, q_gain, k_gain, *, tm):
    T = proj.shape[0]
    tm = min(tm, T)
    kvw = SWA_KV_HEADS * HEAD_DIM

    def body(qb_ref, kb_ref, dqr_ref, dkr_ref, dva_ref, dga_ref, dqn_ref, dkn_ref, dvb_ref, c_ref, s_ref, qg_ref, kg_ref,
             dp_ref, dqg_ref, dkg_ref):
        @pl.when(pl.program_id(0) == 0)
        def _():
            dqg_ref[...] = jnp.zeros_like(dqg_ref)
            dkg_ref[...] = jnp.zeros_like(dkg_ref)
        c = jnp.concatenate([c_ref[...]] * RET_HEADS, axis=1)
        s = jnp.concatenate([s_ref[...]] * RET_HEADS, axis=1)
        dq = dqr_ref[...]
        dp_ref[:, 0:RET_Q] = (dq * c + _swap_halves(dq * s, RET_DK // 2)).astype(dp_ref.dtype)
        dk = dkr_ref[...] * (RET_DK ** -0.5)
        dp_ref[:, RET_Q:2 * RET_Q] = (dk * c + _swap_halves(dk * s, RET_DK // 2)).astype(dp_ref.dtype)
        off = 2 * RET_Q
        dp_ref[:, off:off + RET_V] = dva_ref[...].astype(dp_ref.dtype)
        dp_ref[:, off + RET_V:off + 2 * RET_V] = dga_ref[...].astype(dp_ref.dtype)
        off += 2 * RET_V
        for src, dsrc, gain, dgain, heads, base in ((qb_ref, dqn_ref, qg_ref, dqg_ref, SWA_HEADS, off),
                                                    (kb_ref, dkn_ref, kg_ref, dkg_ref, SWA_KV_HEADS, off + D_MODEL)):
            for h in range(heads):
                sl = slice(h * HEAD_DIM, (h + 1) * HEAD_DIM)
                dx, dgx = _headnorm_bwd(src[:, sl], dsrc[:, sl], gain[...])
                dp_ref[:, base + h * HEAD_DIM:base + (h + 1) * HEAD_DIM] = dx.astype(dp_ref.dtype)
                dgain[...] += _rowsum8(dgx)
        dp_ref[:, off + D_MODEL + kvw:] = dvb_ref[...].astype(dp_ref.dtype)

    row = lambda i: (i, 0)
    const = lambda i: (0, 0)
    return pl.pallas_call(
        body, name="post_even", grid=(T // tm,),
        in_specs=[pl.BlockSpec((tm, D_MODEL), lambda i: (i, 3)), pl.BlockSpec((tm, kvw), lambda i: (i, 16)),
                  pl.BlockSpec((tm, RET_Q), row), pl.BlockSpec((tm, RET_Q), row),
                  pl.BlockSpec((tm, RET_V), row), pl.BlockSpec((tm, RET_V), row),
                  pl.BlockSpec((tm, D_MODEL), row), pl.BlockSpec((tm, kvw), row), pl.BlockSpec((tm, kvw), row),
                  pl.BlockSpec((tm, RET_DK), row), pl.BlockSpec((tm, RET_DK), row),
                  pl.BlockSpec((1, HEAD_DIM), const), pl.BlockSpec((1, HEAD_DIM), const)],
        out_specs=[pl.BlockSpec((tm, EVEN_IN), row), pl.BlockSpec((8, HEAD_DIM), const), pl.BlockSpec((8, HEAD_DIM), const)],
        out_shape=[_sds((T, EVEN_IN), _MXU), _sds((8, HEAD_DIM), F32), _sds((8, HEAD_DIM), F32)],
        compiler_params=_params("arbitrary"),
    )(proj, proj, dqr, dkr, dva, dga, dqn, dkn, dvb, cos, sin, q_gain, k_gain)


def _prep_odd(proj, cos, sin, q_gain, k_gain, *, tm):
    T = proj.shape[0]
    tm = min(tm, T)
    kvw = AX_KV_HEADS * HEAD_DIM

    def body(q_ref, k_ref, v_ref, c_ref, s_ref, qg_ref, kg_ref, qx_ref, kx_ref, vx_ref):
        c, s = c_ref[...], s_ref[...]
        for src, gain, dst, heads, scale in ((q_ref, qg_ref, qx_ref, AX_HEADS, SCORE_SCALE_LOG2), (k_ref, kg_ref, kx_ref, AX_KV_HEADS, 1.0)):
            for h in range(heads):
                sl = slice(h * HEAD_DIM, (h + 1) * HEAD_DIM)
                xh = src[:, sl]
                r = lax.rsqrt(jnp.mean(xh * xh, axis=-1, keepdims=True) + EPS)
                xn = xh * r * gain[...]
                dst[:, sl] = ((xn * c + _swap_halves(xn, HEAD_DIM // 4) * s) * scale).astype(dst.dtype)
        vx_ref[...] = v_ref[...].astype(vx_ref.dtype)

    row = lambda i: (i, 0)
    const = lambda i: (0, 0)
    return pl.pallas_call(
        body, name="prep_odd", grid=(T // tm,),
        in_specs=[pl.BlockSpec((tm, D_MODEL), row), pl.BlockSpec((tm, kvw), lambda i: (i, 4)), pl.BlockSpec((tm, kvw), lambda i: (i, 5)),
                  pl.BlockSpec((tm, HEAD_DIM), row), pl.BlockSpec((tm, HEAD_DIM), row),
                  pl.BlockSpec((1, HEAD_DIM), const), pl.BlockSpec((1, HEAD_DIM), const)],
        out_specs=[pl.BlockSpec((tm, D_MODEL), row), pl.BlockSpec((tm, kvw), row), pl.BlockSpec((tm, kvw), row)],
        out_shape=[_sds((T, D_MODEL), _MXU), _sds((T, kvw), _MXU), _sds((T, kvw), _MXU)],
        compiler_params=_params("parallel"),
    )(proj, proj, proj, cos, sin, q_gain, k_gain)


def _post_odd(proj, dqxt, dkx, dvx, cos, sin, q_gain, k_gain, *, tm):
    T = proj.shape[0]
    tm = min(tm, T)
    kvw = AX_KV_HEADS * HEAD_DIM

    def body(q_ref, k_ref, dqt_ref, dk_ref, dv_ref, c_ref, s_ref, qg_ref, kg_ref, dp_ref, dqg_ref, dkg_ref):
        @pl.when(pl.program_id(0) == 0)
        def _():
            dqg_ref[...] = jnp.zeros_like(dqg_ref)
            dkg_ref[...] = jnp.zeros_like(dkg_ref)
        c, s = c_ref[...], s_ref[...]
        for src, dsrc, gain, dgain, heads, base in ((q_ref, dqt_ref, qg_ref, dqg_ref, AX_HEADS, 0),
                                                    (k_ref, dk_ref, kg_ref, dkg_ref, AX_KV_HEADS, D_MODEL)):
            for h in range(heads):
                sl = slice(h * HEAD_DIM, (h + 1) * HEAD_DIM)
                d = dsrc[sl, :].T if dsrc is dqt_ref else dsrc[:, sl]
                dn = d * c + _swap_halves(d * s, HEAD_DIM // 4)
                dx, dgx = _headnorm_bwd(src[:, sl], dn, gain[...])
                dp_ref[:, base + h * HEAD_DIM:base + (h + 1) * HEAD_DIM] = dx.astype(dp_ref.dtype)
                dgain[...] += _rowsum8(dgx)
        dp_ref[:, D_MODEL + kvw:] = dv_ref[...].astype(dp_ref.dtype)

    row = lambda i: (i, 0)
    const = lambda i: (0, 0)
    return pl.pallas_call(
        body, name="post_odd", grid=(T // tm,),
        in_specs=[pl.BlockSpec((tm, D_MODEL), row), pl.BlockSpec((tm, kvw), lambda i: (i, 4)),
                  pl.BlockSpec((D_MODEL, tm), lambda i: (0, i)), pl.BlockSpec((tm, kvw), row), pl.BlockSpec((tm, kvw), row),
                  pl.BlockSpec((tm, HEAD_DIM), row), pl.BlockSpec((tm, HEAD_DIM), row),
                  pl.BlockSpec((1, HEAD_DIM), const), pl.BlockSpec((1, HEAD_DIM), const)],
        out_specs=[pl.BlockSpec((tm, ODD_IN), row), pl.BlockSpec((8, HEAD_DIM), const), pl.BlockSpec((8, HEAD_DIM), const)],
        out_shape=[_sds((T, ODD_IN), _MXU), _sds((8, HEAD_DIM), F32), _sds((8, HEAD_DIM), F32)],
        compiler_params=_params("arbitrary"),
    )(proj, proj, dqxt, dkx, dvx, cos, sin, q_gain, k_gain)


ONES_ROWS = 16


def _flash_fwd(qx, kx, vx, *, tq, tk):
    v1t = jnp.concatenate([vx.T.reshape(AX_KV_HEADS, HEAD_DIM, vx.shape[0]),
                           jnp.ones((AX_KV_HEADS, ONES_ROWS, vx.shape[0]), vx.dtype)], axis=1)
    T = qx.shape[0]
    tq, tk = min(tq, T), min(tk, T)
    nq, nk = T // tq, T // tk
    group = AX_HEADS // AX_KV_HEADS

    def body(k_ref, v_ref, q_ref, o_ref, lse_ref, acc_sc, m_sc, l_sc):
        j = pl.program_id(2)

        @pl.when(j == 0)
        def _():
            m_sc[...] = jnp.full(m_sc.shape, NEG_INF, F32)
            l_sc[...] = jnp.zeros_like(l_sc)
            acc_sc[...] = jnp.zeros_like(acc_sc)
        k, v = k_ref[...], v_ref[0]

        def step(i, carry):
            cols = pl.ds(pl.multiple_of(i * tq, tq), tq)
            st = _dot(k, q_ref[cols, :], _NT)
            m_old = m_sc[i]
            m_new = jnp.maximum(m_old, jnp.max(st, axis=0, keepdims=True))
            p = jnp.exp2(st - m_new)
            alpha = jnp.exp2(m_old - m_new)
            pv = _dot(v, p)
            m_sc[i] = m_new
            l_sc[i] = alpha * l_sc[i] + pv[HEAD_DIM:HEAD_DIM + 1]
            acc_sc[:, cols] = alpha * acc_sc[:, cols] + pv[:HEAD_DIM]
            return carry

        lax.fori_loop(0, nq, step, 0)

        @pl.when(j == nk - 1)
        def _():
            def finish(i, carry):
                cols = pl.ds(pl.multiple_of(i * tq, tq), tq)
                o_ref[cols, :] = (acc_sc[:, cols] / l_sc[i]).T.astype(o_ref.dtype)
                lse_ref[0, i] = m_sc[i] + jnp.log2(l_sc[i])
                return carry

            lax.fori_loop(0, nq, finish, 0)

    kv = lambda g, h, j: (j, g)
    qh = lambda g, h, j: (0, g * group + h)
    o, lse = pl.pallas_call(
        body, name="flash_fwd", grid=(AX_KV_HEADS, group, nk),
        in_specs=[pl.BlockSpec((tk, HEAD_DIM), kv), pl.BlockSpec((1, HEAD_DIM + ONES_ROWS, tk), lambda g, h, j: (g, 0, j)),
                  pl.BlockSpec((T, HEAD_DIM), qh)],
        out_specs=[pl.BlockSpec((T, HEAD_DIM), qh), pl.BlockSpec((1, nq, 1, tq), lambda g, h, j: (g * group + h, 0, 0, 0))],
        out_shape=[_sds((T, D_MODEL), _MXU), _sds((AX_HEADS, nq, 1, tq), F32)],
        scratch_shapes=[pltpu.VMEM((HEAD_DIM, T), F32), pltpu.VMEM((nq, 1, tq), F32), pltpu.VMEM((nq, 1, tq), F32)],
        compiler_params=_params("parallel", "arbitrary", "arbitrary"),
    )(kx, v1t, qx)
    return o, lse.reshape(AX_HEADS, 1, T)


def _flash_bwd(qx, kx, vx, o, do, lse, *, tq, tk):
    T = qx.shape[0]
    tq, tk = min(tq, T), min(tk, T)
    nq = T // tq
    group = AX_HEADS // AX_KV_HEADS
    lse_rows = lse.reshape(AX_HEADS, nq, 1, tq)
    kxt = kx.T.reshape(AX_KV_HEADS, HEAD_DIM, T)

    def body(k_ref, kt_ref, v_ref, q_ref, o_ref, do_ref, lse_ref, dqt_ref, dk_ref, dv_ref, delta_sc):
        j = pl.program_id(2)

        @pl.when(jnp.logical_and(pl.program_id(1) == 0, j == 0))
        def _():
            dk_ref[...] = jnp.zeros_like(dk_ref)
            dv_ref[...] = jnp.zeros_like(dv_ref)

        @pl.when(j == 0)
        def _():
            dqt_ref[...] = jnp.zeros_like(dqt_ref)

            def row_delta(i, carry):
                rows = pl.ds(pl.multiple_of(i * tq, tq), tq)
                prod = do_ref[rows, :].astype(F32) * o_ref[rows, :].astype(F32)
                delta_sc[i] = jnp.sum(prod.T, axis=0, keepdims=True)
                return carry

            lax.fori_loop(0, nq, row_delta, 0)
        k, v = k_ref[...], v_ref[...]

        def step(i, carry):
            dk, dv = carry
            off = pl.multiple_of(i * tq, tq)
            q, do_blk = q_ref[pl.ds(off, tq), :], do_ref[pl.ds(off, tq), :]
            pt = jnp.exp2(_dot(k, q, _NT) - lse_ref[0, i])
            dst = pt * (_dot(v, do_blk, _NT) - delta_sc[i])
            dqt_ref[:, pl.ds(off, tq)] += _dot(kt_ref[0], dst) * ATT_SCALE
            return dk + _dot(dst, q), dv + _dot(pt, do_blk)

        zero = jnp.zeros((tk, HEAD_DIM), F32)
        dk, dv = lax.fori_loop(0, nq, step, (zero, zero))
        rows = pl.ds(pl.multiple_of(j * tk, tk), tk)
        dk_ref[rows, :] += dk * (ATT_SCALE / SCORE_SCALE_LOG2)
        dv_ref[rows, :] += dv

    kv = lambda g, h, j: (j, g)
    qh = lambda g, h, j: (0, g * group + h)
    st = lambda g, h, j: (g * group + h, 0, 0, 0)
    acc = lambda g, h, j: (0, g)
    return pl.pallas_call(
        body, name="flash_bwd", grid=(AX_KV_HEADS, group, T // tk),
        in_specs=[pl.BlockSpec((tk, HEAD_DIM), kv), pl.BlockSpec((1, HEAD_DIM, tk), lambda g, h, j: (g, 0, j)),
                  pl.BlockSpec((tk, HEAD_DIM), kv),
                  pl.BlockSpec((T, HEAD_DIM), qh), pl.BlockSpec((T, HEAD_DIM), qh), pl.BlockSpec((T, HEAD_DIM), qh),
                  pl.BlockSpec((1, nq, 1, tq), st)],
        out_specs=[pl.BlockSpec((HEAD_DIM, T), lambda g, h, j: (g * group + h, 0)),
                   pl.BlockSpec((T, HEAD_DIM), acc), pl.BlockSpec((T, HEAD_DIM), acc)],
        out_shape=[_sds((D_MODEL, T), F32), _sds((T, AX_KV_HEADS * HEAD_DIM), F32), _sds((T, AX_KV_HEADS * HEAD_DIM), F32)],
        scratch_shapes=[pltpu.VMEM((nq, 1, tq), F32)],
        compiler_params=_params("parallel", "arbitrary", "arbitrary"),
    )(kx, kxt, vx, qx, o, do, lse_rows)


TM = 1024
TM_WIDE = 512


def _mlp_fwd(tag, x, gain, w_up, w_down, target=None):
    u, h = _norm_matmul(f"mlp_up{tag}", x, gain, w_up, tm=TM_WIDE, tn=D_FF, out_dtype=_MXU)
    out = _matmul_res(f"mlp_down{tag}", [u], w_down, x, tm=TM if target is None else TM_WIDE, relu2=True, target=target)
    return out, (x, u, h)


def _local_step(x, target, p, w_first, fetch_rest, push, tokens=()):
    T = x.shape[0]
    cos_r, sin_r = _ret_rope_tables(T)
    cos_a, sin_a = _axial_rope_tables(T)
    tabs, rw, log_gamma = _retention_tables(p["ret_decay_logit"][0])
    bias = _swa_bias(p["t5_table"])
    sink = p["swa_sink"][0][:, None] * jnp.ones((1, HEAD_DIM), F32)
    nm, nl = p["norm_mix"], p["norm_mlp"]
    pending = [t for t in tokens if t is not None]

    def send(tag, weight, dw):
        token = push(tag, weight, dw[None])
        if token is not None:
            pending.append(token)

    def tied(operand):
        while pending:
            operand = operand + pending.pop()[0:1, 0:1]
        return operand

    def mlp_bwd(tag, saved, gain, w_up, w_down, dy, dy16):
        xs, u, h = saved
        du = _matmul_nt(f"mlp_down{tag}_bwd", dy16, w_down, tm=TM_WIDE, tn=D_FF, out_dtype=_MXU, relu_of=u)
        send(f"mlp_down{tag}", "w_mlp_down", _matmul_tn(f"mlp_down{tag}_dw", u, dy16, tk=2048, tn=1024, tt=1024, out_dtype=_WIRE, relu2=True))
        dx, dx16, dgain = _matmul_nt_normbwd(f"mlp_up{tag}_bwd", du, w_up, xs, tied(gain), dy, tm=TM_WIDE)
        send(f"mlp_up{tag}", "w_mlp_up", _matmul_tn(f"mlp_up{tag}_dw", h, du, tk=1024, tn=2048, tt=1024, out_dtype=_WIRE))
        return dx, dx16, dgain

    proj0, h0 = _norm_matmul("in_even", x, tied(nm[0:1]), w_first["w_in_even"], tm=TM_WIDE, tn=EVEN_IN, out_dtype=F32)
    qr, kr, qn, kn = _prep_even(proj0, cos_r, sin_r, p["swa_q_norm"], p["swa_k_norm"], tm=TM)
    sf, sb = _ret_scan("ret_scan_fwd", kr, proj0, 1, rw["cf"], rw["dec_f"], rw["cb"], rw["dec_b"])
    ret_o, ya = _ret_out(qr, kr, proj0, sf, sb, tabs[:, (TAB_D, TAB_A, TAB_B)], p["ret_norm"])
    yb = _swa_fwd(qn, kn, proj0, bias, sink)
    wf = {**w_first, **fetch_rest(yb)}
    x1 = _matmul_res("out_even", [ya, yb], wf["w_out_even"], x, tm=TM)
    x2, mlp0 = _mlp_fwd(0, x1, nl[0:1], wf["w_mlp_up"][0], wf["w_mlp_down"][0])
    proj1, h1 = _norm_matmul("in_odd", x2, nm[1:2], wf["w_in_odd"], tm=TM, tn=ODD_IN, out_dtype=F32)
    qx, kx, vx = _prep_odd(proj1, cos_a, sin_a, p["ax_q_norm"], p["ax_k_norm"], tm=TM)
    o, lse = _flash_fwd(qx, kx, vx, tq=2048, tk=1024)
    x3 = _matmul_res("out_odd", [o], wf["w_out_odd"], x2, tm=TM)
    (g4, g4_16, loss_part), mlp1 = _mlp_fwd(1, x3, nl[1:2], wf["w_mlp_up"][1], wf["w_mlp_down"][1], target=target)

    dx3, dx3_16, dnl1 = mlp_bwd(1, mlp1, nl[1:2], wf["w_mlp_up"][1], wf["w_mlp_down"][1], g4, g4_16)
    do = _matmul_nt("out_odd_bwd", dx3_16, wf["w_out_odd"], tm=TM, tn=1024, out_dtype=_MXU)
    send("out_odd", "w_out_odd", _matmul_tn("out_odd_dw", o, dx3_16, tk=1024, tn=1024, tt=1024, out_dtype=_WIRE))
    dqxt, dkx, dvx = _flash_bwd(qx, kx, vx, o, do, lse, tq=2048, tk=512)
    dproj1, dqg1, dkg1 = _post_odd(proj1, dqxt, dkx, dvx, cos_a, sin_a, tied(p["ax_q_norm"]), p["ax_k_norm"], tm=TM)
    send("in_odd", "w_in_odd", _matmul_tn("in_odd_dw", h1, dproj1, tk=1024, tn=768, tt=1024, out_dtype=_WIRE))
    dx2, dx2_16, dnm1 = _matmul_nt_normbwd("in_odd_bwd", dproj1, wf["w_in_odd"], x2, tied(nm[1:2]), dx3, tm=TM_WIDE)
    dx1, dx1_16, dnl0 = mlp_bwd(0, mlp0, nl[0:1], wf["w_mlp_up"][0], wf["w_mlp_down"][0], dx2, dx2_16)
    dycat = _matmul_nt("out_even_bwd", dx1_16, wf["w_out_even"], tm=TM, tn=2 * D_MODEL, out_dtype=F32)
    send("out_even", "w_out_even", jnp.concatenate([
        _matmul_tn("out_even_dw_ret", ya, dx1_16, tk=1024, tn=1024, tt=1024, out_dtype=_WIRE),
        _matmul_tn("out_even_dw_swa", yb, dx1_16, tk=1024, tn=1024, tt=1024, out_dtype=_WIRE)], axis=0))
    g_out, dga, dretg = _ret_gate_bwd(dycat, proj0, ret_o, tied(p["ret_norm"]), tm=TM)
    rb, rf = _ret_scan("ret_scan_bwd", qr, g_out, 0, rw["b"], rw["dec_b"], rw["a"], rw["dec_f"])
    dqr, dkr, dva, dlog = _ret_bwd(qr, kr, proj0, g_out, sf, sb, rf, rb, tabs)
    dqn, dkn, dvb, dbias, dsink = _swa_bwd(qn, kn, proj0, dycat, bias, sink)
    dt5 = _t5_bucket_reduce(dbias, _t5_bucket(_swa_rel()).astype(jnp.int32))
    dproj0, dqg0, dkg0 = _post_even(proj0, dqr, dkr, dva, dga, dqn, dkn, dvb, cos_r, sin_r,
                                    p["swa_q_norm"], p["swa_k_norm"], tm=TM_WIDE)
    send("in_even", "w_in_even", _matmul_tn("in_even_dw", h0, dproj0, tk=1024, tn=2304, tt=1024, out_dtype=_WIRE))
    dx0, _, dnm0 = _matmul_nt_normbwd("in_even_bwd", dproj0, w_first["w_in_even"], x, tied(nm[0:1]), dx1, tm=TM_WIDE)

    fold = lambda part: jnp.sum(part, axis=0)
    dlam = jnp.sum(dlog, axis=1).reshape(RET_HEADS, 2).T
    small = {
        "norm_mix": jnp.stack([fold(dnm0), fold(dnm1)]),
        "norm_mlp": jnp.stack([fold(dnl0), fold(dnl1)]),
        "ret_decay_logit": (dlam * (1.0 - jnp.exp(log_gamma)))[None],
        "ret_norm": fold(dretg)[None],
        "swa_q_norm": fold(dqg0)[None], "swa_k_norm": fold(dkg0)[None],
        "swa_sink": dsink[:, 0][None],
        "t5_table": dt5[:, :T5_BUCKETS].T,
        "ax_q_norm": fold(dqg1)[None], "ax_k_norm": fold(dkg1)[None],
    }
    return loss_part, dx0, small


BIG = ("w_in_even", "w_out_even", "w_in_odd", "w_out_odd", "w_mlp_up", "w_mlp_down")
SMALL = ("norm_mix", "norm_mlp", "ret_decay_logit", "ret_norm", "swa_q_norm", "swa_k_norm", "swa_sink", "t5_table",
         "ax_q_norm", "ax_k_norm")
WEIGHTS = ("norm_mix", "norm_mlp", "w_in_even", "w_out_even", "ret_decay_logit", "ret_norm", "swa_q_norm", "swa_k_norm",
           "swa_sink", "t5_table", "w_in_odd", "w_out_odd", "ax_q_norm", "ax_k_norm", "w_mlp_up", "w_mlp_down")
SHARD_AXIS = {"w_in_even": 2, "w_out_even": 1, "w_in_odd": 2, "w_out_odd": 1, "w_mlp_up": 2, "w_mlp_down": 1}
N_CHIPS = 4
ANY = pl.BlockSpec(memory_space=pl.ANY)
HBM = pl.BlockSpec(memory_space=pltpu.HBM)
SEM = pl.BlockSpec(memory_space=pltpu.SEMAPHORE)
SPLIT_COPY = pltpu.CompilerParams(has_side_effects=pltpu.SideEffectType.DATAFLOW_SIDE_EFFECTING)


def _in_hbm(a):
    return pltpu.with_memory_space_constraint(a, pltpu.HBM)


def _mesh_pos():
    return lax.axis_index("x"), lax.axis_index("y"), lax.axis_index("c")


def _window(ref, axis, start, size):
    idx = [slice(None)] * len(ref.shape)
    idx[axis] = pl.ds(start, size)
    return ref.at[tuple(idx)]


def _cast_place(key, shard, chip, *, tr=256):
    L, R, C = shard.shape
    tr = min(tr, R)
    axis = SHARD_AXIS[key]
    whole = tuple(d * (N_CHIPS if a == axis else 1) for a, d in enumerate(shard.shape))

    def body(chip_ref, s_ref, o_ref):
        o_ref[...] = s_ref[...].astype(o_ref.dtype)

    if axis == 2:
        out_map = lambda l, i, chip_ref: (l, i, chip_ref[0])
    else:
        out_map = lambda l, i, chip_ref: (l, i + chip_ref[0] * (R // tr), 0)
    grid_spec = pltpu.PrefetchScalarGridSpec(
        num_scalar_prefetch=1, grid=(L, R // tr),
        in_specs=[pl.BlockSpec((1, tr, C), lambda l, i, chip_ref: (l, i, 0))],
        out_specs=pl.BlockSpec((1, tr, C), out_map))
    return pl.pallas_call(body, name=f"cast_place_{key}", grid_spec=grid_spec, out_shape=_sds(whole, _MXU),
                          compiler_params=_params("parallel", "parallel"))(chip, shard)


def _gather_copies(names, refs, send_sems, recv_sems, *, outgoing=True, incoming=True):
    x, y, c = _mesh_pos()
    chips = [(1 - x, y), (x, 1 - y), (1 - x, 1 - y)]
    out, inc = [], []
    for t, key in enumerate(names):
        size = refs[t].shape[SHARD_AXIS[key]] // N_CHIPS
        slot = lambda px, py: _window(refs[t], SHARD_AXIS[key], pl.multiple_of((2 * px + py) * size, 128), size)
        for k, (px, py) in enumerate(chips):
            sems = dict(send_sem=send_sems.at[3 * t + k], recv_sem=recv_sems.at[3 * t + k], device_id=(px, py, c), device_id_type=MESH)
            if outgoing:
                out.append(pltpu.make_async_remote_copy(slot(x, y), slot(x, y), **sems))
            if incoming:
                inc.append(pltpu.make_async_remote_copy(slot(x, y), slot(px, py), **sems))
    return out, inc


def _allgather_start(groups):
    names = [list(g) for g in groups]
    flat = [g[k] for g in groups for k in g]
    n, ng = len(flat), len(groups)

    def body(*refs):
        start = 0
        for gi, keys in enumerate(names):
            copies, _ = _gather_copies(keys, refs[start:start + len(keys)], refs[n + 2 * gi], refs[n + 2 * gi + 1], incoming=False)
            for cp in copies:
                cp.start()
            start += len(keys)
        token = refs[-1]
        token[...] = jnp.zeros_like(token)

    sem_shapes = [pltpu.SemaphoreType.DMA((3 * len(keys),)) for keys in names for _ in (0, 1)]
    outs = pl.pallas_call(
        body, name="allgather_start", in_specs=[HBM] * n,
        out_specs=[SEM] * (2 * ng) + [HBM] * n + [pl.BlockSpec(memory_space=pltpu.VMEM)],
        out_shape=sem_shapes + [pltpu.HBM(a.shape, a.dtype) for a in flat] + [_sds((8, HEAD_DIM), F32)],
        input_output_aliases={t: 2 * ng + t for t in range(n)},
        compiler_params=SPLIT_COPY,
    )(*[_in_hbm(a) for a in flat])
    states, start = [], 2 * ng
    for gi, keys in enumerate(names):
        states.append((gi, keys, outs[2 * gi], outs[2 * gi + 1], outs[start:start + len(keys)]))
        start += len(keys)
    return states, outs[-1]


def _allgather_wait(state, after):
    gi, names, send_sems, recv_sems, thru = state
    n = len(names)

    def body(*refs):
        outgoing, incoming = _gather_copies(names, refs[:n], refs[n], refs[n + 1])
        for cp in outgoing:
            cp.wait_send()
        for cp in incoming:
            cp.wait_recv()

    outs = pl.pallas_call(
        body, name=f"allgather_wait_{gi}", in_specs=[HBM] * n + [SEM, SEM, ANY], out_specs=[HBM] * n,
        out_shape=[pltpu.HBM(t.shape, t.dtype) for t in thru],
        input_output_aliases={t: t for t in range(n)},
        compiler_params=SPLIT_COPY,
    )(*thru, send_sems, recv_sems, after)
    return dict(zip(names, outs))


FLIPS = [(a, b, d) for a in (0, 1) for b in (0, 1) for d in (0, 1) if (a, b, d) != (0, 0, 0)]


def _flip(pos, f):
    return tuple(1 - p if fi else p for p, fi in zip(pos, f))


def _piece_shape(weight, shape):
    out = list(shape)
    out[SHARD_AXIS[weight]] //= N_CHIPS
    out[1] //= 2
    return tuple(out)


def _piece(ref, weight, chip, core):
    piece = _piece_shape(weight, ref.shape)
    if SHARD_AXIS[weight] == 1:
        return _window(ref, 1, pl.multiple_of((2 * chip + core) * piece[1], 8), piece[1])
    return _window(_window(ref, 2, pl.multiple_of(chip * piece[2], 128), piece[2]), 1, pl.multiple_of(core * piece[1], 8), piece[1])


def _scatter_copies(weight, grad_ref, land_ref, send_sems, recv_sems, *, outgoing=True, incoming=True):
    pos = _mesh_pos()
    out, inc = [], []
    for k, f in enumerate(FLIPS):
        peer = _flip(pos, f)
        sems = dict(send_sem=send_sems.at[k], recv_sem=recv_sems.at[k], device_id=peer, device_id_type=MESH)
        if outgoing:
            out.append(pltpu.make_async_remote_copy(_piece(grad_ref, weight, 2 * peer[0] + peer[1], peer[2]), land_ref.at[k], **sems))
        if incoming:
            inc.append(pltpu.make_async_remote_copy(_piece(grad_ref, weight, 2 * pos[0] + pos[1], pos[2]), land_ref.at[k], **sems))
    return out, inc


def _scatter_start(tag, weight, grad):
    n_peer = len(FLIPS)
    land = lax.empty((n_peer,) + _piece_shape(weight, grad.shape), grad.dtype)

    def body(grad_ref, land_ref, send_sems, recv_sems, grad_thru, land_thru, token):
        copies, _ = _scatter_copies(weight, grad_ref, land_ref, send_sems, recv_sems, incoming=False)
        for cp in copies:
            cp.start()
        token[...] = jnp.zeros_like(token)

    outs = pl.pallas_call(
        body, name=f"scatter_start_{tag}", in_specs=[HBM, HBM],
        out_specs=[SEM, SEM, HBM, HBM, pl.BlockSpec(memory_space=pltpu.VMEM)],
        out_shape=[pltpu.SemaphoreType.DMA((n_peer,)), pltpu.SemaphoreType.DMA((n_peer,)),
                   pltpu.HBM(grad.shape, grad.dtype), pltpu.HBM(land.shape, land.dtype), _sds((8, HEAD_DIM), F32)],
        input_output_aliases={0: 2, 1: 3},
        compiler_params=SPLIT_COPY,
    )(_in_hbm(grad), _in_hbm(land))
    return (tag, weight, outs[:4]), outs[4]


def _scatter_wait(state, after):
    tag, weight, (send_sems, recv_sems, grad_thru, land_thru) = state

    def body(grad_ref, land_ref, send_ref, recv_ref, after_ref, grad_out, land_out):
        outgoing, incoming = _scatter_copies(weight, grad_ref, land_ref, send_ref, recv_ref)
        for cp in outgoing:
            cp.wait_send()
        for cp in incoming:
            cp.wait_recv()

    return pl.pallas_call(
        body, name=f"scatter_wait_{tag}", in_specs=[HBM, HBM, SEM, SEM, ANY], out_specs=[HBM, HBM],
        out_shape=[pltpu.HBM(grad_thru.shape, grad_thru.dtype), pltpu.HBM(land_thru.shape, land_thru.dtype)],
        input_output_aliases={0: 0, 1: 1},
        compiler_params=SPLIT_COPY,
    )(grad_thru, land_thru, send_sems, recv_sems, after)


def _sum_pieces(tag, weight, grad, land, where, *, tr=128):
    _, R, C = _piece_shape(weight, grad.shape)
    tr = min(tr, R)
    nr = R // tr

    def body(where_ref, g_ref, l_ref, o_ref):
        acc = g_ref[...].astype(F32)
        for s in range(len(FLIPS)):
            acc = acc + l_ref[s].astype(F32)
        o_ref[...] = acc

    if SHARD_AXIS[weight] == 1:
        own = lambda i, where_ref: (0, (2 * where_ref[0] + where_ref[1]) * nr + i, 0)
    else:
        own = lambda i, where_ref: (0, where_ref[1] * nr + i, where_ref[0])
    grid_spec = pltpu.PrefetchScalarGridSpec(
        num_scalar_prefetch=1, grid=(nr,),
        in_specs=[pl.BlockSpec((1, tr, C), own), pl.BlockSpec((len(FLIPS), 1, tr, C), lambda i, where_ref: (0, 0, i, 0))],
        out_specs=pl.BlockSpec((1, tr, C), lambda i, where_ref: (0, where_ref[1] * nr + i, 0)))
    return pl.pallas_call(body, name=f"sum_{tag}", grid_spec=grid_spec, out_shape=_sds((1, 2 * R, C), F32),
                          compiler_params=_params("parallel"))(where, grad, land)


def _exchange_halves(shards):
    names = list(shards)
    n = len(names)
    half_sizes = [shards[k].shape[1] // 2 for k in names]

    def body(*refs):
        outs = refs[n:2 * n]
        send_sems, recv_sems = refs[2 * n:]
        x, y, c = _mesh_pos()
        half = lambda t, core: _window(outs[t], 1, pl.multiple_of(core * half_sizes[t], 8), half_sizes[t])
        sends = []
        for t in range(n):
            sends.append(pltpu.make_async_remote_copy(half(t, c), half(t, c), send_sems.at[t], recv_sems.at[t],
                                                      device_id=(x, y, 1 - c), device_id_type=MESH))
            sends[-1].start()
        for t in range(n):
            pltpu.make_async_remote_copy(half(t, c), half(t, 1 - c), send_sems.at[t], recv_sems.at[t],
                                         device_id=(x, y, 1 - c), device_id_type=MESH).wait_recv()
        for cp in sends:
            cp.wait_send()

    outs = pl.pallas_call(
        body, name="exchange_halves", in_specs=[ANY] * n, out_specs=[ANY] * n,
        out_shape=[_sds(shards[k].shape, F32) for k in names],
        input_output_aliases={t: t for t in range(n)},
        scratch_shapes=[pltpu.SemaphoreType.DMA((n,)), pltpu.SemaphoreType.DMA((n,))],
    )(*[shards[k] for k in names])
    return dict(zip(names, outs))


def _adamw_math(w, g, m, v):
    m = ADAM_B1 * m + (1.0 - ADAM_B1) * g
    v = ADAM_B2 * v + (1.0 - ADAM_B2) * jnp.square(g)
    m_hat = m / (1.0 - ADAM_B1 ** ADAM_STEP)
    v_hat = v / (1.0 - ADAM_B2 ** ADAM_STEP)
    return -ADAM_LR * (m_hat / (jnp.sqrt(v_hat) + ADAM_EPS) + ADAM_WD * w), m, v


def _adamw(name, w, g, m, v, *, tr=256):
    R, C = w.shape
    tr = min(tr, R)

    def body(w_ref, g_ref, m_ref, v_ref, d_ref, mo_ref, vo_ref):
        d_ref[...], mo_ref[...], vo_ref[...] = _adamw_math(w_ref[...], g_ref[...], m_ref[...], v_ref[...])

    spec = pl.BlockSpec((tr, C), lambda i: (i, 0))
    return pl.pallas_call(body, name=name, grid=(R // tr,), in_specs=[spec] * 4, out_specs=[spec] * 3,
                          out_shape=[_sds((R, C), F32)] * 3, compiler_params=_params("parallel"))(w, g, m, v)


SLAB_ROWS = 8
LOSS_ROW = 7


def _pack_small(d):
    pad = lambda a, width: jnp.pad(a.reshape(-1), (0, width - a.size))
    row5 = jnp.concatenate([d["swa_q_norm"].reshape(-1), d["swa_k_norm"].reshape(-1), d["ax_q_norm"].reshape(-1),
                            d["ax_k_norm"].reshape(-1), pad(d["swa_sink"], HEAD_DIM), pad(d["ret_decay_logit"], HEAD_DIM),
                            jnp.zeros((2 * HEAD_DIM,), F32)])
    return jnp.concatenate([d["norm_mix"], d["norm_mlp"], d["ret_norm"], row5[None], pad(d["t5_table"], D_MODEL)[None],
                            jnp.zeros((1, D_MODEL), F32)], axis=0)


def _unpack_small(slab):
    r5 = slab[5]
    return {
        "norm_mix": slab[0:2], "norm_mlp": slab[2:4], "ret_norm": slab[4:5],
        "swa_q_norm": r5[None, 0:128], "swa_k_norm": r5[None, 128:256], "ax_q_norm": r5[None, 256:384],
        "ax_k_norm": r5[None, 384:512], "swa_sink": r5[None, 512:512 + SWA_HEADS],
        "ret_decay_logit": r5[640:640 + 2 * RET_HEADS].reshape(1, 2, RET_HEADS),
        "t5_table": slab[6, :T5_BUCKETS * SWA_HEADS].reshape(T5_BUCKETS, SWA_HEADS),
    }


def _small_allreduce_adamw(g_slab, w_slab, m_slab, v_slab, loss_part):
    def body(g_ref, w_ref, m_ref, v_ref, lp_ref, go_ref, d_ref, mo_ref, vo_ref, gath, send_sems, recv_sems):
        pos = _mesh_pos()
        ident = lambda p: 4 * p[0] + 2 * p[1] + p[2]
        me = ident(pos)
        row = lax.broadcasted_iota(jnp.int32, (SLAB_ROWS, D_MODEL), 0)
        lane = lax.broadcasted_iota(jnp.int32, (SLAB_ROWS, D_MODEL), 1)
        loss = jnp.sum(jnp.sum(lp_ref[...], axis=0, keepdims=True), axis=1, keepdims=True) * (0.5 / D_MODEL)
        gath[me] = jnp.where(jnp.logical_and(row == LOSS_ROW, lane == 0), loss, g_ref[...])
        sends = []
        for k, f in enumerate(FLIPS):
            sends.append(pltpu.make_async_remote_copy(gath.at[me], gath.at[me], send_sems.at[k], recv_sems.at[k],
                                                      device_id=_flip(pos, f), device_id_type=MESH))
            sends[-1].start()
        for k, f in enumerate(FLIPS):
            peer = _flip(pos, f)
            pltpu.make_async_remote_copy(gath.at[me], gath.at[ident(peer)], send_sems.at[k], recv_sems.at[k],
                                         device_id=peer, device_id_type=MESH).wait_recv()
        for cp in sends:
            cp.wait_send()
        total = gath[0]
        for s in range(1, N_DEV):
            total = total + gath[s]
        go_ref[...] = total
        d_ref[...], mo_ref[...], vo_ref[...] = _adamw_math(w_ref[...], total, m_ref[...], v_ref[...])

    vmem = pl.BlockSpec(memory_space=pltpu.VMEM)
    return pl.pallas_call(
        body, name="small_allreduce_adamw", in_specs=[vmem] * 5, out_specs=[vmem] * 4,
        out_shape=[_sds((SLAB_ROWS, D_MODEL), F32)] * 4,
        scratch_shapes=[pltpu.VMEM((N_DEV, SLAB_ROWS, D_MODEL), F32),
                        pltpu.SemaphoreType.DMA((len(FLIPS),)), pltpu.SemaphoreType.DMA((len(FLIPS),))],
    )(g_slab, w_slab, m_slab, v_slab, loss_part)


def kernel(x, norm_mix, norm_mlp, w_in_even, w_out_even, ret_decay_logit, ret_norm, swa_q_norm, swa_k_norm, swa_sink, t5_table, w_in_odd, w_out_odd, ax_q_norm, ax_k_norm, w_mlp_up, w_mlp_down, loss_target, m_norm_mix, m_norm_mlp, m_w_in_even, m_w_out_even, m_ret_decay_logit, m_ret_norm, m_swa_q_norm, m_swa_k_norm, m_swa_sink, m_t5_table, m_w_in_odd, m_w_out_odd, m_ax_q_norm, m_ax_k_norm, m_w_mlp_up, m_w_mlp_down, v_norm_mix, v_norm_mlp, v_w_in_even, v_w_out_even, v_ret_decay_logit, v_ret_norm, v_swa_q_norm, v_swa_k_norm, v_swa_sink, v_t5_table, v_w_in_odd, v_w_out_odd, v_ax_q_norm, v_ax_k_norm, v_w_mlp_up, v_w_mlp_down):
    w = dict(zip(WEIGHTS, (norm_mix, norm_mlp, w_in_even, w_out_even, ret_decay_logit, ret_norm, swa_q_norm, swa_k_norm,
                           swa_sink, t5_table, w_in_odd, w_out_odd, ax_q_norm, ax_k_norm, w_mlp_up, w_mlp_down)))
    m = dict(zip(WEIGHTS, (m_norm_mix, m_norm_mlp, m_w_in_even, m_w_out_even, m_ret_decay_logit, m_ret_norm, m_swa_q_norm,
                           m_swa_k_norm, m_swa_sink, m_t5_table, m_w_in_odd, m_w_out_odd, m_ax_q_norm, m_ax_k_norm,
                           m_w_mlp_up, m_w_mlp_down)))
    v = dict(zip(WEIGHTS, (v_norm_mix, v_norm_mlp, v_w_in_even, v_w_out_even, v_ret_decay_logit, v_ret_norm, v_swa_q_norm,
                           v_swa_k_norm, v_swa_sink, v_t5_table, v_w_in_odd, v_w_out_odd, v_ax_q_norm, v_ax_k_norm,
                           v_w_mlp_up, v_w_mlp_down)))
    flat = lambda a: a.reshape(-1, a.shape[-1])

    chip = (2 * lax.axis_index("x") + lax.axis_index("y")).astype(jnp.int32)
    where = jnp.stack([chip, lax.axis_index("c").astype(jnp.int32)])

    placed = {k: _cast_place(k, w[k], where[0:1]) for k in BIG}
    (gather_first, gather_rest), gather_token = _allgather_start(
        [{"w_in_even": placed["w_in_even"]}, {k: placed[k] for k in BIG if k != "w_in_even"}])
    unstack = lambda whole: {k: (a if k.startswith("w_mlp") else a[0]) for k, a in whole.items()}

    in_flight = []

    def push(tag, weight, dw):
        state, token = _scatter_start(tag, weight, dw)
        in_flight.append(state)
        return token

    loss_part, dx, small_g = _local_step(x[0], loss_target[0], {k: w[k] for k in SMALL},
                                         unstack(_allgather_wait(gather_first, gather_token)),
                                         lambda after: unstack(_allgather_wait(gather_rest, after)), push)

    halves = {}
    for state in in_flight:
        tag, weight = state[0], state[1]
        dw, land = _scatter_wait(state, dx)
        halves[tag] = _sum_pieces(tag, weight, dw, land, where)
    reduced = _exchange_halves(halves)
    grad = {"w_in_even": reduced["in_even"], "w_out_even": reduced["out_even"],
            "w_in_odd": reduced["in_odd"], "w_out_odd": reduced["out_odd"],
            "w_mlp_up": jnp.concatenate([reduced["mlp_up0"], reduced["mlp_up1"]], axis=0),
            "w_mlp_down": jnp.concatenate([reduced["mlp_down0"], reduced["mlp_down1"]], axis=0)}
    delta, new_m, new_v = {}, {}, {}
    for k in BIG:
        d_k, m_k, v_k = _adamw(f"adamw_{k}", flat(w[k]), flat(grad[k]), flat(m[k]), flat(v[k]))
        delta[k], new_m[k], new_v[k] = d_k.reshape(w[k].shape), m_k.reshape(w[k].shape), v_k.reshape(w[k].shape)

    slabs = _small_allreduce_adamw(_pack_small(small_g), _pack_small({k: w[k] for k in SMALL}),
                                   _pack_small({k: m[k] for k in SMALL}), _pack_small({k: v[k] for k in SMALL}), loss_part)
    loss = slabs[0][LOSS_ROW, 0]
    for out, slab in zip((grad, delta, new_m, new_v), slabs):
        out.update(_unpack_small(slab))

    return (loss, dx[None], *[grad[k] for k in WEIGHTS], *[delta[k] for k in WEIGHTS],
            *[new_m[k] for k in WEIGHTS], *[new_v[k] for k in WEIGHTS])
```

```python
import functools
import math

import jax
import jax.numpy as jnp
from jax import lax
from jax.experimental import pallas as pl
from jax.experimental.pallas import tpu as pltpu

F32 = jnp.float32
BF16 = jnp.bfloat16
_MXU = BF16
_WIRE = BF16

D_MODEL = 1024
HEAD_DIM = 128
EPS = 1e-6
NEG_INF = -1e30
CHUNK = 128
RET_CHUNKS_PER_STEP = 4
GRID_W = 64
RET_HEADS, RET_DK, RET_DV = 4, 128, 256
RET_Q, RET_V = RET_HEADS * RET_DK, RET_HEADS * RET_DV
RET_THETA = 10000.0
SWA_HEADS, SWA_KV_HEADS = 8, 2
T5_BUCKETS, T5_MAX_DIST = 32, 128
AX_HEADS, AX_KV_HEADS = 8, 2
AX_THETA = 10000.0
D_FF = 4 * D_MODEL
EVEN_IN = 2 * RET_Q + 2 * RET_V + D_MODEL + 2 * SWA_KV_HEADS * HEAD_DIM
ODD_IN = D_MODEL + 2 * AX_KV_HEADS * HEAD_DIM
ATT_SCALE = HEAD_DIM ** -0.5
SCORE_SCALE_LOG2 = ATT_SCALE * math.log2(math.e)

ADAM_LR, ADAM_B1, ADAM_B2, ADAM_EPS, ADAM_WD, ADAM_STEP = 0.001, 0.9, 0.999, 1e-08, 0.01, 10

N_DEV = 8
VMEM_LIMIT_BYTES = 56 << 20
MESH = pl.DeviceIdType.MESH

_NN = (((1,), (0,)), ((), ()))
_NT = (((1,), (1,)), ((), ()))
_TN = (((0,), (0,)), ((), ()))


def _dot(a, b, dn=_NN):
    return lax.dot_general(a.astype(_MXU), b.astype(_MXU), dn, preferred_element_type=F32)


def _params(*sem):
    return pltpu.CompilerParams(dimension_semantics=sem, vmem_limit_bytes=VMEM_LIMIT_BYTES)


def _sds(shape, dtype):
    return jax.ShapeDtypeStruct(tuple(shape), dtype)


def _rowsum8(x):
    return jnp.sum(x.reshape(x.shape[0] // 8, 8, x.shape[1]), axis=0)


def _swap_halves(x, half):
    width = x.shape[1]
    lane = lax.broadcasted_iota(jnp.int32, x.shape, 1)
    up = pltpu.roll(x, width - half, axis=1)
    down = pltpu.roll(x, half, axis=1)
    return jnp.where((lane & (2 * half - 1)) < half, up, down)


def _sigmoid(x):
    return 1.0 / (1.0 + jnp.exp(-x))


def _weight_spec(w, block, index_map):
    if isinstance(w, tuple):
        stacked, layer = w
        return stacked, pl.BlockSpec((None,) + block, lambda *idx: (layer,) + index_map(*idx))
    return w, pl.BlockSpec(block, index_map)


def _weight_dims(w):
    return (w[0] if isinstance(w, tuple) else w).shape[-2:]


def _norm_matmul(name, x, gain, w, *, tm, tn, out_dtype):
    T, K = x.shape
    N = _weight_dims(w)[1]
    tm, tn = min(tm, T), min(tn, N)
    w, w_spec = _weight_spec(w, (K, tn), lambda i, j: (0, j))

    def body(x_ref, g_ref, w_ref, y_ref, h_ref, h_sc):
        @pl.when(pl.program_id(1) == 0)
        def _():
            xv = x_ref[...]
            r = lax.rsqrt(jnp.mean(xv * xv, axis=-1, keepdims=True) + EPS)
            h = (xv * r * g_ref[...]).astype(_MXU)
            h_sc[...] = h
            h_ref[...] = h
        y_ref[...] = jnp.dot(h_sc[...], w_ref[...], preferred_element_type=F32).astype(y_ref.dtype)

    return pl.pallas_call(
        body, name=name, grid=(T // tm, N // tn),
        in_specs=[pl.BlockSpec((tm, K), lambda i, j: (i, 0)),
                  pl.BlockSpec((1, K), lambda i, j: (0, 0)),
                  w_spec],
        out_specs=[pl.BlockSpec((tm, tn), lambda i, j: (i, j)),
                   pl.BlockSpec((tm, K), lambda i, j: (i, 0))],
        out_shape=[_sds((T, N), out_dtype), _sds((T, K), _MXU)],
        scratch_shapes=[pltpu.VMEM((tm, K), _MXU)],
        compiler_params=_params("parallel", "arbitrary"),
    )(x, gain, w)


def _matmul_res(name, a_list, w, res, *, tm, relu2=False, target=None):
    T = res.shape[0]
    N = _weight_dims(w)[1]
    K = a_list[0].shape[1]
    n_a = len(a_list)
    tm = min(tm, T)
    with_loss = target is not None
    w_specs = [_weight_spec(w, (K, N), functools.partial(lambda i, b: (b, 0), b=b)) for b in range(n_a)]

    def body(*refs):
        a_refs = refs[:n_a]
        w_refs = refs[n_a:2 * n_a]
        res_ref = refs[2 * n_a]
        acc = res_ref[...]
        for a_ref, w_ref in zip(a_refs, w_refs):
            a = a_ref[...]
            if relu2:
                a = jnp.square(jnp.maximum(a.astype(F32), 0.0))
            acc = acc + _dot(a, w_ref[...])
        if with_loss:
            tgt_ref, g_ref, g16_ref, loss_ref = refs[2 * n_a + 1:]
            diff = acc - tgt_ref[...]
            g = diff * (1.0 / N)
            g_ref[...] = g
            g16_ref[...] = g.astype(g16_ref.dtype)

            @pl.when(pl.program_id(0) == 0)
            def _():
                loss_ref[...] = jnp.zeros_like(loss_ref)
            loss_ref[...] += _rowsum8(diff * diff)
        else:
            refs[2 * n_a + 1][...] = acc

    row = lambda i: (i, 0)
    in_specs = [pl.BlockSpec((tm, K), row) for _ in a_list]
    in_specs += [spec for _, spec in w_specs]
    in_specs += [pl.BlockSpec((tm, N), row)]
    args = list(a_list) + [arr for arr, _ in w_specs] + [res]
    if with_loss:
        in_specs.append(pl.BlockSpec((tm, N), row))
        args.append(target)
        out_specs = [pl.BlockSpec((tm, N), row), pl.BlockSpec((tm, N), row), pl.BlockSpec((8, N), lambda i: (0, 0))]
        out_shape = [_sds((T, N), F32), _sds((T, N), _MXU), _sds((8, N), F32)]
        sem = "arbitrary"
    else:
        out_specs = pl.BlockSpec((tm, N), row)
        out_shape = _sds((T, N), F32)
        sem = "parallel"
    return pl.pallas_call(body, name=name, grid=(T // tm,), in_specs=in_specs, out_specs=out_specs,
                          out_shape=out_shape, compiler_params=_params(sem))(*args)


def _matmul_nt(name, a, w, *, tm, tn, out_dtype, relu_of=None):
    T, K = a.shape
    N = _weight_dims(w)[0]
    tm, tn = min(tm, T), min(tn, N)
    w, w_spec = _weight_spec(w, (tn, K), lambda i, j: (j, 0))

    def body(*refs):
        if relu_of is None:
            a_ref, w_ref, o_ref = refs
            o_ref[...] = _dot(a_ref[...], w_ref[...], _NT).astype(o_ref.dtype)
        else:
            a_ref, w_ref, u_ref, o_ref = refs
            da = _dot(a_ref[...], w_ref[...], _NT)
            o_ref[...] = (da * (2.0 * jnp.maximum(u_ref[...].astype(F32), 0.0))).astype(o_ref.dtype)

    in_specs = [pl.BlockSpec((tm, K), lambda i, j: (i, 0)), w_spec]
    args = [a, w]
    if relu_of is not None:
        in_specs.append(pl.BlockSpec((tm, tn), lambda i, j: (i, j)))
        args.append(relu_of)
    return pl.pallas_call(body, name=name, grid=(T // tm, N // tn), in_specs=in_specs,
                          out_specs=pl.BlockSpec((tm, tn), lambda i, j: (i, j)),
                          out_shape=_sds((T, N), out_dtype),
                          compiler_params=_params("parallel", "parallel"))(*args)


def _matmul_nt_normbwd(name, dy, w, x, gain, dres, *, tm):
    T, K = dy.shape
    N = _weight_dims(w)[0]
    tm = min(tm, T)
    w, w_spec = _weight_spec(w, (N, K), lambda i: (0, 0))

    def body(dy_ref, w_ref, x_ref, g_ref, dres_ref, dx_ref, dx16_ref, dg_ref):
        dh = _dot(dy_ref[...], w_ref[...], _NT)
        xv = x_ref[...]
        r = lax.rsqrt(jnp.mean(xv * xv, axis=-1, keepdims=True) + EPS)
        xhat = xv * r
        dxhat = dh * g_ref[...]
        dx = dres_ref[...] + r * (dxhat - xhat * jnp.mean(dxhat * xhat, axis=-1, keepdims=True))
        dx_ref[...] = dx
        dx16_ref[...] = dx.astype(dx16_ref.dtype)

        @pl.when(pl.program_id(0) == 0)
        def _():
            dg_ref[...] = jnp.zeros_like(dg_ref)
        dg_ref[...] += _rowsum8(dh * xhat)

    row = lambda i: (i, 0)
    return pl.pallas_call(
        body, name=name, grid=(T // tm,),
        in_specs=[pl.BlockSpec((tm, K), row), w_spec,
                  pl.BlockSpec((tm, N), row), pl.BlockSpec((1, N), lambda i: (0, 0)), pl.BlockSpec((tm, N), row)],
        out_specs=[pl.BlockSpec((tm, N), row), pl.BlockSpec((tm, N), row), pl.BlockSpec((8, N), lambda i: (0, 0))],
        out_shape=[_sds((T, N), F32), _sds((T, N), _MXU), _sds((8, N), F32)],
        compiler_params=_params("arbitrary"),
    )(dy, w, x, gain, dres)


def _matmul_tn(name, a, b, *, tk, tn, tt, out_dtype, relu2=False):
    T, Ka = a.shape
    Nb = b.shape[1]
    tk, tn, tt = min(tk, Ka), min(tn, Nb), min(tt, T)
    nt = T // tt

    def body(a_ref, b_ref, o_ref, acc):
        t = pl.program_id(2)

        @pl.when(t == 0)
        def _():
            acc[...] = jnp.zeros_like(acc)
        av = a_ref[...]
        if relu2:
            av = jnp.square(jnp.maximum(av.astype(F32), 0.0))
        acc[...] += _dot(av, b_ref[...], _TN)

        @pl.when(t == nt - 1)
        def _():
            o_ref[...] = acc[...].astype(o_ref.dtype)

    return pl.pallas_call(
        body, name=name, grid=(Ka // tk, Nb // tn, nt),
        in_specs=[pl.BlockSpec((tt, tk), lambda i, j, t: (t, i)), pl.BlockSpec((tt, tn), lambda i, j, t: (t, j))],
        out_specs=pl.BlockSpec((tk, tn), lambda i, j, t: (i, j)),
        out_shape=_sds((Ka, Nb), out_dtype),
        scratch_shapes=[pltpu.VMEM((tk, tn), F32)],
        compiler_params=_params("parallel", "parallel", "arbitrary"),
    )(a, b)


def _rope_angles(pos, dim, theta):
    inv = theta ** (-jnp.arange(0, dim, 2, dtype=F32) / dim)
    return pos.astype(F32)[:, None] * inv[None, :]


def _ret_rope_tables(T):
    ang = _rope_angles(jnp.arange(T), RET_DK, RET_THETA)
    c, s = jnp.cos(ang), jnp.sin(ang)
    return jnp.concatenate([c, c], axis=1), jnp.concatenate([-s, s], axis=1)


def _axial_rope_tables(T):
    rows = T // GRID_W
    ar = _rope_angles(jnp.arange(rows), HEAD_DIM // 2, AX_THETA)
    ac = _rope_angles(jnp.arange(GRID_W), HEAD_DIM // 2, AX_THETA)
    by_row = lambda a: jnp.repeat(a, GRID_W, axis=0)
    by_col = lambda a: jnp.tile(a, (rows, 1))
    cos = jnp.concatenate([by_row(jnp.cos(ar)), by_row(jnp.cos(ar)), by_col(jnp.cos(ac)), by_col(jnp.cos(ac))], axis=1)
    sin = jnp.concatenate([by_row(-jnp.sin(ar)), by_row(jnp.sin(ar)), by_col(-jnp.sin(ac)), by_col(jnp.sin(ac))], axis=1)
    return cos, sin


(TAB_D, TAB_DT, TAB_EF, TAB_EB, TAB_A, TAB_B, TAB_CF, TAB_CB,
 TAB_RA, TAB_RB, TAB_RCF, TAB_RCB, TAB_KF, TAB_KB) = range(14)


def _retention_tables(decay_logit):
    lg = jax.nn.log_sigmoid(decay_logit.astype(F32))
    lam, mu = lg[0][:, None, None], lg[1][:, None, None]
    idx = jnp.arange(CHUNK, dtype=F32)
    diff = (idx[:, None] - idx[None, :])[None]
    df = jnp.where(diff >= 0, jnp.exp(jnp.maximum(diff, 0.0) * lam), 0.0)
    db = jnp.where(diff < 0, jnp.exp(jnp.maximum(-diff, 0.0) * mu), 0.0)
    d = df + db
    r = idx[None, :, None]
    ones = jnp.ones((1, 1, CHUNK), F32)
    a = jnp.exp((r + 1.0) * lam) * ones
    b = jnp.exp((CHUNK - r) * mu) * ones
    cf = jnp.exp((CHUNK - 1.0 - r) * lam) * ones
    cb = jnp.exp(r * mu) * ones
    full = jnp.ones((1, CHUNK, CHUNK), F32)
    kf = CHUNK * jnp.exp(CHUNK * lam) * full
    kb = CHUNK * jnp.exp(CHUNK * mu) * full
    tabs = jnp.stack([d, jnp.swapaxes(d, 1, 2), diff * df, -diff * db, a, b, cf, cb,
                      (r + 1.0) * a, (CHUNK - r) * b, (CHUNK - 1.0 - r) * cf, r * cb, kf, kb], axis=1)

    def lanes(tab):
        return jnp.transpose(tab, (1, 0, 2)).reshape(CHUNK, RET_HEADS * CHUNK)

    def dec(l):
        return jnp.exp(CHUNK * l)[:, 0, :] * jnp.ones((1, RET_DV), F32)

    weights = dict(a=lanes(a), b=lanes(b), cf=lanes(cf), cb=lanes(cb), dec_f=dec(lam), dec_b=dec(mu))
    return tabs, weights, lg


def _t5_bucket(rel):
    nb = T5_BUCKETS // 2
    max_exact = nb // 2
    ret = jnp.where(rel > 0, nb, 0)
    n = jnp.abs(rel)
    nf = jnp.maximum(n, 1).astype(F32)
    large = max_exact + (jnp.log(nf / max_exact) / math.log(T5_MAX_DIST / max_exact)
                         * (nb - max_exact)).astype(jnp.int32)
    large = jnp.minimum(large, nb - 1)
    return ret + jnp.where(n < max_exact, n, large)


def _swa_rel():
    r = jnp.arange(CHUNK)
    j = jnp.arange(3 * CHUNK)
    return j[None, :] - CHUNK - r[:, None]


def _swa_bias(t5_table):
    rel = _swa_rel()
    bucket = jnp.where(jnp.abs(rel) <= CHUNK, _t5_bucket(rel), -1).astype(jnp.int32)

    def body(tab_ref, bk_ref, o_ref):
        bk = bk_ref[...]
        for h in range(SWA_HEADS):
            pick = lambda b, acc, h=h: jnp.where(bk == b, tab_ref[b, h], acc)
            o_ref[h] = lax.fori_loop(0, T5_BUCKETS, pick, jnp.full(bk.shape, NEG_INF, F32))

    return pl.pallas_call(
        body, name="t5_bias",
        in_specs=[pl.BlockSpec(memory_space=pltpu.SMEM), pl.BlockSpec(memory_space=pltpu.VMEM)],
        out_specs=pl.BlockSpec(memory_space=pltpu.VMEM),
        out_shape=_sds((SWA_HEADS, CHUNK, 3 * CHUNK), F32),
    )(t5_table.astype(F32), bucket)


def _prep_even(proj, cos, sin, q_gain, k_gain, *, tm):
    T = proj.shape[0]
    tm = min(tm, T)

    def body(qa_ref, ka_ref, qb_ref, kb_ref, c_ref, s_ref, qg_ref, kg_ref, qr_ref, kr_ref, qn_ref, kn_ref):
        c = jnp.concatenate([c_ref[...]] * RET_HEADS, axis=1)
        s = jnp.concatenate([s_ref[...]] * RET_HEADS, axis=1)
        qa = qa_ref[...]
        qr_ref[...] = (qa * c + _swap_halves(qa, RET_DK // 2) * s).astype(qr_ref.dtype)
        ka = ka_ref[...]
        kr_ref[...] = ((ka * c + _swap_halves(ka, RET_DK // 2) * s) * (RET_DK ** -0.5)).astype(kr_ref.dtype)
        for src, gain, dst, heads in ((qb_ref, qg_ref, qn_ref, SWA_HEADS), (kb_ref, kg_ref, kn_ref, SWA_KV_HEADS)):
            for h in range(heads):
                sl = slice(h * HEAD_DIM, (h + 1) * HEAD_DIM)
                xh = src[:, sl]
                r = lax.rsqrt(jnp.mean(xh * xh, axis=-1, keepdims=True) + EPS)
                dst[:, sl] = (xh * r * gain[...]).astype(dst.dtype)

    row = lambda i: (i, 0)
    const = lambda i: (0, 0)
    return pl.pallas_call(
        body, name="prep_even", grid=(T // tm,),
        in_specs=[pl.BlockSpec((tm, RET_Q), lambda i: (i, 0)), pl.BlockSpec((tm, RET_Q), lambda i: (i, 1)),
                  pl.BlockSpec((tm, D_MODEL), lambda i: (i, 3)), pl.BlockSpec((tm, 256), lambda i: (i, 16)),
                  pl.BlockSpec((tm, RET_DK), row), pl.BlockSpec((tm, RET_DK), row),
                  pl.BlockSpec((1, HEAD_DIM), const), pl.BlockSpec((1, HEAD_DIM), const)],
        out_specs=[pl.BlockSpec((tm, RET_Q), row), pl.BlockSpec((tm, RET_Q), row),
                   pl.BlockSpec((tm, D_MODEL), row), pl.BlockSpec((tm, 256), row)],
        out_shape=[_sds((T, RET_Q), _MXU), _sds((T, RET_Q), _MXU), _sds((T, D_MODEL), _MXU), _sds((T, 256), _MXU)],
        compiler_params=_params("parallel"),
    )(proj, proj, proj, proj, cos, sin, q_gain, k_gain)


def _ret_scan(name, x, y, y_col, w_asc, dec_asc, w_desc, dec_desc):
    T = x.shape[0]
    nc = T // CHUNK
    per = min(RET_CHUNKS_PER_STEP, nc)
    nb = nc // per
    rows_per = per * CHUNK

    def body(xa_ref, ya_ref, xd_ref, yd_ref, wa_ref, da_ref, wd_ref, dd_ref, sa_out, sd_out, sa, sd):
        @pl.when(pl.program_id(0) == 0)
        def _():
            sa[...] = jnp.zeros_like(sa)
            sd[...] = jnp.zeros_like(sd)
        for step in range(per):
            for c, x_ref, y_ref, w_ref, d_ref, st, out in ((step, xa_ref, ya_ref, wa_ref, da_ref, sa, sa_out),
                                                       (per - 1 - step, xd_ref, yd_ref, wd_ref, dd_ref, sd, sd_out)):
                rows = slice(c * CHUNK, (c + 1) * CHUNK)
                out[c] = st[...].astype(out.dtype)
                for h in range(RET_HEADS):
                    ks = slice(h * RET_DK, (h + 1) * RET_DK)
                    vs = slice(h * RET_DV, (h + 1) * RET_DV)
                    u = _dot(x_ref[rows, ks].astype(F32) * w_ref[:, ks], y_ref[rows, vs], _TN)
                    st[ks, :] = st[ks, :] * d_ref[h:h + 1, :] + u

    asc = lambda i: (i, 0)
    desc = lambda i: (nb - 1 - i, 0)
    const = lambda i: (0, 0)
    return pl.pallas_call(
        body, name=name, grid=(nb,),
        in_specs=[pl.BlockSpec((rows_per, RET_Q), asc), pl.BlockSpec((rows_per, RET_V), lambda i: (i, y_col)),
                  pl.BlockSpec((rows_per, RET_Q), desc), pl.BlockSpec((rows_per, RET_V), lambda i: (nb - 1 - i, y_col)),
                  pl.BlockSpec((CHUNK, RET_Q), const), pl.BlockSpec((RET_HEADS, RET_DV), const),
                  pl.BlockSpec((CHUNK, RET_Q), const), pl.BlockSpec((RET_HEADS, RET_DV), const)],
        out_specs=[pl.BlockSpec((per, RET_Q, RET_DV), lambda i: (i, 0, 0)),
                   pl.BlockSpec((per, RET_Q, RET_DV), lambda i: (nb - 1 - i, 0, 0))],
        out_shape=[_sds((nc, RET_Q, RET_DV), _MXU), _sds((nc, RET_Q, RET_DV), _MXU)],
        scratch_shapes=[pltpu.VMEM((RET_Q, RET_DV), F32), pltpu.VMEM((RET_Q, RET_DV), F32)],
        compiler_params=_params("arbitrary"),
    )(x, y, x, y, w_asc, dec_asc, w_desc, dec_desc)


def _ret_out(qr, kr, proj, sf, sb, tabs, gain):
    T = qr.shape[0]
    nc = T // CHUNK
    per = min(RET_CHUNKS_PER_STEP, nc)
    rows_per = per * CHUNK

    def body(q_ref, k_ref, v_ref, g_ref, sf_ref, sb_ref, tab_ref, gain_ref, o_ref, y_ref):
        for c in range(per):
            rows = slice(c * CHUNK, (c + 1) * CHUNK)
            for h in range(RET_HEADS):
                ks = slice(h * RET_DK, (h + 1) * RET_DK)
                vs = slice(h * RET_DV, (h + 1) * RET_DV)
                q, k, v = q_ref[rows, ks], k_ref[rows, ks], v_ref[rows, vs]
                qf = q.astype(F32)
                a_mat = _dot(q, k, _NT) * tab_ref[h, 0]
                o = (_dot(a_mat, v) + _dot(qf * tab_ref[h, 1], sf_ref[c, ks, :]) + _dot(qf * tab_ref[h, 2], sb_ref[c, ks, :]))
                o_ref[rows, vs] = o
                r = lax.rsqrt(jnp.mean(o * o, axis=-1, keepdims=True) + EPS)
                g = g_ref[rows, vs]
                y_ref[rows, vs] = (g * _sigmoid(g) * (o * r * gain_ref[:, vs])).astype(y_ref.dtype)

    row = lambda i: (i, 0)
    st = lambda i: (i, 0, 0)
    return pl.pallas_call(
        body, name="ret_out", grid=(nc // per,),
        in_specs=[pl.BlockSpec((rows_per, RET_Q), row), pl.BlockSpec((rows_per, RET_Q), row),
                  pl.BlockSpec((rows_per, RET_V), lambda i: (i, 1)), pl.BlockSpec((rows_per, RET_V), lambda i: (i, 2)),
                  pl.BlockSpec((per, RET_Q, RET_DV), st), pl.BlockSpec((per, RET_Q, RET_DV), st),
                  pl.BlockSpec((RET_HEADS, 3, CHUNK, CHUNK), lambda i: (0, 0, 0, 0)),
                  pl.BlockSpec((1, RET_V), lambda i: (0, 0))],
        out_specs=[pl.BlockSpec((rows_per, RET_V), row), pl.BlockSpec((rows_per, RET_V), row)],
        out_shape=[_sds((T, RET_V), F32), _sds((T, RET_V), _MXU)],
        compiler_params=_params("parallel"),
    )(qr, kr, proj, proj, sf, sb, tabs, gain)


def _ret_gate_bwd(dycat, proj, ret_o, gain, *, tm):
    T = ret_o.shape[0]
    tm = min(tm, T)

    def body(dy_ref, g_ref, o_ref, gain_ref, do_ref, dg_ref, dgain_ref):
        @pl.when(pl.program_id(0) == 0)
        def _():
            dgain_ref[...] = jnp.zeros_like(dgain_ref)
        for h in range(RET_HEADS):
            vs = slice(h * RET_DV, (h + 1) * RET_DV)
            o, g, dya, gn = o_ref[:, vs], g_ref[:, vs], dy_ref[:, vs], gain_ref[:, vs]
            r = lax.rsqrt(jnp.mean(o * o, axis=-1, keepdims=True) + EPS)
            ohat = o * r
            sg = _sigmoid(g)
            dy = dya * (g * sg)
            dg_ref[:, vs] = (dya * (ohat * gn) * (sg * (1.0 + g * (1.0 - sg)))).astype(dg_ref.dtype)
            dyg = dy * gn
            do_ref[:, vs] = (r * (dyg - ohat * jnp.mean(dyg * ohat, axis=-1, keepdims=True))).astype(do_ref.dtype)
            dgain_ref[:, vs] += _rowsum8(dy * ohat)

    row = lambda i: (i, 0)
    return pl.pallas_call(
        body, name="ret_gate_bwd", grid=(T // tm,),
        in_specs=[pl.BlockSpec((tm, RET_V), row), pl.BlockSpec((tm, RET_V), lambda i: (i, 2)),
                  pl.BlockSpec((tm, RET_V), row), pl.BlockSpec((1, RET_V), lambda i: (0, 0))],
        out_specs=[pl.BlockSpec((tm, RET_V), row), pl.BlockSpec((tm, RET_V), row), pl.BlockSpec((8, RET_V), lambda i: (0, 0))],
        out_shape=[_sds((T, RET_V), _MXU), _sds((T, RET_V), _MXU), _sds((8, RET_V), F32)],
        compiler_params=_params("arbitrary"),
    )(dycat, proj, ret_o, gain)


def _ret_bwd(qr, kr, proj, g_out, sf, sb, rf, rb, tabs):
    T = qr.shape[0]
    nc = T // CHUNK
    per = min(RET_CHUNKS_PER_STEP, nc)
    rows_per = per * CHUNK

    def body(q_ref, k_ref, v_ref, g_ref, sf_ref, sb_ref, rf_ref, rb_ref, tab_ref, dq_ref, dk_ref, dv_ref, dl_ref):
        @pl.when(pl.program_id(0) == 0)
        def _():
            dl_ref[...] = jnp.zeros_like(dl_ref)
        for c in range(per):
            rows = slice(c * CHUNK, (c + 1) * CHUNK)
            for h in range(RET_HEADS):
                ks = slice(h * RET_DK, (h + 1) * RET_DK)
                vs = slice(h * RET_DV, (h + 1) * RET_DV)
                q, k, v, g = q_ref[rows, ks], k_ref[rows, ks], v_ref[rows, vs], g_ref[rows, vs]
                s_f, s_b, r_f, r_b = sf_ref[c, ks, :], sb_ref[c, ks, :], rf_ref[c, ks, :], rb_ref[c, ks, :]
                tab = lambda t, h=h: tab_ref[h, t]
                qf, kf = q.astype(F32), k.astype(F32)
                qk = _dot(q, k, _NT)
                da_raw = _dot(g, v, _NT)
                x_f, x_b = _dot(g, s_f, _NT), _dot(g, s_b, _NT)
                dq_ref[rows, ks] = _dot(da_raw * tab(TAB_D), k) + tab(TAB_A) * x_f + tab(TAB_B) * x_b
                at = _dot(k, q, _NT) * tab(TAB_DT)
                dat = _dot(v, g, _NT) * tab(TAB_DT)
                y_f, y_b = _dot(v, r_f, _NT), _dot(v, r_b, _NT)
                dk_ref[rows, ks] = _dot(dat, q) + tab(TAB_CF) * y_f + tab(TAB_CB) * y_b
                dv_ref[rows, vs] = (_dot(at, g) + _dot(kf * tab(TAB_CF), r_f) + _dot(kf * tab(TAB_CB), r_b)).astype(dv_ref.dtype)
                inner = da_raw * qk
                rs_f = r_f.astype(F32) * s_f.astype(F32)
                rs_b = r_b.astype(F32) * s_b.astype(F32)
                dl_f = (inner * tab(TAB_EF) + tab(TAB_RA) * qf * x_f + tab(TAB_RCF) * kf * y_f
                        + tab(TAB_KF) * (rs_f[:, :CHUNK] + rs_f[:, CHUNK:]))
                dl_b = (inner * tab(TAB_EB) + tab(TAB_RB) * qf * x_b + tab(TAB_RCB) * kf * y_b
                        + tab(TAB_KB) * (rs_b[:, :CHUNK] + rs_b[:, CHUNK:]))
                dl_ref[2 * h:2 * h + 1, :] += jnp.sum(dl_f, axis=0, keepdims=True)
                dl_ref[2 * h + 1:2 * h + 2, :] += jnp.sum(dl_b, axis=0, keepdims=True)

    row = lambda i: (i, 0)
    st = lambda i: (i, 0, 0)
    return pl.pallas_call(
        body, name="ret_bwd", grid=(nc // per,),
        in_specs=[pl.BlockSpec((rows_per, RET_Q), row), pl.BlockSpec((rows_per, RET_Q), row),
                  pl.BlockSpec((rows_per, RET_V), lambda i: (i, 1)), pl.BlockSpec((rows_per, RET_V), row),
                  pl.BlockSpec((per, RET_Q, RET_DV), st), pl.BlockSpec((per, RET_Q, RET_DV), st),
                  pl.BlockSpec((per, RET_Q, RET_DV), st), pl.BlockSpec((per, RET_Q, RET_DV), st),
                  pl.BlockSpec((RET_HEADS, 14, CHUNK, CHUNK), lambda i: (0, 0, 0, 0))],
        out_specs=[pl.BlockSpec((rows_per, RET_Q), row), pl.BlockSpec((rows_per, RET_Q), row),
                   pl.BlockSpec((rows_per, RET_V), row), pl.BlockSpec((8, CHUNK), lambda i: (0, 0))],
        out_shape=[_sds((T, RET_Q), F32), _sds((T, RET_Q), F32), _sds((T, RET_V), _MXU), _sds((8, CHUNK), F32)],
        compiler_params=_params("arbitrary"),
    )(qr, kr, proj, g_out, sf, sb, rf, rb, tabs)


SWA_GROUP = SWA_HEADS // SWA_KV_HEADS
SWA_COLS = SWA_GROUP * CHUNK


def _swa_stack(ref, g):
    return jnp.concatenate([ref[:, h * HEAD_DIM:(h + 1) * HEAD_DIM] for h in range(g * SWA_GROUP, (g + 1) * SWA_GROUP)], axis=0)


def _swa_probs_t(q, k_win, bias_t, sink_row, i, nb):
    st = _dot(k_win, q, _NT) * ATT_SCALE + bias_t
    key = lax.broadcasted_iota(jnp.int32, st.shape, 0)
    valid = jnp.logical_and(jnp.logical_or(key >= CHUNK, i > 0), jnp.logical_or(key < 2 * CHUNK, i < nb - 1))
    st = jnp.where(valid, st, NEG_INF)
    m = jnp.maximum(jnp.max(st, axis=0, keepdims=True), sink_row)
    p = jnp.exp(st - m)
    e_sink = jnp.exp(sink_row - m)
    inv = 1.0 / (jnp.sum(p, axis=0, keepdims=True) + e_sink)
    return p * inv, e_sink * inv


def _swa_layouts(bias, sink):
    bias_t = bias.reshape(SWA_KV_HEADS, SWA_GROUP, CHUNK, 3 * CHUNK).transpose(0, 3, 1, 2).reshape(SWA_KV_HEADS, 3 * CHUNK, SWA_COLS)
    return bias_t, jnp.repeat(sink[:, 0], CHUNK).reshape(SWA_KV_HEADS, SWA_COLS)


def _swa_window_specs(nb, width, col_block, clamp):
    prev = lambda i: (jnp.maximum(clamp(i) - 1, 0), col_block)
    cur = lambda i: (clamp(i), col_block)
    nxt = lambda i: (jnp.minimum(clamp(i) + 1, nb - 1), col_block)
    return [pl.BlockSpec((CHUNK, width), f) for f in (prev, cur, nxt)]


def _swa_fwd(qn, kn, proj, bias, sink):
    T = qn.shape[0]
    nb = T // CHUNK
    kvw = SWA_KV_HEADS * HEAD_DIM
    bias_t, sink_rows = _swa_layouts(bias, sink)

    def body(q_ref, k0, k1, k2, v0, v1, v2, bias_ref, sink_ref, y_ref):
        i = pl.program_id(0)
        for g in range(SWA_KV_HEADS):
            gs = slice(g * HEAD_DIM, (g + 1) * HEAD_DIM)
            k_win = jnp.concatenate([k0[:, gs], k1[:, gs], k2[:, gs]], axis=0)
            v_win = jnp.concatenate([v0[:, gs], v1[:, gs], v2[:, gs]], axis=0).astype(_MXU)
            pt, _ = _swa_probs_t(_swa_stack(q_ref, g), k_win, bias_ref[g], sink_ref[g:g + 1, :], i, nb)
            o = _dot(v_win, pt, _TN).T
            for hh in range(SWA_GROUP):
                h = g * SWA_GROUP + hh
                y_ref[:, h * HEAD_DIM:(h + 1) * HEAD_DIM] = o[hh * CHUNK:(hh + 1) * CHUNK].astype(y_ref.dtype)

    ident = lambda i: i
    return pl.pallas_call(
        body, name="swa_fwd", grid=(nb,),
        in_specs=[pl.BlockSpec((CHUNK, D_MODEL), lambda i: (i, 0))]
        + _swa_window_specs(nb, kvw, 0, ident) + _swa_window_specs(nb, kvw, 17, ident)
        + [pl.BlockSpec((SWA_KV_HEADS, 3 * CHUNK, SWA_COLS), lambda i: (0, 0, 0)), pl.BlockSpec((SWA_KV_HEADS, SWA_COLS), lambda i: (0, 0))],
        out_specs=pl.BlockSpec((CHUNK, D_MODEL), lambda i: (i, 0)),
        out_shape=_sds((T, D_MODEL), _MXU),
        compiler_params=_params("parallel"),
    )(qn, kn, kn, kn, proj, proj, proj, bias_t, sink_rows)


def _swa_bwd(qn, kn, proj, dycat, bias, sink):
    T = qn.shape[0]
    nb = T // CHUNK
    kvw = SWA_KV_HEADS * HEAD_DIM
    bias_t, sink_rows = _swa_layouts(bias, sink)

    def body(q_ref, k0, k1, k2, v0, v1, v2, dy_ref, bias_ref, sink_ref,
             dq_ref, dk_ref, dv_ref, dbias_ref, dsink_ref, acc_a, acc_b):
        i = pl.program_id(0)

        @pl.when(i == 0)
        def _():
            dbias_ref[...] = jnp.zeros_like(dbias_ref)
            dsink_ref[...] = jnp.zeros_like(dsink_ref)
            acc_a[...] = jnp.zeros_like(acc_a)
            acc_b[...] = jnp.zeros_like(acc_b)

        @pl.when(i < nb)
        def _():
            for g in range(SWA_KV_HEADS):
                gs = slice(g * HEAD_DIM, (g + 1) * HEAD_DIM)
                k_win = jnp.concatenate([k0[:, gs], k1[:, gs], k2[:, gs]], axis=0)
                v_win = jnp.concatenate([v0[:, gs], v1[:, gs], v2[:, gs]], axis=0).astype(_MXU)
                q, dy = _swa_stack(q_ref, g), _swa_stack(dy_ref, g)
                pt, p_sink = _swa_probs_t(q, k_win, bias_ref[g], sink_ref[g:g + 1, :], i, nb)
                dpt = _dot(v_win, dy, _NT)
                delta = jnp.sum(pt * dpt, axis=0, keepdims=True)
                dst = pt * (dpt - delta)
                dbias_ref[g] += dst
                dsink_ref[g:g + 1, :] += -p_sink * delta
                dq = (_dot(k_win, dst, _TN) * ATT_SCALE).T
                for hh in range(SWA_GROUP):
                    h = g * SWA_GROUP + hh
                    dq_ref[:, h * HEAD_DIM:(h + 1) * HEAD_DIM] = dq[hh * CHUNK:(hh + 1) * CHUNK]
                dk_win = _dot(dst, q) * ATT_SCALE
                dv_win = _dot(pt, dy)
                for win, out_ref, col0 in ((dk_win, dk_ref, 0), (dv_win, dv_ref, kvw)):
                    cs = slice(col0 + g * HEAD_DIM, col0 + (g + 1) * HEAD_DIM)
                    out_ref[:, gs] = acc_a[:, cs] + win[:CHUNK]
                    acc_a[:, cs] = acc_b[:, cs] + win[CHUNK:2 * CHUNK]
                    acc_b[:, cs] = win[2 * CHUNK:]

        @pl.when(i == nb)
        def _():
            dk_ref[...] = acc_a[:, :kvw]
            dv_ref[...] = acc_a[:, kvw:]

    clamp = lambda i: jnp.minimum(i, nb - 1)
    late = lambda i: (jnp.maximum(i - 1, 0), 0)
    bias_spec = pl.BlockSpec((SWA_KV_HEADS, 3 * CHUNK, SWA_COLS), lambda i: (0, 0, 0))
    sink_spec = pl.BlockSpec((SWA_KV_HEADS, SWA_COLS), lambda i: (0, 0))
    dq, dk, dv, dbias_t, dsink_rows = pl.pallas_call(
        body, name="swa_bwd", grid=(nb + 1,),
        in_specs=[pl.BlockSpec((CHUNK, D_MODEL), lambda i: (clamp(i), 0))]
        + _swa_window_specs(nb, kvw, 0, clamp) + _swa_window_specs(nb, kvw, 17, clamp)
        + [pl.BlockSpec((CHUNK, D_MODEL), lambda i: (clamp(i), 1)), bias_spec, sink_spec],
        out_specs=[pl.BlockSpec((CHUNK, D_MODEL), lambda i: (clamp(i), 0)),
                   pl.BlockSpec((CHUNK, kvw), late), pl.BlockSpec((CHUNK, kvw), late), bias_spec, sink_spec],
        out_shape=[_sds((T, D_MODEL), F32), _sds((T, kvw), F32), _sds((T, kvw), F32),
                   _sds((SWA_KV_HEADS, 3 * CHUNK, SWA_COLS), F32), _sds((SWA_KV_HEADS, SWA_COLS), F32)],
        scratch_shapes=[pltpu.VMEM((CHUNK, 2 * kvw), F32), pltpu.VMEM((CHUNK, 2 * kvw), F32)],
        compiler_params=_params("arbitrary"),
    )(qn, kn, kn, kn, proj, proj, proj, dycat, bias_t, sink_rows)
    dbias = dbias_t.reshape(SWA_KV_HEADS, 3 * CHUNK, SWA_GROUP, CHUNK).transpose(0, 2, 3, 1).reshape(SWA_HEADS, CHUNK, 3 * CHUNK)
    dsink = jnp.sum(dsink_rows.reshape(SWA_HEADS, CHUNK), axis=1, keepdims=True) * jnp.ones((1, HEAD_DIM), F32)
    return dq, dk, dv, dbias, dsink


def _t5_bucket_reduce(dbias, bucket):
    def body(db_ref, bk_ref, o_ref):
        bk = bk_ref[...]
        row = lax.broadcasted_iota(jnp.int32, (SWA_HEADS, HEAD_DIM), 0)
        lane = lax.broadcasted_iota(jnp.int32, (SWA_HEADS, HEAD_DIM), 1)

        def per_bucket(b, acc):
            mask = bk == b
            for h in range(SWA_HEADS):
                tot = jnp.sum(jnp.sum(jnp.where(mask, db_ref[h], 0.0), axis=0, keepdims=True), axis=1, keepdims=True)
                acc = acc + jnp.where(jnp.logical_and(row == h, lane == b), tot, 0.0)
            return acc

        o_ref[...] = lax.fori_loop(0, T5_BUCKETS, per_bucket, jnp.zeros((SWA_HEADS, HEAD_DIM), F32))

    return pl.pallas_call(body, name="t5_bucket_reduce", out_shape=_sds((SWA_HEADS, HEAD_DIM), F32),
                          compiler_params=pltpu.CompilerParams(vmem_limit_bytes=VMEM_LIMIT_BYTES))(dbias, bucket)


def _headnorm_bwd(x, dy, gain):
    r = lax.rsqrt(jnp.mean(x * x, axis=-1, keepdims=True) + EPS)
    xhat = x * r
    dyg = dy * gain
    return r * (dyg - xhat * jnp.mean(dyg * xhat, axis=-1, keepdims=True)), dy * xhat


def _post_even(proj, dqr, dkr, dva, dga, dqn, dkn, dvb, cos, sin, q_gain, k_gain, *, tm):
    T = proj.shape[0]
    tm = min(tm, T)
    kvw = SWA_KV_HEADS * HEAD_DIM

    def body(qb_ref, kb_ref, dqr_ref, dkr_ref, dva_ref, dga_ref, dqn_ref, dkn_ref, dvb_ref, c_ref, s_ref, qg_ref, kg_ref,
             dp_ref, dqg_ref, dkg_ref):
        @pl.when(pl.program_id(0) == 0)
        def _():
            dqg_ref[...] = jnp.zeros_like(dqg_ref)
            dkg_ref[...] = jnp.zeros_like(dkg_ref)
        c = jnp.concatenate([c_ref[...]] * RET_HEADS, axis=1)
        s = jnp.concatenate([s_ref[...]] * RET_HEADS, axis=1)
        dq = dqr_ref[...]
        dp_ref[:, 0:RET_Q] = (dq * c + _swap_halves(dq * s, RET_DK // 2)).astype(dp_ref.dtype)
        dk = dkr_ref[...] * (RET_DK ** -0.5)
        dp_ref[:, RET_Q:2 * RET_Q] = (dk * c + _swap_halves(dk * s, RET_DK // 2)).astype(dp_ref.dtype)
        off = 2 * RET_Q
        dp_ref[:, off:off + RET_V] = dva_ref[...].astype(dp_ref.dtype)
        dp_ref[:, off + RET_V:off + 2 * RET_V] = dga_ref[...].astype(dp_ref.dtype)
        off += 2 * RET_V
        for src, dsrc, gain, dgain, heads, base in ((qb_ref, dqn_ref, qg_ref, dqg_ref, SWA_HEADS, off),
                                                    (kb_ref, dkn_ref, kg_ref, dkg_ref, SWA_KV_HEADS, off + D_MODEL)):
            for h in range(heads):
                sl = slice(h * HEAD_DIM, (h + 1) * HEAD_DIM)
                dx, dgx = _headnorm_bwd(src[:, sl], dsrc[:, sl], gain[...])
                dp_ref[:, base + h * HEAD_DIM:base + (h + 1) * HEAD_DIM] = dx.astype(dp_ref.dtype)
                dgain[...] += _rowsum8(dgx)
        dp_ref[:, off + D_MODEL + kvw:] = dvb_ref[...].astype(dp_ref.dtype)

    row = lambda i: (i, 0)
    const = lambda i: (0, 0)
    return pl.pallas_call(
        body, name="post_even", grid=(T // tm,),
        in_specs=[pl.BlockSpec((tm, D_MODEL), lambda i: (i, 3)), pl.BlockSpec((tm, kvw), lambda i: (i, 16)),
                  pl.BlockSpec((tm, RET_Q), row), pl.BlockSpec((tm, RET_Q), row),
                  pl.BlockSpec((tm, RET_V), row), pl.BlockSpec((tm, RET_V), row),
                  pl.BlockSpec((tm, D_MODEL), row), pl.BlockSpec((tm, kvw), row), pl.BlockSpec((tm, kvw), row),
                  pl.BlockSpec((tm, RET_DK), row), pl.BlockSpec((tm, RET_DK), row),
                  pl.BlockSpec((1, HEAD_DIM), const), pl.BlockSpec((1, HEAD_DIM), const)],
        out_specs=[pl.BlockSpec((tm, EVEN_IN), row), pl.BlockSpec((8, HEAD_DIM), const), pl.BlockSpec((8, HEAD_DIM), const)],
        out_shape=[_sds((T, EVEN_IN), _MXU), _sds((8, HEAD_DIM), F32), _sds((8, HEAD_DIM), F32)],
        compiler_params=_params("arbitrary"),
    )(proj, proj, dqr, dkr, dva, dga, dqn, dkn, dvb, cos, sin, q_gain, k_gain)


def _prep_odd(proj, cos, sin, q_gain, k_gain, *, tm):
    T = proj.shape[0]
    tm = min(tm, T)
    kvw = AX_KV_HEADS * HEAD_DIM

    def body(q_ref, k_ref, v_ref, c_ref, s_ref, qg_ref, kg_ref, qx_ref, kx_ref, vx_ref):
        c, s = c_ref[...], s_ref[...]
        for src, gain, dst, heads, scale in ((q_ref, qg_ref, qx_ref, AX_HEADS, SCORE_SCALE_LOG2), (k_ref, kg_ref, kx_ref, AX_KV_HEADS, 1.0)):
            for h in range(heads):
                sl = slice(h * HEAD_DIM, (h + 1) * HEAD_DIM)
                xh = src[:, sl]
                r = lax.rsqrt(jnp.mean(xh * xh, axis=-1, keepdims=True) + EPS)
                xn = xh * r * gain[...]
                dst[:, sl] = ((xn * c + _swap_halves(xn, HEAD_DIM // 4) * s) * scale).astype(dst.dtype)
        vx_ref[...] = v_ref[...].astype(vx_ref.dtype)

    row = lambda i: (i, 0)
    const = lambda i: (0, 0)
    return pl.pallas_call(
        body, name="prep_odd", grid=(T // tm,),
        in_specs=[pl.BlockSpec((tm, D_MODEL), row), pl.BlockSpec((tm, kvw), lambda i: (i, 4)), pl.BlockSpec((tm, kvw), lambda i: (i, 5)),
                  pl.BlockSpec((tm, HEAD_DIM), row), pl.BlockSpec((tm, HEAD_DIM), row),
                  pl.BlockSpec((1, HEAD_DIM), const), pl.BlockSpec((1, HEAD_DIM), const)],
        out_specs=[pl.BlockSpec((tm, D_MODEL), row), pl.BlockSpec((tm, kvw), row), pl.BlockSpec((tm, kvw), row)],
        out_shape=[_sds((T, D_MODEL), _MXU), _sds((T, kvw), _MXU), _sds((T, kvw), _MXU)],
        compiler_params=_params("parallel"),
    )(proj, proj, proj, cos, sin, q_gain, k_gain)


def _post_odd(proj, dqxt, dkx, dvx, cos, sin, q_gain, k_gain, *, tm):
    T = proj.shape[0]
    tm = min(tm, T)
    kvw = AX_KV_HEADS * HEAD_DIM

    def body(q_ref, k_ref, dqt_ref, dk_ref, dv_ref, c_ref, s_ref, qg_ref, kg_ref, dp_ref, dqg_ref, dkg_ref):
        @pl.when(pl.program_id(0) == 0)
        def _():
            dqg_ref[...] = jnp.zeros_like(dqg_ref)
            dkg_ref[...] = jnp.zeros_like(dkg_ref)
        c, s = c_ref[...], s_ref[...]
        for src, dsrc, gain, dgain, heads, base in ((q_ref, dqt_ref, qg_ref, dqg_ref, AX_HEADS, 0),
                                                    (k_ref, dk_ref, kg_ref, dkg_ref, AX_KV_HEADS, D_MODEL)):
            for h in range(heads):
                sl = slice(h * HEAD_DIM, (h + 1) * HEAD_DIM)
                d = dsrc[sl, :].T if dsrc is dqt_ref else dsrc[:, sl]
                dn = d * c + _swap_halves(d * s, HEAD_DIM // 4)
                dx, dgx = _headnorm_bwd(src[:, sl], dn, gain[...])
                dp_ref[:, base + h * HEAD_DIM:base + (h + 1) * HEAD_DIM] = dx.astype(dp_ref.dtype)
                dgain[...] += _rowsum8(dgx)
        dp_ref[:, D_MODEL + kvw:] = dv_ref[...].astype(dp_ref.dtype)

    row = lambda i: (i, 0)
    const = lambda i: (0, 0)
    return pl.pallas_call(
        body, name="post_odd", grid=(T // tm,),
        in_specs=[pl.BlockSpec((tm, D_MODEL), row), pl.BlockSpec((tm, kvw), lambda i: (i, 4)),
                  pl.BlockSpec((D_MODEL, tm), lambda i: (0, i)), pl.BlockSpec((tm, kvw), row), pl.BlockSpec((tm, kvw), row),
                  pl.BlockSpec((tm, HEAD_DIM), row), pl.BlockSpec((tm, HEAD_DIM), row),
                  pl.BlockSpec((1, HEAD_DIM), const), pl.BlockSpec((1, HEAD_DIM), const)],
        out_specs=[pl.BlockSpec((tm, ODD_IN), row), pl.BlockSpec((8, HEAD_DIM), const), pl.BlockSpec((8, HEAD_DIM), const)],
        out_shape=[_sds((T, ODD_IN), _MXU), _sds((8, HEAD_DIM), F32), _sds((8, HEAD_DIM), F32)],
        compiler_params=_params("arbitrary"),
    )(proj, proj, dqxt, dkx, dvx, cos, sin, q_gain, k_gain)


ONES_ROWS = 16


def _flash_fwd(qx, kx, vx, *, tq, tk):
    v1t = jnp.concatenate([vx.T.reshape(AX_KV_HEADS, HEAD_DIM, vx.shape[0]),
                           jnp.ones((AX_KV_HEADS, ONES_ROWS, vx.shape[0]), vx.dtype)], axis=1)
    T = qx.shape[0]
    tq, tk = min(tq, T), min(tk, T)
    nq, nk = T // tq, T // tk
    group = AX_HEADS // AX_KV_HEADS

    def body(k_ref, v_ref, q_ref, o_ref, lse_ref, acc_sc, m_sc, l_sc):
        j = pl.program_id(2)

        @pl.when(j == 0)
        def _():
            m_sc[...] = jnp.full(m_sc.shape, NEG_INF, F32)
            l_sc[...] = jnp.zeros_like(l_sc)
            acc_sc[...] = jnp.zeros_like(acc_sc)
        k, v = k_ref[...], v_ref[0]

        def step(i, carry):
            cols = pl.ds(pl.multiple_of(i * tq, tq), tq)
            st = _dot(k, q_ref[cols, :], _NT)
            m_old = m_sc[i]
            m_new = jnp.maximum(m_old, jnp.max(st, axis=0, keepdims=True))
            p = jnp.exp2(st - m_new)
            alpha = jnp.exp2(m_old - m_new)
            pv = _dot(v, p)
            m_sc[i] = m_new
            l_sc[i] = alpha * l_sc[i] + pv[HEAD_DIM:HEAD_DIM + 1]
            acc_sc[:, cols] = alpha * acc_sc[:, cols] + pv[:HEAD_DIM]
            return carry

        lax.fori_loop(0, nq, step, 0)

        @pl.when(j == nk - 1)
        def _():
            def finish(i, carry):
                cols = pl.ds(pl.multiple_of(i * tq, tq), tq)
                o_ref[cols, :] = (acc_sc[:, cols] / l_sc[i]).T.astype(o_ref.dtype)
                lse_ref[0, i] = m_sc[i] + jnp.log2(l_sc[i])
                return carry

            lax.fori_loop(0, nq, finish, 0)

    kv = lambda g, h, j: (j, g)
    qh = lambda g, h, j: (0, g * group + h)
    o, lse = pl.pallas_call(
        body, name="flash_fwd", grid=(AX_KV_HEADS, group, nk),
        in_specs=[pl.BlockSpec((tk, HEAD_DIM), kv), pl.BlockSpec((1, HEAD_DIM + ONES_ROWS, tk), lambda g, h, j: (g, 0, j)),
                  pl.BlockSpec((T, HEAD_DIM), qh)],
        out_specs=[pl.BlockSpec((T, HEAD_DIM), qh), pl.BlockSpec((1, nq, 1, tq), lambda g, h, j: (g * group + h, 0, 0, 0))],
        out_shape=[_sds((T, D_MODEL), _MXU), _sds((AX_HEADS, nq, 1, tq), F32)],
        scratch_shapes=[pltpu.VMEM((HEAD_DIM, T), F32), pltpu.VMEM((nq, 1, tq), F32), pltpu.VMEM((nq, 1, tq), F32)],
        compiler_params=_params("parallel", "arbitrary", "arbitrary"),
    )(kx, v1t, qx)
    return o, lse.reshape(AX_HEADS, 1, T)


def _flash_bwd(qx, kx, vx, o, do, lse, *, tq, tk):
    T = qx.shape[0]
    tq, tk = min(tq, T), min(tk, T)
    nq = T // tq
    group = AX_HEADS // AX_KV_HEADS
    lse_rows = lse.reshape(AX_HEADS, nq, 1, tq)
    kxt = kx.T.reshape(AX_KV_HEADS, HEAD_DIM, T)

    def body(k_ref, kt_ref, v_ref, q_ref, o_ref, do_ref, lse_ref, dqt_ref, dk_ref, dv_ref, delta_sc):
        j = pl.program_id(2)

        @pl.when(jnp.logical_and(pl.program_id(1) == 0, j == 0))
        def _():
            dk_ref[...] = jnp.zeros_like(dk_ref)
            dv_ref[...] = jnp.zeros_like(dv_ref)

        @pl.when(j == 0)
        def _():
            dqt_ref[...] = jnp.zeros_like(dqt_ref)

            def row_delta(i, carry):
                rows = pl.ds(pl.multiple_of(i * tq, tq), tq)
                prod = do_ref[rows, :].astype(F32) * o_ref[rows, :].astype(F32)
                delta_sc[i] = jnp.sum(prod.T, axis=0, keepdims=True)
                return carry

            lax.fori_loop(0, nq, row_delta, 0)
        k, v = k_ref[...], v_ref[...]

        def step(i, carry):
            dk, dv = carry
            off = pl.multiple_of(i * tq, tq)
            q, do_blk = q_ref[pl.ds(off, tq), :], do_ref[pl.ds(off, tq), :]
            pt = jnp.exp2(_dot(k, q, _NT) - lse_ref[0, i])
            dst = pt * (_dot(v, do_blk, _NT) - delta_sc[i])
            dqt_ref[:, pl.ds(off, tq)] += _dot(kt_ref[0], dst) * ATT_SCALE
            return dk + _dot(dst, q), dv + _dot(pt, do_blk)

        zero = jnp.zeros((tk, HEAD_DIM), F32)
        dk, dv = lax.fori_loop(0, nq, step, (zero, zero))
        rows = pl.ds(pl.multiple_of(j * tk, tk), tk)
        dk_ref[rows, :] += dk * (ATT_SCALE / SCORE_SCALE_LOG2)
        dv_ref[rows, :] += dv

    kv = lambda g, h, j: (j, g)
    qh = lambda g, h, j: (0, g * group + h)
    st = lambda g, h, j: (g * group + h, 0, 0, 0)
    acc = lambda g, h, j: (0, g)
    return pl.pallas_call(
        body, name="flash_bwd", grid=(AX_KV_HEADS, group, T // tk),
        in_specs=[pl.BlockSpec((tk, HEAD_DIM), kv), pl.BlockSpec((1, HEAD_DIM, tk), lambda g, h, j: (g, 0, j)),
                  pl.BlockSpec((tk, HEAD_DIM), kv),
                  pl.BlockSpec((T, HEAD_DIM), qh), pl.BlockSpec((T, HEAD_DIM), qh), pl.BlockSpec((T, HEAD_DIM), qh),
                  pl.BlockSpec((1, nq, 1, tq), st)],
        out_specs=[pl.BlockSpec((HEAD_DIM, T), lambda g, h, j: (g * group + h, 0)),
                   pl.BlockSpec((T, HEAD_DIM), acc), pl.BlockSpec((T, HEAD_DIM), acc)],
        out_shape=[_sds((D_MODEL, T), F32), _sds((T, AX_KV_HEADS * HEAD_DIM), F32), _sds((T, AX_KV_HEADS * HEAD_DIM), F32)],
        scratch_shapes=[pltpu.VMEM((nq, 1, tq), F32)],
        compiler_params=_params("parallel", "arbitrary", "arbitrary"),
    )(kx, kxt, vx, qx, o, do, lse_rows)


TM = 1024
TM_WIDE = 512


def _mlp_fwd(tag, x, gain, fetch, target=None):
    u, h = _norm_matmul(f"mlp_up{tag}", x, gain, (fetch("w_mlp_up", x), tag), tm=TM_WIDE, tn=D_FF, out_dtype=_MXU)
    out = _matmul_res(f"mlp_down{tag}", [u], (fetch("w_mlp_down", u), tag), x, tm=TM if target is None else TM_WIDE,
                      relu2=True, target=target)
    return out, (x, u, h)


def _local_step(x, target, p, fetch, push, tokens=()):
    T = x.shape[0]
    cos_r, sin_r = _ret_rope_tables(T)
    cos_a, sin_a = _axial_rope_tables(T)
    tabs, rw, log_gamma = _retention_tables(p["ret_decay_logit"][0])
    bias = _swa_bias(p["t5_table"])
    sink = p["swa_sink"][0][:, None] * jnp.ones((1, HEAD_DIM), F32)
    nm, nl = p["norm_mix"], p["norm_mlp"]
    pending = [t for t in tokens if t is not None]

    def send(tag, weight, dw):
        token = push(tag, weight, dw[None])
        if token is not None:
            pending.append(token)

    def tied(operand):
        while pending:
            operand = operand + pending.pop()[0:1, 0:1]
        return operand

    def mlp_bwd(tag, saved, gain, dy, dy16):
        xs, u, h = saved
        w_up, w_down = (fetch("w_mlp_up", None), tag), (fetch("w_mlp_down", None), tag)
        du = _matmul_nt(f"mlp_down{tag}_bwd", dy16, w_down, tm=TM_WIDE, tn=D_FF, out_dtype=_MXU, relu_of=u)
        send(f"mlp_down{tag}", "w_mlp_down", _matmul_tn(f"mlp_down{tag}_dw", u, dy16, tk=2048, tn=1024, tt=1024, out_dtype=_WIRE, relu2=True))
        dx, dx16, dgain = _matmul_nt_normbwd(f"mlp_up{tag}_bwd", du, w_up, xs, tied(gain), dy, tm=TM_WIDE)
        send(f"mlp_up{tag}", "w_mlp_up", _matmul_tn(f"mlp_up{tag}_dw", h, du, tk=1024, tn=2048, tt=1024, out_dtype=_WIRE))
        return dx, dx16, dgain

    w_in_even = fetch("w_in_even", None)
    proj0, h0 = _norm_matmul("in_even", x, tied(nm[0:1]), w_in_even, tm=TM_WIDE, tn=EVEN_IN, out_dtype=F32)
    qr, kr, qn, kn = _prep_even(proj0, cos_r, sin_r, p["swa_q_norm"], p["swa_k_norm"], tm=TM)
    sf, sb = _ret_scan("ret_scan_fwd", kr, proj0, 1, rw["cf"], rw["dec_f"], rw["cb"], rw["dec_b"])
    ret_o, ya = _ret_out(qr, kr, proj0, sf, sb, tabs[:, (TAB_D, TAB_A, TAB_B)], p["ret_norm"])
    yb = _swa_fwd(qn, kn, proj0, bias, sink)
    w_out_even = fetch("w_out_even", yb)
    x1 = _matmul_res("out_even", [ya, yb], w_out_even, x, tm=TM)
    x2, mlp0 = _mlp_fwd(0, x1, nl[0:1], fetch)
    w_in_odd, w_out_odd = fetch("w_in_odd", x2), fetch("w_out_odd", x2)
    proj1, h1 = _norm_matmul("in_odd", x2, nm[1:2], w_in_odd, tm=TM, tn=ODD_IN, out_dtype=F32)
    qx, kx, vx = _prep_odd(proj1, cos_a, sin_a, p["ax_q_norm"], p["ax_k_norm"], tm=TM)
    o, lse = _flash_fwd(qx, kx, vx, tq=2048, tk=1024)
    x3 = _matmul_res("out_odd", [o], w_out_odd, x2, tm=TM)
    (g4, g4_16, loss_part), mlp1 = _mlp_fwd(1, x3, nl[1:2], fetch, target=target)

    dx3, dx3_16, dnl1 = mlp_bwd(1, mlp1, nl[1:2], g4, g4_16)
    do = _matmul_nt("out_odd_bwd", dx3_16, w_out_odd, tm=TM, tn=1024, out_dtype=_MXU)
    send("out_odd", "w_out_odd", _matmul_tn("out_odd_dw", o, dx3_16, tk=1024, tn=1024, tt=1024, out_dtype=_WIRE))
    dqxt, dkx, dvx = _flash_bwd(qx, kx, vx, o, do, lse, tq=2048, tk=512)
    dproj1, dqg1, dkg1 = _post_odd(proj1, dqxt, dkx, dvx, cos_a, sin_a, tied(p["ax_q_norm"]), p["ax_k_norm"], tm=TM)
    send("in_odd", "w_in_odd", _matmul_tn("in_odd_dw", h1, dproj1, tk=1024, tn=768, tt=1024, out_dtype=_WIRE))
    dx2, dx2_16, dnm1 = _matmul_nt_normbwd("in_odd_bwd", dproj1, w_in_odd, x2, tied(nm[1:2]), dx3, tm=TM_WIDE)
    dx1, dx1_16, dnl0 = mlp_bwd(0, mlp0, nl[0:1], dx2, dx2_16)
    dycat = _matmul_nt("out_even_bwd", dx1_16, w_out_even, tm=TM, tn=2 * D_MODEL, out_dtype=F32)
    send("out_even", "w_out_even", jnp.concatenate([
        _matmul_tn("out_even_dw_ret", ya, dx1_16, tk=1024, tn=1024, tt=1024, out_dtype=_WIRE),
        _matmul_tn("out_even_dw_swa", yb, dx1_16, tk=1024, tn=1024, tt=1024, out_dtype=_WIRE)], axis=0))
    g_out, dga, dretg = _ret_gate_bwd(dycat, proj0, ret_o, tied(p["ret_norm"]), tm=TM)
    rb, rf = _ret_scan("ret_scan_bwd", qr, g_out, 0, rw["b"], rw["dec_b"], rw["a"], rw["dec_f"])
    dqr, dkr, dva, dlog = _ret_bwd(qr, kr, proj0, g_out, sf, sb, rf, rb, tabs)
    dqn, dkn, dvb, dbias, dsink = _swa_bwd(qn, kn, proj0, dycat, bias, sink)
    dt5 = _t5_bucket_reduce(dbias, _t5_bucket(_swa_rel()).astype(jnp.int32))
    dproj0, dqg0, dkg0 = _post_even(proj0, dqr, dkr, dva, dga, dqn, dkn, dvb, cos_r, sin_r,
                                    p["swa_q_norm"], p["swa_k_norm"], tm=TM_WIDE)
    send("in_even", "w_in_even", _matmul_tn("in_even_dw", h0, dproj0, tk=1024, tn=2304, tt=1024, out_dtype=_WIRE))
    dx0, _, dnm0 = _matmul_nt_normbwd("in_even_bwd", dproj0, w_in_even, x, tied(nm[0:1]), dx1, tm=TM_WIDE)

    fold = lambda part: jnp.sum(part, axis=0)
    dlam = jnp.sum(dlog, axis=1).reshape(RET_HEADS, 2).T
    small = {
        "norm_mix": jnp.stack([fold(dnm0), fold(dnm1)]),
        "norm_mlp": jnp.stack([fold(dnl0), fold(dnl1)]),
        "ret_decay_logit": (dlam * (1.0 - jnp.exp(log_gamma)))[None],
        "ret_norm": fold(dretg)[None],
        "swa_q_norm": fold(dqg0)[None], "swa_k_norm": fold(dkg0)[None],
        "swa_sink": dsink[:, 0][None],
        "t5_table": dt5[:, :T5_BUCKETS].T,
        "ax_q_norm": fold(dqg1)[None], "ax_k_norm": fold(dkg1)[None],
    }
    return loss_part, dx0, small


BIG = ("w_in_even", "w_out_even", "w_in_odd", "w_out_odd", "w_mlp_up", "w_mlp_down")
SMALL = ("norm_mix", "norm_mlp", "ret_decay_logit", "ret_norm", "swa_q_norm", "swa_k_norm", "swa_sink", "t5_table",
         "ax_q_norm", "ax_k_norm")
WEIGHTS = ("norm_mix", "norm_mlp", "w_in_even", "w_out_even", "ret_decay_logit", "ret_norm", "swa_q_norm", "swa_k_norm",
           "swa_sink", "t5_table", "w_in_odd", "w_out_odd", "ax_q_norm", "ax_k_norm", "w_mlp_up", "w_mlp_down")
SHARD_AXIS = {"w_in_even": 2, "w_out_even": 1, "w_in_odd": 2, "w_out_odd": 1, "w_mlp_up": 2, "w_mlp_down": 1}
N_CHIPS = 4
GATHER_ORDER = (("w_in_even",), ("w_out_even",), ("w_mlp_up",), ("w_mlp_down",), ("w_in_odd", "w_out_odd"))
ANY = pl.BlockSpec(memory_space=pl.ANY)
HBM = pl.BlockSpec(memory_space=pltpu.HBM)
SEM = pl.BlockSpec(memory_space=pltpu.SEMAPHORE)
SPLIT_COPY = pltpu.CompilerParams(has_side_effects=pltpu.SideEffectType.DATAFLOW_SIDE_EFFECTING)


def _in_hbm(a):
    return pltpu.with_memory_space_constraint(a, pltpu.HBM)


def _mesh_pos():
    return lax.axis_index("x"), lax.axis_index("y"), lax.axis_index("c")


def _window(ref, axis, start, size):
    idx = [slice(None)] * len(ref.shape)
    idx[axis] = pl.ds(start, size)
    return ref.at[tuple(idx)]


def _cast_place(key, shard, chip, *, tr=256):
    L, R, C = shard.shape
    tr = min(tr, R)
    axis = SHARD_AXIS[key]
    whole = tuple(d * (N_CHIPS if a == axis else 1) for a, d in enumerate(shard.shape))

    def body(chip_ref, s_ref, o_ref):
        o_ref[...] = s_ref[...].astype(o_ref.dtype)

    if axis == 2:
        out_map = lambda l, i, chip_ref: (l, i, chip_ref[0])
    else:
        out_map = lambda l, i, chip_ref: (l, i + chip_ref[0] * (R // tr), 0)
    grid_spec = pltpu.PrefetchScalarGridSpec(
        num_scalar_prefetch=1, grid=(L, R // tr),
        in_specs=[pl.BlockSpec((1, tr, C), lambda l, i, chip_ref: (l, i, 0))],
        out_specs=pl.BlockSpec((1, tr, C), out_map))
    return pl.pallas_call(body, name=f"cast_place_{key}", grid_spec=grid_spec, out_shape=_sds(whole, _MXU),
                          compiler_params=_params("parallel", "parallel"))(chip, shard)


def _gather_copies(names, refs, send_sems, recv_sems, *, outgoing=True, incoming=True):
    x, y, c = _mesh_pos()
    chips = [(1 - x, y), (x, 1 - y), (1 - x, 1 - y)]
    out, inc = [], []
    for t, key in enumerate(names):
        size = refs[t].shape[SHARD_AXIS[key]] // N_CHIPS
        slot = lambda px, py: _window(refs[t], SHARD_AXIS[key], pl.multiple_of((2 * px + py) * size, 128), size)
        for k, (px, py) in enumerate(chips):
            sems = dict(send_sem=send_sems.at[3 * t + k], recv_sem=recv_sems.at[3 * t + k], device_id=(px, py, c), device_id_type=MESH)
            if outgoing:
                out.append(pltpu.make_async_remote_copy(slot(x, y), slot(x, y), **sems))
            if incoming:
                inc.append(pltpu.make_async_remote_copy(slot(x, y), slot(px, py), **sems))
    return out, inc


def _allgather_start(groups):
    names = [list(g) for g in groups]
    flat = [g[k] for g in groups for k in g]
    n, ng = len(flat), len(groups)

    def body(*refs):
        start = 0
        for gi, keys in enumerate(names):
            copies, _ = _gather_copies(keys, refs[start:start + len(keys)], refs[n + 2 * gi], refs[n + 2 * gi + 1], incoming=False)
            for cp in copies:
                cp.start()
            start += len(keys)
        token = refs[-1]
        token[...] = jnp.zeros_like(token)

    sem_shapes = [pltpu.SemaphoreType.DMA((3 * len(keys),)) for keys in names for _ in (0, 1)]
    outs = pl.pallas_call(
        body, name="allgather_start", in_specs=[HBM] * n,
        out_specs=[SEM] * (2 * ng) + [HBM] * n + [pl.BlockSpec(memory_space=pltpu.VMEM)],
        out_shape=sem_shapes + [pltpu.HBM(a.shape, a.dtype) for a in flat] + [_sds((8, HEAD_DIM), F32)],
        input_output_aliases={t: 2 * ng + t for t in range(n)},
        compiler_params=SPLIT_COPY,
    )(*[_in_hbm(a) for a in flat])
    states, start = [], 2 * ng
    for gi, keys in enumerate(names):
        states.append((gi, keys, outs[2 * gi], outs[2 * gi + 1], outs[start:start + len(keys)]))
        start += len(keys)
    return states, outs[-1]


def _allgather_wait(state, after):
    gi, names, send_sems, recv_sems, thru = state
    n = len(names)

    def body(*refs):
        outgoing, incoming = _gather_copies(names, refs[:n], refs[n], refs[n + 1])
        for cp in outgoing:
            cp.wait_send()
        for cp in incoming:
            cp.wait_recv()

    outs = pl.pallas_call(
        body, name=f"allgather_wait_{gi}", in_specs=[HBM] * n + [SEM, SEM, ANY], out_specs=[HBM] * n,
        out_shape=[pltpu.HBM(t.shape, t.dtype) for t in thru],
        input_output_aliases={t: t for t in range(n)},
        compiler_params=SPLIT_COPY,
    )(*thru, send_sems, recv_sems, after)
    return dict(zip(names, outs))


FLIPS = [(a, b, d) for a in (0, 1) for b in (0, 1) for d in (0, 1) if (a, b, d) != (0, 0, 0)]


def _flip(pos, f):
    return tuple(1 - p if fi else p for p, fi in zip(pos, f))


def _piece_shape(weight, shape):
    out = list(shape)
    out[SHARD_AXIS[weight]] //= N_CHIPS
    out[1] //= 2
    return tuple(out)


def _piece(ref, weight, chip, core):
    piece = _piece_shape(weight, ref.shape)
    if SHARD_AXIS[weight] == 1:
        return _window(ref, 1, pl.multiple_of((2 * chip + core) * piece[1], 8), piece[1])
    return _window(_window(ref, 2, pl.multiple_of(chip * piece[2], 128), piece[2]), 1, pl.multiple_of(core * piece[1], 8), piece[1])


def _scatter_copies(weight, grad_ref, land_ref, send_sems, recv_sems, *, outgoing=True, incoming=True):
    pos = _mesh_pos()
    out, inc = [], []
    for k, f in enumerate(FLIPS):
        peer = _flip(pos, f)
        sems = dict(send_sem=send_sems.at[k], recv_sem=recv_sems.at[k], device_id=peer, device_id_type=MESH)
        if outgoing:
            out.append(pltpu.make_async_remote_copy(_piece(grad_ref, weight, 2 * peer[0] + peer[1], peer[2]), land_ref.at[k], **sems))
        if incoming:
            inc.append(pltpu.make_async_remote_copy(_piece(grad_ref, weight, 2 * pos[0] + pos[1], pos[2]), land_ref.at[k], **sems))
    return out, inc


def _scatter_start(tag, weight, grad):
    n_peer = len(FLIPS)
    land = lax.empty((n_peer,) + _piece_shape(weight, grad.shape), grad.dtype)

    def body(grad_ref, land_ref, send_sems, recv_sems, grad_thru, land_thru, token):
        copies, _ = _scatter_copies(weight, grad_ref, land_ref, send_sems, recv_sems, incoming=False)
        for cp in copies:
            cp.start()
        token[...] = jnp.zeros_like(token)

    outs = pl.pallas_call(
        body, name=f"scatter_start_{tag}", in_specs=[HBM, HBM],
        out_specs=[SEM, SEM, HBM, HBM, pl.BlockSpec(memory_space=pltpu.VMEM)],
        out_shape=[pltpu.SemaphoreType.DMA((n_peer,)), pltpu.SemaphoreType.DMA((n_peer,)),
                   pltpu.HBM(grad.shape, grad.dtype), pltpu.HBM(land.shape, land.dtype), _sds((8, HEAD_DIM), F32)],
        input_output_aliases={0: 2, 1: 3},
        compiler_params=SPLIT_COPY,
    )(_in_hbm(grad), _in_hbm(land))
    return (tag, weight, outs[:4]), outs[4]


def _scatter_wait(state, after):
    tag, weight, (send_sems, recv_sems, grad_thru, land_thru) = state

    def body(grad_ref, land_ref, send_ref, recv_ref, after_ref, grad_out, land_out):
        outgoing, incoming = _scatter_copies(weight, grad_ref, land_ref, send_ref, recv_ref)
        for cp in outgoing:
            cp.wait_send()
        for cp in incoming:
            cp.wait_recv()

    return pl.pallas_call(
        body, name=f"scatter_wait_{tag}", in_specs=[HBM, HBM, SEM, SEM, ANY], out_specs=[HBM, HBM],
        out_shape=[pltpu.HBM(grad_thru.shape, grad_thru.dtype), pltpu.HBM(land_thru.shape, land_thru.dtype)],
        input_output_aliases={0: 0, 1: 1},
        compiler_params=SPLIT_COPY,
    )(grad_thru, land_thru, send_sems, recv_sems, after)


def _sum_pieces(tag, weight, grad, land, where, *, tr=128):
    _, R, C = _piece_shape(weight, grad.shape)
    tr = min(tr, R)
    nr = R // tr

    def body(where_ref, g_ref, l_ref, o_ref):
        acc = g_ref[...].astype(F32)
        for s in range(len(FLIPS)):
            acc = acc + l_ref[s].astype(F32)
        o_ref[...] = acc

    if SHARD_AXIS[weight] == 1:
        own = lambda i, where_ref: (0, (2 * where_ref[0] + where_ref[1]) * nr + i, 0)
    else:
        own = lambda i, where_ref: (0, where_ref[1] * nr + i, where_ref[0])
    grid_spec = pltpu.PrefetchScalarGridSpec(
        num_scalar_prefetch=1, grid=(nr,),
        in_specs=[pl.BlockSpec((1, tr, C), own), pl.BlockSpec((len(FLIPS), 1, tr, C), lambda i, where_ref: (0, 0, i, 0))],
        out_specs=pl.BlockSpec((1, tr, C), lambda i, where_ref: (0, where_ref[1] * nr + i, 0)))
    return pl.pallas_call(body, name=f"sum_{tag}", grid_spec=grid_spec, out_shape=_sds((1, 2 * R, C), F32),
                          compiler_params=_params("parallel"))(where, grad, land)


def _exchange_halves(shards):
    names = list(shards)
    n = len(names)
    half_sizes = [shards[k].shape[1] // 2 for k in names]

    def body(*refs):
        outs = refs[n:2 * n]
        send_sems, recv_sems = refs[2 * n:]
        x, y, c = _mesh_pos()
        half = lambda t, core: _window(outs[t], 1, pl.multiple_of(core * half_sizes[t], 8), half_sizes[t])
        sends = []
        for t in range(n):
            sends.append(pltpu.make_async_remote_copy(half(t, c), half(t, c), send_sems.at[t], recv_sems.at[t],
                                                      device_id=(x, y, 1 - c), device_id_type=MESH))
            sends[-1].start()
        for t in range(n):
            pltpu.make_async_remote_copy(half(t, c), half(t, 1 - c), send_sems.at[t], recv_sems.at[t],
                                         device_id=(x, y, 1 - c), device_id_type=MESH).wait_recv()
        for cp in sends:
            cp.wait_send()

    outs = pl.pallas_call(
        body, name="exchange_halves", in_specs=[ANY] * n, out_specs=[ANY] * n,
        out_shape=[_sds(shards[k].shape, F32) for k in names],
        input_output_aliases={t: t for t in range(n)},
        scratch_shapes=[pltpu.SemaphoreType.DMA((n,)), pltpu.SemaphoreType.DMA((n,))],
    )(*[shards[k] for k in names])
    return dict(zip(names, outs))


def _adamw_math(w, g, m, v):
    m = ADAM_B1 * m + (1.0 - ADAM_B1) * g
    v = ADAM_B2 * v + (1.0 - ADAM_B2) * jnp.square(g)
    m_hat = m / (1.0 - ADAM_B1 ** ADAM_STEP)
    v_hat = v / (1.0 - ADAM_B2 ** ADAM_STEP)
    return -ADAM_LR * (m_hat / (jnp.sqrt(v_hat) + ADAM_EPS) + ADAM_WD * w), m, v


def _adamw(name, w, g, m, v, *, tr=256):
    R, C = w.shape
    tr = min(tr, R)

    def body(w_ref, g_ref, m_ref, v_ref, d_ref, mo_ref, vo_ref):
        d_ref[...], mo_ref[...], vo_ref[...] = _adamw_math(w_ref[...], g_ref[...], m_ref[...], v_ref[...])

    spec = pl.BlockSpec((tr, C), lambda i: (i, 0))
    return pl.pallas_call(body, name=name, grid=(R // tr,), in_specs=[spec] * 4, out_specs=[spec] * 3,
                          out_shape=[_sds((R, C), F32)] * 3, compiler_params=_params("parallel"))(w, g, m, v)


SLAB_ROWS = 8
LOSS_ROW = 7


def _pack_small(d):
    pad = lambda a, width: jnp.pad(a.reshape(-1), (0, width - a.size))
    row5 = jnp.concatenate([d["swa_q_norm"].reshape(-1), d["swa_k_norm"].reshape(-1), d["ax_q_norm"].reshape(-1),
                            d["ax_k_norm"].reshape(-1), pad(d["swa_sink"], HEAD_DIM), pad(d["ret_decay_logit"], HEAD_DIM),
                            jnp.zeros((2 * HEAD_DIM,), F32)])
    return jnp.concatenate([d["norm_mix"], d["norm_mlp"], d["ret_norm"], row5[None], pad(d["t5_table"], D_MODEL)[None],
                            jnp.zeros((1, D_MODEL), F32)], axis=0)


def _unpack_small(slab):
    r5 = slab[5]
    return {
        "norm_mix": slab[0:2], "norm_mlp": slab[2:4], "ret_norm": slab[4:5],
        "swa_q_norm": r5[None, 0:128], "swa_k_norm": r5[None, 128:256], "ax_q_norm": r5[None, 256:384],
        "ax_k_norm": r5[None, 384:512], "swa_sink": r5[None, 512:512 + SWA_HEADS],
        "ret_decay_logit": r5[640:640 + 2 * RET_HEADS].reshape(1, 2, RET_HEADS),
        "t5_table": slab[6, :T5_BUCKETS * SWA_HEADS].reshape(T5_BUCKETS, SWA_HEADS),
    }


def _small_allreduce_adamw(g_slab, w_slab, m_slab, v_slab, loss_part):
    def body(g_ref, w_ref, m_ref, v_ref, lp_ref, go_ref, d_ref, mo_ref, vo_ref, gath, send_sems, recv_sems):
        pos = _mesh_pos()
        ident = lambda p: 4 * p[0] + 2 * p[1] + p[2]
        me = ident(pos)
        row = lax.broadcasted_iota(jnp.int32, (SLAB_ROWS, D_MODEL), 0)
        lane = lax.broadcasted_iota(jnp.int32, (SLAB_ROWS, D_MODEL), 1)
        loss = jnp.sum(jnp.sum(lp_ref[...], axis=0, keepdims=True), axis=1, keepdims=True) * (0.5 / D_MODEL)
        gath[me] = jnp.where(jnp.logical_and(row == LOSS_ROW, lane == 0), loss, g_ref[...])
        sends = []
        for k, f in enumerate(FLIPS):
            sends.append(pltpu.make_async_remote_copy(gath.at[me], gath.at[me], send_sems.at[k], recv_sems.at[k],
                                                      device_id=_flip(pos, f), device_id_type=MESH))
            sends[-1].start()
        for k, f in enumerate(FLIPS):
            peer = _flip(pos, f)
            pltpu.make_async_remote_copy(gath.at[me], gath.at[ident(peer)], send_sems.at[k], recv_sems.at[k],
                                         device_id=peer, device_id_type=MESH).wait_recv()
        for cp in sends:
            cp.wait_send()
        total = gath[0]
        for s in range(1, N_DEV):
            total = total + gath[s]
        go_ref[...] = total
        d_ref[...], mo_ref[...], vo_ref[...] = _adamw_math(w_ref[...], total, m_ref[...], v_ref[...])

    vmem = pl.BlockSpec(memory_space=pltpu.VMEM)
    return pl.pallas_call(
        body, name="small_allreduce_adamw", in_specs=[vmem] * 5, out_specs=[vmem] * 4,
        out_shape=[_sds((SLAB_ROWS, D_MODEL), F32)] * 4,
        scratch_shapes=[pltpu.VMEM((N_DEV, SLAB_ROWS, D_MODEL), F32),
                        pltpu.SemaphoreType.DMA((len(FLIPS),)), pltpu.SemaphoreType.DMA((len(FLIPS),))],
    )(g_slab, w_slab, m_slab, v_slab, loss_part)


def kernel(x, norm_mix, norm_mlp, w_in_even, w_out_even, ret_decay_logit, ret_norm, swa_q_norm, swa_k_norm, swa_sink, t5_table, w_in_odd, w_out_odd, ax_q_norm, ax_k_norm, w_mlp_up, w_mlp_down, loss_target, m_norm_mix, m_norm_mlp, m_w_in_even, m_w_out_even, m_ret_decay_logit, m_ret_norm, m_swa_q_norm, m_swa_k_norm, m_swa_sink, m_t5_table, m_w_in_odd, m_w_out_odd, m_ax_q_norm, m_ax_k_norm, m_w_mlp_up, m_w_mlp_down, v_norm_mix, v_norm_mlp, v_w_in_even, v_w_out_even, v_ret_decay_logit, v_ret_norm, v_swa_q_norm, v_swa_k_norm, v_swa_sink, v_t5_table, v_w_in_odd, v_w_out_odd, v_ax_q_norm, v_ax_k_norm, v_w_mlp_up, v_w_mlp_down):
    w = dict(zip(WEIGHTS, (norm_mix, norm_mlp, w_in_even, w_out_even, ret_decay_logit, ret_norm, swa_q_norm, swa_k_norm,
                           swa_sink, t5_table, w_in_odd, w_out_odd, ax_q_norm, ax_k_norm, w_mlp_up, w_mlp_down)))
    m = dict(zip(WEIGHTS, (m_norm_mix, m_norm_mlp, m_w_in_even, m_w_out_even, m_ret_decay_logit, m_ret_norm, m_swa_q_norm,
                           m_swa_k_norm, m_swa_sink, m_t5_table, m_w_in_odd, m_w_out_odd, m_ax_q_norm, m_ax_k_norm,
                           m_w_mlp_up, m_w_mlp_down)))
    v = dict(zip(WEIGHTS, (v_norm_mix, v_norm_mlp, v_w_in_even, v_w_out_even, v_ret_decay_logit, v_ret_norm, v_swa_q_norm,
                           v_swa_k_norm, v_swa_sink, v_t5_table, v_w_in_odd, v_w_out_odd, v_ax_q_norm, v_ax_k_norm,
                           v_w_mlp_up, v_w_mlp_down)))
    flat = lambda a: a.reshape(-1, a.shape[-1])

    chip = (2 * lax.axis_index("x") + lax.axis_index("y")).astype(jnp.int32)
    where = jnp.stack([chip, lax.axis_index("c").astype(jnp.int32)])

    placed = {k: _cast_place(k, w[k], where[0:1]) for k in BIG}
    gather, gather_token = _allgather_start([{k: placed[k] for k in group} for group in GATHER_ORDER])
    gathered = {}

    def fetch(name, after):
        if name not in gathered:
            state = gather[[name in group for group in GATHER_ORDER].index(True)]
            gathered.update(_allgather_wait(state, gather_token if after is None else after))
        return gathered[name] if name.startswith("w_mlp") else gathered[name][0]

    in_flight = []

    def push(tag, weight, dw):
        state, token = _scatter_start(tag, weight, dw)
        in_flight.append(state)
        return token

    loss_part, dx, small_g = _local_step(x[0], loss_target[0], {k: w[k] for k in SMALL}, fetch, push)

    halves = {}
    for state in in_flight:
        tag, weight = state[0], state[1]
        dw, land = _scatter_wait(state, dx)
        halves[tag] = _sum_pieces(tag, weight, dw, land, where)
    reduced = _exchange_halves(halves)
    grad = {"w_in_even": reduced["in_even"], "w_out_even": reduced["out_even"],
            "w_in_odd": reduced["in_odd"], "w_out_odd": reduced["out_odd"],
            "w_mlp_up": jnp.concatenate([reduced["mlp_up0"], reduced["mlp_up1"]], axis=0),
            "w_mlp_down": jnp.concatenate([reduced["mlp_down0"], reduced["mlp_down1"]], axis=0)}
    delta, new_m, new_v = {}, {}, {}
    for k in BIG:
        d_k, m_k, v_k = _adamw(f"adamw_{k}", flat(w[k]), flat(grad[k]), flat(m[k]), flat(v[k]))
        delta[k], new_m[k], new_v[k] = d_k.reshape(w[k].shape), m_k.reshape(w[k].shape), v_k.reshape(w[k].shape)

    slabs = _small_allreduce_adamw(_pack_small(small_g), _pack_small({k: w[k] for k in SMALL}),
                                   _pack_small({k: m[k] for k in SMALL}), _pack_small({k: v[k] for k in SMALL}), loss_part)
    loss = slabs[0][LOSS_ROW, 0]
    for out, slab in zip((grad, delta, new_m, new_v), slabs):
        out.update(_unpack_small(slab))

    return (loss, dx[None], *[grad[k] for k in WEIGHTS], *[delta[k] for k in WEIGHTS],
            *[new_m[k] for k in WEIGHTS], *[new_v[k] for k in WEIGHTS])
```

```python
import functools
import math

import jax
import jax.numpy as jnp
from jax import lax
from jax.experimental import pallas as pl
from jax.experimental.pallas import tpu as pltpu

F32 = jnp.float32
BF16 = jnp.bfloat16
_MXU = BF16
_WIRE = BF16

D_MODEL = 1024
HEAD_DIM = 128
EPS = 1e-6
NEG_INF = -1e30
CHUNK = 128
RET_CHUNKS_PER_STEP = 4
GRID_W = 64
RET_HEADS, RET_DK, RET_DV = 4, 128, 256
RET_Q, RET_V = RET_HEADS * RET_DK, RET_HEADS * RET_DV
RET_THETA = 10000.0
SWA_HEADS, SWA_KV_HEADS = 8, 2
T5_BUCKETS, T5_MAX_DIST = 32, 128
AX_HEADS, AX_KV_HEADS = 8, 2
AX_THETA = 10000.0
D_FF = 4 * D_MODEL
EVEN_IN = 2 * RET_Q + 2 * RET_V + D_MODEL + 2 * SWA_KV_HEADS * HEAD_DIM
ODD_IN = D_MODEL + 2 * AX_KV_HEADS * HEAD_DIM
ATT_SCALE = HEAD_DIM ** -0.5
SCORE_SCALE_LOG2 = ATT_SCALE * math.log2(math.e)

ADAM_LR, ADAM_B1, ADAM_B2, ADAM_EPS, ADAM_WD, ADAM_STEP = 0.001, 0.9, 0.999, 1e-08, 0.01, 10

N_DEV = 8
VMEM_LIMIT_BYTES = 56 << 20
MESH = pl.DeviceIdType.MESH

_NN = (((1,), (0,)), ((), ()))
_NT = (((1,), (1,)), ((), ()))
_TN = (((0,), (0,)), ((), ()))


def _dot(a, b, dn=_NN):
    return lax.dot_general(a.astype(_MXU), b.astype(_MXU), dn, preferred_element_type=F32)


def _params(*sem):
    return pltpu.CompilerParams(dimension_semantics=sem, vmem_limit_bytes=VMEM_LIMIT_BYTES)


def _sds(shape, dtype):
    return jax.ShapeDtypeStruct(tuple(shape), dtype)


def _rowsum8(x):
    return jnp.sum(x.reshape(x.shape[0] // 8, 8, x.shape[1]), axis=0)


def _swap_halves(x, half):
    width = x.shape[1]
    lane = lax.broadcasted_iota(jnp.int32, x.shape, 1)
    up = pltpu.roll(x, width - half, axis=1)
    down = pltpu.roll(x, half, axis=1)
    return jnp.where((lane & (2 * half - 1)) < half, up, down)


def _sigmoid(x):
    return 1.0 / (1.0 + jnp.exp(-x))


def _weight_spec(w, block, index_map):
    if isinstance(w, tuple):
        stacked, layer = w
        return stacked, pl.BlockSpec((None,) + block, lambda *idx: (layer,) + index_map(*idx))
    return w, pl.BlockSpec(block, index_map)


def _weight_dims(w):
    return (w[0] if isinstance(w, tuple) else w).shape[-2:]


def _norm_matmul(name, x, gain, w, *, tm, tn, out_dtype):
    T, K = x.shape
    N = _weight_dims(w)[1]
    tm, tn = min(tm, T), min(tn, N)
    w, w_spec = _weight_spec(w, (K, tn), lambda i, j: (0, j))

    def body(x_ref, g_ref, w_ref, y_ref, h_ref, h_sc):
        @pl.when(pl.program_id(1) == 0)
        def _():
            xv = x_ref[...]
            r = lax.rsqrt(jnp.mean(xv * xv, axis=-1, keepdims=True) + EPS)
            h = (xv * r * g_ref[...]).astype(_MXU)
            h_sc[...] = h
            h_ref[...] = h
        y_ref[...] = jnp.dot(h_sc[...], w_ref[...], preferred_element_type=F32).astype(y_ref.dtype)

    return pl.pallas_call(
        body, name=name, grid=(T // tm, N // tn),
        in_specs=[pl.BlockSpec((tm, K), lambda i, j: (i, 0)),
                  pl.BlockSpec((1, K), lambda i, j: (0, 0)),
                  w_spec],
        out_specs=[pl.BlockSpec((tm, tn), lambda i, j: (i, j)),
                   pl.BlockSpec((tm, K), lambda i, j: (i, 0))],
        out_shape=[_sds((T, N), out_dtype), _sds((T, K), _MXU)],
        scratch_shapes=[pltpu.VMEM((tm, K), _MXU)],
        compiler_params=_params("parallel", "arbitrary"),
    )(x, gain, w)


def _matmul_res(name, a_list, w, res, *, tm, relu2=False, target=None):
    T = res.shape[0]
    N = _weight_dims(w)[1]
    K = a_list[0].shape[1]
    n_a = len(a_list)
    tm = min(tm, T)
    with_loss = target is not None
    w_specs = [_weight_spec(w, (K, N), functools.partial(lambda i, b: (b, 0), b=b)) for b in range(n_a)]

    def body(*refs):
        a_refs = refs[:n_a]
        w_refs = refs[n_a:2 * n_a]
        res_ref = refs[2 * n_a]
        acc = res_ref[...]
        for a_ref, w_ref in zip(a_refs, w_refs):
            a = a_ref[...]
            if relu2:
                a = jnp.square(jnp.maximum(a.astype(F32), 0.0))
            acc = acc + _dot(a, w_ref[...])
        if with_loss:
            tgt_ref, g_ref, g16_ref, loss_ref = refs[2 * n_a + 1:]
            diff = acc - tgt_ref[...]
            g = diff * (1.0 / N)
            g_ref[...] = g
            g16_ref[...] = g.astype(g16_ref.dtype)

            @pl.when(pl.program_id(0) == 0)
            def _():
                loss_ref[...] = jnp.zeros_like(loss_ref)
            loss_ref[...] += _rowsum8(diff * diff)
        else:
            refs[2 * n_a + 1][...] = acc

    row = lambda i: (i, 0)
    in_specs = [pl.BlockSpec((tm, K), row) for _ in a_list]
    in_specs += [spec for _, spec in w_specs]
    in_specs += [pl.BlockSpec((tm, N), row)]
    args = list(a_list) + [arr for arr, _ in w_specs] + [res]
    if with_loss:
        in_specs.append(pl.BlockSpec((tm, N), row))
        args.append(target)
        out_specs = [pl.BlockSpec((tm, N), row), pl.BlockSpec((tm, N), row), pl.BlockSpec((8, N), lambda i: (0, 0))]
        out_shape = [_sds((T, N), F32), _sds((T, N), _MXU), _sds((8, N), F32)]
        sem = "arbitrary"
    else:
        out_specs = pl.BlockSpec((tm, N), row)
        out_shape = _sds((T, N), F32)
        sem = "parallel"
    return pl.pallas_call(body, name=name, grid=(T // tm,), in_specs=in_specs, out_specs=out_specs,
                          out_shape=out_shape, compiler_params=_params(sem))(*args)


def _matmul_nt(name, a, w, *, tm, tn, out_dtype, relu_of=None):
    T, K = a.shape
    N = _weight_dims(w)[0]
    tm, tn = min(tm, T), min(tn, N)
    w, w_spec = _weight_spec(w, (tn, K), lambda i, j: (j, 0))

    def body(*refs):
        if relu_of is None:
            a_ref, w_ref, o_ref = refs
            o_ref[...] = _dot(a_ref[...], w_ref[...], _NT).astype(o_ref.dtype)
        else:
            a_ref, w_ref, u_ref, o_ref = refs
            da = _dot(a_ref[...], w_ref[...], _NT)
            o_ref[...] = (da * (2.0 * jnp.maximum(u_ref[...].astype(F32), 0.0))).astype(o_ref.dtype)

    in_specs = [pl.BlockSpec((tm, K), lambda i, j: (i, 0)), w_spec]
    args = [a, w]
    if relu_of is not None:
        in_specs.append(pl.BlockSpec((tm, tn), lambda i, j: (i, j)))
        args.append(relu_of)
    return pl.pallas_call(body, name=name, grid=(T // tm, N // tn), in_specs=in_specs,
                          out_specs=pl.BlockSpec((tm, tn), lambda i, j: (i, j)),
                          out_shape=_sds((T, N), out_dtype),
                          compiler_params=_params("parallel", "parallel"))(*args)


def _matmul_nt_normbwd(name, dy, w, x, gain, dres, *, tm):
    T, K = dy.shape
    N = _weight_dims(w)[0]
    tm = min(tm, T)
    w, w_spec = _weight_spec(w, (N, K), lambda i: (0, 0))

    def body(dy_ref, w_ref, x_ref, g_ref, dres_ref, dx_ref, dx16_ref, dg_ref):
        dh = _dot(dy_ref[...], w_ref[...], _NT)
        xv = x_ref[...]
        r = lax.rsqrt(jnp.mean(xv * xv, axis=-1, keepdims=True) + EPS)
        xhat = xv * r
        dxhat = dh * g_ref[...]
        dx = dres_ref[...] + r * (dxhat - xhat * jnp.mean(dxhat * xhat, axis=-1, keepdims=True))
        dx_ref[...] = dx
        dx16_ref[...] = dx.astype(dx16_ref.dtype)

        @pl.when(pl.program_id(0) == 0)
        def _():
            dg_ref[...] = jnp.zeros_like(dg_ref)
        dg_ref[...] += _rowsum8(dh * xhat)

    row = lambda i: (i, 0)
    return pl.pallas_call(
        body, name=name, grid=(T // tm,),
        in_specs=[pl.BlockSpec((tm, K), row), w_spec,
                  pl.BlockSpec((tm, N), row), pl.BlockSpec((1, N), lambda i: (0, 0)), pl.BlockSpec((tm, N), row)],
        out_specs=[pl.BlockSpec((tm, N), row), pl.BlockSpec((tm, N), row), pl.BlockSpec((8, N), lambda i: (0, 0))],
        out_shape=[_sds((T, N), F32), _sds((T, N), _MXU), _sds((8, N), F32)],
        compiler_params=_params("arbitrary"),
    )(dy, w, x, gain, dres)


def _matmul_tn(name, a, b, *, tk, tn, tt, out_dtype, relu2=False):
    T, Ka = a.shape
    Nb = b.shape[1]
    tk, tn, tt = min(tk, Ka), min(tn, Nb), min(tt, T)
    nt = T // tt

    def body(a_ref, b_ref, o_ref, acc):
        t = pl.program_id(2)

        @pl.when(t == 0)
        def _():
            acc[...] = jnp.zeros_like(acc)
        av = a_ref[...]
        if relu2:
            av = jnp.square(jnp.maximum(av.astype(F32), 0.0))
        acc[...] += _dot(av, b_ref[...], _TN)

        @pl.when(t == nt - 1)
        def _():
            o_ref[...] = acc[...].astype(o_ref.dtype)

    return pl.pallas_call(
        body, name=name, grid=(Ka // tk, Nb // tn, nt),
        in_specs=[pl.BlockSpec((tt, tk), lambda i, j, t: (t, i)), pl.BlockSpec((tt, tn), lambda i, j, t: (t, j))],
        out_specs=pl.BlockSpec((tk, tn), lambda i, j, t: (i, j)),
        out_shape=_sds((Ka, Nb), out_dtype),
        scratch_shapes=[pltpu.VMEM((tk, tn), F32)],
        compiler_params=_params("parallel", "parallel", "arbitrary"),
    )(a, b)


def _rope_angles(pos, dim, theta):
    inv = theta ** (-jnp.arange(0, dim, 2, dtype=F32) / dim)
    return pos.astype(F32)[:, None] * inv[None, :]


def _ret_rope_tables(T):
    ang = _rope_angles(jnp.arange(T), RET_DK, RET_THETA)
    c, s = jnp.cos(ang), jnp.sin(ang)
    return jnp.concatenate([c, c], axis=1), jnp.concatenate([-s, s], axis=1)


def _axial_rope_tables(T):
    rows = T // GRID_W
    ar = _rope_angles(jnp.arange(rows), HEAD_DIM // 2, AX_THETA)
    ac = _rope_angles(jnp.arange(GRID_W), HEAD_DIM // 2, AX_THETA)
    by_row = lambda a: jnp.repeat(a, GRID_W, axis=0)
    by_col = lambda a: jnp.tile(a, (rows, 1))
    cos = jnp.concatenate([by_row(jnp.cos(ar)), by_row(jnp.cos(ar)), by_col(jnp.cos(ac)), by_col(jnp.cos(ac))], axis=1)
    sin = jnp.concatenate([by_row(-jnp.sin(ar)), by_row(jnp.sin(ar)), by_col(-jnp.sin(ac)), by_col(jnp.sin(ac))], axis=1)
    return cos, sin


(TAB_D, TAB_DT, TAB_EF, TAB_EB, TAB_A, TAB_B, TAB_CF, TAB_CB,
 TAB_RA, TAB_RB, TAB_RCF, TAB_RCB, TAB_KF, TAB_KB) = range(14)


def _retention_tables(decay_logit):
    lg = jax.nn.log_sigmoid(decay_logit.astype(F32))
    lam, mu = lg[0][:, None, None], lg[1][:, None, None]
    idx = jnp.arange(CHUNK, dtype=F32)
    diff = (idx[:, None] - idx[None, :])[None]
    df = jnp.where(diff >= 0, jnp.exp(jnp.maximum(diff, 0.0) * lam), 0.0)
    db = jnp.where(diff < 0, jnp.exp(jnp.maximum(-diff, 0.0) * mu), 0.0)
    d = df + db
    r = idx[None, :, None]
    ones = jnp.ones((1, 1, CHUNK), F32)
    a = jnp.exp((r + 1.0) * lam) * ones
    b = jnp.exp((CHUNK - r) * mu) * ones
    cf = jnp.exp((CHUNK - 1.0 - r) * lam) * ones
    cb = jnp.exp(r * mu) * ones
    full = jnp.ones((1, CHUNK, CHUNK), F32)
    kf = CHUNK * jnp.exp(CHUNK * lam) * full
    kb = CHUNK * jnp.exp(CHUNK * mu) * full
    tabs = jnp.stack([d, jnp.swapaxes(d, 1, 2), diff * df, -diff * db, a, b, cf, cb,
                      (r + 1.0) * a, (CHUNK - r) * b, (CHUNK - 1.0 - r) * cf, r * cb, kf, kb], axis=1)

    def lanes(tab):
        return jnp.transpose(tab, (1, 0, 2)).reshape(CHUNK, RET_HEADS * CHUNK)

    def dec(l):
        return jnp.exp(CHUNK * l)[:, 0, :] * jnp.ones((1, RET_DV), F32)

    weights = dict(a=lanes(a), b=lanes(b), cf=lanes(cf), cb=lanes(cb), dec_f=dec(lam), dec_b=dec(mu))
    return tabs, weights, lg


def _t5_bucket(rel):
    nb = T5_BUCKETS // 2
    max_exact = nb // 2
    ret = jnp.where(rel > 0, nb, 0)
    n = jnp.abs(rel)
    nf = jnp.maximum(n, 1).astype(F32)
    large = max_exact + (jnp.log(nf / max_exact) / math.log(T5_MAX_DIST / max_exact)
                         * (nb - max_exact)).astype(jnp.int32)
    large = jnp.minimum(large, nb - 1)
    return ret + jnp.where(n < max_exact, n, large)


def _swa_rel():
    r = jnp.arange(CHUNK)
    j = jnp.arange(3 * CHUNK)
    return j[None, :] - CHUNK - r[:, None]


def _swa_bias(t5_table):
    rel = _swa_rel()
    bucket = jnp.where(jnp.abs(rel) <= CHUNK, _t5_bucket(rel), -1).astype(jnp.int32)

    def body(tab_ref, bk_ref, o_ref):
        bk = bk_ref[...]
        for h in range(SWA_HEADS):
            pick = lambda b, acc, h=h: jnp.where(bk == b, tab_ref[b, h], acc)
            o_ref[h] = lax.fori_loop(0, T5_BUCKETS, pick, jnp.full(bk.shape, NEG_INF, F32))

    return pl.pallas_call(
        body, name="t5_bias",
        in_specs=[pl.BlockSpec(memory_space=pltpu.SMEM), pl.BlockSpec(memory_space=pltpu.VMEM)],
        out_specs=pl.BlockSpec(memory_space=pltpu.VMEM),
        out_shape=_sds((SWA_HEADS, CHUNK, 3 * CHUNK), F32),
    )(t5_table.astype(F32), bucket)


def _prep_even(proj, cos, sin, q_gain, k_gain, *, tm):
    T = proj.shape[0]
    tm = min(tm, T)

    def body(qa_ref, ka_ref, qb_ref, kb_ref, c_ref, s_ref, qg_ref, kg_ref, qr_ref, kr_ref, qn_ref, kn_ref):
        c = jnp.concatenate([c_ref[...]] * RET_HEADS, axis=1)
        s = jnp.concatenate([s_ref[...]] * RET_HEADS, axis=1)
        qa = qa_ref[...]
        qr_ref[...] = (qa * c + _swap_halves(qa, RET_DK // 2) * s).astype(qr_ref.dtype)
        ka = ka_ref[...]
        kr_ref[...] = ((ka * c + _swap_halves(ka, RET_DK // 2) * s) * (RET_DK ** -0.5)).astype(kr_ref.dtype)
        for src, gain, dst, heads in ((qb_ref, qg_ref, qn_ref, SWA_HEADS), (kb_ref, kg_ref, kn_ref, SWA_KV_HEADS)):
            for h in range(heads):
                sl = slice(h * HEAD_DIM, (h + 1) * HEAD_DIM)
                xh = src[:, sl]
                r = lax.rsqrt(jnp.mean(xh * xh, axis=-1, keepdims=True) + EPS)
                dst[:, sl] = (xh * r * gain[...]).astype(dst.dtype)

    row = lambda i: (i, 0)
    const = lambda i: (0, 0)
    return pl.pallas_call(
        body, name="prep_even", grid=(T // tm,),
        in_specs=[pl.BlockSpec((tm, RET_Q), lambda i: (i, 0)), pl.BlockSpec((tm, RET_Q), lambda i: (i, 1)),
                  pl.BlockSpec((tm, D_MODEL), lambda i: (i, 3)), pl.BlockSpec((tm, 256), lambda i: (i, 16)),
                  pl.BlockSpec((tm, RET_DK), row), pl.BlockSpec((tm, RET_DK), row),
                  pl.BlockSpec((1, HEAD_DIM), const), pl.BlockSpec((1, HEAD_DIM), const)],
        out_specs=[pl.BlockSpec((tm, RET_Q), row), pl.BlockSpec((tm, RET_Q), row),
                   pl.BlockSpec((tm, D_MODEL), row), pl.BlockSpec((tm, 256), row)],
        out_shape=[_sds((T, RET_Q), _MXU), _sds((T, RET_Q), _MXU), _sds((T, D_MODEL), _MXU), _sds((T, 256), _MXU)],
        compiler_params=_params("parallel"),
    )(proj, proj, proj, proj, cos, sin, q_gain, k_gain)


def _ret_scan(name, x, y, y_col, w_asc, dec_asc, w_desc, dec_desc):
    T = x.shape[0]
    nc = T // CHUNK
    per = min(RET_CHUNKS_PER_STEP, nc)
    nb = nc // per
    rows_per = per * CHUNK

    def body(xa_ref, ya_ref, xd_ref, yd_ref, wa_ref, da_ref, wd_ref, dd_ref, sa_out, sd_out, sa, sd):
        @pl.when(pl.program_id(0) == 0)
        def _():
            sa[...] = jnp.zeros_like(sa)
            sd[...] = jnp.zeros_like(sd)
        for step in range(per):
            for c, x_ref, y_ref, w_ref, d_ref, st, out in ((step, xa_ref, ya_ref, wa_ref, da_ref, sa, sa_out),
                                                       (per - 1 - step, xd_ref, yd_ref, wd_ref, dd_ref, sd, sd_out)):
                rows = slice(c * CHUNK, (c + 1) * CHUNK)
                out[c] = st[...].astype(out.dtype)
                for h in range(RET_HEADS):
                    ks = slice(h * RET_DK, (h + 1) * RET_DK)
                    vs = slice(h * RET_DV, (h + 1) * RET_DV)
                    u = _dot(x_ref[rows, ks].astype(F32) * w_ref[:, ks], y_ref[rows, vs], _TN)
                    st[ks, :] = st[ks, :] * d_ref[h:h + 1, :] + u

    asc = lambda i: (i, 0)
    desc = lambda i: (nb - 1 - i, 0)
    const = lambda i: (0, 0)
    return pl.pallas_call(
        body, name=name, grid=(nb,),
        in_specs=[pl.BlockSpec((rows_per, RET_Q), asc), pl.BlockSpec((rows_per, RET_V), lambda i: (i, y_col)),
                  pl.BlockSpec((rows_per, RET_Q), desc), pl.BlockSpec((rows_per, RET_V), lambda i: (nb - 1 - i, y_col)),
                  pl.BlockSpec((CHUNK, RET_Q), const), pl.BlockSpec((RET_HEADS, RET_DV), const),
                  pl.BlockSpec((CHUNK, RET_Q), const), pl.BlockSpec((RET_HEADS, RET_DV), const)],
        out_specs=[pl.BlockSpec((per, RET_Q, RET_DV), lambda i: (i, 0, 0)),
                   pl.BlockSpec((per, RET_Q, RET_DV), lambda i: (nb - 1 - i, 0, 0))],
        out_shape=[_sds((nc, RET_Q, RET_DV), _MXU), _sds((nc, RET_Q, RET_DV), _MXU)],
        scratch_shapes=[pltpu.VMEM((RET_Q, RET_DV), F32), pltpu.VMEM((RET_Q, RET_DV), F32)],
        compiler_params=_params("arbitrary"),
    )(x, y, x, y, w_asc, dec_asc, w_desc, dec_desc)


def _ret_out(qr, kr, proj, sf, sb, tabs, gain):
    T = qr.shape[0]
    nc = T // CHUNK
    per = min(RET_CHUNKS_PER_STEP, nc)
    rows_per = per * CHUNK

    def body(q_ref, k_ref, v_ref, g_ref, sf_ref, sb_ref, tab_ref, gain_ref, o_ref, y_ref):
        for c in range(per):
            rows = slice(c * CHUNK, (c + 1) * CHUNK)
            for h in range(RET_HEADS):
                ks = slice(h * RET_DK, (h + 1) * RET_DK)
                vs = slice(h * RET_DV, (h + 1) * RET_DV)
                q, k, v = q_ref[rows, ks], k_ref[rows, ks], v_ref[rows, vs]
                qf = q.astype(F32)
                a_mat = _dot(q, k, _NT) * tab_ref[h, 0]
                o = (_dot(a_mat, v) + _dot(qf * tab_ref[h, 1], sf_ref[c, ks, :]) + _dot(qf * tab_ref[h, 2], sb_ref[c, ks, :]))
                o_ref[rows, vs] = o
                r = lax.rsqrt(jnp.mean(o * o, axis=-1, keepdims=True) + EPS)
                g = g_ref[rows, vs]
                y_ref[rows, vs] = (g * _sigmoid(g) * (o * r * gain_ref[:, vs])).astype(y_ref.dtype)

    row = lambda i: (i, 0)
    st = lambda i: (i, 0, 0)
    return pl.pallas_call(
        body, name="ret_out", grid=(nc // per,),
        in_specs=[pl.BlockSpec((rows_per, RET_Q), row), pl.BlockSpec((rows_per, RET_Q), row),
                  pl.BlockSpec((rows_per, RET_V), lambda i: (i, 1)), pl.BlockSpec((rows_per, RET_V), lambda i: (i, 2)),
                  pl.BlockSpec((per, RET_Q, RET_DV), st), pl.BlockSpec((per, RET_Q, RET_DV), st),
                  pl.BlockSpec((RET_HEADS, 3, CHUNK, CHUNK), lambda i: (0, 0, 0, 0)),
                  pl.BlockSpec((1, RET_V), lambda i: (0, 0))],
        out_specs=[pl.BlockSpec((rows_per, RET_V), row), pl.BlockSpec((rows_per, RET_V), row)],
        out_shape=[_sds((T, RET_V), F32), _sds((T, RET_V), _MXU)],
        compiler_params=_params("parallel"),
    )(qr, kr, proj, proj, sf, sb, tabs, gain)


def _ret_gate_bwd(dycat, proj, ret_o, gain, *, tm):
    T = ret_o.shape[0]
    tm = min(tm, T)

    def body(dy_ref, g_ref, o_ref, gain_ref, do_ref, dg_ref, dgain_ref):
        @pl.when(pl.program_id(0) == 0)
        def _():
            dgain_ref[...] = jnp.zeros_like(dgain_ref)
        for h in range(RET_HEADS):
            vs = slice(h * RET_DV, (h + 1) * RET_DV)
            o, g, dya, gn = o_ref[:, vs], g_ref[:, vs], dy_ref[:, vs], gain_ref[:, vs]
            r = lax.rsqrt(jnp.mean(o * o, axis=-1, keepdims=True) + EPS)
            ohat = o * r
            sg = _sigmoid(g)
            dy = dya * (g * sg)
            dg_ref[:, vs] = (dya * (ohat * gn) * (sg * (1.0 + g * (1.0 - sg)))).astype(dg_ref.dtype)
            dyg = dy * gn
            do_ref[:, vs] = (r * (dyg - ohat * jnp.mean(dyg * ohat, axis=-1, keepdims=True))).astype(do_ref.dtype)
            dgain_ref[:, vs] += _rowsum8(dy * ohat)

    row = lambda i: (i, 0)
    return pl.pallas_call(
        body, name="ret_gate_bwd", grid=(T // tm,),
        in_specs=[pl.BlockSpec((tm, RET_V), row), pl.BlockSpec((tm, RET_V), lambda i: (i, 2)),
                  pl.BlockSpec((tm, RET_V), row), pl.BlockSpec((1, RET_V), lambda i: (0, 0))],
        out_specs=[pl.BlockSpec((tm, RET_V), row), pl.BlockSpec((tm, RET_V), row), pl.BlockSpec((8, RET_V), lambda i: (0, 0))],
        out_shape=[_sds((T, RET_V), _MXU), _sds((T, RET_V), _MXU), _sds((8, RET_V), F32)],
        compiler_params=_params("arbitrary"),
    )(dycat, proj, ret_o, gain)


def _ret_bwd(qr, kr, proj, g_out, sf, sb, rf, rb, tabs):
    T = qr.shape[0]
    nc = T // CHUNK
    per = min(RET_CHUNKS_PER_STEP, nc)
    rows_per = per * CHUNK

    def body(q_ref, k_ref, v_ref, g_ref, sf_ref, sb_ref, rf_ref, rb_ref, tab_ref, dq_ref, dk_ref, dv_ref, dl_ref):
        @pl.when(pl.program_id(0) == 0)
        def _():
            dl_ref[...] = jnp.zeros_like(dl_ref)
        for c in range(per):
            rows = slice(c * CHUNK, (c + 1) * CHUNK)
            for h in range(RET_HEADS):
                ks = slice(h * RET_DK, (h + 1) * RET_DK)
                vs = slice(h * RET_DV, (h + 1) * RET_DV)
                q, k, v, g = q_ref[rows, ks], k_ref[rows, ks], v_ref[rows, vs], g_ref[rows, vs]
                s_f, s_b, r_f, r_b = sf_ref[c, ks, :], sb_ref[c, ks, :], rf_ref[c, ks, :], rb_ref[c, ks, :]
                tab = lambda t, h=h: tab_ref[h, t]
                qf, kf = q.astype(F32), k.astype(F32)
                qk = _dot(q, k, _NT)
                da_raw = _dot(g, v, _NT)
                x_f, x_b = _dot(g, s_f, _NT), _dot(g, s_b, _NT)
                dq_ref[rows, ks] = _dot(da_raw * tab(TAB_D), k) + tab(TAB_A) * x_f + tab(TAB_B) * x_b
                at = _dot(k, q, _NT) * tab(TAB_DT)
                dat = _dot(v, g, _NT) * tab(TAB_DT)
                y_f, y_b = _dot(v, r_f, _NT), _dot(v, r_b, _NT)
                dk_ref[rows, ks] = _dot(dat, q) + tab(TAB_CF) * y_f + tab(TAB_CB) * y_b
                dv_ref[rows, vs] = (_dot(at, g) + _dot(kf * tab(TAB_CF), r_f) + _dot(kf * tab(TAB_CB), r_b)).astype(dv_ref.dtype)
                inner = da_raw * qk
                rs_f = r_f.astype(F32) * s_f.astype(F32)
                rs_b = r_b.astype(F32) * s_b.astype(F32)
                dl_f = (inner * tab(TAB_EF) + tab(TAB_RA) * qf * x_f + tab(TAB_RCF) * kf * y_f
                        + tab(TAB_KF) * (rs_f[:, :CHUNK] + rs_f[:, CHUNK:]))
                dl_b = (inner * tab(TAB_EB) + tab(TAB_RB) * qf * x_b + tab(TAB_RCB) * kf * y_b
                        + tab(TAB_KB) * (rs_b[:, :CHUNK] + rs_b[:, CHUNK:]))
                dl_ref[2 * h:2 * h + 1, :] += jnp.sum(dl_f, axis=0, keepdims=True)
                dl_ref[2 * h + 1:2 * h + 2, :] += jnp.sum(dl_b, axis=0, keepdims=True)

    row = lambda i: (i, 0)
    st = lambda i: (i, 0, 0)
    return pl.pallas_call(
        body, name="ret_bwd", grid=(nc // per,),
        in_specs=[pl.BlockSpec((rows_per, RET_Q), row), pl.BlockSpec((rows_per, RET_Q), row),
                  pl.BlockSpec((rows_per, RET_V), lambda i: (i, 1)), pl.BlockSpec((rows_per, RET_V), row),
                  pl.BlockSpec((per, RET_Q, RET_DV), st), pl.BlockSpec((per, RET_Q, RET_DV), st),
                  pl.BlockSpec((per, RET_Q, RET_DV), st), pl.BlockSpec((per, RET_Q, RET_DV), st),
                  pl.BlockSpec((RET_HEADS, 14, CHUNK, CHUNK), lambda i: (0, 0, 0, 0))],
        out_specs=[pl.BlockSpec((rows_per, RET_Q), row), pl.BlockSpec((rows_per, RET_Q), row),
                   pl.BlockSpec((rows_per, RET_V), row), pl.BlockSpec((8, CHUNK), lambda i: (0, 0))],
        out_shape=[_sds((T, RET_Q), F32), _sds((T, RET_Q), F32), _sds((T, RET_V), _MXU), _sds((8, CHUNK), F32)],
        compiler_params=_params("arbitrary"),
    )(qr, kr, proj, g_out, sf, sb, rf, rb, tabs)


SWA_GROUP = SWA_HEADS // SWA_KV_HEADS
SWA_COLS = SWA_GROUP * CHUNK


def _swa_stack(ref, g):
    return jnp.concatenate([ref[:, h * HEAD_DIM:(h + 1) * HEAD_DIM] for h in range(g * SWA_GROUP, (g + 1) * SWA_GROUP)], axis=0)


def _swa_probs_t(q, k_win, bias_t, sink_row, i, nb):
    st = _dot(k_win, q, _NT) * ATT_SCALE + bias_t
    key = lax.broadcasted_iota(jnp.int32, st.shape, 0)
    valid = jnp.logical_and(jnp.logical_or(key >= CHUNK, i > 0), jnp.logical_or(key < 2 * CHUNK, i < nb - 1))
    st = jnp.where(valid, st, NEG_INF)
    m = jnp.maximum(jnp.max(st, axis=0, keepdims=True), sink_row)
    p = jnp.exp(st - m)
    e_sink = jnp.exp(sink_row - m)
    inv = 1.0 / (jnp.sum(p, axis=0, keepdims=True) + e_sink)
    return p * inv, e_sink * inv


def _swa_layouts(bias, sink):
    bias_t = bias.reshape(SWA_KV_HEADS, SWA_GROUP, CHUNK, 3 * CHUNK).transpose(0, 3, 1, 2).reshape(SWA_KV_HEADS, 3 * CHUNK, SWA_COLS)
    return bias_t, jnp.repeat(sink[:, 0], CHUNK).reshape(SWA_KV_HEADS, SWA_COLS)


def _swa_window_specs(nb, width, col_block, clamp):
    prev = lambda i: (jnp.maximum(clamp(i) - 1, 0), col_block)
    cur = lambda i: (clamp(i), col_block)
    nxt = lambda i: (jnp.minimum(clamp(i) + 1, nb - 1), col_block)
    return [pl.BlockSpec((CHUNK, width), f) for f in (prev, cur, nxt)]


def _swa_fwd(qn, kn, proj, bias, sink):
    T = qn.shape[0]
    nb = T // CHUNK
    kvw = SWA_KV_HEADS * HEAD_DIM
    bias_t, sink_rows = _swa_layouts(bias, sink)

    def body(q_ref, k0, k1, k2, v0, v1, v2, bias_ref, sink_ref, y_ref):
        i = pl.program_id(0)
        for g in range(SWA_KV_HEADS):
            gs = slice(g * HEAD_DIM, (g + 1) * HEAD_DIM)
            k_win = jnp.concatenate([k0[:, gs], k1[:, gs], k2[:, gs]], axis=0)
            v_win = jnp.concatenate([v0[:, gs], v1[:, gs], v2[:, gs]], axis=0).astype(_MXU)
            pt, _ = _swa_probs_t(_swa_stack(q_ref, g), k_win, bias_ref[g], sink_ref[g:g + 1, :], i, nb)
            o = _dot(v_win, pt, _TN).T
            for hh in range(SWA_GROUP):
                h = g * SWA_GROUP + hh
                y_ref[:, h * HEAD_DIM:(h + 1) * HEAD_DIM] = o[hh * CHUNK:(hh + 1) * CHUNK].astype(y_ref.dtype)

    ident = lambda i: i
    return pl.pallas_call(
        body, name="swa_fwd", grid=(nb,),
        in_specs=[pl.BlockSpec((CHUNK, D_MODEL), lambda i: (i, 0))]
        + _swa_window_specs(nb, kvw, 0, ident) + _swa_window_specs(nb, kvw, 17, ident)
        + [pl.BlockSpec((SWA_KV_HEADS, 3 * CHUNK, SWA_COLS), lambda i: (0, 0, 0)), pl.BlockSpec((SWA_KV_HEADS, SWA_COLS), lambda i: (0, 0))],
        out_specs=pl.BlockSpec((CHUNK, D_MODEL), lambda i: (i, 0)),
        out_shape=_sds((T, D_MODEL), _MXU),
        compiler_params=_params("parallel"),
    )(qn, kn, kn, kn, proj, proj, proj, bias_t, sink_rows)


def _swa_bwd(qn, kn, proj, dycat, bias, sink):
    T = qn.shape[0]
    nb = T // CHUNK
    kvw = SWA_KV_HEADS * HEAD_DIM
    bias_t, sink_rows = _swa_layouts(bias, sink)

    def body(q_ref, k0, k1, k2, v0, v1, v2, dy_ref, bias_ref, sink_ref,
             dq_ref, dk_ref, dv_ref, dbias_ref, dsink_ref, acc_a, acc_b):
        i = pl.program_id(0)

        @pl.when(i == 0)
        def _():
            dbias_ref[...] = jnp.zeros_like(dbias_ref)
            dsink_ref[...] = jnp.zeros_like(dsink_ref)
            acc_a[...] = jnp.zeros_like(acc_a)
            acc_b[...] = jnp.zeros_like(acc_b)

        @pl.when(i < nb)
        def _():
            for g in range(SWA_KV_HEADS):
                gs = slice(g * HEAD_DIM, (g + 1) * HEAD_DIM)
                k_win = jnp.concatenate([k0[:, gs], k1[:, gs], k2[:, gs]], axis=0)
                v_win = jnp.concatenate([v0[:, gs], v1[:, gs], v2[:, gs]], axis=0).astype(_MXU)
                q, dy = _swa_stack(q_ref, g), _swa_stack(dy_ref, g)
                pt, p_sink = _swa_probs_t(q, k_win, bias_ref[g], sink_ref[g:g + 1, :], i, nb)
                dpt = _dot(v_win, dy, _NT)
                delta = jnp.sum(pt * dpt, axis=0, keepdims=True)
                dst = pt * (dpt - delta)
                dbias_ref[g] += dst
                dsink_ref[g:g + 1, :] += -p_sink * delta
                dq = (_dot(k_win, dst, _TN) * ATT_SCALE).T
                for hh in range(SWA_GROUP):
                    h = g * SWA_GROUP + hh
                    dq_ref[:, h * HEAD_DIM:(h + 1) * HEAD_DIM] = dq[hh * CHUNK:(hh + 1) * CHUNK]
                dk_win = _dot(dst, q) * ATT_SCALE
                dv_win = _dot(pt, dy)
                for win, out_ref, col0 in ((dk_win, dk_ref, 0), (dv_win, dv_ref, kvw)):
                    cs = slice(col0 + g * HEAD_DIM, col0 + (g + 1) * HEAD_DIM)
                    out_ref[:, gs] = acc_a[:, cs] + win[:CHUNK]
                    acc_a[:, cs] = acc_b[:, cs] + win[CHUNK:2 * CHUNK]
                    acc_b[:, cs] = win[2 * CHUNK:]

        @pl.when(i == nb)
        def _():
            dk_ref[...] = acc_a[:, :kvw]
            dv_ref[...] = acc_a[:, kvw:]

    clamp = lambda i: jnp.minimum(i, nb - 1)
    late = lambda i: (jnp.maximum(i - 1, 0), 0)
    bias_spec = pl.BlockSpec((SWA_KV_HEADS, 3 * CHUNK, SWA_COLS), lambda i: (0, 0, 0))
    sink_spec = pl.BlockSpec((SWA_KV_HEADS, SWA_COLS), lambda i: (0, 0))
    dq, dk, dv, dbias_t, dsink_rows = pl.pallas_call(
        body, name="swa_bwd", grid=(nb + 1,),
        in_specs=[pl.BlockSpec((CHUNK, D_MODEL), lambda i: (clamp(i), 0))]
        + _swa_window_specs(nb, kvw, 0, clamp) + _swa_window_specs(nb, kvw, 17, clamp)
        + [pl.BlockSpec((CHUNK, D_MODEL), lambda i: (clamp(i), 1)), bias_spec, sink_spec],
        out_specs=[pl.BlockSpec((CHUNK, D_MODEL), lambda i: (clamp(i), 0)),
                   pl.BlockSpec((CHUNK, kvw), late), pl.BlockSpec((CHUNK, kvw), late), bias_spec, sink_spec],
        out_shape=[_sds((T, D_MODEL), F32), _sds((T, kvw), F32), _sds((T, kvw), F32),
                   _sds((SWA_KV_HEADS, 3 * CHUNK, SWA_COLS), F32), _sds((SWA_KV_HEADS, SWA_COLS), F32)],
        scratch_shapes=[pltpu.VMEM((CHUNK, 2 * kvw), F32), pltpu.VMEM((CHUNK, 2 * kvw), F32)],
        compiler_params=_params("arbitrary"),
    )(qn, kn, kn, kn, proj, proj, proj, dycat, bias_t, sink_rows)
    dbias = dbias_t.reshape(SWA_KV_HEADS, 3 * CHUNK, SWA_GROUP, CHUNK).transpose(0, 2, 3, 1).reshape(SWA_HEADS, CHUNK, 3 * CHUNK)
    dsink = jnp.sum(dsink_rows.reshape(SWA_HEADS, CHUNK), axis=1, keepdims=True) * jnp.ones((1, HEAD_DIM), F32)
    return dq, dk, dv, dbias, dsink


def _t5_bucket_reduce(dbias, bucket):
    def body(db_ref, bk_ref, o_ref):
        bk = bk_ref[...]
        row = lax.broadcasted_iota(jnp.int32, (SWA_HEADS, HEAD_DIM), 0)
        lane = lax.broadcasted_iota(jnp.int32, (SWA_HEADS, HEAD_DIM), 1)

        def per_bucket(b, acc):
            mask = bk == b
            for h in range(SWA_HEADS):
                tot = jnp.sum(jnp.sum(jnp.where(mask, db_ref[h], 0.0), axis=0, keepdims=True), axis=1, keepdims=True)
                acc = acc + jnp.where(jnp.logical_and(row == h, lane == b), tot, 0.0)
            return acc

        o_ref[...] = lax.fori_loop(0, T5_BUCKETS, per_bucket, jnp.zeros((SWA_HEADS, HEAD_DIM), F32))

    return pl.pallas_call(body, name="t5_bucket_reduce", out_shape=_sds((SWA_HEADS, HEAD_DIM), F32),
                          compiler_params=pltpu.CompilerParams(vmem_limit_bytes=VMEM_LIMIT_BYTES))(dbias, bucket)


def _headnorm_bwd(x, dy, gain):
    r = lax.rsqrt(jnp.mean(x * x, axis=-1, keepdims=True) + EPS)
    xhat = x * r
    dyg = dy * gain
    return r * (dyg - xhat * jnp.mean(dyg * xhat, axis=-1, keepdims=True)), dy * xhat


def _post_even(proj, dqr, dkr, dva, dga, dqn, dkn, dvb, cos, sin, q_gain, k_gain, *, tm):
    T = proj.shape[0]
    tm = min(tm, T)
    kvw = SWA_KV_HEADS * HEAD_DIM

    def body(qb_ref, kb_ref, dqr_ref, dkr_ref, dva_ref, dga_ref, dqn_ref, dkn_ref, dvb_ref, c_ref, s_ref, qg_ref, kg_ref,
             dp_ref, dqg_ref, dkg_ref):
        @pl.when(pl.program_id(0) == 0)
        def _():
            dqg_ref[...] = jnp.zeros_like(dqg_ref)
            dkg_ref[...] = jnp.zeros_like(dkg_ref)
        c = jnp.concatenate([c_ref[...]] * RET_HEADS, axis=1)
        s = jnp.concatenate([s_ref[...]] * RET_HEADS, axis=1)
        dq = dqr_ref[...]
        dp_ref[:, 0:RET_Q] = (dq * c + _swap_halves(dq * s, RET_DK // 2)).astype(dp_ref.dtype)
        dk = dkr_ref[...] * (RET_DK ** -0.5)
        dp_ref[:, RET_Q:2 * RET_Q] = (dk * c + _swap_halves(dk * s, RET_DK // 2)).astype(dp_ref.dtype)
        off = 2 * RET_Q
        dp_ref[:, off:off + RET_V] = dva_ref[...].astype(dp_ref.dtype)
        dp_ref[:, off + RET_V:off + 2 * RET_V] = dga_ref[...].astype(dp_ref.dtype)
        off += 2 * RET_V
        for src, dsrc, gain, dgain, heads, base in ((qb_ref, dqn_ref, qg_ref, dqg_ref, SWA_HEADS, off),
                                                    (kb_ref, dkn_ref, kg_ref, dkg_ref, SWA_KV_HEADS, off + D_MODEL)):
            for h in range(heads):
                sl = slice(h * HEAD_DIM, (h + 1) * HEAD_DIM)
                dx, dgx = _headnorm_bwd(src[:, sl], dsrc[:, sl], gain[...])
                dp_ref[:, base + h * HEAD_DIM:base + (h + 1) * HEAD_DIM] = dx.astype(dp_ref.dtype)
                dgain[...] += _rowsum8(dgx)
        dp_ref[:, off + D_MODEL + kvw:] = dvb_ref[...].astype(dp_ref.dtype)

    row = lambda i: (i, 0)
    const = lambda i: (0, 0)
    return pl.pallas_call(
        body, name="post_even", grid=(T // tm,),
        in_specs=[pl.BlockSpec((tm, D_MODEL), lambda i: (i, 3)), pl.BlockSpec((tm, kvw), lambda i: (i, 16)),
                  pl.BlockSpec((tm, RET_Q), row), pl.BlockSpec((tm, RET_Q), row),
                  pl.BlockSpec((tm, RET_V), row), pl.BlockSpec((tm, RET_V), row),
                  pl.BlockSpec((tm, D_MODEL), row), pl.BlockSpec((tm, kvw), row), pl.BlockSpec((tm, kvw), row),
                  pl.BlockSpec((tm, RET_DK), row), pl.BlockSpec((tm, RET_DK), row),
                  pl.BlockSpec((1, HEAD_DIM), const), pl.BlockSpec((1, HEAD_DIM), const)],
        out_specs=[pl.BlockSpec((tm, EVEN_IN), row), pl.BlockSpec((8, HEAD_DIM), const), pl.BlockSpec((8, HEAD_DIM), const)],
        out_shape=[_sds((T, EVEN_IN), _MXU), _sds((8, HEAD_DIM), F32), _sds((8, HEAD_DIM), F32)],
        compiler_params=_params("arbitrary"),
    )(proj, proj, dqr, dkr, dva, dga, dqn, dkn, dvb, cos, sin, q_gain, k_gain)


def _prep_odd(proj, cos, sin, q_gain, k_gain, *, tm):
    T = proj.shape[0]
    tm = min(tm, T)
    kvw = AX_KV_HEADS * HEAD_DIM

    def body(q_ref, k_ref, v_ref, c_ref, s_ref, qg_ref, kg_ref, qx_ref, kx_ref, vx_ref):
        c, s = c_ref[...], s_ref[...]
        for src, gain, dst, heads, scale in ((q_ref, qg_ref, qx_ref, AX_HEADS, SCORE_SCALE_LOG2), (k_ref, kg_ref, kx_ref, AX_KV_HEADS, 1.0)):
            for h in range(heads):
                sl = slice(h * HEAD_DIM, (h + 1) * HEAD_DIM)
                xh = src[:, sl]
                r = lax.rsqrt(jnp.mean(xh * xh, axis=-1, keepdims=True) + EPS)
                xn = xh * r * gain[...]
                dst[:, sl] = ((xn * c + _swap_halves(xn, HEAD_DIM // 4) * s) * scale).astype(dst.dtype)
        vx_ref[...] = v_ref[...].astype(vx_ref.dtype)

    row = lambda i: (i, 0)
    const = lambda i: (0, 0)
    return pl.pallas_call(
        body, name="prep_odd", grid=(T // tm,),
        in_specs=[pl.BlockSpec((tm, D_MODEL), row), pl.BlockSpec((tm, kvw), lambda i: (i, 4)), pl.BlockSpec((tm, kvw), lambda i: (i, 5)),
                  pl.BlockSpec((tm, HEAD_DIM), row), pl.BlockSpec((tm, HEAD_DIM), row),
                  pl.BlockSpec((1, HEAD_DIM), const), pl.BlockSpec((1, HEAD_DIM), const)],
        out_specs=[pl.BlockSpec((tm, D_MODEL), row), pl.BlockSpec((tm, kvw), row), pl.BlockSpec((tm, kvw), row)],
        out_shape=[_sds((T, D_MODEL), _MXU), _sds((T, kvw), _MXU), _sds((T, kvw), _MXU)],
        compiler_params=_params("parallel"),
    )(proj, proj, proj, cos, sin, q_gain, k_gain)


def _post_odd(proj, dqxt, dkx, dvx, cos, sin, q_gain, k_gain, *, tm):
    T = proj.shape[0]
    tm = min(tm, T)
    kvw = AX_KV_HEADS * HEAD_DIM

    def body(q_ref, k_ref, dqt_ref, dk_ref, dv_ref, c_ref, s_ref, qg_ref, kg_ref, dp_ref, dqg_ref, dkg_ref):
        @pl.when(pl.program_id(0) == 0)
        def _():
            dqg_ref[...] = jnp.zeros_like(dqg_ref)
            dkg_ref[...] = jnp.zeros_like(dkg_ref)
        c, s = c_ref[...], s_ref[...]
        for src, dsrc, gain, dgain, heads, base in ((q_ref, dqt_ref, qg_ref, dqg_ref, AX_HEADS, 0),
                                                    (k_ref, dk_ref, kg_ref, dkg_ref, AX_KV_HEADS, D_MODEL)):
            for h in range(heads):
                sl = slice(h * HEAD_DIM, (h + 1) * HEAD_DIM)
                d = dsrc[sl, :].T if dsrc is dqt_ref else dsrc[:, sl]
                dn = d * c + _swap_halves(d * s, HEAD_DIM // 4)
                dx, dgx = _headnorm_bwd(src[:, sl], dn, gain[...])
                dp_ref[:, base + h * HEAD_DIM:base + (h + 1) * HEAD_DIM] = dx.astype(dp_ref.dtype)
                dgain[...] += _rowsum8(dgx)
        dp_ref[:, D_MODEL + kvw:] = dv_ref[...].astype(dp_ref.dtype)

    row = lambda i: (i, 0)
    const = lambda i: (0, 0)
    return pl.pallas_call(
        body, name="post_odd", grid=(T // tm,),
        in_specs=[pl.BlockSpec((tm, D_MODEL), row), pl.BlockSpec((tm, kvw), lambda i: (i, 4)),
                  pl.BlockSpec((D_MODEL, tm), lambda i: (0, i)), pl.BlockSpec((tm, kvw), row), pl.BlockSpec((tm, kvw), row),
                  pl.BlockSpec((tm, HEAD_DIM), row), pl.BlockSpec((tm, HEAD_DIM), row),
                  pl.BlockSpec((1, HEAD_DIM), const), pl.BlockSpec((1, HEAD_DIM), const)],
        out_specs=[pl.BlockSpec((tm, ODD_IN), row), pl.BlockSpec((8, HEAD_DIM), const), pl.BlockSpec((8, HEAD_DIM), const)],
        out_shape=[_sds((T, ODD_IN), _MXU), _sds((8, HEAD_DIM), F32), _sds((8, HEAD_DIM), F32)],
        compiler_params=_params("arbitrary"),
    )(proj, proj, dqxt, dkx, dvx, cos, sin, q_gain, k_gain)


ONES_ROWS = 16


def _flash_fwd(qx, kx, vx, *, tq, tk):
    v1t = jnp.concatenate([vx.T.reshape(AX_KV_HEADS, HEAD_DIM, vx.shape[0]),
                           jnp.ones((AX_KV_HEADS, ONES_ROWS, vx.shape[0]), vx.dtype)], axis=1)
    T = qx.shape[0]
    tq, tk = min(tq, T), min(tk, T)
    nq, nk = T // tq, T // tk
    group = AX_HEADS // AX_KV_HEADS

    def body(k_ref, v_ref, q_ref, o_ref, lse_ref, acc_sc, m_sc, l_sc):
        j = pl.program_id(2)

        @pl.when(j == 0)
        def _():
            m_sc[...] = jnp.full(m_sc.shape, NEG_INF, F32)
            l_sc[...] = jnp.zeros_like(l_sc)
            acc_sc[...] = jnp.zeros_like(acc_sc)
        k, v = k_ref[...], v_ref[0]

        def step(i, carry):
            cols = pl.ds(pl.multiple_of(i * tq, tq), tq)
            st = _dot(k, q_ref[cols, :], _NT)
            m_old = m_sc[i]
            m_new = jnp.maximum(m_old, jnp.max(st, axis=0, keepdims=True))
            p = jnp.exp2(st - m_new)
            alpha = jnp.exp2(m_old - m_new)
            pv = _dot(v, p)
            m_sc[i] = m_new
            l_sc[i] = alpha * l_sc[i] + pv[HEAD_DIM:HEAD_DIM + 1]
            acc_sc[:, cols] = alpha * acc_sc[:, cols] + pv[:HEAD_DIM]
            return carry

        lax.fori_loop(0, nq, step, 0)

        @pl.when(j == nk - 1)
        def _():
            def finish(i, carry):
                cols = pl.ds(pl.multiple_of(i * tq, tq), tq)
                o_ref[cols, :] = (acc_sc[:, cols] / l_sc[i]).T.astype(o_ref.dtype)
                lse_ref[0, i] = m_sc[i] + jnp.log2(l_sc[i])
                return carry

            lax.fori_loop(0, nq, finish, 0)

    kv = lambda g, h, j: (j, g)
    qh = lambda g, h, j: (0, g * group + h)
    o, lse = pl.pallas_call(
        body, name="flash_fwd", grid=(AX_KV_HEADS, group, nk),
        in_specs=[pl.BlockSpec((tk, HEAD_DIM), kv), pl.BlockSpec((1, HEAD_DIM + ONES_ROWS, tk), lambda g, h, j: (g, 0, j)),
                  pl.BlockSpec((T, HEAD_DIM), qh)],
        out_specs=[pl.BlockSpec((T, HEAD_DIM), qh), pl.BlockSpec((1, nq, 1, tq), lambda g, h, j: (g * group + h, 0, 0, 0))],
        out_shape=[_sds((T, D_MODEL), _MXU), _sds((AX_HEADS, nq, 1, tq), F32)],
        scratch_shapes=[pltpu.VMEM((HEAD_DIM, T), F32), pltpu.VMEM((nq, 1, tq), F32), pltpu.VMEM((nq, 1, tq), F32)],
        compiler_params=_params("parallel", "arbitrary", "arbitrary"),
    )(kx, v1t, qx)
    return o, lse.reshape(AX_HEADS, 1, T)


def _flash_bwd(qx, kx, vx, o, do, lse, *, tq, tk):
    T = qx.shape[0]
    tq, tk = min(tq, T), min(tk, T)
    nq = T // tq
    group = AX_HEADS // AX_KV_HEADS
    lse_rows = lse.reshape(AX_HEADS, nq, 1, tq)
    kxt = kx.T.reshape(AX_KV_HEADS, HEAD_DIM, T)

    def body(k_ref, kt_ref, v_ref, q_ref, o_ref, do_ref, lse_ref, dqt_ref, dk_ref, dv_ref, delta_sc):
        j = pl.program_id(2)

        @pl.when(jnp.logical_and(pl.program_id(1) == 0, j == 0))
        def _():
            dk_ref[...] = jnp.zeros_like(dk_ref)
            dv_ref[...] = jnp.zeros_like(dv_ref)

        @pl.when(j == 0)
        def _():
            dqt_ref[...] = jnp.zeros_like(dqt_ref)

            def row_delta(i, carry):
                rows = pl.ds(pl.multiple_of(i * tq, tq), tq)
                prod = do_ref[rows, :].astype(F32) * o_ref[rows, :].astype(F32)
                delta_sc[i] = jnp.sum(prod.T, axis=0, keepdims=True)
                return carry

            lax.fori_loop(0, nq, row_delta, 0)
        k, v = k_ref[...], v_ref[...]

        def step(i, carry):
            dk, dv = carry
            off = pl.multiple_of(i * tq, tq)
            q, do_blk = q_ref[pl.ds(off, tq), :], do_ref[pl.ds(off, tq), :]
            pt = jnp.exp2(_dot(k, q, _NT) - lse_ref[0, i])
            dst = pt * (_dot(v, do_blk, _NT) - delta_sc[i])
            dqt_ref[:, pl.ds(off, tq)] += _dot(kt_ref[0], dst) * ATT_SCALE
            return dk + _dot(dst, q), dv + _dot(pt, do_blk)

        zero = jnp.zeros((tk, HEAD_DIM), F32)
        dk, dv = lax.fori_loop(0, nq, step, (zero, zero))
        rows = pl.ds(pl.multiple_of(j * tk, tk), tk)
        dk_ref[rows, :] += dk * (ATT_SCALE / SCORE_SCALE_LOG2)
        dv_ref[rows, :] += dv

    kv = lambda g, h, j: (j, g)
    qh = lambda g, h, j: (0, g * group + h)
    st = lambda g, h, j: (g * group + h, 0, 0, 0)
    acc = lambda g, h, j: (0, g)
    return pl.pallas_call(
        body, name="flash_bwd", grid=(AX_KV_HEADS, group, T // tk),
        in_specs=[pl.BlockSpec((tk, HEAD_DIM), kv), pl.BlockSpec((1, HEAD_DIM, tk), lambda g, h, j: (g, 0, j)),
                  pl.BlockSpec((tk, HEAD_DIM), kv),
                  pl.BlockSpec((T, HEAD_DIM), qh), pl.BlockSpec((T, HEAD_DIM), qh), pl.BlockSpec((T, HEAD_DIM), qh),
                  pl.BlockSpec((1, nq, 1, tq), st)],
        out_specs=[pl.BlockSpec((HEAD_DIM, T), lambda g, h, j: (g * group + h, 0)),
                   pl.BlockSpec((T, HEAD_DIM), acc), pl.BlockSpec((T, HEAD_DIM), acc)],
        out_shape=[_sds((D_MODEL, T), F32), _sds((T, AX_KV_HEADS * HEAD_DIM), F32), _sds((T, AX_KV_HEADS * HEAD_DIM), F32)],
        scratch_shapes=[pltpu.VMEM((nq, 1, tq), F32)],
        compiler_params=_params("parallel", "arbitrary", "arbitrary"),
    )(kx, kxt, vx, qx, o, do, lse_rows)


TM = 1024
TM_WIDE = 512


def _mlp_fwd(tag, x, gain, fetch, target=None):
    u, h = _norm_matmul(f"mlp_up{tag}", x, gain, (fetch("w_mlp_up", x), tag), tm=TM_WIDE, tn=D_FF, out_dtype=_MXU)
    out = _matmul_res(f"mlp_down{tag}", [u], (fetch("w_mlp_down", u), tag), x, tm=TM if target is None else TM_WIDE,
                      relu2=True, target=target)
    return out, (x, u, h)


def _local_step(x, target, p, fetch, push, tokens=()):
    T = x.shape[0]
    cos_r, sin_r = _ret_rope_tables(T)
    cos_a, sin_a = _axial_rope_tables(T)
    tabs, rw, log_gamma = _retention_tables(p["ret_decay_logit"][0])
    bias = _swa_bias(p["t5_table"])
    sink = p["swa_sink"][0][:, None] * jnp.ones((1, HEAD_DIM), F32)
    nm, nl = p["norm_mix"], p["norm_mlp"]
    pending = [t for t in tokens if t is not None]

    def send(tag, weight, dw):
        token = push(tag, weight, dw[None])
        if token is not None:
            pending.append(token)

    def tied(operand):
        while pending:
            operand = operand + pending.pop()[0:1, 0:1]
        return operand

    def mlp_bwd(tag, saved, gain, dy, dy16):
        xs, u, h = saved
        w_up, w_down = (fetch("w_mlp_up", None), tag), (fetch("w_mlp_down", None), tag)
        du = _matmul_nt(f"mlp_down{tag}_bwd", dy16, w_down, tm=TM_WIDE, tn=D_FF, out_dtype=_MXU, relu_of=u)
        send(f"mlp_down{tag}", "w_mlp_down", _matmul_tn(f"mlp_down{tag}_dw", u, dy16, tk=2048, tn=1024, tt=1024, out_dtype=_WIRE, relu2=True))
        dx, dx16, dgain = _matmul_nt_normbwd(f"mlp_up{tag}_bwd", du, w_up, xs, tied(gain), dy, tm=TM_WIDE)
        send(f"mlp_up{tag}", "w_mlp_up", _matmul_tn(f"mlp_up{tag}_dw", h, du, tk=1024, tn=2048, tt=1024, out_dtype=_WIRE))
        return dx, dx16, dgain

    w_in_even = fetch("w_in_even", None)
    proj0, h0 = _norm_matmul("in_even", x, tied(nm[0:1]), w_in_even, tm=TM_WIDE, tn=EVEN_IN, out_dtype=F32)
    qr, kr, qn, kn = _prep_even(proj0, cos_r, sin_r, p["swa_q_norm"], p["swa_k_norm"], tm=TM)
    sf, sb = _ret_scan("ret_scan_fwd", kr, proj0, 1, rw["cf"], rw["dec_f"], rw["cb"], rw["dec_b"])
    ret_o, ya = _ret_out(qr, kr, proj0, sf, sb, tabs[:, (TAB_D, TAB_A, TAB_B)], p["ret_norm"])
    yb = _swa_fwd(qn, kn, proj0, bias, sink)
    w_out_even = fetch("w_out_even", yb)
    x1 = _matmul_res("out_even", [ya, yb], w_out_even, x, tm=TM)
    x2, mlp0 = _mlp_fwd(0, x1, nl[0:1], fetch)
    w_in_odd, w_out_odd = fetch("w_in_odd", x2), fetch("w_out_odd", x2)
    proj1, h1 = _norm_matmul("in_odd", x2, nm[1:2], w_in_odd, tm=TM, tn=ODD_IN, out_dtype=F32)
    qx, kx, vx = _prep_odd(proj1, cos_a, sin_a, p["ax_q_norm"], p["ax_k_norm"], tm=TM)
    o, lse = _flash_fwd(qx, kx, vx, tq=2048, tk=1024)
    x3 = _matmul_res("out_odd", [o], w_out_odd, x2, tm=TM)
    (g4, g4_16, loss_part), mlp1 = _mlp_fwd(1, x3, nl[1:2], fetch, target=target)

    dx3, dx3_16, dnl1 = mlp_bwd(1, mlp1, nl[1:2], g4, g4_16)
    do = _matmul_nt("out_odd_bwd", dx3_16, w_out_odd, tm=TM, tn=1024, out_dtype=_MXU)
    send("out_odd", "w_out_odd", _matmul_tn("out_odd_dw", o, dx3_16, tk=1024, tn=1024, tt=1024, out_dtype=_WIRE))
    dqxt, dkx, dvx = _flash_bwd(qx, kx, vx, o, do, lse, tq=2048, tk=512)
    dproj1, dqg1, dkg1 = _post_odd(proj1, dqxt, dkx, dvx, cos_a, sin_a, tied(p["ax_q_norm"]), p["ax_k_norm"], tm=TM)
    send("in_odd", "w_in_odd", _matmul_tn("in_odd_dw", h1, dproj1, tk=1024, tn=768, tt=1024, out_dtype=_WIRE))
    dx2, dx2_16, dnm1 = _matmul_nt_normbwd("in_odd_bwd", dproj1, w_in_odd, x2, tied(nm[1:2]), dx3, tm=TM_WIDE)
    dx1, dx1_16, dnl0 = mlp_bwd(0, mlp0, nl[0:1], dx2, dx2_16)
    dycat = _matmul_nt("out_even_bwd", dx1_16, w_out_even, tm=TM, tn=2 * D_MODEL, out_dtype=F32)
    send("out_even", "w_out_even", jnp.concatenate([
        _matmul_tn("out_even_dw_ret", ya, dx1_16, tk=1024, tn=1024, tt=1024, out_dtype=_WIRE),
        _matmul_tn("out_even_dw_swa", yb, dx1_16, tk=1024, tn=1024, tt=1024, out_dtype=_WIRE)], axis=0))
    g_out, dga, dretg = _ret_gate_bwd(dycat, proj0, ret_o, tied(p["ret_norm"]), tm=TM)
    rb, rf = _ret_scan("ret_scan_bwd", qr, g_out, 0, rw["b"], rw["dec_b"], rw["a"], rw["dec_f"])
    dqr, dkr, dva, dlog = _ret_bwd(qr, kr, proj0, g_out, sf, sb, rf, rb, tabs)
    dqn, dkn, dvb, dbias, dsink = _swa_bwd(qn, kn, proj0, dycat, bias, sink)
    dt5 = _t5_bucket_reduce(dbias, _t5_bucket(_swa_rel()).astype(jnp.int32))
    dproj0, dqg0, dkg0 = _post_even(proj0, dqr, dkr, dva, dga, dqn, dkn, dvb, cos_r, sin_r,
                                    p["swa_q_norm"], p["swa_k_norm"], tm=TM_WIDE)
    send("in_even", "w_in_even", _matmul_tn("in_even_dw", h0, dproj0, tk=1024, tn=2304, tt=1024, out_dtype=_WIRE))
    dx0, _, dnm0 = _matmul_nt_normbwd("in_even_bwd", dproj0, w_in_even, x, tied(nm[0:1]), dx1, tm=TM_WIDE)

    fold = lambda part: jnp.sum(part, axis=0)
    dlam = jnp.sum(dlog, axis=1).reshape(RET_HEADS, 2).T
    small = {
        "norm_mix": jnp.stack([fold(dnm0), fold(dnm1)]),
        "norm_mlp": jnp.stack([fold(dnl0), fold(dnl1)]),
        "ret_decay_logit": (dlam * (1.0 - jnp.exp(log_gamma)))[None],
        "ret_norm": fold(dretg)[None],
        "swa_q_norm": fold(dqg0)[None], "swa_k_norm": fold(dkg0)[None],
        "swa_sink": dsink[:, 0][None],
        "t5_table": dt5[:, :T5_BUCKETS].T,
        "ax_q_norm": fold(dqg1)[None], "ax_k_norm": fold(dkg1)[None],
    }
    return loss_part, dx0, small


BIG = ("w_in_even", "w_out_even", "w_in_odd", "w_out_odd", "w_mlp_up", "w_mlp_down")
SMALL = ("norm_mix", "norm_mlp", "ret_decay_logit", "ret_norm", "swa_q_norm", "swa_k_norm", "swa_sink", "t5_table",
         "ax_q_norm", "ax_k_norm")
WEIGHTS = ("norm_mix", "norm_mlp", "w_in_even", "w_out_even", "ret_decay_logit", "ret_norm", "swa_q_norm", "swa_k_norm",
           "swa_sink", "t5_table", "w_in_odd", "w_out_odd", "ax_q_norm", "ax_k_norm", "w_mlp_up", "w_mlp_down")
SHARD_AXIS = {"w_in_even": 2, "w_out_even": 1, "w_in_odd": 2, "w_out_odd": 1, "w_mlp_up": 2, "w_mlp_down": 1}
N_CHIPS = 4
GATHER_ORDER = (("w_in_even",), ("w_out_even",), ("w_mlp_up",), ("w_mlp_down",), ("w_in_odd", "w_out_odd"))
ANY = pl.BlockSpec(memory_space=pl.ANY)
HBM = pl.BlockSpec(memory_space=pltpu.HBM)
SEM = pl.BlockSpec(memory_space=pltpu.SEMAPHORE)
SPLIT_COPY = pltpu.CompilerParams(has_side_effects=pltpu.SideEffectType.DATAFLOW_SIDE_EFFECTING)


def _in_hbm(a):
    return pltpu.with_memory_space_constraint(a, pltpu.HBM)


def _mesh_pos():
    return lax.axis_index("x"), lax.axis_index("y"), lax.axis_index("c")


def _window(ref, axis, start, size):
    idx = [slice(None)] * len(ref.shape)
    idx[axis] = pl.ds(start, size)
    return ref.at[tuple(idx)]


def _cast_place(key, shard, chip, *, tr=256):
    L, R, C = shard.shape
    tr = min(tr, R)
    axis = SHARD_AXIS[key]
    whole = tuple(d * (N_CHIPS if a == axis else 1) for a, d in enumerate(shard.shape))

    def body(chip_ref, s_ref, o_ref):
        o_ref[...] = s_ref[...].astype(o_ref.dtype)

    if axis == 2:
        out_map = lambda l, i, chip_ref: (l, i, chip_ref[0])
    else:
        out_map = lambda l, i, chip_ref: (l, i + chip_ref[0] * (R // tr), 0)
    grid_spec = pltpu.PrefetchScalarGridSpec(
        num_scalar_prefetch=1, grid=(L, R // tr),
        in_specs=[pl.BlockSpec((1, tr, C), lambda l, i, chip_ref: (l, i, 0))],
        out_specs=pl.BlockSpec((1, tr, C), out_map))
    return pl.pallas_call(body, name=f"cast_place_{key}", grid_spec=grid_spec, out_shape=_sds(whole, _MXU),
                          compiler_params=_params("parallel", "parallel"))(chip, shard)


def _gather_copies(names, refs, send_sems, recv_sems, *, outgoing=True, incoming=True):
    x, y, c = _mesh_pos()
    chips = [(1 - x, y), (x, 1 - y), (1 - x, 1 - y)]
    out, inc = [], []
    for t, key in enumerate(names):
        size = refs[t].shape[SHARD_AXIS[key]] // N_CHIPS
        slot = lambda px, py: _window(refs[t], SHARD_AXIS[key], pl.multiple_of((2 * px + py) * size, 128), size)
        for k, (px, py) in enumerate(chips):
            sems = dict(send_sem=send_sems.at[3 * t + k], recv_sem=recv_sems.at[3 * t + k], device_id=(px, py, c), device_id_type=MESH)
            if outgoing:
                out.append(pltpu.make_async_remote_copy(slot(x, y), slot(x, y), **sems))
            if incoming:
                inc.append(pltpu.make_async_remote_copy(slot(x, y), slot(px, py), **sems))
    return out, inc


def _allgather_start(groups):
    names = [list(g) for g in groups]
    flat = [g[k] for g in groups for k in g]
    n, ng = len(flat), len(groups)

    def body(*refs):
        start = 0
        for gi, keys in enumerate(names):
            copies, _ = _gather_copies(keys, refs[start:start + len(keys)], refs[n + 2 * gi], refs[n + 2 * gi + 1], incoming=False)
            for cp in copies:
                cp.start()
            start += len(keys)
        token = refs[-1]
        token[...] = jnp.zeros_like(token)

    sem_shapes = [pltpu.SemaphoreType.DMA((3 * len(keys),)) for keys in names for _ in (0, 1)]
    outs = pl.pallas_call(
        body, name="allgather_start", in_specs=[HBM] * n,
        out_specs=[SEM] * (2 * ng) + [HBM] * n + [pl.BlockSpec(memory_space=pltpu.VMEM)],
        out_shape=sem_shapes + [pltpu.HBM(a.shape, a.dtype) for a in flat] + [_sds((8, HEAD_DIM), F32)],
        input_output_aliases={t: 2 * ng + t for t in range(n)},
        compiler_params=SPLIT_COPY,
    )(*[_in_hbm(a) for a in flat])
    states, start = [], 2 * ng
    for gi, keys in enumerate(names):
        states.append((gi, keys, outs[2 * gi], outs[2 * gi + 1], outs[start:start + len(keys)]))
        start += len(keys)
    return states, outs[-1]


def _allgather_wait(state, after):
    gi, names, send_sems, recv_sems, thru = state
    n = len(names)

    def body(*refs):
        outgoing, incoming = _gather_copies(names, refs[:n], refs[n], refs[n + 1])
        for cp in outgoing:
            cp.wait_send()
        for cp in incoming:
            cp.wait_recv()

    outs = pl.pallas_call(
        body, name=f"allgather_wait_{gi}", in_specs=[HBM] * n + [SEM, SEM, ANY], out_specs=[HBM] * n,
        out_shape=[pltpu.HBM(t.shape, t.dtype) for t in thru],
        input_output_aliases={t: t for t in range(n)},
        compiler_params=SPLIT_COPY,
    )(*thru, send_sems, recv_sems, after)
    return dict(zip(names, outs))


FLIPS = [(a, b, d) for a in (0, 1) for b in (0, 1) for d in (0, 1) if (a, b, d) != (0, 0, 0)]


def _flip(pos, f):
    return tuple(1 - p if fi else p for p, fi in zip(pos, f))


def _piece_shape(weight, shape):
    out = list(shape)
    out[SHARD_AXIS[weight]] //= N_CHIPS
    out[1] //= 2
    return tuple(out)


def _piece(ref, weight, chip, core):
    piece = _piece_shape(weight, ref.shape)
    if SHARD_AXIS[weight] == 1:
        return _window(ref, 1, pl.multiple_of((2 * chip + core) * piece[1], 8), piece[1])
    return _window(_window(ref, 2, pl.multiple_of(chip * piece[2], 128), piece[2]), 1, pl.multiple_of(core * piece[1], 8), piece[1])


def _scatter_copies(weight, grad_ref, land_ref, send_sems, recv_sems, *, outgoing=True, incoming=True):
    pos = _mesh_pos()
    out, inc = [], []
    for k, f in enumerate(FLIPS):
        peer = _flip(pos, f)
        sems = dict(send_sem=send_sems.at[k], recv_sem=recv_sems.at[k], device_id=peer, device_id_type=MESH)
        if outgoing:
            out.append(pltpu.make_async_remote_copy(_piece(grad_ref, weight, 2 * peer[0] + peer[1], peer[2]), land_ref.at[k], **sems))
        if incoming:
            inc.append(pltpu.make_async_remote_copy(_piece(grad_ref, weight, 2 * pos[0] + pos[1], pos[2]), land_ref.at[k], **sems))
    return out, inc


def _scatter_start(tag, weight, grad):
    n_peer = len(FLIPS)
    land = lax.empty((n_peer,) + _piece_shape(weight, grad.shape), grad.dtype)

    def body(grad_ref, land_ref, send_sems, recv_sems, grad_thru, land_thru, token):
        copies, _ = _scatter_copies(weight, grad_ref, land_ref, send_sems, recv_sems, incoming=False)
        for cp in copies:
            cp.start()
        token[...] = jnp.zeros_like(token)

    outs = pl.pallas_call(
        body, name=f"scatter_start_{tag}", in_specs=[HBM, HBM],
        out_specs=[SEM, SEM, HBM, HBM, pl.BlockSpec(memory_space=pltpu.VMEM)],
        out_shape=[pltpu.SemaphoreType.DMA((n_peer,)), pltpu.SemaphoreType.DMA((n_peer,)),
                   pltpu.HBM(grad.shape, grad.dtype), pltpu.HBM(land.shape, land.dtype), _sds((8, HEAD_DIM), F32)],
        input_output_aliases={0: 2, 1: 3},
        compiler_params=SPLIT_COPY,
    )(_in_hbm(grad), _in_hbm(land))
    return (tag, weight, outs[:4]), outs[4]


def _scatter_wait(state, after):
    tag, weight, (send_sems, recv_sems, grad_thru, land_thru) = state

    def body(grad_ref, land_ref, send_ref, recv_ref, after_ref, grad_out, land_out):
        outgoing, incoming = _scatter_copies(weight, grad_ref, land_ref, send_ref, recv_ref)
        for cp in outgoing:
            cp.wait_send()
        for cp in incoming:
            cp.wait_recv()

    return pl.pallas_call(
        body, name=f"scatter_wait_{tag}", in_specs=[HBM, HBM, SEM, SEM, ANY], out_specs=[HBM, HBM],
        out_shape=[pltpu.HBM(grad_thru.shape, grad_thru.dtype), pltpu.HBM(land_thru.shape, land_thru.dtype)],
        input_output_aliases={0: 0, 1: 1},
        compiler_params=SPLIT_COPY,
    )(grad_thru, land_thru, send_sems, recv_sems, after)


def _sum_pieces(tag, weight, grad, land, where, *, tr=256):
    _, R, C = _piece_shape(weight, grad.shape)
    tr = min(tr, R)
    nr = R // tr

    def body(where_ref, g_ref, l_ref, o_ref):
        acc = g_ref[...].astype(F32)
        for s in range(len(FLIPS)):
            acc = acc + l_ref[s].astype(F32)
        o_ref[...] = acc

    if SHARD_AXIS[weight] == 1:
        own = lambda i, where_ref: (0, (2 * where_ref[0] + where_ref[1]) * nr + i, 0)
    else:
        own = lambda i, where_ref: (0, where_ref[1] * nr + i, where_ref[0])
    grid_spec = pltpu.PrefetchScalarGridSpec(
        num_scalar_prefetch=1, grid=(nr,),
        in_specs=[pl.BlockSpec((1, tr, C), own), pl.BlockSpec((len(FLIPS), 1, tr, C), lambda i, where_ref: (0, 0, i, 0))],
        out_specs=pl.BlockSpec((1, tr, C), lambda i, where_ref: (0, where_ref[1] * nr + i, 0)))
    return pl.pallas_call(body, name=f"sum_{tag}", grid_spec=grid_spec, out_shape=_sds((1, 2 * R, C), F32),
                          compiler_params=_params("parallel"))(where, grad, land)


def _exchange_halves(shards):
    names = list(shards)
    n = len(names)
    half_sizes = [shards[k].shape[1] // 2 for k in names]

    def body(*refs):
        outs = refs[n:2 * n]
        send_sems, recv_sems = refs[2 * n:]
        x, y, c = _mesh_pos()
        half = lambda t, core: _window(outs[t], 1, pl.multiple_of(core * half_sizes[t], 8), half_sizes[t])
        sends = []
        for t in range(n):
            sends.append(pltpu.make_async_remote_copy(half(t, c), half(t, c), send_sems.at[t], recv_sems.at[t],
                                                      device_id=(x, y, 1 - c), device_id_type=MESH))
            sends[-1].start()
        for t in range(n):
            pltpu.make_async_remote_copy(half(t, c), half(t, 1 - c), send_sems.at[t], recv_sems.at[t],
                                         device_id=(x, y, 1 - c), device_id_type=MESH).wait_recv()
        for cp in sends:
            cp.wait_send()

    outs = pl.pallas_call(
        body, name="exchange_halves", in_specs=[ANY] * n, out_specs=[ANY] * n,
        out_shape=[_sds(shards[k].shape, F32) for k in names],
        input_output_aliases={t: t for t in range(n)},
        scratch_shapes=[pltpu.SemaphoreType.DMA((n,)), pltpu.SemaphoreType.DMA((n,))],
    )(*[shards[k] for k in names])
    return dict(zip(names, outs))


def _adamw_math(w, g, m, v):
    m = ADAM_B1 * m + (1.0 - ADAM_B1) * g
    v = ADAM_B2 * v + (1.0 - ADAM_B2) * jnp.square(g)
    m_hat = m / (1.0 - ADAM_B1 ** ADAM_STEP)
    v_hat = v / (1.0 - ADAM_B2 ** ADAM_STEP)
    return -ADAM_LR * (m_hat / (jnp.sqrt(v_hat) + ADAM_EPS) + ADAM_WD * w), m, v


def _adamw(name, w, g, m, v, *, tr=512):
    R, C = w.shape
    tr = min(tr, R)

    def body(w_ref, g_ref, m_ref, v_ref, d_ref, mo_ref, vo_ref):
        d_ref[...], mo_ref[...], vo_ref[...] = _adamw_math(w_ref[...], g_ref[...], m_ref[...], v_ref[...])

    spec = pl.BlockSpec((tr, C), lambda i: (i, 0))
    return pl.pallas_call(body, name=name, grid=(R // tr,), in_specs=[spec] * 4, out_specs=[spec] * 3,
                          out_shape=[_sds((R, C), F32)] * 3, compiler_params=_params("parallel"))(w, g, m, v)


SLAB_ROWS = 8
LOSS_ROW = 7


def _pack_small(d):
    pad = lambda a, width: jnp.pad(a.reshape(-1), (0, width - a.size))
    row5 = jnp.concatenate([d["swa_q_norm"].reshape(-1), d["swa_k_norm"].reshape(-1), d["ax_q_norm"].reshape(-1),
                            d["ax_k_norm"].reshape(-1), pad(d["swa_sink"], HEAD_DIM), pad(d["ret_decay_logit"], HEAD_DIM),
                            jnp.zeros((2 * HEAD_DIM,), F32)])
    return jnp.concatenate([d["norm_mix"], d["norm_mlp"], d["ret_norm"], row5[None], pad(d["t5_table"], D_MODEL)[None],
                            jnp.zeros((1, D_MODEL), F32)], axis=0)


def _unpack_small(slab):
    r5 = slab[5]
    return {
        "norm_mix": slab[0:2], "norm_mlp": slab[2:4], "ret_norm": slab[4:5],
        "swa_q_norm": r5[None, 0:128], "swa_k_norm": r5[None, 128:256], "ax_q_norm": r5[None, 256:384],
        "ax_k_norm": r5[None, 384:512], "swa_sink": r5[None, 512:512 + SWA_HEADS],
        "ret_decay_logit": r5[640:640 + 2 * RET_HEADS].reshape(1, 2, RET_HEADS),
        "t5_table": slab[6, :T5_BUCKETS * SWA_HEADS].reshape(T5_BUCKETS, SWA_HEADS),
    }


def _small_allreduce_adamw(g_slab, w_slab, m_slab, v_slab, loss_part):
    def body(g_ref, w_ref, m_ref, v_ref, lp_ref, go_ref, d_ref, mo_ref, vo_ref, gath, send_sems, recv_sems):
        pos = _mesh_pos()
        ident = lambda p: 4 * p[0] + 2 * p[1] + p[2]
        me = ident(pos)
        row = lax.broadcasted_iota(jnp.int32, (SLAB_ROWS, D_MODEL), 0)
        lane = lax.broadcasted_iota(jnp.int32, (SLAB_ROWS, D_MODEL), 1)
        loss = jnp.sum(jnp.sum(lp_ref[...], axis=0, keepdims=True), axis=1, keepdims=True) * (0.5 / D_MODEL)
        gath[me] = jnp.where(jnp.logical_and(row == LOSS_ROW, lane == 0), loss, g_ref[...])
        sends = []
        for k, f in enumerate(FLIPS):
            sends.append(pltpu.make_async_remote_copy(gath.at[me], gath.at[me], send_sems.at[k], recv_sems.at[k],
                                                      device_id=_flip(pos, f), device_id_type=MESH))
            sends[-1].start()
        for k, f in enumerate(FLIPS):
            peer = _flip(pos, f)
            pltpu.make_async_remote_copy(gath.at[me], gath.at[ident(peer)], send_sems.at[k], recv_sems.at[k],
                                         device_id=peer, device_id_type=MESH).wait_recv()
        for cp in sends:
            cp.wait_send()
        total = gath[0]
        for s in range(1, N_DEV):
            total = total + gath[s]
        go_ref[...] = total
        d_ref[...], mo_ref[...], vo_ref[...] = _adamw_math(w_ref[...], total, m_ref[...], v_ref[...])

    vmem = pl.BlockSpec(memory_space=pltpu.VMEM)
    return pl.pallas_call(
        body, name="small_allreduce_adamw", in_specs=[vmem] * 5, out_specs=[vmem] * 4,
        out_shape=[_sds((SLAB_ROWS, D_MODEL), F32)] * 4,
        scratch_shapes=[pltpu.VMEM((N_DEV, SLAB_ROWS, D_MODEL), F32),
                        pltpu.SemaphoreType.DMA((len(FLIPS),)), pltpu.SemaphoreType.DMA((len(FLIPS),))],
    )(g_slab, w_slab, m_slab, v_slab, loss_part)


def kernel(x, norm_mix, norm_mlp, w_in_even, w_out_even, ret_decay_logit, ret_norm, swa_q_norm, swa_k_norm, swa_sink, t5_table, w_in_odd, w_out_odd, ax_q_norm, ax_k_norm, w_mlp_up, w_mlp_down, loss_target, m_norm_mix, m_norm_mlp, m_w_in_even, m_w_out_even, m_ret_decay_logit, m_ret_norm, m_swa_q_norm, m_swa_k_norm, m_swa_sink, m_t5_table, m_w_in_odd, m_w_out_odd, m_ax_q_norm, m_ax_k_norm, m_w_mlp_up, m_w_mlp_down, v_norm_mix, v_norm_mlp, v_w_in_even, v_w_out_even, v_ret_decay_logit, v_ret_norm, v_swa_q_norm, v_swa_k_norm, v_swa_sink, v_t5_table, v_w_in_odd, v_w_out_odd, v_ax_q_norm, v_ax_k_norm, v_w_mlp_up, v_w_mlp_down):
    w = dict(zip(WEIGHTS, (norm_mix, norm_mlp, w_in_even, w_out_even, ret_decay_logit, ret_norm, swa_q_norm, swa_k_norm,
                           swa_sink, t5_table, w_in_odd, w_out_odd, ax_q_norm, ax_k_norm, w_mlp_up, w_mlp_down)))
    m = dict(zip(WEIGHTS, (m_norm_mix, m_norm_mlp, m_w_in_even, m_w_out_even, m_ret_decay_logit, m_ret_norm, m_swa_q_norm,
                           m_swa_k_norm, m_swa_sink, m_t5_table, m_w_in_odd, m_w_out_odd, m_ax_q_norm, m_ax_k_norm,
                           m_w_mlp_up, m_w_mlp_down)))
    v = dict(zip(WEIGHTS, (v_norm_mix, v_norm_mlp, v_w_in_even, v_w_out_even, v_ret_decay_logit, v_ret_norm, v_swa_q_norm,
                           v_swa_k_norm, v_swa_sink, v_t5_table, v_w_in_odd, v_w_out_odd, v_ax_q_norm, v_ax_k_norm,
                           v_w_mlp_up, v_w_mlp_down)))
    flat = lambda a: a.reshape(-1, a.shape[-1])

    chip = (2 * lax.axis_index("x") + lax.axis_index("y")).astype(jnp.int32)
    where = jnp.stack([chip, lax.axis_index("c").astype(jnp.int32)])

    placed = {k: _cast_place(k, w[k], where[0:1]) for k in BIG}
    gather, gather_token = _allgather_start([{k: placed[k] for k in group} for group in GATHER_ORDER])
    gathered = {}

    def fetch(name, after):
        if name not in gathered:
            state = gather[[name in group for group in GATHER_ORDER].index(True)]
            gathered.update(_allgather_wait(state, gather_token if after is None else after))
        return gathered[name] if name.startswith("w_mlp") else gathered[name][0]

    in_flight = []

    def push(tag, weight, dw):
        state, token = _scatter_start(tag, weight, dw)
        in_flight.append(state)
        return token

    loss_part, dx, small_g = _local_step(x[0], loss_target[0], {k: w[k] for k in SMALL}, fetch, push)

    halves = {}
    for state in in_flight:
        tag, weight = state[0], state[1]
        dw, land = _scatter_wait(state, dx)
        halves[tag] = _sum_pieces(tag, weight, dw, land, where)
    reduced = _exchange_halves(halves)
    grad = {"w_in_even": reduced["in_even"], "w_out_even": reduced["out_even"],
            "w_in_odd": reduced["in_odd"], "w_out_odd": reduced["out_odd"],
            "w_mlp_up": jnp.concatenate([reduced["mlp_up0"], reduced["mlp_up1"]], axis=0),
            "w_mlp_down": jnp.concatenate([reduced["mlp_down0"], reduced["mlp_down1"]], axis=0)}
    delta, new_m, new_v = {}, {}, {}
    for k in BIG:
        d_k, m_k, v_k = _adamw(f"adamw_{k}", flat(w[k]), flat(grad[k]), flat(m[k]), flat(v[k]))
        delta[k], new_m[k], new_v[k] = d_k.reshape(w[k].shape), m_k.reshape(w[k].shape), v_k.reshape(w[k].shape)

    slabs = _small_allreduce_adamw(_pack_small(small_g), _pack_small({k: w[k] for k in SMALL}),
                                   _pack_small({k: m[k] for k in SMALL}), _pack_small({k: v[k] for k in SMALL}), loss_part)
    loss = slabs[0][LOSS_ROW, 0]
    for out, slab in zip((grad, delta, new_m, new_v), slabs):
        out.update(_unpack_small(slab))

    return (loss, dx[None], *[grad[k] for k in WEIGHTS], *[delta[k] for k in WEIGHTS],
            *[new_m[k] for k in WEIGHTS], *[new_v[k] for k in WEIGHTS])
```

```python
import functools
import math

import jax
import jax.numpy as jnp
from jax import lax
from jax.experimental import pallas as pl
from jax.experimental.pallas import tpu as pltpu

F32 = jnp.float32
BF16 = jnp.bfloat16
_MXU = BF16
_WIRE = BF16

D_MODEL = 1024
HEAD_DIM = 128
EPS = 1e-6
NEG_INF = -1e30
CHUNK = 128
RET_CHUNKS_PER_STEP = 8
GRID_W = 64
RET_HEADS, RET_DK, RET_DV = 4, 128, 256
RET_Q, RET_V = RET_HEADS * RET_DK, RET_HEADS * RET_DV
RET_THETA = 10000.0
SWA_HEADS, SWA_KV_HEADS = 8, 2
T5_BUCKETS, T5_MAX_DIST = 32, 128
AX_HEADS, AX_KV_HEADS = 8, 2
AX_THETA = 10000.0
D_FF = 4 * D_MODEL
EVEN_IN = 2 * RET_Q + 2 * RET_V + D_MODEL + 2 * SWA_KV_HEADS * HEAD_DIM
ODD_IN = D_MODEL + 2 * AX_KV_HEADS * HEAD_DIM
ATT_SCALE = HEAD_DIM ** -0.5
SCORE_SCALE_LOG2 = ATT_SCALE * math.log2(math.e)

ADAM_LR, ADAM_B1, ADAM_B2, ADAM_EPS, ADAM_WD, ADAM_STEP = 0.001, 0.9, 0.999, 1e-08, 0.01, 10

N_DEV = 8
VMEM_LIMIT_BYTES = 56 << 20
MESH = pl.DeviceIdType.MESH

_NN = (((1,), (0,)), ((), ()))
_NT = (((1,), (1,)), ((), ()))
_TN = (((0,), (0,)), ((), ()))


def _dot(a, b, dn=_NN):
    return lax.dot_general(a.astype(_MXU), b.astype(_MXU), dn, preferred_element_type=F32)


def _params(*sem):
    return pltpu.CompilerParams(dimension_semantics=sem, vmem_limit_bytes=VMEM_LIMIT_BYTES)


def _sds(shape, dtype):
    return jax.ShapeDtypeStruct(tuple(shape), dtype)


def _rowsum8(x):
    return jnp.sum(x.reshape(x.shape[0] // 8, 8, x.shape[1]), axis=0)


def _swap_halves(x, half):
    width = x.shape[1]
    lane = lax.broadcasted_iota(jnp.int32, x.shape, 1)
    up = pltpu.roll(x, width - half, axis=1)
    down = pltpu.roll(x, half, axis=1)
    return jnp.where((lane & (2 * half - 1)) < half, up, down)


def _sigmoid(x):
    return 1.0 / (1.0 + jnp.exp(-x))


def _weight_spec(w, block, index_map):
    if isinstance(w, tuple):
        stacked, layer = w
        return stacked, pl.BlockSpec((None,) + block, lambda *idx: (layer,) + index_map(*idx))
    return w, pl.BlockSpec(block, index_map)


def _weight_dims(w):
    return (w[0] if isinstance(w, tuple) else w).shape[-2:]


def _norm_matmul(name, x, gain, w, *, tm, tn, out_dtype):
    T, K = x.shape
    N = _weight_dims(w)[1]
    tm, tn = min(tm, T), min(tn, N)
    w, w_spec = _weight_spec(w, (K, tn), lambda i, j: (0, j))

    def body(x_ref, g_ref, w_ref, y_ref, h_ref, h_sc):
        @pl.when(pl.program_id(1) == 0)
        def _():
            xv = x_ref[...]
            r = lax.rsqrt(jnp.mean(xv * xv, axis=-1, keepdims=True) + EPS)
            h = (xv * r * g_ref[...]).astype(_MXU)
            h_sc[...] = h
            h_ref[...] = h
        y_ref[...] = jnp.dot(h_sc[...], w_ref[...], preferred_element_type=F32).astype(y_ref.dtype)

    return pl.pallas_call(
        body, name=name, grid=(T // tm, N // tn),
        in_specs=[pl.BlockSpec((tm, K), lambda i, j: (i, 0)),
                  pl.BlockSpec((1, K), lambda i, j: (0, 0)),
                  w_spec],
        out_specs=[pl.BlockSpec((tm, tn), lambda i, j: (i, j)),
                   pl.BlockSpec((tm, K), lambda i, j: (i, 0))],
        out_shape=[_sds((T, N), out_dtype), _sds((T, K), _MXU)],
        scratch_shapes=[pltpu.VMEM((tm, K), _MXU)],
        compiler_params=_params("parallel", "arbitrary"),
    )(x, gain, w)


def _matmul_res(name, a_list, w, res, *, tm, relu2=False, target=None):
    T = res.shape[0]
    N = _weight_dims(w)[1]
    K = a_list[0].shape[1]
    n_a = len(a_list)
    tm = min(tm, T)
    with_loss = target is not None
    w_specs = [_weight_spec(w, (K, N), functools.partial(lambda i, b: (b, 0), b=b)) for b in range(n_a)]

    def body(*refs):
        a_refs = refs[:n_a]
        w_refs = refs[n_a:2 * n_a]
        res_ref = refs[2 * n_a]
        acc = res_ref[...]
        for a_ref, w_ref in zip(a_refs, w_refs):
            a = a_ref[...]
            if relu2:
                a = jnp.square(jnp.maximum(a.astype(F32), 0.0))
            acc = acc + _dot(a, w_ref[...])
        if with_loss:
            tgt_ref, g_ref, g16_ref, loss_ref = refs[2 * n_a + 1:]
            diff = acc - tgt_ref[...]
            g = diff * (1.0 / N)
            g_ref[...] = g
            g16_ref[...] = g.astype(g16_ref.dtype)

            @pl.when(pl.program_id(0) == 0)
            def _():
                loss_ref[...] = jnp.zeros_like(loss_ref)
            loss_ref[...] += _rowsum8(diff * diff)
        else:
            refs[2 * n_a + 1][...] = acc

    row = lambda i: (i, 0)
    in_specs = [pl.BlockSpec((tm, K), row) for _ in a_list]
    in_specs += [spec for _, spec in w_specs]
    in_specs += [pl.BlockSpec((tm, N), row)]
    args = list(a_list) + [arr for arr, _ in w_specs] + [res]
    if with_loss:
        in_specs.append(pl.BlockSpec((tm, N), row))
        args.append(target)
        out_specs = [pl.BlockSpec((tm, N), row), pl.BlockSpec((tm, N), row), pl.BlockSpec((8, N), lambda i: (0, 0))]
        out_shape = [_sds((T, N), F32), _sds((T, N), _MXU), _sds((8, N), F32)]
        sem = "arbitrary"
    else:
        out_specs = pl.BlockSpec((tm, N), row)
        out_shape = _sds((T, N), F32)
        sem = "parallel"
    return pl.pallas_call(body, name=name, grid=(T // tm,), in_specs=in_specs, out_specs=out_specs,
                          out_shape=out_shape, compiler_params=_params(sem))(*args)


def _matmul_nt(name, a, w, *, tm, tn, out_dtype, relu_of=None):
    T, K = a.shape
    N = _weight_dims(w)[0]
    tm, tn = min(tm, T), min(tn, N)
    w, w_spec = _weight_spec(w, (tn, K), lambda i, j: (j, 0))

    def body(*refs):
        if relu_of is None:
            a_ref, w_ref, o_ref = refs
            o_ref[...] = _dot(a_ref[...], w_ref[...], _NT).astype(o_ref.dtype)
        else:
            a_ref, w_ref, u_ref, o_ref = refs
            da = _dot(a_ref[...], w_ref[...], _NT)
            o_ref[...] = (da * (2.0 * jnp.maximum(u_ref[...].astype(F32), 0.0))).astype(o_ref.dtype)

    in_specs = [pl.BlockSpec((tm, K), lambda i, j: (i, 0)), w_spec]
    args = [a, w]
    if relu_of is not None:
        in_specs.append(pl.BlockSpec((tm, tn), lambda i, j: (i, j)))
        args.append(relu_of)
    return pl.pallas_call(body, name=name, grid=(T // tm, N // tn), in_specs=in_specs,
                          out_specs=pl.BlockSpec((tm, tn), lambda i, j: (i, j)),
                          out_shape=_sds((T, N), out_dtype),
                          compiler_params=_params("parallel", "parallel"))(*args)


def _matmul_nt_normbwd(name, dy, w, x, gain, dres, *, tm):
    T, K = dy.shape
    N = _weight_dims(w)[0]
    tm = min(tm, T)
    w, w_spec = _weight_spec(w, (N, K), lambda i: (0, 0))

    def body(dy_ref, w_ref, x_ref, g_ref, dres_ref, dx_ref, dx16_ref, dg_ref):
        dh = _dot(dy_ref[...], w_ref[...], _NT)
        xv = x_ref[...]
        r = lax.rsqrt(jnp.mean(xv * xv, axis=-1, keepdims=True) + EPS)
        xhat = xv * r
        dxhat = dh * g_ref[...]
        dx = dres_ref[...] + r * (dxhat - xhat * jnp.mean(dxhat * xhat, axis=-1, keepdims=True))
        dx_ref[...] = dx
        dx16_ref[...] = dx.astype(dx16_ref.dtype)

        @pl.when(pl.program_id(0) == 0)
        def _():
            dg_ref[...] = jnp.zeros_like(dg_ref)
        dg_ref[...] += _rowsum8(dh * xhat)

    row = lambda i: (i, 0)
    return pl.pallas_call(
        body, name=name, grid=(T // tm,),
        in_specs=[pl.BlockSpec((tm, K), row), w_spec,
                  pl.BlockSpec((tm, N), row), pl.BlockSpec((1, N), lambda i: (0, 0)), pl.BlockSpec((tm, N), row)],
        out_specs=[pl.BlockSpec((tm, N), row), pl.BlockSpec((tm, N), row), pl.BlockSpec((8, N), lambda i: (0, 0))],
        out_shape=[_sds((T, N), F32), _sds((T, N), _MXU), _sds((8, N), F32)],
        compiler_params=_params("arbitrary"),
    )(dy, w, x, gain, dres)


def _matmul_tn(name, a, b, *, tk, tn, tt, out_dtype, relu2=False):
    T, Ka = a.shape
    Nb = b.shape[1]
    tk, tn, tt = min(tk, Ka), min(tn, Nb), min(tt, T)
    nt = T // tt

    def body(a_ref, b_ref, o_ref, acc):
        t = pl.program_id(2)

        @pl.when(t == 0)
        def _():
            acc[...] = jnp.zeros_like(acc)
        av = a_ref[...]
        if relu2:
            av = jnp.square(jnp.maximum(av.astype(F32), 0.0))
        acc[...] += _dot(av, b_ref[...], _TN)

        @pl.when(t == nt - 1)
        def _():
            o_ref[...] = acc[...].astype(o_ref.dtype)

    return pl.pallas_call(
        body, name=name, grid=(Ka // tk, Nb // tn, nt),
        in_specs=[pl.BlockSpec((tt, tk), lambda i, j, t: (t, i)), pl.BlockSpec((tt, tn), lambda i, j, t: (t, j))],
        out_specs=pl.BlockSpec((tk, tn), lambda i, j, t: (i, j)),
        out_shape=_sds((Ka, Nb), out_dtype),
        scratch_shapes=[pltpu.VMEM((tk, tn), F32)],
        compiler_params=_params("parallel", "parallel", "arbitrary"),
    )(a, b)


def _rope_angles(pos, dim, theta):
    inv = theta ** (-jnp.arange(0, dim, 2, dtype=F32) / dim)
    return pos.astype(F32)[:, None] * inv[None, :]


def _ret_rope_tables(T):
    ang = _rope_angles(jnp.arange(T), RET_DK, RET_THETA)
    c, s = jnp.cos(ang), jnp.sin(ang)
    return jnp.concatenate([c, c], axis=1), jnp.concatenate([-s, s], axis=1)


def _axial_rope_tables(T):
    rows = T // GRID_W
    ar = _rope_angles(jnp.arange(rows), HEAD_DIM // 2, AX_THETA)
    ac = _rope_angles(jnp.arange(GRID_W), HEAD_DIM // 2, AX_THETA)
    by_row = lambda a: jnp.repeat(a, GRID_W, axis=0)
    by_col = lambda a: jnp.tile(a, (rows, 1))
    cos = jnp.concatenate([by_row(jnp.cos(ar)), by_row(jnp.cos(ar)), by_col(jnp.cos(ac)), by_col(jnp.cos(ac))], axis=1)
    sin = jnp.concatenate([by_row(-jnp.sin(ar)), by_row(jnp.sin(ar)), by_col(-jnp.sin(ac)), by_col(jnp.sin(ac))], axis=1)
    return cos, sin


(TAB_D, TAB_DT, TAB_EF, TAB_EB, TAB_A, TAB_B, TAB_CF, TAB_CB,
 TAB_RA, TAB_RB, TAB_RCF, TAB_RCB, TAB_KF, TAB_KB) = range(14)


def _retention_tables(decay_logit):
    lg = jax.nn.log_sigmoid(decay_logit.astype(F32))
    lam, mu = lg[0][:, None, None], lg[1][:, None, None]
    idx = jnp.arange(CHUNK, dtype=F32)
    diff = (idx[:, None] - idx[None, :])[None]
    df = jnp.where(diff >= 0, jnp.exp(jnp.maximum(diff, 0.0) * lam), 0.0)
    db = jnp.where(diff < 0, jnp.exp(jnp.maximum(-diff, 0.0) * mu), 0.0)
    d = df + db
    r = idx[None, :, None]
    ones = jnp.ones((1, 1, CHUNK), F32)
    a = jnp.exp((r + 1.0) * lam) * ones
    b = jnp.exp((CHUNK - r) * mu) * ones
    cf = jnp.exp((CHUNK - 1.0 - r) * lam) * ones
    cb = jnp.exp(r * mu) * ones
    full = jnp.ones((1, CHUNK, CHUNK), F32)
    kf = CHUNK * jnp.exp(CHUNK * lam) * full
    kb = CHUNK * jnp.exp(CHUNK * mu) * full
    tabs = jnp.stack([d, jnp.swapaxes(d, 1, 2), diff * df, -diff * db, a, b, cf, cb,
                      (r + 1.0) * a, (CHUNK - r) * b, (CHUNK - 1.0 - r) * cf, r * cb, kf, kb], axis=1)

    def lanes(tab):
        return jnp.transpose(tab, (1, 0, 2)).reshape(CHUNK, RET_HEADS * CHUNK)

    def dec(l):
        return jnp.exp(CHUNK * l)[:, 0, :] * jnp.ones((1, RET_DV), F32)

    weights = dict(a=lanes(a), b=lanes(b), cf=lanes(cf), cb=lanes(cb), dec_f=dec(lam), dec_b=dec(mu))
    return tabs, weights, lg


def _t5_bucket(rel):
    nb = T5_BUCKETS // 2
    max_exact = nb // 2
    ret = jnp.where(rel > 0, nb, 0)
    n = jnp.abs(rel)
    nf = jnp.maximum(n, 1).astype(F32)
    large = max_exact + (jnp.log(nf / max_exact) / math.log(T5_MAX_DIST / max_exact)
                         * (nb - max_exact)).astype(jnp.int32)
    large = jnp.minimum(large, nb - 1)
    return ret + jnp.where(n < max_exact, n, large)


def _swa_rel():
    r = jnp.arange(CHUNK)
    j = jnp.arange(3 * CHUNK)
    return j[None, :] - CHUNK - r[:, None]


def _swa_bias(t5_table):
    rel = _swa_rel()
    bucket = jnp.where(jnp.abs(rel) <= CHUNK, _t5_bucket(rel), -1).astype(jnp.int32)

    def body(tab_ref, bk_ref, o_ref):
        bk = bk_ref[...]
        for h in range(SWA_HEADS):
            pick = lambda b, acc, h=h: jnp.where(bk == b, tab_ref[b, h], acc)
            o_ref[h] = lax.fori_loop(0, T5_BUCKETS, pick, jnp.full(bk.shape, NEG_INF, F32))

    return pl.pallas_call(
        body, name="t5_bias",
        in_specs=[pl.BlockSpec(memory_space=pltpu.SMEM), pl.BlockSpec(memory_space=pltpu.VMEM)],
        out_specs=pl.BlockSpec(memory_space=pltpu.VMEM),
        out_shape=_sds((SWA_HEADS, CHUNK, 3 * CHUNK), F32),
    )(t5_table.astype(F32), bucket)


def _prep_even(proj, cos, sin, q_gain, k_gain, *, tm):
    T = proj.shape[0]
    tm = min(tm, T)

    def body(qa_ref, ka_ref, qb_ref, kb_ref, c_ref, s_ref, qg_ref, kg_ref, qr_ref, kr_ref, qn_ref, kn_ref):
        c = jnp.concatenate([c_ref[...]] * RET_HEADS, axis=1)
        s = jnp.concatenate([s_ref[...]] * RET_HEADS, axis=1)
        qa = qa_ref[...]
        qr_ref[...] = (qa * c + _swap_halves(qa, RET_DK // 2) * s).astype(qr_ref.dtype)
        ka = ka_ref[...]
        kr_ref[...] = ((ka * c + _swap_halves(ka, RET_DK // 2) * s) * (RET_DK ** -0.5)).astype(kr_ref.dtype)
        for src, gain, dst, heads in ((qb_ref, qg_ref, qn_ref, SWA_HEADS), (kb_ref, kg_ref, kn_ref, SWA_KV_HEADS)):
            for h in range(heads):
                sl = slice(h * HEAD_DIM, (h + 1) * HEAD_DIM)
                xh = src[:, sl]
                r = lax.rsqrt(jnp.mean(xh * xh, axis=-1, keepdims=True) + EPS)
                dst[:, sl] = (xh * r * gain[...]).astype(dst.dtype)

    row = lambda i: (i, 0)
    const = lambda i: (0, 0)
    return pl.pallas_call(
        body, name="prep_even", grid=(T // tm,),
        in_specs=[pl.BlockSpec((tm, RET_Q), lambda i: (i, 0)), pl.BlockSpec((tm, RET_Q), lambda i: (i, 1)),
                  pl.BlockSpec((tm, D_MODEL), lambda i: (i, 3)), pl.BlockSpec((tm, 256), lambda i: (i, 16)),
                  pl.BlockSpec((tm, RET_DK), row), pl.BlockSpec((tm, RET_DK), row),
                  pl.BlockSpec((1, HEAD_DIM), const), pl.BlockSpec((1, HEAD_DIM), const)],
        out_specs=[pl.BlockSpec((tm, RET_Q), row), pl.BlockSpec((tm, RET_Q), row),
                   pl.BlockSpec((tm, D_MODEL), row), pl.BlockSpec((tm, 256), row)],
        out_shape=[_sds((T, RET_Q), _MXU), _sds((T, RET_Q), _MXU), _sds((T, D_MODEL), _MXU), _sds((T, 256), _MXU)],
        compiler_params=_params("parallel"),
    )(proj, proj, proj, proj, cos, sin, q_gain, k_gain)


def _ret_scan(name, x, y, y_col, w_asc, dec_asc, w_desc, dec_desc):
    T = x.shape[0]
    nc = T // CHUNK
    per = min(RET_CHUNKS_PER_STEP, nc)
    nb = nc // per
    rows_per = per * CHUNK

    def body(xa_ref, ya_ref, xd_ref, yd_ref, wa_ref, da_ref, wd_ref, dd_ref, sa_out, sd_out, sa, sd):
        @pl.when(pl.program_id(0) == 0)
        def _():
            sa[...] = jnp.zeros_like(sa)
            sd[...] = jnp.zeros_like(sd)
        for step in range(per):
            for c, x_ref, y_ref, w_ref, d_ref, st, out in ((step, xa_ref, ya_ref, wa_ref, da_ref, sa, sa_out),
                                                       (per - 1 - step, xd_ref, yd_ref, wd_ref, dd_ref, sd, sd_out)):
                rows = slice(c * CHUNK, (c + 1) * CHUNK)
                out[c] = st[...].astype(out.dtype)
                for h in range(RET_HEADS):
                    ks = slice(h * RET_DK, (h + 1) * RET_DK)
                    vs = slice(h * RET_DV, (h + 1) * RET_DV)
                    u = _dot(x_ref[rows, ks].astype(F32) * w_ref[:, ks], y_ref[rows, vs], _TN)
                    st[ks, :] = st[ks, :] * d_ref[h:h + 1, :] + u

    asc = lambda i: (i, 0)
    desc = lambda i: (nb - 1 - i, 0)
    const = lambda i: (0, 0)
    return pl.pallas_call(
        body, name=name, grid=(nb,),
        in_specs=[pl.BlockSpec((rows_per, RET_Q), asc), pl.BlockSpec((rows_per, RET_V), lambda i: (i, y_col)),
                  pl.BlockSpec((rows_per, RET_Q), desc), pl.BlockSpec((rows_per, RET_V), lambda i: (nb - 1 - i, y_col)),
                  pl.BlockSpec((CHUNK, RET_Q), const), pl.BlockSpec((RET_HEADS, RET_DV), const),
                  pl.BlockSpec((CHUNK, RET_Q), const), pl.BlockSpec((RET_HEADS, RET_DV), const)],
        out_specs=[pl.BlockSpec((per, RET_Q, RET_DV), lambda i: (i, 0, 0)),
                   pl.BlockSpec((per, RET_Q, RET_DV), lambda i: (nb - 1 - i, 0, 0))],
        out_shape=[_sds((nc, RET_Q, RET_DV), _MXU), _sds((nc, RET_Q, RET_DV), _MXU)],
        scratch_shapes=[pltpu.VMEM((RET_Q, RET_DV), F32), pltpu.VMEM((RET_Q, RET_DV), F32)],
        compiler_params=_params("arbitrary"),
    )(x, y, x, y, w_asc, dec_asc, w_desc, dec_desc)


def _ret_out(qr, kr, proj, sf, sb, tabs, gain):
    T = qr.shape[0]
    nc = T // CHUNK
    per = min(RET_CHUNKS_PER_STEP, nc)
    rows_per = per * CHUNK

    def body(q_ref, k_ref, v_ref, g_ref, sf_ref, sb_ref, tab_ref, gain_ref, o_ref, y_ref):
        for c in range(per):
            rows = slice(c * CHUNK, (c + 1) * CHUNK)
            for h in range(RET_HEADS):
                ks = slice(h * RET_DK, (h + 1) * RET_DK)
                vs = slice(h * RET_DV, (h + 1) * RET_DV)
                q, k, v = q_ref[rows, ks], k_ref[rows, ks], v_ref[rows, vs]
                qf = q.astype(F32)
                a_mat = _dot(q, k, _NT) * tab_ref[h, 0]
                o = (_dot(a_mat, v) + _dot(qf * tab_ref[h, 1], sf_ref[c, ks, :]) + _dot(qf * tab_ref[h, 2], sb_ref[c, ks, :]))
                o_ref[rows, vs] = o
                r = lax.rsqrt(jnp.mean(o * o, axis=-1, keepdims=True) + EPS)
                g = g_ref[rows, vs]
                y_ref[rows, vs] = (g * _sigmoid(g) * (o * r * gain_ref[:, vs])).astype(y_ref.dtype)

    row = lambda i: (i, 0)
    st = lambda i: (i, 0, 0)
    return pl.pallas_call(
        body, name="ret_out", grid=(nc // per,),
        in_specs=[pl.BlockSpec((rows_per, RET_Q), row), pl.BlockSpec((rows_per, RET_Q), row),
                  pl.BlockSpec((rows_per, RET_V), lambda i: (i, 1)), pl.BlockSpec((rows_per, RET_V), lambda i: (i, 2)),
                  pl.BlockSpec((per, RET_Q, RET_DV), st), pl.BlockSpec((per, RET_Q, RET_DV), st),
                  pl.BlockSpec((RET_HEADS, 3, CHUNK, CHUNK), lambda i: (0, 0, 0, 0)),
                  pl.BlockSpec((1, RET_V), lambda i: (0, 0))],
        out_specs=[pl.BlockSpec((rows_per, RET_V), row), pl.BlockSpec((rows_per, RET_V), row)],
        out_shape=[_sds((T, RET_V), F32), _sds((T, RET_V), _MXU)],
        compiler_params=_params("parallel"),
    )(qr, kr, proj, proj, sf, sb, tabs, gain)


def _ret_gate_bwd(dycat, proj, ret_o, gain, *, tm):
    T = ret_o.shape[0]
    tm = min(tm, T)

    def body(dy_ref, g_ref, o_ref, gain_ref, do_ref, dg_ref, dgain_ref):
        @pl.when(pl.program_id(0) == 0)
        def _():
            dgain_ref[...] = jnp.zeros_like(dgain_ref)
        for h in range(RET_HEADS):
            vs = slice(h * RET_DV, (h + 1) * RET_DV)
            o, g, dya, gn = o_ref[:, vs], g_ref[:, vs], dy_ref[:, vs], gain_ref[:, vs]
            r = lax.rsqrt(jnp.mean(o * o, axis=-1, keepdims=True) + EPS)
            ohat = o * r
            sg = _sigmoid(g)
            dy = dya * (g * sg)
            dg_ref[:, vs] = (dya * (ohat * gn) * (sg * (1.0 + g * (1.0 - sg)))).astype(dg_ref.dtype)
            dyg = dy * gn
            do_ref[:, vs] = (r * (dyg - ohat * jnp.mean(dyg * ohat, axis=-1, keepdims=True))).astype(do_ref.dtype)
            dgain_ref[:, vs] += _rowsum8(dy * ohat)

    row = lambda i: (i, 0)
    return pl.pallas_call(
        body, name="ret_gate_bwd", grid=(T // tm,),
        in_specs=[pl.BlockSpec((tm, RET_V), row), pl.BlockSpec((tm, RET_V), lambda i: (i, 2)),
                  pl.BlockSpec((tm, RET_V), row), pl.BlockSpec((1, RET_V), lambda i: (0, 0))],
        out_specs=[pl.BlockSpec((tm, RET_V), row), pl.BlockSpec((tm, RET_V), row), pl.BlockSpec((8, RET_V), lambda i: (0, 0))],
        out_shape=[_sds((T, RET_V), _MXU), _sds((T, RET_V), _MXU), _sds((8, RET_V), F32)],
        compiler_params=_params("arbitrary"),
    )(dycat, proj, ret_o, gain)


def _ret_bwd(qr, kr, proj, g_out, sf, sb, rf, rb, tabs):
    T = qr.shape[0]
    nc = T // CHUNK
    per = min(RET_CHUNKS_PER_STEP, nc)
    rows_per = per * CHUNK

    def body(q_ref, k_ref, v_ref, g_ref, sf_ref, sb_ref, rf_ref, rb_ref, tab_ref, dq_ref, dk_ref, dv_ref, dl_ref):
        @pl.when(pl.program_id(0) == 0)
        def _():
            dl_ref[...] = jnp.zeros_like(dl_ref)
        for c in range(per):
            rows = slice(c * CHUNK, (c + 1) * CHUNK)
            for h in range(RET_HEADS):
                ks = slice(h * RET_DK, (h + 1) * RET_DK)
                vs = slice(h * RET_DV, (h + 1) * RET_DV)
                q, k, v, g = q_ref[rows, ks], k_ref[rows, ks], v_ref[rows, vs], g_ref[rows, vs]
                s_f, s_b, r_f, r_b = sf_ref[c, ks, :], sb_ref[c, ks, :], rf_ref[c, ks, :], rb_ref[c, ks, :]
                tab = lambda t, h=h: tab_ref[h, t]
                qf, kf = q.astype(F32), k.astype(F32)
                qk = _dot(q, k, _NT)
                da_raw = _dot(g, v, _NT)
                x_f, x_b = _dot(g, s_f, _NT), _dot(g, s_b, _NT)
                dq_ref[rows, ks] = _dot(da_raw * tab(TAB_D), k) + tab(TAB_A) * x_f + tab(TAB_B) * x_b
                at = _dot(k, q, _NT) * tab(TAB_DT)
                dat = _dot(v, g, _NT) * tab(TAB_DT)
                y_f, y_b = _dot(v, r_f, _NT), _dot(v, r_b, _NT)
                dk_ref[rows, ks] = _dot(dat, q) + tab(TAB_CF) * y_f + tab(TAB_CB) * y_b
                dv_ref[rows, vs] = (_dot(at, g) + _dot(kf * tab(TAB_CF), r_f) + _dot(kf * tab(TAB_CB), r_b)).astype(dv_ref.dtype)
                inner = da_raw * qk
                rs_f = r_f.astype(F32) * s_f.astype(F32)
                rs_b = r_b.astype(F32) * s_b.astype(F32)
                dl_f = (inner * tab(TAB_EF) + tab(TAB_RA) * qf * x_f + tab(TAB_RCF) * kf * y_f
                        + tab(TAB_KF) * (rs_f[:, :CHUNK] + rs_f[:, CHUNK:]))
                dl_b = (inner * tab(TAB_EB) + tab(TAB_RB) * qf * x_b + tab(TAB_RCB) * kf * y_b
                        + tab(TAB_KB) * (rs_b[:, :CHUNK] + rs_b[:, CHUNK:]))
                dl_ref[2 * h:2 * h + 1, :] += jnp.sum(dl_f, axis=0, keepdims=True)
                dl_ref[2 * h + 1:2 * h + 2, :] += jnp.sum(dl_b, axis=0, keepdims=True)

    row = lambda i: (i, 0)
    st = lambda i: (i, 0, 0)
    return pl.pallas_call(
        body, name="ret_bwd", grid=(nc // per,),
        in_specs=[pl.BlockSpec((rows_per, RET_Q), row), pl.BlockSpec((rows_per, RET_Q), row),
                  pl.BlockSpec((rows_per, RET_V), lambda i: (i, 1)), pl.BlockSpec((rows_per, RET_V), row),
                  pl.BlockSpec((per, RET_Q, RET_DV), st), pl.BlockSpec((per, RET_Q, RET_DV), st),
                  pl.BlockSpec((per, RET_Q, RET_DV), st), pl.BlockSpec((per, RET_Q, RET_DV), st),
                  pl.BlockSpec((RET_HEADS, 14, CHUNK, CHUNK), lambda i: (0, 0, 0, 0))],
        out_specs=[pl.BlockSpec((rows_per, RET_Q), row), pl.BlockSpec((rows_per, RET_Q), row),
                   pl.BlockSpec((rows_per, RET_V), row), pl.BlockSpec((8, CHUNK), lambda i: (0, 0))],
        out_shape=[_sds((T, RET_Q), F32), _sds((T, RET_Q), F32), _sds((T, RET_V), _MXU), _sds((8, CHUNK), F32)],
        compiler_params=_params("arbitrary"),
    )(qr, kr, proj, g_out, sf, sb, rf, rb, tabs)


SWA_GROUP = SWA_HEADS // SWA_KV_HEADS
SWA_COLS = SWA_GROUP * CHUNK


def _swa_stack(ref, g):
    return jnp.concatenate([ref[:, h * HEAD_DIM:(h + 1) * HEAD_DIM] for h in range(g * SWA_GROUP, (g + 1) * SWA_GROUP)], axis=0)


def _swa_probs_t(q, k_win, bias_t, sink_row, i, nb):
    st = _dot(k_win, q, _NT) * ATT_SCALE + bias_t
    key = lax.broadcasted_iota(jnp.int32, st.shape, 0)
    valid = jnp.logical_and(jnp.logical_or(key >= CHUNK, i > 0), jnp.logical_or(key < 2 * CHUNK, i < nb - 1))
    st = jnp.where(valid, st, NEG_INF)
    m = jnp.maximum(jnp.max(st, axis=0, keepdims=True), sink_row)
    p = jnp.exp(st - m)
    e_sink = jnp.exp(sink_row - m)
    inv = 1.0 / (jnp.sum(p, axis=0, keepdims=True) + e_sink)
    return p * inv, e_sink * inv


def _swa_layouts(bias, sink):
    bias_t = bias.reshape(SWA_KV_HEADS, SWA_GROUP, CHUNK, 3 * CHUNK).transpose(0, 3, 1, 2).reshape(SWA_KV_HEADS, 3 * CHUNK, SWA_COLS)
    return bias_t, jnp.repeat(sink[:, 0], CHUNK).reshape(SWA_KV_HEADS, SWA_COLS)


def _swa_window_specs(nb, width, col_block, clamp):
    prev = lambda i: (jnp.maximum(clamp(i) - 1, 0), col_block)
    cur = lambda i: (clamp(i), col_block)
    nxt = lambda i: (jnp.minimum(clamp(i) + 1, nb - 1), col_block)
    return [pl.BlockSpec((CHUNK, width), f) for f in (prev, cur, nxt)]


def _swa_fwd(qn, kn, proj, bias, sink):
    T = qn.shape[0]
    nb = T // CHUNK
    kvw = SWA_KV_HEADS * HEAD_DIM
    bias_t, sink_rows = _swa_layouts(bias, sink)

    def body(q_ref, k0, k1, k2, v0, v1, v2, bias_ref, sink_ref, y_ref):
        i = pl.program_id(0)
        for g in range(SWA_KV_HEADS):
            gs = slice(g * HEAD_DIM, (g + 1) * HEAD_DIM)
            k_win = jnp.concatenate([k0[:, gs], k1[:, gs], k2[:, gs]], axis=0)
            v_win = jnp.concatenate([v0[:, gs], v1[:, gs], v2[:, gs]], axis=0).astype(_MXU)
            pt, _ = _swa_probs_t(_swa_stack(q_ref, g), k_win, bias_ref[g], sink_ref[g:g + 1, :], i, nb)
            o = _dot(v_win, pt, _TN).T
            for hh in range(SWA_GROUP):
                h = g * SWA_GROUP + hh
                y_ref[:, h * HEAD_DIM:(h + 1) * HEAD_DIM] = o[hh * CHUNK:(hh + 1) * CHUNK].astype(y_ref.dtype)

    ident = lambda i: i
    return pl.pallas_call(
        body, name="swa_fwd", grid=(nb,),
        in_specs=[pl.BlockSpec((CHUNK, D_MODEL), lambda i: (i, 0))]
        + _swa_window_specs(nb, kvw, 0, ident) + _swa_window_specs(nb, kvw, 17, ident)
        + [pl.BlockSpec((SWA_KV_HEADS, 3 * CHUNK, SWA_COLS), lambda i: (0, 0, 0)), pl.BlockSpec((SWA_KV_HEADS, SWA_COLS), lambda i: (0, 0))],
        out_specs=pl.BlockSpec((CHUNK, D_MODEL), lambda i: (i, 0)),
        out_shape=_sds((T, D_MODEL), _MXU),
        compiler_params=_params("parallel"),
    )(qn, kn, kn, kn, proj, proj, proj, bias_t, sink_rows)


def _swa_bwd(qn, kn, proj, dycat, bias, sink):
    T = qn.shape[0]
    nb = T // CHUNK
    kvw = SWA_KV_HEADS * HEAD_DIM
    bias_t, sink_rows = _swa_layouts(bias, sink)

    def body(q_ref, k0, k1, k2, v0, v1, v2, dy_ref, bias_ref, sink_ref,
             dq_ref, dk_ref, dv_ref, dbias_ref, dsink_ref, acc_a, acc_b):
        i = pl.program_id(0)

        @pl.when(i == 0)
        def _():
            dbias_ref[...] = jnp.zeros_like(dbias_ref)
            dsink_ref[...] = jnp.zeros_like(dsink_ref)
            acc_a[...] = jnp.zeros_like(acc_a)
            acc_b[...] = jnp.zeros_like(acc_b)

        @pl.when(i < nb)
        def _():
            for g in range(SWA_KV_HEADS):
                gs = slice(g * HEAD_DIM, (g + 1) * HEAD_DIM)
                k_win = jnp.concatenate([k0[:, gs], k1[:, gs], k2[:, gs]], axis=0)
                v_win = jnp.concatenate([v0[:, gs], v1[:, gs], v2[:, gs]], axis=0).astype(_MXU)
                q, dy = _swa_stack(q_ref, g), _swa_stack(dy_ref, g)
                pt, p_sink = _swa_probs_t(q, k_win, bias_ref[g], sink_ref[g:g + 1, :], i, nb)
                dpt = _dot(v_win, dy, _NT)
                delta = jnp.sum(pt * dpt, axis=0, keepdims=True)
                dst = pt * (dpt - delta)
                dbias_ref[g] += dst
                dsink_ref[g:g + 1, :] += -p_sink * delta
                dq = (_dot(k_win, dst, _TN) * ATT_SCALE).T
                for hh in range(SWA_GROUP):
                    h = g * SWA_GROUP + hh
                    dq_ref[:, h * HEAD_DIM:(h + 1) * HEAD_DIM] = dq[hh * CHUNK:(hh + 1) * CHUNK]
                dk_win = _dot(dst, q) * ATT_SCALE
                dv_win = _dot(pt, dy)
                for win, out_ref, col0 in ((dk_win, dk_ref, 0), (dv_win, dv_ref, kvw)):
                    cs = slice(col0 + g * HEAD_DIM, col0 + (g + 1) * HEAD_DIM)
                    out_ref[:, gs] = acc_a[:, cs] + win[:CHUNK]
                    acc_a[:, cs] = acc_b[:, cs] + win[CHUNK:2 * CHUNK]
                    acc_b[:, cs] = win[2 * CHUNK:]

        @pl.when(i == nb)
        def _():
            dk_ref[...] = acc_a[:, :kvw]
            dv_ref[...] = acc_a[:, kvw:]

    clamp = lambda i: jnp.minimum(i, nb - 1)
    late = lambda i: (jnp.maximum(i - 1, 0), 0)
    bias_spec = pl.BlockSpec((SWA_KV_HEADS, 3 * CHUNK, SWA_COLS), lambda i: (0, 0, 0))
    sink_spec = pl.BlockSpec((SWA_KV_HEADS, SWA_COLS), lambda i: (0, 0))
    dq, dk, dv, dbias_t, dsink_rows = pl.pallas_call(
        body, name="swa_bwd", grid=(nb + 1,),
        in_specs=[pl.BlockSpec((CHUNK, D_MODEL), lambda i: (clamp(i), 0))]
        + _swa_window_specs(nb, kvw, 0, clamp) + _swa_window_specs(nb, kvw, 17, clamp)
        + [pl.BlockSpec((CHUNK, D_MODEL), lambda i: (clamp(i), 1)), bias_spec, sink_spec],
        out_specs=[pl.BlockSpec((CHUNK, D_MODEL), lambda i: (clamp(i), 0)),
                   pl.BlockSpec((CHUNK, kvw), late), pl.BlockSpec((CHUNK, kvw), late), bias_spec, sink_spec],
        out_shape=[_sds((T, D_MODEL), F32), _sds((T, kvw), F32), _sds((T, kvw), F32),
                   _sds((SWA_KV_HEADS, 3 * CHUNK, SWA_COLS), F32), _sds((SWA_KV_HEADS, SWA_COLS), F32)],
        scratch_shapes=[pltpu.VMEM((CHUNK, 2 * kvw), F32), pltpu.VMEM((CHUNK, 2 * kvw), F32)],
        compiler_params=_params("arbitrary"),
    )(qn, kn, kn, kn, proj, proj, proj, dycat, bias_t, sink_rows)
    dbias = dbias_t.reshape(SWA_KV_HEADS, 3 * CHUNK, SWA_GROUP, CHUNK).transpose(0, 2, 3, 1).reshape(SWA_HEADS, CHUNK, 3 * CHUNK)
    dsink = jnp.sum(dsink_rows.reshape(SWA_HEADS, CHUNK), axis=1, keepdims=True) * jnp.ones((1, HEAD_DIM), F32)
    return dq, dk, dv, dbias, dsink


def _t5_bucket_reduce(dbias, bucket):
    def body(db_ref, bk_ref, o_ref):
        bk = bk_ref[...]
        row = lax.broadcasted_iota(jnp.int32, (SWA_HEADS, HEAD_DIM), 0)
        lane = lax.broadcasted_iota(jnp.int32, (SWA_HEADS, HEAD_DIM), 1)

        def per_bucket(b, acc):
            mask = bk == b
            for h in range(SWA_HEADS):
                tot = jnp.sum(jnp.sum(jnp.where(mask, db_ref[h], 0.0), axis=0, keepdims=True), axis=1, keepdims=True)
                acc = acc + jnp.where(jnp.logical_and(row == h, lane == b), tot, 0.0)
            return acc

        o_ref[...] = lax.fori_loop(0, T5_BUCKETS, per_bucket, jnp.zeros((SWA_HEADS, HEAD_DIM), F32))

    return pl.pallas_call(body, name="t5_bucket_reduce", out_shape=_sds((SWA_HEADS, HEAD_DIM), F32),
                          compiler_params=pltpu.CompilerParams(vmem_limit_bytes=VMEM_LIMIT_BYTES))(dbias, bucket)


def _headnorm_bwd(x, dy, gain):
    r = lax.rsqrt(jnp.mean(x * x, axis=-1, keepdims=True) + EPS)
    xhat = x * r
    dyg = dy * gain
    return r * (dyg - xhat * jnp.mean(dyg * xhat, axis=-1, keepdims=True)), dy * xhat


def _post_even(proj, dqr, dkr, dva, dga, dqn, dkn, dvb, cos, sin, q_gain, k_gain, *, tm):
    T = proj.shape[0]
    tm = min(tm, T)
    kvw = SWA_KV_HEADS * HEAD_DIM

    def body(qb_ref, kb_ref, dqr_ref, dkr_ref, dva_ref, dga_ref, dqn_ref, dkn_ref, dvb_ref, c_ref, s_ref, qg_ref, kg_ref,
             dp_ref, dqg_ref, dkg_ref):
        @pl.when(pl.program_id(0) == 0)
        def _():
            dqg_ref[...] = jnp.zeros_like(dqg_ref)
            dkg_ref[...] = jnp.zeros_like(dkg_ref)
        c = jnp.concatenate([c_ref[...]] * RET_HEADS, axis=1)
        s = jnp.concatenate([s_ref[...]] * RET_HEADS, axis=1)
        dq = dqr_ref[...]
        dp_ref[:, 0:RET_Q] = (dq * c + _swap_halves(dq * s, RET_DK // 2)).astype(dp_ref.dtype)
        dk = dkr_ref[...] * (RET_DK ** -0.5)
        dp_ref[:, RET_Q:2 * RET_Q] = (dk * c + _swap_halves(dk * s, RET_DK // 2)).astype(dp_ref.dtype)
        off = 2 * RET_Q
        dp_ref[:, off:off + RET_V] = dva_ref[...].astype(dp_ref.dtype)
        dp_ref[:, off + RET_V:off + 2 * RET_V] = dga_ref[...].astype(dp_ref.dtype)
        off += 2 * RET_V
        for src, dsrc, gain, dgain, heads, base in ((qb_ref, dqn_ref, qg_ref, dqg_ref, SWA_HEADS, off),
                                                    (kb_ref, dkn_ref, kg_ref, dkg_ref, SWA_KV_HEADS, off + D_MODEL)):
            for h in range(heads):
                sl = slice(h * HEAD_DIM, (h + 1) * HEAD_DIM)
                dx, dgx = _headnorm_bwd(src[:, sl], dsrc[:, sl], gain[...])
                dp_ref[:, base + h * HEAD_DIM:base + (h + 1) * HEAD_DIM] = dx.astype(dp_ref.dtype)
                dgain[...] += _rowsum8(dgx)
        dp_ref[:, off + D_MODEL + kvw:] = dvb_ref[...].astype(dp_ref.dtype)

    row = lambda i: (i, 0)
    const = lambda i: (0, 0)
    return pl.pallas_call(
        body, name="post_even", grid=(T // tm,),
        in_specs=[pl.BlockSpec((tm, D_MODEL), lambda i: (i, 3)), pl.BlockSpec((tm, kvw), lambda i: (i, 16)),
                  pl.BlockSpec((tm, RET_Q), row), pl.BlockSpec((tm, RET_Q), row),
                  pl.BlockSpec((tm, RET_V), row), pl.BlockSpec((tm, RET_V), row),
                  pl.BlockSpec((tm, D_MODEL), row), pl.BlockSpec((tm, kvw), row), pl.BlockSpec((tm, kvw), row),
                  pl.BlockSpec((tm, RET_DK), row), pl.BlockSpec((tm, RET_DK), row),
                  pl.BlockSpec((1, HEAD_DIM), const), pl.BlockSpec((1, HEAD_DIM), const)],
        out_specs=[pl.BlockSpec((tm, EVEN_IN), row), pl.BlockSpec((8, HEAD_DIM), const), pl.BlockSpec((8, HEAD_DIM), const)],
        out_shape=[_sds((T, EVEN_IN), _MXU), _sds((8, HEAD_DIM), F32), _sds((8, HEAD_DIM), F32)],
        compiler_params=_params("arbitrary"),
    )(proj, proj, dqr, dkr, dva, dga, dqn, dkn, dvb, cos, sin, q_gain, k_gain)


def _prep_odd(proj, cos, sin, q_gain, k_gain, *, tm):
    T = proj.shape[0]
    tm = min(tm, T)
    kvw = AX_KV_HEADS * HEAD_DIM

    def body(q_ref, k_ref, v_ref, c_ref, s_ref, qg_ref, kg_ref, qx_ref, kx_ref, vx_ref):
        c, s = c_ref[...], s_ref[...]
        for src, gain, dst, heads, scale in ((q_ref, qg_ref, qx_ref, AX_HEADS, SCORE_SCALE_LOG2), (k_ref, kg_ref, kx_ref, AX_KV_HEADS, 1.0)):
            for h in range(heads):
                sl = slice(h * HEAD_DIM, (h + 1) * HEAD_DIM)
                xh = src[:, sl]
                r = lax.rsqrt(jnp.mean(xh * xh, axis=-1, keepdims=True) + EPS)
                xn = xh * r * gain[...]
                dst[:, sl] = ((xn * c + _swap_halves(xn, HEAD_DIM // 4) * s) * scale).astype(dst.dtype)
        vx_ref[...] = v_ref[...].astype(vx_ref.dtype)

    row = lambda i: (i, 0)
    const = lambda i: (0, 0)
    return pl.pallas_call(
        body, name="prep_odd", grid=(T // tm,),
        in_specs=[pl.BlockSpec((tm, D_MODEL), row), pl.BlockSpec((tm, kvw), lambda i: (i, 4)), pl.BlockSpec((tm, kvw), lambda i: (i, 5)),
                  pl.BlockSpec((tm, HEAD_DIM), row), pl.BlockSpec((tm, HEAD_DIM), row),
                  pl.BlockSpec((1, HEAD_DIM), const), pl.BlockSpec((1, HEAD_DIM), const)],
        out_specs=[pl.BlockSpec((tm, D_MODEL), row), pl.BlockSpec((tm, kvw), row), pl.BlockSpec((tm, kvw), row)],
        out_shape=[_sds((T, D_MODEL), _MXU), _sds((T, kvw), _MXU), _sds((T, kvw), _MXU)],
        compiler_params=_params("parallel"),
    )(proj, proj, proj, cos, sin, q_gain, k_gain)


def _post_odd(proj, dqxt, dkx, dvx, cos, sin, q_gain, k_gain, *, tm):
    T = proj.shape[0]
    tm = min(tm, T)
    kvw = AX_KV_HEADS * HEAD_DIM

    def body(q_ref, k_ref, dqt_ref, dk_ref, dv_ref, c_ref, s_ref, qg_ref, kg_ref, dp_ref, dqg_ref, dkg_ref):
        @pl.when(pl.program_id(0) == 0)
        def _():
            dqg_ref[...] = jnp.zeros_like(dqg_ref)
            dkg_ref[...] = jnp.zeros_like(dkg_ref)
        c, s = c_ref[...], s_ref[...]
        for src, dsrc, gain, dgain, heads, base in ((q_ref, dqt_ref, qg_ref, dqg_ref, AX_HEADS, 0),
                                                    (k_ref, dk_ref, kg_ref, dkg_ref, AX_KV_HEADS, D_MODEL)):
            for h in range(heads):
                sl = slice(h * HEAD_DIM, (h + 1) * HEAD_DIM)
                d = dsrc[sl, :].T if dsrc is dqt_ref else dsrc[:, sl]
                dn = d * c + _swap_halves(d * s, HEAD_DIM // 4)
                dx, dgx = _headnorm_bwd(src[:, sl], dn, gain[...])
                dp_ref[:, base + h * HEAD_DIM:base + (h + 1) * HEAD_DIM] = dx.astype(dp_ref.dtype)
                dgain[...] += _rowsum8(dgx)
        dp_ref[:, D_MODEL + kvw:] = dv_ref[...].astype(dp_ref.dtype)

    row = lambda i: (i, 0)
    const = lambda i: (0, 0)
    return pl.pallas_call(
        body, name="post_odd", grid=(T // tm,),
        in_specs=[pl.BlockSpec((tm, D_MODEL), row), pl.BlockSpec((tm, kvw), lambda i: (i, 4)),
                  pl.BlockSpec((D_MODEL, tm), lambda i: (0, i)), pl.BlockSpec((tm, kvw), row), pl.BlockSpec((tm, kvw), row),
                  pl.BlockSpec((tm, HEAD_DIM), row), pl.BlockSpec((tm, HEAD_DIM), row),
                  pl.BlockSpec((1, HEAD_DIM), const), pl.BlockSpec((1, HEAD_DIM), const)],
        out_specs=[pl.BlockSpec((tm, ODD_IN), row), pl.BlockSpec((8, HEAD_DIM), const), pl.BlockSpec((8, HEAD_DIM), const)],
        out_shape=[_sds((T, ODD_IN), _MXU), _sds((8, HEAD_DIM), F32), _sds((8, HEAD_DIM), F32)],
        compiler_params=_params("arbitrary"),
    )(proj, proj, dqxt, dkx, dvx, cos, sin, q_gain, k_gain)


ONES_ROWS = 16


def _flash_fwd(qx, kx, vx, *, tq, tk):
    v1t = jnp.concatenate([vx.T.reshape(AX_KV_HEADS, HEAD_DIM, vx.shape[0]),
                           jnp.ones((AX_KV_HEADS, ONES_ROWS, vx.shape[0]), vx.dtype)], axis=1)
    T = qx.shape[0]
    tq, tk = min(tq, T), min(tk, T)
    nq, nk = T // tq, T // tk
    group = AX_HEADS // AX_KV_HEADS

    def body(k_ref, v_ref, q_ref, o_ref, lse_ref, acc_sc, m_sc, l_sc):
        j = pl.program_id(2)

        @pl.when(j == 0)
        def _():
            m_sc[...] = jnp.full(m_sc.shape, NEG_INF, F32)
            l_sc[...] = jnp.zeros_like(l_sc)
            acc_sc[...] = jnp.zeros_like(acc_sc)
        k, v = k_ref[...], v_ref[0]

        def step(i, carry):
            cols = pl.ds(pl.multiple_of(i * tq, tq), tq)
            st = _dot(k, q_ref[cols, :], _NT)
            m_old = m_sc[i]
            m_new = jnp.maximum(m_old, jnp.max(st, axis=0, keepdims=True))
            p = jnp.exp2(st - m_new)
            alpha = jnp.exp2(m_old - m_new)
            pv = _dot(v, p)
            m_sc[i] = m_new
            l_sc[i] = alpha * l_sc[i] + pv[HEAD_DIM:HEAD_DIM + 1]
            acc_sc[:, cols] = alpha * acc_sc[:, cols] + pv[:HEAD_DIM]
            return carry

        lax.fori_loop(0, nq, step, 0)

        @pl.when(j == nk - 1)
        def _():
            def finish(i, carry):
                cols = pl.ds(pl.multiple_of(i * tq, tq), tq)
                o_ref[cols, :] = (acc_sc[:, cols] / l_sc[i]).T.astype(o_ref.dtype)
                lse_ref[0, i] = m_sc[i] + jnp.log2(l_sc[i])
                return carry

            lax.fori_loop(0, nq, finish, 0)

    kv = lambda g, h, j: (j, g)
    qh = lambda g, h, j: (0, g * group + h)
    o, lse = pl.pallas_call(
        body, name="flash_fwd", grid=(AX_KV_HEADS, group, nk),
        in_specs=[pl.BlockSpec((tk, HEAD_DIM), kv), pl.BlockSpec((1, HEAD_DIM + ONES_ROWS, tk), lambda g, h, j: (g, 0, j)),
                  pl.BlockSpec((T, HEAD_DIM), qh)],
        out_specs=[pl.BlockSpec((T, HEAD_DIM), qh), pl.BlockSpec((1, nq, 1, tq), lambda g, h, j: (g * group + h, 0, 0, 0))],
        out_shape=[_sds((T, D_MODEL), _MXU), _sds((AX_HEADS, nq, 1, tq), F32)],
        scratch_shapes=[pltpu.VMEM((HEAD_DIM, T), F32), pltpu.VMEM((nq, 1, tq), F32), pltpu.VMEM((nq, 1, tq), F32)],
        compiler_params=_params("parallel", "arbitrary", "arbitrary"),
    )(kx, v1t, qx)
    return o, lse.reshape(AX_HEADS, 1, T)


def _flash_bwd(qx, kx, vx, o, do, lse, *, tq, tk):
    T = qx.shape[0]
    tq, tk = min(tq, T), min(tk, T)
    nq = T // tq
    group = AX_HEADS // AX_KV_HEADS
    lse_rows = lse.reshape(AX_HEADS, nq, 1, tq)
    kxt = kx.T.reshape(AX_KV_HEADS, HEAD_DIM, T)

    def body(k_ref, kt_ref, v_ref, q_ref, o_ref, do_ref, lse_ref, dqt_ref, dk_ref, dv_ref, delta_sc):
        j = pl.program_id(2)

        @pl.when(jnp.logical_and(pl.program_id(1) == 0, j == 0))
        def _():
            dk_ref[...] = jnp.zeros_like(dk_ref)
            dv_ref[...] = jnp.zeros_like(dv_ref)

        @pl.when(j == 0)
        def _():
            dqt_ref[...] = jnp.zeros_like(dqt_ref)

            def row_delta(i, carry):
                rows = pl.ds(pl.multiple_of(i * tq, tq), tq)
                prod = do_ref[rows, :].astype(F32) * o_ref[rows, :].astype(F32)
                delta_sc[i] = jnp.sum(prod.T, axis=0, keepdims=True)
                return carry

            lax.fori_loop(0, nq, row_delta, 0)
        k, v = k_ref[...], v_ref[...]

        def step(i, carry):
            dk, dv = carry
            off = pl.multiple_of(i * tq, tq)
            q, do_blk = q_ref[pl.ds(off, tq), :], do_ref[pl.ds(off, tq), :]
            pt = jnp.exp2(_dot(k, q, _NT) - lse_ref[0, i])
            dst = pt * (_dot(v, do_blk, _NT) - delta_sc[i])
            dqt_ref[:, pl.ds(off, tq)] += _dot(kt_ref[0], dst) * ATT_SCALE
            return dk + _dot(dst, q), dv + _dot(pt, do_blk)

        zero = jnp.zeros((tk, HEAD_DIM), F32)
        dk, dv = lax.fori_loop(0, nq, step, (zero, zero))
        rows = pl.ds(pl.multiple_of(j * tk, tk), tk)
        dk_ref[rows, :] += dk * (ATT_SCALE / SCORE_SCALE_LOG2)
        dv_ref[rows, :] += dv

    kv = lambda g, h, j: (j, g)
    qh = lambda g, h, j: (0, g * group + h)
    st = lambda g, h, j: (g * group + h, 0, 0, 0)
    acc = lambda g, h, j: (0, g)
    return pl.pallas_call(
        body, name="flash_bwd", grid=(AX_KV_HEADS, group, T // tk),
        in_specs=[pl.BlockSpec((tk, HEAD_DIM), kv), pl.BlockSpec((1, HEAD_DIM, tk), lambda g, h, j: (g, 0, j)),
                  pl.BlockSpec((tk, HEAD_DIM), kv),
                  pl.BlockSpec((T, HEAD_DIM), qh), pl.BlockSpec((T, HEAD_DIM), qh), pl.BlockSpec((T, HEAD_DIM), qh),
                  pl.BlockSpec((1, nq, 1, tq), st)],
        out_specs=[pl.BlockSpec((HEAD_DIM, T), lambda g, h, j: (g * group + h, 0)),
                   pl.BlockSpec((T, HEAD_DIM), acc), pl.BlockSpec((T, HEAD_DIM), acc)],
        out_shape=[_sds((D_MODEL, T), F32), _sds((T, AX_KV_HEADS * HEAD_DIM), F32), _sds((T, AX_KV_HEADS * HEAD_DIM), F32)],
        scratch_shapes=[pltpu.VMEM((nq, 1, tq), F32)],
        compiler_params=_params("parallel", "arbitrary", "arbitrary"),
    )(kx, kxt, vx, qx, o, do, lse_rows)


TM = 1024
TM_WIDE = 512


def _mlp_fwd(tag, x, gain, fetch, target=None):
    u, h = _norm_matmul(f"mlp_up{tag}", x, gain, (fetch("w_mlp_up", x), tag), tm=TM_WIDE, tn=D_FF, out_dtype=_MXU)
    out = _matmul_res(f"mlp_down{tag}", [u], (fetch("w_mlp_down", u), tag), x, tm=TM if target is None else TM_WIDE,
                      relu2=True, target=target)
    return out, (x, u, h)


def _local_step(x, target, p, fetch, push, tokens=()):
    T = x.shape[0]
    cos_r, sin_r = _ret_rope_tables(T)
    cos_a, sin_a = _axial_rope_tables(T)
    tabs, rw, log_gamma = _retention_tables(p["ret_decay_logit"][0])
    bias = _swa_bias(p["t5_table"])
    sink = p["swa_sink"][0][:, None] * jnp.ones((1, HEAD_DIM), F32)
    nm, nl = p["norm_mix"], p["norm_mlp"]
    pending = [t for t in tokens if t is not None]

    def send(tag, weight, dw):
        token = push(tag, weight, dw[None])
        if token is not None:
            pending.append(token)

    def tied(operand):
        while pending:
            operand = operand + pending.pop()[0:1, 0:1]
        return operand

    def mlp_bwd(tag, saved, gain, dy, dy16):
        xs, u, h = saved
        w_up, w_down = (fetch("w_mlp_up", None), tag), (fetch("w_mlp_down", None), tag)
        du = _matmul_nt(f"mlp_down{tag}_bwd", dy16, w_down, tm=TM_WIDE, tn=D_FF, out_dtype=_MXU, relu_of=u)
        send(f"mlp_down{tag}", "w_mlp_down", _matmul_tn(f"mlp_down{tag}_dw", u, dy16, tk=2048, tn=1024, tt=1024, out_dtype=_WIRE, relu2=True))
        dx, dx16, dgain = _matmul_nt_normbwd(f"mlp_up{tag}_bwd", du, w_up, xs, tied(gain), dy, tm=TM_WIDE)
        send(f"mlp_up{tag}", "w_mlp_up", _matmul_tn(f"mlp_up{tag}_dw", h, du, tk=1024, tn=2048, tt=1024, out_dtype=_WIRE))
        return dx, dx16, dgain

    w_in_even = fetch("w_in_even", cos_r[:8] + cos_a[:8] + bias[0, :8, :HEAD_DIM] + tabs[0, 0, :8] + rw["cf"][:8, :HEAD_DIM])
    proj0, h0 = _norm_matmul("in_even", x, tied(nm[0:1]), w_in_even, tm=TM_WIDE, tn=EVEN_IN, out_dtype=F32)
    qr, kr, qn, kn = _prep_even(proj0, cos_r, sin_r, p["swa_q_norm"], p["swa_k_norm"], tm=TM)
    sf, sb = _ret_scan("ret_scan_fwd", kr, proj0, 1, rw["cf"], rw["dec_f"], rw["cb"], rw["dec_b"])
    ret_o, ya = _ret_out(qr, kr, proj0, sf, sb, tabs[:, (TAB_D, TAB_A, TAB_B)], p["ret_norm"])
    yb = _swa_fwd(qn, kn, proj0, bias, sink)
    w_out_even = fetch("w_out_even", yb)
    x1 = _matmul_res("out_even", [ya, yb], w_out_even, x, tm=TM)
    x2, mlp0 = _mlp_fwd(0, x1, nl[0:1], fetch)
    w_in_odd, w_out_odd = fetch("w_in_odd", x2), fetch("w_out_odd", x2)
    proj1, h1 = _norm_matmul("in_odd", x2, nm[1:2], w_in_odd, tm=TM, tn=ODD_IN, out_dtype=F32)
    qx, kx, vx = _prep_odd(proj1, cos_a, sin_a, p["ax_q_norm"], p["ax_k_norm"], tm=TM)
    o, lse = _flash_fwd(qx, kx, vx, tq=2048, tk=1024)
    x3 = _matmul_res("out_odd", [o], w_out_odd, x2, tm=TM)
    (g4, g4_16, loss_part), mlp1 = _mlp_fwd(1, x3, nl[1:2], fetch, target=target)

    dx3, dx3_16, dnl1 = mlp_bwd(1, mlp1, nl[1:2], g4, g4_16)
    do = _matmul_nt("out_odd_bwd", dx3_16, w_out_odd, tm=TM, tn=1024, out_dtype=_MXU)
    send("out_odd", "w_out_odd", _matmul_tn("out_odd_dw", o, dx3_16, tk=1024, tn=1024, tt=1024, out_dtype=_WIRE))
    dqxt, dkx, dvx = _flash_bwd(qx, kx, vx, o, do, lse, tq=2048, tk=512)
    dproj1, dqg1, dkg1 = _post_odd(proj1, dqxt, dkx, dvx, cos_a, sin_a, tied(p["ax_q_norm"]), p["ax_k_norm"], tm=TM)
    send("in_odd", "w_in_odd", _matmul_tn("in_odd_dw", h1, dproj1, tk=1024, tn=768, tt=1024, out_dtype=_WIRE))
    dx2, dx2_16, dnm1 = _matmul_nt_normbwd("in_odd_bwd", dproj1, w_in_odd, x2, tied(nm[1:2]), dx3, tm=TM_WIDE)
    dx1, dx1_16, dnl0 = mlp_bwd(0, mlp0, nl[0:1], dx2, dx2_16)
    dycat = _matmul_nt("out_even_bwd", dx1_16, w_out_even, tm=TM, tn=2 * D_MODEL, out_dtype=F32)
    send("out_even", "w_out_even", jnp.concatenate([
        _matmul_tn("out_even_dw_ret", ya, dx1_16, tk=1024, tn=1024, tt=1024, out_dtype=_WIRE),
        _matmul_tn("out_even_dw_swa", yb, dx1_16, tk=1024, tn=1024, tt=1024, out_dtype=_WIRE)], axis=0))
    g_out, dga, dretg = _ret_gate_bwd(dycat, proj0, ret_o, tied(p["ret_norm"]), tm=TM)
    rb, rf = _ret_scan("ret_scan_bwd", qr, g_out, 0, rw["b"], rw["dec_b"], rw["a"], rw["dec_f"])
    dqr, dkr, dva, dlog = _ret_bwd(qr, kr, proj0, g_out, sf, sb, rf, rb, tabs)
    dqn, dkn, dvb, dbias, dsink = _swa_bwd(qn, kn, proj0, dycat, bias, sink)
    dt5 = _t5_bucket_reduce(dbias, _t5_bucket(_swa_rel()).astype(jnp.int32))
    dproj0, dqg0, dkg0 = _post_even(proj0, dqr, dkr, dva, dga, dqn, dkn, dvb, cos_r, sin_r,
                                    p["swa_q_norm"], p["swa_k_norm"], tm=TM_WIDE)
    send("in_even", "w_in_even", _matmul_tn("in_even_dw", h0, dproj0, tk=1024, tn=2304, tt=1024, out_dtype=_WIRE))
    dx0, _, dnm0 = _matmul_nt_normbwd("in_even_bwd", dproj0, w_in_even, x, tied(nm[0:1]), dx1, tm=TM_WIDE)

    fold = lambda part: jnp.sum(part, axis=0)
    dlam = jnp.sum(dlog, axis=1).reshape(RET_HEADS, 2).T
    small = {
        "norm_mix": jnp.stack([fold(dnm0), fold(dnm1)]),
        "norm_mlp": jnp.stack([fold(dnl0), fold(dnl1)]),
        "ret_decay_logit": (dlam * (1.0 - jnp.exp(log_gamma)))[None],
        "ret_norm": fold(dretg)[None],
        "swa_q_norm": fold(dqg0)[None], "swa_k_norm": fold(dkg0)[None],
        "swa_sink": dsink[:, 0][None],
        "t5_table": dt5[:, :T5_BUCKETS].T,
        "ax_q_norm": fold(dqg1)[None], "ax_k_norm": fold(dkg1)[None],
    }
    return loss_part, dx0, small


BIG = ("w_in_even", "w_out_even", "w_in_odd", "w_out_odd", "w_mlp_up", "w_mlp_down")
SMALL = ("norm_mix", "norm_mlp", "ret_decay_logit", "ret_norm", "swa_q_norm", "swa_k_norm", "swa_sink", "t5_table",
         "ax_q_norm", "ax_k_norm")
WEIGHTS = ("norm_mix", "norm_mlp", "w_in_even", "w_out_even", "ret_decay_logit", "ret_norm", "swa_q_norm", "swa_k_norm",
           "swa_sink", "t5_table", "w_in_odd", "w_out_odd", "ax_q_norm", "ax_k_norm", "w_mlp_up", "w_mlp_down")
SHARD_AXIS = {"w_in_even": 2, "w_out_even": 1, "w_in_odd": 2, "w_out_odd": 1, "w_mlp_up": 2, "w_mlp_down": 1}
N_CHIPS = 4
GATHER_ORDER = (("w_in_even",), ("w_out_even",), ("w_mlp_up",), ("w_mlp_down",), ("w_in_odd", "w_out_odd"))
ANY = pl.BlockSpec(memory_space=pl.ANY)
HBM = pl.BlockSpec(memory_space=pltpu.HBM)
SEM = pl.BlockSpec(memory_space=pltpu.SEMAPHORE)
SPLIT_COPY = pltpu.CompilerParams(has_side_effects=pltpu.SideEffectType.DATAFLOW_SIDE_EFFECTING)


def _in_hbm(a):
    return pltpu.with_memory_space_constraint(a, pltpu.HBM)


def _mesh_pos():
    return lax.axis_index("x"), lax.axis_index("y"), lax.axis_index("c")


def _window(ref, axis, start, size):
    idx = [slice(None)] * len(ref.shape)
    idx[axis] = pl.ds(start, size)
    return ref.at[tuple(idx)]


def _cast_place(key, shard, chip, *, tr=256):
    L, R, C = shard.shape
    tr = min(tr, R)
    axis = SHARD_AXIS[key]
    whole = tuple(d * (N_CHIPS if a == axis else 1) for a, d in enumerate(shard.shape))

    def body(chip_ref, s_ref, o_ref):
        o_ref[...] = s_ref[...].astype(o_ref.dtype)

    if axis == 2:
        out_map = lambda l, i, chip_ref: (l, i, chip_ref[0])
    else:
        out_map = lambda l, i, chip_ref: (l, i + chip_ref[0] * (R // tr), 0)
    grid_spec = pltpu.PrefetchScalarGridSpec(
        num_scalar_prefetch=1, grid=(L, R // tr),
        in_specs=[pl.BlockSpec((1, tr, C), lambda l, i, chip_ref: (l, i, 0))],
        out_specs=pl.BlockSpec((1, tr, C), out_map))
    return pl.pallas_call(body, name=f"cast_place_{key}", grid_spec=grid_spec, out_shape=_sds(whole, _MXU),
                          compiler_params=_params("parallel", "parallel"))(chip, shard)


def _gather_copies(names, refs, send_sems, recv_sems, *, outgoing=True, incoming=True):
    x, y, c = _mesh_pos()
    chips = [(1 - x, y), (x, 1 - y), (1 - x, 1 - y)]
    out, inc = [], []
    for t, key in enumerate(names):
        size = refs[t].shape[SHARD_AXIS[key]] // N_CHIPS
        slot = lambda px, py: _window(refs[t], SHARD_AXIS[key], pl.multiple_of((2 * px + py) * size, 128), size)
        for k, (px, py) in enumerate(chips):
            sems = dict(send_sem=send_sems.at[3 * t + k], recv_sem=recv_sems.at[3 * t + k], device_id=(px, py, c), device_id_type=MESH)
            if outgoing:
                out.append(pltpu.make_async_remote_copy(slot(x, y), slot(x, y), **sems))
            if incoming:
                inc.append(pltpu.make_async_remote_copy(slot(x, y), slot(px, py), **sems))
    return out, inc


def _allgather_start(groups):
    names = [list(g) for g in groups]
    flat = [g[k] for g in groups for k in g]
    n, ng = len(flat), len(groups)

    def body(*refs):
        start = 0
        for gi, keys in enumerate(names):
            copies, _ = _gather_copies(keys, refs[start:start + len(keys)], refs[n + 2 * gi], refs[n + 2 * gi + 1], incoming=False)
            for cp in copies:
                cp.start()
            start += len(keys)
        token = refs[-1]
        token[...] = jnp.zeros_like(token)

    sem_shapes = [pltpu.SemaphoreType.DMA((3 * len(keys),)) for keys in names for _ in (0, 1)]
    outs = pl.pallas_call(
        body, name="allgather_start", in_specs=[HBM] * n,
        out_specs=[SEM] * (2 * ng) + [HBM] * n + [pl.BlockSpec(memory_space=pltpu.VMEM)],
        out_shape=sem_shapes + [pltpu.HBM(a.shape, a.dtype) for a in flat] + [_sds((8, HEAD_DIM), F32)],
        input_output_aliases={t: 2 * ng + t for t in range(n)},
        compiler_params=SPLIT_COPY,
    )(*[_in_hbm(a) for a in flat])
    states, start = [], 2 * ng
    for gi, keys in enumerate(names):
        states.append((gi, keys, outs[2 * gi], outs[2 * gi + 1], outs[start:start + len(keys)]))
        start += len(keys)
    return states, outs[-1]


def _allgather_wait(state, after):
    gi, names, send_sems, recv_sems, thru = state
    n = len(names)

    def body(*refs):
        outgoing, incoming = _gather_copies(names, refs[:n], refs[n], refs[n + 1])
        for cp in outgoing:
            cp.wait_send()
        for cp in incoming:
            cp.wait_recv()

    outs = pl.pallas_call(
        body, name=f"allgather_wait_{gi}", in_specs=[HBM] * n + [SEM, SEM, ANY], out_specs=[HBM] * n,
        out_shape=[pltpu.HBM(t.shape, t.dtype) for t in thru],
        input_output_aliases={t: t for t in range(n)},
        compiler_params=SPLIT_COPY,
    )(*thru, send_sems, recv_sems, after)
    return dict(zip(names, outs))


FLIPS = [(a, b, d) for a in (0, 1) for b in (0, 1) for d in (0, 1) if (a, b, d) != (0, 0, 0)]


def _flip(pos, f):
    return tuple(1 - p if fi else p for p, fi in zip(pos, f))


def _piece_shape(weight, shape):
    out = list(shape)
    out[SHARD_AXIS[weight]] //= N_CHIPS
    out[1] //= 2
    return tuple(out)


def _piece(ref, weight, chip, core):
    piece = _piece_shape(weight, ref.shape)
    if SHARD_AXIS[weight] == 1:
        return _window(ref, 1, pl.multiple_of((2 * chip + core) * piece[1], 8), piece[1])
    return _window(_window(ref, 2, pl.multiple_of(chip * piece[2], 128), piece[2]), 1, pl.multiple_of(core * piece[1], 8), piece[1])


def _scatter_copies(weight, grad_ref, land_ref, send_sems, recv_sems, *, outgoing=True, incoming=True):
    pos = _mesh_pos()
    out, inc = [], []
    for k, f in enumerate(FLIPS):
        peer = _flip(pos, f)
        sems = dict(send_sem=send_sems.at[k], recv_sem=recv_sems.at[k], device_id=peer, device_id_type=MESH)
        if outgoing:
            out.append(pltpu.make_async_remote_copy(_piece(grad_ref, weight, 2 * peer[0] + peer[1], peer[2]), land_ref.at[k], **sems))
        if incoming:
            inc.append(pltpu.make_async_remote_copy(_piece(grad_ref, weight, 2 * pos[0] + pos[1], pos[2]), land_ref.at[k], **sems))
    return out, inc


def _scatter_start(tag, weight, grad):
    n_peer = len(FLIPS)
    land = lax.empty((n_peer,) + _piece_shape(weight, grad.shape), grad.dtype)

    def body(grad_ref, land_ref, send_sems, recv_sems, grad_thru, land_thru, token):
        copies, _ = _scatter_copies(weight, grad_ref, land_ref, send_sems, recv_sems, incoming=False)
        for cp in copies:
            cp.start()
        token[...] = jnp.zeros_like(token)

    outs = pl.pallas_call(
        body, name=f"scatter_start_{tag}", in_specs=[HBM, HBM],
        out_specs=[SEM, SEM, HBM, HBM, pl.BlockSpec(memory_space=pltpu.VMEM)],
        out_shape=[pltpu.SemaphoreType.DMA((n_peer,)), pltpu.SemaphoreType.DMA((n_peer,)),
                   pltpu.HBM(grad.shape, grad.dtype), pltpu.HBM(land.shape, land.dtype), _sds((8, HEAD_DIM), F32)],
        input_output_aliases={0: 2, 1: 3},
        compiler_params=SPLIT_COPY,
    )(_in_hbm(grad), _in_hbm(land))
    return (tag, weight, outs[:4]), outs[4]


def _scatter_wait(state, after):
    tag, weight, (send_sems, recv_sems, grad_thru, land_thru) = state

    def body(grad_ref, land_ref, send_ref, recv_ref, after_ref, grad_out, land_out):
        outgoing, incoming = _scatter_copies(weight, grad_ref, land_ref, send_ref, recv_ref)
        for cp in outgoing:
            cp.wait_send()
        for cp in incoming:
            cp.wait_recv()

    return pl.pallas_call(
        body, name=f"scatter_wait_{tag}", in_specs=[HBM, HBM, SEM, SEM, ANY], out_specs=[HBM, HBM],
        out_shape=[pltpu.HBM(grad_thru.shape, grad_thru.dtype), pltpu.HBM(land_thru.shape, land_thru.dtype)],
        input_output_aliases={0: 0, 1: 1},
        compiler_params=SPLIT_COPY,
    )(grad_thru, land_thru, send_sems, recv_sems, after)


def _sum_pieces(tag, weight, grad, land, where, *, tr=256):
    _, R, C = _piece_shape(weight, grad.shape)
    tr = min(tr, R)
    nr = R // tr

    def body(where_ref, g_ref, l_ref, o_ref):
        acc = g_ref[...].astype(F32)
        for s in range(len(FLIPS)):
            acc = acc + l_ref[s].astype(F32)
        o_ref[...] = acc

    if SHARD_AXIS[weight] == 1:
        own = lambda i, where_ref: (0, (2 * where_ref[0] + where_ref[1]) * nr + i, 0)
    else:
        own = lambda i, where_ref: (0, where_ref[1] * nr + i, where_ref[0])
    grid_spec = pltpu.PrefetchScalarGridSpec(
        num_scalar_prefetch=1, grid=(nr,),
        in_specs=[pl.BlockSpec((1, tr, C), own), pl.BlockSpec((len(FLIPS), 1, tr, C), lambda i, where_ref: (0, 0, i, 0))],
        out_specs=pl.BlockSpec((1, tr, C), lambda i, where_ref: (0, where_ref[1] * nr + i, 0)))
    return pl.pallas_call(body, name=f"sum_{tag}", grid_spec=grid_spec, out_shape=_sds((1, 2 * R, C), F32),
                          compiler_params=_params("parallel"))(where, grad, land)


def _exchange_halves(shards):
    names = list(shards)
    n = len(names)
    half_sizes = [shards[k].shape[1] // 2 for k in names]

    def body(*refs):
        outs = refs[n:2 * n]
        send_sems, recv_sems = refs[2 * n:]
        x, y, c = _mesh_pos()
        half = lambda t, core: _window(outs[t], 1, pl.multiple_of(core * half_sizes[t], 8), half_sizes[t])
        sends = []
        for t in range(n):
            sends.append(pltpu.make_async_remote_copy(half(t, c), half(t, c), send_sems.at[t], recv_sems.at[t],
                                                      device_id=(x, y, 1 - c), device_id_type=MESH))
            sends[-1].start()
        for t in range(n):
            pltpu.make_async_remote_copy(half(t, c), half(t, 1 - c), send_sems.at[t], recv_sems.at[t],
                                         device_id=(x, y, 1 - c), device_id_type=MESH).wait_recv()
        for cp in sends:
            cp.wait_send()

    outs = pl.pallas_call(
        body, name="exchange_halves", in_specs=[ANY] * n, out_specs=[ANY] * n,
        out_shape=[_sds(shards[k].shape, F32) for k in names],
        input_output_aliases={t: t for t in range(n)},
        scratch_shapes=[pltpu.SemaphoreType.DMA((n,)), pltpu.SemaphoreType.DMA((n,))],
    )(*[shards[k] for k in names])
    return dict(zip(names, outs))


def _adamw_math(w, g, m, v):
    m = ADAM_B1 * m + (1.0 - ADAM_B1) * g
    v = ADAM_B2 * v + (1.0 - ADAM_B2) * jnp.square(g)
    m_hat = m / (1.0 - ADAM_B1 ** ADAM_STEP)
    v_hat = v / (1.0 - ADAM_B2 ** ADAM_STEP)
    return -ADAM_LR * (m_hat / (jnp.sqrt(v_hat) + ADAM_EPS) + ADAM_WD * w), m, v


def _adamw(name, w, g, m, v, *, tr=512):
    R, C = w.shape
    tr = min(tr, R)

    def body(w_ref, g_ref, m_ref, v_ref, d_ref, mo_ref, vo_ref):
        d_ref[...], mo_ref[...], vo_ref[...] = _adamw_math(w_ref[...], g_ref[...], m_ref[...], v_ref[...])

    spec = pl.BlockSpec((tr, C), lambda i: (i, 0))
    return pl.pallas_call(body, name=name, grid=(R // tr,), in_specs=[spec] * 4, out_specs=[spec] * 3,
                          out_shape=[_sds((R, C), F32)] * 3, compiler_params=_params("parallel"))(w, g, m, v)


SLAB_ROWS = 8
LOSS_ROW = 7


def _pack_small(d):
    pad = lambda a, width: jnp.pad(a.reshape(-1), (0, width - a.size))
    row5 = jnp.concatenate([d["swa_q_norm"].reshape(-1), d["swa_k_norm"].reshape(-1), d["ax_q_norm"].reshape(-1),
                            d["ax_k_norm"].reshape(-1), pad(d["swa_sink"], HEAD_DIM), pad(d["ret_decay_logit"], HEAD_DIM),
                            jnp.zeros((2 * HEAD_DIM,), F32)])
    return jnp.concatenate([d["norm_mix"], d["norm_mlp"], d["ret_norm"], row5[None], pad(d["t5_table"], D_MODEL)[None],
                            jnp.zeros((1, D_MODEL), F32)], axis=0)


def _unpack_small(slab):
    r5 = slab[5]
    return {
        "norm_mix": slab[0:2], "norm_mlp": slab[2:4], "ret_norm": slab[4:5],
        "swa_q_norm": r5[None, 0:128], "swa_k_norm": r5[None, 128:256], "ax_q_norm": r5[None, 256:384],
        "ax_k_norm": r5[None, 384:512], "swa_sink": r5[None, 512:512 + SWA_HEADS],
        "ret_decay_logit": r5[640:640 + 2 * RET_HEADS].reshape(1, 2, RET_HEADS),
        "t5_table": slab[6, :T5_BUCKETS * SWA_HEADS].reshape(T5_BUCKETS, SWA_HEADS),
    }


def _small_allreduce_adamw(g_slab, w_slab, m_slab, v_slab, loss_part):
    def body(g_ref, w_ref, m_ref, v_ref, lp_ref, go_ref, d_ref, mo_ref, vo_ref, gath, send_sems, recv_sems):
        pos = _mesh_pos()
        ident = lambda p: 4 * p[0] + 2 * p[1] + p[2]
        me = ident(pos)
        row = lax.broadcasted_iota(jnp.int32, (SLAB_ROWS, D_MODEL), 0)
        lane = lax.broadcasted_iota(jnp.int32, (SLAB_ROWS, D_MODEL), 1)
        loss = jnp.sum(jnp.sum(lp_ref[...], axis=0, keepdims=True), axis=1, keepdims=True) * (0.5 / D_MODEL)
        gath[me] = jnp.where(jnp.logical_and(row == LOSS_ROW, lane == 0), loss, g_ref[...])
        sends = []
        for k, f in enumerate(FLIPS):
            sends.append(pltpu.make_async_remote_copy(gath.at[me], gath.at[me], send_sems.at[k], recv_sems.at[k],
                                                      device_id=_flip(pos, f), device_id_type=MESH))
            sends[-1].start()
        for k, f in enumerate(FLIPS):
            peer = _flip(pos, f)
            pltpu.make_async_remote_copy(gath.at[me], gath.at[ident(peer)], send_sems.at[k], recv_sems.at[k],
                                         device_id=peer, device_id_type=MESH).wait_recv()
        for cp in sends:
            cp.wait_send()
        total = gath[0]
        for s in range(1, N_DEV):
            total = total + gath[s]
        go_ref[...] = total
        d_ref[...], mo_ref[...], vo_ref[...] = _adamw_math(w_ref[...], total, m_ref[...], v_ref[...])

    vmem = pl.BlockSpec(memory_space=pltpu.VMEM)
    return pl.pallas_call(
        body, name="small_allreduce_adamw", in_specs=[vmem] * 5, out_specs=[vmem] * 4,
        out_shape=[_sds((SLAB_ROWS, D_MODEL), F32)] * 4,
        scratch_shapes=[pltpu.VMEM((N_DEV, SLAB_ROWS, D_MODEL), F32),
                        pltpu.SemaphoreType.DMA((len(FLIPS),)), pltpu.SemaphoreType.DMA((len(FLIPS),))],
    )(g_slab, w_slab, m_slab, v_slab, loss_part)


def kernel(x, norm_mix, norm_mlp, w_in_even, w_out_even, ret_decay_logit, ret_norm, swa_q_norm, swa_k_norm, swa_sink, t5_table, w_in_odd, w_out_odd, ax_q_norm, ax_k_norm, w_mlp_up, w_mlp_down, loss_target, m_norm_mix, m_norm_mlp, m_w_in_even, m_w_out_even, m_ret_decay_logit, m_ret_norm, m_swa_q_norm, m_swa_k_norm, m_swa_sink, m_t5_table, m_w_in_odd, m_w_out_odd, m_ax_q_norm, m_ax_k_norm, m_w_mlp_up, m_w_mlp_down, v_norm_mix, v_norm_mlp, v_w_in_even, v_w_out_even, v_ret_decay_logit, v_ret_norm, v_swa_q_norm, v_swa_k_norm, v_swa_sink, v_t5_table, v_w_in_odd, v_w_out_odd, v_ax_q_norm, v_ax_k_norm, v_w_mlp_up, v_w_mlp_down):
    w = dict(zip(WEIGHTS, (norm_mix, norm_mlp, w_in_even, w_out_even, ret_decay_logit, ret_norm, swa_q_norm, swa_k_norm,
                           swa_sink, t5_table, w_in_odd, w_out_odd, ax_q_norm, ax_k_norm, w_mlp_up, w_mlp_down)))
    m = dict(zip(WEIGHTS, (m_norm_mix, m_norm_mlp, m_w_in_even, m_w_out_even, m_ret_decay_logit, m_ret_norm, m_swa_q_norm,
                           m_swa_k_norm, m_swa_sink, m_t5_table, m_w_in_odd, m_w_out_odd, m_ax_q_norm, m_ax_k_norm,
                           m_w_mlp_up, m_w_mlp_down)))
    v = dict(zip(WEIGHTS, (v_norm_mix, v_norm_mlp, v_w_in_even, v_w_out_even, v_ret_decay_logit, v_ret_norm, v_swa_q_norm,
                           v_swa_k_norm, v_swa_sink, v_t5_table, v_w_in_odd, v_w_out_odd, v_ax_q_norm, v_ax_k_norm,
                           v_w_mlp_up, v_w_mlp_down)))
    flat = lambda a: a.reshape(-1, a.shape[-1])

    chip = (2 * lax.axis_index("x") + lax.axis_index("y")).astype(jnp.int32)
    where = jnp.stack([chip, lax.axis_index("c").astype(jnp.int32)])

    placed = {k: _cast_place(k, w[k], where[0:1]) for k in BIG}
    gather, gather_token = _allgather_start([{k: placed[k] for k in group} for group in GATHER_ORDER])
    gathered = {}

    def fetch(name, after):
        if name not in gathered:
            state = gather[[name in group for group in GATHER_ORDER].index(True)]
            gathered.update(_allgather_wait(state, gather_token if after is None else after))
        return gathered[name] if name.startswith("w_mlp") else gathered[name][0]

    in_flight = []

    def push(tag, weight, dw):
        state, token = _scatter_start(tag, weight, dw)
        in_flight.append(state)
        return token

    loss_part, dx, small_g = _local_step(x[0], loss_target[0], {k: w[k] for k in SMALL}, fetch, push)

    halves = {}
    for state in in_flight:
        tag, weight = state[0], state[1]
        dw, land = _scatter_wait(state, dx)
        halves[tag] = _sum_pieces(tag, weight, dw, land, where)
    reduced = _exchange_halves(halves)
    grad = {"w_in_even": reduced["in_even"], "w_out_even": reduced["out_even"],
            "w_in_odd": reduced["in_odd"], "w_out_odd": reduced["out_odd"],
            "w_mlp_up": jnp.concatenate([reduced["mlp_up0"], reduced["mlp_up1"]], axis=0),
            "w_mlp_down": jnp.concatenate([reduced["mlp_down0"], reduced["mlp_down1"]], axis=0)}
    delta, new_m, new_v = {}, {}, {}
    for k in BIG:
        d_k, m_k, v_k = _adamw(f"adamw_{k}", flat(w[k]), flat(grad[k]), flat(m[k]), flat(v[k]))
        delta[k], new_m[k], new_v[k] = d_k.reshape(w[k].shape), m_k.reshape(w[k].shape), v_k.reshape(w[k].shape)

    slabs = _small_allreduce_adamw(_pack_small(small_g), _pack_small({k: w[k] for k in SMALL}),
                                   _pack_small({k: m[k] for k in SMALL}), _pack_small({k: v[k] for k in SMALL}), loss_part)
    loss = slabs[0][LOSS_ROW, 0]
    for out, slab in zip((grad, delta, new_m, new_v), slabs):
        out.update(_unpack_small(slab))

    return (loss, dx[None], *[grad[k] for k in WEIGHTS], *[delta[k] for k in WEIGHTS],
            *[new_m[k] for k in WEIGHTS], *[new_v[k] for k in WEIGHTS])
```

```python
import functools
import math

import jax
import jax.numpy as jnp
from jax import lax
from jax.experimental import pallas as pl
from jax.experimental.pallas import tpu as pltpu

F32 = jnp.float32
BF16 = jnp.bfloat16
_MXU = BF16
_WIRE = BF16

D_MODEL = 1024
HEAD_DIM = 128
EPS = 1e-6
NEG_INF = -1e30
CHUNK = 128
RET_CHUNKS_PER_STEP = 8
GRID_W = 64
RET_HEADS, RET_DK, RET_DV = 4, 128, 256
RET_Q, RET_V = RET_HEADS * RET_DK, RET_HEADS * RET_DV
RET_THETA = 10000.0
SWA_HEADS, SWA_KV_HEADS = 8, 2
T5_BUCKETS, T5_MAX_DIST = 32, 128
AX_HEADS, AX_KV_HEADS = 8, 2
AX_THETA = 10000.0
D_FF = 4 * D_MODEL
EVEN_IN = 2 * RET_Q + 2 * RET_V + D_MODEL + 2 * SWA_KV_HEADS * HEAD_DIM
ODD_IN = D_MODEL + 2 * AX_KV_HEADS * HEAD_DIM
ATT_SCALE = HEAD_DIM ** -0.5
SCORE_SCALE_LOG2 = ATT_SCALE * math.log2(math.e)

ADAM_LR, ADAM_B1, ADAM_B2, ADAM_EPS, ADAM_WD, ADAM_STEP = 0.001, 0.9, 0.999, 1e-08, 0.01, 10

N_DEV = 8
VMEM_LIMIT_BYTES = 56 << 20
MESH = pl.DeviceIdType.MESH

_NN = (((1,), (0,)), ((), ()))
_NT = (((1,), (1,)), ((), ()))
_TN = (((0,), (0,)), ((), ()))


def _dot(a, b, dn=_NN):
    return lax.dot_general(a.astype(_MXU), b.astype(_MXU), dn, preferred_element_type=F32)


def _params(*sem):
    return pltpu.CompilerParams(dimension_semantics=sem, vmem_limit_bytes=VMEM_LIMIT_BYTES)


def _sds(shape, dtype):
    return jax.ShapeDtypeStruct(tuple(shape), dtype)


def _rowsum8(x):
    return jnp.sum(x.reshape(x.shape[0] // 8, 8, x.shape[1]), axis=0)


def _swap_halves(x, half):
    width = x.shape[1]
    lane = lax.broadcasted_iota(jnp.int32, x.shape, 1)
    up = pltpu.roll(x, width - half, axis=1)
    down = pltpu.roll(x, half, axis=1)
    return jnp.where((lane & (2 * half - 1)) < half, up, down)


def _sigmoid(x):
    return 1.0 / (1.0 + jnp.exp(-x))


def _weight_spec(w, block, index_map):
    if isinstance(w, tuple):
        stacked, layer = w
        return stacked, pl.BlockSpec((None,) + block, lambda *idx: (layer,) + index_map(*idx))
    return w, pl.BlockSpec(block, index_map)


def _weight_dims(w):
    return (w[0] if isinstance(w, tuple) else w).shape[-2:]


def _norm_matmul(name, x, gain, w, *, tm, tn, out_dtype):
    T, K = x.shape
    N = _weight_dims(w)[1]
    tm, tn = min(tm, T), min(tn, N)
    w, w_spec = _weight_spec(w, (K, tn), lambda i, j: (0, j))

    def body(x_ref, g_ref, w_ref, y_ref, h_ref, h_sc):
        @pl.when(pl.program_id(1) == 0)
        def _():
            xv = x_ref[...]
            r = lax.rsqrt(jnp.mean(xv * xv, axis=-1, keepdims=True) + EPS)
            h = (xv * r * g_ref[...]).astype(_MXU)
            h_sc[...] = h
            h_ref[...] = h
        y_ref[...] = jnp.dot(h_sc[...], w_ref[...], preferred_element_type=F32).astype(y_ref.dtype)

    return pl.pallas_call(
        body, name=name, grid=(T // tm, N // tn),
        in_specs=[pl.BlockSpec((tm, K), lambda i, j: (i, 0)),
                  pl.BlockSpec((1, K), lambda i, j: (0, 0)),
                  w_spec],
        out_specs=[pl.BlockSpec((tm, tn), lambda i, j: (i, j)),
                   pl.BlockSpec((tm, K), lambda i, j: (i, 0))],
        out_shape=[_sds((T, N), out_dtype), _sds((T, K), _MXU)],
        scratch_shapes=[pltpu.VMEM((tm, K), _MXU)],
        compiler_params=_params("parallel", "arbitrary"),
    )(x, gain, w)


def _matmul_res(name, a_list, w, res, *, tm, relu2=False, target=None):
    T = res.shape[0]
    N = _weight_dims(w)[1]
    K = a_list[0].shape[1]
    n_a = len(a_list)
    tm = min(tm, T)
    with_loss = target is not None
    w_specs = [_weight_spec(w, (K, N), functools.partial(lambda i, b: (b, 0), b=b)) for b in range(n_a)]

    def body(*refs):
        a_refs = refs[:n_a]
        w_refs = refs[n_a:2 * n_a]
        res_ref = refs[2 * n_a]
        acc = res_ref[...]
        for a_ref, w_ref in zip(a_refs, w_refs):
            a = a_ref[...]
            if relu2:
                a = jnp.square(jnp.maximum(a.astype(F32), 0.0))
            acc = acc + _dot(a, w_ref[...])
        if with_loss:
            tgt_ref, g_ref, g16_ref, loss_ref = refs[2 * n_a + 1:]
            diff = acc - tgt_ref[...]
            g = diff * (1.0 / N)
            g_ref[...] = g
            g16_ref[...] = g.astype(g16_ref.dtype)

            @pl.when(pl.program_id(0) == 0)
            def _():
                loss_ref[...] = jnp.zeros_like(loss_ref)
            loss_ref[...] += _rowsum8(diff * diff)
        else:
            refs[2 * n_a + 1][...] = acc

    row = lambda i: (i, 0)
    in_specs = [pl.BlockSpec((tm, K), row) for _ in a_list]
    in_specs += [spec for _, spec in w_specs]
    in_specs += [pl.BlockSpec((tm, N), row)]
    args = list(a_list) + [arr for arr, _ in w_specs] + [res]
    if with_loss:
        in_specs.append(pl.BlockSpec((tm, N), row))
        args.append(target)
        out_specs = [pl.BlockSpec((tm, N), row), pl.BlockSpec((tm, N), row), pl.BlockSpec((8, N), lambda i: (0, 0))]
        out_shape = [_sds((T, N), F32), _sds((T, N), _MXU), _sds((8, N), F32)]
        sem = "arbitrary"
    else:
        out_specs = pl.BlockSpec((tm, N), row)
        out_shape = _sds((T, N), F32)
        sem = "parallel"
    return pl.pallas_call(body, name=name, grid=(T // tm,), in_specs=in_specs, out_specs=out_specs,
                          out_shape=out_shape, compiler_params=_params(sem))(*args)


def _matmul_nt(name, a, w, *, tm, tn, out_dtype, relu_of=None):
    T, K = a.shape
    N = _weight_dims(w)[0]
    tm, tn = min(tm, T), min(tn, N)
    w, w_spec = _weight_spec(w, (tn, K), lambda i, j: (j, 0))

    def body(*refs):
        if relu_of is None:
            a_ref, w_ref, o_ref = refs
            o_ref[...] = _dot(a_ref[...], w_ref[...], _NT).astype(o_ref.dtype)
        else:
            a_ref, w_ref, u_ref, o_ref = refs
            da = _dot(a_ref[...], w_ref[...], _NT)
            o_ref[...] = (da * (2.0 * jnp.maximum(u_ref[...].astype(F32), 0.0))).astype(o_ref.dtype)

    in_specs = [pl.BlockSpec((tm, K), lambda i, j: (i, 0)), w_spec]
    args = [a, w]
    if relu_of is not None:
        in_specs.append(pl.BlockSpec((tm, tn), lambda i, j: (i, j)))
        args.append(relu_of)
    return pl.pallas_call(body, name=name, grid=(T // tm, N // tn), in_specs=in_specs,
                          out_specs=pl.BlockSpec((tm, tn), lambda i, j: (i, j)),
                          out_shape=_sds((T, N), out_dtype),
                          compiler_params=_params("parallel", "parallel"))(*args)


def _matmul_nt_normbwd(name, dy, w, x, gain, dres, *, tm):
    T, K = dy.shape
    N = _weight_dims(w)[0]
    tm = min(tm, T)
    w, w_spec = _weight_spec(w, (N, K), lambda i: (0, 0))

    def body(dy_ref, w_ref, x_ref, g_ref, dres_ref, dx_ref, dx16_ref, dg_ref):
        dh = _dot(dy_ref[...], w_ref[...], _NT)
        xv = x_ref[...]
        r = lax.rsqrt(jnp.mean(xv * xv, axis=-1, keepdims=True) + EPS)
        xhat = xv * r
        dxhat = dh * g_ref[...]
        dx = dres_ref[...] + r * (dxhat - xhat * jnp.mean(dxhat * xhat, axis=-1, keepdims=True))
        dx_ref[...] = dx
        dx16_ref[...] = dx.astype(dx16_ref.dtype)

        @pl.when(pl.program_id(0) == 0)
        def _():
            dg_ref[...] = jnp.zeros_like(dg_ref)
        dg_ref[...] += _rowsum8(dh * xhat)

    row = lambda i: (i, 0)
    return pl.pallas_call(
        body, name=name, grid=(T // tm,),
        in_specs=[pl.BlockSpec((tm, K), row), w_spec,
                  pl.BlockSpec((tm, N), row), pl.BlockSpec((1, N), lambda i: (0, 0)), pl.BlockSpec((tm, N), row)],
        out_specs=[pl.BlockSpec((tm, N), row), pl.BlockSpec((tm, N), row), pl.BlockSpec((8, N), lambda i: (0, 0))],
        out_shape=[_sds((T, N), F32), _sds((T, N), _MXU), _sds((8, N), F32)],
        compiler_params=_params("arbitrary"),
    )(dy, w, x, gain, dres)


def _matmul_tn(name, a, b, *, tk, tn, tt, out_dtype, relu2=False):
    T, Ka = a.shape
    Nb = b.shape[1]
    tk, tn, tt = min(tk, Ka), min(tn, Nb), min(tt, T)
    nt = T // tt

    def body(a_ref, b_ref, o_ref, acc):
        t = pl.program_id(2)

        @pl.when(t == 0)
        def _():
            acc[...] = jnp.zeros_like(acc)
        av = a_ref[...]
        if relu2:
            av = jnp.square(jnp.maximum(av.astype(F32), 0.0))
        acc[...] += _dot(av, b_ref[...], _TN)

        @pl.when(t == nt - 1)
        def _():
            o_ref[...] = acc[...].astype(o_ref.dtype)

    return pl.pallas_call(
        body, name=name, grid=(Ka // tk, Nb // tn, nt),
        in_specs=[pl.BlockSpec((tt, tk), lambda i, j, t: (t, i)), pl.BlockSpec((tt, tn), lambda i, j, t: (t, j))],
        out_specs=pl.BlockSpec((tk, tn), lambda i, j, t: (i, j)),
        out_shape=_sds((Ka, Nb), out_dtype),
        scratch_shapes=[pltpu.VMEM((tk, tn), F32)],
        compiler_params=_params("parallel", "parallel", "arbitrary"),
    )(a, b)


def _rope_angles(pos, dim, theta):
    inv = theta ** (-jnp.arange(0, dim, 2, dtype=F32) / dim)
    return pos.astype(F32)[:, None] * inv[None, :]


def _ret_rope_tables(T):
    ang = _rope_angles(jnp.arange(T), RET_DK, RET_THETA)
    c, s = jnp.cos(ang), jnp.sin(ang)
    return jnp.concatenate([c, c], axis=1), jnp.concatenate([-s, s], axis=1)


def _axial_rope_tables(T):
    rows = T // GRID_W
    ar = _rope_angles(jnp.arange(rows), HEAD_DIM // 2, AX_THETA)
    ac = _rope_angles(jnp.arange(GRID_W), HEAD_DIM // 2, AX_THETA)
    by_row = lambda a: jnp.repeat(a, GRID_W, axis=0)
    by_col = lambda a: jnp.tile(a, (rows, 1))
    cos = jnp.concatenate([by_row(jnp.cos(ar)), by_row(jnp.cos(ar)), by_col(jnp.cos(ac)), by_col(jnp.cos(ac))], axis=1)
    sin = jnp.concatenate([by_row(-jnp.sin(ar)), by_row(jnp.sin(ar)), by_col(-jnp.sin(ac)), by_col(jnp.sin(ac))], axis=1)
    return cos, sin


(TAB_D, TAB_DT, TAB_EF, TAB_EB, TAB_A, TAB_B, TAB_CF, TAB_CB,
 TAB_RA, TAB_RB, TAB_RCF, TAB_RCB, TAB_KF, TAB_KB) = range(14)


def _retention_tables(decay_logit):
    lg = jax.nn.log_sigmoid(decay_logit.astype(F32))
    lam, mu = lg[0][:, None, None], lg[1][:, None, None]
    idx = jnp.arange(CHUNK, dtype=F32)
    diff = (idx[:, None] - idx[None, :])[None]
    df = jnp.where(diff >= 0, jnp.exp(jnp.maximum(diff, 0.0) * lam), 0.0)
    db = jnp.where(diff < 0, jnp.exp(jnp.maximum(-diff, 0.0) * mu), 0.0)
    d = df + db
    r = idx[None, :, None]
    ones = jnp.ones((1, 1, CHUNK), F32)
    a = jnp.exp((r + 1.0) * lam) * ones
    b = jnp.exp((CHUNK - r) * mu) * ones
    cf = jnp.exp((CHUNK - 1.0 - r) * lam) * ones
    cb = jnp.exp(r * mu) * ones
    full = jnp.ones((1, CHUNK, CHUNK), F32)
    kf = CHUNK * jnp.exp(CHUNK * lam) * full
    kb = CHUNK * jnp.exp(CHUNK * mu) * full
    tabs = jnp.stack([d, jnp.swapaxes(d, 1, 2), diff * df, -diff * db, a, b, cf, cb,
                      (r + 1.0) * a, (CHUNK - r) * b, (CHUNK - 1.0 - r) * cf, r * cb, kf, kb], axis=1)

    def lanes(tab):
        return jnp.transpose(tab, (1, 0, 2)).reshape(CHUNK, RET_HEADS * CHUNK)

    def dec(l):
        return jnp.exp(CHUNK * l)[:, 0, :] * jnp.ones((1, RET_DV), F32)

    weights = dict(a=lanes(a), b=lanes(b), cf=lanes(cf), cb=lanes(cb), dec_f=dec(lam), dec_b=dec(mu))
    return tabs, weights, lg


def _t5_bucket(rel):
    nb = T5_BUCKETS // 2
    max_exact = nb // 2
    ret = jnp.where(rel > 0, nb, 0)
    n = jnp.abs(rel)
    nf = jnp.maximum(n, 1).astype(F32)
    large = max_exact + (jnp.log(nf / max_exact) / math.log(T5_MAX_DIST / max_exact)
                         * (nb - max_exact)).astype(jnp.int32)
    large = jnp.minimum(large, nb - 1)
    return ret + jnp.where(n < max_exact, n, large)


def _swa_rel():
    r = jnp.arange(CHUNK)
    j = jnp.arange(3 * CHUNK)
    return j[None, :] - CHUNK - r[:, None]


def _swa_bias(t5_table):
    rel = _swa_rel()
    bucket = jnp.where(jnp.abs(rel) <= CHUNK, _t5_bucket(rel), -1).astype(jnp.int32)

    def body(tab_ref, bk_ref, o_ref):
        bk = bk_ref[...]
        for h in range(SWA_HEADS):
            pick = lambda b, acc, h=h: jnp.where(bk == b, tab_ref[b, h], acc)
            o_ref[h] = lax.fori_loop(0, T5_BUCKETS, pick, jnp.full(bk.shape, NEG_INF, F32))

    return pl.pallas_call(
        body, name="t5_bias",
        in_specs=[pl.BlockSpec(memory_space=pltpu.SMEM), pl.BlockSpec(memory_space=pltpu.VMEM)],
        out_specs=pl.BlockSpec(memory_space=pltpu.VMEM),
        out_shape=_sds((SWA_HEADS, CHUNK, 3 * CHUNK), F32),
    )(t5_table.astype(F32), bucket)


def _prep_even(proj, cos, sin, q_gain, k_gain, *, tm):
    T = proj.shape[0]
    tm = min(tm, T)

    def body(qa_ref, ka_ref, qb_ref, kb_ref, c_ref, s_ref, qg_ref, kg_ref, qr_ref, kr_ref, qn_ref, kn_ref):
        c = jnp.concatenate([c_ref[...]] * RET_HEADS, axis=1)
        s = jnp.concatenate([s_ref[...]] * RET_HEADS, axis=1)
        qa = qa_ref[...]
        qr_ref[...] = (qa * c + _swap_halves(qa, RET_DK // 2) * s).astype(qr_ref.dtype)
        ka = ka_ref[...]
        kr_ref[...] = ((ka * c + _swap_halves(ka, RET_DK // 2) * s) * (RET_DK ** -0.5)).astype(kr_ref.dtype)
        for src, gain, dst, heads in ((qb_ref, qg_ref, qn_ref, SWA_HEADS), (kb_ref, kg_ref, kn_ref, SWA_KV_HEADS)):
            for h in range(heads):
                sl = slice(h * HEAD_DIM, (h + 1) * HEAD_DIM)
                xh = src[:, sl]
                r = lax.rsqrt(jnp.mean(xh * xh, axis=-1, keepdims=True) + EPS)
                dst[:, sl] = (xh * r * gain[...]).astype(dst.dtype)

    row = lambda i: (i, 0)
    const = lambda i: (0, 0)
    return pl.pallas_call(
        body, name="prep_even", grid=(T // tm,),
        in_specs=[pl.BlockSpec((tm, RET_Q), lambda i: (i, 0)), pl.BlockSpec((tm, RET_Q), lambda i: (i, 1)),
                  pl.BlockSpec((tm, D_MODEL), lambda i: (i, 3)), pl.BlockSpec((tm, 256), lambda i: (i, 16)),
                  pl.BlockSpec((tm, RET_DK), row), pl.BlockSpec((tm, RET_DK), row),
                  pl.BlockSpec((1, HEAD_DIM), const), pl.BlockSpec((1, HEAD_DIM), const)],
        out_specs=[pl.BlockSpec((tm, RET_Q), row), pl.BlockSpec((tm, RET_Q), row),
                   pl.BlockSpec((tm, D_MODEL), row), pl.BlockSpec((tm, 256), row)],
        out_shape=[_sds((T, RET_Q), _MXU), _sds((T, RET_Q), _MXU), _sds((T, D_MODEL), _MXU), _sds((T, 256), _MXU)],
        compiler_params=_params("parallel"),
    )(proj, proj, proj, proj, cos, sin, q_gain, k_gain)


def _ret_scan(name, x, y, y_col, w_asc, dec_asc, w_desc, dec_desc):
    T = x.shape[0]
    nc = T // CHUNK
    per = min(RET_CHUNKS_PER_STEP, nc)
    nb = nc // per
    rows_per = per * CHUNK

    def body(xa_ref, ya_ref, xd_ref, yd_ref, wa_ref, da_ref, wd_ref, dd_ref, sa_out, sd_out, sa, sd):
        @pl.when(pl.program_id(0) == 0)
        def _():
            sa[...] = jnp.zeros_like(sa)
            sd[...] = jnp.zeros_like(sd)
        for step in range(per):
            for c, x_ref, y_ref, w_ref, d_ref, st, out in ((step, xa_ref, ya_ref, wa_ref, da_ref, sa, sa_out),
                                                       (per - 1 - step, xd_ref, yd_ref, wd_ref, dd_ref, sd, sd_out)):
                rows = slice(c * CHUNK, (c + 1) * CHUNK)
                out[c] = st[...].astype(out.dtype)
                for h in range(RET_HEADS):
                    ks = slice(h * RET_DK, (h + 1) * RET_DK)
                    vs = slice(h * RET_DV, (h + 1) * RET_DV)
                    u = _dot(x_ref[rows, ks].astype(F32) * w_ref[:, ks], y_ref[rows, vs], _TN)
                    st[ks, :] = st[ks, :] * d_ref[h:h + 1, :] + u

    asc = lambda i: (i, 0)
    desc = lambda i: (nb - 1 - i, 0)
    const = lambda i: (0, 0)
    return pl.pallas_call(
        body, name=name, grid=(nb,),
        in_specs=[pl.BlockSpec((rows_per, RET_Q), asc), pl.BlockSpec((rows_per, RET_V), lambda i: (i, y_col)),
                  pl.BlockSpec((rows_per, RET_Q), desc), pl.BlockSpec((rows_per, RET_V), lambda i: (nb - 1 - i, y_col)),
                  pl.BlockSpec((CHUNK, RET_Q), const), pl.BlockSpec((RET_HEADS, RET_DV), const),
                  pl.BlockSpec((CHUNK, RET_Q), const), pl.BlockSpec((RET_HEADS, RET_DV), const)],
        out_specs=[pl.BlockSpec((per, RET_Q, RET_DV), lambda i: (i, 0, 0)),
                   pl.BlockSpec((per, RET_Q, RET_DV), lambda i: (nb - 1 - i, 0, 0))],
        out_shape=[_sds((nc, RET_Q, RET_DV), _MXU), _sds((nc, RET_Q, RET_DV), _MXU)],
        scratch_shapes=[pltpu.VMEM((RET_Q, RET_DV), F32), pltpu.VMEM((RET_Q, RET_DV), F32)],
        compiler_params=_params("arbitrary"),
    )(x, y, x, y, w_asc, dec_asc, w_desc, dec_desc)


def _ret_out(qr, kr, proj, sf, sb, tabs, gain):
    T = qr.shape[0]
    nc = T // CHUNK
    per = min(RET_CHUNKS_PER_STEP, nc)
    rows_per = per * CHUNK

    def body(q_ref, k_ref, v_ref, g_ref, sf_ref, sb_ref, tab_ref, gain_ref, o_ref, y_ref):
        for c in range(per):
            rows = slice(c * CHUNK, (c + 1) * CHUNK)
            for h in range(RET_HEADS):
                ks = slice(h * RET_DK, (h + 1) * RET_DK)
                vs = slice(h * RET_DV, (h + 1) * RET_DV)
                q, k, v = q_ref[rows, ks], k_ref[rows, ks], v_ref[rows, vs]
                qf = q.astype(F32)
                a_mat = _dot(q, k, _NT) * tab_ref[h, 0]
                o = (_dot(a_mat, v) + _dot(qf * tab_ref[h, 1], sf_ref[c, ks, :]) + _dot(qf * tab_ref[h, 2], sb_ref[c, ks, :]))
                o_ref[rows, vs] = o
                r = lax.rsqrt(jnp.mean(o * o, axis=-1, keepdims=True) + EPS)
                g = g_ref[rows, vs]
                y_ref[rows, vs] = (g * _sigmoid(g) * (o * r * gain_ref[:, vs])).astype(y_ref.dtype)

    row = lambda i: (i, 0)
    st = lambda i: (i, 0, 0)
    return pl.pallas_call(
        body, name="ret_out", grid=(nc // per,),
        in_specs=[pl.BlockSpec((rows_per, RET_Q), row), pl.BlockSpec((rows_per, RET_Q), row),
                  pl.BlockSpec((rows_per, RET_V), lambda i: (i, 1)), pl.BlockSpec((rows_per, RET_V), lambda i: (i, 2)),
                  pl.BlockSpec((per, RET_Q, RET_DV), st), pl.BlockSpec((per, RET_Q, RET_DV), st),
                  pl.BlockSpec((RET_HEADS, 3, CHUNK, CHUNK), lambda i: (0, 0, 0, 0)),
                  pl.BlockSpec((1, RET_V), lambda i: (0, 0))],
        out_specs=[pl.BlockSpec((rows_per, RET_V), row), pl.BlockSpec((rows_per, RET_V), row)],
        out_shape=[_sds((T, RET_V), F32), _sds((T, RET_V), _MXU)],
        compiler_params=_params("parallel"),
    )(qr, kr, proj, proj, sf, sb, tabs, gain)


def _ret_gate_bwd(dycat, proj, ret_o, gain, *, tm):
    T = ret_o.shape[0]
    tm = min(tm, T)

    def body(dy_ref, g_ref, o_ref, gain_ref, do_ref, dg_ref, dgain_ref):
        @pl.when(pl.program_id(0) == 0)
        def _():
            dgain_ref[...] = jnp.zeros_like(dgain_ref)
        for h in range(RET_HEADS):
            vs = slice(h * RET_DV, (h + 1) * RET_DV)
            o, g, dya, gn = o_ref[:, vs], g_ref[:, vs], dy_ref[:, vs], gain_ref[:, vs]
            r = lax.rsqrt(jnp.mean(o * o, axis=-1, keepdims=True) + EPS)
            ohat = o * r
            sg = _sigmoid(g)
            dy = dya * (g * sg)
            dg_ref[:, vs] = (dya * (ohat * gn) * (sg * (1.0 + g * (1.0 - sg)))).astype(dg_ref.dtype)
            dyg = dy * gn
            do_ref[:, vs] = (r * (dyg - ohat * jnp.mean(dyg * ohat, axis=-1, keepdims=True))).astype(do_ref.dtype)
            dgain_ref[:, vs] += _rowsum8(dy * ohat)

    row = lambda i: (i, 0)
    return pl.pallas_call(
        body, name="ret_gate_bwd", grid=(T // tm,),
        in_specs=[pl.BlockSpec((tm, RET_V), row), pl.BlockSpec((tm, RET_V), lambda i: (i, 2)),
                  pl.BlockSpec((tm, RET_V), row), pl.BlockSpec((1, RET_V), lambda i: (0, 0))],
        out_specs=[pl.BlockSpec((tm, RET_V), row), pl.BlockSpec((tm, RET_V), row), pl.BlockSpec((8, RET_V), lambda i: (0, 0))],
        out_shape=[_sds((T, RET_V), _MXU), _sds((T, RET_V), _MXU), _sds((8, RET_V), F32)],
        compiler_params=_params("arbitrary"),
    )(dycat, proj, ret_o, gain)


def _ret_bwd(qr, kr, proj, g_out, sf, sb, rf, rb, tabs):
    T = qr.shape[0]
    nc = T // CHUNK
    per = min(RET_CHUNKS_PER_STEP, nc)
    rows_per = per * CHUNK

    def body(q_ref, k_ref, v_ref, g_ref, sf_ref, sb_ref, rf_ref, rb_ref, tab_ref, dq_ref, dk_ref, dv_ref, dl_ref):
        @pl.when(pl.program_id(0) == 0)
        def _():
            dl_ref[...] = jnp.zeros_like(dl_ref)
        for c in range(per):
            rows = slice(c * CHUNK, (c + 1) * CHUNK)
            for h in range(RET_HEADS):
                ks = slice(h * RET_DK, (h + 1) * RET_DK)
                vs = slice(h * RET_DV, (h + 1) * RET_DV)
                q, k, v, g = q_ref[rows, ks], k_ref[rows, ks], v_ref[rows, vs], g_ref[rows, vs]
                s_f, s_b, r_f, r_b = sf_ref[c, ks, :], sb_ref[c, ks, :], rf_ref[c, ks, :], rb_ref[c, ks, :]
                tab = lambda t, h=h: tab_ref[h, t]
                qf, kf = q.astype(F32), k.astype(F32)
                qk = _dot(q, k, _NT)
                da_raw = _dot(g, v, _NT)
                x_f, x_b = _dot(g, s_f, _NT), _dot(g, s_b, _NT)
                dq_ref[rows, ks] = _dot(da_raw * tab(TAB_D), k) + tab(TAB_A) * x_f + tab(TAB_B) * x_b
                at = _dot(k, q, _NT) * tab(TAB_DT)
                dat = _dot(v, g, _NT) * tab(TAB_DT)
                y_f, y_b = _dot(v, r_f, _NT), _dot(v, r_b, _NT)
                dk_ref[rows, ks] = _dot(dat, q) + tab(TAB_CF) * y_f + tab(TAB_CB) * y_b
                dv_ref[rows, vs] = (_dot(at, g) + _dot(kf * tab(TAB_CF), r_f) + _dot(kf * tab(TAB_CB), r_b)).astype(dv_ref.dtype)
                inner = da_raw * qk
                rs_f = r_f.astype(F32) * s_f.astype(F32)
                rs_b = r_b.astype(F32) * s_b.astype(F32)
                dl_f = (inner * tab(TAB_EF) + tab(TAB_RA) * qf * x_f + tab(TAB_RCF) * kf * y_f
                        + tab(TAB_KF) * (rs_f[:, :CHUNK] + rs_f[:, CHUNK:]))
                dl_b = (inner * tab(TAB_EB) + tab(TAB_RB) * qf * x_b + tab(TAB_RCB) * kf * y_b
                        + tab(TAB_KB) * (rs_b[:, :CHUNK] + rs_b[:, CHUNK:]))
                dl_ref[2 * h:2 * h + 1, :] += jnp.sum(dl_f, axis=0, keepdims=True)
                dl_ref[2 * h + 1:2 * h + 2, :] += jnp.sum(dl_b, axis=0, keepdims=True)

    row = lambda i: (i, 0)
    st = lambda i: (i, 0, 0)
    return pl.pallas_call(
        body, name="ret_bwd", grid=(nc // per,),
        in_specs=[pl.BlockSpec((rows_per, RET_Q), row), pl.BlockSpec((rows_per, RET_Q), row),
                  pl.BlockSpec((rows_per, RET_V), lambda i: (i, 1)), pl.BlockSpec((rows_per, RET_V), row),
                  pl.BlockSpec((per, RET_Q, RET_DV), st), pl.BlockSpec((per, RET_Q, RET_DV), st),
                  pl.BlockSpec((per, RET_Q, RET_DV), st), pl.BlockSpec((per, RET_Q, RET_DV), st),
                  pl.BlockSpec((RET_HEADS, 14, CHUNK, CHUNK), lambda i: (0, 0, 0, 0))],
        out_specs=[pl.BlockSpec((rows_per, RET_Q), row), pl.BlockSpec((rows_per, RET_Q), row),
                   pl.BlockSpec((rows_per, RET_V), row), pl.BlockSpec((8, CHUNK), lambda i: (0, 0))],
        out_shape=[_sds((T, RET_Q), F32), _sds((T, RET_Q), F32), _sds((T, RET_V), _MXU), _sds((8, CHUNK), F32)],
        compiler_params=_params("arbitrary"),
    )(qr, kr, proj, g_out, sf, sb, rf, rb, tabs)


SWA_GROUP = SWA_HEADS // SWA_KV_HEADS
SWA_COLS = SWA_GROUP * CHUNK


def _swa_stack(ref, g):
    return jnp.concatenate([ref[:, h * HEAD_DIM:(h + 1) * HEAD_DIM] for h in range(g * SWA_GROUP, (g + 1) * SWA_GROUP)], axis=0)


def _swa_probs_t(q, k_win, bias_t, sink_row, i, nb):
    st = _dot(k_win, q, _NT) * ATT_SCALE + bias_t
    key = lax.broadcasted_iota(jnp.int32, st.shape, 0)
    valid = jnp.logical_and(jnp.logical_or(key >= CHUNK, i > 0), jnp.logical_or(key < 2 * CHUNK, i < nb - 1))
    st = jnp.where(valid, st, NEG_INF)
    m = jnp.maximum(jnp.max(st, axis=0, keepdims=True), sink_row)
    p = jnp.exp(st - m)
    e_sink = jnp.exp(sink_row - m)
    inv = 1.0 / (jnp.sum(p, axis=0, keepdims=True) + e_sink)
    return p * inv, e_sink * inv


def _swa_layouts(bias, sink):
    bias_t = bias.reshape(SWA_KV_HEADS, SWA_GROUP, CHUNK, 3 * CHUNK).transpose(0, 3, 1, 2).reshape(SWA_KV_HEADS, 3 * CHUNK, SWA_COLS)
    return bias_t, jnp.repeat(sink[:, 0], CHUNK).reshape(SWA_KV_HEADS, SWA_COLS)


def _swa_window_specs(nb, width, col_block, clamp):
    prev = lambda i: (jnp.maximum(clamp(i) - 1, 0), col_block)
    cur = lambda i: (clamp(i), col_block)
    nxt = lambda i: (jnp.minimum(clamp(i) + 1, nb - 1), col_block)
    return [pl.BlockSpec((CHUNK, width), f) for f in (prev, cur, nxt)]


def _swa_fwd(qn, kn, proj, bias, sink):
    T = qn.shape[0]
    nb = T // CHUNK
    kvw = SWA_KV_HEADS * HEAD_DIM
    bias_t, sink_rows = _swa_layouts(bias, sink)

    def body(q_ref, k0, k1, k2, v0, v1, v2, bias_ref, sink_ref, y_ref):
        i = pl.program_id(0)
        for g in range(SWA_KV_HEADS):
            gs = slice(g * HEAD_DIM, (g + 1) * HEAD_DIM)
            k_win = jnp.concatenate([k0[:, gs], k1[:, gs], k2[:, gs]], axis=0)
            v_win = jnp.concatenate([v0[:, gs], v1[:, gs], v2[:, gs]], axis=0).astype(_MXU)
            pt, _ = _swa_probs_t(_swa_stack(q_ref, g), k_win, bias_ref[g], sink_ref[g:g + 1, :], i, nb)
            o = _dot(v_win, pt, _TN).T
            for hh in range(SWA_GROUP):
                h = g * SWA_GROUP + hh
                y_ref[:, h * HEAD_DIM:(h + 1) * HEAD_DIM] = o[hh * CHUNK:(hh + 1) * CHUNK].astype(y_ref.dtype)

    ident = lambda i: i
    return pl.pallas_call(
        body, name="swa_fwd", grid=(nb,),
        in_specs=[pl.BlockSpec((CHUNK, D_MODEL), lambda i: (i, 0))]
        + _swa_window_specs(nb, kvw, 0, ident) + _swa_window_specs(nb, kvw, 17, ident)
        + [pl.BlockSpec((SWA_KV_HEADS, 3 * CHUNK, SWA_COLS), lambda i: (0, 0, 0)), pl.BlockSpec((SWA_KV_HEADS, SWA_COLS), lambda i: (0, 0))],
        out_specs=pl.BlockSpec((CHUNK, D_MODEL), lambda i: (i, 0)),
        out_shape=_sds((T, D_MODEL), _MXU),
        compiler_params=_params("parallel"),
    )(qn, kn, kn, kn, proj, proj, proj, bias_t, sink_rows)


def _swa_bwd(qn, kn, proj, dycat, bias, sink):
    T = qn.shape[0]
    nb = T // CHUNK
    kvw = SWA_KV_HEADS * HEAD_DIM
    bias_t, sink_rows = _swa_layouts(bias, sink)

    def body(q_ref, k0, k1, k2, v0, v1, v2, dy_ref, bias_ref, sink_ref,
             dq_ref, dk_ref, dv_ref, dbias_ref, dsink_ref, acc_a, acc_b):
        i = pl.program_id(0)

        @pl.when(i == 0)
        def _():
            dbias_ref[...] = jnp.zeros_like(dbias_ref)
            dsink_ref[...] = jnp.zeros_like(dsink_ref)
            acc_a[...] = jnp.zeros_like(acc_a)
            acc_b[...] = jnp.zeros_like(acc_b)

        @pl.when(i < nb)
        def _():
            for g in range(SWA_KV_HEADS):
                gs = slice(g * HEAD_DIM, (g + 1) * HEAD_DIM)
                k_win = jnp.concatenate([k0[:, gs], k1[:, gs], k2[:, gs]], axis=0)
                v_win = jnp.concatenate([v0[:, gs], v1[:, gs], v2[:, gs]], axis=0).astype(_MXU)
                q, dy = _swa_stack(q_ref, g), _swa_stack(dy_ref, g)
                pt, p_sink = _swa_probs_t(q, k_win, bias_ref[g], sink_ref[g:g + 1, :], i, nb)
                dpt = _dot(v_win, dy, _NT)
                delta = jnp.sum(pt * dpt, axis=0, keepdims=True)
                dst = pt * (dpt - delta)
                dbias_ref[g] += dst
                dsink_ref[g:g + 1, :] += -p_sink * delta
                dq = (_dot(k_win, dst, _TN) * ATT_SCALE).T
                for hh in range(SWA_GROUP):
                    h = g * SWA_GROUP + hh
                    dq_ref[:, h * HEAD_DIM:(h + 1) * HEAD_DIM] = dq[hh * CHUNK:(hh + 1) * CHUNK]
                dk_win = _dot(dst, q) * ATT_SCALE
                dv_win = _dot(pt, dy)
                for win, out_ref, col0 in ((dk_win, dk_ref, 0), (dv_win, dv_ref, kvw)):
                    cs = slice(col0 + g * HEAD_DIM, col0 + (g + 1) * HEAD_DIM)
                    out_ref[:, gs] = acc_a[:, cs] + win[:CHUNK]
                    acc_a[:, cs] = acc_b[:, cs] + win[CHUNK:2 * CHUNK]
                    acc_b[:, cs] = win[2 * CHUNK:]

        @pl.when(i == nb)
        def _():
            dk_ref[...] = acc_a[:, :kvw]
            dv_ref[...] = acc_a[:, kvw:]

    clamp = lambda i: jnp.minimum(i, nb - 1)
    late = lambda i: (jnp.maximum(i - 1, 0), 0)
    bias_spec = pl.BlockSpec((SWA_KV_HEADS, 3 * CHUNK, SWA_COLS), lambda i: (0, 0, 0))
    sink_spec = pl.BlockSpec((SWA_KV_HEADS, SWA_COLS), lambda i: (0, 0))
    dq, dk, dv, dbias_t, dsink_rows = pl.pallas_call(
        body, name="swa_bwd", grid=(nb + 1,),
        in_specs=[pl.BlockSpec((CHUNK, D_MODEL), lambda i: (clamp(i), 0))]
        + _swa_window_specs(nb, kvw, 0, clamp) + _swa_window_specs(nb, kvw, 17, clamp)
        + [pl.BlockSpec((CHUNK, D_MODEL), lambda i: (clamp(i), 1)), bias_spec, sink_spec],
        out_specs=[pl.BlockSpec((CHUNK, D_MODEL), lambda i: (clamp(i), 0)),
                   pl.BlockSpec((CHUNK, kvw), late), pl.BlockSpec((CHUNK, kvw), late), bias_spec, sink_spec],
        out_shape=[_sds((T, D_MODEL), F32), _sds((T, kvw), F32), _sds((T, kvw), F32),
                   _sds((SWA_KV_HEADS, 3 * CHUNK, SWA_COLS), F32), _sds((SWA_KV_HEADS, SWA_COLS), F32)],
        scratch_shapes=[pltpu.VMEM((CHUNK, 2 * kvw), F32), pltpu.VMEM((CHUNK, 2 * kvw), F32)],
        compiler_params=_params("arbitrary"),
    )(qn, kn, kn, kn, proj, proj, proj, dycat, bias_t, sink_rows)
    dbias = dbias_t.reshape(SWA_KV_HEADS, 3 * CHUNK, SWA_GROUP, CHUNK).transpose(0, 2, 3, 1).reshape(SWA_HEADS, CHUNK, 3 * CHUNK)
    dsink = jnp.sum(dsink_rows.reshape(SWA_HEADS, CHUNK), axis=1, keepdims=True) * jnp.ones((1, HEAD_DIM), F32)
    return dq, dk, dv, dbias, dsink


def _t5_bucket_reduce(dbias, bucket):
    def body(db_ref, bk_ref, o_ref):
        bk = bk_ref[...]
        row = lax.broadcasted_iota(jnp.int32, (SWA_HEADS, HEAD_DIM), 0)
        lane = lax.broadcasted_iota(jnp.int32, (SWA_HEADS, HEAD_DIM), 1)

        def per_bucket(b, acc):
            mask = bk == b
            for h in range(SWA_HEADS):
                tot = jnp.sum(jnp.sum(jnp.where(mask, db_ref[h], 0.0), axis=0, keepdims=True), axis=1, keepdims=True)
                acc = acc + jnp.where(jnp.logical_and(row == h, lane == b), tot, 0.0)
            return acc

        o_ref[...] = lax.fori_loop(0, T5_BUCKETS, per_bucket, jnp.zeros((SWA_HEADS, HEAD_DIM), F32))

    return pl.pallas_call(body, name="t5_bucket_reduce", out_shape=_sds((SWA_HEADS, HEAD_DIM), F32),
                          compiler_params=pltpu.CompilerParams(vmem_limit_bytes=VMEM_LIMIT_BYTES))(dbias, bucket)


def _headnorm_bwd(x, dy, gain):
    r = lax.rsqrt(jnp.mean(x * x, axis=-1, keepdims=True) + EPS)
    xhat = x * r
    dyg = dy * gain
    return r * (dyg - xhat * jnp.mean(dyg * xhat, axis=-1, keepdims=True)), dy * xhat


def _post_even(proj, dqr, dkr, dva, dga, dqn, dkn, dvb, cos, sin, q_gain, k_gain, *, tm):
    T = proj.shape[0]
    tm = min(tm, T)
    kvw = SWA_KV_HEADS * HEAD_DIM

    def body(qb_ref, kb_ref, dqr_ref, dkr_ref, dva_ref, dga_ref, dqn_ref, dkn_ref, dvb_ref, c_ref, s_ref, qg_ref, kg_ref,
             dp_ref, dqg_ref, dkg_ref):
        @pl.when(pl.program_id(0) == 0)
        def _():
            dqg_ref[...] = jnp.zeros_like(dqg_ref)
            dkg_ref[...] = jnp.zeros_like(dkg_ref)
        c = jnp.concatenate([c_ref[...]] * RET_HEADS, axis=1)
        s = jnp.concatenate([s_ref[...]] * RET_HEADS, axis=1)
        dq = dqr_ref[...]
        dp_ref[:, 0:RET_Q] = (dq * c + _swap_halves(dq * s, RET_DK // 2)).astype(dp_ref.dtype)
        dk = dkr_ref[...] * (RET_DK ** -0.5)
        dp_ref[:, RET_Q:2 * RET_Q] = (dk * c + _swap_halves(dk * s, RET_DK // 2)).astype(dp_ref.dtype)
        off = 2 * RET_Q
        dp_ref[:, off:off + RET_V] = dva_ref[...].astype(dp_ref.dtype)
        dp_ref[:, off + RET_V:off + 2 * RET_V] = dga_ref[...].astype(dp_ref.dtype)
        off += 2 * RET_V
        for src, dsrc, gain, dgain, heads, base in ((qb_ref, dqn_ref, qg_ref, dqg_ref, SWA_HEADS, off),
                                                    (kb_ref, dkn_ref, kg_ref, dkg_ref, SWA_KV_HEADS, off + D_MODEL)):
            for h in range(heads):
                sl = slice(h * HEAD_DIM, (h + 1) * HEAD_DIM)
                dx, dgx = _headnorm_bwd(src[:, sl], dsrc[:, sl], gain[...])
                dp_ref[:, base + h * HEAD_DIM:base + (h + 1) * HEAD_DIM] = dx.astype(dp_ref.dtype)
                dgain[...] += _rowsum8(dgx)
        dp_ref[:, off + D_MODEL + kvw:] = dvb_ref[...].astype(dp_ref.dtype)

    row = lambda i: (i, 0)
    const = lambda i: (0, 0)
    return pl.pallas_call(
        body, name="post_even", grid=(T // tm,),
        in_specs=[pl.BlockSpec((tm, D_MODEL), lambda i: (i, 3)), pl.BlockSpec((tm, kvw), lambda i: (i, 16)),
                  pl.BlockSpec((tm, RET_Q), row), pl.BlockSpec((tm, RET_Q), row),
                  pl.BlockSpec((tm, RET_V), row), pl.BlockSpec((tm, RET_V), row),
                  pl.BlockSpec((tm, D_MODEL), row), pl.BlockSpec((tm, kvw), row), pl.BlockSpec((tm, kvw), row),
                  pl.BlockSpec((tm, RET_DK), row), pl.BlockSpec((tm, RET_DK), row),
                  pl.BlockSpec((1, HEAD_DIM), const), pl.BlockSpec((1, HEAD_DIM), const)],
        out_specs=[pl.BlockSpec((tm, EVEN_IN), row), pl.BlockSpec((8, HEAD_DIM), const), pl.BlockSpec((8, HEAD_DIM), const)],
        out_shape=[_sds((T, EVEN_IN), _MXU), _sds((8, HEAD_DIM), F32), _sds((8, HEAD_DIM), F32)],
        compiler_params=_params("arbitrary"),
    )(proj, proj, dqr, dkr, dva, dga, dqn, dkn, dvb, cos, sin, q_gain, k_gain)


def _prep_odd(proj, cos, sin, q_gain, k_gain, *, tm):
    T = proj.shape[0]
    tm = min(tm, T)
    kvw = AX_KV_HEADS * HEAD_DIM

    def body(q_ref, k_ref, v_ref, c_ref, s_ref, qg_ref, kg_ref, qx_ref, kx_ref, vx_ref):
        c, s = c_ref[...], s_ref[...]
        for src, gain, dst, heads, scale in ((q_ref, qg_ref, qx_ref, AX_HEADS, SCORE_SCALE_LOG2), (k_ref, kg_ref, kx_ref, AX_KV_HEADS, 1.0)):
            for h in range(heads):
                sl = slice(h * HEAD_DIM, (h + 1) * HEAD_DIM)
                xh = src[:, sl]
                r = lax.rsqrt(jnp.mean(xh * xh, axis=-1, keepdims=True) + EPS)
                xn = xh * r * gain[...]
                dst[:, sl] = ((xn * c + _swap_halves(xn, HEAD_DIM // 4) * s) * scale).astype(dst.dtype)
        vx_ref[...] = v_ref[...].astype(vx_ref.dtype)

    row = lambda i: (i, 0)
    const = lambda i: (0, 0)
    return pl.pallas_call(
        body, name="prep_odd", grid=(T // tm,),
        in_specs=[pl.BlockSpec((tm, D_MODEL), row), pl.BlockSpec((tm, kvw), lambda i: (i, 4)), pl.BlockSpec((tm, kvw), lambda i: (i, 5)),
                  pl.BlockSpec((tm, HEAD_DIM), row), pl.BlockSpec((tm, HEAD_DIM), row),
                  pl.BlockSpec((1, HEAD_DIM), const), pl.BlockSpec((1, HEAD_DIM), const)],
        out_specs=[pl.BlockSpec((tm, D_MODEL), row), pl.BlockSpec((tm, kvw), row), pl.BlockSpec((tm, kvw), row)],
        out_shape=[_sds((T, D_MODEL), _MXU), _sds((T, kvw), _MXU), _sds((T, kvw), _MXU)],
        compiler_params=_params("parallel"),
    )(proj, proj, proj, cos, sin, q_gain, k_gain)


def _post_odd(proj, dqxt, dkx, dvx, cos, sin, q_gain, k_gain, *, tm):
    T = proj.shape[0]
    tm = min(tm, T)
    kvw = AX_KV_HEADS * HEAD_DIM

    def body(q_ref, k_ref, dqt_ref, dk_ref, dv_ref, c_ref, s_ref, qg_ref, kg_ref, dp_ref, dqg_ref, dkg_ref):
        @pl.when(pl.program_id(0) == 0)
        def _():
            dqg_ref[...] = jnp.zeros_like(dqg_ref)
            dkg_ref[...] = jnp.zeros_like(dkg_ref)
        c, s = c_ref[...], s_ref[...]
        for src, dsrc, gain, dgain, heads, base in ((q_ref, dqt_ref, qg_ref, dqg_ref, AX_HEADS, 0),
                                                    (k_ref, dk_ref, kg_ref, dkg_ref, AX_KV_HEADS, D_MODEL)):
            for h in range(heads):
                sl = slice(h * HEAD_DIM, (h + 1) * HEAD_DIM)
                d = dsrc[sl, :].T if dsrc is dqt_ref else dsrc[:, sl]
                dn = d * c + _swap_halves(d * s, HEAD_DIM // 4)
                dx, dgx = _headnorm_bwd(src[:, sl], dn, gain[...])
                dp_ref[:, base + h * HEAD_DIM:base + (h + 1) * HEAD_DIM] = dx.astype(dp_ref.dtype)
                dgain[...] += _rowsum8(dgx)
        dp_ref[:, D_MODEL + kvw:] = dv_ref[...].astype(dp_ref.dtype)

    row = lambda i: (i, 0)
    const = lambda i: (0, 0)
    return pl.pallas_call(
        body, name="post_odd", grid=(T // tm,),
        in_specs=[pl.BlockSpec((tm, D_MODEL), row), pl.BlockSpec((tm, kvw), lambda i: (i, 4)),
                  pl.BlockSpec((D_MODEL, tm), lambda i: (0, i)), pl.BlockSpec((tm, kvw), row), pl.BlockSpec((tm, kvw), row),
                  pl.BlockSpec((tm, HEAD_DIM), row), pl.BlockSpec((tm, HEAD_DIM), row),
                  pl.BlockSpec((1, HEAD_DIM), const), pl.BlockSpec((1, HEAD_DIM), const)],
        out_specs=[pl.BlockSpec((tm, ODD_IN), row), pl.BlockSpec((8, HEAD_DIM), const), pl.BlockSpec((8, HEAD_DIM), const)],
        out_shape=[_sds((T, ODD_IN), _MXU), _sds((8, HEAD_DIM), F32), _sds((8, HEAD_DIM), F32)],
        compiler_params=_params("arbitrary"),
    )(proj, proj, dqxt, dkx, dvx, cos, sin, q_gain, k_gain)


ONES_ROWS = 16


def _flash_fwd(qx, kx, vx, *, tq, tk):
    v1t = jnp.concatenate([vx.T.reshape(AX_KV_HEADS, HEAD_DIM, vx.shape[0]),
                           jnp.ones((AX_KV_HEADS, ONES_ROWS, vx.shape[0]), vx.dtype)], axis=1)
    T = qx.shape[0]
    tq, tk = min(tq, T), min(tk, T)
    nq, nk = T // tq, T // tk
    group = AX_HEADS // AX_KV_HEADS

    def body(k_ref, v_ref, q_ref, o_ref, lse_ref, acc_sc, m_sc, l_sc):
        j = pl.program_id(2)

        @pl.when(j == 0)
        def _():
            m_sc[...] = jnp.full(m_sc.shape, NEG_INF, F32)
            l_sc[...] = jnp.zeros_like(l_sc)
            acc_sc[...] = jnp.zeros_like(acc_sc)
        k, v = k_ref[...], v_ref[0]

        def step(i, carry):
            cols = pl.ds(pl.multiple_of(i * tq, tq), tq)
            st = _dot(k, q_ref[cols, :], _NT)
            m_old = m_sc[i]
            m_new = jnp.maximum(m_old, jnp.max(st, axis=0, keepdims=True))
            p = jnp.exp2(st - m_new)
            alpha = jnp.exp2(m_old - m_new)
            pv = _dot(v, p)
            m_sc[i] = m_new
            l_sc[i] = alpha * l_sc[i] + pv[HEAD_DIM:HEAD_DIM + 1]
            acc_sc[:, cols] = alpha * acc_sc[:, cols] + pv[:HEAD_DIM]
            return carry

        lax.fori_loop(0, nq, step, 0)

        @pl.when(j == nk - 1)
        def _():
            def finish(i, carry):
                cols = pl.ds(pl.multiple_of(i * tq, tq), tq)
                o_ref[cols, :] = (acc_sc[:, cols] / l_sc[i]).T.astype(o_ref.dtype)
                lse_ref[0, i] = m_sc[i] + jnp.log2(l_sc[i])
                return carry

            lax.fori_loop(0, nq, finish, 0)

    kv = lambda g, h, j: (j, g)
    qh = lambda g, h, j: (0, g * group + h)
    o, lse = pl.pallas_call(
        body, name="flash_fwd", grid=(AX_KV_HEADS, group, nk),
        in_specs=[pl.BlockSpec((tk, HEAD_DIM), kv), pl.BlockSpec((1, HEAD_DIM + ONES_ROWS, tk), lambda g, h, j: (g, 0, j)),
                  pl.BlockSpec((T, HEAD_DIM), qh)],
        out_specs=[pl.BlockSpec((T, HEAD_DIM), qh), pl.BlockSpec((1, nq, 1, tq), lambda g, h, j: (g * group + h, 0, 0, 0))],
        out_shape=[_sds((T, D_MODEL), _MXU), _sds((AX_HEADS, nq, 1, tq), F32)],
        scratch_shapes=[pltpu.VMEM((HEAD_DIM, T), F32), pltpu.VMEM((nq, 1, tq), F32), pltpu.VMEM((nq, 1, tq), F32)],
        compiler_params=_params("parallel", "arbitrary", "arbitrary"),
    )(kx, v1t, qx)
    return o, lse.reshape(AX_HEADS, 1, T)


def _flash_bwd(qx, kx, vx, o, do, lse, *, tq, tk):
    T = qx.shape[0]
    tq, tk = min(tq, T), min(tk, T)
    nq = T // tq
    group = AX_HEADS // AX_KV_HEADS
    lse_rows = lse.reshape(AX_HEADS, nq, 1, tq)
    kxt = kx.T.reshape(AX_KV_HEADS, HEAD_DIM, T)

    def body(k_ref, kt_ref, v_ref, q_ref, o_ref, do_ref, lse_ref, dqt_ref, dk_ref, dv_ref, delta_sc):
        j = pl.program_id(2)

        @pl.when(jnp.logical_and(pl.program_id(1) == 0, j == 0))
        def _():
            dk_ref[...] = jnp.zeros_like(dk_ref)
            dv_ref[...] = jnp.zeros_like(dv_ref)

        @pl.when(j == 0)
        def _():
            dqt_ref[...] = jnp.zeros_like(dqt_ref)

            def row_delta(i, carry):
                rows = pl.ds(pl.multiple_of(i * tq, tq), tq)
                prod = do_ref[rows, :].astype(F32) * o_ref[rows, :].astype(F32)
                delta_sc[i] = jnp.sum(prod.T, axis=0, keepdims=True)
                return carry

            lax.fori_loop(0, nq, row_delta, 0)
        k, v = k_ref[...], v_ref[...]

        def step(i, carry):
            dk, dv = carry
            off = pl.multiple_of(i * tq, tq)
            q, do_blk = q_ref[pl.ds(off, tq), :], do_ref[pl.ds(off, tq), :]
            pt = jnp.exp2(_dot(k, q, _NT) - lse_ref[0, i])
            dst = pt * (_dot(v, do_blk, _NT) - delta_sc[i])
            dqt_ref[:, pl.ds(off, tq)] += _dot(kt_ref[0], dst) * ATT_SCALE
            return dk + _dot(dst, q), dv + _dot(pt, do_blk)

        zero = jnp.zeros((tk, HEAD_DIM), F32)
        dk, dv = lax.fori_loop(0, nq, step, (zero, zero))
        rows = pl.ds(pl.multiple_of(j * tk, tk), tk)
        dk_ref[rows, :] += dk * (ATT_SCALE / SCORE_SCALE_LOG2)
        dv_ref[rows, :] += dv

    kv = lambda g, h, j: (j, g)
    qh = lambda g, h, j: (0, g * group + h)
    st = lambda g, h, j: (g * group + h, 0, 0, 0)
    acc = lambda g, h, j: (0, g)
    return pl.pallas_call(
        body, name="flash_bwd", grid=(AX_KV_HEADS, group, T // tk),
        in_specs=[pl.BlockSpec((tk, HEAD_DIM), kv), pl.BlockSpec((1, HEAD_DIM, tk), lambda g, h, j: (g, 0, j)),
                  pl.BlockSpec((tk, HEAD_DIM), kv),
                  pl.BlockSpec((T, HEAD_DIM), qh), pl.BlockSpec((T, HEAD_DIM), qh), pl.BlockSpec((T, HEAD_DIM), qh),
                  pl.BlockSpec((1, nq, 1, tq), st)],
        out_specs=[pl.BlockSpec((HEAD_DIM, T), lambda g, h, j: (g * group + h, 0)),
                   pl.BlockSpec((T, HEAD_DIM), acc), pl.BlockSpec((T, HEAD_DIM), acc)],
        out_shape=[_sds((D_MODEL, T), F32), _sds((T, AX_KV_HEADS * HEAD_DIM), F32), _sds((T, AX_KV_HEADS * HEAD_DIM), F32)],
        scratch_shapes=[pltpu.VMEM((nq, 1, tq), F32)],
        compiler_params=_params("parallel", "arbitrary", "arbitrary"),
    )(kx, kxt, vx, qx, o, do, lse_rows)


TM = 1024
TM_WIDE = 512


def _mlp_fwd(tag, x, gain, fetch, target=None):
    u, h = _norm_matmul(f"mlp_up{tag}", x, gain, (fetch("w_mlp_up", x), tag), tm=TM_WIDE, tn=D_FF, out_dtype=_MXU)
    out = _matmul_res(f"mlp_down{tag}", [u], (fetch("w_mlp_down", u), tag), x, tm=TM if target is None else TM_WIDE,
                      relu2=True, target=target)
    return out, (x, u, h)


def _local_step(x, target, p, fetch, push, tokens=()):
    T = x.shape[0]
    cos_r, sin_r = _ret_rope_tables(T)
    cos_a, sin_a = _axial_rope_tables(T)
    tabs, rw, log_gamma = _retention_tables(p["ret_decay_logit"][0])
    bias = _swa_bias(p["t5_table"])
    sink = p["swa_sink"][0][:, None] * jnp.ones((1, HEAD_DIM), F32)
    nm, nl = p["norm_mix"], p["norm_mlp"]
    pending = [t for t in tokens if t is not None]

    def send(tag, weight, dw):
        token = push(tag, weight, dw[None])
        if token is not None:
            pending.append(token)

    def tied(operand):
        while pending:
            operand = operand + pending.pop()[0:1, 0:1]
        return operand

    def mlp_bwd(tag, saved, gain, dy, dy16):
        xs, u, h = saved
        w_up, w_down = (fetch("w_mlp_up", None), tag), (fetch("w_mlp_down", None), tag)
        du = _matmul_nt(f"mlp_down{tag}_bwd", dy16, w_down, tm=TM_WIDE, tn=D_FF, out_dtype=_MXU, relu_of=u)
        send(f"mlp_down{tag}", "w_mlp_down", _matmul_tn(f"mlp_down{tag}_dw", u, dy16, tk=2048, tn=1024, tt=1024, out_dtype=_WIRE, relu2=True))
        dx, dx16, dgain = _matmul_nt_normbwd(f"mlp_up{tag}_bwd", du, w_up, xs, tied(gain), dy, tm=TM_WIDE)
        send(f"mlp_up{tag}", "w_mlp_up", _matmul_tn(f"mlp_up{tag}_dw", h, du, tk=1024, tn=2048, tt=1024, out_dtype=_WIRE))
        return dx, dx16, dgain

    w_in_even = fetch("w_in_even", cos_r[:8] + cos_a[:8] + bias[0, :8, :HEAD_DIM] + tabs[0, 0, :8] + rw["cf"][:8, :HEAD_DIM])
    proj0, h0 = _norm_matmul("in_even", x, tied(nm[0:1]), w_in_even, tm=TM_WIDE, tn=EVEN_IN, out_dtype=F32)
    qr, kr, qn, kn = _prep_even(proj0, cos_r, sin_r, p["swa_q_norm"], p["swa_k_norm"], tm=TM)
    sf, sb = _ret_scan("ret_scan_fwd", kr, proj0, 1, rw["cf"], rw["dec_f"], rw["cb"], rw["dec_b"])
    ret_o, ya = _ret_out(qr, kr, proj0, sf, sb, tabs[:, (TAB_D, TAB_A, TAB_B)], p["ret_norm"])
    yb = _swa_fwd(qn, kn, proj0, bias, sink)
    w_out_even = fetch("w_out_even", yb)
    x1 = _matmul_res("out_even", [ya, yb], w_out_even, x, tm=TM)
    x2, mlp0 = _mlp_fwd(0, x1, nl[0:1], fetch)
    w_in_odd, w_out_odd = fetch("w_in_odd", x2), fetch("w_out_odd", x2)
    proj1, h1 = _norm_matmul("in_odd", x2, nm[1:2], w_in_odd, tm=TM, tn=ODD_IN, out_dtype=F32)
    qx, kx, vx = _prep_odd(proj1, cos_a, sin_a, p["ax_q_norm"], p["ax_k_norm"], tm=TM)
    o, lse = _flash_fwd(qx, kx, vx, tq=2048, tk=2048)
    x3 = _matmul_res("out_odd", [o], w_out_odd, x2, tm=TM)
    (g4, g4_16, loss_part), mlp1 = _mlp_fwd(1, x3, nl[1:2], fetch, target=target)

    dx3, dx3_16, dnl1 = mlp_bwd(1, mlp1, nl[1:2], g4, g4_16)
    do = _matmul_nt("out_odd_bwd", dx3_16, w_out_odd, tm=TM, tn=1024, out_dtype=_MXU)
    send("out_odd", "w_out_odd", _matmul_tn("out_odd_dw", o, dx3_16, tk=1024, tn=1024, tt=1024, out_dtype=_WIRE))
    dqxt, dkx, dvx = _flash_bwd(qx, kx, vx, o, do, lse, tq=2048, tk=512)
    dproj1, dqg1, dkg1 = _post_odd(proj1, dqxt, dkx, dvx, cos_a, sin_a, tied(p["ax_q_norm"]), p["ax_k_norm"], tm=TM)
    send("in_odd", "w_in_odd", _matmul_tn("in_odd_dw", h1, dproj1, tk=1024, tn=768, tt=1024, out_dtype=_WIRE))
    dx2, dx2_16, dnm1 = _matmul_nt_normbwd("in_odd_bwd", dproj1, w_in_odd, x2, tied(nm[1:2]), dx3, tm=TM_WIDE)
    dx1, dx1_16, dnl0 = mlp_bwd(0, mlp0, nl[0:1], dx2, dx2_16)
    dycat = _matmul_nt("out_even_bwd", dx1_16, w_out_even, tm=TM, tn=2 * D_MODEL, out_dtype=F32)
    send("out_even", "w_out_even", jnp.concatenate([
        _matmul_tn("out_even_dw_ret", ya, dx1_16, tk=1024, tn=1024, tt=1024, out_dtype=_WIRE),
        _matmul_tn("out_even_dw_swa", yb, dx1_16, tk=1024, tn=1024, tt=1024, out_dtype=_WIRE)], axis=0))
    g_out, dga, dretg = _ret_gate_bwd(dycat, proj0, ret_o, tied(p["ret_norm"]), tm=TM)
    rb, rf = _ret_scan("ret_scan_bwd", qr, g_out, 0, rw["b"], rw["dec_b"], rw["a"], rw["dec_f"])
    dqr, dkr, dva, dlog = _ret_bwd(qr, kr, proj0, g_out, sf, sb, rf, rb, tabs)
    dqn, dkn, dvb, dbias, dsink = _swa_bwd(qn, kn, proj0, dycat, bias, sink)
    dt5 = _t5_bucket_reduce(dbias, _t5_bucket(_swa_rel()).astype(jnp.int32))
    dproj0, dqg0, dkg0 = _post_even(proj0, dqr, dkr, dva, dga, dqn, dkn, dvb, cos_r, sin_r,
                                    p["swa_q_norm"], p["swa_k_norm"], tm=TM_WIDE)
    send("in_even", "w_in_even", _matmul_tn("in_even_dw", h0, dproj0, tk=1024, tn=2304, tt=1024, out_dtype=_WIRE))
    dx0, _, dnm0 = _matmul_nt_normbwd("in_even_bwd", dproj0, w_in_even, x, tied(nm[0:1]), dx1, tm=TM_WIDE)

    fold = lambda part: jnp.sum(part, axis=0)
    dlam = jnp.sum(dlog, axis=1).reshape(RET_HEADS, 2).T
    small = {
        "norm_mix": jnp.stack([fold(dnm0), fold(dnm1)]),
        "norm_mlp": jnp.stack([fold(dnl0), fold(dnl1)]),
        "ret_decay_logit": (dlam * (1.0 - jnp.exp(log_gamma)))[None],
        "ret_norm": fold(dretg)[None],
        "swa_q_norm": fold(dqg0)[None], "swa_k_norm": fold(dkg0)[None],
        "swa_sink": dsink[:, 0][None],
        "t5_table": dt5[:, :T5_BUCKETS].T,
        "ax_q_norm": fold(dqg1)[None], "ax_k_norm": fold(dkg1)[None],
    }
    return loss_part, dx0, small


BIG = ("w_in_even", "w_out_even", "w_in_odd", "w_out_odd", "w_mlp_up", "w_mlp_down")
SMALL = ("norm_mix", "norm_mlp", "ret_decay_logit", "ret_norm", "swa_q_norm", "swa_k_norm", "swa_sink", "t5_table",
         "ax_q_norm", "ax_k_norm")
WEIGHTS = ("norm_mix", "norm_mlp", "w_in_even", "w_out_even", "ret_decay_logit", "ret_norm", "swa_q_norm", "swa_k_norm",
           "swa_sink", "t5_table", "w_in_odd", "w_out_odd", "ax_q_norm", "ax_k_norm", "w_mlp_up", "w_mlp_down")
SHARD_AXIS = {"w_in_even": 2, "w_out_even": 1, "w_in_odd": 2, "w_out_odd": 1, "w_mlp_up": 2, "w_mlp_down": 1}
N_CHIPS = 4
GATHER_ORDER = (("w_in_even",), ("w_out_even",), ("w_mlp_up",), ("w_mlp_down",), ("w_in_odd", "w_out_odd"))
ANY = pl.BlockSpec(memory_space=pl.ANY)
HBM = pl.BlockSpec(memory_space=pltpu.HBM)
SEM = pl.BlockSpec(memory_space=pltpu.SEMAPHORE)
SPLIT_COPY = pltpu.CompilerParams(has_side_effects=pltpu.SideEffectType.DATAFLOW_SIDE_EFFECTING)


def _in_hbm(a):
    return pltpu.with_memory_space_constraint(a, pltpu.HBM)


def _mesh_pos():
    return lax.axis_index("x"), lax.axis_index("y"), lax.axis_index("c")


def _window(ref, axis, start, size):
    idx = [slice(None)] * len(ref.shape)
    idx[axis] = pl.ds(start, size)
    return ref.at[tuple(idx)]


def _cast_place(key, shard, chip, *, tr=256):
    L, R, C = shard.shape
    tr = min(tr, R)
    axis = SHARD_AXIS[key]
    whole = tuple(d * (N_CHIPS if a == axis else 1) for a, d in enumerate(shard.shape))

    def body(chip_ref, s_ref, o_ref):
        o_ref[...] = s_ref[...].astype(o_ref.dtype)

    if axis == 2:
        out_map = lambda l, i, chip_ref: (l, i, chip_ref[0])
    else:
        out_map = lambda l, i, chip_ref: (l, i + chip_ref[0] * (R // tr), 0)
    grid_spec = pltpu.PrefetchScalarGridSpec(
        num_scalar_prefetch=1, grid=(L, R // tr),
        in_specs=[pl.BlockSpec((1, tr, C), lambda l, i, chip_ref: (l, i, 0))],
        out_specs=pl.BlockSpec((1, tr, C), out_map))
    return pl.pallas_call(body, name=f"cast_place_{key}", grid_spec=grid_spec, out_shape=_sds(whole, _MXU),
                          compiler_params=_params("parallel", "parallel"))(chip, shard)


def _gather_copies(names, refs, send_sems, recv_sems, *, outgoing=True, incoming=True):
    x, y, c = _mesh_pos()
    chips = [(1 - x, y), (x, 1 - y), (1 - x, 1 - y)]
    out, inc = [], []
    for t, key in enumerate(names):
        size = refs[t].shape[SHARD_AXIS[key]] // N_CHIPS
        slot = lambda px, py: _window(refs[t], SHARD_AXIS[key], pl.multiple_of((2 * px + py) * size, 128), size)
        for k, (px, py) in enumerate(chips):
            sems = dict(send_sem=send_sems.at[3 * t + k], recv_sem=recv_sems.at[3 * t + k], device_id=(px, py, c), device_id_type=MESH)
            if outgoing:
                out.append(pltpu.make_async_remote_copy(slot(x, y), slot(x, y), **sems))
            if incoming:
                inc.append(pltpu.make_async_remote_copy(slot(x, y), slot(px, py), **sems))
    return out, inc


def _allgather_start(groups):
    names = [list(g) for g in groups]
    flat = [g[k] for g in groups for k in g]
    n, ng = len(flat), len(groups)

    def body(*refs):
        start = 0
        for gi, keys in enumerate(names):
            copies, _ = _gather_copies(keys, refs[start:start + len(keys)], refs[n + 2 * gi], refs[n + 2 * gi + 1], incoming=False)
            for cp in copies:
                cp.start()
            start += len(keys)
        token = refs[-1]
        token[...] = jnp.zeros_like(token)

    sem_shapes = [pltpu.SemaphoreType.DMA((3 * len(keys),)) for keys in names for _ in (0, 1)]
    outs = pl.pallas_call(
        body, name="allgather_start", in_specs=[HBM] * n,
        out_specs=[SEM] * (2 * ng) + [HBM] * n + [pl.BlockSpec(memory_space=pltpu.VMEM)],
        out_shape=sem_shapes + [pltpu.HBM(a.shape, a.dtype) for a in flat] + [_sds((8, HEAD_DIM), F32)],
        input_output_aliases={t: 2 * ng + t for t in range(n)},
        compiler_params=SPLIT_COPY,
    )(*[_in_hbm(a) for a in flat])
    states, start = [], 2 * ng
    for gi, keys in enumerate(names):
        states.append((gi, keys, outs[2 * gi], outs[2 * gi + 1], outs[start:start + len(keys)]))
        start += len(keys)
    return states, outs[-1]


def _allgather_wait(state, after):
    gi, names, send_sems, recv_sems, thru = state
    n = len(names)

    def body(*refs):
        outgoing, incoming = _gather_copies(names, refs[:n], refs[n], refs[n + 1])
        for cp in outgoing:
            cp.wait_send()
        for cp in incoming:
            cp.wait_recv()

    outs = pl.pallas_call(
        body, name=f"allgather_wait_{gi}", in_specs=[HBM] * n + [SEM, SEM, ANY], out_specs=[HBM] * n,
        out_shape=[pltpu.HBM(t.shape, t.dtype) for t in thru],
        input_output_aliases={t: t for t in range(n)},
        compiler_params=SPLIT_COPY,
    )(*thru, send_sems, recv_sems, after)
    return dict(zip(names, outs))


FLIPS = [(a, b, d) for a in (0, 1) for b in (0, 1) for d in (0, 1) if (a, b, d) != (0, 0, 0)]


def _flip(pos, f):
    return tuple(1 - p if fi else p for p, fi in zip(pos, f))


def _piece_shape(weight, shape):
    out = list(shape)
    out[SHARD_AXIS[weight]] //= N_CHIPS
    out[1] //= 2
    return tuple(out)


def _piece(ref, weight, chip, core):
    piece = _piece_shape(weight, ref.shape)
    if SHARD_AXIS[weight] == 1:
        return _window(ref, 1, pl.multiple_of((2 * chip + core) * piece[1], 8), piece[1])
    return _window(_window(ref, 2, pl.multiple_of(chip * piece[2], 128), piece[2]), 1, pl.multiple_of(core * piece[1], 8), piece[1])


def _scatter_copies(weight, grad_ref, land_ref, send_sems, recv_sems, *, outgoing=True, incoming=True):
    pos = _mesh_pos()
    out, inc = [], []
    for k, f in enumerate(FLIPS):
        peer = _flip(pos, f)
        sems = dict(send_sem=send_sems.at[k], recv_sem=recv_sems.at[k], device_id=peer, device_id_type=MESH)
        if outgoing:
            out.append(pltpu.make_async_remote_copy(_piece(grad_ref, weight, 2 * peer[0] + peer[1], peer[2]), land_ref.at[k], **sems))
        if incoming:
            inc.append(pltpu.make_async_remote_copy(_piece(grad_ref, weight, 2 * pos[0] + pos[1], pos[2]), land_ref.at[k], **sems))
    return out, inc


def _scatter_start(tag, weight, grad):
    n_peer = len(FLIPS)
    land = lax.empty((n_peer,) + _piece_shape(weight, grad.shape), grad.dtype)

    def body(grad_ref, land_ref, send_sems, recv_sems, grad_thru, land_thru, token):
        copies, _ = _scatter_copies(weight, grad_ref, land_ref, send_sems, recv_sems, incoming=False)
        for cp in copies:
            cp.start()
        token[...] = jnp.zeros_like(token)

    outs = pl.pallas_call(
        body, name=f"scatter_start_{tag}", in_specs=[HBM, HBM],
        out_specs=[SEM, SEM, HBM, HBM, pl.BlockSpec(memory_space=pltpu.VMEM)],
        out_shape=[pltpu.SemaphoreType.DMA((n_peer,)), pltpu.SemaphoreType.DMA((n_peer,)),
                   pltpu.HBM(grad.shape, grad.dtype), pltpu.HBM(land.shape, land.dtype), _sds((8, HEAD_DIM), F32)],
        input_output_aliases={0: 2, 1: 3},
        compiler_params=SPLIT_COPY,
    )(_in_hbm(grad), _in_hbm(land))
    return (tag, weight, outs[:4]), outs[4]


def _scatter_wait(state, after):
    tag, weight, (send_sems, recv_sems, grad_thru, land_thru) = state

    def body(grad_ref, land_ref, send_ref, recv_ref, after_ref, grad_out, land_out):
        outgoing, incoming = _scatter_copies(weight, grad_ref, land_ref, send_ref, recv_ref)
        for cp in outgoing:
            cp.wait_send()
        for cp in incoming:
            cp.wait_recv()

    return pl.pallas_call(
        body, name=f"scatter_wait_{tag}", in_specs=[HBM, HBM, SEM, SEM, ANY], out_specs=[HBM, HBM],
        out_shape=[pltpu.HBM(grad_thru.shape, grad_thru.dtype), pltpu.HBM(land_thru.shape, land_thru.dtype)],
        input_output_aliases={0: 0, 1: 1},
        compiler_params=SPLIT_COPY,
    )(grad_thru, land_thru, send_sems, recv_sems, after)


def _sum_pieces(tag, weight, grad, land, where, *, tr=256):
    _, R, C = _piece_shape(weight, grad.shape)
    tr = min(tr, R)
    nr = R // tr

    def body(where_ref, g_ref, l_ref, o_ref):
        acc = g_ref[...].astype(F32)
        for s in range(len(FLIPS)):
            acc = acc + l_ref[s].astype(F32)
        o_ref[...] = acc

    if SHARD_AXIS[weight] == 1:
        own = lambda i, where_ref: (0, (2 * where_ref[0] + where_ref[1]) * nr + i, 0)
    else:
        own = lambda i, where_ref: (0, where_ref[1] * nr + i, where_ref[0])
    grid_spec = pltpu.PrefetchScalarGridSpec(
        num_scalar_prefetch=1, grid=(nr,),
        in_specs=[pl.BlockSpec((1, tr, C), own), pl.BlockSpec((len(FLIPS), 1, tr, C), lambda i, where_ref: (0, 0, i, 0))],
        out_specs=pl.BlockSpec((1, tr, C), lambda i, where_ref: (0, where_ref[1] * nr + i, 0)))
    return pl.pallas_call(body, name=f"sum_{tag}", grid_spec=grid_spec, out_shape=_sds((1, 2 * R, C), F32),
                          compiler_params=_params("parallel"))(where, grad, land)


def _exchange_halves(shards):
    names = list(shards)
    n = len(names)
    half_sizes = [shards[k].shape[1] // 2 for k in names]

    def body(*refs):
        outs = refs[n:2 * n]
        send_sems, recv_sems = refs[2 * n:]
        x, y, c = _mesh_pos()
        half = lambda t, core: _window(outs[t], 1, pl.multiple_of(core * half_sizes[t], 8), half_sizes[t])
        sends = []
        for t in range(n):
            sends.append(pltpu.make_async_remote_copy(half(t, c), half(t, c), send_sems.at[t], recv_sems.at[t],
                                                      device_id=(x, y, 1 - c), device_id_type=MESH))
            sends[-1].start()
        for t in range(n):
            pltpu.make_async_remote_copy(half(t, c), half(t, 1 - c), send_sems.at[t], recv_sems.at[t],
                                         device_id=(x, y, 1 - c), device_id_type=MESH).wait_recv()
        for cp in sends:
            cp.wait_send()

    outs = pl.pallas_call(
        body, name="exchange_halves", in_specs=[ANY] * n, out_specs=[ANY] * n,
        out_shape=[_sds(shards[k].shape, F32) for k in names],
        input_output_aliases={t: t for t in range(n)},
        scratch_shapes=[pltpu.SemaphoreType.DMA((n,)), pltpu.SemaphoreType.DMA((n,))],
    )(*[shards[k] for k in names])
    return dict(zip(names, outs))


def _adamw_math(w, g, m, v):
    m = ADAM_B1 * m + (1.0 - ADAM_B1) * g
    v = ADAM_B2 * v + (1.0 - ADAM_B2) * jnp.square(g)
    m_hat = m / (1.0 - ADAM_B1 ** ADAM_STEP)
    v_hat = v / (1.0 - ADAM_B2 ** ADAM_STEP)
    return -ADAM_LR * (m_hat / (jnp.sqrt(v_hat) + ADAM_EPS) + ADAM_WD * w), m, v


def _adamw(name, w, g, m, v, *, tr=512):
    R, C = w.shape
    tr = min(tr, R)

    def body(w_ref, g_ref, m_ref, v_ref, d_ref, mo_ref, vo_ref):
        d_ref[...], mo_ref[...], vo_ref[...] = _adamw_math(w_ref[...], g_ref[...], m_ref[...], v_ref[...])

    spec = pl.BlockSpec((tr, C), lambda i: (i, 0))
    return pl.pallas_call(body, name=name, grid=(R // tr,), in_specs=[spec] * 4, out_specs=[spec] * 3,
                          out_shape=[_sds((R, C), F32)] * 3, compiler_params=_params("parallel"))(w, g, m, v)


SLAB_ROWS = 8
LOSS_ROW = 7


def _pack_small(d):
    pad = lambda a, width: jnp.pad(a.reshape(-1), (0, width - a.size))
    row5 = jnp.concatenate([d["swa_q_norm"].reshape(-1), d["swa_k_norm"].reshape(-1), d["ax_q_norm"].reshape(-1),
                            d["ax_k_norm"].reshape(-1), pad(d["swa_sink"], HEAD_DIM), pad(d["ret_decay_logit"], HEAD_DIM),
                            jnp.zeros((2 * HEAD_DIM,), F32)])
    return jnp.concatenate([d["norm_mix"], d["norm_mlp"], d["ret_norm"], row5[None], pad(d["t5_table"], D_MODEL)[None],
                            jnp.zeros((1, D_MODEL), F32)], axis=0)


def _unpack_small(slab):
    r5 = slab[5]
    return {
        "norm_mix": slab[0:2], "norm_mlp": slab[2:4], "ret_norm": slab[4:5],
        "swa_q_norm": r5[None, 0:128], "swa_k_norm": r5[None, 128:256], "ax_q_norm": r5[None, 256:384],
        "ax_k_norm": r5[None, 384:512], "swa_sink": r5[None, 512:512 + SWA_HEADS],
        "ret_decay_logit": r5[640:640 + 2 * RET_HEADS].reshape(1, 2, RET_HEADS),
        "t5_table": slab[6, :T5_BUCKETS * SWA_HEADS].reshape(T5_BUCKETS, SWA_HEADS),
    }


def _small_allreduce_adamw(g_slab, w_slab, m_slab, v_slab, loss_part):
    def body(g_ref, w_ref, m_ref, v_ref, lp_ref, go_ref, d_ref, mo_ref, vo_ref, gath, send_sems, recv_sems):
        pos = _mesh_pos()
        ident = lambda p: 4 * p[0] + 2 * p[1] + p[2]
        me = ident(pos)
        row = lax.broadcasted_iota(jnp.int32, (SLAB_ROWS, D_MODEL), 0)
        lane = lax.broadcasted_iota(jnp.int32, (SLAB_ROWS, D_MODEL), 1)
        loss = jnp.sum(jnp.sum(lp_ref[...], axis=0, keepdims=True), axis=1, keepdims=True) * (0.5 / D_MODEL)
        gath[me] = jnp.where(jnp.logical_and(row == LOSS_ROW, lane == 0), loss, g_ref[...])
        sends = []
        for k, f in enumerate(FLIPS):
            sends.append(pltpu.make_async_remote_copy(gath.at[me], gath.at[me], send_sems.at[k], recv_sems.at[k],
                                                      device_id=_flip(pos, f), device_id_type=MESH))
            sends[-1].start()
        for k, f in enumerate(FLIPS):
            peer = _flip(pos, f)
            pltpu.make_async_remote_copy(gath.at[me], gath.at[ident(peer)], send_sems.at[k], recv_sems.at[k],
                                         device_id=peer, device_id_type=MESH).wait_recv()
        for cp in sends:
            cp.wait_send()
        total = gath[0]
        for s in range(1, N_DEV):
            total = total + gath[s]
        go_ref[...] = total
        d_ref[...], mo_ref[...], vo_ref[...] = _adamw_math(w_ref[...], total, m_ref[...], v_ref[...])

    vmem = pl.BlockSpec(memory_space=pltpu.VMEM)
    return pl.pallas_call(
        body, name="small_allreduce_adamw", in_specs=[vmem] * 5, out_specs=[vmem] * 4,
        out_shape=[_sds((SLAB_ROWS, D_MODEL), F32)] * 4,
        scratch_shapes=[pltpu.VMEM((N_DEV, SLAB_ROWS, D_MODEL), F32),
                        pltpu.SemaphoreType.DMA((len(FLIPS),)), pltpu.SemaphoreType.DMA((len(FLIPS),))],
    )(g_slab, w_slab, m_slab, v_slab, loss_part)


def kernel(x, norm_mix, norm_mlp, w_in_even, w_out_even, ret_decay_logit, ret_norm, swa_q_norm, swa_k_norm, swa_sink, t5_table, w_in_odd, w_out_odd, ax_q_norm, ax_k_norm, w_mlp_up, w_mlp_down, loss_target, m_norm_mix, m_norm_mlp, m_w_in_even, m_w_out_even, m_ret_decay_logit, m_ret_norm, m_swa_q_norm, m_swa_k_norm, m_swa_sink, m_t5_table, m_w_in_odd, m_w_out_odd, m_ax_q_norm, m_ax_k_norm, m_w_mlp_up, m_w_mlp_down, v_norm_mix, v_norm_mlp, v_w_in_even, v_w_out_even, v_ret_decay_logit, v_ret_norm, v_swa_q_norm, v_swa_k_norm, v_swa_sink, v_t5_table, v_w_in_odd, v_w_out_odd, v_ax_q_norm, v_ax_k_norm, v_w_mlp_up, v_w_mlp_down):
    w = dict(zip(WEIGHTS, (norm_mix, norm_mlp, w_in_even, w_out_even, ret_decay_logit, ret_norm, swa_q_norm, swa_k_norm,
                           swa_sink, t5_table, w_in_odd, w_out_odd, ax_q_norm, ax_k_norm, w_mlp_up, w_mlp_down)))
    m = dict(zip(WEIGHTS, (m_norm_mix, m_norm_mlp, m_w_in_even, m_w_out_even, m_ret_decay_logit, m_ret_norm, m_swa_q_norm,
                           m_swa_k_norm, m_swa_sink, m_t5_table, m_w_in_odd, m_w_out_odd, m_ax_q_norm, m_ax_k_norm,
                           m_w_mlp_up, m_w_mlp_down)))
    v = dict(zip(WEIGHTS, (v_norm_mix, v_norm_mlp, v_w_in_even, v_w_out_even, v_ret_decay_logit, v_ret_norm, v_swa_q_norm,
                           v_swa_k_norm, v_swa_sink, v_t5_table, v_w_in_odd, v_w_out_odd, v_ax_q_norm, v_ax_k_norm,
                           v_w_mlp_up, v_w_mlp_down)))
    flat = lambda a: a.reshape(-1, a.shape[-1])

    chip = (2 * lax.axis_index("x") + lax.axis_index("y")).astype(jnp.int32)
    where = jnp.stack([chip, lax.axis_index("c").astype(jnp.int32)])

    placed = {k: _cast_place(k, w[k], where[0:1]) for k in BIG}
    gather, gather_token = _allgather_start([{k: placed[k] for k in group} for group in GATHER_ORDER])
    gathered = {}

    def fetch(name, after):
        if name not in gathered:
            state = gather[[name in group for group in GATHER_ORDER].index(True)]
            gathered.update(_allgather_wait(state, gather_token if after is None else after))
        return gathered[name] if name.startswith("w_mlp") else gathered[name][0]

    in_flight = []

    def push(tag, weight, dw):
        state, token = _scatter_start(tag, weight, dw)
        in_flight.append(state)
        return token

    loss_part, dx, small_g = _local_step(x[0], loss_target[0], {k: w[k] for k in SMALL}, fetch, push)

    halves = {}
    for state in in_flight:
        tag, weight = state[0], state[1]
        dw, land = _scatter_wait(state, dx)
        halves[tag] = _sum_pieces(tag, weight, dw, land, where)
    reduced = _exchange_halves(halves)
    grad = {"w_in_even": reduced["in_even"], "w_out_even": reduced["out_even"],
            "w_in_odd": reduced["in_odd"], "w_out_odd": reduced["out_odd"],
            "w_mlp_up": jnp.concatenate([reduced["mlp_up0"], reduced["mlp_up1"]], axis=0),
            "w_mlp_down": jnp.concatenate([reduced["mlp_down0"], reduced["mlp_down1"]], axis=0)}
    delta, new_m, new_v = {}, {}, {}
    for k in BIG:
        d_k, m_k, v_k = _adamw(f"adamw_{k}", flat(w[k]), flat(grad[k]), flat(m[k]), flat(v[k]))
        delta[k], new_m[k], new_v[k] = d_k.reshape(w[k].shape), m_k.reshape(w[k].shape), v_k.reshape(w[k].shape)

    slabs = _small_allreduce_adamw(_pack_small(small_g), _pack_small({k: w[k] for k in SMALL}),
                                   _pack_small({k: m[k] for k in SMALL}), _pack_small({k: v[k] for k in SMALL}), loss_part)
    loss = slabs[0][LOSS_ROW, 0]
    for out, slab in zip((grad, delta, new_m, new_v), slabs):
        out.update(_unpack_small(slab))

    return (loss, dx[None], *[grad[k] for k in WEIGHTS], *[delta[k] for k in WEIGHTS],
            *[new_m[k] for k in WEIGHTS], *[new_v[k] for k in WEIGHTS])
```

```python
import functools
import math

import jax
import jax.numpy as jnp
from jax import lax
from jax.experimental import pallas as pl
from jax.experimental.pallas import tpu as pltpu

F32 = jnp.float32
BF16 = jnp.bfloat16
_MXU = BF16
_WIRE = BF16

D_MODEL = 1024
HEAD_DIM = 128
EPS = 1e-6
NEG_INF = -1e30
CHUNK = 128
RET_CHUNKS_PER_STEP = 8
GRID_W = 64
RET_HEADS, RET_DK, RET_DV = 4, 128, 256
RET_Q, RET_V = RET_HEADS * RET_DK, RET_HEADS * RET_DV
RET_THETA = 10000.0
SWA_HEADS, SWA_KV_HEADS = 8, 2
T5_BUCKETS, T5_MAX_DIST = 32, 128
AX_HEADS, AX_KV_HEADS = 8, 2
AX_THETA = 10000.0
D_FF = 4 * D_MODEL
EVEN_IN = 2 * RET_Q + 2 * RET_V + D_MODEL + 2 * SWA_KV_HEADS * HEAD_DIM
ODD_IN = D_MODEL + 2 * AX_KV_HEADS * HEAD_DIM
ATT_SCALE = HEAD_DIM ** -0.5
SCORE_SCALE_LOG2 = ATT_SCALE * math.log2(math.e)

ADAM_LR, ADAM_B1, ADAM_B2, ADAM_EPS, ADAM_WD, ADAM_STEP = 0.001, 0.9, 0.999, 1e-08, 0.01, 10

N_DEV = 8
VMEM_LIMIT_BYTES = 56 << 20
MESH = pl.DeviceIdType.MESH

_NN = (((1,), (0,)), ((), ()))
_NT = (((1,), (1,)), ((), ()))
_TN = (((0,), (0,)), ((), ()))


def _dot(a, b, dn=_NN):
    return lax.dot_general(a.astype(_MXU), b.astype(_MXU), dn, preferred_element_type=F32)


def _params(*sem):
    return pltpu.CompilerParams(dimension_semantics=sem, vmem_limit_bytes=VMEM_LIMIT_BYTES)


def _sds(shape, dtype):
    return jax.ShapeDtypeStruct(tuple(shape), dtype)


def _rowsum8(x):
    return jnp.sum(x.reshape(x.shape[0] // 8, 8, x.shape[1]), axis=0)


def _swap_halves(x, half):
    width = x.shape[1]
    lane = lax.broadcasted_iota(jnp.int32, x.shape, 1)
    up = pltpu.roll(x, width - half, axis=1)
    down = pltpu.roll(x, half, axis=1)
    return jnp.where((lane & (2 * half - 1)) < half, up, down)


def _sigmoid(x):
    return 1.0 / (1.0 + jnp.exp(-x))


def _weight_spec(w, block, index_map):
    if isinstance(w, tuple):
        stacked, layer = w
        return stacked, pl.BlockSpec((None,) + block, lambda *idx: (layer,) + index_map(*idx))
    return w, pl.BlockSpec(block, index_map)


def _weight_dims(w):
    return (w[0] if isinstance(w, tuple) else w).shape[-2:]


def _norm_matmul(name, x, gain, w, *, tm, tn, out_dtype):
    T, K = x.shape
    N = _weight_dims(w)[1]
    tm, tn = min(tm, T), min(tn, N)
    w, w_spec = _weight_spec(w, (K, tn), lambda i, j: (0, j))

    def body(x_ref, g_ref, w_ref, y_ref, h_ref, h_sc):
        @pl.when(pl.program_id(1) == 0)
        def _():
            xv = x_ref[...]
            r = lax.rsqrt(jnp.mean(xv * xv, axis=-1, keepdims=True) + EPS)
            h = (xv * r * g_ref[...]).astype(_MXU)
            h_sc[...] = h
            h_ref[...] = h
        y_ref[...] = jnp.dot(h_sc[...], w_ref[...], preferred_element_type=F32).astype(y_ref.dtype)

    return pl.pallas_call(
        body, name=name, grid=(T // tm, N // tn),
        in_specs=[pl.BlockSpec((tm, K), lambda i, j: (i, 0)),
                  pl.BlockSpec((1, K), lambda i, j: (0, 0)),
                  w_spec],
        out_specs=[pl.BlockSpec((tm, tn), lambda i, j: (i, j)),
                   pl.BlockSpec((tm, K), lambda i, j: (i, 0))],
        out_shape=[_sds((T, N), out_dtype), _sds((T, K), _MXU)],
        scratch_shapes=[pltpu.VMEM((tm, K), _MXU)],
        compiler_params=_params("parallel", "arbitrary"),
    )(x, gain, w)


def _matmul_res(name, a_list, w, res, *, tm, relu2=False, target=None):
    T = res.shape[0]
    N = _weight_dims(w)[1]
    K = a_list[0].shape[1]
    n_a = len(a_list)
    tm = min(tm, T)
    with_loss = target is not None
    w_specs = [_weight_spec(w, (K, N), functools.partial(lambda i, b: (b, 0), b=b)) for b in range(n_a)]

    def body(*refs):
        a_refs = refs[:n_a]
        w_refs = refs[n_a:2 * n_a]
        res_ref = refs[2 * n_a]
        acc = res_ref[...]
        for a_ref, w_ref in zip(a_refs, w_refs):
            a = a_ref[...]
            if relu2:
                a = jnp.square(jnp.maximum(a.astype(F32), 0.0))
            acc = acc + _dot(a, w_ref[...])
        if with_loss:
            tgt_ref, g_ref, g16_ref, loss_ref = refs[2 * n_a + 1:]
            diff = acc - tgt_ref[...]
            g = diff * (1.0 / N)
            g_ref[...] = g
            g16_ref[...] = g.astype(g16_ref.dtype)

            @pl.when(pl.program_id(0) == 0)
            def _():
                loss_ref[...] = jnp.zeros_like(loss_ref)
            loss_ref[...] += _rowsum8(diff * diff)
        else:
            refs[2 * n_a + 1][...] = acc

    row = lambda i: (i, 0)
    in_specs = [pl.BlockSpec((tm, K), row) for _ in a_list]
    in_specs += [spec for _, spec in w_specs]
    in_specs += [pl.BlockSpec((tm, N), row)]
    args = list(a_list) + [arr for arr, _ in w_specs] + [res]
    if with_loss:
        in_specs.append(pl.BlockSpec((tm, N), row))
        args.append(target)
        out_specs = [pl.BlockSpec((tm, N), row), pl.BlockSpec((tm, N), row), pl.BlockSpec((8, N), lambda i: (0, 0))]
        out_shape = [_sds((T, N), F32), _sds((T, N), _MXU), _sds((8, N), F32)]
        sem = "arbitrary"
    else:
        out_specs = pl.BlockSpec((tm, N), row)
        out_shape = _sds((T, N), F32)
        sem = "parallel"
    return pl.pallas_call(body, name=name, grid=(T // tm,), in_specs=in_specs, out_specs=out_specs,
                          out_shape=out_shape, compiler_params=_params(sem))(*args)


def _matmul_nt(name, a, w, *, tm, tn, out_dtype, relu_of=None):
    T, K = a.shape
    N = _weight_dims(w)[0]
    tm, tn = min(tm, T), min(tn, N)
    w, w_spec = _weight_spec(w, (tn, K), lambda i, j: (j, 0))

    def body(*refs):
        if relu_of is None:
            a_ref, w_ref, o_ref = refs
            o_ref[...] = _dot(a_ref[...], w_ref[...], _NT).astype(o_ref.dtype)
        else:
            a_ref, w_ref, u_ref, o_ref = refs
            da = _dot(a_ref[...], w_ref[...], _NT)
            o_ref[...] = (da * (2.0 * jnp.maximum(u_ref[...].astype(F32), 0.0))).astype(o_ref.dtype)

    in_specs = [pl.BlockSpec((tm, K), lambda i, j: (i, 0)), w_spec]
    args = [a, w]
    if relu_of is not None:
        in_specs.append(pl.BlockSpec((tm, tn), lambda i, j: (i, j)))
        args.append(relu_of)
    return pl.pallas_call(body, name=name, grid=(T // tm, N // tn), in_specs=in_specs,
                          out_specs=pl.BlockSpec((tm, tn), lambda i, j: (i, j)),
                          out_shape=_sds((T, N), out_dtype),
                          compiler_params=_params("parallel", "parallel"))(*args)


def _matmul_nt_normbwd(name, dy, w, x, gain, dres, *, tm):
    T, K = dy.shape
    N = _weight_dims(w)[0]
    tm = min(tm, T)
    w, w_spec = _weight_spec(w, (N, K), lambda i: (0, 0))

    def body(dy_ref, w_ref, x_ref, g_ref, dres_ref, dx_ref, dx16_ref, dg_ref):
        dh = _dot(dy_ref[...], w_ref[...], _NT)
        xv = x_ref[...]
        r = lax.rsqrt(jnp.mean(xv * xv, axis=-1, keepdims=True) + EPS)
        xhat = xv * r
        dxhat = dh * g_ref[...]
        dx = dres_ref[...] + r * (dxhat - xhat * jnp.mean(dxhat * xhat, axis=-1, keepdims=True))
        dx_ref[...] = dx
        dx16_ref[...] = dx.astype(dx16_ref.dtype)

        @pl.when(pl.program_id(0) == 0)
        def _():
            dg_ref[...] = jnp.zeros_like(dg_ref)
        dg_ref[...] += _rowsum8(dh * xhat)

    row = lambda i: (i, 0)
    return pl.pallas_call(
        body, name=name, grid=(T // tm,),
        in_specs=[pl.BlockSpec((tm, K), row), w_spec,
                  pl.BlockSpec((tm, N), row), pl.BlockSpec((1, N), lambda i: (0, 0)), pl.BlockSpec((tm, N), row)],
        out_specs=[pl.BlockSpec((tm, N), row), pl.BlockSpec((tm, N), row), pl.BlockSpec((8, N), lambda i: (0, 0))],
        out_shape=[_sds((T, N), F32), _sds((T, N), _MXU), _sds((8, N), F32)],
        compiler_params=_params("arbitrary"),
    )(dy, w, x, gain, dres)


def _matmul_tn(name, a, b, *, tk, tn, tt, out_dtype, relu2=False):
    T, Ka = a.shape
    Nb = b.shape[1]
    tk, tn, tt = min(tk, Ka), min(tn, Nb), min(tt, T)
    nt = T // tt

    def body(a_ref, b_ref, o_ref, acc):
        t = pl.program_id(2)

        @pl.when(t == 0)
        def _():
            acc[...] = jnp.zeros_like(acc)
        av = a_ref[...]
        if relu2:
            av = jnp.square(jnp.maximum(av.astype(F32), 0.0))
        acc[...] += _dot(av, b_ref[...], _TN)

        @pl.when(t == nt - 1)
        def _():
            o_ref[...] = acc[...].astype(o_ref.dtype)

    return pl.pallas_call(
        body, name=name, grid=(Ka // tk, Nb // tn, nt),
        in_specs=[pl.BlockSpec((tt, tk), lambda i, j, t: (t, i)), pl.BlockSpec((tt, tn), lambda i, j, t: (t, j))],
        out_specs=pl.BlockSpec((tk, tn), lambda i, j, t: (i, j)),
        out_shape=_sds((Ka, Nb), out_dtype),
        scratch_shapes=[pltpu.VMEM((tk, tn), F32)],
        compiler_params=_params("parallel", "parallel", "arbitrary"),
    )(a, b)


def _rope_angles(pos, dim, theta):
    inv = theta ** (-jnp.arange(0, dim, 2, dtype=F32) / dim)
    return pos.astype(F32)[:, None] * inv[None, :]


def _ret_rope_tables(T):
    ang = _rope_angles(jnp.arange(T), RET_DK, RET_THETA)
    c, s = jnp.cos(ang), jnp.sin(ang)
    return jnp.concatenate([c, c], axis=1), jnp.concatenate([-s, s], axis=1)


def _axial_rope_tables(T):
    rows = T // GRID_W
    ar = _rope_angles(jnp.arange(rows), HEAD_DIM // 2, AX_THETA)
    ac = _rope_angles(jnp.arange(GRID_W), HEAD_DIM // 2, AX_THETA)
    by_row = lambda a: jnp.repeat(a, GRID_W, axis=0)
    by_col = lambda a: jnp.tile(a, (rows, 1))
    cos = jnp.concatenate([by_row(jnp.cos(ar)), by_row(jnp.cos(ar)), by_col(jnp.cos(ac)), by_col(jnp.cos(ac))], axis=1)
    sin = jnp.concatenate([by_row(-jnp.sin(ar)), by_row(jnp.sin(ar)), by_col(-jnp.sin(ac)), by_col(jnp.sin(ac))], axis=1)
    return cos, sin


(TAB_D, TAB_DT, TAB_EF, TAB_EB, TAB_A, TAB_B, TAB_CF, TAB_CB,
 TAB_RA, TAB_RB, TAB_RCF, TAB_RCB, TAB_KF, TAB_KB) = range(14)


def _retention_tables(decay_logit):
    lg = jax.nn.log_sigmoid(decay_logit.astype(F32))
    lam, mu = lg[0][:, None, None], lg[1][:, None, None]
    idx = jnp.arange(CHUNK, dtype=F32)
    diff = (idx[:, None] - idx[None, :])[None]
    df = jnp.where(diff >= 0, jnp.exp(jnp.maximum(diff, 0.0) * lam), 0.0)
    db = jnp.where(diff < 0, jnp.exp(jnp.maximum(-diff, 0.0) * mu), 0.0)
    d = df + db
    r = idx[None, :, None]
    ones = jnp.ones((1, 1, CHUNK), F32)
    a = jnp.exp((r + 1.0) * lam) * ones
    b = jnp.exp((CHUNK - r) * mu) * ones
    cf = jnp.exp((CHUNK - 1.0 - r) * lam) * ones
    cb = jnp.exp(r * mu) * ones
    full = jnp.ones((1, CHUNK, CHUNK), F32)
    kf = CHUNK * jnp.exp(CHUNK * lam) * full
    kb = CHUNK * jnp.exp(CHUNK * mu) * full
    tabs = jnp.stack([d, jnp.swapaxes(d, 1, 2), diff * df, -diff * db, a, b, cf, cb,
                      (r + 1.0) * a, (CHUNK - r) * b, (CHUNK - 1.0 - r) * cf, r * cb, kf, kb], axis=1)

    def lanes(tab):
        return jnp.transpose(tab, (1, 0, 2)).reshape(CHUNK, RET_HEADS * CHUNK)

    def dec(l):
        return jnp.exp(CHUNK * l)[:, 0, :] * jnp.ones((1, RET_DV), F32)

    weights = dict(a=lanes(a), b=lanes(b), cf=lanes(cf), cb=lanes(cb), dec_f=dec(lam), dec_b=dec(mu))
    return tabs, weights, lg


def _t5_bucket(rel):
    nb = T5_BUCKETS // 2
    max_exact = nb // 2
    ret = jnp.where(rel > 0, nb, 0)
    n = jnp.abs(rel)
    nf = jnp.maximum(n, 1).astype(F32)
    large = max_exact + (jnp.log(nf / max_exact) / math.log(T5_MAX_DIST / max_exact)
                         * (nb - max_exact)).astype(jnp.int32)
    large = jnp.minimum(large, nb - 1)
    return ret + jnp.where(n < max_exact, n, large)


def _swa_rel():
    r = jnp.arange(CHUNK)
    j = jnp.arange(3 * CHUNK)
    return j[None, :] - CHUNK - r[:, None]


def _swa_bias(t5_table):
    rel = _swa_rel()
    bucket = jnp.where(jnp.abs(rel) <= CHUNK, _t5_bucket(rel), -1).astype(jnp.int32)

    def body(tab_ref, bk_ref, o_ref):
        bk = bk_ref[...]
        for h in range(SWA_HEADS):
            pick = lambda b, acc, h=h: jnp.where(bk == b, tab_ref[b, h], acc)
            o_ref[h] = lax.fori_loop(0, T5_BUCKETS, pick, jnp.full(bk.shape, NEG_INF, F32))

    return pl.pallas_call(
        body, name="t5_bias",
        in_specs=[pl.BlockSpec(memory_space=pltpu.SMEM), pl.BlockSpec(memory_space=pltpu.VMEM)],
        out_specs=pl.BlockSpec(memory_space=pltpu.VMEM),
        out_shape=_sds((SWA_HEADS, CHUNK, 3 * CHUNK), F32),
    )(t5_table.astype(F32), bucket)


def _prep_even(proj, cos, sin, q_gain, k_gain, *, tm):
    T = proj.shape[0]
    tm = min(tm, T)

    def body(qa_ref, ka_ref, qb_ref, kb_ref, c_ref, s_ref, qg_ref, kg_ref, qr_ref, kr_ref, qn_ref, kn_ref):
        c = jnp.concatenate([c_ref[...]] * RET_HEADS, axis=1)
        s = jnp.concatenate([s_ref[...]] * RET_HEADS, axis=1)
        qa = qa_ref[...]
        qr_ref[...] = (qa * c + _swap_halves(qa, RET_DK // 2) * s).astype(qr_ref.dtype)
        ka = ka_ref[...]
        kr_ref[...] = ((ka * c + _swap_halves(ka, RET_DK // 2) * s) * (RET_DK ** -0.5)).astype(kr_ref.dtype)
        for src, gain, dst, heads in ((qb_ref, qg_ref, qn_ref, SWA_HEADS), (kb_ref, kg_ref, kn_ref, SWA_KV_HEADS)):
            for h in range(heads):
                sl = slice(h * HEAD_DIM, (h + 1) * HEAD_DIM)
                xh = src[:, sl]
                r = lax.rsqrt(jnp.mean(xh * xh, axis=-1, keepdims=True) + EPS)
                dst[:, sl] = (xh * r * gain[...]).astype(dst.dtype)

    row = lambda i: (i, 0)
    const = lambda i: (0, 0)
    return pl.pallas_call(
        body, name="prep_even", grid=(T // tm,),
        in_specs=[pl.BlockSpec((tm, RET_Q), lambda i: (i, 0)), pl.BlockSpec((tm, RET_Q), lambda i: (i, 1)),
                  pl.BlockSpec((tm, D_MODEL), lambda i: (i, 3)), pl.BlockSpec((tm, 256), lambda i: (i, 16)),
                  pl.BlockSpec((tm, RET_DK), row), pl.BlockSpec((tm, RET_DK), row),
                  pl.BlockSpec((1, HEAD_DIM), const), pl.BlockSpec((1, HEAD_DIM), const)],
        out_specs=[pl.BlockSpec((tm, RET_Q), row), pl.BlockSpec((tm, RET_Q), row),
                   pl.BlockSpec((tm, D_MODEL), row), pl.BlockSpec((tm, 256), row)],
        out_shape=[_sds((T, RET_Q), _MXU), _sds((T, RET_Q), _MXU), _sds((T, D_MODEL), _MXU), _sds((T, 256), _MXU)],
        compiler_params=_params("parallel"),
    )(proj, proj, proj, proj, cos, sin, q_gain, k_gain)


def _ret_scan(name, x, y, y_col, w_asc, dec_asc, w_desc, dec_desc):
    T = x.shape[0]
    nc = T // CHUNK
    per = min(RET_CHUNKS_PER_STEP, nc)
    nb = nc // per
    rows_per = per * CHUNK

    def body(xa_ref, ya_ref, xd_ref, yd_ref, wa_ref, da_ref, wd_ref, dd_ref, sa_out, sd_out, sa, sd):
        @pl.when(pl.program_id(0) == 0)
        def _():
            sa[...] = jnp.zeros_like(sa)
            sd[...] = jnp.zeros_like(sd)
        for step in range(per):
            for c, x_ref, y_ref, w_ref, d_ref, st, out in ((step, xa_ref, ya_ref, wa_ref, da_ref, sa, sa_out),
                                                       (per - 1 - step, xd_ref, yd_ref, wd_ref, dd_ref, sd, sd_out)):
                rows = slice(c * CHUNK, (c + 1) * CHUNK)
                out[c] = st[...].astype(out.dtype)
                for h in range(RET_HEADS):
                    ks = slice(h * RET_DK, (h + 1) * RET_DK)
                    vs = slice(h * RET_DV, (h + 1) * RET_DV)
                    u = _dot(x_ref[rows, ks].astype(F32) * w_ref[:, ks], y_ref[rows, vs], _TN)
                    st[ks, :] = st[ks, :] * d_ref[h:h + 1, :] + u

    asc = lambda i: (i, 0)
    desc = lambda i: (nb - 1 - i, 0)
    const = lambda i: (0, 0)
    return pl.pallas_call(
        body, name=name, grid=(nb,),
        in_specs=[pl.BlockSpec((rows_per, RET_Q), asc), pl.BlockSpec((rows_per, RET_V), lambda i: (i, y_col)),
                  pl.BlockSpec((rows_per, RET_Q), desc), pl.BlockSpec((rows_per, RET_V), lambda i: (nb - 1 - i, y_col)),
                  pl.BlockSpec((CHUNK, RET_Q), const), pl.BlockSpec((RET_HEADS, RET_DV), const),
                  pl.BlockSpec((CHUNK, RET_Q), const), pl.BlockSpec((RET_HEADS, RET_DV), const)],
        out_specs=[pl.BlockSpec((per, RET_Q, RET_DV), lambda i: (i, 0, 0)),
                   pl.BlockSpec((per, RET_Q, RET_DV), lambda i: (nb - 1 - i, 0, 0))],
        out_shape=[_sds((nc, RET_Q, RET_DV), _MXU), _sds((nc, RET_Q, RET_DV), _MXU)],
        scratch_shapes=[pltpu.VMEM((RET_Q, RET_DV), F32), pltpu.VMEM((RET_Q, RET_DV), F32)],
        compiler_params=_params("arbitrary"),
    )(x, y, x, y, w_asc, dec_asc, w_desc, dec_desc)


def _ret_out(qr, kr, proj, sf, sb, tabs, gain):
    T = qr.shape[0]
    nc = T // CHUNK
    per = min(RET_CHUNKS_PER_STEP, nc)
    rows_per = per * CHUNK

    def body(q_ref, k_ref, v_ref, g_ref, sf_ref, sb_ref, tab_ref, gain_ref, o_ref, y_ref):
        for c in range(per):
            rows = slice(c * CHUNK, (c + 1) * CHUNK)
            for h in range(RET_HEADS):
                ks = slice(h * RET_DK, (h + 1) * RET_DK)
                vs = slice(h * RET_DV, (h + 1) * RET_DV)
                q, k, v = q_ref[rows, ks], k_ref[rows, ks], v_ref[rows, vs]
                qf = q.astype(F32)
                a_mat = _dot(q, k, _NT) * tab_ref[h, 0]
                o = (_dot(a_mat, v) + _dot(qf * tab_ref[h, 1], sf_ref[c, ks, :]) + _dot(qf * tab_ref[h, 2], sb_ref[c, ks, :]))
                o_ref[rows, vs] = o
                r = lax.rsqrt(jnp.mean(o * o, axis=-1, keepdims=True) + EPS)
                g = g_ref[rows, vs]
                y_ref[rows, vs] = (g * _sigmoid(g) * (o * r * gain_ref[:, vs])).astype(y_ref.dtype)

    row = lambda i: (i, 0)
    st = lambda i: (i, 0, 0)
    return pl.pallas_call(
        body, name="ret_out", grid=(nc // per,),
        in_specs=[pl.BlockSpec((rows_per, RET_Q), row), pl.BlockSpec((rows_per, RET_Q), row),
                  pl.BlockSpec((rows_per, RET_V), lambda i: (i, 1)), pl.BlockSpec((rows_per, RET_V), lambda i: (i, 2)),
                  pl.BlockSpec((per, RET_Q, RET_DV), st), pl.BlockSpec((per, RET_Q, RET_DV), st),
                  pl.BlockSpec((RET_HEADS, 3, CHUNK, CHUNK), lambda i: (0, 0, 0, 0)),
                  pl.BlockSpec((1, RET_V), lambda i: (0, 0))],
        out_specs=[pl.BlockSpec((rows_per, RET_V), row), pl.BlockSpec((rows_per, RET_V), row)],
        out_shape=[_sds((T, RET_V), F32), _sds((T, RET_V), _MXU)],
        compiler_params=_params("parallel"),
    )(qr, kr, proj, proj, sf, sb, tabs, gain)


def _ret_gate_bwd(dycat, proj, ret_o, gain, *, tm):
    T = ret_o.shape[0]
    tm = min(tm, T)

    def body(dy_ref, g_ref, o_ref, gain_ref, do_ref, dg_ref, dgain_ref):
        @pl.when(pl.program_id(0) == 0)
        def _():
            dgain_ref[...] = jnp.zeros_like(dgain_ref)
        for h in range(RET_HEADS):
            vs = slice(h * RET_DV, (h + 1) * RET_DV)
            o, g, dya, gn = o_ref[:, vs], g_ref[:, vs], dy_ref[:, vs], gain_ref[:, vs]
            r = lax.rsqrt(jnp.mean(o * o, axis=-1, keepdims=True) + EPS)
            ohat = o * r
            sg = _sigmoid(g)
            dy = dya * (g * sg)
            dg_ref[:, vs] = (dya * (ohat * gn) * (sg * (1.0 + g * (1.0 - sg)))).astype(dg_ref.dtype)
            dyg = dy * gn
            do_ref[:, vs] = (r * (dyg - ohat * jnp.mean(dyg * ohat, axis=-1, keepdims=True))).astype(do_ref.dtype)
            dgain_ref[:, vs] += _rowsum8(dy * ohat)

    row = lambda i: (i, 0)
    return pl.pallas_call(
        body, name="ret_gate_bwd", grid=(T // tm,),
        in_specs=[pl.BlockSpec((tm, RET_V), row), pl.BlockSpec((tm, RET_V), lambda i: (i, 2)),
                  pl.BlockSpec((tm, RET_V), row), pl.BlockSpec((1, RET_V), lambda i: (0, 0))],
        out_specs=[pl.BlockSpec((tm, RET_V), row), pl.BlockSpec((tm, RET_V), row), pl.BlockSpec((8, RET_V), lambda i: (0, 0))],
        out_shape=[_sds((T, RET_V), _MXU), _sds((T, RET_V), _MXU), _sds((8, RET_V), F32)],
        compiler_params=_params("arbitrary"),
    )(dycat, proj, ret_o, gain)


def _ret_bwd(qr, kr, proj, g_out, sf, sb, rf, rb, tabs):
    T = qr.shape[0]
    nc = T // CHUNK
    per = min(RET_CHUNKS_PER_STEP, nc)
    rows_per = per * CHUNK

    def body(q_ref, k_ref, v_ref, g_ref, sf_ref, sb_ref, rf_ref, rb_ref, tab_ref, dq_ref, dk_ref, dv_ref, dl_ref):
        @pl.when(pl.program_id(0) == 0)
        def _():
            dl_ref[...] = jnp.zeros_like(dl_ref)
        for c in range(per):
            rows = slice(c * CHUNK, (c + 1) * CHUNK)
            for h in range(RET_HEADS):
                ks = slice(h * RET_DK, (h + 1) * RET_DK)
                vs = slice(h * RET_DV, (h + 1) * RET_DV)
                q, k, v, g = q_ref[rows, ks], k_ref[rows, ks], v_ref[rows, vs], g_ref[rows, vs]
                s_f, s_b, r_f, r_b = sf_ref[c, ks, :], sb_ref[c, ks, :], rf_ref[c, ks, :], rb_ref[c, ks, :]
                tab = lambda t, h=h: tab_ref[h, t]
                qf, kf = q.astype(F32), k.astype(F32)
                qk = _dot(q, k, _NT)
                da_raw = _dot(g, v, _NT)
                x_f, x_b = _dot(g, s_f, _NT), _dot(g, s_b, _NT)
                dq_ref[rows, ks] = _dot(da_raw * tab(TAB_D), k) + tab(TAB_A) * x_f + tab(TAB_B) * x_b
                at = _dot(k, q, _NT) * tab(TAB_DT)
                dat = _dot(v, g, _NT) * tab(TAB_DT)
                y_f, y_b = _dot(v, r_f, _NT), _dot(v, r_b, _NT)
                dk_ref[rows, ks] = _dot(dat, q) + tab(TAB_CF) * y_f + tab(TAB_CB) * y_b
                dv_ref[rows, vs] = (_dot(at, g) + _dot(kf * tab(TAB_CF), r_f) + _dot(kf * tab(TAB_CB), r_b)).astype(dv_ref.dtype)
                inner = da_raw * qk
                rs_f = r_f.astype(F32) * s_f.astype(F32)
                rs_b = r_b.astype(F32) * s_b.astype(F32)
                dl_f = (inner * tab(TAB_EF) + tab(TAB_RA) * qf * x_f + tab(TAB_RCF) * kf * y_f
                        + tab(TAB_KF) * (rs_f[:, :CHUNK] + rs_f[:, CHUNK:]))
                dl_b = (inner * tab(TAB_EB) + tab(TAB_RB) * qf * x_b + tab(TAB_RCB) * kf * y_b
                        + tab(TAB_KB) * (rs_b[:, :CHUNK] + rs_b[:, CHUNK:]))
                dl_ref[2 * h:2 * h + 1, :] += jnp.sum(dl_f, axis=0, keepdims=True)
                dl_ref[2 * h + 1:2 * h + 2, :] += jnp.sum(dl_b, axis=0, keepdims=True)

    row = lambda i: (i, 0)
    st = lambda i: (i, 0, 0)
    return pl.pallas_call(
        body, name="ret_bwd", grid=(nc // per,),
        in_specs=[pl.BlockSpec((rows_per, RET_Q), row), pl.BlockSpec((rows_per, RET_Q), row),
                  pl.BlockSpec((rows_per, RET_V), lambda i: (i, 1)), pl.BlockSpec((rows_per, RET_V), row),
                  pl.BlockSpec((per, RET_Q, RET_DV), st), pl.BlockSpec((per, RET_Q, RET_DV), st),
                  pl.BlockSpec((per, RET_Q, RET_DV), st), pl.BlockSpec((per, RET_Q, RET_DV), st),
                  pl.BlockSpec((RET_HEADS, 14, CHUNK, CHUNK), lambda i: (0, 0, 0, 0))],
        out_specs=[pl.BlockSpec((rows_per, RET_Q), row), pl.BlockSpec((rows_per, RET_Q), row),
                   pl.BlockSpec((rows_per, RET_V), row), pl.BlockSpec((8, CHUNK), lambda i: (0, 0))],
        out_shape=[_sds((T, RET_Q), F32), _sds((T, RET_Q), F32), _sds((T, RET_V), _MXU), _sds((8, CHUNK), F32)],
        compiler_params=_params("arbitrary"),
    )(qr, kr, proj, g_out, sf, sb, rf, rb, tabs)


SWA_GROUP = SWA_HEADS // SWA_KV_HEADS
SWA_COLS = SWA_GROUP * CHUNK


def _swa_stack(ref, g):
    return jnp.concatenate([ref[:, h * HEAD_DIM:(h + 1) * HEAD_DIM] for h in range(g * SWA_GROUP, (g + 1) * SWA_GROUP)], axis=0)


def _swa_probs_t(q, k_win, bias_t, sink_row, i, nb):
    st = _dot(k_win, q, _NT) * ATT_SCALE + bias_t
    key = lax.broadcasted_iota(jnp.int32, st.shape, 0)
    valid = jnp.logical_and(jnp.logical_or(key >= CHUNK, i > 0), jnp.logical_or(key < 2 * CHUNK, i < nb - 1))
    st = jnp.where(valid, st, NEG_INF)
    m = jnp.maximum(jnp.max(st, axis=0, keepdims=True), sink_row)
    p = jnp.exp(st - m)
    e_sink = jnp.exp(sink_row - m)
    inv = 1.0 / (jnp.sum(p, axis=0, keepdims=True) + e_sink)
    return p * inv, e_sink * inv


def _swa_layouts(bias, sink):
    bias_t = bias.reshape(SWA_KV_HEADS, SWA_GROUP, CHUNK, 3 * CHUNK).transpose(0, 3, 1, 2).reshape(SWA_KV_HEADS, 3 * CHUNK, SWA_COLS)
    return bias_t, jnp.repeat(sink[:, 0], CHUNK).reshape(SWA_KV_HEADS, SWA_COLS)


def _swa_window_specs(nb, width, col_block, clamp):
    prev = lambda i: (jnp.maximum(clamp(i) - 1, 0), col_block)
    cur = lambda i: (clamp(i), col_block)
    nxt = lambda i: (jnp.minimum(clamp(i) + 1, nb - 1), col_block)
    return [pl.BlockSpec((CHUNK, width), f) for f in (prev, cur, nxt)]


def _swa_fwd(qn, kn, proj, bias, sink):
    T = qn.shape[0]
    nb = T // CHUNK
    kvw = SWA_KV_HEADS * HEAD_DIM
    bias_t, sink_rows = _swa_layouts(bias, sink)

    def body(q_ref, k0, k1, k2, v0, v1, v2, bias_ref, sink_ref, y_ref):
        i = pl.program_id(0)
        for g in range(SWA_KV_HEADS):
            gs = slice(g * HEAD_DIM, (g + 1) * HEAD_DIM)
            k_win = jnp.concatenate([k0[:, gs], k1[:, gs], k2[:, gs]], axis=0)
            v_win = jnp.concatenate([v0[:, gs], v1[:, gs], v2[:, gs]], axis=0).astype(_MXU)
            pt, _ = _swa_probs_t(_swa_stack(q_ref, g), k_win, bias_ref[g], sink_ref[g:g + 1, :], i, nb)
            o = _dot(v_win, pt, _TN).T
            for hh in range(SWA_GROUP):
                h = g * SWA_GROUP + hh
                y_ref[:, h * HEAD_DIM:(h + 1) * HEAD_DIM] = o[hh * CHUNK:(hh + 1) * CHUNK].astype(y_ref.dtype)

    ident = lambda i: i
    return pl.pallas_call(
        body, name="swa_fwd", grid=(nb,),
        in_specs=[pl.BlockSpec((CHUNK, D_MODEL), lambda i: (i, 0))]
        + _swa_window_specs(nb, kvw, 0, ident) + _swa_window_specs(nb, kvw, 17, ident)
        + [pl.BlockSpec((SWA_KV_HEADS, 3 * CHUNK, SWA_COLS), lambda i: (0, 0, 0)), pl.BlockSpec((SWA_KV_HEADS, SWA_COLS), lambda i: (0, 0))],
        out_specs=pl.BlockSpec((CHUNK, D_MODEL), lambda i: (i, 0)),
        out_shape=_sds((T, D_MODEL), _MXU),
        compiler_params=_params("parallel"),
    )(qn, kn, kn, kn, proj, proj, proj, bias_t, sink_rows)


def _swa_bwd(qn, kn, proj, dycat, bias, sink):
    T = qn.shape[0]
    nb = T // CHUNK
    kvw = SWA_KV_HEADS * HEAD_DIM
    bias_t, sink_rows = _swa_layouts(bias, sink)

    def body(q_ref, k0, k1, k2, v0, v1, v2, dy_ref, bias_ref, sink_ref,
             dq_ref, dk_ref, dv_ref, dbias_ref, dsink_ref, acc_a, acc_b):
        i = pl.program_id(0)

        @pl.when(i == 0)
        def _():
            dbias_ref[...] = jnp.zeros_like(dbias_ref)
            dsink_ref[...] = jnp.zeros_like(dsink_ref)
            acc_a[...] = jnp.zeros_like(acc_a)
            acc_b[...] = jnp.zeros_like(acc_b)

        @pl.when(i < nb)
        def _():
            for g in range(SWA_KV_HEADS):
                gs = slice(g * HEAD_DIM, (g + 1) * HEAD_DIM)
                k_win = jnp.concatenate([k0[:, gs], k1[:, gs], k2[:, gs]], axis=0)
                v_win = jnp.concatenate([v0[:, gs], v1[:, gs], v2[:, gs]], axis=0).astype(_MXU)
                q, dy = _swa_stack(q_ref, g), _swa_stack(dy_ref, g)
                pt, p_sink = _swa_probs_t(q, k_win, bias_ref[g], sink_ref[g:g + 1, :], i, nb)
                dpt = _dot(v_win, dy, _NT)
                delta = jnp.sum(pt * dpt, axis=0, keepdims=True)
                dst = pt * (dpt - delta)
                dbias_ref[g] += dst
                dsink_ref[g:g + 1, :] += -p_sink * delta
                dq = (_dot(k_win, dst, _TN) * ATT_SCALE).T
                for hh in range(SWA_GROUP):
                    h = g * SWA_GROUP + hh
                    dq_ref[:, h * HEAD_DIM:(h + 1) * HEAD_DIM] = dq[hh * CHUNK:(hh + 1) * CHUNK]
                dk_win = _dot(dst, q) * ATT_SCALE
                dv_win = _dot(pt, dy)
                for win, out_ref, col0 in ((dk_win, dk_ref, 0), (dv_win, dv_ref, kvw)):
                    cs = slice(col0 + g * HEAD_DIM, col0 + (g + 1) * HEAD_DIM)
                    out_ref[:, gs] = acc_a[:, cs] + win[:CHUNK]
                    acc_a[:, cs] = acc_b[:, cs] + win[CHUNK:2 * CHUNK]
                    acc_b[:, cs] = win[2 * CHUNK:]

        @pl.when(i == nb)
        def _():
            dk_ref[...] = acc_a[:, :kvw]
            dv_ref[...] = acc_a[:, kvw:]

    clamp = lambda i: jnp.minimum(i, nb - 1)
    late = lambda i: (jnp.maximum(i - 1, 0), 0)
    bias_spec = pl.BlockSpec((SWA_KV_HEADS, 3 * CHUNK, SWA_COLS), lambda i: (0, 0, 0))
    sink_spec = pl.BlockSpec((SWA_KV_HEADS, SWA_COLS), lambda i: (0, 0))
    dq, dk, dv, dbias_t, dsink_rows = pl.pallas_call(
        body, name="swa_bwd", grid=(nb + 1,),
        in_specs=[pl.BlockSpec((CHUNK, D_MODEL), lambda i: (clamp(i), 0))]
        + _swa_window_specs(nb, kvw, 0, clamp) + _swa_window_specs(nb, kvw, 17, clamp)
        + [pl.BlockSpec((CHUNK, D_MODEL), lambda i: (clamp(i), 1)), bias_spec, sink_spec],
        out_specs=[pl.BlockSpec((CHUNK, D_MODEL), lambda i: (clamp(i), 0)),
                   pl.BlockSpec((CHUNK, kvw), late), pl.BlockSpec((CHUNK, kvw), late), bias_spec, sink_spec],
        out_shape=[_sds((T, D_MODEL), F32), _sds((T, kvw), F32), _sds((T, kvw), F32),
                   _sds((SWA_KV_HEADS, 3 * CHUNK, SWA_COLS), F32), _sds((SWA_KV_HEADS, SWA_COLS), F32)],
        scratch_shapes=[pltpu.VMEM((CHUNK, 2 * kvw), F32), pltpu.VMEM((CHUNK, 2 * kvw), F32)],
        compiler_params=_params("arbitrary"),
    )(qn, kn, kn, kn, proj, proj, proj, dycat, bias_t, sink_rows)
    dbias = dbias_t.reshape(SWA_KV_HEADS, 3 * CHUNK, SWA_GROUP, CHUNK).transpose(0, 2, 3, 1).reshape(SWA_HEADS, CHUNK, 3 * CHUNK)
    dsink = jnp.sum(dsink_rows.reshape(SWA_HEADS, CHUNK), axis=1, keepdims=True) * jnp.ones((1, HEAD_DIM), F32)
    return dq, dk, dv, dbias, dsink


def _t5_bucket_reduce(dbias, bucket):
    def body(db_ref, bk_ref, o_ref):
        bk = bk_ref[...]
        row = lax.broadcasted_iota(jnp.int32, (SWA_HEADS, HEAD_DIM), 0)
        lane = lax.broadcasted_iota(jnp.int32, (SWA_HEADS, HEAD_DIM), 1)

        def per_bucket(b, acc):
            mask = bk == b
            for h in range(SWA_HEADS):
                tot = jnp.sum(jnp.sum(jnp.where(mask, db_ref[h], 0.0), axis=0, keepdims=True), axis=1, keepdims=True)
                acc = acc + jnp.where(jnp.logical_and(row == h, lane == b), tot, 0.0)
            return acc

        o_ref[...] = lax.fori_loop(0, T5_BUCKETS, per_bucket, jnp.zeros((SWA_HEADS, HEAD_DIM), F32))

    return pl.pallas_call(body, name="t5_bucket_reduce", out_shape=_sds((SWA_HEADS, HEAD_DIM), F32),
                          compiler_params=pltpu.CompilerParams(vmem_limit_bytes=VMEM_LIMIT_BYTES))(dbias, bucket)


def _headnorm_bwd(x, dy, gain):
    r = lax.rsqrt(jnp.mean(x * x, axis=-1, keepdims=True) + EPS)
    xhat = x * r
    dyg = dy * gain
    return r * (dyg - xhat * jnp.mean(dyg * xhat, axis=-1, keepdims=True)), dy * xhat


def _post_even(proj, dqr, dkr, dva, dga, dqn, dkn, dvb, cos, sin, q_gain, k_gain, *, tm):
    T = proj.shape[0]
    tm = min(tm, T)
    kvw = SWA_KV_HEADS * HEAD_DIM

    def body(qb_ref, kb_ref, dqr_ref, dkr_ref, dva_ref, dga_ref, dqn_ref, dkn_ref, dvb_ref, c_ref, s_ref, qg_ref, kg_ref,
             dp_ref, dqg_ref, dkg_ref):
        @pl.when(pl.program_id(0) == 0)
        def _():
            dqg_ref[...] = jnp.zeros_like(dqg_ref)
            dkg_ref[...] = jnp.zeros_like(dkg_ref)
        c = jnp.concatenate([c_ref[...]] * RET_HEADS, axis=1)
        s = jnp.concatenate([s_ref[...]] * RET_HEADS, axis=1)
        dq = dqr_ref[...]
        dp_ref[:, 0:RET_Q] = (dq * c + _swap_halves(dq * s, RET_DK // 2)).astype(dp_ref.dtype)
        dk = dkr_ref[...] * (RET_DK ** -0.5)
        dp_ref[:, RET_Q:2 * RET_Q] = (dk * c + _swap_halves(dk * s, RET_DK // 2)).astype(dp_ref.dtype)
        off = 2 * RET_Q
        dp_ref[:, off:off + RET_V] = dva_ref[...].astype(dp_ref.dtype)
        dp_ref[:, off + RET_V:off + 2 * RET_V] = dga_ref[...].astype(dp_ref.dtype)
        off += 2 * RET_V
        for src, dsrc, gain, dgain, heads, base in ((qb_ref, dqn_ref, qg_ref, dqg_ref, SWA_HEADS, off),
                                                    (kb_ref, dkn_ref, kg_ref, dkg_ref, SWA_KV_HEADS, off + D_MODEL)):
            for h in range(heads):
                sl = slice(h * HEAD_DIM, (h + 1) * HEAD_DIM)
                dx, dgx = _headnorm_bwd(src[:, sl], dsrc[:, sl], gain[...])
                dp_ref[:, base + h * HEAD_DIM:base + (h + 1) * HEAD_DIM] = dx.astype(dp_ref.dtype)
                dgain[...] += _rowsum8(dgx)
        dp_ref[:, off + D_MODEL + kvw:] = dvb_ref[...].astype(dp_ref.dtype)

    row = lambda i: (i, 0)
    const = lambda i: (0, 0)
    return pl.pallas_call(
        body, name="post_even", grid=(T // tm,),
        in_specs=[pl.BlockSpec((tm, D_MODEL), lambda i: (i, 3)), pl.BlockSpec((tm, kvw), lambda i: (i, 16)),
                  pl.BlockSpec((tm, RET_Q), row), pl.BlockSpec((tm, RET_Q), row),
                  pl.BlockSpec((tm, RET_V), row), pl.BlockSpec((tm, RET_V), row),
                  pl.BlockSpec((tm, D_MODEL), row), pl.BlockSpec((tm, kvw), row), pl.BlockSpec((tm, kvw), row),
                  pl.BlockSpec((tm, RET_DK), row), pl.BlockSpec((tm, RET_DK), row),
                  pl.BlockSpec((1, HEAD_DIM), const), pl.BlockSpec((1, HEAD_DIM), const)],
        out_specs=[pl.BlockSpec((tm, EVEN_IN), row), pl.BlockSpec((8, HEAD_DIM), const), pl.BlockSpec((8, HEAD_DIM), const)],
        out_shape=[_sds((T, EVEN_IN), _MXU), _sds((8, HEAD_DIM), F32), _sds((8, HEAD_DIM), F32)],
        compiler_params=_params("arbitrary"),
    )(proj, proj, dqr, dkr, dva, dga, dqn, dkn, dvb, cos, sin, q_gain, k_gain)


def _prep_odd(proj, cos, sin, q_gain, k_gain, *, tm):
    T = proj.shape[0]
    tm = min(tm, T)
    kvw = AX_KV_HEADS * HEAD_DIM

    def body(q_ref, k_ref, v_ref, c_ref, s_ref, qg_ref, kg_ref, qx_ref, kx_ref, vx_ref):
        c, s = c_ref[...], s_ref[...]
        for src, gain, dst, heads, scale in ((q_ref, qg_ref, qx_ref, AX_HEADS, SCORE_SCALE_LOG2), (k_ref, kg_ref, kx_ref, AX_KV_HEADS, 1.0)):
            for h in range(heads):
                sl = slice(h * HEAD_DIM, (h + 1) * HEAD_DIM)
                xh = src[:, sl]
                r = lax.rsqrt(jnp.mean(xh * xh, axis=-1, keepdims=True) + EPS)
                xn = xh * r * gain[...]
                dst[:, sl] = ((xn * c + _swap_halves(xn, HEAD_DIM // 4) * s) * scale).astype(dst.dtype)
        vx_ref[...] = v_ref[...].astype(vx_ref.dtype)

    row = lambda i: (i, 0)
    const = lambda i: (0, 0)
    return pl.pallas_call(
        body, name="prep_odd", grid=(T // tm,),
        in_specs=[pl.BlockSpec((tm, D_MODEL), row), pl.BlockSpec((tm, kvw), lambda i: (i, 4)), pl.BlockSpec((tm, kvw), lambda i: (i, 5)),
                  pl.BlockSpec((tm, HEAD_DIM), row), pl.BlockSpec((tm, HEAD_DIM), row),
                  pl.BlockSpec((1, HEAD_DIM), const), pl.BlockSpec((1, HEAD_DIM), const)],
        out_specs=[pl.BlockSpec((tm, D_MODEL), row), pl.BlockSpec((tm, kvw), row), pl.BlockSpec((tm, kvw), row)],
        out_shape=[_sds((T, D_MODEL), _MXU), _sds((T, kvw), _MXU), _sds((T, kvw), _MXU)],
        compiler_params=_params("parallel"),
    )(proj, proj, proj, cos, sin, q_gain, k_gain)


def _post_odd(proj, dqxt, dkx, dvx, cos, sin, q_gain, k_gain, *, tm):
    T = proj.shape[0]
    tm = min(tm, T)
    kvw = AX_KV_HEADS * HEAD_DIM

    def body(q_ref, k_ref, dqt_ref, dk_ref, dv_ref, c_ref, s_ref, qg_ref, kg_ref, dp_ref, dqg_ref, dkg_ref):
        @pl.when(pl.program_id(0) == 0)
        def _():
            dqg_ref[...] = jnp.zeros_like(dqg_ref)
            dkg_ref[...] = jnp.zeros_like(dkg_ref)
        c, s = c_ref[...], s_ref[...]
        for src, dsrc, gain, dgain, heads, base in ((q_ref, dqt_ref, qg_ref, dqg_ref, AX_HEADS, 0),
                                                    (k_ref, dk_ref, kg_ref, dkg_ref, AX_KV_HEADS, D_MODEL)):
            for h in range(heads):
                sl = slice(h * HEAD_DIM, (h + 1) * HEAD_DIM)
                d = dsrc[sl, :].T if dsrc is dqt_ref else dsrc[:, sl]
                dn = d * c + _swap_halves(d * s, HEAD_DIM // 4)
                dx, dgx = _headnorm_bwd(src[:, sl], dn, gain[...])
                dp_ref[:, base + h * HEAD_DIM:base + (h + 1) * HEAD_DIM] = dx.astype(dp_ref.dtype)
                dgain[...] += _rowsum8(dgx)
        dp_ref[:, D_MODEL + kvw:] = dv_ref[...].astype(dp_ref.dtype)

    row = lambda i: (i, 0)
    const = lambda i: (0, 0)
    return pl.pallas_call(
        body, name="post_odd", grid=(T // tm,),
        in_specs=[pl.BlockSpec((tm, D_MODEL), row), pl.BlockSpec((tm, kvw), lambda i: (i, 4)),
                  pl.BlockSpec((D_MODEL, tm), lambda i: (0, i)), pl.BlockSpec((tm, kvw), row), pl.BlockSpec((tm, kvw), row),
                  pl.BlockSpec((tm, HEAD_DIM), row), pl.BlockSpec((tm, HEAD_DIM), row),
                  pl.BlockSpec((1, HEAD_DIM), const), pl.BlockSpec((1, HEAD_DIM), const)],
        out_specs=[pl.BlockSpec((tm, ODD_IN), row), pl.BlockSpec((8, HEAD_DIM), const), pl.BlockSpec((8, HEAD_DIM), const)],
        out_shape=[_sds((T, ODD_IN), _MXU), _sds((8, HEAD_DIM), F32), _sds((8, HEAD_DIM), F32)],
        compiler_params=_params("arbitrary"),
    )(proj, proj, dqxt, dkx, dvx, cos, sin, q_gain, k_gain)


ONES_ROWS = 16


def _flash_fwd(qx, kx, vx, *, tq, tk):
    v1t = jnp.concatenate([vx.T.reshape(AX_KV_HEADS, HEAD_DIM, vx.shape[0]),
                           jnp.ones((AX_KV_HEADS, ONES_ROWS, vx.shape[0]), vx.dtype)], axis=1)
    T = qx.shape[0]
    tq, tk = min(tq, T), min(tk, T)
    nq, nk = T // tq, T // tk
    group = AX_HEADS // AX_KV_HEADS

    def body(k_ref, v_ref, q_ref, o_ref, lse_ref, acc_sc, m_sc, l_sc):
        j = pl.program_id(2)

        @pl.when(j == 0)
        def _():
            m_sc[...] = jnp.full(m_sc.shape, NEG_INF, F32)
            l_sc[...] = jnp.zeros_like(l_sc)
            acc_sc[...] = jnp.zeros_like(acc_sc)
        k, v = k_ref[...], v_ref[0]

        def step(i, carry):
            cols = pl.ds(pl.multiple_of(i * tq, tq), tq)
            st = _dot(k, q_ref[cols, :], _NT)
            m_old = m_sc[i]
            m_new = jnp.maximum(m_old, jnp.max(st, axis=0, keepdims=True))
            p = jnp.exp2(st - m_new)
            alpha = jnp.exp2(m_old - m_new)
            pv = _dot(v, p)
            m_sc[i] = m_new
            l_sc[i] = alpha * l_sc[i] + pv[HEAD_DIM:HEAD_DIM + 1]
            acc_sc[:, cols] = alpha * acc_sc[:, cols] + pv[:HEAD_DIM]
            return carry

        lax.fori_loop(0, nq, step, 0)

        @pl.when(j == nk - 1)
        def _():
            def finish(i, carry):
                cols = pl.ds(pl.multiple_of(i * tq, tq), tq)
                o_ref[cols, :] = (acc_sc[:, cols] / l_sc[i]).T.astype(o_ref.dtype)
                lse_ref[0, i] = m_sc[i] + jnp.log2(l_sc[i])
                return carry

            lax.fori_loop(0, nq, finish, 0)

    kv = lambda g, h, j: (j, g)
    qh = lambda g, h, j: (0, g * group + h)
    o, lse = pl.pallas_call(
        body, name="flash_fwd", grid=(AX_KV_HEADS, group, nk),
        in_specs=[pl.BlockSpec((tk, HEAD_DIM), kv), pl.BlockSpec((1, HEAD_DIM + ONES_ROWS, tk), lambda g, h, j: (g, 0, j)),
                  pl.BlockSpec((T, HEAD_DIM), qh)],
        out_specs=[pl.BlockSpec((T, HEAD_DIM), qh), pl.BlockSpec((1, nq, 1, tq), lambda g, h, j: (g * group + h, 0, 0, 0))],
        out_shape=[_sds((T, D_MODEL), _MXU), _sds((AX_HEADS, nq, 1, tq), F32)],
        scratch_shapes=[pltpu.VMEM((HEAD_DIM, T), F32), pltpu.VMEM((nq, 1, tq), F32), pltpu.VMEM((nq, 1, tq), F32)],
        compiler_params=_params("parallel", "arbitrary", "arbitrary"),
    )(kx, v1t, qx)
    return o, lse.reshape(AX_HEADS, 1, T)


def _flash_bwd(qx, kx, vx, o, do, lse, *, tq, tk):
    T = qx.shape[0]
    tq, tk = min(tq, T), min(tk, T)
    nq = T // tq
    group = AX_HEADS // AX_KV_HEADS
    lse_rows = lse.reshape(AX_HEADS, nq, 1, tq)
    kxt = kx.T.reshape(AX_KV_HEADS, HEAD_DIM, T)

    def body(k_ref, kt_ref, v_ref, q_ref, o_ref, do_ref, lse_ref, dqt_ref, dk_ref, dv_ref, delta_sc):
        j = pl.program_id(2)

        @pl.when(jnp.logical_and(pl.program_id(1) == 0, j == 0))
        def _():
            dk_ref[...] = jnp.zeros_like(dk_ref)
            dv_ref[...] = jnp.zeros_like(dv_ref)

        @pl.when(j == 0)
        def _():
            dqt_ref[...] = jnp.zeros_like(dqt_ref)

            def row_delta(i, carry):
                rows = pl.ds(pl.multiple_of(i * tq, tq), tq)
                prod = do_ref[rows, :].astype(F32) * o_ref[rows, :].astype(F32)
                delta_sc[i] = jnp.sum(prod.T, axis=0, keepdims=True)
                return carry

            lax.fori_loop(0, nq, row_delta, 0)
        k, v = k_ref[...], v_ref[...]

        def step(i, carry):
            dk, dv = carry
            off = pl.multiple_of(i * tq, tq)
            q, do_blk = q_ref[pl.ds(off, tq), :], do_ref[pl.ds(off, tq), :]
            pt = jnp.exp2(_dot(k, q, _NT) - lse_ref[0, i])
            dst = pt * (_dot(v, do_blk, _NT) - delta_sc[i])
            dqt_ref[:, pl.ds(off, tq)] += _dot(kt_ref[0], dst) * ATT_SCALE
            return dk + _dot(dst, q), dv + _dot(pt, do_blk)

        zero = jnp.zeros((tk, HEAD_DIM), F32)
        dk, dv = lax.fori_loop(0, nq, step, (zero, zero))
        rows = pl.ds(pl.multiple_of(j * tk, tk), tk)
        dk_ref[rows, :] += dk * (ATT_SCALE / SCORE_SCALE_LOG2)
        dv_ref[rows, :] += dv

    kv = lambda g, h, j: (j, g)
    qh = lambda g, h, j: (0, g * group + h)
    st = lambda g, h, j: (g * group + h, 0, 0, 0)
    acc = lambda g, h, j: (0, g)
    return pl.pallas_call(
        body, name="flash_bwd", grid=(AX_KV_HEADS, group, T // tk),
        in_specs=[pl.BlockSpec((tk, HEAD_DIM), kv), pl.BlockSpec((1, HEAD_DIM, tk), lambda g, h, j: (g, 0, j)),
                  pl.BlockSpec((tk, HEAD_DIM), kv),
                  pl.BlockSpec((T, HEAD_DIM), qh), pl.BlockSpec((T, HEAD_DIM), qh), pl.BlockSpec((T, HEAD_DIM), qh),
                  pl.BlockSpec((1, nq, 1, tq), st)],
        out_specs=[pl.BlockSpec((HEAD_DIM, T), lambda g, h, j: (g * group + h, 0)),
                   pl.BlockSpec((T, HEAD_DIM), acc), pl.BlockSpec((T, HEAD_DIM), acc)],
        out_shape=[_sds((D_MODEL, T), F32), _sds((T, AX_KV_HEADS * HEAD_DIM), F32), _sds((T, AX_KV_HEADS * HEAD_DIM), F32)],
        scratch_shapes=[pltpu.VMEM((nq, 1, tq), F32)],
        compiler_params=_params("parallel", "arbitrary", "arbitrary"),
    )(kx, kxt, vx, qx, o, do, lse_rows)


TM = 1024
TM_WIDE = 512


def _mlp_fwd(tag, x, gain, fetch, target=None):
    u, h = _norm_matmul(f"mlp_up{tag}", x, gain, (fetch("w_mlp_up", x), tag), tm=TM_WIDE, tn=D_FF, out_dtype=_MXU)
    out = _matmul_res(f"mlp_down{tag}", [u], (fetch("w_mlp_down", u), tag), x, tm=TM if target is None else TM_WIDE,
                      relu2=True, target=target)
    return out, (x, u, h)


def _local_step(x, target, p, fetch, push, tokens=()):
    T = x.shape[0]
    cos_r, sin_r = _ret_rope_tables(T)
    cos_a, sin_a = _axial_rope_tables(T)
    tabs, rw, log_gamma = _retention_tables(p["ret_decay_logit"][0])
    bias = _swa_bias(p["t5_table"])
    sink = p["swa_sink"][0][:, None] * jnp.ones((1, HEAD_DIM), F32)
    nm, nl = p["norm_mix"], p["norm_mlp"]
    pending = [t for t in tokens if t is not None]

    def send(tag, weight, dw):
        token = push(tag, weight, dw[None])
        if token is not None:
            pending.append(token)

    def tied(operand):
        while pending:
            operand = operand + pending.pop()[0:1, 0:1]
        return operand

    def mlp_bwd(tag, saved, gain, dy, dy16):
        xs, u, h = saved
        w_up, w_down = (fetch("w_mlp_up", None), tag), (fetch("w_mlp_down", None), tag)
        du = _matmul_nt(f"mlp_down{tag}_bwd", dy16, w_down, tm=TM_WIDE, tn=D_FF, out_dtype=_MXU, relu_of=u)
        send(f"mlp_down{tag}", "w_mlp_down", _matmul_tn(f"mlp_down{tag}_dw", u, dy16, tk=2048, tn=1024, tt=1024, out_dtype=_WIRE, relu2=True))
        dx, dx16, dgain = _matmul_nt_normbwd(f"mlp_up{tag}_bwd", du, w_up, xs, tied(gain), dy, tm=TM_WIDE)
        send(f"mlp_up{tag}", "w_mlp_up", _matmul_tn(f"mlp_up{tag}_dw", h, du, tk=1024, tn=2048, tt=1024, out_dtype=_WIRE))
        return dx, dx16, dgain

    w_in_even = fetch("w_in_even", cos_r[:8] + cos_a[:8] + bias[0, :8, :HEAD_DIM] + tabs[0, 0, :8] + rw["cf"][:8, :HEAD_DIM])
    proj0, h0 = _norm_matmul("in_even", x, tied(nm[0:1]), w_in_even, tm=TM_WIDE, tn=EVEN_IN, out_dtype=F32)
    qr, kr, qn, kn = _prep_even(proj0, cos_r, sin_r, p["swa_q_norm"], p["swa_k_norm"], tm=TM)
    sf, sb = _ret_scan("ret_scan_fwd", kr, proj0, 1, rw["cf"], rw["dec_f"], rw["cb"], rw["dec_b"])
    ret_o, ya = _ret_out(qr, kr, proj0, sf, sb, tabs[:, (TAB_D, TAB_A, TAB_B)], p["ret_norm"])
    yb = _swa_fwd(qn, kn, proj0, bias, sink)
    w_out_even = fetch("w_out_even", yb)
    x1 = _matmul_res("out_even", [ya, yb], w_out_even, x, tm=TM)
    x2, mlp0 = _mlp_fwd(0, x1, nl[0:1], fetch)
    w_in_odd, w_out_odd = fetch("w_in_odd", x2), fetch("w_out_odd", x2)
    proj1, h1 = _norm_matmul("in_odd", x2, nm[1:2], w_in_odd, tm=TM, tn=ODD_IN, out_dtype=F32)
    qx, kx, vx = _prep_odd(proj1, cos_a, sin_a, p["ax_q_norm"], p["ax_k_norm"], tm=TM)
    o, lse = _flash_fwd(qx, kx, vx, tq=2048, tk=2048)
    x3 = _matmul_res("out_odd", [o], w_out_odd, x2, tm=TM)
    (g4, g4_16, loss_part), mlp1 = _mlp_fwd(1, x3, nl[1:2], fetch, target=target)

    dx3, dx3_16, dnl1 = mlp_bwd(1, mlp1, nl[1:2], g4, g4_16)
    do = _matmul_nt("out_odd_bwd", dx3_16, w_out_odd, tm=TM, tn=1024, out_dtype=_MXU)
    send("out_odd", "w_out_odd", _matmul_tn("out_odd_dw", o, dx3_16, tk=1024, tn=1024, tt=1024, out_dtype=_WIRE))
    dqxt, dkx, dvx = _flash_bwd(qx, kx, vx, o, do, lse, tq=2048, tk=512)
    dproj1, dqg1, dkg1 = _post_odd(proj1, dqxt, dkx, dvx, cos_a, sin_a, tied(p["ax_q_norm"]), p["ax_k_norm"], tm=TM)
    send("in_odd", "w_in_odd", _matmul_tn("in_odd_dw", h1, dproj1, tk=1024, tn=768, tt=1024, out_dtype=_WIRE))
    dx2, dx2_16, dnm1 = _matmul_nt_normbwd("in_odd_bwd", dproj1, w_in_odd, x2, tied(nm[1:2]), dx3, tm=TM_WIDE)
    dx1, dx1_16, dnl0 = mlp_bwd(0, mlp0, nl[0:1], dx2, dx2_16)
    dycat = _matmul_nt("out_even_bwd", dx1_16, w_out_even, tm=TM, tn=2 * D_MODEL, out_dtype=F32)
    send("out_even", "w_out_even", jnp.concatenate([
        _matmul_tn("out_even_dw_ret", ya, dx1_16, tk=1024, tn=1024, tt=1024, out_dtype=_WIRE),
        _matmul_tn("out_even_dw_swa", yb, dx1_16, tk=1024, tn=1024, tt=1024, out_dtype=_WIRE)], axis=0))
    g_out, dga, dretg = _ret_gate_bwd(dycat, proj0, ret_o, tied(p["ret_norm"]), tm=TM)
    rb, rf = _ret_scan("ret_scan_bwd", qr, g_out, 0, rw["b"], rw["dec_b"], rw["a"], rw["dec_f"])
    dqr, dkr, dva, dlog = _ret_bwd(qr, kr, proj0, g_out, sf, sb, rf, rb, tabs)
    dqn, dkn, dvb, dbias, dsink = _swa_bwd(qn, kn, proj0, dycat, bias, sink)
    dt5 = _t5_bucket_reduce(dbias, _t5_bucket(_swa_rel()).astype(jnp.int32))
    dproj0, dqg0, dkg0 = _post_even(proj0, dqr, dkr, dva, dga, dqn, dkn, dvb, cos_r, sin_r,
                                    p["swa_q_norm"], p["swa_k_norm"], tm=TM_WIDE)
    send("in_even", "w_in_even", _matmul_tn("in_even_dw", h0, dproj0, tk=1024, tn=2304, tt=1024, out_dtype=_WIRE))
    dx0, _, dnm0 = _matmul_nt_normbwd("in_even_bwd", dproj0, w_in_even, x, tied(nm[0:1]), dx1, tm=TM_WIDE)

    fold = lambda part: jnp.sum(part, axis=0)
    dlam = jnp.sum(dlog, axis=1).reshape(RET_HEADS, 2).T
    small = {
        "norm_mix": jnp.stack([fold(dnm0), fold(dnm1)]),
        "norm_mlp": jnp.stack([fold(dnl0), fold(dnl1)]),
        "ret_decay_logit": (dlam * (1.0 - jnp.exp(log_gamma)))[None],
        "ret_norm": fold(dretg)[None],
        "swa_q_norm": fold(dqg0)[None], "swa_k_norm": fold(dkg0)[None],
        "swa_sink": dsink[:, 0][None],
        "t5_table": dt5[:, :T5_BUCKETS].T,
        "ax_q_norm": fold(dqg1)[None], "ax_k_norm": fold(dkg1)[None],
    }
    return loss_part, dx0, small


BIG = ("w_in_even", "w_out_even", "w_in_odd", "w_out_odd", "w_mlp_up", "w_mlp_down")
SMALL = ("norm_mix", "norm_mlp", "ret_decay_logit", "ret_norm", "swa_q_norm", "swa_k_norm", "swa_sink", "t5_table",
         "ax_q_norm", "ax_k_norm")
WEIGHTS = ("norm_mix", "norm_mlp", "w_in_even", "w_out_even", "ret_decay_logit", "ret_norm", "swa_q_norm", "swa_k_norm",
           "swa_sink", "t5_table", "w_in_odd", "w_out_odd", "ax_q_norm", "ax_k_norm", "w_mlp_up", "w_mlp_down")
SHARD_AXIS = {"w_in_even": 2, "w_out_even": 1, "w_in_odd": 2, "w_out_odd": 1, "w_mlp_up": 2, "w_mlp_down": 1}
N_CHIPS = 4
GATHER_ORDER = (("w_in_even",), ("w_out_even",), ("w_mlp_up",), ("w_mlp_down",), ("w_in_odd", "w_out_odd"))
ANY = pl.BlockSpec(memory_space=pl.ANY)
HBM = pl.BlockSpec(memory_space=pltpu.HBM)
SEM = pl.BlockSpec(memory_space=pltpu.SEMAPHORE)
SPLIT_COPY = pltpu.CompilerParams(has_side_effects=pltpu.SideEffectType.DATAFLOW_SIDE_EFFECTING)


def _in_hbm(a):
    return pltpu.with_memory_space_constraint(a, pltpu.HBM)


def _mesh_pos():
    return lax.axis_index("x"), lax.axis_index("y"), lax.axis_index("c")


def _window(ref, axis, start, size):
    idx = [slice(None)] * len(ref.shape)
    idx[axis] = pl.ds(start, size)
    return ref.at[tuple(idx)]


def _cast_place(key, shard, chip, *, tr=256):
    L, R, C = shard.shape
    tr = min(tr, R)
    axis = SHARD_AXIS[key]
    whole = tuple(d * (N_CHIPS if a == axis else 1) for a, d in enumerate(shard.shape))

    def body(chip_ref, s_ref, o_ref):
        o_ref[...] = s_ref[...].astype(o_ref.dtype)

    if axis == 2:
        out_map = lambda l, i, chip_ref: (l, i, chip_ref[0])
    else:
        out_map = lambda l, i, chip_ref: (l, i + chip_ref[0] * (R // tr), 0)
    grid_spec = pltpu.PrefetchScalarGridSpec(
        num_scalar_prefetch=1, grid=(L, R // tr),
        in_specs=[pl.BlockSpec((1, tr, C), lambda l, i, chip_ref: (l, i, 0))],
        out_specs=pl.BlockSpec((1, tr, C), out_map))
    return pl.pallas_call(body, name=f"cast_place_{key}", grid_spec=grid_spec, out_shape=_sds(whole, _MXU),
                          compiler_params=_params("parallel", "parallel"))(chip, shard)


def _gather_copies(names, refs, send_sems, recv_sems, *, outgoing=True, incoming=True):
    x, y, c = _mesh_pos()
    chips = [(1 - x, y), (x, 1 - y), (1 - x, 1 - y)]
    out, inc = [], []
    for t, key in enumerate(names):
        size = refs[t].shape[SHARD_AXIS[key]] // N_CHIPS
        slot = lambda px, py: _window(refs[t], SHARD_AXIS[key], pl.multiple_of((2 * px + py) * size, 128), size)
        for k, (px, py) in enumerate(chips):
            sems = dict(send_sem=send_sems.at[3 * t + k], recv_sem=recv_sems.at[3 * t + k], device_id=(px, py, c), device_id_type=MESH)
            if outgoing:
                out.append(pltpu.make_async_remote_copy(slot(x, y), slot(x, y), **sems))
            if incoming:
                inc.append(pltpu.make_async_remote_copy(slot(x, y), slot(px, py), **sems))
    return out, inc


def _allgather_start(groups):
    names = [list(g) for g in groups]
    flat = [g[k] for g in groups for k in g]
    n, ng = len(flat), len(groups)

    def body(*refs):
        start = 0
        for gi, keys in enumerate(names):
            copies, _ = _gather_copies(keys, refs[start:start + len(keys)], refs[n + 2 * gi], refs[n + 2 * gi + 1], incoming=False)
            for cp in copies:
                cp.start()
            start += len(keys)
        token = refs[-1]
        token[...] = jnp.zeros_like(token)

    sem_shapes = [pltpu.SemaphoreType.DMA((3 * len(keys),)) for keys in names for _ in (0, 1)]
    outs = pl.pallas_call(
        body, name="allgather_start", in_specs=[HBM] * n,
        out_specs=[SEM] * (2 * ng) + [HBM] * n + [pl.BlockSpec(memory_space=pltpu.VMEM)],
        out_shape=sem_shapes + [pltpu.HBM(a.shape, a.dtype) for a in flat] + [_sds((8, HEAD_DIM), F32)],
        input_output_aliases={t: 2 * ng + t for t in range(n)},
        compiler_params=SPLIT_COPY,
    )(*[_in_hbm(a) for a in flat])
    states, start = [], 2 * ng
    for gi, keys in enumerate(names):
        states.append((gi, keys, outs[2 * gi], outs[2 * gi + 1], outs[start:start + len(keys)]))
        start += len(keys)
    return states, outs[-1]


def _allgather_wait(state, after):
    gi, names, send_sems, recv_sems, thru = state
    n = len(names)

    def body(*refs):
        outgoing, incoming = _gather_copies(names, refs[:n], refs[n], refs[n + 1])
        for cp in outgoing:
            cp.wait_send()
        for cp in incoming:
            cp.wait_recv()

    outs = pl.pallas_call(
        body, name=f"allgather_wait_{gi}", in_specs=[HBM] * n + [SEM, SEM, ANY], out_specs=[HBM] * n,
        out_shape=[pltpu.HBM(t.shape, t.dtype) for t in thru],
        input_output_aliases={t: t for t in range(n)},
        compiler_params=SPLIT_COPY,
    )(*thru, send_sems, recv_sems, after)
    return dict(zip(names, outs))


FLIPS = [(a, b, d) for a in (0, 1) for b in (0, 1) for d in (0, 1) if (a, b, d) != (0, 0, 0)]


def _flip(pos, f):
    return tuple(1 - p if fi else p for p, fi in zip(pos, f))


def _piece_shape(weight, shape):
    out = list(shape)
    out[SHARD_AXIS[weight]] //= N_CHIPS
    out[1] //= 2
    return tuple(out)


def _piece(ref, weight, chip, core):
    piece = _piece_shape(weight, ref.shape)
    if SHARD_AXIS[weight] == 1:
        return _window(ref, 1, pl.multiple_of((2 * chip + core) * piece[1], 8), piece[1])
    return _window(_window(ref, 2, pl.multiple_of(chip * piece[2], 128), piece[2]), 1, pl.multiple_of(core * piece[1], 8), piece[1])


def _scatter_copies(weight, grad_ref, land_ref, send_sems, recv_sems, *, outgoing=True, incoming=True):
    pos = _mesh_pos()
    out, inc = [], []
    for k, f in enumerate(FLIPS):
        peer = _flip(pos, f)
        sems = dict(send_sem=send_sems.at[k], recv_sem=recv_sems.at[k], device_id=peer, device_id_type=MESH)
        if outgoing:
            out.append(pltpu.make_async_remote_copy(_piece(grad_ref, weight, 2 * peer[0] + peer[1], peer[2]), land_ref.at[k], **sems))
        if incoming:
            inc.append(pltpu.make_async_remote_copy(_piece(grad_ref, weight, 2 * pos[0] + pos[1], pos[2]), land_ref.at[k], **sems))
    return out, inc


def _scatter_start(tag, weight, grad):
    n_peer = len(FLIPS)
    land = lax.empty((n_peer,) + _piece_shape(weight, grad.shape), grad.dtype)

    def body(grad_ref, land_ref, send_sems, recv_sems, grad_thru, land_thru, token):
        copies, _ = _scatter_copies(weight, grad_ref, land_ref, send_sems, recv_sems, incoming=False)
        for cp in copies:
            cp.start()
        token[...] = jnp.zeros_like(token)

    outs = pl.pallas_call(
        body, name=f"scatter_start_{tag}", in_specs=[HBM, HBM],
        out_specs=[SEM, SEM, HBM, HBM, pl.BlockSpec(memory_space=pltpu.VMEM)],
        out_shape=[pltpu.SemaphoreType.DMA((n_peer,)), pltpu.SemaphoreType.DMA((n_peer,)),
                   pltpu.HBM(grad.shape, grad.dtype), pltpu.HBM(land.shape, land.dtype), _sds((8, HEAD_DIM), F32)],
        input_output_aliases={0: 2, 1: 3},
        compiler_params=SPLIT_COPY,
    )(_in_hbm(grad), _in_hbm(land))
    return (tag, weight, outs[:4]), outs[4]


def _scatter_wait(state, after):
    tag, weight, (send_sems, recv_sems, grad_thru, land_thru) = state

    def body(grad_ref, land_ref, send_ref, recv_ref, after_ref, grad_out, land_out):
        outgoing, incoming = _scatter_copies(weight, grad_ref, land_ref, send_ref, recv_ref)
        for cp in outgoing:
            cp.wait_send()
        for cp in incoming:
            cp.wait_recv()

    return pl.pallas_call(
        body, name=f"scatter_wait_{tag}", in_specs=[HBM, HBM, SEM, SEM, ANY], out_specs=[HBM, HBM],
        out_shape=[pltpu.HBM(grad_thru.shape, grad_thru.dtype), pltpu.HBM(land_thru.shape, land_thru.dtype)],
        input_output_aliases={0: 0, 1: 1},
        compiler_params=SPLIT_COPY,
    )(grad_thru, land_thru, send_sems, recv_sems, after)


def _sum_pieces(tag, weight, grad, land, where, *, tr=256):
    _, R, C = _piece_shape(weight, grad.shape)
    tr = min(tr, R)
    nr = R // tr

    def body(where_ref, g_ref, l_ref, o_ref):
        acc = g_ref[...].astype(F32)
        for s in range(len(FLIPS)):
            acc = acc + l_ref[s].astype(F32)
        o_ref[...] = acc

    if SHARD_AXIS[weight] == 1:
        own = lambda i, where_ref: (0, (2 * where_ref[0] + where_ref[1]) * nr + i, 0)
    else:
        own = lambda i, where_ref: (0, where_ref[1] * nr + i, where_ref[0])
    grid_spec = pltpu.PrefetchScalarGridSpec(
        num_scalar_prefetch=1, grid=(nr,),
        in_specs=[pl.BlockSpec((1, tr, C), own), pl.BlockSpec((len(FLIPS), 1, tr, C), lambda i, where_ref: (0, 0, i, 0))],
        out_specs=pl.BlockSpec((1, tr, C), lambda i, where_ref: (0, where_ref[1] * nr + i, 0)))
    return pl.pallas_call(body, name=f"sum_{tag}", grid_spec=grid_spec, out_shape=_sds((1, 2 * R, C), F32),
                          compiler_params=_params("parallel"))(where, grad, land)


def _exchange_halves(shards):
    names = list(shards)
    n = len(names)
    half_sizes = [shards[k].shape[1] // 2 for k in names]

    def body(*refs):
        outs = refs[n:2 * n]
        send_sems, recv_sems = refs[2 * n:]
        x, y, c = _mesh_pos()
        half = lambda t, core: _window(outs[t], 1, pl.multiple_of(core * half_sizes[t], 8), half_sizes[t])
        sends = []
        for t in range(n):
            sends.append(pltpu.make_async_remote_copy(half(t, c), half(t, c), send_sems.at[t], recv_sems.at[t],
                                                      device_id=(x, y, 1 - c), device_id_type=MESH))
            sends[-1].start()
        for t in range(n):
            pltpu.make_async_remote_copy(half(t, c), half(t, 1 - c), send_sems.at[t], recv_sems.at[t],
                                         device_id=(x, y, 1 - c), device_id_type=MESH).wait_recv()
        for cp in sends:
            cp.wait_send()

    outs = pl.pallas_call(
        body, name="exchange_halves", in_specs=[ANY] * n, out_specs=[ANY] * n,
        out_shape=[_sds(shards[k].shape, F32) for k in names],
        input_output_aliases={t: t for t in range(n)},
        scratch_shapes=[pltpu.SemaphoreType.DMA((n,)), pltpu.SemaphoreType.DMA((n,))],
    )(*[shards[k] for k in names])
    return dict(zip(names, outs))


def _adamw_math(w, g, m, v):
    m = ADAM_B1 * m + (1.0 - ADAM_B1) * g
    v = ADAM_B2 * v + (1.0 - ADAM_B2) * jnp.square(g)
    m_hat = m / (1.0 - ADAM_B1 ** ADAM_STEP)
    v_hat = v / (1.0 - ADAM_B2 ** ADAM_STEP)
    return -ADAM_LR * (m_hat / (jnp.sqrt(v_hat) + ADAM_EPS) + ADAM_WD * w), m, v


def _adamw(name, w, g, m, v, *, tr=512):
    R, C = w.shape
    tr = min(tr, R)

    def body(w_ref, g_ref, m_ref, v_ref, d_ref, mo_ref, vo_ref):
        d_ref[...], mo_ref[...], vo_ref[...] = _adamw_math(w_ref[...], g_ref[...], m_ref[...], v_ref[...])

    spec = pl.BlockSpec((tr, C), lambda i: (i, 0))
    return pl.pallas_call(body, name=name, grid=(R // tr,), in_specs=[spec] * 4, out_specs=[spec] * 3,
                          out_shape=[_sds((R, C), F32)] * 3, compiler_params=_params("parallel"))(w, g, m, v)


SLAB_ROWS = 8
LOSS_ROW = 7


def _pack_small(d):
    pad = lambda a, width: jnp.pad(a.reshape(-1), (0, width - a.size))
    row5 = jnp.concatenate([d["swa_q_norm"].reshape(-1), d["swa_k_norm"].reshape(-1), d["ax_q_norm"].reshape(-1),
                            d["ax_k_norm"].reshape(-1), pad(d["swa_sink"], HEAD_DIM), pad(d["ret_decay_logit"], HEAD_DIM),
                            jnp.zeros((2 * HEAD_DIM,), F32)])
    return jnp.concatenate([d["norm_mix"], d["norm_mlp"], d["ret_norm"], row5[None], pad(d["t5_table"], D_MODEL)[None],
                            jnp.zeros((1, D_MODEL), F32)], axis=0)


def _unpack_small(slab):
    r5 = slab[5]
    return {
        "norm_mix": slab[0:2], "norm_mlp": slab[2:4], "ret_norm": slab[4:5],
        "swa_q_norm": r5[None, 0:128], "swa_k_norm": r5[None, 128:256], "ax_q_norm": r5[None, 256:384],
        "ax_k_norm": r5[None, 384:512], "swa_sink": r5[None, 512:512 + SWA_HEADS],
        "ret_decay_logit": r5[640:640 + 2 * RET_HEADS].reshape(1, 2, RET_HEADS),
        "t5_table": slab[6, :T5_BUCKETS * SWA_HEADS].reshape(T5_BUCKETS, SWA_HEADS),
    }


def _small_allreduce_adamw(g_slab, w_slab, m_slab, v_slab, loss_part):
    def body(g_ref, w_ref, m_ref, v_ref, lp_ref, go_ref, d_ref, mo_ref, vo_ref, gath, send_sems, recv_sems):
        pos = _mesh_pos()
        ident = lambda p: 4 * p[0] + 2 * p[1] + p[2]
        me = ident(pos)
        row = lax.broadcasted_iota(jnp.int32, (SLAB_ROWS, D_MODEL), 0)
        lane = lax.broadcasted_iota(jnp.int32, (SLAB_ROWS, D_MODEL), 1)
        loss = jnp.sum(jnp.sum(lp_ref[...], axis=0, keepdims=True), axis=1, keepdims=True) * (0.5 / D_MODEL)
        gath[me] = jnp.where(jnp.logical_and(row == LOSS_ROW, lane == 0), loss, g_ref[...])
        sends = []
        for k, f in enumerate(FLIPS):
            sends.append(pltpu.make_async_remote_copy(gath.at[me], gath.at[me], send_sems.at[k], recv_sems.at[k],
                                                      device_id=_flip(pos, f), device_id_type=MESH))
            sends[-1].start()
        for k, f in enumerate(FLIPS):
            peer = _flip(pos, f)
            pltpu.make_async_remote_copy(gath.at[me], gath.at[ident(peer)], send_sems.at[k], recv_sems.at[k],
                                         device_id=peer, device_id_type=MESH).wait_recv()
        for cp in sends:
            cp.wait_send()
        total = gath[0]
        for s in range(1, N_DEV):
            total = total + gath[s]
        go_ref[...] = total
        d_ref[...], mo_ref[...], vo_ref[...] = _adamw_math(w_ref[...], total, m_ref[...], v_ref[...])

    vmem = pl.BlockSpec(memory_space=pltpu.VMEM)
    return pl.pallas_call(
        body, name="small_allreduce_adamw", in_specs=[vmem] * 5, out_specs=[vmem] * 4,
        out_shape=[_sds((SLAB_ROWS, D_MODEL), F32)] * 4,
        scratch_shapes=[pltpu.VMEM((N_DEV, SLAB_ROWS, D_MODEL), F32),
                        pltpu.SemaphoreType.DMA((len(FLIPS),)), pltpu.SemaphoreType.DMA((len(FLIPS),))],
    )(g_slab, w_slab, m_slab, v_slab, loss_part)


def kernel(x, norm_mix, norm_mlp, w_in_even, w_out_even, ret_decay_logit, ret_norm, swa_q_norm, swa_k_norm, swa_sink, t5_table, w_in_odd, w_out_odd, ax_q_norm, ax_k_norm, w_mlp_up, w_mlp_down, loss_target, m_norm_mix, m_norm_mlp, m_w_in_even, m_w_out_even, m_ret_decay_logit, m_ret_norm, m_swa_q_norm, m_swa_k_norm, m_swa_sink, m_t5_table, m_w_in_odd, m_w_out_odd, m_ax_q_norm, m_ax_k_norm, m_w_mlp_up, m_w_mlp_down, v_norm_mix, v_norm_mlp, v_w_in_even, v_w_out_even, v_ret_decay_logit, v_ret_norm, v_swa_q_norm, v_swa_k_norm, v_swa_sink, v_t5_table, v_w_in_odd, v_w_out_odd, v_ax_q_norm, v_ax_k_norm, v_w_mlp_up, v_w_mlp_down):
    w = dict(zip(WEIGHTS, (norm_mix, norm_mlp, w_in_even, w_out_even, ret_decay_logit, ret_norm, swa_q_norm, swa_k_norm,
                           swa_sink, t5_table, w_in_odd, w_out_odd, ax_q_norm, ax_k_norm, w_mlp_up, w_mlp_down)))
    m = dict(zip(WEIGHTS, (m_norm_mix, m_norm_mlp, m_w_in_even, m_w_out_even, m_ret_decay_logit, m_ret_norm, m_swa_q_norm,
                           m_swa_k_norm, m_swa_sink, m_t5_table, m_w_in_odd, m_w_out_odd, m_ax_q_norm, m_ax_k_norm,
                           m_w_mlp_up, m_w_mlp_down)))
    v = dict(zip(WEIGHTS, (v_norm_mix, v_norm_mlp, v_w_in_even, v_w_out_even, v_ret_decay_logit, v_ret_norm, v_swa_q_norm,
                           v_swa_k_norm, v_swa_sink, v_t5_table, v_w_in_odd, v_w_out_odd, v_ax_q_norm, v_ax_k_norm,
                           v_w_mlp_up, v_w_mlp_down)))
    flat = lambda a: a.reshape(-1, a.shape[-1])

    chip = (2 * lax.axis_index("x") + lax.axis_index("y")).astype(jnp.int32)
    where = jnp.stack([chip, lax.axis_index("c").astype(jnp.int32)])

    placed = {k: _cast_place(k, w[k], where[0:1]) for k in BIG}
    gather, gather_token = _allgather_start([{k: placed[k] for k in group} for group in GATHER_ORDER])
    gathered = {}
    w_slab, m_slab, v_slab = (_pack_small({k: d[k] for k in SMALL}) for d in (w, m, v))

    def fetch(name, after):
        if name not in gathered:
            state = gather[[name in group for group in GATHER_ORDER].index(True)]
            after = gather_token if after is None else after
            if name == GATHER_ORDER[0][0]:
                after = after + (w_slab + m_slab + v_slab)[:, :HEAD_DIM]
            gathered.update(_allgather_wait(state, after))
        return gathered[name] if name.startswith("w_mlp") else gathered[name][0]

    in_flight = []

    def push(tag, weight, dw):
        state, token = _scatter_start(tag, weight, dw)
        in_flight.append(state)
        return token

    loss_part, dx, small_g = _local_step(x[0], loss_target[0], {k: w[k] for k in SMALL}, fetch, push)

    halves = {}
    for state in in_flight:
        tag, weight = state[0], state[1]
        dw, land = _scatter_wait(state, dx)
        halves[tag] = _sum_pieces(tag, weight, dw, land, where)
    reduced = _exchange_halves(halves)
    grad = {"w_in_even": reduced["in_even"], "w_out_even": reduced["out_even"],
            "w_in_odd": reduced["in_odd"], "w_out_odd": reduced["out_odd"],
            "w_mlp_up": jnp.concatenate([reduced["mlp_up0"], reduced["mlp_up1"]], axis=0),
            "w_mlp_down": jnp.concatenate([reduced["mlp_down0"], reduced["mlp_down1"]], axis=0)}
    delta, new_m, new_v = {}, {}, {}
    for k in BIG:
        d_k, m_k, v_k = _adamw(f"adamw_{k}", flat(w[k]), flat(grad[k]), flat(m[k]), flat(v[k]))
        delta[k], new_m[k], new_v[k] = d_k.reshape(w[k].shape), m_k.reshape(w[k].shape), v_k.reshape(w[k].shape)

    slabs = _small_allreduce_adamw(_pack_small(small_g), w_slab, m_slab, v_slab, loss_part)
    loss = slabs[0][LOSS_ROW, 0]
    for out, slab in zip((grad, delta, new_m, new_v), slabs):
        out.update(_unpack_small(slab))

    return (loss, dx[None], *[grad[k] for k in WEIGHTS], *[delta[k] for k in WEIGHTS],
            *[new_m[k] for k in WEIGHTS], *[new_v[k] for k in WEIGHTS])
```

```python
import functools
import math

import jax
import jax.numpy as jnp
from jax import lax
from jax.experimental import pallas as pl
from jax.experimental.pallas import tpu as pltpu

F32 = jnp.float32
BF16 = jnp.bfloat16
_MXU = BF16
_WIRE = BF16

D_MODEL = 1024
HEAD_DIM = 128
EPS = 1e-6
NEG_INF = -1e30
CHUNK = 128
RET_CHUNKS_PER_STEP = 8
GRID_W = 64
RET_HEADS, RET_DK, RET_DV = 4, 128, 256
RET_Q, RET_V = RET_HEADS * RET_DK, RET_HEADS * RET_DV
RET_THETA = 10000.0
SWA_HEADS, SWA_KV_HEADS = 8, 2
T5_BUCKETS, T5_MAX_DIST = 32, 128
AX_HEADS, AX_KV_HEADS = 8, 2
AX_THETA = 10000.0
D_FF = 4 * D_MODEL
EVEN_IN = 2 * RET_Q + 2 * RET_V + D_MODEL + 2 * SWA_KV_HEADS * HEAD_DIM
ODD_IN = D_MODEL + 2 * AX_KV_HEADS * HEAD_DIM
ATT_SCALE = HEAD_DIM ** -0.5
SCORE_SCALE_LOG2 = ATT_SCALE * math.log2(math.e)

ADAM_LR, ADAM_B1, ADAM_B2, ADAM_EPS, ADAM_WD, ADAM_STEP = 0.001, 0.9, 0.999, 1e-08, 0.01, 10

N_DEV = 8
VMEM_LIMIT_BYTES = 56 << 20
MESH = pl.DeviceIdType.MESH

_NN = (((1,), (0,)), ((), ()))
_NT = (((1,), (1,)), ((), ()))
_TN = (((0,), (0,)), ((), ()))


def _dot(a, b, dn=_NN):
    return lax.dot_general(a.astype(_MXU), b.astype(_MXU), dn, preferred_element_type=F32)


def _params(*sem):
    return pltpu.CompilerParams(dimension_semantics=sem, vmem_limit_bytes=VMEM_LIMIT_BYTES)


def _sds(shape, dtype):
    return jax.ShapeDtypeStruct(tuple(shape), dtype)


def _rowsum8(x):
    return jnp.sum(x.reshape(x.shape[0] // 8, 8, x.shape[1]), axis=0)


def _swap_halves(x, half):
    width = x.shape[1]
    lane = lax.broadcasted_iota(jnp.int32, x.shape, 1)
    up = pltpu.roll(x, width - half, axis=1)
    down = pltpu.roll(x, half, axis=1)
    return jnp.where((lane & (2 * half - 1)) < half, up, down)


def _sigmoid(x):
    return 1.0 / (1.0 + jnp.exp(-x))


def _weight_spec(w, block, index_map):
    if isinstance(w, tuple):
        stacked, layer = w
        return stacked, pl.BlockSpec((None,) + block, lambda *idx: (layer,) + index_map(*idx))
    return w, pl.BlockSpec(block, index_map)


def _weight_dims(w):
    return (w[0] if isinstance(w, tuple) else w).shape[-2:]


def _norm_matmul(name, x, gain, w, *, tm, tn, out_dtype):
    T, K = x.shape
    N = _weight_dims(w)[1]
    tm, tn = min(tm, T), min(tn, N)
    w, w_spec = _weight_spec(w, (K, tn), lambda i, j: (0, j))

    def body(x_ref, g_ref, w_ref, y_ref, h_ref, h_sc):
        @pl.when(pl.program_id(1) == 0)
        def _():
            xv = x_ref[...]
            r = lax.rsqrt(jnp.mean(xv * xv, axis=-1, keepdims=True) + EPS)
            h = (xv * r * g_ref[...]).astype(_MXU)
            h_sc[...] = h
            h_ref[...] = h
        y_ref[...] = jnp.dot(h_sc[...], w_ref[...], preferred_element_type=F32).astype(y_ref.dtype)

    return pl.pallas_call(
        body, name=name, grid=(T // tm, N // tn),
        in_specs=[pl.BlockSpec((tm, K), lambda i, j: (i, 0)),
                  pl.BlockSpec((1, K), lambda i, j: (0, 0)),
                  w_spec],
        out_specs=[pl.BlockSpec((tm, tn), lambda i, j: (i, j)),
                   pl.BlockSpec((tm, K), lambda i, j: (i, 0))],
        out_shape=[_sds((T, N), out_dtype), _sds((T, K), _MXU)],
        scratch_shapes=[pltpu.VMEM((tm, K), _MXU)],
        compiler_params=_params("parallel", "arbitrary"),
    )(x, gain, w)


def _matmul_res(name, a_list, w, res, *, tm, relu2=False, target=None):
    T = res.shape[0]
    N = _weight_dims(w)[1]
    K = a_list[0].shape[1]
    n_a = len(a_list)
    tm = min(tm, T)
    with_loss = target is not None
    w_specs = [_weight_spec(w, (K, N), functools.partial(lambda i, b: (b, 0), b=b)) for b in range(n_a)]

    def body(*refs):
        a_refs = refs[:n_a]
        w_refs = refs[n_a:2 * n_a]
        res_ref = refs[2 * n_a]
        acc = res_ref[...]
        for a_ref, w_ref in zip(a_refs, w_refs):
            a = a_ref[...]
            if relu2:
                a = jnp.square(jnp.maximum(a.astype(F32), 0.0))
            acc = acc + _dot(a, w_ref[...])
        if with_loss:
            tgt_ref, g_ref, g16_ref, loss_ref = refs[2 * n_a + 1:]
            diff = acc - tgt_ref[...]
            g = diff * (1.0 / N)
            g_ref[...] = g
            g16_ref[...] = g.astype(g16_ref.dtype)

            @pl.when(pl.program_id(0) == 0)
            def _():
                loss_ref[...] = jnp.zeros_like(loss_ref)
            loss_ref[...] += _rowsum8(diff * diff)
        else:
            refs[2 * n_a + 1][...] = acc

    row = lambda i: (i, 0)
    in_specs = [pl.BlockSpec((tm, K), row) for _ in a_list]
    in_specs += [spec for _, spec in w_specs]
    in_specs += [pl.BlockSpec((tm, N), row)]
    args = list(a_list) + [arr for arr, _ in w_specs] + [res]
    if with_loss:
        in_specs.append(pl.BlockSpec((tm, N), row))
        args.append(target)
        out_specs = [pl.BlockSpec((tm, N), row), pl.BlockSpec((tm, N), row), pl.BlockSpec((8, N), lambda i: (0, 0))]
        out_shape = [_sds((T, N), F32), _sds((T, N), _MXU), _sds((8, N), F32)]
        sem = "arbitrary"
    else:
        out_specs = pl.BlockSpec((tm, N), row)
        out_shape = _sds((T, N), F32)
        sem = "parallel"
    return pl.pallas_call(body, name=name, grid=(T // tm,), in_specs=in_specs, out_specs=out_specs,
                          out_shape=out_shape, compiler_params=_params(sem))(*args)


def _matmul_nt(name, a, w, *, tm, tn, out_dtype, relu_of=None):
    T, K = a.shape
    N = _weight_dims(w)[0]
    tm, tn = min(tm, T), min(tn, N)
    w, w_spec = _weight_spec(w, (tn, K), lambda i, j: (j, 0))

    def body(*refs):
        if relu_of is None:
            a_ref, w_ref, o_ref = refs
            o_ref[...] = _dot(a_ref[...], w_ref[...], _NT).astype(o_ref.dtype)
        else:
            a_ref, w_ref, u_ref, o_ref = refs
            da = _dot(a_ref[...], w_ref[...], _NT)
            o_ref[...] = (da * (2.0 * jnp.maximum(u_ref[...].astype(F32), 0.0))).astype(o_ref.dtype)

    in_specs = [pl.BlockSpec((tm, K), lambda i, j: (i, 0)), w_spec]
    args = [a, w]
    if relu_of is not None:
        in_specs.append(pl.BlockSpec((tm, tn), lambda i, j: (i, j)))
        args.append(relu_of)
    return pl.pallas_call(body, name=name, grid=(T // tm, N // tn), in_specs=in_specs,
                          out_specs=pl.BlockSpec((tm, tn), lambda i, j: (i, j)),
                          out_shape=_sds((T, N), out_dtype),
                          compiler_params=_params("parallel", "parallel"))(*args)


def _matmul_nt_normbwd(name, dy, w, x, gain, dres, *, tm):
    T, K = dy.shape
    N = _weight_dims(w)[0]
    tm = min(tm, T)
    w, w_spec = _weight_spec(w, (N, K), lambda i: (0, 0))

    def body(dy_ref, w_ref, x_ref, g_ref, dres_ref, dx_ref, dx16_ref, dg_ref):
        dh = _dot(dy_ref[...], w_ref[...], _NT)
        xv = x_ref[...]
        r = lax.rsqrt(jnp.mean(xv * xv, axis=-1, keepdims=True) + EPS)
        xhat = xv * r
        dxhat = dh * g_ref[...]
        dx = dres_ref[...] + r * (dxhat - xhat * jnp.mean(dxhat * xhat, axis=-1, keepdims=True))
        dx_ref[...] = dx
        dx16_ref[...] = dx.astype(dx16_ref.dtype)

        @pl.when(pl.program_id(0) == 0)
        def _():
            dg_ref[...] = jnp.zeros_like(dg_ref)
        dg_ref[...] += _rowsum8(dh * xhat)

    row = lambda i: (i, 0)
    return pl.pallas_call(
        body, name=name, grid=(T // tm,),
        in_specs=[pl.BlockSpec((tm, K), row), w_spec,
                  pl.BlockSpec((tm, N), row), pl.BlockSpec((1, N), lambda i: (0, 0)), pl.BlockSpec((tm, N), row)],
        out_specs=[pl.BlockSpec((tm, N), row), pl.BlockSpec((tm, N), row), pl.BlockSpec((8, N), lambda i: (0, 0))],
        out_shape=[_sds((T, N), F32), _sds((T, N), _MXU), _sds((8, N), F32)],
        compiler_params=_params("arbitrary"),
    )(dy, w, x, gain, dres)


def _matmul_tn(name, a, b, *, tk, tn, tt, out_dtype, relu2=False):
    T, Ka = a.shape
    Nb = b.shape[1]
    tk, tn, tt = min(tk, Ka), min(tn, Nb), min(tt, T)
    nt = T // tt

    def body(a_ref, b_ref, o_ref, acc):
        t = pl.program_id(2)

        @pl.when(t == 0)
        def _():
            acc[...] = jnp.zeros_like(acc)
        av = a_ref[...]
        if relu2:
            av = jnp.square(jnp.maximum(av.astype(F32), 0.0))
        acc[...] += _dot(av, b_ref[...], _TN)

        @pl.when(t == nt - 1)
        def _():
            o_ref[...] = acc[...].astype(o_ref.dtype)

    return pl.pallas_call(
        body, name=name, grid=(Ka // tk, Nb // tn, nt),
        in_specs=[pl.BlockSpec((tt, tk), lambda i, j, t: (t, i)), pl.BlockSpec((tt, tn), lambda i, j, t: (t, j))],
        out_specs=pl.BlockSpec((tk, tn), lambda i, j, t: (i, j)),
        out_shape=_sds((Ka, Nb), out_dtype),
        scratch_shapes=[pltpu.VMEM((tk, tn), F32)],
        compiler_params=_params("parallel", "parallel", "arbitrary"),
    )(a, b)


def _rope_angles(pos, dim, theta):
    inv = theta ** (-jnp.arange(0, dim, 2, dtype=F32) / dim)
    return pos.astype(F32)[:, None] * inv[None, :]


def _ret_rope_tables(T):
    ang = _rope_angles(jnp.arange(T), RET_DK, RET_THETA)
    c, s = jnp.cos(ang), jnp.sin(ang)
    return jnp.concatenate([c, c], axis=1), jnp.concatenate([-s, s], axis=1)


def _axial_rope_tables(T):
    rows = T // GRID_W
    ar = _rope_angles(jnp.arange(rows), HEAD_DIM // 2, AX_THETA)
    ac = _rope_angles(jnp.arange(GRID_W), HEAD_DIM // 2, AX_THETA)
    by_row = lambda a: jnp.repeat(a, GRID_W, axis=0)
    by_col = lambda a: jnp.tile(a, (rows, 1))
    cos = jnp.concatenate([by_row(jnp.cos(ar)), by_row(jnp.cos(ar)), by_col(jnp.cos(ac)), by_col(jnp.cos(ac))], axis=1)
    sin = jnp.concatenate([by_row(-jnp.sin(ar)), by_row(jnp.sin(ar)), by_col(-jnp.sin(ac)), by_col(jnp.sin(ac))], axis=1)
    return cos, sin


(TAB_D, TAB_DT, TAB_EF, TAB_EB, TAB_A, TAB_B, TAB_CF, TAB_CB,
 TAB_RA, TAB_RB, TAB_RCF, TAB_RCB, TAB_KF, TAB_KB) = range(14)


def _retention_tables(decay_logit):
    lg = jax.nn.log_sigmoid(decay_logit.astype(F32))
    lam, mu = lg[0][:, None, None], lg[1][:, None, None]
    idx = jnp.arange(CHUNK, dtype=F32)
    diff = (idx[:, None] - idx[None, :])[None]
    df = jnp.where(diff >= 0, jnp.exp(jnp.maximum(diff, 0.0) * lam), 0.0)
    db = jnp.where(diff < 0, jnp.exp(jnp.maximum(-diff, 0.0) * mu), 0.0)
    d = df + db
    r = idx[None, :, None]
    ones = jnp.ones((1, 1, CHUNK), F32)
    a = jnp.exp((r + 1.0) * lam) * ones
    b = jnp.exp((CHUNK - r) * mu) * ones
    cf = jnp.exp((CHUNK - 1.0 - r) * lam) * ones
    cb = jnp.exp(r * mu) * ones
    full = jnp.ones((1, CHUNK, CHUNK), F32)
    kf = CHUNK * jnp.exp(CHUNK * lam) * full
    kb = CHUNK * jnp.exp(CHUNK * mu) * full
    tabs = jnp.stack([d, jnp.swapaxes(d, 1, 2), diff * df, -diff * db, a, b, cf, cb,
                      (r + 1.0) * a, (CHUNK - r) * b, (CHUNK - 1.0 - r) * cf, r * cb, kf, kb], axis=1)

    def lanes(tab):
        return jnp.transpose(tab, (1, 0, 2)).reshape(CHUNK, RET_HEADS * CHUNK)

    def dec(l):
        return jnp.exp(CHUNK * l)[:, 0, :] * jnp.ones((1, RET_DV), F32)

    weights = dict(a=lanes(a), b=lanes(b), cf=lanes(cf), cb=lanes(cb), dec_f=dec(lam), dec_b=dec(mu))
    return tabs, weights, lg


def _t5_bucket(rel):
    nb = T5_BUCKETS // 2
    max_exact = nb // 2
    ret = jnp.where(rel > 0, nb, 0)
    n = jnp.abs(rel)
    nf = jnp.maximum(n, 1).astype(F32)
    large = max_exact + (jnp.log(nf / max_exact) / math.log(T5_MAX_DIST / max_exact)
                         * (nb - max_exact)).astype(jnp.int32)
    large = jnp.minimum(large, nb - 1)
    return ret + jnp.where(n < max_exact, n, large)


def _swa_rel():
    r = jnp.arange(CHUNK)
    j = jnp.arange(3 * CHUNK)
    return j[None, :] - CHUNK - r[:, None]


def _swa_bias(t5_table):
    rel = _swa_rel()
    bucket = jnp.where(jnp.abs(rel) <= CHUNK, _t5_bucket(rel), -1).astype(jnp.int32)

    def body(tab_ref, bk_ref, o_ref):
        bk = bk_ref[...]
        for h in range(SWA_HEADS):
            pick = lambda b, acc, h=h: jnp.where(bk == b, tab_ref[b, h], acc)
            o_ref[h] = lax.fori_loop(0, T5_BUCKETS, pick, jnp.full(bk.shape, NEG_INF, F32))

    return pl.pallas_call(
        body, name="t5_bias",
        in_specs=[pl.BlockSpec(memory_space=pltpu.SMEM), pl.BlockSpec(memory_space=pltpu.VMEM)],
        out_specs=pl.BlockSpec(memory_space=pltpu.VMEM),
        out_shape=_sds((SWA_HEADS, CHUNK, 3 * CHUNK), F32),
    )(t5_table.astype(F32), bucket)


def _prep_even(proj, cos, sin, q_gain, k_gain, *, tm):
    T = proj.shape[0]
    tm = min(tm, T)

    def body(qa_ref, ka_ref, qb_ref, kb_ref, c_ref, s_ref, qg_ref, kg_ref, qr_ref, kr_ref, qn_ref, kn_ref):
        c = jnp.concatenate([c_ref[...]] * RET_HEADS, axis=1)
        s = jnp.concatenate([s_ref[...]] * RET_HEADS, axis=1)
        qa = qa_ref[...]
        qr_ref[...] = (qa * c + _swap_halves(qa, RET_DK // 2) * s).astype(qr_ref.dtype)
        ka = ka_ref[...]
        kr_ref[...] = ((ka * c + _swap_halves(ka, RET_DK // 2) * s) * (RET_DK ** -0.5)).astype(kr_ref.dtype)
        for src, gain, dst, heads in ((qb_ref, qg_ref, qn_ref, SWA_HEADS), (kb_ref, kg_ref, kn_ref, SWA_KV_HEADS)):
            for h in range(heads):
                sl = slice(h * HEAD_DIM, (h + 1) * HEAD_DIM)
                xh = src[:, sl]
                r = lax.rsqrt(jnp.mean(xh * xh, axis=-1, keepdims=True) + EPS)
                dst[:, sl] = (xh * r * gain[...]).astype(dst.dtype)

    row = lambda i: (i, 0)
    const = lambda i: (0, 0)
    return pl.pallas_call(
        body, name="prep_even", grid=(T // tm,),
        in_specs=[pl.BlockSpec((tm, RET_Q), lambda i: (i, 0)), pl.BlockSpec((tm, RET_Q), lambda i: (i, 1)),
                  pl.BlockSpec((tm, D_MODEL), lambda i: (i, 3)), pl.BlockSpec((tm, 256), lambda i: (i, 16)),
                  pl.BlockSpec((tm, RET_DK), row), pl.BlockSpec((tm, RET_DK), row),
                  pl.BlockSpec((1, HEAD_DIM), const), pl.BlockSpec((1, HEAD_DIM), const)],
        out_specs=[pl.BlockSpec((tm, RET_Q), row), pl.BlockSpec((tm, RET_Q), row),
                   pl.BlockSpec((tm, D_MODEL), row), pl.BlockSpec((tm, 256), row)],
        out_shape=[_sds((T, RET_Q), _MXU), _sds((T, RET_Q), _MXU), _sds((T, D_MODEL), _MXU), _sds((T, 256), _MXU)],
        compiler_params=_params("parallel"),
    )(proj, proj, proj, proj, cos, sin, q_gain, k_gain)


def _ret_scan(name, x, y, y_col, w_asc, dec_asc, w_desc, dec_desc):
    T = x.shape[0]
    nc = T // CHUNK
    per = min(RET_CHUNKS_PER_STEP, nc)
    nb = nc // per
    rows_per = per * CHUNK

    def body(xa_ref, ya_ref, xd_ref, yd_ref, wa_ref, da_ref, wd_ref, dd_ref, sa_out, sd_out, sa, sd):
        @pl.when(pl.program_id(0) == 0)
        def _():
            sa[...] = jnp.zeros_like(sa)
            sd[...] = jnp.zeros_like(sd)
        for step in range(per):
            for c, x_ref, y_ref, w_ref, d_ref, st, out in ((step, xa_ref, ya_ref, wa_ref, da_ref, sa, sa_out),
                                                       (per - 1 - step, xd_ref, yd_ref, wd_ref, dd_ref, sd, sd_out)):
                rows = slice(c * CHUNK, (c + 1) * CHUNK)
                out[c] = st[...].astype(out.dtype)
                for h in range(RET_HEADS):
                    ks = slice(h * RET_DK, (h + 1) * RET_DK)
                    vs = slice(h * RET_DV, (h + 1) * RET_DV)
                    u = _dot(x_ref[rows, ks].astype(F32) * w_ref[:, ks], y_ref[rows, vs], _TN)
                    st[ks, :] = st[ks, :] * d_ref[h:h + 1, :] + u

    asc = lambda i: (i, 0)
    desc = lambda i: (nb - 1 - i, 0)
    const = lambda i: (0, 0)
    return pl.pallas_call(
        body, name=name, grid=(nb,),
        in_specs=[pl.BlockSpec((rows_per, RET_Q), asc), pl.BlockSpec((rows_per, RET_V), lambda i: (i, y_col)),
                  pl.BlockSpec((rows_per, RET_Q), desc), pl.BlockSpec((rows_per, RET_V), lambda i: (nb - 1 - i, y_col)),
                  pl.BlockSpec((CHUNK, RET_Q), const), pl.BlockSpec((RET_HEADS, RET_DV), const),
                  pl.BlockSpec((CHUNK, RET_Q), const), pl.BlockSpec((RET_HEADS, RET_DV), const)],
        out_specs=[pl.BlockSpec((per, RET_Q, RET_DV), lambda i: (i, 0, 0)),
                   pl.BlockSpec((per, RET_Q, RET_DV), lambda i: (nb - 1 - i, 0, 0))],
        out_shape=[_sds((nc, RET_Q, RET_DV), _MXU), _sds((nc, RET_Q, RET_DV), _MXU)],
        scratch_shapes=[pltpu.VMEM((RET_Q, RET_DV), F32), pltpu.VMEM((RET_Q, RET_DV), F32)],
        compiler_params=_params("arbitrary"),
    )(x, y, x, y, w_asc, dec_asc, w_desc, dec_desc)


def _ret_out(qr, kr, proj, sf, sb, tabs, gain):
    T = qr.shape[0]
    nc = T // CHUNK
    per = min(RET_CHUNKS_PER_STEP, nc)
    rows_per = per * CHUNK

    def body(q_ref, k_ref, v_ref, g_ref, sf_ref, sb_ref, tab_ref, gain_ref, o_ref, y_ref):
        for c in range(per):
            rows = slice(c * CHUNK, (c + 1) * CHUNK)
            for h in range(RET_HEADS):
                ks = slice(h * RET_DK, (h + 1) * RET_DK)
                vs = slice(h * RET_DV, (h + 1) * RET_DV)
                q, k, v = q_ref[rows, ks], k_ref[rows, ks], v_ref[rows, vs]
                qf = q.astype(F32)
                a_mat = _dot(q, k, _NT) * tab_ref[h, 0]
                o = (_dot(a_mat, v) + _dot(qf * tab_ref[h, 1], sf_ref[c, ks, :]) + _dot(qf * tab_ref[h, 2], sb_ref[c, ks, :]))
                o_ref[rows, vs] = o
                r = lax.rsqrt(jnp.mean(o * o, axis=-1, keepdims=True) + EPS)
                g = g_ref[rows, vs]
                y_ref[rows, vs] = (g * _sigmoid(g) * (o * r * gain_ref[:, vs])).astype(y_ref.dtype)

    row = lambda i: (i, 0)
    st = lambda i: (i, 0, 0)
    return pl.pallas_call(
        body, name="ret_out", grid=(nc // per,),
        in_specs=[pl.BlockSpec((rows_per, RET_Q), row), pl.BlockSpec((rows_per, RET_Q), row),
                  pl.BlockSpec((rows_per, RET_V), lambda i: (i, 1)), pl.BlockSpec((rows_per, RET_V), lambda i: (i, 2)),
                  pl.BlockSpec((per, RET_Q, RET_DV), st), pl.BlockSpec((per, RET_Q, RET_DV), st),
                  pl.BlockSpec((RET_HEADS, 3, CHUNK, CHUNK), lambda i: (0, 0, 0, 0)),
                  pl.BlockSpec((1, RET_V), lambda i: (0, 0))],
        out_specs=[pl.BlockSpec((rows_per, RET_V), row), pl.BlockSpec((rows_per, RET_V), row)],
        out_shape=[_sds((T, RET_V), F32), _sds((T, RET_V), _MXU)],
        compiler_params=_params("parallel"),
    )(qr, kr, proj, proj, sf, sb, tabs, gain)


def _ret_gate_bwd(dycat, proj, ret_o, gain, *, tm):
    T = ret_o.shape[0]
    tm = min(tm, T)

    def body(dy_ref, g_ref, o_ref, gain_ref, do_ref, dg_ref, dgain_ref):
        @pl.when(pl.program_id(0) == 0)
        def _():
            dgain_ref[...] = jnp.zeros_like(dgain_ref)
        for h in range(RET_HEADS):
            vs = slice(h * RET_DV, (h + 1) * RET_DV)
            o, g, dya, gn = o_ref[:, vs], g_ref[:, vs], dy_ref[:, vs], gain_ref[:, vs]
            r = lax.rsqrt(jnp.mean(o * o, axis=-1, keepdims=True) + EPS)
            ohat = o * r
            sg = _sigmoid(g)
            dy = dya * (g * sg)
            dg_ref[:, vs] = (dya * (ohat * gn) * (sg * (1.0 + g * (1.0 - sg)))).astype(dg_ref.dtype)
            dyg = dy * gn
            do_ref[:, vs] = (r * (dyg - ohat * jnp.mean(dyg * ohat, axis=-1, keepdims=True))).astype(do_ref.dtype)
            dgain_ref[:, vs] += _rowsum8(dy * ohat)

    row = lambda i: (i, 0)
    return pl.pallas_call(
        body, name="ret_gate_bwd", grid=(T // tm,),
        in_specs=[pl.BlockSpec((tm, RET_V), row), pl.BlockSpec((tm, RET_V), lambda i: (i, 2)),
                  pl.BlockSpec((tm, RET_V), row), pl.BlockSpec((1, RET_V), lambda i: (0, 0))],
        out_specs=[pl.BlockSpec((tm, RET_V), row), pl.BlockSpec((tm, RET_V), row), pl.BlockSpec((8, RET_V), lambda i: (0, 0))],
        out_shape=[_sds((T, RET_V), _MXU), _sds((T, RET_V), _MXU), _sds((8, RET_V), F32)],
        compiler_params=_params("arbitrary"),
    )(dycat, proj, ret_o, gain)


def _ret_bwd(qr, kr, proj, g_out, sf, sb, rf, rb, tabs):
    T = qr.shape[0]
    nc = T // CHUNK
    per = min(RET_CHUNKS_PER_STEP, nc)
    rows_per = per * CHUNK

    def body(q_ref, k_ref, v_ref, g_ref, sf_ref, sb_ref, rf_ref, rb_ref, tab_ref, dq_ref, dk_ref, dv_ref, dl_ref):
        @pl.when(pl.program_id(0) == 0)
        def _():
            dl_ref[...] = jnp.zeros_like(dl_ref)
        for c in range(per):
            rows = slice(c * CHUNK, (c + 1) * CHUNK)
            for h in range(RET_HEADS):
                ks = slice(h * RET_DK, (h + 1) * RET_DK)
                vs = slice(h * RET_DV, (h + 1) * RET_DV)
                q, k, v, g = q_ref[rows, ks], k_ref[rows, ks], v_ref[rows, vs], g_ref[rows, vs]
                s_f, s_b, r_f, r_b = sf_ref[c, ks, :], sb_ref[c, ks, :], rf_ref[c, ks, :], rb_ref[c, ks, :]
                tab = lambda t, h=h: tab_ref[h, t]
                qf, kf = q.astype(F32), k.astype(F32)
                qk = _dot(q, k, _NT)
                da_raw = _dot(g, v, _NT)
                x_f, x_b = _dot(g, s_f, _NT), _dot(g, s_b, _NT)
                dq_ref[rows, ks] = (_dot(da_raw * tab(TAB_D), k) + tab(TAB_A) * x_f + tab(TAB_B) * x_b).astype(dq_ref.dtype)
                at = _dot(k, q, _NT) * tab(TAB_DT)
                dat = _dot(v, g, _NT) * tab(TAB_DT)
                y_f, y_b = _dot(v, r_f, _NT), _dot(v, r_b, _NT)
                dk_ref[rows, ks] = (_dot(dat, q) + tab(TAB_CF) * y_f + tab(TAB_CB) * y_b).astype(dk_ref.dtype)
                dv_ref[rows, vs] = (_dot(at, g) + _dot(kf * tab(TAB_CF), r_f) + _dot(kf * tab(TAB_CB), r_b)).astype(dv_ref.dtype)
                inner = da_raw * qk
                rs_f = r_f.astype(F32) * s_f.astype(F32)
                rs_b = r_b.astype(F32) * s_b.astype(F32)
                dl_f = (inner * tab(TAB_EF) + tab(TAB_RA) * qf * x_f + tab(TAB_RCF) * kf * y_f
                        + tab(TAB_KF) * (rs_f[:, :CHUNK] + rs_f[:, CHUNK:]))
                dl_b = (inner * tab(TAB_EB) + tab(TAB_RB) * qf * x_b + tab(TAB_RCB) * kf * y_b
                        + tab(TAB_KB) * (rs_b[:, :CHUNK] + rs_b[:, CHUNK:]))
                dl_ref[2 * h:2 * h + 1, :] += jnp.sum(dl_f, axis=0, keepdims=True)
                dl_ref[2 * h + 1:2 * h + 2, :] += jnp.sum(dl_b, axis=0, keepdims=True)

    row = lambda i: (i, 0)
    st = lambda i: (i, 0, 0)
    return pl.pallas_call(
        body, name="ret_bwd", grid=(nc // per,),
        in_specs=[pl.BlockSpec((rows_per, RET_Q), row), pl.BlockSpec((rows_per, RET_Q), row),
                  pl.BlockSpec((rows_per, RET_V), lambda i: (i, 1)), pl.BlockSpec((rows_per, RET_V), row),
                  pl.BlockSpec((per, RET_Q, RET_DV), st), pl.BlockSpec((per, RET_Q, RET_DV), st),
                  pl.BlockSpec((per, RET_Q, RET_DV), st), pl.BlockSpec((per, RET_Q, RET_DV), st),
                  pl.BlockSpec((RET_HEADS, 14, CHUNK, CHUNK), lambda i: (0, 0, 0, 0))],
        out_specs=[pl.BlockSpec((rows_per, RET_Q), row), pl.BlockSpec((rows_per, RET_Q), row),
                   pl.BlockSpec((rows_per, RET_V), row), pl.BlockSpec((8, CHUNK), lambda i: (0, 0))],
        out_shape=[_sds((T, RET_Q), _MXU), _sds((T, RET_Q), _MXU), _sds((T, RET_V), _MXU), _sds((8, CHUNK), F32)],
        compiler_params=_params("arbitrary"),
    )(qr, kr, proj, g_out, sf, sb, rf, rb, tabs)


SWA_GROUP = SWA_HEADS // SWA_KV_HEADS
SWA_COLS = SWA_GROUP * CHUNK


def _swa_stack(ref, g):
    return jnp.concatenate([ref[:, h * HEAD_DIM:(h + 1) * HEAD_DIM] for h in range(g * SWA_GROUP, (g + 1) * SWA_GROUP)], axis=0)


def _swa_probs_t(q, k_win, bias_t, sink_row, i, nb):
    st = _dot(k_win, q, _NT) * ATT_SCALE + bias_t
    key = lax.broadcasted_iota(jnp.int32, st.shape, 0)
    valid = jnp.logical_and(jnp.logical_or(key >= CHUNK, i > 0), jnp.logical_or(key < 2 * CHUNK, i < nb - 1))
    st = jnp.where(valid, st, NEG_INF)
    m = jnp.maximum(jnp.max(st, axis=0, keepdims=True), sink_row)
    p = jnp.exp(st - m)
    e_sink = jnp.exp(sink_row - m)
    inv = 1.0 / (jnp.sum(p, axis=0, keepdims=True) + e_sink)
    return p * inv, e_sink * inv


def _swa_layouts(bias, sink):
    bias_t = bias.reshape(SWA_KV_HEADS, SWA_GROUP, CHUNK, 3 * CHUNK).transpose(0, 3, 1, 2).reshape(SWA_KV_HEADS, 3 * CHUNK, SWA_COLS)
    return bias_t, jnp.repeat(sink[:, 0], CHUNK).reshape(SWA_KV_HEADS, SWA_COLS)


def _swa_window_specs(nb, width, col_block, clamp):
    prev = lambda i: (jnp.maximum(clamp(i) - 1, 0), col_block)
    cur = lambda i: (clamp(i), col_block)
    nxt = lambda i: (jnp.minimum(clamp(i) + 1, nb - 1), col_block)
    return [pl.BlockSpec((CHUNK, width), f) for f in (prev, cur, nxt)]


def _swa_fwd(qn, kn, proj, bias, sink):
    T = qn.shape[0]
    nb = T // CHUNK
    kvw = SWA_KV_HEADS * HEAD_DIM
    bias_t, sink_rows = _swa_layouts(bias, sink)

    def body(q_ref, k0, k1, k2, v0, v1, v2, bias_ref, sink_ref, y_ref):
        i = pl.program_id(0)
        for g in range(SWA_KV_HEADS):
            gs = slice(g * HEAD_DIM, (g + 1) * HEAD_DIM)
            k_win = jnp.concatenate([k0[:, gs], k1[:, gs], k2[:, gs]], axis=0)
            v_win = jnp.concatenate([v0[:, gs], v1[:, gs], v2[:, gs]], axis=0).astype(_MXU)
            pt, _ = _swa_probs_t(_swa_stack(q_ref, g), k_win, bias_ref[g], sink_ref[g:g + 1, :], i, nb)
            o = _dot(v_win, pt, _TN).T
            for hh in range(SWA_GROUP):
                h = g * SWA_GROUP + hh
                y_ref[:, h * HEAD_DIM:(h + 1) * HEAD_DIM] = o[hh * CHUNK:(hh + 1) * CHUNK].astype(y_ref.dtype)

    ident = lambda i: i
    return pl.pallas_call(
        body, name="swa_fwd", grid=(nb,),
        in_specs=[pl.BlockSpec((CHUNK, D_MODEL), lambda i: (i, 0))]
        + _swa_window_specs(nb, kvw, 0, ident) + _swa_window_specs(nb, kvw, 17, ident)
        + [pl.BlockSpec((SWA_KV_HEADS, 3 * CHUNK, SWA_COLS), lambda i: (0, 0, 0)), pl.BlockSpec((SWA_KV_HEADS, SWA_COLS), lambda i: (0, 0))],
        out_specs=pl.BlockSpec((CHUNK, D_MODEL), lambda i: (i, 0)),
        out_shape=_sds((T, D_MODEL), _MXU),
        compiler_params=_params("parallel"),
    )(qn, kn, kn, kn, proj, proj, proj, bias_t, sink_rows)


def _swa_bwd(qn, kn, proj, dycat, bias, sink):
    T = qn.shape[0]
    nb = T // CHUNK
    kvw = SWA_KV_HEADS * HEAD_DIM
    bias_t, sink_rows = _swa_layouts(bias, sink)

    def body(q_ref, k0, k1, k2, v0, v1, v2, dy_ref, bias_ref, sink_ref,
             dq_ref, dk_ref, dv_ref, dbias_ref, dsink_ref, acc_a, acc_b):
        i = pl.program_id(0)

        @pl.when(i == 0)
        def _():
            dbias_ref[...] = jnp.zeros_like(dbias_ref)
            dsink_ref[...] = jnp.zeros_like(dsink_ref)
            acc_a[...] = jnp.zeros_like(acc_a)
            acc_b[...] = jnp.zeros_like(acc_b)

        @pl.when(i < nb)
        def _():
            for g in range(SWA_KV_HEADS):
                gs = slice(g * HEAD_DIM, (g + 1) * HEAD_DIM)
                k_win = jnp.concatenate([k0[:, gs], k1[:, gs], k2[:, gs]], axis=0)
                v_win = jnp.concatenate([v0[:, gs], v1[:, gs], v2[:, gs]], axis=0).astype(_MXU)
                q, dy = _swa_stack(q_ref, g), _swa_stack(dy_ref, g)
                pt, p_sink = _swa_probs_t(q, k_win, bias_ref[g], sink_ref[g:g + 1, :], i, nb)
                dpt = _dot(v_win, dy, _NT)
                delta = jnp.sum(pt * dpt, axis=0, keepdims=True)
                dst = pt * (dpt - delta)
                dbias_ref[g] += dst
                dsink_ref[g:g + 1, :] += -p_sink * delta
                dq = (_dot(k_win, dst, _TN) * ATT_SCALE).T
                for hh in range(SWA_GROUP):
                    h = g * SWA_GROUP + hh
                    dq_ref[:, h * HEAD_DIM:(h + 1) * HEAD_DIM] = dq[hh * CHUNK:(hh + 1) * CHUNK].astype(dq_ref.dtype)
                dk_win = _dot(dst, q) * ATT_SCALE
                dv_win = _dot(pt, dy)
                for win, out_ref, col0 in ((dk_win, dk_ref, 0), (dv_win, dv_ref, kvw)):
                    cs = slice(col0 + g * HEAD_DIM, col0 + (g + 1) * HEAD_DIM)
                    out_ref[:, gs] = acc_a[:, cs] + win[:CHUNK]
                    acc_a[:, cs] = acc_b[:, cs] + win[CHUNK:2 * CHUNK]
                    acc_b[:, cs] = win[2 * CHUNK:]

        @pl.when(i == nb)
        def _():
            dk_ref[...] = acc_a[:, :kvw]
            dv_ref[...] = acc_a[:, kvw:]

    clamp = lambda i: jnp.minimum(i, nb - 1)
    late = lambda i: (jnp.maximum(i - 1, 0), 0)
    bias_spec = pl.BlockSpec((SWA_KV_HEADS, 3 * CHUNK, SWA_COLS), lambda i: (0, 0, 0))
    sink_spec = pl.BlockSpec((SWA_KV_HEADS, SWA_COLS), lambda i: (0, 0))
    dq, dk, dv, dbias_t, dsink_rows = pl.pallas_call(
        body, name="swa_bwd", grid=(nb + 1,),
        in_specs=[pl.BlockSpec((CHUNK, D_MODEL), lambda i: (clamp(i), 0))]
        + _swa_window_specs(nb, kvw, 0, clamp) + _swa_window_specs(nb, kvw, 17, clamp)
        + [pl.BlockSpec((CHUNK, D_MODEL), lambda i: (clamp(i), 1)), bias_spec, sink_spec],
        out_specs=[pl.BlockSpec((CHUNK, D_MODEL), lambda i: (clamp(i), 0)),
                   pl.BlockSpec((CHUNK, kvw), late), pl.BlockSpec((CHUNK, kvw), late), bias_spec, sink_spec],
        out_shape=[_sds((T, D_MODEL), _MXU), _sds((T, kvw), F32), _sds((T, kvw), F32),
                   _sds((SWA_KV_HEADS, 3 * CHUNK, SWA_COLS), F32), _sds((SWA_KV_HEADS, SWA_COLS), F32)],
        scratch_shapes=[pltpu.VMEM((CHUNK, 2 * kvw), F32), pltpu.VMEM((CHUNK, 2 * kvw), F32)],
        compiler_params=_params("arbitrary"),
    )(qn, kn, kn, kn, proj, proj, proj, dycat, bias_t, sink_rows)
    dbias = dbias_t.reshape(SWA_KV_HEADS, 3 * CHUNK, SWA_GROUP, CHUNK).transpose(0, 2, 3, 1).reshape(SWA_HEADS, CHUNK, 3 * CHUNK)
    dsink = jnp.sum(dsink_rows.reshape(SWA_HEADS, CHUNK), axis=1, keepdims=True) * jnp.ones((1, HEAD_DIM), F32)
    return dq, dk, dv, dbias, dsink


def _t5_bucket_reduce(dbias, bucket):
    def body(db_ref, bk_ref, o_ref):
        bk = bk_ref[...]
        row = lax.broadcasted_iota(jnp.int32, (SWA_HEADS, HEAD_DIM), 0)
        lane = lax.broadcasted_iota(jnp.int32, (SWA_HEADS, HEAD_DIM), 1)

        def per_bucket(b, acc):
            mask = bk == b
            for h in range(SWA_HEADS):
                tot = jnp.sum(jnp.sum(jnp.where(mask, db_ref[h], 0.0), axis=0, keepdims=True), axis=1, keepdims=True)
                acc = acc + jnp.where(jnp.logical_and(row == h, lane == b), tot, 0.0)
            return acc

        o_ref[...] = lax.fori_loop(0, T5_BUCKETS, per_bucket, jnp.zeros((SWA_HEADS, HEAD_DIM), F32))

    return pl.pallas_call(body, name="t5_bucket_reduce", out_shape=_sds((SWA_HEADS, HEAD_DIM), F32),
                          compiler_params=pltpu.CompilerParams(vmem_limit_bytes=VMEM_LIMIT_BYTES))(dbias, bucket)


def _headnorm_bwd(x, dy, gain):
    r = lax.rsqrt(jnp.mean(x * x, axis=-1, keepdims=True) + EPS)
    xhat = x * r
    dyg = dy * gain
    return r * (dyg - xhat * jnp.mean(dyg * xhat, axis=-1, keepdims=True)), dy * xhat


def _post_even(proj, dqr, dkr, dva, dga, dqn, dkn, dvb, cos, sin, q_gain, k_gain, *, tm):
    T = proj.shape[0]
    tm = min(tm, T)
    kvw = SWA_KV_HEADS * HEAD_DIM

    def body(qb_ref, kb_ref, dqr_ref, dkr_ref, dva_ref, dga_ref, dqn_ref, dkn_ref, dvb_ref, c_ref, s_ref, qg_ref, kg_ref,
             dp_ref, dqg_ref, dkg_ref):
        @pl.when(pl.program_id(0) == 0)
        def _():
            dqg_ref[...] = jnp.zeros_like(dqg_ref)
            dkg_ref[...] = jnp.zeros_like(dkg_ref)
        c = jnp.concatenate([c_ref[...]] * RET_HEADS, axis=1)
        s = jnp.concatenate([s_ref[...]] * RET_HEADS, axis=1)
        dq = dqr_ref[...].astype(F32)
        dp_ref[:, 0:RET_Q] = (dq * c + _swap_halves(dq * s, RET_DK // 2)).astype(dp_ref.dtype)
        dk = dkr_ref[...].astype(F32) * (RET_DK ** -0.5)
        dp_ref[:, RET_Q:2 * RET_Q] = (dk * c + _swap_halves(dk * s, RET_DK // 2)).astype(dp_ref.dtype)
        off = 2 * RET_Q
        dp_ref[:, off:off + RET_V] = dva_ref[...].astype(dp_ref.dtype)
        dp_ref[:, off + RET_V:off + 2 * RET_V] = dga_ref[...].astype(dp_ref.dtype)
        off += 2 * RET_V
        for src, dsrc, gain, dgain, heads, base in ((qb_ref, dqn_ref, qg_ref, dqg_ref, SWA_HEADS, off),
                                                    (kb_ref, dkn_ref, kg_ref, dkg_ref, SWA_KV_HEADS, off + D_MODEL)):
            for h in range(heads):
                sl = slice(h * HEAD_DIM, (h + 1) * HEAD_DIM)
                dx, dgx = _headnorm_bwd(src[:, sl], dsrc[:, sl].astype(F32), gain[...])
                dp_ref[:, base + h * HEAD_DIM:base + (h + 1) * HEAD_DIM] = dx.astype(dp_ref.dtype)
                dgain[...] += _rowsum8(dgx)
        dp_ref[:, off + D_MODEL + kvw:] = dvb_ref[...].astype(dp_ref.dtype)

    row = lambda i: (i, 0)
    const = lambda i: (0, 0)
    return pl.pallas_call(
        body, name="post_even", grid=(T // tm,),
        in_specs=[pl.BlockSpec((tm, D_MODEL), lambda i: (i, 3)), pl.BlockSpec((tm, kvw), lambda i: (i, 16)),
                  pl.BlockSpec((tm, RET_Q), row), pl.BlockSpec((tm, RET_Q), row),
                  pl.BlockSpec((tm, RET_V), row), pl.BlockSpec((tm, RET_V), row),
                  pl.BlockSpec((tm, D_MODEL), row), pl.BlockSpec((tm, kvw), row), pl.BlockSpec((tm, kvw), row),
                  pl.BlockSpec((tm, RET_DK), row), pl.BlockSpec((tm, RET_DK), row),
                  pl.BlockSpec((1, HEAD_DIM), const), pl.BlockSpec((1, HEAD_DIM), const)],
        out_specs=[pl.BlockSpec((tm, EVEN_IN), row), pl.BlockSpec((8, HEAD_DIM), const), pl.BlockSpec((8, HEAD_DIM), const)],
        out_shape=[_sds((T, EVEN_IN), _MXU), _sds((8, HEAD_DIM), F32), _sds((8, HEAD_DIM), F32)],
        compiler_params=_params("arbitrary"),
    )(proj, proj, dqr, dkr, dva, dga, dqn, dkn, dvb, cos, sin, q_gain, k_gain)


def _prep_odd(proj, cos, sin, q_gain, k_gain, *, tm):
    T = proj.shape[0]
    tm = min(tm, T)
    kvw = AX_KV_HEADS * HEAD_DIM

    def body(q_ref, k_ref, v_ref, c_ref, s_ref, qg_ref, kg_ref, qx_ref, kx_ref, vx_ref):
        c, s = c_ref[...], s_ref[...]
        for src, gain, dst, heads, scale in ((q_ref, qg_ref, qx_ref, AX_HEADS, SCORE_SCALE_LOG2), (k_ref, kg_ref, kx_ref, AX_KV_HEADS, 1.0)):
            for h in range(heads):
                sl = slice(h * HEAD_DIM, (h + 1) * HEAD_DIM)
                xh = src[:, sl]
                r = lax.rsqrt(jnp.mean(xh * xh, axis=-1, keepdims=True) + EPS)
                xn = xh * r * gain[...]
                dst[:, sl] = ((xn * c + _swap_halves(xn, HEAD_DIM // 4) * s) * scale).astype(dst.dtype)
        vx_ref[...] = v_ref[...].astype(vx_ref.dtype)

    row = lambda i: (i, 0)
    const = lambda i: (0, 0)
    return pl.pallas_call(
        body, name="prep_odd", grid=(T // tm,),
        in_specs=[pl.BlockSpec((tm, D_MODEL), row), pl.BlockSpec((tm, kvw), lambda i: (i, 4)), pl.BlockSpec((tm, kvw), lambda i: (i, 5)),
                  pl.BlockSpec((tm, HEAD_DIM), row), pl.BlockSpec((tm, HEAD_DIM), row),
                  pl.BlockSpec((1, HEAD_DIM), const), pl.BlockSpec((1, HEAD_DIM), const)],
        out_specs=[pl.BlockSpec((tm, D_MODEL), row), pl.BlockSpec((tm, kvw), row), pl.BlockSpec((tm, kvw), row)],
        out_shape=[_sds((T, D_MODEL), _MXU), _sds((T, kvw), _MXU), _sds((T, kvw), _MXU)],
        compiler_params=_params("parallel"),
    )(proj, proj, proj, cos, sin, q_gain, k_gain)


def _post_odd(proj, dqxt, dkx, dvx, cos, sin, q_gain, k_gain, *, tm):
    T = proj.shape[0]
    tm = min(tm, T)
    kvw = AX_KV_HEADS * HEAD_DIM

    def body(q_ref, k_ref, dqt_ref, dk_ref, dv_ref, c_ref, s_ref, qg_ref, kg_ref, dp_ref, dqg_ref, dkg_ref):
        @pl.when(pl.program_id(0) == 0)
        def _():
            dqg_ref[...] = jnp.zeros_like(dqg_ref)
            dkg_ref[...] = jnp.zeros_like(dkg_ref)
        c, s = c_ref[...], s_ref[...]
        for src, dsrc, gain, dgain, heads, base in ((q_ref, dqt_ref, qg_ref, dqg_ref, AX_HEADS, 0),
                                                    (k_ref, dk_ref, kg_ref, dkg_ref, AX_KV_HEADS, D_MODEL)):
            for h in range(heads):
                sl = slice(h * HEAD_DIM, (h + 1) * HEAD_DIM)
                d = dsrc[sl, :].T if dsrc is dqt_ref else dsrc[:, sl]
                dn = d * c + _swap_halves(d * s, HEAD_DIM // 4)
                dx, dgx = _headnorm_bwd(src[:, sl], dn, gain[...])
                dp_ref[:, base + h * HEAD_DIM:base + (h + 1) * HEAD_DIM] = dx.astype(dp_ref.dtype)
                dgain[...] += _rowsum8(dgx)
        dp_ref[:, D_MODEL + kvw:] = dv_ref[...].astype(dp_ref.dtype)

    row = lambda i: (i, 0)
    const = lambda i: (0, 0)
    return pl.pallas_call(
        body, name="post_odd", grid=(T // tm,),
        in_specs=[pl.BlockSpec((tm, D_MODEL), row), pl.BlockSpec((tm, kvw), lambda i: (i, 4)),
                  pl.BlockSpec((D_MODEL, tm), lambda i: (0, i)), pl.BlockSpec((tm, kvw), row), pl.BlockSpec((tm, kvw), row),
                  pl.BlockSpec((tm, HEAD_DIM), row), pl.BlockSpec((tm, HEAD_DIM), row),
                  pl.BlockSpec((1, HEAD_DIM), const), pl.BlockSpec((1, HEAD_DIM), const)],
        out_specs=[pl.BlockSpec((tm, ODD_IN), row), pl.BlockSpec((8, HEAD_DIM), const), pl.BlockSpec((8, HEAD_DIM), const)],
        out_shape=[_sds((T, ODD_IN), _MXU), _sds((8, HEAD_DIM), F32), _sds((8, HEAD_DIM), F32)],
        compiler_params=_params("arbitrary"),
    )(proj, proj, dqxt, dkx, dvx, cos, sin, q_gain, k_gain)


ONES_ROWS = 16


def _flash_fwd(qx, kx, vx, *, tq, tk):
    v1t = jnp.concatenate([vx.T.reshape(AX_KV_HEADS, HEAD_DIM, vx.shape[0]),
                           jnp.ones((AX_KV_HEADS, ONES_ROWS, vx.shape[0]), vx.dtype)], axis=1)
    T = qx.shape[0]
    tq, tk = min(tq, T), min(tk, T)
    nq, nk = T // tq, T // tk
    group = AX_HEADS // AX_KV_HEADS

    def body(k_ref, v_ref, q_ref, o_ref, lse_ref, acc_sc, m_sc, l_sc):
        j = pl.program_id(2)

        @pl.when(j == 0)
        def _():
            m_sc[...] = jnp.full(m_sc.shape, NEG_INF, F32)
            l_sc[...] = jnp.zeros_like(l_sc)
            acc_sc[...] = jnp.zeros_like(acc_sc)
        k, v = k_ref[...], v_ref[0]

        def step(i, carry):
            cols = pl.ds(pl.multiple_of(i * tq, tq), tq)
            st = _dot(k, q_ref[cols, :], _NT)
            m_old = m_sc[i]
            m_new = jnp.maximum(m_old, jnp.max(st, axis=0, keepdims=True))
            p = jnp.exp2(st - m_new)
            alpha = jnp.exp2(m_old - m_new)
            pv = _dot(v, p)
            m_sc[i] = m_new
            l_sc[i] = alpha * l_sc[i] + pv[HEAD_DIM:HEAD_DIM + 1]
            acc_sc[:, cols] = alpha * acc_sc[:, cols] + pv[:HEAD_DIM]
            return carry

        lax.fori_loop(0, nq, step, 0)

        @pl.when(j == nk - 1)
        def _():
            def finish(i, carry):
                cols = pl.ds(pl.multiple_of(i * tq, tq), tq)
                o_ref[cols, :] = (acc_sc[:, cols] / l_sc[i]).T.astype(o_ref.dtype)
                lse_ref[0, i] = m_sc[i] + jnp.log2(l_sc[i])
                return carry

            lax.fori_loop(0, nq, finish, 0)

    kv = lambda g, h, j: (j, g)
    qh = lambda g, h, j: (0, g * group + h)
    o, lse = pl.pallas_call(
        body, name="flash_fwd", grid=(AX_KV_HEADS, group, nk),
        in_specs=[pl.BlockSpec((tk, HEAD_DIM), kv), pl.BlockSpec((1, HEAD_DIM + ONES_ROWS, tk), lambda g, h, j: (g, 0, j)),
                  pl.BlockSpec((T, HEAD_DIM), qh)],
        out_specs=[pl.BlockSpec((T, HEAD_DIM), qh), pl.BlockSpec((1, nq, 1, tq), lambda g, h, j: (g * group + h, 0, 0, 0))],
        out_shape=[_sds((T, D_MODEL), _MXU), _sds((AX_HEADS, nq, 1, tq), F32)],
        scratch_shapes=[pltpu.VMEM((HEAD_DIM, T), F32), pltpu.VMEM((nq, 1, tq), F32), pltpu.VMEM((nq, 1, tq), F32)],
        compiler_params=_params("parallel", "arbitrary", "arbitrary"),
    )(kx, v1t, qx)
    return o, lse.reshape(AX_HEADS, 1, T)


def _flash_bwd(qx, kx, vx, o, do, lse, *, tq, tk):
    T = qx.shape[0]
    tq, tk = min(tq, T), min(tk, T)
    nq = T // tq
    group = AX_HEADS // AX_KV_HEADS
    lse_rows = lse.reshape(AX_HEADS, nq, 1, tq)
    kxt = kx.T.reshape(AX_KV_HEADS, HEAD_DIM, T)

    def body(k_ref, kt_ref, v_ref, q_ref, o_ref, do_ref, lse_ref, dqt_ref, dk_ref, dv_ref, delta_sc):
        j = pl.program_id(2)

        @pl.when(jnp.logical_and(pl.program_id(1) == 0, j == 0))
        def _():
            dk_ref[...] = jnp.zeros_like(dk_ref)
            dv_ref[...] = jnp.zeros_like(dv_ref)

        @pl.when(j == 0)
        def _():
            dqt_ref[...] = jnp.zeros_like(dqt_ref)

            def row_delta(i, carry):
                rows = pl.ds(pl.multiple_of(i * tq, tq), tq)
                prod = do_ref[rows, :].astype(F32) * o_ref[rows, :].astype(F32)
                delta_sc[i] = jnp.sum(prod.T, axis=0, keepdims=True)
                return carry

            lax.fori_loop(0, nq, row_delta, 0)
        k, v = k_ref[...], v_ref[...]

        def step(i, carry):
            dk, dv = carry
            off = pl.multiple_of(i * tq, tq)
            q, do_blk = q_ref[pl.ds(off, tq), :], do_ref[pl.ds(off, tq), :]
            pt = jnp.exp2(_dot(k, q, _NT) - lse_ref[0, i])
            dst = pt * (_dot(v, do_blk, _NT) - delta_sc[i])
            dqt_ref[:, pl.ds(off, tq)] += _dot(kt_ref[0], dst) * ATT_SCALE
            return dk + _dot(dst, q), dv + _dot(pt, do_blk)

        zero = jnp.zeros((tk, HEAD_DIM), F32)
        dk, dv = lax.fori_loop(0, nq, step, (zero, zero))
        rows = pl.ds(pl.multiple_of(j * tk, tk), tk)
        dk_ref[rows, :] += dk * (ATT_SCALE / SCORE_SCALE_LOG2)
        dv_ref[rows, :] += dv

    kv = lambda g, h, j: (j, g)
    qh = lambda g, h, j: (0, g * group + h)
    st = lambda g, h, j: (g * group + h, 0, 0, 0)
    acc = lambda g, h, j: (0, g)
    return pl.pallas_call(
        body, name="flash_bwd", grid=(AX_KV_HEADS, group, T // tk),
        in_specs=[pl.BlockSpec((tk, HEAD_DIM), kv), pl.BlockSpec((1, HEAD_DIM, tk), lambda g, h, j: (g, 0, j)),
                  pl.BlockSpec((tk, HEAD_DIM), kv),
                  pl.BlockSpec((T, HEAD_DIM), qh), pl.BlockSpec((T, HEAD_DIM), qh), pl.BlockSpec((T, HEAD_DIM), qh),
                  pl.BlockSpec((1, nq, 1, tq), st)],
        out_specs=[pl.BlockSpec((HEAD_DIM, T), lambda g, h, j: (g * group + h, 0)),
                   pl.BlockSpec((T, HEAD_DIM), acc), pl.BlockSpec((T, HEAD_DIM), acc)],
        out_shape=[_sds((D_MODEL, T), F32), _sds((T, AX_KV_HEADS * HEAD_DIM), F32), _sds((T, AX_KV_HEADS * HEAD_DIM), F32)],
        scratch_shapes=[pltpu.VMEM((nq, 1, tq), F32)],
        compiler_params=_params("parallel", "arbitrary", "arbitrary"),
    )(kx, kxt, vx, qx, o, do, lse_rows)


TM = 1024
TM_WIDE = 512


def _mlp_fwd(tag, x, gain, fetch, target=None):
    u, h = _norm_matmul(f"mlp_up{tag}", x, gain, (fetch("w_mlp_up", x), tag), tm=TM_WIDE, tn=D_FF, out_dtype=_MXU)
    out = _matmul_res(f"mlp_down{tag}", [u], (fetch("w_mlp_down", u), tag), x, tm=TM if target is None else TM_WIDE,
                      relu2=True, target=target)
    return out, (x, u, h)


def _local_step(x, target, p, fetch, push, tokens=()):
    T = x.shape[0]
    cos_r, sin_r = _ret_rope_tables(T)
    cos_a, sin_a = _axial_rope_tables(T)
    tabs, rw, log_gamma = _retention_tables(p["ret_decay_logit"][0])
    bias = _swa_bias(p["t5_table"])
    sink = p["swa_sink"][0][:, None] * jnp.ones((1, HEAD_DIM), F32)
    nm, nl = p["norm_mix"], p["norm_mlp"]
    pending = [t for t in tokens if t is not None]

    def send(tag, weight, dw):
        token = push(tag, weight, dw[None])
        if token is not None:
            pending.append(token)

    def tied(operand):
        while pending:
            operand = operand + pending.pop()[0:1, 0:1]
        return operand

    def mlp_bwd(tag, saved, gain, dy, dy16):
        xs, u, h = saved
        w_up, w_down = (fetch("w_mlp_up", None), tag), (fetch("w_mlp_down", None), tag)
        du = _matmul_nt(f"mlp_down{tag}_bwd", dy16, w_down, tm=TM_WIDE, tn=D_FF, out_dtype=_MXU, relu_of=u)
        send(f"mlp_down{tag}", "w_mlp_down", _matmul_tn(f"mlp_down{tag}_dw", u, dy16, tk=2048, tn=1024, tt=1024, out_dtype=_WIRE, relu2=True))
        dx, dx16, dgain = _matmul_nt_normbwd(f"mlp_up{tag}_bwd", du, w_up, xs, tied(gain), dy, tm=TM_WIDE)
        send(f"mlp_up{tag}", "w_mlp_up", _matmul_tn(f"mlp_up{tag}_dw", h, du, tk=1024, tn=2048, tt=1024, out_dtype=_WIRE))
        return dx, dx16, dgain

    w_in_even = fetch("w_in_even", cos_r[:8] + cos_a[:8] + bias[0, :8, :HEAD_DIM] + tabs[0, 0, :8] + rw["cf"][:8, :HEAD_DIM])
    proj0, h0 = _norm_matmul("in_even", x, tied(nm[0:1]), w_in_even, tm=TM_WIDE, tn=EVEN_IN, out_dtype=F32)
    qr, kr, qn, kn = _prep_even(proj0, cos_r, sin_r, p["swa_q_norm"], p["swa_k_norm"], tm=TM)
    sf, sb = _ret_scan("ret_scan_fwd", kr, proj0, 1, rw["cf"], rw["dec_f"], rw["cb"], rw["dec_b"])
    ret_o, ya = _ret_out(qr, kr, proj0, sf, sb, tabs[:, (TAB_D, TAB_A, TAB_B)], p["ret_norm"])
    yb = _swa_fwd(qn, kn, proj0, bias, sink)
    w_out_even = fetch("w_out_even", yb)
    x1 = _matmul_res("out_even", [ya, yb], w_out_even, x, tm=TM)
    x2, mlp0 = _mlp_fwd(0, x1, nl[0:1], fetch)
    w_in_odd, w_out_odd = fetch("w_in_odd", x2), fetch("w_out_odd", x2)
    proj1, h1 = _norm_matmul("in_odd", x2, nm[1:2], w_in_odd, tm=TM, tn=ODD_IN, out_dtype=F32)
    qx, kx, vx = _prep_odd(proj1, cos_a, sin_a, p["ax_q_norm"], p["ax_k_norm"], tm=TM)
    o, lse = _flash_fwd(qx, kx, vx, tq=2048, tk=2048)
    x3 = _matmul_res("out_odd", [o], w_out_odd, x2, tm=TM)
    (g4, g4_16, loss_part), mlp1 = _mlp_fwd(1, x3, nl[1:2], fetch, target=target)

    dx3, dx3_16, dnl1 = mlp_bwd(1, mlp1, nl[1:2], g4, g4_16)
    do = _matmul_nt("out_odd_bwd", dx3_16, w_out_odd, tm=TM, tn=1024, out_dtype=_MXU)
    send("out_odd", "w_out_odd", _matmul_tn("out_odd_dw", o, dx3_16, tk=1024, tn=1024, tt=1024, out_dtype=_WIRE))
    dqxt, dkx, dvx = _flash_bwd(qx, kx, vx, o, do, lse, tq=2048, tk=512)
    dproj1, dqg1, dkg1 = _post_odd(proj1, dqxt, dkx, dvx, cos_a, sin_a, tied(p["ax_q_norm"]), p["ax_k_norm"], tm=TM)
    send("in_odd", "w_in_odd", _matmul_tn("in_odd_dw", h1, dproj1, tk=1024, tn=768, tt=1024, out_dtype=_WIRE))
    dx2, dx2_16, dnm1 = _matmul_nt_normbwd("in_odd_bwd", dproj1, w_in_odd, x2, tied(nm[1:2]), dx3, tm=TM_WIDE)
    dx1, dx1_16, dnl0 = mlp_bwd(0, mlp0, nl[0:1], dx2, dx2_16)
    dycat = _matmul_nt("out_even_bwd", dx1_16, w_out_even, tm=TM, tn=2 * D_MODEL, out_dtype=F32)
    send("out_even", "w_out_even", jnp.concatenate([
        _matmul_tn("out_even_dw_ret", ya, dx1_16, tk=1024, tn=1024, tt=1024, out_dtype=_WIRE),
        _matmul_tn("out_even_dw_swa", yb, dx1_16, tk=1024, tn=1024, tt=1024, out_dtype=_WIRE)], axis=0))
    g_out, dga, dretg = _ret_gate_bwd(dycat, proj0, ret_o, tied(p["ret_norm"]), tm=TM)
    rb, rf = _ret_scan("ret_scan_bwd", qr, g_out, 0, rw["b"], rw["dec_b"], rw["a"], rw["dec_f"])
    dqr, dkr, dva, dlog = _ret_bwd(qr, kr, proj0, g_out, sf, sb, rf, rb, tabs)
    dqn, dkn, dvb, dbias, dsink = _swa_bwd(qn, kn, proj0, dycat, bias, sink)
    dt5 = _t5_bucket_reduce(dbias, _t5_bucket(_swa_rel()).astype(jnp.int32))
    dproj0, dqg0, dkg0 = _post_even(proj0, dqr, dkr, dva, dga, dqn, dkn, dvb, cos_r, sin_r,
                                    p["swa_q_norm"], p["swa_k_norm"], tm=TM_WIDE)
    send("in_even", "w_in_even", _matmul_tn("in_even_dw", h0, dproj0, tk=1024, tn=2304, tt=1024, out_dtype=_WIRE))
    dx0, _, dnm0 = _matmul_nt_normbwd("in_even_bwd", dproj0, w_in_even, x, tied(nm[0:1]), dx1, tm=TM_WIDE)

    fold = lambda part: jnp.sum(part, axis=0)
    dlam = jnp.sum(dlog, axis=1).reshape(RET_HEADS, 2).T
    small = {
        "norm_mix": jnp.stack([fold(dnm0), fold(dnm1)]),
        "norm_mlp": jnp.stack([fold(dnl0), fold(dnl1)]),
        "ret_decay_logit": (dlam * (1.0 - jnp.exp(log_gamma)))[None],
        "ret_norm": fold(dretg)[None],
        "swa_q_norm": fold(dqg0)[None], "swa_k_norm": fold(dkg0)[None],
        "swa_sink": dsink[:, 0][None],
        "t5_table": dt5[:, :T5_BUCKETS].T,
        "ax_q_norm": fold(dqg1)[None], "ax_k_norm": fold(dkg1)[None],
    }
    return loss_part, dx0, small


BIG = ("w_in_even", "w_out_even", "w_in_odd", "w_out_odd", "w_mlp_up", "w_mlp_down")
SMALL = ("norm_mix", "norm_mlp", "ret_decay_logit", "ret_norm", "swa_q_norm", "swa_k_norm", "swa_sink", "t5_table",
         "ax_q_norm", "ax_k_norm")
WEIGHTS = ("norm_mix", "norm_mlp", "w_in_even", "w_out_even", "ret_decay_logit", "ret_norm", "swa_q_norm", "swa_k_norm",
           "swa_sink", "t5_table", "w_in_odd", "w_out_odd", "ax_q_norm", "ax_k_norm", "w_mlp_up", "w_mlp_down")
SHARD_AXIS = {"w_in_even": 2, "w_out_even": 1, "w_in_odd": 2, "w_out_odd": 1, "w_mlp_up": 2, "w_mlp_down": 1}
N_CHIPS = 4
GATHER_ORDER = (("w_in_even",), ("w_out_even",), ("w_mlp_up",), ("w_mlp_down",), ("w_in_odd", "w_out_odd"))
ANY = pl.BlockSpec(memory_space=pl.ANY)
HBM = pl.BlockSpec(memory_space=pltpu.HBM)
SEM = pl.BlockSpec(memory_space=pltpu.SEMAPHORE)
SPLIT_COPY = pltpu.CompilerParams(has_side_effects=pltpu.SideEffectType.DATAFLOW_SIDE_EFFECTING)


def _in_hbm(a):
    return pltpu.with_memory_space_constraint(a, pltpu.HBM)


def _mesh_pos():
    return lax.axis_index("x"), lax.axis_index("y"), lax.axis_index("c")


def _window(ref, axis, start, size):
    idx = [slice(None)] * len(ref.shape)
    idx[axis] = pl.ds(start, size)
    return ref.at[tuple(idx)]


def _cast_place(key, shard, chip, *, tr=256):
    L, R, C = shard.shape
    tr = min(tr, R)
    axis = SHARD_AXIS[key]
    whole = tuple(d * (N_CHIPS if a == axis else 1) for a, d in enumerate(shard.shape))

    def body(chip_ref, s_ref, o_ref):
        o_ref[...] = s_ref[...].astype(o_ref.dtype)

    if axis == 2:
        out_map = lambda l, i, chip_ref: (l, i, chip_ref[0])
    else:
        out_map = lambda l, i, chip_ref: (l, i + chip_ref[0] * (R // tr), 0)
    grid_spec = pltpu.PrefetchScalarGridSpec(
        num_scalar_prefetch=1, grid=(L, R // tr),
        in_specs=[pl.BlockSpec((1, tr, C), lambda l, i, chip_ref: (l, i, 0))],
        out_specs=pl.BlockSpec((1, tr, C), out_map))
    return pl.pallas_call(body, name=f"cast_place_{key}", grid_spec=grid_spec, out_shape=_sds(whole, _MXU),
                          compiler_params=_params("parallel", "parallel"))(chip, shard)


def _gather_copies(names, refs, send_sems, recv_sems, *, outgoing=True, incoming=True):
    x, y, c = _mesh_pos()
    chips = [(1 - x, y), (x, 1 - y), (1 - x, 1 - y)]
    out, inc = [], []
    for t, key in enumerate(names):
        size = refs[t].shape[SHARD_AXIS[key]] // N_CHIPS
        slot = lambda px, py: _window(refs[t], SHARD_AXIS[key], pl.multiple_of((2 * px + py) * size, 128), size)
        for k, (px, py) in enumerate(chips):
            sems = dict(send_sem=send_sems.at[3 * t + k], recv_sem=recv_sems.at[3 * t + k], device_id=(px, py, c), device_id_type=MESH)
            if outgoing:
                out.append(pltpu.make_async_remote_copy(slot(x, y), slot(x, y), **sems))
            if incoming:
                inc.append(pltpu.make_async_remote_copy(slot(x, y), slot(px, py), **sems))
    return out, inc


def _allgather_start(groups):
    names = [list(g) for g in groups]
    flat = [g[k] for g in groups for k in g]
    n, ng = len(flat), len(groups)

    def body(*refs):
        start = 0
        for gi, keys in enumerate(names):
            copies, _ = _gather_copies(keys, refs[start:start + len(keys)], refs[n + 2 * gi], refs[n + 2 * gi + 1], incoming=False)
            for cp in copies:
                cp.start()
            start += len(keys)
        token = refs[-1]
        token[...] = jnp.zeros_like(token)

    sem_shapes = [pltpu.SemaphoreType.DMA((3 * len(keys),)) for keys in names for _ in (0, 1)]
    outs = pl.pallas_call(
        body, name="allgather_start", in_specs=[HBM] * n,
        out_specs=[SEM] * (2 * ng) + [HBM] * n + [pl.BlockSpec(memory_space=pltpu.VMEM)],
        out_shape=sem_shapes + [pltpu.HBM(a.shape, a.dtype) for a in flat] + [_sds((8, HEAD_DIM), F32)],
        input_output_aliases={t: 2 * ng + t for t in range(n)},
        compiler_params=SPLIT_COPY,
    )(*[_in_hbm(a) for a in flat])
    states, start = [], 2 * ng
    for gi, keys in enumerate(names):
        states.append((gi, keys, outs[2 * gi], outs[2 * gi + 1], outs[start:start + len(keys)]))
        start += len(keys)
    return states, outs[-1]


def _allgather_wait(state, after):
    gi, names, send_sems, recv_sems, thru = state
    n = len(names)

    def body(*refs):
        outgoing, incoming = _gather_copies(names, refs[:n], refs[n], refs[n + 1])
        for cp in outgoing:
            cp.wait_send()
        for cp in incoming:
            cp.wait_recv()

    outs = pl.pallas_call(
        body, name=f"allgather_wait_{gi}", in_specs=[HBM] * n + [SEM, SEM, ANY], out_specs=[HBM] * n,
        out_shape=[pltpu.HBM(t.shape, t.dtype) for t in thru],
        input_output_aliases={t: t for t in range(n)},
        compiler_params=SPLIT_COPY,
    )(*thru, send_sems, recv_sems, after)
    return dict(zip(names, outs))


FLIPS = [(a, b, d) for a in (0, 1) for b in (0, 1) for d in (0, 1) if (a, b, d) != (0, 0, 0)]


def _flip(pos, f):
    return tuple(1 - p if fi else p for p, fi in zip(pos, f))


def _piece_shape(weight, shape):
    out = list(shape)
    out[SHARD_AXIS[weight]] //= N_CHIPS
    out[1] //= 2
    return tuple(out)


def _piece(ref, weight, chip, core):
    piece = _piece_shape(weight, ref.shape)
    if SHARD_AXIS[weight] == 1:
        return _window(ref, 1, pl.multiple_of((2 * chip + core) * piece[1], 8), piece[1])
    return _window(_window(ref, 2, pl.multiple_of(chip * piece[2], 128), piece[2]), 1, pl.multiple_of(core * piece[1], 8), piece[1])


def _scatter_copies(weight, grad_ref, land_ref, send_sems, recv_sems, *, outgoing=True, incoming=True):
    pos = _mesh_pos()
    out, inc = [], []
    for k, f in enumerate(FLIPS):
        peer = _flip(pos, f)
        sems = dict(send_sem=send_sems.at[k], recv_sem=recv_sems.at[k], device_id=peer, device_id_type=MESH)
        if outgoing:
            out.append(pltpu.make_async_remote_copy(_piece(grad_ref, weight, 2 * peer[0] + peer[1], peer[2]), land_ref.at[k], **sems))
        if incoming:
            inc.append(pltpu.make_async_remote_copy(_piece(grad_ref, weight, 2 * pos[0] + pos[1], pos[2]), land_ref.at[k], **sems))
    return out, inc


def _scatter_start(tag, weight, grad):
    n_peer = len(FLIPS)
    land = lax.empty((n_peer,) + _piece_shape(weight, grad.shape), grad.dtype)

    def body(grad_ref, land_ref, send_sems, recv_sems, grad_thru, land_thru, token):
        copies, _ = _scatter_copies(weight, grad_ref, land_ref, send_sems, recv_sems, incoming=False)
        for cp in copies:
            cp.start()
        token[...] = jnp.zeros_like(token)

    outs = pl.pallas_call(
        body, name=f"scatter_start_{tag}", in_specs=[HBM, HBM],
        out_specs=[SEM, SEM, HBM, HBM, pl.BlockSpec(memory_space=pltpu.VMEM)],
        out_shape=[pltpu.SemaphoreType.DMA((n_peer,)), pltpu.SemaphoreType.DMA((n_peer,)),
                   pltpu.HBM(grad.shape, grad.dtype), pltpu.HBM(land.shape, land.dtype), _sds((8, HEAD_DIM), F32)],
        input_output_aliases={0: 2, 1: 3},
        compiler_params=SPLIT_COPY,
    )(_in_hbm(grad), _in_hbm(land))
    return (tag, weight, outs[:4]), outs[4]


def _scatter_wait(state, after):
    tag, weight, (send_sems, recv_sems, grad_thru, land_thru) = state

    def body(grad_ref, land_ref, send_ref, recv_ref, after_ref, grad_out, land_out):
        outgoing, incoming = _scatter_copies(weight, grad_ref, land_ref, send_ref, recv_ref)
        for cp in outgoing:
            cp.wait_send()
        for cp in incoming:
            cp.wait_recv()

    return pl.pallas_call(
        body, name=f"scatter_wait_{tag}", in_specs=[HBM, HBM, SEM, SEM, ANY], out_specs=[HBM, HBM],
        out_shape=[pltpu.HBM(grad_thru.shape, grad_thru.dtype), pltpu.HBM(land_thru.shape, land_thru.dtype)],
        input_output_aliases={0: 0, 1: 1},
        compiler_params=SPLIT_COPY,
    )(grad_thru, land_thru, send_sems, recv_sems, after)


def _sum_pieces(tag, weight, grad, land, where, *, tr=256):
    _, R, C = _piece_shape(weight, grad.shape)
    tr = min(tr, R)
    nr = R // tr

    def body(where_ref, g_ref, l_ref, o_ref):
        acc = g_ref[...].astype(F32)
        for s in range(len(FLIPS)):
            acc = acc + l_ref[s].astype(F32)
        o_ref[...] = acc

    if SHARD_AXIS[weight] == 1:
        own = lambda i, where_ref: (0, (2 * where_ref[0] + where_ref[1]) * nr + i, 0)
    else:
        own = lambda i, where_ref: (0, where_ref[1] * nr + i, where_ref[0])
    grid_spec = pltpu.PrefetchScalarGridSpec(
        num_scalar_prefetch=1, grid=(nr,),
        in_specs=[pl.BlockSpec((1, tr, C), own), pl.BlockSpec((len(FLIPS), 1, tr, C), lambda i, where_ref: (0, 0, i, 0))],
        out_specs=pl.BlockSpec((1, tr, C), lambda i, where_ref: (0, where_ref[1] * nr + i, 0)))
    return pl.pallas_call(body, name=f"sum_{tag}", grid_spec=grid_spec, out_shape=_sds((1, 2 * R, C), F32),
                          compiler_params=_params("parallel"))(where, grad, land)


def _exchange_halves(shards):
    names = list(shards)
    n = len(names)
    half_sizes = [shards[k].shape[1] // 2 for k in names]

    def body(*refs):
        outs = refs[n:2 * n]
        send_sems, recv_sems = refs[2 * n:]
        x, y, c = _mesh_pos()
        half = lambda t, core: _window(outs[t], 1, pl.multiple_of(core * half_sizes[t], 8), half_sizes[t])
        sends = []
        for t in range(n):
            sends.append(pltpu.make_async_remote_copy(half(t, c), half(t, c), send_sems.at[t], recv_sems.at[t],
                                                      device_id=(x, y, 1 - c), device_id_type=MESH))
            sends[-1].start()
        for t in range(n):
            pltpu.make_async_remote_copy(half(t, c), half(t, 1 - c), send_sems.at[t], recv_sems.at[t],
                                         device_id=(x, y, 1 - c), device_id_type=MESH).wait_recv()
        for cp in sends:
            cp.wait_send()

    outs = pl.pallas_call(
        body, name="exchange_halves", in_specs=[ANY] * n, out_specs=[ANY] * n,
        out_shape=[_sds(shards[k].shape, F32) for k in names],
        input_output_aliases={t: t for t in range(n)},
        scratch_shapes=[pltpu.SemaphoreType.DMA((n,)), pltpu.SemaphoreType.DMA((n,))],
    )(*[shards[k] for k in names])
    return dict(zip(names, outs))


def _adamw_math(w, g, m, v):
    m = ADAM_B1 * m + (1.0 - ADAM_B1) * g
    v = ADAM_B2 * v + (1.0 - ADAM_B2) * jnp.square(g)
    m_hat = m / (1.0 - ADAM_B1 ** ADAM_STEP)
    v_hat = v / (1.0 - ADAM_B2 ** ADAM_STEP)
    return -ADAM_LR * (m_hat / (jnp.sqrt(v_hat) + ADAM_EPS) + ADAM_WD * w), m, v


def _adamw(name, w, g, m, v, *, tr=512):
    R, C = w.shape
    tr = min(tr, R)

    def body(w_ref, g_ref, m_ref, v_ref, d_ref, mo_ref, vo_ref):
        d_ref[...], mo_ref[...], vo_ref[...] = _adamw_math(w_ref[...], g_ref[...], m_ref[...], v_ref[...])

    spec = pl.BlockSpec((tr, C), lambda i: (i, 0))
    return pl.pallas_call(body, name=name, grid=(R // tr,), in_specs=[spec] * 4, out_specs=[spec] * 3,
                          out_shape=[_sds((R, C), F32)] * 3, compiler_params=_params("parallel"))(w, g, m, v)


SLAB_ROWS = 8
LOSS_ROW = 7


def _pack_small(d):
    pad = lambda a, width: jnp.pad(a.reshape(-1), (0, width - a.size))
    row5 = jnp.concatenate([d["swa_q_norm"].reshape(-1), d["swa_k_norm"].reshape(-1), d["ax_q_norm"].reshape(-1),
                            d["ax_k_norm"].reshape(-1), pad(d["swa_sink"], HEAD_DIM), pad(d["ret_decay_logit"], HEAD_DIM),
                            jnp.zeros((2 * HEAD_DIM,), F32)])
    return jnp.concatenate([d["norm_mix"], d["norm_mlp"], d["ret_norm"], row5[None], pad(d["t5_table"], D_MODEL)[None],
                            jnp.zeros((1, D_MODEL), F32)], axis=0)


def _unpack_small(slab):
    r5 = slab[5]
    return {
        "norm_mix": slab[0:2], "norm_mlp": slab[2:4], "ret_norm": slab[4:5],
        "swa_q_norm": r5[None, 0:128], "swa_k_norm": r5[None, 128:256], "ax_q_norm": r5[None, 256:384],
        "ax_k_norm": r5[None, 384:512], "swa_sink": r5[None, 512:512 + SWA_HEADS],
        "ret_decay_logit": r5[640:640 + 2 * RET_HEADS].reshape(1, 2, RET_HEADS),
        "t5_table": slab[6, :T5_BUCKETS * SWA_HEADS].reshape(T5_BUCKETS, SWA_HEADS),
    }


def _small_allreduce_adamw(g_slab, w_slab, m_slab, v_slab, loss_part):
    def body(g_ref, w_ref, m_ref, v_ref, lp_ref, go_ref, d_ref, mo_ref, vo_ref, gath, send_sems, recv_sems):
        pos = _mesh_pos()
        ident = lambda p: 4 * p[0] + 2 * p[1] + p[2]
        me = ident(pos)
        row = lax.broadcasted_iota(jnp.int32, (SLAB_ROWS, D_MODEL), 0)
        lane = lax.broadcasted_iota(jnp.int32, (SLAB_ROWS, D_MODEL), 1)
        loss = jnp.sum(jnp.sum(lp_ref[...], axis=0, keepdims=True), axis=1, keepdims=True) * (0.5 / D_MODEL)
        gath[me] = jnp.where(jnp.logical_and(row == LOSS_ROW, lane == 0), loss, g_ref[...])
        sends = []
        for k, f in enumerate(FLIPS):
            sends.append(pltpu.make_async_remote_copy(gath.at[me], gath.at[me], send_sems.at[k], recv_sems.at[k],
                                                      device_id=_flip(pos, f), device_id_type=MESH))
            sends[-1].start()
        for k, f in enumerate(FLIPS):
            peer = _flip(pos, f)
            pltpu.make_async_remote_copy(gath.at[me], gath.at[ident(peer)], send_sems.at[k], recv_sems.at[k],
                                         device_id=peer, device_id_type=MESH).wait_recv()
        for cp in sends:
            cp.wait_send()
        total = gath[0]
        for s in range(1, N_DEV):
            total = total + gath[s]
        go_ref[...] = total
        d_ref[...], mo_ref[...], vo_ref[...] = _adamw_math(w_ref[...], total, m_ref[...], v_ref[...])

    vmem = pl.BlockSpec(memory_space=pltpu.VMEM)
    return pl.pallas_call(
        body, name="small_allreduce_adamw", in_specs=[vmem] * 5, out_specs=[vmem] * 4,
        out_shape=[_sds((SLAB_ROWS, D_MODEL), F32)] * 4,
        scratch_shapes=[pltpu.VMEM((N_DEV, SLAB_ROWS, D_MODEL), F32),
                        pltpu.SemaphoreType.DMA((len(FLIPS),)), pltpu.SemaphoreType.DMA((len(FLIPS),))],
    )(g_slab, w_slab, m_slab, v_slab, loss_part)


def kernel(x, norm_mix, norm_mlp, w_in_even, w_out_even, ret_decay_logit, ret_norm, swa_q_norm, swa_k_norm, swa_sink, t5_table, w_in_odd, w_out_odd, ax_q_norm, ax_k_norm, w_mlp_up, w_mlp_down, loss_target, m_norm_mix, m_norm_mlp, m_w_in_even, m_w_out_even, m_ret_decay_logit, m_ret_norm, m_swa_q_norm, m_swa_k_norm, m_swa_sink, m_t5_table, m_w_in_odd, m_w_out_odd, m_ax_q_norm, m_ax_k_norm, m_w_mlp_up, m_w_mlp_down, v_norm_mix, v_norm_mlp, v_w_in_even, v_w_out_even, v_ret_decay_logit, v_ret_norm, v_swa_q_norm, v_swa_k_norm, v_swa_sink, v_t5_table, v_w_in_odd, v_w_out_odd, v_ax_q_norm, v_ax_k_norm, v_w_mlp_up, v_w_mlp_down):
    w = dict(zip(WEIGHTS, (norm_mix, norm_mlp, w_in_even, w_out_even, ret_decay_logit, ret_norm, swa_q_norm, swa_k_norm,
                           swa_sink, t5_table, w_in_odd, w_out_odd, ax_q_norm, ax_k_norm, w_mlp_up, w_mlp_down)))
    m = dict(zip(WEIGHTS, (m_norm_mix, m_norm_mlp, m_w_in_even, m_w_out_even, m_ret_decay_logit, m_ret_norm, m_swa_q_norm,
                           m_swa_k_norm, m_swa_sink, m_t5_table, m_w_in_odd, m_w_out_odd, m_ax_q_norm, m_ax_k_norm,
                           m_w_mlp_up, m_w_mlp_down)))
    v = dict(zip(WEIGHTS, (v_norm_mix, v_norm_mlp, v_w_in_even, v_w_out_even, v_ret_decay_logit, v_ret_norm, v_swa_q_norm,
                           v_swa_k_norm, v_swa_sink, v_t5_table, v_w_in_odd, v_w_out_odd, v_ax_q_norm, v_ax_k_norm,
                           v_w_mlp_up, v_w_mlp_down)))
    flat = lambda a: a.reshape(-1, a.shape[-1])

    chip = (2 * lax.axis_index("x") + lax.axis_index("y")).astype(jnp.int32)
    where = jnp.stack([chip, lax.axis_index("c").astype(jnp.int32)])

    placed = {k: _cast_place(k, w[k], where[0:1]) for k in BIG}
    gather, gather_token = _allgather_start([{k: placed[k] for k in group} for group in GATHER_ORDER])
    gathered = {}
    w_slab, m_slab, v_slab = (_pack_small({k: d[k] for k in SMALL}) for d in (w, m, v))

    def fetch(name, after):
        if name not in gathered:
            state = gather[[name in group for group in GATHER_ORDER].index(True)]
            after = gather_token if after is None else after
            if name == GATHER_ORDER[0][0]:
                after = after + (w_slab + m_slab + v_slab)[:, :HEAD_DIM]
            gathered.update(_allgather_wait(state, after))
        return gathered[name] if name.startswith("w_mlp") else gathered[name][0]

    in_flight = []

    def push(tag, weight, dw):
        state, token = _scatter_start(tag, weight, dw)
        in_flight.append(state)
        return token

    loss_part, dx, small_g = _local_step(x[0], loss_target[0], {k: w[k] for k in SMALL}, fetch, push)

    halves = {}
    for state in in_flight:
        tag, weight = state[0], state[1]
        dw, land = _scatter_wait(state, dx)
        halves[tag] = _sum_pieces(tag, weight, dw, land, where)
    reduced = _exchange_halves(halves)
    grad = {"w_in_even": reduced["in_even"], "w_out_even": reduced["out_even"],
            "w_in_odd": reduced["in_odd"], "w_out_odd": reduced["out_odd"],
            "w_mlp_up": jnp.concatenate([reduced["mlp_up0"], reduced["mlp_up1"]], axis=0),
            "w_mlp_down": jnp.concatenate([reduced["mlp_down0"], reduced["mlp_down1"]], axis=0)}
    delta, new_m, new_v = {}, {}, {}
    for k in BIG:
        d_k, m_k, v_k = _adamw(f"adamw_{k}", flat(w[k]), flat(grad[k]), flat(m[k]), flat(v[k]))
        delta[k], new_m[k], new_v[k] = d_k.reshape(w[k].shape), m_k.reshape(w[k].shape), v_k.reshape(w[k].shape)

    slabs = _small_allreduce_adamw(_pack_small(small_g), w_slab, m_slab, v_slab, loss_part)
    loss = slabs[0][LOSS_ROW, 0]
    for out, slab in zip((grad, delta, new_m, new_v), slabs):
        out.update(_unpack_small(slab))

    return (loss, dx[None], *[grad[k] for k in WEIGHTS], *[delta[k] for k in WEIGHTS],
            *[new_m[k] for k in WEIGHTS], *[new_v[k] for k in WEIGHTS])
```
